```python
import jax
import jax.numpy as jnp
from jax import lax
import numpy as np

D_MODEL = 1024
BATCH = 16
SEQ = 2048
DEPTH = 4

GRID_W = 64
CTX_LEN = 256
N_MIXERS = 4
MIX_CONV = 0
MIX_POOL = 1
MIX_MLA = 2
MIX_CHUNK = 3
BRANCH = D_MODEL
EPS = 1e-6
CONV_WIDTH = 31
POOL_WINDOWS = (2, 4, 8, 16)
POOL_GROUP = BRANCH // len(POOL_WINDOWS)
MLA_HEADS = D_MODEL // 128
MLA_NOPE = 128
MLA_ROPE = 64
MLA_V = 128
MLA_Q_RANK = 3 * D_MODEL // 8
MLA_KV_RANK = D_MODEL // 4
MLA_KVC = MLA_KV_RANK + MLA_ROPE
MLA_SCALE = (MLA_NOPE + MLA_ROPE) ** -0.5
ROPE_THETA = 10000.0
Q_BLOCK = 128
CHUNK = 128
CHUNK_GROUPS = 8
CHUNK_GC = BRANCH // CHUNK_GROUPS

kernel_name = 'hybrid_interleaved_diffusion_block'


def _n_layers_of(kind):
    return len(range(kind, DEPTH, N_MIXERS))


def _rms(x, g):
    xf = x.astype(jnp.float32)
    y = xf * lax.rsqrt(jnp.mean(xf * xf, axis=-1, keepdims=True) + EPS)
    return (y * g.astype(jnp.float32)).astype(x.dtype)


def _layernorm(x, g, b):
    xf = x.astype(jnp.float32)
    mu = jnp.mean(xf, axis=-1, keepdims=True)
    var = jnp.mean(jnp.square(xf - mu), axis=-1, keepdims=True)
    y = (xf - mu) * lax.rsqrt(var + EPS)
    return (y * g.astype(jnp.float32) + b.astype(jnp.float32)).astype(x.dtype)


def _rope_tables(rows):
    row_id = jnp.repeat(jnp.arange(rows), GRID_W).astype(jnp.float32)
    col_id = jnp.tile(jnp.arange(GRID_W), rows).astype(jnp.float32)
    axis_dim = MLA_ROPE // 2
    freqs = ROPE_THETA ** (-jnp.arange(0, axis_dim, 2, dtype=jnp.float32) / axis_dim)
    ar = row_id[:, None] * freqs
    ac = col_id[:, None] * freqs
    return (jnp.cos(ar), jnp.sin(ar), jnp.cos(ac), jnp.sin(ac))


def _rope1d(x, cos, sin):
    x1, x2 = jnp.split(x, 2, axis=-1)
    return jnp.concatenate([x1 * cos - x2 * sin, x1 * sin + x2 * cos], axis=-1)


def _rope2d(x, tabs):
    cr, sr, cc, sc = [t.astype(x.dtype) for t in tabs]
    if x.ndim == 4:
        cr, sr, cc, sc = cr[:, None], sr[:, None], cc[:, None], sc[:, None]
    xr, xc = jnp.split(x, 2, axis=-1)
    return jnp.concatenate([_rope1d(xr, cr, sr), _rope1d(xc, cc, sc)], axis=-1)


def _conv_mixer(h, w_in, dw, db, ln_g, ln_b, w_out):
    a, b, g = jnp.split(h @ w_in, 3, axis=-1)
    y = a * jax.nn.sigmoid(b)
    y = lax.conv_general_dilated(
        y, dw[:, None, :].astype(y.dtype), window_strides=(1,),
        padding=[(CONV_WIDTH // 2, CONV_WIDTH // 2)],
        dimension_numbers=('NWC', 'WIO', 'NWC'),
        feature_group_count=BRANCH) + db
    y = jax.nn.silu(_layernorm(y, ln_g, ln_b)) * jax.nn.silu(g)
    return y @ w_out


def _window_mean(v, w):
    L = v.shape[1]
    cs = jnp.pad(jnp.cumsum(v.astype(jnp.float32), axis=1), ((0, 0), (1, 0), (0, 0)))
    t = jnp.arange(L)
    start = jnp.clip(t - w // 2, 0, L)
    end = jnp.clip(t + (w - w // 2), 0, L)
    total = jnp.take(cs, end, axis=1) - jnp.take(cs, start, axis=1)
    cnt = (end - start).astype(jnp.float32)
    return (total / cnt[None, :, None]).astype(v.dtype)


def _pool_mixer(h, w_in, w_grp, scale, w_out):
    v, g = jnp.split(h @ w_in, 2, axis=-1)
    B, L, _ = v.shape
    vg = v.reshape(B, L, len(POOL_WINDOWS), POOL_GROUP)
    pooled = jnp.stack([_window_mean(vg[:, :, k], w) for k, w in enumerate(POOL_WINDOWS)], axis=2) - vg
    y = jnp.einsum('blgc,gcd->blgd', pooled, w_grp).reshape(B, L, BRANCH) * scale
    return (y * jax.nn.silu(g)) @ w_out


def _mla_keys(pkv, kv_norm, w_ukv, k_nope_g, k_rope_g, tabs):
    ckv, kr = jnp.split(pkv, [MLA_KV_RANK], axis=-1)
    B, L, _ = ckv.shape
    kv = (_rms(ckv, kv_norm) @ w_ukv).reshape(B, L, MLA_HEADS, MLA_NOPE + MLA_V)
    kn, v = jnp.split(kv, [MLA_NOPE], axis=-1)
    kn = _rms(kn, k_nope_g)
    kr = _rms(kr, k_rope_g)
    if tabs is not None:
        kr = _rope2d(kr, tabs)
    return kn, kr, v


def _mla_queries(cq, q_norm, w_uq, q_nope_g, q_rope_g, tabs):
    B, L, _ = cq.shape
    q = (_rms(cq, q_norm) @ w_uq).reshape(B, L, MLA_HEADS, MLA_NOPE + MLA_ROPE)
    qn, qr = jnp.split(q, [MLA_NOPE], axis=-1)
    qn = _rms(qn, q_nope_g)
    qr = _rms(qr, q_rope_g)
    if tabs is not None:
        qr = _rope2d(qr, tabs)
    return qn, qr


def _attend(qn, qr, kn, kr, v):
    s = (jnp.einsum('bqhd,bkhd->bhqk', qn, kn)
         + jnp.einsum('bqhr,bkr->bhqk', qr, kr)).astype(jnp.float32) * MLA_SCALE
    p = jax.nn.softmax(s, axis=-1).astype(v.dtype)
    return jnp.einsum('bhqk,bkhd->bqhd', p, v)


def _attend_blocks(qn, qr, kn, kr, v):
    B, L, H, _ = qn.shape
    nb = L // Q_BLOCK

    def blk(t):
        return jnp.moveaxis(t.reshape((B, nb, Q_BLOCK) + t.shape[2:]), 1, 0)

    out = lax.map(lambda q: _attend(q[0], q[1], kn, kr, v), (blk(qn), blk(qr)))
    return jnp.moveaxis(out, 0, 1).reshape(B, L, H * MLA_V)


def _mla_mixer(h, hc, with_ctx_out, tabs, w_in, q_norm, kv_norm, w_uq, w_ukv, nope_g, rope_g, w_out):
    pkv, cq, g = jnp.split(h @ w_in, [MLA_KVC, MLA_KVC + MLA_Q_RANK], axis=-1)
    kn, kr, v = _mla_keys(pkv, kv_norm, w_ukv, nope_g[1], rope_g[1], tabs)
    qn, qr = _mla_queries(cq, q_norm, w_uq, nope_g[0], rope_g[0], tabs)
    pc = hc @ (w_in if with_ctx_out else w_in[:, :MLA_KVC])
    knc, krc, vc = _mla_keys(pc[..., :MLA_KVC], kv_norm, w_ukv, nope_g[1], rope_g[1], None)
    o = _attend_blocks(qn, qr,
                       jnp.concatenate([kn, knc], axis=1),
                       jnp.concatenate([kr, krc], axis=1),
                       jnp.concatenate([v, vc], axis=1))
    out = (o * jax.nn.silu(g)) @ w_out
    out_c = None
    if with_ctx_out:
        cqc, gc = jnp.split(pc[..., MLA_KVC:], [MLA_Q_RANK], axis=-1)
        qnc, qrc = _mla_queries(cqc, q_norm, w_uq, nope_g[0], rope_g[0], None)
        Bc, Lc = hc.shape[0], hc.shape[1]
        oc = _attend(qnc, qrc, knc, krc, vc).reshape(Bc, Lc, MLA_HEADS * MLA_V)
        out_c = (oc * jax.nn.silu(gc)) @ w_out
    return out, out_c


def _chunk_mixer(h, w_in, ln_g, ln_b, w_s, b_s, w_out):
    u, v, g = jnp.split(h @ w_in, 3, axis=-1)
    B, L, _ = v.shape
    v = _layernorm(v, ln_g, ln_b).reshape(B, L // CHUNK, CHUNK, CHUNK_GROUPS, CHUNK_GC)
    s = jnp.einsum('gpq,bnqgc->bnpgc', w_s, v) + b_s[:, :, None]
    y = u * s.reshape(B, L, BRANCH) * jax.nn.silu(g)
    return y @ w_out


def _fwd_setup_inputs(seed: int = 0) -> dict:
    key = jax.random.key(seed)
    ks = iter(jax.random.split(key, 40))

    def nrm(shape, s):
        return jax.random.normal(next(ks), shape, jnp.float32) * s

    nA, nB, nC, nD = (_n_layers_of(k) for k in range(N_MIXERS))
    D, E = D_MODEL, BRANCH
    HQK = MLA_HEADS * (MLA_NOPE + MLA_ROPE)
    HKV = MLA_HEADS * (MLA_NOPE + MLA_V)
    HV = MLA_HEADS * MLA_V
    return {
        'x': nrm((BATCH, SEQ, D), 1.0),
        'c': nrm((BATCH, D), 1.0),
        'ctx': nrm((BATCH, CTX_LEN, D), 1.0),
        'c_ctx': nrm((D,), 1.0),
        'norm_g': 1.0 + nrm((DEPTH, D), 0.05),
        'w_mod': nrm((DEPTH, D, 3 * D), 0.5 * D ** -0.5),
        'b_mod': nrm((DEPTH, 3 * D), 0.01),
        'cv_w_in': nrm((nA, D, 3 * E), D ** -0.5),
        'cv_dw': nrm((nA, CONV_WIDTH, E), CONV_WIDTH ** -0.5),
        'cv_db': nrm((nA, E), 0.01),
        'cv_ln_g': 1.0 + nrm((nA, E), 0.05),
        'cv_ln_b': nrm((nA, E), 0.01),
        'cv_w_out': nrm((nA, E, D), E ** -0.5),
        'pl_w_in': nrm((nB, D, 2 * E), D ** -0.5),
        'pl_w_grp': nrm((nB, len(POOL_WINDOWS), POOL_GROUP, POOL_GROUP), POOL_GROUP ** -0.5),
        'pl_scale': 1.0 + nrm((nB, E), 0.05),
        'pl_w_out': nrm((nB, E, D), E ** -0.5),
        'ml_w_in': nrm((nC, D, MLA_KVC + MLA_Q_RANK + HV), D ** -0.5),
        'ml_q_norm': 1.0 + nrm((nC, MLA_Q_RANK), 0.05),
        'ml_kv_norm': 1.0 + nrm((nC, MLA_KV_RANK), 0.05),
        'ml_w_uq': nrm((nC, MLA_Q_RANK, HQK), MLA_Q_RANK ** -0.5),
        'ml_w_ukv': nrm((nC, MLA_KV_RANK, HKV), MLA_KV_RANK ** -0.5),
        'ml_nope_norm': 1.0 + nrm((nC, 2, MLA_NOPE), 0.05),
        'ml_rope_norm': 1.0 + nrm((nC, 2, MLA_ROPE), 0.05),
        'ml_w_out': nrm((nC, HV, D), HV ** -0.5),
        'ch_w_in': nrm((nD, D, 3 * E), D ** -0.5),
        'ch_ln_g': 1.0 + nrm((nD, E), 0.05),
        'ch_ln_b': nrm((nD, E), 0.01),
        'ch_w_s': nrm((nD, CHUNK_GROUPS, CHUNK, CHUNK), CHUNK ** -0.5),
        'ch_b_s': 1.0 + nrm((nD, CHUNK, CHUNK_GROUPS), 0.05),
        'ch_w_out': nrm((nD, E, D), E ** -0.5),
    }


def _fwd_reference(x, c, ctx, c_ctx, norm_g, w_mod, b_mod,
              cv_w_in, cv_dw, cv_db, cv_ln_g, cv_ln_b, cv_w_out,
              pl_w_in, pl_w_grp, pl_scale, pl_w_out,
              ml_w_in, ml_q_norm, ml_kv_norm, ml_w_uq, ml_w_ukv, ml_nope_norm, ml_rope_norm, ml_w_out,
              ch_w_in, ch_ln_g, ch_ln_b, ch_w_s, ch_b_s, ch_w_out):
    L = x.shape[1]
    ROWS = L // GRID_W
    tabs = _rope_tables(ROWS)
    s_lat = jax.nn.silu(c)
    s_ctx = jax.nn.silu(c_ctx)
    cx = ctx
    for i in range(DEPTH):
        kind, j = i % N_MIXERS, i // N_MIXERS
        ctx_out = any(k % N_MIXERS == MIX_MLA for k in range(i + 1, DEPTH))
        ctx_in = ctx_out or kind == MIX_MLA
        sh, sc, gt = jnp.split((s_lat @ w_mod[i] + b_mod[i])[:, None, :], 3, axis=-1)
        h = _rms(x, norm_g[i]) * (1.0 + sc) + sh
        if ctx_in:
            shc, scc, gtc = jnp.split(s_ctx @ w_mod[i] + b_mod[i], 3, axis=-1)
            hc = _rms(cx, norm_g[i]) * (1.0 + scc) + shc
        if kind == MIX_CONV:
            args = (cv_w_in[j], cv_dw[j], cv_db[j], cv_ln_g[j], cv_ln_b[j], cv_w_out[j])
            o = _conv_mixer(h, *args)
            oc = _conv_mixer(hc, *args) if ctx_out else None
        elif kind == MIX_POOL:
            args = (pl_w_in[j], pl_w_grp[j], pl_scale[j], pl_w_out[j])
            o = _pool_mixer(h, *args)
            oc = _pool_mixer(hc, *args) if ctx_out else None
        elif kind == MIX_MLA:
            o, oc = _mla_mixer(h, hc, ctx_out, tabs, ml_w_in[j], ml_q_norm[j], ml_kv_norm[j],
                               ml_w_uq[j], ml_w_ukv[j], ml_nope_norm[j], ml_rope_norm[j], ml_w_out[j])
        else:
            args = (ch_w_in[j], ch_ln_g[j], ch_ln_b[j], ch_w_s[j], ch_b_s[j], ch_w_out[j])
            o = _chunk_mixer(h, *args)
            oc = _chunk_mixer(hc, *args) if ctx_out else None
        x = x + gt * o
        if ctx_out:
            cx = cx + gtc * oc
    return x


import jax as _jax
import jax.numpy as _jnp

TWIN_FORMAT = 'train_step'
FWD_PARAMS = ['x', 'c', 'ctx', 'c_ctx', 'norm_g', 'w_mod', 'b_mod', 'cv_w_in', 'cv_dw', 'cv_db', 'cv_ln_g', 'cv_ln_b', 'cv_w_out', 'pl_w_in', 'pl_w_grp', 'pl_scale', 'pl_w_out', 'ml_w_in', 'ml_q_norm', 'ml_kv_norm', 'ml_w_uq', 'ml_w_ukv', 'ml_nope_norm', 'ml_rope_norm', 'ml_w_out', 'ch_w_in', 'ch_ln_g', 'ch_ln_b', 'ch_w_s', 'ch_b_s', 'ch_w_out']
TWIN_WEIGHTS = ['c_ctx', 'norm_g', 'w_mod', 'b_mod', 'cv_w_in', 'cv_dw', 'cv_db', 'cv_ln_g', 'cv_ln_b', 'cv_w_out', 'pl_w_in', 'pl_w_grp', 'pl_scale', 'pl_w_out', 'ml_w_in', 'ml_q_norm', 'ml_kv_norm', 'ml_w_uq', 'ml_w_ukv', 'ml_nope_norm', 'ml_rope_norm', 'ml_w_out', 'ch_w_in', 'ch_ln_g', 'ch_ln_b', 'ch_w_s', 'ch_b_s', 'ch_w_out']
TWIN_DIFF_INPUT = 'x'
TWIN_INPUTS = ['x', 'c', 'ctx', 'c_ctx', 'norm_g', 'w_mod', 'b_mod', 'cv_w_in', 'cv_dw', 'cv_db', 'cv_ln_g', 'cv_ln_b', 'cv_w_out', 'pl_w_in', 'pl_w_grp', 'pl_scale', 'pl_w_out', 'ml_w_in', 'ml_q_norm', 'ml_kv_norm', 'ml_w_uq', 'ml_w_ukv', 'ml_nope_norm', 'ml_rope_norm', 'ml_w_out', 'ch_w_in', 'ch_ln_g', 'ch_ln_b', 'ch_w_s', 'ch_b_s', 'ch_w_out', 'loss_target', 'm_c_ctx', 'm_norm_g', 'm_w_mod', 'm_b_mod', 'm_cv_w_in', 'm_cv_dw', 'm_cv_db', 'm_cv_ln_g', 'm_cv_ln_b', 'm_cv_w_out', 'm_pl_w_in', 'm_pl_w_grp', 'm_pl_scale', 'm_pl_w_out', 'm_ml_w_in', 'm_ml_q_norm', 'm_ml_kv_norm', 'm_ml_w_uq', 'm_ml_w_ukv', 'm_ml_nope_norm', 'm_ml_rope_norm', 'm_ml_w_out', 'm_ch_w_in', 'm_ch_ln_g', 'm_ch_ln_b', 'm_ch_w_s', 'm_ch_b_s', 'm_ch_w_out', 'v_c_ctx', 'v_norm_g', 'v_w_mod', 'v_b_mod', 'v_cv_w_in', 'v_cv_dw', 'v_cv_db', 'v_cv_ln_g', 'v_cv_ln_b', 'v_cv_w_out', 'v_pl_w_in', 'v_pl_w_grp', 'v_pl_scale', 'v_pl_w_out', 'v_ml_w_in', 'v_ml_q_norm', 'v_ml_kv_norm', 'v_ml_w_uq', 'v_ml_w_ukv', 'v_ml_nope_norm', 'v_ml_rope_norm', 'v_ml_w_out', 'v_ch_w_in', 'v_ch_ln_g', 'v_ch_ln_b', 'v_ch_w_s', 'v_ch_b_s', 'v_ch_w_out']
TWIN_OUTPUTS = ['loss', 'grad_x', 'grad_c_ctx', 'grad_norm_g', 'grad_w_mod', 'grad_b_mod', 'grad_cv_w_in', 'grad_cv_dw', 'grad_cv_db', 'grad_cv_ln_g', 'grad_cv_ln_b', 'grad_cv_w_out', 'grad_pl_w_in', 'grad_pl_w_grp', 'grad_pl_scale', 'grad_pl_w_out', 'grad_ml_w_in', 'grad_ml_q_norm', 'grad_ml_kv_norm', 'grad_ml_w_uq', 'grad_ml_w_ukv', 'grad_ml_nope_norm', 'grad_ml_rope_norm', 'grad_ml_w_out', 'grad_ch_w_in', 'grad_ch_ln_g', 'grad_ch_ln_b', 'grad_ch_w_s', 'grad_ch_b_s', 'grad_ch_w_out', 'delta_c_ctx', 'delta_norm_g', 'delta_w_mod', 'delta_b_mod', 'delta_cv_w_in', 'delta_cv_dw', 'delta_cv_db', 'delta_cv_ln_g', 'delta_cv_ln_b', 'delta_cv_w_out', 'delta_pl_w_in', 'delta_pl_w_grp', 'delta_pl_scale', 'delta_pl_w_out', 'delta_ml_w_in', 'delta_ml_q_norm', 'delta_ml_kv_norm', 'delta_ml_w_uq', 'delta_ml_w_ukv', 'delta_ml_nope_norm', 'delta_ml_rope_norm', 'delta_ml_w_out', 'delta_ch_w_in', 'delta_ch_ln_g', 'delta_ch_ln_b', 'delta_ch_w_s', 'delta_ch_b_s', 'delta_ch_w_out', 'new_m_c_ctx', 'new_m_norm_g', 'new_m_w_mod', 'new_m_b_mod', 'new_m_cv_w_in', 'new_m_cv_dw', 'new_m_cv_db', 'new_m_cv_ln_g', 'new_m_cv_ln_b', 'new_m_cv_w_out', 'new_m_pl_w_in', 'new_m_pl_w_grp', 'new_m_pl_scale', 'new_m_pl_w_out', 'new_m_ml_w_in', 'new_m_ml_q_norm', 'new_m_ml_kv_norm', 'new_m_ml_w_uq', 'new_m_ml_w_ukv', 'new_m_ml_nope_norm', 'new_m_ml_rope_norm', 'new_m_ml_w_out', 'new_m_ch_w_in', 'new_m_ch_ln_g', 'new_m_ch_ln_b', 'new_m_ch_w_s', 'new_m_ch_b_s', 'new_m_ch_w_out', 'new_v_c_ctx', 'new_v_norm_g', 'new_v_w_mod', 'new_v_b_mod', 'new_v_cv_w_in', 'new_v_cv_dw', 'new_v_cv_db', 'new_v_cv_ln_g', 'new_v_cv_ln_b', 'new_v_cv_w_out', 'new_v_pl_w_in', 'new_v_pl_w_grp', 'new_v_pl_scale', 'new_v_pl_w_out', 'new_v_ml_w_in', 'new_v_ml_q_norm', 'new_v_ml_kv_norm', 'new_v_ml_w_uq', 'new_v_ml_w_ukv', 'new_v_ml_nope_norm', 'new_v_ml_rope_norm', 'new_v_ml_w_out', 'new_v_ch_w_in', 'new_v_ch_ln_g', 'new_v_ch_ln_b', 'new_v_ch_w_s', 'new_v_ch_b_s', 'new_v_ch_w_out']
TWIN_LEAF_KINDS = {'loss': 'loss', 'grad_x': 'grad_x', 'grad_c_ctx': 'grad_w', 'grad_norm_g': 'grad_w', 'grad_w_mod': 'grad_w', 'grad_b_mod': 'grad_w', 'grad_cv_w_in': 'grad_w', 'grad_cv_dw': 'grad_w', 'grad_cv_db': 'grad_w', 'grad_cv_ln_g': 'grad_w', 'grad_cv_ln_b': 'grad_w', 'grad_cv_w_out': 'grad_w', 'grad_pl_w_in': 'grad_w', 'grad_pl_w_grp': 'grad_w', 'grad_pl_scale': 'grad_w', 'grad_pl_w_out': 'grad_w', 'grad_ml_w_in': 'grad_w', 'grad_ml_q_norm': 'grad_w', 'grad_ml_kv_norm': 'grad_w', 'grad_ml_w_uq': 'grad_w', 'grad_ml_w_ukv': 'grad_w', 'grad_ml_nope_norm': 'grad_w', 'grad_ml_rope_norm': 'grad_w', 'grad_ml_w_out': 'grad_w', 'grad_ch_w_in': 'grad_w', 'grad_ch_ln_g': 'grad_w', 'grad_ch_ln_b': 'grad_w', 'grad_ch_w_s': 'grad_w', 'grad_ch_b_s': 'grad_w', 'grad_ch_w_out': 'grad_w', 'delta_c_ctx': 'delta_w', 'delta_norm_g': 'delta_w', 'delta_w_mod': 'delta_w', 'delta_b_mod': 'delta_w', 'delta_cv_w_in': 'delta_w', 'delta_cv_dw': 'delta_w', 'delta_cv_db': 'delta_w', 'delta_cv_ln_g': 'delta_w', 'delta_cv_ln_b': 'delta_w', 'delta_cv_w_out': 'delta_w', 'delta_pl_w_in': 'delta_w', 'delta_pl_w_grp': 'delta_w', 'delta_pl_scale': 'delta_w', 'delta_pl_w_out': 'delta_w', 'delta_ml_w_in': 'delta_w', 'delta_ml_q_norm': 'delta_w', 'delta_ml_kv_norm': 'delta_w', 'delta_ml_w_uq': 'delta_w', 'delta_ml_w_ukv': 'delta_w', 'delta_ml_nope_norm': 'delta_w', 'delta_ml_rope_norm': 'delta_w', 'delta_ml_w_out': 'delta_w', 'delta_ch_w_in': 'delta_w', 'delta_ch_ln_g': 'delta_w', 'delta_ch_ln_b': 'delta_w', 'delta_ch_w_s': 'delta_w', 'delta_ch_b_s': 'delta_w', 'delta_ch_w_out': 'delta_w', 'new_m_c_ctx': 'new_m', 'new_m_norm_g': 'new_m', 'new_m_w_mod': 'new_m', 'new_m_b_mod': 'new_m', 'new_m_cv_w_in': 'new_m', 'new_m_cv_dw': 'new_m', 'new_m_cv_db': 'new_m', 'new_m_cv_ln_g': 'new_m', 'new_m_cv_ln_b': 'new_m', 'new_m_cv_w_out': 'new_m', 'new_m_pl_w_in': 'new_m', 'new_m_pl_w_grp': 'new_m', 'new_m_pl_scale': 'new_m', 'new_m_pl_w_out': 'new_m', 'new_m_ml_w_in': 'new_m', 'new_m_ml_q_norm': 'new_m', 'new_m_ml_kv_norm': 'new_m', 'new_m_ml_w_uq': 'new_m', 'new_m_ml_w_ukv': 'new_m', 'new_m_ml_nope_norm': 'new_m', 'new_m_ml_rope_norm': 'new_m', 'new_m_ml_w_out': 'new_m', 'new_m_ch_w_in': 'new_m', 'new_m_ch_ln_g': 'new_m', 'new_m_ch_ln_b': 'new_m', 'new_m_ch_w_s': 'new_m', 'new_m_ch_b_s': 'new_m', 'new_m_ch_w_out': 'new_m', 'new_v_c_ctx': 'new_v', 'new_v_norm_g': 'new_v', 'new_v_w_mod': 'new_v', 'new_v_b_mod': 'new_v', 'new_v_cv_w_in': 'new_v', 'new_v_cv_dw': 'new_v', 'new_v_cv_db': 'new_v', 'new_v_cv_ln_g': 'new_v', 'new_v_cv_ln_b': 'new_v', 'new_v_cv_w_out': 'new_v', 'new_v_pl_w_in': 'new_v', 'new_v_pl_w_grp': 'new_v', 'new_v_pl_scale': 'new_v', 'new_v_pl_w_out': 'new_v', 'new_v_ml_w_in': 'new_v', 'new_v_ml_q_norm': 'new_v', 'new_v_ml_kv_norm': 'new_v', 'new_v_ml_w_uq': 'new_v', 'new_v_ml_w_ukv': 'new_v', 'new_v_ml_nope_norm': 'new_v', 'new_v_ml_rope_norm': 'new_v', 'new_v_ml_w_out': 'new_v', 'new_v_ch_w_in': 'new_v', 'new_v_ch_ln_g': 'new_v', 'new_v_ch_ln_b': 'new_v', 'new_v_ch_w_s': 'new_v', 'new_v_ch_b_s': 'new_v', 'new_v_ch_w_out': 'new_v'}


def _forward(args):
    return _fwd_reference(*[args[k] for k in FWD_PARAMS])


def _output_shape():
    out = _jax.eval_shape(lambda: _forward(_fwd_setup_inputs(0)))
    return out.shape, out.dtype

N_MICROBATCH = 1
ADAM_LR = 0.001
ADAM_B1 = 0.9
ADAM_B2 = 0.999
ADAM_EPS = 1e-08
ADAM_WD = 0.01
ADAM_STEP = 10
PER_EXAMPLE_BATCH_AXIS = {'x': 0, 'c': 0, 'ctx': 0, 'loss_target': 0}
SHARED_INPUTS = []
_WEIGHT_DTYPES = {'c_ctx': _jnp.float32, 'norm_g': _jnp.float32, 'w_mod': _jnp.float32, 'b_mod': _jnp.float32, 'cv_w_in': _jnp.float32, 'cv_dw': _jnp.float32, 'cv_db': _jnp.float32, 'cv_ln_g': _jnp.float32, 'cv_ln_b': _jnp.float32, 'cv_w_out': _jnp.float32, 'pl_w_in': _jnp.float32, 'pl_w_grp': _jnp.float32, 'pl_scale': _jnp.float32, 'pl_w_out': _jnp.float32, 'ml_w_in': _jnp.float32, 'ml_q_norm': _jnp.float32, 'ml_kv_norm': _jnp.float32, 'ml_w_uq': _jnp.float32, 'ml_w_ukv': _jnp.float32, 'ml_nope_norm': _jnp.float32, 'ml_rope_norm': _jnp.float32, 'ml_w_out': _jnp.float32, 'ch_w_in': _jnp.float32, 'ch_ln_g': _jnp.float32, 'ch_ln_b': _jnp.float32, 'ch_w_s': _jnp.float32, 'ch_b_s': _jnp.float32, 'ch_w_out': _jnp.float32}
MOMENT_SCALE = {'c_ctx': 2.005158e-02, 'norm_g': 3.424085e+00, 'w_mod': 1.157648e+00, 'b_mod': 2.538883e+00, 'cv_w_in': 5.037735e-02, 'cv_dw': 5.591226e-02, 'cv_db': 2.289034e-01, 'cv_ln_g': 6.026191e-01, 'cv_ln_b': 3.762426e-01, 'cv_w_out': 4.966669e-02, 'pl_w_in': 9.994706e-02, 'pl_w_grp': 8.264902e-02, 'pl_scale': 1.198260e+00, 'pl_w_out': 6.972715e-02, 'ml_w_in': 7.222479e-02, 'ml_q_norm': 1.013427e-02, 'ml_kv_norm': 3.677277e-01, 'ml_w_uq': 4.920366e-03, 'ml_w_ukv': 4.042466e-02, 'ml_nope_norm': 2.076125e-02, 'ml_rope_norm': 1.378238e-02, 'ml_w_out': 3.997767e-02, 'ch_w_in': 2.137464e-01, 'ch_ln_g': 1.511129e+00, 'ch_ln_b': 2.024471e-01, 'ch_w_s': 1.340699e-01, 'ch_b_s': 1.484911e+00, 'ch_w_out': 1.161411e-01}


def _to_microbatches(a, axis):
    t = _jnp.moveaxis(a, axis, 0)
    t = t.reshape((N_MICROBATCH, t.shape[0] // N_MICROBATCH) + t.shape[1:])
    return _jnp.moveaxis(t, 1, axis + 1)


def setup_inputs(seed: int = 0) -> dict:
    inp = _fwd_setup_inputs(seed)
    key = _jax.random.fold_in(_jax.random.key(seed), 7919)
    shape, _ = _output_shape()
    out = dict(inp)
    out["loss_target"] = _jax.random.normal(_jax.random.fold_in(key, 0), shape, _jnp.float32)
    for i, name in enumerate(TWIN_WEIGHTS):
        w = inp[name].astype(_jnp.float32)
        if MOMENT_SCALE is None:
            s = _jnp.sqrt(_jnp.mean(_jnp.square(w)) + 1e-30)
        else:
            s = MOMENT_SCALE[name]
        km, kv = _jax.random.split(_jax.random.fold_in(key, i + 1))
        out[name] = w
        out["m_" + name] = s * _jax.random.normal(km, w.shape, _jnp.float32)
        out["v_" + name] = (s * s) * _jax.random.uniform(kv, w.shape, _jnp.float32, 0.5, 1.5)
    if N_MICROBATCH > 1:
        for name, axis in PER_EXAMPLE_BATCH_AXIS.items():
            out[name] = _to_microbatches(out[name], axis)
    return {'x': out['x'], 'c': out['c'], 'ctx': out['ctx'], 'c_ctx': out['c_ctx'], 'norm_g': out['norm_g'], 'w_mod': out['w_mod'], 'b_mod': out['b_mod'], 'cv_w_in': out['cv_w_in'], 'cv_dw': out['cv_dw'], 'cv_db': out['cv_db'], 'cv_ln_g': out['cv_ln_g'], 'cv_ln_b': out['cv_ln_b'], 'cv_w_out': out['cv_w_out'], 'pl_w_in': out['pl_w_in'], 'pl_w_grp': out['pl_w_grp'], 'pl_scale': out['pl_scale'], 'pl_w_out': out['pl_w_out'], 'ml_w_in': out['ml_w_in'], 'ml_q_norm': out['ml_q_norm'], 'ml_kv_norm': out['ml_kv_norm'], 'ml_w_uq': out['ml_w_uq'], 'ml_w_ukv': out['ml_w_ukv'], 'ml_nope_norm': out['ml_nope_norm'], 'ml_rope_norm': out['ml_rope_norm'], 'ml_w_out': out['ml_w_out'], 'ch_w_in': out['ch_w_in'], 'ch_ln_g': out['ch_ln_g'], 'ch_ln_b': out['ch_ln_b'], 'ch_w_s': out['ch_w_s'], 'ch_b_s': out['ch_b_s'], 'ch_w_out': out['ch_w_out'], 'loss_target': out['loss_target'], 'm_c_ctx': out['m_c_ctx'], 'm_norm_g': out['m_norm_g'], 'm_w_mod': out['m_w_mod'], 'm_b_mod': out['m_b_mod'], 'm_cv_w_in': out['m_cv_w_in'], 'm_cv_dw': out['m_cv_dw'], 'm_cv_db': out['m_cv_db'], 'm_cv_ln_g': out['m_cv_ln_g'], 'm_cv_ln_b': out['m_cv_ln_b'], 'm_cv_w_out': out['m_cv_w_out'], 'm_pl_w_in': out['m_pl_w_in'], 'm_pl_w_grp': out['m_pl_w_grp'], 'm_pl_scale': out['m_pl_scale'], 'm_pl_w_out': out['m_pl_w_out'], 'm_ml_w_in': out['m_ml_w_in'], 'm_ml_q_norm': out['m_ml_q_norm'], 'm_ml_kv_norm': out['m_ml_kv_norm'], 'm_ml_w_uq': out['m_ml_w_uq'], 'm_ml_w_ukv': out['m_ml_w_ukv'], 'm_ml_nope_norm': out['m_ml_nope_norm'], 'm_ml_rope_norm': out['m_ml_rope_norm'], 'm_ml_w_out': out['m_ml_w_out'], 'm_ch_w_in': out['m_ch_w_in'], 'm_ch_ln_g': out['m_ch_ln_g'], 'm_ch_ln_b': out['m_ch_ln_b'], 'm_ch_w_s': out['m_ch_w_s'], 'm_ch_b_s': out['m_ch_b_s'], 'm_ch_w_out': out['m_ch_w_out'], 'v_c_ctx': out['v_c_ctx'], 'v_norm_g': out['v_norm_g'], 'v_w_mod': out['v_w_mod'], 'v_b_mod': out['v_b_mod'], 'v_cv_w_in': out['v_cv_w_in'], 'v_cv_dw': out['v_cv_dw'], 'v_cv_db': out['v_cv_db'], 'v_cv_ln_g': out['v_cv_ln_g'], 'v_cv_ln_b': out['v_cv_ln_b'], 'v_cv_w_out': out['v_cv_w_out'], 'v_pl_w_in': out['v_pl_w_in'], 'v_pl_w_grp': out['v_pl_w_grp'], 'v_pl_scale': out['v_pl_scale'], 'v_pl_w_out': out['v_pl_w_out'], 'v_ml_w_in': out['v_ml_w_in'], 'v_ml_q_norm': out['v_ml_q_norm'], 'v_ml_kv_norm': out['v_ml_kv_norm'], 'v_ml_w_uq': out['v_ml_w_uq'], 'v_ml_w_ukv': out['v_ml_w_ukv'], 'v_ml_nope_norm': out['v_ml_nope_norm'], 'v_ml_rope_norm': out['v_ml_rope_norm'], 'v_ml_w_out': out['v_ml_w_out'], 'v_ch_w_in': out['v_ch_w_in'], 'v_ch_ln_g': out['v_ch_ln_g'], 'v_ch_ln_b': out['v_ch_ln_b'], 'v_ch_w_s': out['v_ch_w_s'], 'v_ch_b_s': out['v_ch_b_s'], 'v_ch_w_out': out['v_ch_w_out']}


def _loss(weights, diff, rest, loss_target):
    with _jax.named_scope("forward"):
        args = {**rest, TWIN_DIFF_INPUT: diff, **{k: w.astype(_WEIGHT_DTYPES[k]) for k, w in weights.items()}}
        y = _forward(args)
    with _jax.named_scope("loss_head"):
        err = _jnp.square(y.astype(_jnp.float32) - loss_target)
        return 0.5 * _jnp.sum(_jnp.mean(err, axis=-1)) if err.ndim else 0.5 * err


def _adamw(w, g, m, v):
    m = ADAM_B1 * m + (1.0 - ADAM_B1) * g
    v = ADAM_B2 * v + (1.0 - ADAM_B2) * _jnp.square(g)
    m_hat = m / (1.0 - ADAM_B1 ** ADAM_STEP)
    v_hat = v / (1.0 - ADAM_B2 ** ADAM_STEP)
    delta = -ADAM_LR * (m_hat / (_jnp.sqrt(v_hat) + ADAM_EPS) + ADAM_WD * w)
    return delta, m, v


def reference(x, c, ctx, c_ctx, norm_g, w_mod, b_mod, cv_w_in, cv_dw, cv_db, cv_ln_g, cv_ln_b, cv_w_out, pl_w_in, pl_w_grp, pl_scale, pl_w_out, ml_w_in, ml_q_norm, ml_kv_norm, ml_w_uq, ml_w_ukv, ml_nope_norm, ml_rope_norm, ml_w_out, ch_w_in, ch_ln_g, ch_ln_b, ch_w_s, ch_b_s, ch_w_out, loss_target, m_c_ctx, m_norm_g, m_w_mod, m_b_mod, m_cv_w_in, m_cv_dw, m_cv_db, m_cv_ln_g, m_cv_ln_b, m_cv_w_out, m_pl_w_in, m_pl_w_grp, m_pl_scale, m_pl_w_out, m_ml_w_in, m_ml_q_norm, m_ml_kv_norm, m_ml_w_uq, m_ml_w_ukv, m_ml_nope_norm, m_ml_rope_norm, m_ml_w_out, m_ch_w_in, m_ch_ln_g, m_ch_ln_b, m_ch_w_s, m_ch_b_s, m_ch_w_out, v_c_ctx, v_norm_g, v_w_mod, v_b_mod, v_cv_w_in, v_cv_dw, v_cv_db, v_cv_ln_g, v_cv_ln_b, v_cv_w_out, v_pl_w_in, v_pl_w_grp, v_pl_scale, v_pl_w_out, v_ml_w_in, v_ml_q_norm, v_ml_kv_norm, v_ml_w_uq, v_ml_w_ukv, v_ml_nope_norm, v_ml_rope_norm, v_ml_w_out, v_ch_w_in, v_ch_ln_g, v_ch_ln_b, v_ch_w_s, v_ch_b_s, v_ch_w_out):
    given = dict(x=x, c=c, ctx=ctx, c_ctx=c_ctx, norm_g=norm_g, w_mod=w_mod, b_mod=b_mod, cv_w_in=cv_w_in, cv_dw=cv_dw, cv_db=cv_db, cv_ln_g=cv_ln_g, cv_ln_b=cv_ln_b, cv_w_out=cv_w_out, pl_w_in=pl_w_in, pl_w_grp=pl_w_grp, pl_scale=pl_scale, pl_w_out=pl_w_out, ml_w_in=ml_w_in, ml_q_norm=ml_q_norm, ml_kv_norm=ml_kv_norm, ml_w_uq=ml_w_uq, ml_w_ukv=ml_w_ukv, ml_nope_norm=ml_nope_norm, ml_rope_norm=ml_rope_norm, ml_w_out=ml_w_out, ch_w_in=ch_w_in, ch_ln_g=ch_ln_g, ch_ln_b=ch_ln_b, ch_w_s=ch_w_s, ch_b_s=ch_b_s, ch_w_out=ch_w_out, loss_target=loss_target, m_c_ctx=m_c_ctx, m_norm_g=m_norm_g, m_w_mod=m_w_mod, m_b_mod=m_b_mod, m_cv_w_in=m_cv_w_in, m_cv_dw=m_cv_dw, m_cv_db=m_cv_db, m_cv_ln_g=m_cv_ln_g, m_cv_ln_b=m_cv_ln_b, m_cv_w_out=m_cv_w_out, m_pl_w_in=m_pl_w_in, m_pl_w_grp=m_pl_w_grp, m_pl_scale=m_pl_scale, m_pl_w_out=m_pl_w_out, m_ml_w_in=m_ml_w_in, m_ml_q_norm=m_ml_q_norm, m_ml_kv_norm=m_ml_kv_norm, m_ml_w_uq=m_ml_w_uq, m_ml_w_ukv=m_ml_w_ukv, m_ml_nope_norm=m_ml_nope_norm, m_ml_rope_norm=m_ml_rope_norm, m_ml_w_out=m_ml_w_out, m_ch_w_in=m_ch_w_in, m_ch_ln_g=m_ch_ln_g, m_ch_ln_b=m_ch_ln_b, m_ch_w_s=m_ch_w_s, m_ch_b_s=m_ch_b_s, m_ch_w_out=m_ch_w_out, v_c_ctx=v_c_ctx, v_norm_g=v_norm_g, v_w_mod=v_w_mod, v_b_mod=v_b_mod, v_cv_w_in=v_cv_w_in, v_cv_dw=v_cv_dw, v_cv_db=v_cv_db, v_cv_ln_g=v_cv_ln_g, v_cv_ln_b=v_cv_ln_b, v_cv_w_out=v_cv_w_out, v_pl_w_in=v_pl_w_in, v_pl_w_grp=v_pl_w_grp, v_pl_scale=v_pl_scale, v_pl_w_out=v_pl_w_out, v_ml_w_in=v_ml_w_in, v_ml_q_norm=v_ml_q_norm, v_ml_kv_norm=v_ml_kv_norm, v_ml_w_uq=v_ml_w_uq, v_ml_w_ukv=v_ml_w_ukv, v_ml_nope_norm=v_ml_nope_norm, v_ml_rope_norm=v_ml_rope_norm, v_ml_w_out=v_ml_w_out, v_ch_w_in=v_ch_w_in, v_ch_ln_g=v_ch_ln_g, v_ch_ln_b=v_ch_ln_b, v_ch_w_s=v_ch_w_s, v_ch_b_s=v_ch_b_s, v_ch_w_out=v_ch_w_out)
    weights = {n: given[n] for n in TWIN_WEIGHTS}
    shared = {n: given[n] for n in SHARED_INPUTS}
    per_example = {n: given[n] for n in ['x', 'c', 'ctx']}
    grad_fn = _jax.value_and_grad(_loss, argnums=(0, 1))

    def one_microbatch(ex, loss_target):
        ex = dict(ex)
        diff = ex.pop(TWIN_DIFF_INPUT)
        return grad_fn(weights, diff, {**shared, **ex}, loss_target)

    if N_MICROBATCH == 1:
        loss, (grad_w, grad_x) = one_microbatch(per_example, given["loss_target"])
    else:
        def body(carry, xs):
            loss_sum, grad_sum = carry
            l_k, (gw_k, gx_k) = one_microbatch(xs[0], xs[1])
            with _jax.named_scope("update"):
                return (loss_sum + l_k, _jax.tree.map(_jnp.add, grad_sum, gw_k)), gx_k

        init = (_jnp.zeros((), _jnp.float32), _jax.tree.map(_jnp.zeros_like, weights))
        (loss, grad_w), grad_x = _jax.lax.scan(body, init, (per_example, given["loss_target"]))
    with _jax.named_scope("update"):
        delta_w, new_m, new_v = {}, {}, {}
        for n in TWIN_WEIGHTS:
            delta_w[n], new_m[n], new_v[n] = _adamw(weights[n], grad_w[n], given["m_" + n], given["v_" + n])
    return (loss, grad_x, *[grad_w[n] for n in TWIN_WEIGHTS], *[delta_w[n] for n in TWIN_WEIGHTS],
            *[new_m[n] for n in TWIN_WEIGHTS], *[new_v[n] for n in TWIN_WEIGHTS])
```

```python
import functools
import math

import jax
import jax.numpy as jnp
from jax import lax
from jax.experimental import pallas as pl
from jax.experimental.pallas import tpu as pltpu

F32 = jnp.float32
BF16 = jnp.bfloat16
MESH = pl.DeviceIdType.MESH

EPS = 1e-6
GRID_W = 64
CONV_WIDTH = 31
CONV_HALF = CONV_WIDTH // 2
CONV_PAD = 16
POOL_WINDOWS = (2, 4, 8, 16)
POOL_HALF = max(POOL_WINDOWS) // 2
HEADS = 8
NOPE = 128
ROPE = 64
HEAD_W = 256
VDIM = 128
KV_RANK = 256
Q_RANK = 384
ATT_SCALE = (NOPE + ROPE) ** -0.5
ROPE_THETA = 10000.0
CHUNK = 128
CHUNK_GROUPS = 8
LANES = 128
TM = 256
TQ = 256
VMEM_LIMIT = 56 * 1024 * 1024

ADAM_LR = 0.001
ADAM_B1 = 0.9
ADAM_B2 = 0.999
ADAM_EPS = 1e-08
ADAM_WD = 0.01
ADAM_STEP = 10


def _dot(a, b):
    return jnp.dot(a.astype(BF16), b.astype(BF16), preferred_element_type=F32)


def _dot_nt(a, b):
    return lax.dot_general(a.astype(BF16), b.astype(BF16), (((1,), (1,)), ((), ())), preferred_element_type=F32)


def _dot_tn(a, b):
    return lax.dot_general(a.astype(BF16), b.astype(BF16), (((0,), (0,)), ((), ())), preferred_element_type=F32)


@jax.custom_vjp
def _mm(a, w):
    return _dot(a, w)


def _mm_fwd(a, w):
    return _dot(a, w), (a, w)


def _mm_bwd(res, ct):
    a, w = res
    return _dot_nt(ct, w), _dot_tn(a, ct)


_mm.defvjp(_mm_fwd, _mm_bwd)


def _swap16_impl(x):
    n = x.shape[-1]
    ax = x.ndim - 1
    lane = lax.broadcasted_iota(jnp.int32, x.shape, ax)
    up = pltpu.roll(x, n - 16, ax)
    dn = pltpu.roll(x, 16, ax)
    return jnp.where((lane % 32) < 16, up, dn)


@jax.custom_vjp
def _swap16(x):
    return _swap16_impl(x)


_swap16.defvjp(lambda x: (_swap16_impl(x), None), lambda _, ct: (_swap16_impl(ct),))


def _rms(x, g, n=None):
    n = x.shape[-1] if n is None else n
    return x * lax.rsqrt(jnp.sum(x * x, axis=-1, keepdims=True) * (1.0 / n) + EPS) * g


def _layernorm(x, g, b):
    mu = jnp.mean(x, axis=-1, keepdims=True)
    xc = x - mu
    var = jnp.mean(xc * xc, axis=-1, keepdims=True)
    return xc * lax.rsqrt(var + EPS) * g + b


def _silu(x):
    return x * jax.nn.sigmoid(x)


def _rope(x, cos, sin):
    return x * cos + _swap16(x) * sin


def _const_spec(shape, single=False):
    nd = len(shape)
    if single:
        return pl.BlockSpec(shape, lambda b, i: (0,) * nd, pipeline_mode=pl.Buffered(1))
    return pl.BlockSpec(shape, lambda b, i: (0,) * nd)


def _tile_spec(arr, n_lat_tiles, lat_only=False):
    bt, _, cw = arr.shape
    if lat_only:
        return pl.BlockSpec((1, TM, cw), lambda b, i: (b if bt > 1 else 0, jnp.minimum(i, n_lat_tiles - 1), 0))
    return pl.BlockSpec((1, TM, cw), lambda b, i: (b if bt > 1 else 0, i, 0))


def _eparam_spec(arr, n_lat_tiles):
    cw = arr.shape[-1]
    return pl.BlockSpec((1, 1, 1, cw), lambda b, i: (b, (i >= n_lat_tiles).astype(jnp.int32), 0, 0))


def _stage_fwd(name, *, pre, post, wsel, splits, tiles, eparams, sparams, weights, out_widths, out_dtypes,
               batch, n_tiles, n_lat_tiles):
    nt, ne, ns, nw = len(tiles), len(eparams), len(sparams), len(weights)

    def body(*refs):
        t_refs = refs[:nt]
        e_refs = refs[nt:nt + ne]
        s_refs = refs[nt + ne:nt + ne + ns]
        w_refs = refs[nt + ne + ns:nt + ne + ns + nw]
        o_refs = refs[nt + ne + ns + nw:]
        tv = [r[0].astype(F32) for r in t_refs]
        ev = [r[0, 0] for r in e_refs]
        sv = [r[...] for r in s_refs]
        a = pre(tv, ev, sv)
        z = [_dot(a[wsel[j]], w_refs[j][...]) for j in range(nw)]
        if post is None:
            outs = [z[j][:, s:s + w] for (j, s, w) in splits]
        else:
            outs = post(z, tv, ev, sv)
        for o_ref, o in zip(o_refs, outs):
            o_ref[0] = o.astype(o_ref.dtype)

    in_specs = ([_tile_spec(t, n_lat_tiles) for t in tiles] + [_eparam_spec(e, n_lat_tiles) for e in eparams]
                + [_const_spec(s.shape) for s in sparams] + [_const_spec(w.shape, single=True) for w in weights])
    out_shape = [jax.ShapeDtypeStruct((batch, n_tiles * TM, w), dt) for w, dt in zip(out_widths, out_dtypes)]
    out_specs = [pl.BlockSpec((1, TM, w), lambda b, i: (b, i, 0)) for w in out_widths]
    return pl.pallas_call(
        body, name=name, grid=(batch, n_tiles), in_specs=in_specs, out_specs=out_specs, out_shape=out_shape,
        compiler_params=pltpu.CompilerParams(dimension_semantics=("arbitrary", "arbitrary"),
                                             vmem_limit_bytes=VMEM_LIMIT),
    )(*tiles, *eparams, *sparams, *weights)


def _stage_bwd(name, *, pre, post, wsel, splits, tiles, tile_diff, eparams, sparams, weights, cots, cot_lat_only,
               batch, n_tiles, n_lat_tiles, add=None, add_lat_only=False):
    nt, ne, ns, nw, nc = len(tiles), len(eparams), len(sparams), len(weights), len(cots)
    diff_idx = [k for k in range(nt) if tile_diff[k]]
    nd = len(diff_idx)
    has_add = add is not None

    def body(*refs):
        pos = 0
        t_refs = refs[pos:pos + nt]; pos += nt
        e_refs = refs[pos:pos + ne]; pos += ne
        s_refs = refs[pos:pos + ns]; pos += ns
        w_refs = refs[pos:pos + nw]; pos += nw
        c_refs = refs[pos:pos + nc]; pos += nc
        if has_add:
            add_ref = refs[pos]; pos += 1
        dt_refs = refs[pos:pos + nd]; pos += nd
        de_refs = refs[pos:pos + ne]; pos += ne
        ds_refs = refs[pos:pos + ns]; pos += ns
        dw_refs = refs[pos:pos + nw]; pos += nw

        b = pl.program_id(0)
        i = pl.program_id(1)
        is_lat = i < n_lat_tiles
        tv = [r[0].astype(F32) for r in t_refs]
        ev = tuple(r[0, 0] for r in e_refs)
        sv = tuple(r[...] for r in s_refs)
        dv0 = tuple(tv[k] for k in diff_idx)

        def merge(dv):
            full = list(tv)
            for k, v in zip(diff_idx, dv):
                full[k] = v
            return full

        def pre_f(dv, ev_, sv_):
            return tuple(pre(merge(dv), list(ev_), list(sv_)))

        a, vjp_pre = jax.vjp(pre_f, dv0, ev, sv)
        cv = []
        for c_ref, lat in zip(c_refs, cot_lat_only):
            c = c_ref[0].astype(F32)
            cv.append(jnp.where(is_lat, c, 0.0) if lat else c)
        if post is None:
            dz = []
            for j in range(nw):
                parts = [cv[k] for k, (jj, _, _) in enumerate(splits) if jj == j]
                dz.append(parts[0] if len(parts) == 1 else jnp.concatenate(parts, axis=1))
            dt2 = de2 = ds2 = None
        else:
            z = tuple(_dot(a[wsel[j]], w_refs[j][...]) for j in range(nw))

            def post_f(z_, dv, ev_, sv_):
                return tuple(post(list(z_), merge(dv), list(ev_), list(sv_)))

            _, vjp_post = jax.vjp(post_f, z, dv0, ev, sv)
            dz, dt2, de2, ds2 = vjp_post(tuple(cv))
        da = [None] * len(a)
        dws = []
        for j in range(nw):
            g = _dot_nt(dz[j], w_refs[j][...])
            da[wsel[j]] = g if da[wsel[j]] is None else da[wsel[j]] + g
            dws.append(_dot_tn(a[wsel[j]], dz[j]))
        da = tuple(jnp.zeros_like(a[k]) if da[k] is None else da[k] for k in range(len(a)))
        dt1, de1, ds1 = vjp_pre(da)

        def plus(u, v):
            return u if v is None else u + v

        for k in range(nd):
            val = plus(dt1[k], None if dt2 is None else dt2[k])
            if has_add and k == 0:
                addv = add_ref[0].astype(F32)
                val = val + (jnp.where(is_lat, addv, 0.0) if add_lat_only else addv)
            dt_refs[k][0] = val.astype(dt_refs[k].dtype)

        seg_first = jnp.logical_or(i == 0, i == n_lat_tiles)
        for k in range(ne):
            val = plus(de1[k], None if de2 is None else de2[k])

            @pl.when(seg_first)
            def _(k=k, val=val):
                de_refs[k][0, 0] = val

            @pl.when(jnp.logical_not(seg_first))
            def _(k=k, val=val):
                de_refs[k][0, 0] += val

        first = jnp.logical_and(b == 0, i == 0)
        acc = [(ds_refs[k], plus(ds1[k], None if ds2 is None else ds2[k])) for k in range(ns)]
        acc += [(dw_refs[j], dws[j]) for j in range(nw)]
        for ref, val in acc:
            @pl.when(first)
            def _(ref=ref, val=val):
                ref[...] = val

            @pl.when(jnp.logical_not(first))
            def _(ref=ref, val=val):
                ref[...] += val

    in_specs = ([_tile_spec(t, n_lat_tiles) for t in tiles] + [_eparam_spec(e, n_lat_tiles) for e in eparams]
                + [_const_spec(s.shape) for s in sparams] + [_const_spec(w.shape, single=True) for w in weights]
                + [_tile_spec(c, n_lat_tiles, lat) for c, lat in zip(cots, cot_lat_only)])
    args = [*tiles, *eparams, *sparams, *weights, *cots]
    if has_add:
        in_specs.append(_tile_spec(add, n_lat_tiles, add_lat_only))
        args.append(add)
    out_shape = [jax.ShapeDtypeStruct((batch, n_tiles * TM, tiles[k].shape[-1]), F32) for k in diff_idx]
    out_specs = [pl.BlockSpec((1, TM, tiles[k].shape[-1]), lambda b, i: (b, i, 0)) for k in diff_idx]
    out_shape += [jax.ShapeDtypeStruct(e.shape, F32) for e in eparams]
    out_specs += [_eparam_spec(e, n_lat_tiles) for e in eparams]
    out_shape += [jax.ShapeDtypeStruct(s.shape, F32) for s in sparams]
    out_specs += [_const_spec(s.shape) for s in sparams]
    out_shape += [jax.ShapeDtypeStruct(w.shape, F32) for w in weights]
    out_specs += [_const_spec(w.shape, single=True) for w in weights]
    res = pl.pallas_call(
        body, name=name, grid=(batch, n_tiles), in_specs=in_specs, out_specs=out_specs, out_shape=out_shape,
        compiler_params=pltpu.CompilerParams(dimension_semantics=("arbitrary", "arbitrary"),
                                             vmem_limit_bytes=VMEM_LIMIT),
    )(*args)
    return res[:nd], res[nd:nd + ne], res[nd + ne:nd + ne + ns], res[nd + ne + ns:]


def _pre_adaln(tv, ev, sv):
    x = tv[0]
    sh, sc = ev[0], ev[1]
    return [_rms(x, sv[0]) * (1.0 + sc) + sh]


def _post_residual(x_index):
    def post(z, tv, ev, sv):
        return [tv[x_index] + ev[-1] * z[0]]
    return post


def _pre_conv_out(tv, ev, sv):
    c1, gg = tv[0], tv[1]
    return [_silu(_layernorm(c1, sv[0], sv[1])) * _silu(gg)]


def _pre_pool_out(tv, ev, sv):
    pooled, gg = tv[0], tv[1]
    w_grp, scale = sv[0], sv[1]
    gw = w_grp.shape[-1]
    y = jnp.concatenate([_mm(pooled[:, k * gw:(k + 1) * gw], w_grp[k]) for k in range(w_grp.shape[0])], axis=1)
    return [y * scale * _silu(gg)]


def _pre_rms_only(tv, ev, sv):
    return [_rms(tv[0], sv[0])]


def _post_mla_keys(z, tv, ev, sv):
    krp, cos, sin = tv[1], tv[2], tv[3]
    nope_g, rope_g = sv[1], sv[2]
    kv = z[0]
    kr = _rope(_rms(krp, rope_g, ROPE), cos, sin)
    ks, vs = [], []
    for h in range(HEADS):
        ks.append(_rms(kv[:, h * 2 * NOPE:h * 2 * NOPE + NOPE], nope_g))
        ks.append(kr)
        vs.append(kv[:, h * 2 * NOPE + NOPE:(h + 1) * 2 * NOPE])
    return [jnp.concatenate(ks, axis=1), jnp.concatenate(vs, axis=1)]


def _post_mla_queries(z, tv, ev, sv):
    cos, sin = tv[1], tv[2]
    nope_g, rope_g = sv[1], sv[2]
    q = z[0]
    qs = []
    for h in range(HEADS):
        qs.append(_rms(q[:, h * HEAD_W:h * HEAD_W + NOPE], nope_g))
        qs.append(_rope(_rms(q[:, h * HEAD_W + NOPE:(h + 1) * HEAD_W], rope_g, ROPE), cos, sin))
    return [jnp.concatenate(qs, axis=1)]


def _pre_mla_out(tv, ev, sv):
    return [tv[0] * _silu(tv[1])]


def _pre_chunk_out(tv, ev, sv):
    u, v, gg = tv[0], tv[1], tv[2]
    ln_g, ln_b, w_s, b_s = sv
    vn = _layernorm(v, ln_g, ln_b)
    rows = []
    for n in range(vn.shape[0] // CHUNK):
        blk = vn[n * CHUNK:(n + 1) * CHUNK]
        cols = [_mm(w_s[g], blk[:, g * LANES:(g + 1) * LANES]) + b_s[:, g:g + 1] for g in range(CHUNK_GROUPS)]
        rows.append(jnp.concatenate(cols, axis=1))
    s = jnp.concatenate(rows, axis=0)
    return [u * s * _silu(gg)]


def _segments(lat_len, tot_len):
    segs = [(0, lat_len)]
    if tot_len > lat_len:
        segs.append((lat_len, tot_len - lat_len))
    return segs


def _shifted(x, j):
    if j == 0:
        return x
    n = x.shape[0]
    rows = lax.broadcasted_iota(jnp.int32, x.shape, 0)
    r = pltpu.roll(x, (-j) % n, 0)
    return jnp.where(jnp.logical_and(rows + j >= 0, rows + j < n), r, 0.0)


def _conv_fwd(a, bgate, dw, db, lat_len):
    batch, tot, e = a.shape
    segs = _segments(lat_len, tot)

    def body(a_ref, b_ref, dw_ref, db_ref, o_ref):
        w = dw_ref[...]
        for (s0, n) in segs:
            y = a_ref[0, s0:s0 + n, :] * jax.nn.sigmoid(b_ref[0, s0:s0 + n, :])
            acc = jnp.zeros_like(y) + db_ref[...]
            for k in range(CONV_WIDTH):
                acc = acc + _shifted(y, k - CONV_HALF) * w[k:k + 1, :]
            o_ref[0, s0:s0 + n, :] = acc

    blk = pl.BlockSpec((1, tot, LANES), lambda b, cb: (b, 0, cb))
    return pl.pallas_call(
        body, name="conv_fwd", grid=(batch, e // LANES),
        in_specs=[blk, blk, pl.BlockSpec((CONV_WIDTH, LANES), lambda b, cb: (0, cb)),
                  pl.BlockSpec((1, LANES), lambda b, cb: (0, cb))],
        out_specs=blk, out_shape=jax.ShapeDtypeStruct(a.shape, F32),
        compiler_params=pltpu.CompilerParams(dimension_semantics=("arbitrary", "arbitrary"),
                                             vmem_limit_bytes=VMEM_LIMIT),
    )(a, bgate, dw, db)


def _conv_bwd(a, bgate, dw, dc1, lat_len):
    batch, tot, e = a.shape
    segs = _segments(lat_len, tot)

    def body(a_ref, b_ref, dw_ref, dc_ref, da_ref, dg_ref, ddw_ref, ddb_ref):
        b = pl.program_id(1)
        w = dw_ref[...]
        ddw_rows = [None] * CONV_WIDTH
        ddb = None
        for (s0, n) in segs:
            av = a_ref[0, s0:s0 + n, :]
            sg = jax.nn.sigmoid(b_ref[0, s0:s0 + n, :])
            y = av * sg
            dc = dc_ref[0, s0:s0 + n, :]
            dy = jnp.zeros_like(y)
            for k in range(CONV_WIDTH):
                j = k - CONV_HALF
                dy = dy + _shifted(dc, -j) * w[k:k + 1, :]
                r = jnp.sum(dc * _shifted(y, j), axis=0, keepdims=True)
                ddw_rows[k] = r if ddw_rows[k] is None else ddw_rows[k] + r
            r = jnp.sum(dc, axis=0, keepdims=True)
            ddb = r if ddb is None else ddb + r
            da_ref[0, s0:s0 + n, :] = dy * sg
            dg_ref[0, s0:s0 + n, :] = dy * av * sg * (1.0 - sg)

        @pl.when(b == 0)
        def _():
            ddw_ref[...] = jnp.zeros_like(ddw_ref)
            ddb_ref[...] = jnp.zeros_like(ddb_ref)

        for k in range(CONV_WIDTH):
            ddw_ref[k:k + 1, :] += ddw_rows[k]
        ddb_ref[...] += ddb

    blk = pl.BlockSpec((1, tot, LANES), lambda cb, b: (b, 0, cb))
    wspec = pl.BlockSpec((CONV_WIDTH, LANES), lambda cb, b: (0, cb))
    bspec = pl.BlockSpec((1, LANES), lambda cb, b: (0, cb))
    return pl.pallas_call(
        body, name="conv_bwd", grid=(e // LANES, batch),
        in_specs=[blk, blk, wspec, blk],
        out_specs=[blk, blk, wspec, bspec],
        out_shape=[jax.ShapeDtypeStruct(a.shape, F32), jax.ShapeDtypeStruct(a.shape, F32),
                   jax.ShapeDtypeStruct((CONV_WIDTH, e), F32), jax.ShapeDtypeStruct((1, e), F32)],
        compiler_params=pltpu.CompilerParams(dimension_semantics=("arbitrary", "arbitrary"),
                                             vmem_limit_bytes=VMEM_LIMIT),
    )(a, bgate, dw, dc1)


def _pool_taps(group):
    half = lax.shift_left(jnp.int32(1), group)
    taps = []
    for j in range(-POOL_HALF, POOL_HALF):
        inside = jnp.logical_and(j >= -half, j < half)
        taps.append(jnp.where(inside, 1.0, 0.0).astype(F32))
    return taps, half


def _pool_counts(n, half, shape):
    t = lax.broadcasted_iota(jnp.int32, shape, 0)
    cnt = jnp.minimum(t + half, n) - jnp.maximum(t - half, 0)
    return cnt.astype(F32)


def _pool_fwd(v, lat_len):
    batch, tot, e = v.shape
    gw = e // len(POOL_WINDOWS)
    segs = _segments(lat_len, tot)

    def body(v_ref, o_ref):
        taps, half = _pool_taps(pl.program_id(1))
        for (s0, n) in segs:
            x = v_ref[0, s0:s0 + n, :]
            acc = jnp.zeros_like(x)
            for idx, j in enumerate(range(-POOL_HALF, POOL_HALF)):
                acc = acc + _shifted(x, j) * taps[idx]
            o_ref[0, s0:s0 + n, :] = acc / _pool_counts(n, half, x.shape) - x

    blk = pl.BlockSpec((1, tot, gw), lambda b, g: (b, 0, g))
    return pl.pallas_call(
        body, name="pool_fwd", grid=(batch, len(POOL_WINDOWS)), in_specs=[blk], out_specs=blk,
        out_shape=jax.ShapeDtypeStruct(v.shape, F32),
        compiler_params=pltpu.CompilerParams(dimension_semantics=("arbitrary", "arbitrary"),
                                             vmem_limit_bytes=VMEM_LIMIT),
    )(v)


def _pool_bwd(dp, lat_len):
    batch, tot, e = dp.shape
    gw = e // len(POOL_WINDOWS)
    segs = _segments(lat_len, tot)

    def body(d_ref, o_ref):
        taps, half = _pool_taps(pl.program_id(1))
        for (s0, n) in segs:
            d = d_ref[0, s0:s0 + n, :]
            dn = d / _pool_counts(n, half, d.shape)
            acc = jnp.zeros_like(d)
            for idx, j in enumerate(range(-POOL_HALF, POOL_HALF)):
                acc = acc + _shifted(dn, -j) * taps[idx]
            o_ref[0, s0:s0 + n, :] = acc - d

    blk = pl.BlockSpec((1, tot, gw), lambda b, g: (b, 0, g))
    return pl.pallas_call(
        body, name="pool_bwd", grid=(batch, len(POOL_WINDOWS)), in_specs=[blk], out_specs=blk,
        out_shape=jax.ShapeDtypeStruct(dp.shape, F32),
        compiler_params=pltpu.CompilerParams(dimension_semantics=("arbitrary", "arbitrary"),
                                             vmem_limit_bytes=VMEM_LIMIT),
    )(dp)


def _softmax_rows(q, k):
    s = _dot_nt(q, k) * ATT_SCALE
    m = jnp.max(s, axis=-1, keepdims=True)
    e = jnp.exp(s - m)
    return e / jnp.sum(e, axis=-1, keepdims=True)


def _attn_fwd(q, k, v):
    batch, lq, _ = q.shape
    tk = k.shape[1]

    def body(q_ref, k_ref, v_ref, o_ref):
        p = _softmax_rows(q_ref[0], k_ref[0])
        o_ref[0] = _dot(p, v_ref[0])

    return pl.pallas_call(
        body, name="attn_fwd", grid=(batch, HEADS, lq // TQ),
        in_specs=[pl.BlockSpec((1, TQ, HEAD_W), lambda b, h, i: (b, i, h)),
                  pl.BlockSpec((1, tk, HEAD_W), lambda b, h, i: (b, 0, h)),
                  pl.BlockSpec((1, tk, VDIM), lambda b, h, i: (b, 0, h))],
        out_specs=pl.BlockSpec((1, TQ, VDIM), lambda b, h, i: (b, i, h)),
        out_shape=jax.ShapeDtypeStruct((batch, lq, HEADS * VDIM), F32),
        compiler_params=pltpu.CompilerParams(dimension_semantics=("arbitrary",) * 3, vmem_limit_bytes=VMEM_LIMIT),
    )(q, k, v)


def _attn_bwd(q, k, v, do):
    batch, lq, _ = q.shape
    tk = k.shape[1]

    def body(q_ref, k_ref, v_ref, do_ref, dq_ref, dk_ref, dv_ref):
        i = pl.program_id(2)
        qv, kv, vv, dov = q_ref[0], k_ref[0], v_ref[0], do_ref[0]
        p = _softmax_rows(qv, kv)
        dp = _dot_nt(dov, vv)
        ds = p * (dp - jnp.sum(p * dp, axis=-1, keepdims=True)) * ATT_SCALE
        dq_ref[0] = _dot(ds, kv)
        dk = _dot_tn(ds, qv)
        dv = _dot_tn(p, dov)

        @pl.when(i == 0)
        def _():
            dk_ref[0] = dk
            dv_ref[0] = dv

        @pl.when(i != 0)
        def _():
            dk_ref[0] += dk
            dv_ref[0] += dv

    return pl.pallas_call(
        body, name="attn_bwd", grid=(batch, HEADS, lq // TQ),
        in_specs=[pl.BlockSpec((1, TQ, HEAD_W), lambda b, h, i: (b, i, h)),
                  pl.BlockSpec((1, tk, HEAD_W), lambda b, h, i: (b, 0, h)),
                  pl.BlockSpec((1, tk, VDIM), lambda b, h, i: (b, 0, h)),
                  pl.BlockSpec((1, TQ, VDIM), lambda b, h, i: (b, i, h))],
        out_specs=[pl.BlockSpec((1, TQ, HEAD_W), lambda b, h, i: (b, i, h)),
                   pl.BlockSpec((1, tk, HEAD_W), lambda b, h, i: (b, 0, h)),
                   pl.BlockSpec((1, tk, VDIM), lambda b, h, i: (b, 0, h))],
        out_shape=[jax.ShapeDtypeStruct(q.shape, F32), jax.ShapeDtypeStruct(k.shape, F32),
                   jax.ShapeDtypeStruct(v.shape, F32)],
        compiler_params=pltpu.CompilerParams(dimension_semantics=("arbitrary",) * 3, vmem_limit_bytes=VMEM_LIMIT),
    )(q, k, v, do)


def _loss_kernel(y, target):
    batch, lq, d = y.shape

    def body(y_ref, t_ref, l_ref, dy_ref):
        first = jnp.logical_and(pl.program_id(0) == 0, pl.program_id(1) == 0)
        err = y_ref[0] - t_ref[0]
        dy_ref[0] = err * (1.0 / d)
        part = jnp.zeros((1, LANES), F32) + jnp.sum(err * err) * (0.5 / d)

        @pl.when(first)
        def _():
            l_ref[...] = part

        @pl.when(jnp.logical_not(first))
        def _():
            l_ref[...] += part

    blk = pl.BlockSpec((1, TM, d), lambda b, i: (b, i, 0))
    return pl.pallas_call(
        body, name="loss_head", grid=(batch, lq // TM), in_specs=[blk, blk],
        out_specs=[pl.BlockSpec((1, LANES), lambda b, i: (0, 0)), blk],
        out_shape=[jax.ShapeDtypeStruct((1, LANES), F32), jax.ShapeDtypeStruct(y.shape, F32)],
        compiler_params=pltpu.CompilerParams(dimension_semantics=("arbitrary", "arbitrary")),
    )(y, target)


def _rope_tables(lat_len, ctx_len):
    rows = lat_len // GRID_W
    row_id = jnp.repeat(jnp.arange(rows), GRID_W).astype(F32)
    col_id = jnp.tile(jnp.arange(GRID_W), rows).astype(F32)
    axis_dim = ROPE // 2
    freqs = ROPE_THETA ** (-jnp.arange(0, axis_dim, 2, dtype=F32) / axis_dim)
    ar = row_id[:, None] * freqs
    ac = col_id[:, None] * freqs
    cr, sr, cc, sc = jnp.cos(ar), jnp.sin(ar), jnp.cos(ac), jnp.sin(ac)
    pad = jnp.zeros((lat_len, LANES - ROPE), F32)
    cos = jnp.concatenate([cr, cr, cc, cc, pad], axis=1)
    sin = jnp.concatenate([-sr, sr, -sc, sc, pad], axis=1)
    ident = jnp.concatenate([jnp.ones((ctx_len, ROPE), F32), jnp.zeros((ctx_len, LANES - ROPE), F32)], axis=1)
    cos = jnp.concatenate([cos, ident], axis=0)
    sin = jnp.concatenate([sin, jnp.zeros((ctx_len, LANES), F32)], axis=0)
    return cos[None], sin[None]


def _prep_weights(w):
    p = dict(w)
    d = w["ml_w_in"].shape[0]
    kvc = KV_RANK + ROPE
    wi = w["ml_w_in"]
    p["ml_w_in"] = jnp.concatenate(
        [wi[:, :kvc], jnp.zeros((d, LANES - ROPE), wi.dtype), wi[:, kvc:]], axis=1)
    uq = w["ml_w_uq"].reshape(Q_RANK, HEADS, NOPE + ROPE)
    p["ml_w_uq"] = jnp.pad(uq, ((0, 0), (0, 0), (0, HEAD_W - NOPE - ROPE))).reshape(Q_RANK, HEADS * HEAD_W)
    p["ml_rope_norm"] = jnp.pad(w["ml_rope_norm"], ((0, 0), (0, LANES - ROPE)))
    return p


def _unprep_grads(g):
    out = dict(g)
    kvc = KV_RANK + ROPE
    wi = g["ml_w_in"]
    out["ml_w_in"] = jnp.concatenate([wi[:, :kvc], wi[:, kvc + LANES - ROPE:]], axis=1)
    uq = g["ml_w_uq"].reshape(Q_RANK, HEADS, HEAD_W)
    out["ml_w_uq"] = uq[:, :, :NOPE + ROPE].reshape(Q_RANK, HEADS * (NOPE + ROPE))
    out["ml_rope_norm"] = g["ml_rope_norm"][:, :ROPE]
    return out


def _local_step(xm, target, mods, w, lat_len):
    batch, tot, d = xm.shape
    e = d
    n_all, n_lat = tot // TM, lat_len // TM
    cos, sin = _rope_tables(lat_len, tot - lat_len)
    g = {}

    def s1_splits(widths):
        out, s = [], 0
        for wd in widths:
            out.append((0, s, wd))
            s += wd
        return out

    def fwd_in(name, x, mod, gi, wname, widths, n_tiles):
        return _stage_fwd(name, pre=_pre_adaln, post=None, wsel=[0], splits=s1_splits(widths), tiles=[x],
                          eparams=[mod[0], mod[1]], sparams=[w["norm_g"][gi:gi + 1]], weights=[w[wname]],
                          out_widths=widths, out_dtypes=[F32] * len(widths), batch=batch, n_tiles=n_tiles,
                          n_lat_tiles=n_lat)

    def bwd_in(name, x, mod, gi, wname, widths, n_tiles, cots, lat_only, add, add_lat_only):
        (dx,), (dsh, dsc), (dg,), (dw,) = _stage_bwd(
            name, pre=_pre_adaln, post=None, wsel=[0], splits=s1_splits(widths), tiles=[x], tile_diff=[True],
            eparams=[mod[0], mod[1]], sparams=[w["norm_g"][gi:gi + 1]], weights=[w[wname]], cots=cots,
            cot_lat_only=lat_only, batch=batch, n_tiles=n_tiles, n_lat_tiles=n_lat, add=add,
            add_lat_only=add_lat_only)
        return dx, dsh, dsc, dg, dw

    def fwd_out(name, pre, tiles, mod, sparams, wname, n_tiles):
        return _stage_fwd(name, pre=pre, post=_post_residual(len(tiles) - 1), wsel=[0], splits=None, tiles=tiles,
                          eparams=[mod[2]], sparams=sparams, weights=[w[wname]], out_widths=[d], out_dtypes=[F32],
                          batch=batch, n_tiles=n_tiles, n_lat_tiles=n_lat)[0]

    def bwd_out(name, pre, tiles, mod, sparams, wname, n_tiles, cot):
        diff = [True] * (len(tiles) - 1) + [False]
        dts, (dgt,), dss, (dw,) = _stage_bwd(
            name, pre=pre, post=_post_residual(len(tiles) - 1), wsel=[0], splits=None, tiles=tiles, tile_diff=diff,
            eparams=[mod[2]], sparams=sparams, weights=[w[wname]], cots=[cot], cot_lat_only=[False], batch=batch,
            n_tiles=n_tiles, n_lat_tiles=n_lat)
        return dts, dgt, dss, dw

    cv_s = [w["cv_ln_g"], w["cv_ln_b"]]
    a0, b0, g0 = fwd_in("cv_in_fwd", xm, mods[0], 0, "cv_w_in", [e, e, e], n_all)
    c1 = _conv_fwd(a0, b0, w["cv_dw"], w["cv_db"], lat_len)
    x1 = fwd_out("cv_out_fwd", _pre_conv_out, [c1, g0, xm], mods[0], cv_s, "cv_w_out", n_all)

    pl_s = [w["pl_w_grp"], w["pl_scale"]]
    v1, g1 = fwd_in("pl_in_fwd", x1, mods[1], 1, "pl_w_in", [e, e], n_all)
    pooled = _pool_fwd(v1, lat_len)
    x2 = fwd_out("pl_out_fwd", _pre_pool_out, [pooled, g1, x1], mods[1], pl_s, "pl_w_out", n_all)

    ml_widths = [KV_RANK, LANES, Q_RANK, HEADS * VDIM]
    ckv, krp, cq, g2 = fwd_in("ml_in_fwd", x2, mods[2], 2, "ml_w_in", ml_widths, n_all)
    k_s = [w["ml_kv_norm"], w["ml_nope_norm"][1:2], w["ml_rope_norm"][1:2]]
    q_s = [w["ml_q_norm"], w["ml_nope_norm"][0:1], w["ml_rope_norm"][0:1]]
    kk, vv = _stage_fwd("ml_keys_fwd", pre=_pre_rms_only, post=_post_mla_keys, wsel=[0], splits=None,
                        tiles=[ckv, krp, cos, sin], eparams=[], sparams=k_s, weights=[w["ml_w_ukv"]],
                        out_widths=[HEADS * HEAD_W, HEADS * VDIM], out_dtypes=[BF16, BF16], batch=batch,
                        n_tiles=n_all, n_lat_tiles=n_lat)
    (qq,) = _stage_fwd("ml_queries_fwd", pre=_pre_rms_only, post=_post_mla_queries, wsel=[0], splits=None,
                       tiles=[cq, cos, sin], eparams=[], sparams=q_s, weights=[w["ml_w_uq"]],
                       out_widths=[HEADS * HEAD_W], out_dtypes=[BF16], batch=batch, n_tiles=n_lat,
                       n_lat_tiles=n_lat)
    att = _attn_fwd(qq, kk, vv)
    x3 = fwd_out("ml_out_fwd", _pre_mla_out, [att, g2, x2], mods[2], [], "ml_w_out", n_lat)

    ch_s = [w["ch_ln_g"], w["ch_ln_b"], w["ch_w_s"], w["ch_b_s"]]
    u3, v3, g3 = fwd_in("ch_in_fwd", x3, mods[3], 3, "ch_w_in", [e, e, e], n_lat)
    x4 = fwd_out("ch_out_fwd", _pre_chunk_out, [u3, v3, g3, x3], mods[3], ch_s, "ch_w_out", n_lat)

    loss_part, dy = _loss_kernel(x4, target)

    dmods = [None] * 4
    dnorm = [None] * 4
    (du, dv, dg), dgt, (g["ch_ln_g"], g["ch_ln_b"], g["ch_w_s"], g["ch_b_s"]), g["ch_w_out"] = bwd_out(
        "ch_out_bwd", _pre_chunk_out, [u3, v3, g3, x3], mods[3], ch_s, "ch_w_out", n_lat, dy)
    dx3, dsh, dsc, dnorm[3], g["ch_w_in"] = bwd_in("ch_in_bwd", x3, mods[3], 3, "ch_w_in", [e, e, e], n_lat,
                                                   [du, dv, dg], [False] * 3, dy, False)
    dmods[3] = (dsh, dsc, dgt)

    (datt, dg), dgt, _, g["ml_w_out"] = bwd_out("ml_out_bwd", _pre_mla_out, [att, g2, x2], mods[2], [], "ml_w_out",
                                                n_lat, dx3)
    dq, dk, dvv = _attn_bwd(qq, kk, vv, datt)
    (dcq,), _, (g["ml_q_norm"], dnope0, drope0), (g["ml_w_uq"],) = _stage_bwd(
        "ml_queries_bwd", pre=_pre_rms_only, post=_post_mla_queries, wsel=[0], splits=None, tiles=[cq, cos, sin],
        tile_diff=[True, False, False], eparams=[], sparams=q_s, weights=[w["ml_w_uq"]], cots=[dq],
        cot_lat_only=[False], batch=batch, n_tiles=n_lat, n_lat_tiles=n_lat)
    (dckv, dkrp), _, (g["ml_kv_norm"], dnope1, drope1), (g["ml_w_ukv"],) = _stage_bwd(
        "ml_keys_bwd", pre=_pre_rms_only, post=_post_mla_keys, wsel=[0], splits=None, tiles=[ckv, krp, cos, sin],
        tile_diff=[True, True, False, False], eparams=[], sparams=k_s, weights=[w["ml_w_ukv"]], cots=[dk, dvv],
        cot_lat_only=[False, False], batch=batch, n_tiles=n_all, n_lat_tiles=n_lat)
    g["ml_nope_norm"] = jnp.concatenate([dnope0, dnope1], axis=0)
    g["ml_rope_norm"] = jnp.concatenate([drope0, drope1], axis=0)
    dx2, dsh, dsc, dnorm[2], g["ml_w_in"] = bwd_in("ml_in_bwd", x2, mods[2], 2, "ml_w_in", ml_widths, n_all,
                                                   [dckv, dkrp, dcq, dg], [False, False, True, True], dx3, True)
    dmods[2] = (dsh, dsc, dgt)

    (dpooled, dg), dgt, (g["pl_w_grp"], g["pl_scale"]), g["pl_w_out"] = bwd_out(
        "pl_out_bwd", _pre_pool_out, [pooled, g1, x1], mods[1], pl_s, "pl_w_out", n_all, dx2)
    dv1 = _pool_bwd(dpooled, lat_len)
    dx1, dsh, dsc, dnorm[1], g["pl_w_in"] = bwd_in("pl_in_bwd", x1, mods[1], 1, "pl_w_in", [e, e], n_all,
                                                   [dv1, dg], [False] * 2, dx2, False)
    dmods[1] = (dsh, dsc, dgt)

    (dc1, dg), dgt, (g["cv_ln_g"], g["cv_ln_b"]), g["cv_w_out"] = bwd_out(
        "cv_out_bwd", _pre_conv_out, [c1, g0, xm], mods[0], cv_s, "cv_w_out", n_all, dx1)
    da, db, g["cv_dw"], g["cv_db"] = _conv_bwd(a0, b0, w["cv_dw"], dc1, lat_len)
    dx0, dsh, dsc, dnorm[0], g["cv_w_in"] = bwd_in("cv_in_bwd", xm, mods[0], 0, "cv_w_in", [e, e, e], n_all,
                                                   [da, db, dg], [False] * 3, dx1, False)
    dmods[0] = (dsh, dsc, dgt)
    g["norm_g"] = jnp.concatenate(dnorm, axis=0)
    return loss_part, dx0[:, :lat_len], dmods, g


N_DEV = 8
N_CHIP = 4
ANY = pl.BlockSpec(memory_space=pl.ANY)


def _my_place():
    return lax.axis_index("x"), lax.axis_index("y"), lax.axis_index("c")


def _flip(v, f):
    return 1 - v if f else v


def _ag8(name, x):
    r, cw = x.shape

    def body(x_ref, o_ref, send_sems, recv_sems, local_sem):
        mx, my, mc = _my_place()
        me = 4 * mx + 2 * my + mc
        mine = pltpu.make_async_copy(x_ref, o_ref.at[me], local_sem)
        mine.start()
        sends = []
        for rel in range(1, N_DEV):
            peer = (_flip(mx, rel & 4), _flip(my, rel & 2), _flip(mc, rel & 1))
            cp = pltpu.make_async_remote_copy(src_ref=x_ref, dst_ref=o_ref.at[me], send_sem=send_sems.at[rel - 1],
                                              recv_sem=recv_sems.at[rel - 1], device_id=peer, device_id_type=MESH)
            cp.start()
            sends.append(cp)
        for rel in range(1, N_DEV):
            peer = (_flip(mx, rel & 4), _flip(my, rel & 2), _flip(mc, rel & 1))
            src_dev = 4 * peer[0] + 2 * peer[1] + peer[2]
            pltpu.make_async_remote_copy(src_ref=x_ref, dst_ref=o_ref.at[src_dev], send_sem=send_sems.at[rel - 1],
                                         recv_sem=recv_sems.at[rel - 1], device_id=peer,
                                         device_id_type=MESH).wait_recv()
        for cp in sends:
            cp.wait_send()
        mine.wait()

    return pl.pallas_call(
        body, name=name, out_shape=jax.ShapeDtypeStruct((N_DEV, r, cw), x.dtype),
        in_specs=[pl.BlockSpec(memory_space=pltpu.VMEM)], out_specs=pl.BlockSpec(memory_space=pltpu.VMEM),
        scratch_shapes=[pltpu.SemaphoreType.DMA((N_DEV - 1,)), pltpu.SemaphoreType.DMA((N_DEV - 1,)),
                        pltpu.SemaphoreType.DMA],
        compiler_params=pltpu.CompilerParams(vmem_limit_bytes=VMEM_LIMIT),
    )(x)


def _chip_peers(mx, my, mc):
    out = []
    for rel in range(1, N_CHIP):
        px, py = _flip(mx, rel & 2), _flip(my, rel & 1)
        out.append((rel - 1, (px, py, mc), 2 * px + py))
    return out


def _half(mc, rows):
    return pl.ds(pl.multiple_of(mc * (rows // 2), 8), rows // 2)


def _gather_ici(shards):
    n = len(shards)

    def body(*refs):
        in_refs, out_refs = refs[:n], refs[n:2 * n]
        send_sems, recv_sems, local_sems = refs[2 * n:]
        mx, my, mc = _my_place()
        chip = 2 * mx + my
        locals_, sends = [], []
        for a in range(n):
            rows = in_refs[a].shape[0]
            cp = pltpu.make_async_copy(in_refs[a], out_refs[a].at[chip], local_sems.at[a])
            cp.start()
            locals_.append(cp)
            for k, peer, _ in _chip_peers(mx, my, mc):
                rc = pltpu.make_async_remote_copy(
                    src_ref=in_refs[a].at[_half(mc, rows)], dst_ref=out_refs[a].at[chip, _half(mc, rows)],
                    send_sem=send_sems.at[3 * a + k], recv_sem=recv_sems.at[3 * a + k], device_id=peer,
                    device_id_type=MESH)
                rc.start()
                sends.append(rc)
        for a in range(n):
            rows = in_refs[a].shape[0]
            for k, peer, pchip in _chip_peers(mx, my, mc):
                pltpu.make_async_remote_copy(
                    src_ref=in_refs[a].at[_half(mc, rows)], dst_ref=out_refs[a].at[pchip, _half(mc, rows)],
                    send_sem=send_sems.at[3 * a + k], recv_sem=recv_sems.at[3 * a + k], device_id=peer,
                    device_id_type=MESH).wait_recv()
        for cp in sends:
            cp.wait_send()
        for cp in locals_:
            cp.wait()

    return pl.pallas_call(
        body, name="gather_ici", in_specs=[ANY] * n, out_specs=[ANY] * n,
        out_shape=[jax.ShapeDtypeStruct((N_CHIP,) + s.shape, s.dtype) for s in shards],
        scratch_shapes=[pltpu.SemaphoreType.DMA((3 * n,)), pltpu.SemaphoreType.DMA((3 * n,)),
                        pltpu.SemaphoreType.DMA((n,))],
    )(*shards)


def _sibling_fill(name, arrays, row_axis, chips_only_other):
    n = len(arrays)
    per = 3 if chips_only_other else 1

    def body(*refs):
        out_refs = refs[n:2 * n]
        send_sems, recv_sems = refs[2 * n:]
        mx, my, mc = _my_place()
        sibling = (mx, my, 1 - mc)
        sends = []

        def views(a, core):
            rows = out_refs[a].shape[row_axis]
            if chips_only_other:
                return [out_refs[a].at[pchip, _half(core, rows)] for _, _, pchip in _chip_peers(mx, my, mc)]
            return [out_refs[a].at[_half(core, rows)]]

        for a in range(n):
            for k, v in enumerate(views(a, mc)):
                cp = pltpu.make_async_remote_copy(src_ref=v, dst_ref=v, send_sem=send_sems.at[per * a + k],
                                                  recv_sem=recv_sems.at[per * a + k], device_id=sibling,
                                                  device_id_type=MESH)
                cp.start()
                sends.append(cp)
        for a in range(n):
            for k, v in enumerate(views(a, 1 - mc)):
                pltpu.make_async_remote_copy(src_ref=v, dst_ref=v, send_sem=send_sems.at[per * a + k],
                                             recv_sem=recv_sems.at[per * a + k], device_id=sibling,
                                             device_id_type=MESH).wait_recv()
        for cp in sends:
            cp.wait_send()

    return pl.pallas_call(
        body, name=name, in_specs=[ANY] * n, out_specs=[ANY] * n,
        out_shape=[jax.ShapeDtypeStruct(s.shape, s.dtype) for s in arrays],
        input_output_aliases={a: a for a in range(n)},
        scratch_shapes=[pltpu.SemaphoreType.DMA((per * n,)), pltpu.SemaphoreType.DMA((per * n,))],
    )(*arrays)


def _grad_swap_d2d(stacks):
    n = len(stacks)

    def body(*refs):
        in_refs, out_refs = refs[:n], refs[n:2 * n]
        send_sems, recv_sems = refs[2 * n:]
        mx, my, mc = _my_place()
        sibling = (mx, my, 1 - mc)
        sends = []
        for a in range(n):
            rows = in_refs[a].shape[1]
            cp = pltpu.make_async_remote_copy(src_ref=in_refs[a].at[:, _half(1 - mc, rows)], dst_ref=out_refs[a],
                                              send_sem=send_sems.at[a], recv_sem=recv_sems.at[a],
                                              device_id=sibling, device_id_type=MESH)
            cp.start()
            sends.append(cp)
        for cp in sends:
            cp.wait()

    return pl.pallas_call(
        body, name="grad_swap_d2d", in_specs=[ANY] * n, out_specs=[ANY] * n,
        out_shape=[jax.ShapeDtypeStruct((N_CHIP, s.shape[1] // 2, s.shape[2]), s.dtype) for s in stacks],
        scratch_shapes=[pltpu.SemaphoreType.DMA((n,)), pltpu.SemaphoreType.DMA((n,))],
    )(*stacks)


def _grad_exchange_ici(parts):
    n = len(parts)

    def body(*refs):
        in_refs, out_refs = refs[:n], refs[n:2 * n]
        send_sems, recv_sems, local_sems = refs[2 * n:]
        mx, my, mc = _my_place()
        chip = 2 * mx + my
        locals_, sends = [], []
        for a in range(n):
            cp = pltpu.make_async_copy(in_refs[a].at[chip], out_refs[a].at[chip], local_sems.at[a])
            cp.start()
            locals_.append(cp)
            for k, peer, pchip in _chip_peers(mx, my, mc):
                rc = pltpu.make_async_remote_copy(src_ref=in_refs[a].at[pchip], dst_ref=out_refs[a].at[chip],
                                                  send_sem=send_sems.at[3 * a + k], recv_sem=recv_sems.at[3 * a + k],
                                                  device_id=peer, device_id_type=MESH)
                rc.start()
                sends.append(rc)
        for a in range(n):
            for k, peer, pchip in _chip_peers(mx, my, mc):
                pltpu.make_async_remote_copy(src_ref=in_refs[a].at[pchip], dst_ref=out_refs[a].at[pchip],
                                             send_sem=send_sems.at[3 * a + k], recv_sem=recv_sems.at[3 * a + k],
                                             device_id=peer, device_id_type=MESH).wait_recv()
        for cp in sends:
            cp.wait_send()
        for cp in locals_:
            cp.wait()

    return pl.pallas_call(
        body, name="grad_exchange_ici", in_specs=[ANY] * n, out_specs=[ANY] * n,
        out_shape=[jax.ShapeDtypeStruct(s.shape, s.dtype) for s in parts],
        scratch_shapes=[pltpu.SemaphoreType.DMA((3 * n,)), pltpu.SemaphoreType.DMA((3 * n,)),
                        pltpu.SemaphoreType.DMA((n,))],
    )(*parts)


def _row_block(rows, limit=256):
    for t in range(min(rows, limit), 7, -8):
        if rows % t == 0 and t % 8 == 0:
            return t
    return rows


def _grad_add_half(core, stack, received):
    _, rows, cw = stack.shape
    rh = rows // 2
    tr = _row_block(rh)

    def body(s_ref, a_ref, b_ref, o_ref):
        o_ref[...] = a_ref[...] + b_ref[...]

    grid_spec = pltpu.PrefetchScalarGridSpec(
        num_scalar_prefetch=1, grid=(rh // tr,),
        in_specs=[pl.BlockSpec((N_CHIP, tr, cw), lambda i, s: (0, s[0] * (rh // tr) + i, 0)),
                  pl.BlockSpec((N_CHIP, tr, cw), lambda i, s: (0, i, 0))],
        out_specs=pl.BlockSpec((N_CHIP, tr, cw), lambda i, s: (0, i, 0)))
    return pl.pallas_call(
        body, name="grad_add_half", grid_spec=grid_spec, out_shape=jax.ShapeDtypeStruct(received.shape, F32),
        compiler_params=pltpu.CompilerParams(dimension_semantics=("arbitrary",), vmem_limit_bytes=VMEM_LIMIT),
    )(core, stack, received)


def _adamw(name, row_off, parts, w, m, v, rows):
    n, _, cw = parts.shape
    tr = _row_block(rows, 128)
    c1 = 1.0 / (1.0 - ADAM_B1 ** ADAM_STEP)
    c2 = 1.0 / (1.0 - ADAM_B2 ** ADAM_STEP)

    def body(s_ref, p_ref, w_ref, m_ref, v_ref, g_ref, d_ref, nm_ref, nv_ref):
        g = p_ref[0]
        for k in range(1, n):
            g = g + p_ref[k]
        nm = ADAM_B1 * m_ref[...] + (1.0 - ADAM_B1) * g
        nv = ADAM_B2 * v_ref[...] + (1.0 - ADAM_B2) * (g * g)
        m_hat = nm / (1.0 - ADAM_B1 ** ADAM_STEP)
        v_hat = nv / (1.0 - ADAM_B2 ** ADAM_STEP)
        g_ref[...] = g
        d_ref[...] = -ADAM_LR * (m_hat / (jnp.sqrt(v_hat) + ADAM_EPS) + ADAM_WD * w_ref[...])
        nm_ref[...] = nm
        nv_ref[...] = nv

    del c1, c2
    full = pl.BlockSpec((tr, cw), lambda i, s: (s[0] // tr + i, 0))
    grid_spec = pltpu.PrefetchScalarGridSpec(
        num_scalar_prefetch=1, grid=(rows // tr,),
        in_specs=[pl.BlockSpec((n, tr, cw), lambda i, s: (0, i, 0)), full, full, full],
        out_specs=[full, full, full, full])
    return pl.pallas_call(
        body, name=name, grid_spec=grid_spec, out_shape=[jax.ShapeDtypeStruct(w.shape, F32)] * 4,
        compiler_params=pltpu.CompilerParams(dimension_semantics=("arbitrary",), vmem_limit_bytes=VMEM_LIMIT),
    )(row_off, parts, w, m, v)


def _sum8(x):
    _, r, cw = x.shape
    tr = _row_block(r, 64)

    def body(x_ref, o_ref):
        acc = x_ref[0]
        for k in range(1, N_DEV):
            acc = acc + x_ref[k]
        o_ref[...] = acc

    return pl.pallas_call(
        body, name="sum8", grid=(r // tr,), in_specs=[pl.BlockSpec((N_DEV, tr, cw), lambda i: (0, i, 0))],
        out_specs=pl.BlockSpec((tr, cw), lambda i: (i, 0)), out_shape=jax.ShapeDtypeStruct((r, cw), F32),
        compiler_params=pltpu.CompilerParams(dimension_semantics=("arbitrary",)),
    )(x)


MOD_ROWS = 24
CTX_ROW = 16


def _mod_fwd(c_rows, w_mod, b_mod):
    nl, d, nn = w_mod.shape

    def body(c_ref, w_ref, b_ref, o_ref):
        o_ref[0] = _dot(_silu(c_ref[...]), w_ref[0]) + b_ref[0]

    return pl.pallas_call(
        body, name="mod_fwd", grid=(nl,),
        in_specs=[pl.BlockSpec((MOD_ROWS, d), lambda i: (0, 0)), pl.BlockSpec((1, d, nn), lambda i: (i, 0, 0)),
                  pl.BlockSpec((1, 1, nn), lambda i: (i, 0, 0))],
        out_specs=pl.BlockSpec((1, MOD_ROWS, nn), lambda i: (i, 0, 0)),
        out_shape=jax.ShapeDtypeStruct((nl, MOD_ROWS, nn), F32),
        compiler_params=pltpu.CompilerParams(dimension_semantics=("arbitrary",), vmem_limit_bytes=VMEM_LIMIT),
    )(c_rows, w_mod, b_mod)


def _mod_bwd_rows(dlat, dctx_parts):
    nl, ne, nn = dlat.shape

    def body(l_ref, c_ref, db_ref, dc_ref):
        dc = c_ref[0, 0:1, :]
        for k in range(1, N_DEV):
            dc = dc + c_ref[0, k:k + 1, :]
        db = dc
        for k in range(ne):
            db = db + l_ref[0, k:k + 1, :]
        db_ref[0] = db
        dc_ref[0] = dc

    return pl.pallas_call(
        body, name="mod_bwd_rows", grid=(nl,),
        in_specs=[pl.BlockSpec((1, ne, nn), lambda i: (i, 0, 0)), pl.BlockSpec((1, N_DEV, nn), lambda i: (i, 0, 0))],
        out_specs=[pl.BlockSpec((1, 1, nn), lambda i: (i, 0, 0))] * 2,
        out_shape=[jax.ShapeDtypeStruct((nl, 1, nn), F32)] * 2,
        compiler_params=pltpu.CompilerParams(dimension_semantics=("arbitrary",)),
    )(dlat, dctx_parts)


def _mod_bwd_w(c_cols, d_rows, w_mod):
    nl, d, nn = w_mod.shape

    def body(c_ref, d_ref, w_ref, dw_ref, dc_ref):
        i = pl.program_id(0)
        c = c_ref[...]
        sg = jax.nn.sigmoid(c)
        s = c * sg
        dv = d_ref[0]
        acc = s[:, 0:1] * dv[0:1, :]
        for r in range(1, CTX_ROW + 1):
            acc = acc + s[:, r:r + 1] * dv[r:r + 1, :]
        dw_ref[0] = acc
        ds_ctx = jnp.sum(w_ref[0] * dv[CTX_ROW:CTX_ROW + 1, :], axis=1, keepdims=True)
        cc, sc = c[:, CTX_ROW:CTX_ROW + 1], sg[:, CTX_ROW:CTX_ROW + 1]
        part = ds_ctx * (sc * (1.0 + cc * (1.0 - sc)))

        @pl.when(i == 0)
        def _():
            dc_ref[...] = part

        @pl.when(i != 0)
        def _():
            dc_ref[...] += part

    return pl.pallas_call(
        body, name="mod_bwd_w", grid=(nl,),
        in_specs=[pl.BlockSpec((d, MOD_ROWS), lambda i: (0, 0)), pl.BlockSpec((1, MOD_ROWS, nn), lambda i: (i, 0, 0)),
                  pl.BlockSpec((1, d, nn), lambda i: (i, 0, 0))],
        out_specs=[pl.BlockSpec((1, d, nn), lambda i: (i, 0, 0)), pl.BlockSpec((d, 1), lambda i: (0, 0))],
        out_shape=[jax.ShapeDtypeStruct((nl, d, nn), F32), jax.ShapeDtypeStruct((d, 1), F32)],
        compiler_params=pltpu.CompilerParams(dimension_semantics=("arbitrary",), vmem_limit_bytes=VMEM_LIMIT),
    )(c_cols, d_rows, w_mod)


def _pack_rows(arrays, width):
    rows, spans, r0 = [], [], 0
    for a in arrays:
        flat = a.reshape(-1)
        nr = -(-flat.shape[0] // width)
        flat = jnp.pad(flat, (0, nr * width - flat.shape[0]))
        rows.append(flat.reshape(nr, width))
        spans.append((r0, nr, a.shape))
        r0 += nr
    pad = (-r0) % 8
    if pad:
        rows.append(jnp.zeros((pad, width), F32))
    return jnp.concatenate(rows, axis=0), spans


def _unpack_rows(packed, spans):
    out = []
    for r0, nr, shape in spans:
        out.append(packed[r0:r0 + nr].reshape(-1)[:math.prod(shape)].reshape(shape))
    return out


BIG = {"cv_w_in": 1, "cv_w_out": 0, "pl_w_in": 1, "pl_w_grp": None, "pl_w_out": 0, "ml_w_in": 1, "ml_w_uq": 1,
       "ml_w_ukv": 1, "ml_w_out": 0, "ch_w_in": 1, "ch_w_out": 0}
SMALL_SHARDED = ["cv_dw", "pl_scale", "ml_q_norm", "ml_kv_norm", "ch_ln_g", "ch_ln_b"]
SMALL_REPLICATED = ["c_ctx", "norm_g", "b_mod", "cv_db", "cv_ln_g", "cv_ln_b", "ml_nope_norm", "ml_rope_norm",
                    "ch_w_s", "ch_b_s"]
WEIGHTS = ['c_ctx', 'norm_g', 'w_mod', 'b_mod', 'cv_w_in', 'cv_dw', 'cv_db', 'cv_ln_g', 'cv_ln_b', 'cv_w_out',
           'pl_w_in', 'pl_w_grp', 'pl_scale', 'pl_w_out', 'ml_w_in', 'ml_q_norm', 'ml_kv_norm', 'ml_w_uq', 'ml_w_ukv',
           'ml_nope_norm', 'ml_rope_norm', 'ml_w_out', 'ch_w_in', 'ch_ln_g', 'ch_ln_b', 'ch_w_s', 'ch_b_s', 'ch_w_out']


def _shard2d(name, a):
    if name == "pl_w_grp":
        return a.reshape(a.shape[-3] * a.shape[-2], a.shape[-1])
    return a.reshape(a.shape[-2], a.shape[-1])


def _unstack(name, s):
    if name == "pl_w_grp":
        ng = len(POOL_WINDOWS)
        return s.reshape(N_CHIP, ng, s.shape[1] // ng, s.shape[2]).transpose(1, 0, 2, 3).reshape(ng, -1, s.shape[2])
    if BIG[name] == 0:
        return s.reshape(-1, s.shape[2])
    return s.transpose(1, 0, 2).reshape(s.shape[1], -1)


def _stack(name, g):
    if name == "pl_w_grp":
        ng = len(POOL_WINDOWS)
        return g.reshape(ng, N_CHIP, -1, g.shape[2]).transpose(1, 0, 2, 3).reshape(N_CHIP, -1, g.shape[2])
    if BIG[name] == 0:
        return g.reshape(N_CHIP, -1, g.shape[1])
    return g.reshape(g.shape[0], N_CHIP, -1).transpose(1, 0, 2)


def kernel(x, c, ctx, c_ctx, norm_g, w_mod, b_mod, cv_w_in, cv_dw, cv_db, cv_ln_g, cv_ln_b, cv_w_out, pl_w_in, pl_w_grp, pl_scale, pl_w_out, ml_w_in, ml_q_norm, ml_kv_norm, ml_w_uq, ml_w_ukv, ml_nope_norm, ml_rope_norm, ml_w_out, ch_w_in, ch_ln_g, ch_ln_b, ch_w_s, ch_b_s, ch_w_out, loss_target, m_c_ctx, m_norm_g, m_w_mod, m_b_mod, m_cv_w_in, m_cv_dw, m_cv_db, m_cv_ln_g, m_cv_ln_b, m_cv_w_out, m_pl_w_in, m_pl_w_grp, m_pl_scale, m_pl_w_out, m_ml_w_in, m_ml_q_norm, m_ml_kv_norm, m_ml_w_uq, m_ml_w_ukv, m_ml_nope_norm, m_ml_rope_norm, m_ml_w_out, m_ch_w_in, m_ch_ln_g, m_ch_ln_b, m_ch_w_s, m_ch_b_s, m_ch_w_out, v_c_ctx, v_norm_g, v_w_mod, v_b_mod, v_cv_w_in, v_cv_dw, v_cv_db, v_cv_ln_g, v_cv_ln_b, v_cv_w_out, v_pl_w_in, v_pl_w_grp, v_pl_scale, v_pl_w_out, v_ml_w_in, v_ml_q_norm, v_ml_kv_norm, v_ml_w_uq, v_ml_w_ukv, v_ml_nope_norm, v_ml_rope_norm, v_ml_w_out, v_ch_w_in, v_ch_ln_g, v_ch_ln_b, v_ch_w_s, v_ch_b_s, v_ch_w_out):
    W = dict(c_ctx=c_ctx, norm_g=norm_g, w_mod=w_mod, b_mod=b_mod, cv_w_in=cv_w_in, cv_dw=cv_dw, cv_db=cv_db, cv_ln_g=cv_ln_g, cv_ln_b=cv_ln_b, cv_w_out=cv_w_out, pl_w_in=pl_w_in, pl_w_grp=pl_w_grp, pl_scale=pl_scale, pl_w_out=pl_w_out, ml_w_in=ml_w_in, ml_q_norm=ml_q_norm, ml_kv_norm=ml_kv_norm, ml_w_uq=ml_w_uq, ml_w_ukv=ml_w_ukv, ml_nope_norm=ml_nope_norm, ml_rope_norm=ml_rope_norm, ml_w_out=ml_w_out, ch_w_in=ch_w_in, ch_ln_g=ch_ln_g, ch_ln_b=ch_ln_b, ch_w_s=ch_w_s, ch_b_s=ch_b_s, ch_w_out=ch_w_out)
    M = dict(c_ctx=m_c_ctx, norm_g=m_norm_g, w_mod=m_w_mod, b_mod=m_b_mod, cv_w_in=m_cv_w_in, cv_dw=m_cv_dw, cv_db=m_cv_db, cv_ln_g=m_cv_ln_g, cv_ln_b=m_cv_ln_b, cv_w_out=m_cv_w_out, pl_w_in=m_pl_w_in, pl_w_grp=m_pl_w_grp, pl_scale=m_pl_scale, pl_w_out=m_pl_w_out, ml_w_in=m_ml_w_in, ml_q_norm=m_ml_q_norm, ml_kv_norm=m_ml_kv_norm, ml_w_uq=m_ml_w_uq, ml_w_ukv=m_ml_w_ukv, ml_nope_norm=m_ml_nope_norm, ml_rope_norm=m_ml_rope_norm, ml_w_out=m_ml_w_out, ch_w_in=m_ch_w_in, ch_ln_g=m_ch_ln_g, ch_ln_b=m_ch_ln_b, ch_w_s=m_ch_w_s, ch_b_s=m_ch_b_s, ch_w_out=m_ch_w_out)
    V = dict(c_ctx=v_c_ctx, norm_g=v_norm_g, w_mod=v_w_mod, b_mod=v_b_mod, cv_w_in=v_cv_w_in, cv_dw=v_cv_dw, cv_db=v_cv_db, cv_ln_g=v_cv_ln_g, cv_ln_b=v_cv_ln_b, cv_w_out=v_cv_w_out, pl_w_in=v_pl_w_in, pl_w_grp=v_pl_w_grp, pl_scale=v_pl_scale, pl_w_out=v_pl_w_out, ml_w_in=v_ml_w_in, ml_q_norm=v_ml_q_norm, ml_kv_norm=v_ml_kv_norm, ml_w_uq=v_ml_w_uq, ml_w_ukv=v_ml_w_ukv, ml_nope_norm=v_ml_nope_norm, ml_rope_norm=v_ml_rope_norm, ml_w_out=v_ml_w_out, ch_w_in=v_ch_w_in, ch_ln_g=v_ch_ln_g, ch_ln_b=v_ch_ln_b, ch_w_s=v_ch_w_s, ch_b_s=v_ch_b_s, ch_w_out=v_ch_w_out)

    batch, lat_len, d = x.shape
    mx, my, mc = _my_place()
    chip = 2 * mx + my
    dev = 2 * chip + mc
    core = jnp.reshape(mc, (1,)).astype(jnp.int32)
    zero_off = jnp.zeros((1,), jnp.int32)
    big_names = list(BIG)

    sw = d // N_CHIP
    small_in = [c] + [jnp.pad(W[n].reshape(-1, W[n].shape[-1]), ((0, 0), (0, sw - W[n].shape[-1])))
                      for n in SMALL_SHARDED]
    pack1, spans1 = _pack_rows(small_in, sw)
    got1 = _ag8("ag8_inputs", pack1)
    c_all = got1[:, spans1[0][0]:spans1[0][0] + spans1[0][1]].reshape(N_DEV * batch, d)
    full_small = {}
    for n, (r0, nr, _) in zip(SMALL_SHARDED, spans1[1:]):
        blk = got1[0::2, r0:r0 + nr, :W[n].shape[-1]]
        full_small[n] = blk.transpose(1, 0, 2).reshape(nr, -1)

    c_rows = jnp.concatenate([c_all, c_ctx[None], jnp.zeros((MOD_ROWS - CTX_ROW - 1, d), F32)], axis=0)
    nmod = w_mod.shape[2]
    b_shard = lax.dynamic_slice(b_mod, (0, chip * nmod), (b_mod.shape[0], nmod))[:, None, :]
    mod_shard = _mod_fwd(c_rows, w_mod, b_shard)
    got2 = _ag8("ag8_mod", mod_shard.reshape(-1, nmod))
    mod_full = got2[0::2].reshape(N_CHIP, 4, MOD_ROWS, nmod).transpose(1, 2, 0, 3).reshape(4, MOD_ROWS, 3 * d)
    mod_lat = lax.dynamic_slice(mod_full, (0, dev * batch, 0), (4, batch, 3 * d))
    mod_ctx = mod_full[:, CTX_ROW]
    mods = []
    for i in range(4):
        mods.append(tuple(
            jnp.stack([mod_lat[i, :, j * d:(j + 1) * d], jnp.broadcast_to(mod_ctx[i, j * d:(j + 1) * d], (batch, d))],
                      axis=1)[:, :, None, :] for j in range(3)))

    stacks = _gather_ici([_shard2d(n, W[n]).astype(BF16) for n in big_names])
    stacks = _sibling_fill("gather_d2d", stacks, 1, True)
    wk = {n: _unstack(n, s) for n, s in zip(big_names, stacks)}
    wk["pl_w_grp"] = wk["pl_w_grp"].astype(F32)
    for n in SMALL_SHARDED:
        wk[n] = full_small[n]
    wk.update(norm_g=norm_g, cv_db=cv_db, cv_ln_g=cv_ln_g, cv_ln_b=cv_ln_b, ml_nope_norm=ml_nope_norm[0],
              ml_rope_norm=ml_rope_norm[0], ch_w_s=ch_w_s[0], ch_b_s=ch_b_s[0])

    xm = jnp.concatenate([x, ctx], axis=1)
    loss_part, grad_x, dmods, g = _local_step(xm, loss_target, mods, _prep_weights(wk), lat_len)
    g = _unprep_grads(g)
    loss = lax.psum(loss_part[0, 0], ("x", "y", "c"))

    lat_rows, ctx_rows = [], []
    for i in range(4):
        dsh, dsc, dgt = dmods[i]
        lat_rows.append(jnp.concatenate([dsh[:, 0, 0], dsc[:, 0, 0], dgt[:, 0, 0]], axis=1))
        zero = jnp.zeros((d,), F32)
        cs = [jnp.sum(t[:, 1, 0], axis=0) if ok else zero
              for t, ok in zip((dsh, dsc, dgt), (i <= 2, i <= 2, i <= 1))]
        ctx_rows.append(jnp.concatenate(cs, axis=0)[None])
    dmod_dev = jnp.concatenate(lat_rows + ctx_rows, axis=0)
    dmod_dev = jnp.pad(dmod_dev, ((0, (-dmod_dev.shape[0]) % 8), (0, 0)))
    got3 = _ag8("ag8_dmod", dmod_dev)
    dlat = got3[:, :4 * batch].reshape(N_DEV, 4, batch, 3 * d).transpose(1, 0, 2, 3).reshape(4, N_DEV * batch, 3 * d)
    dctx_parts = got3[:, 4 * batch:4 * batch + 4].transpose(1, 0, 2)
    g_b_mod, dctx = _mod_bwd_rows(dlat, dctx_parts)
    d_rows = jnp.concatenate([dlat, dctx, jnp.zeros((4, MOD_ROWS - CTX_ROW - 1, 3 * d), F32)], axis=1)
    d_rows = lax.dynamic_slice_in_dim(d_rows, chip * nmod, nmod, axis=2)
    g_w_mod, dcc_part = _mod_bwd_w(c_rows.T, d_rows, w_mod)

    g_small_in = {n: g[n] for n in SMALL_SHARDED}
    g_small_in.update(norm_g=g["norm_g"], cv_db=g["cv_db"], cv_ln_g=g["cv_ln_g"], cv_ln_b=g["cv_ln_b"],
                      ml_nope_norm=g["ml_nope_norm"], ml_rope_norm=g["ml_rope_norm"], ch_w_s=g["ch_w_s"],
                      ch_b_s=g["ch_b_s"],
                      c_ctx=dcc_part.reshape(-1) * (mc == 0).astype(F32))
    small_names = list(g_small_in)
    pack4, spans4 = _pack_rows([g_small_in[n] for n in small_names], d)
    summed = _sum8(_ag8("ag8_small_grads", pack4))
    gs = dict(zip(small_names, _unpack_rows(summed, spans4)))
    gs["b_mod"] = g_b_mod[:, 0]
    for n in SMALL_SHARDED:
        wd = W[n].shape[-1]
        gs[n] = lax.dynamic_slice_in_dim(gs[n], chip * wd, wd, axis=1)
    upd_names = SMALL_REPLICATED + SMALL_SHARDED
    pw, spans_u = _pack_rows([W[n] for n in upd_names], d)
    pm, _ = _pack_rows([M[n] for n in upd_names], d)
    pv, _ = _pack_rows([V[n] for n in upd_names], d)
    pg, _ = _pack_rows([gs[n].reshape(W[n].shape) for n in upd_names], d)
    res_small = _adamw("adamw_small", zero_off, pg[None], pw, pm, pv, pw.shape[0])
    out = {}
    for n, vals in zip(upd_names, zip(*[_unpack_rows(r, spans_u) for r in res_small])):
        out[n] = vals

    wm2 = w_mod.reshape(-1, nmod)
    res_mod = _adamw("adamw_w_mod", zero_off, g_w_mod.reshape(1, -1, nmod), wm2, M["w_mod"].reshape(-1, nmod),
                     V["w_mod"].reshape(-1, nmod), wm2.shape[0])
    out["w_mod"] = tuple(r.reshape(w_mod.shape) for r in res_mod)

    gstacks = [_stack(n, g[n]) for n in big_names]
    received = _grad_swap_d2d(gstacks)
    parts = [_grad_add_half(core, s, r) for s, r in zip(gstacks, received)]
    quarters = _grad_exchange_ici(parts)
    halves = []
    for n, q in zip(big_names, quarters):
        rh = q.shape[1]
        res = _adamw("adamw_" + n, core * rh, q, _shard2d(n, W[n]), _shard2d(n, M[n]), _shard2d(n, V[n]), rh)
        halves.extend(res)
    filled = _sibling_fill("update_d2d", halves, 0, False)
    for k, n in enumerate(big_names):
        out[n] = tuple(r.reshape(W[n].shape) for r in filled[4 * k:4 * k + 4])

    outs = [loss, grad_x]
    for j in range(4):
        outs.extend(out[n][j] for n in WEIGHTS)
    return tuple(outs)
```

```python
import functools
import math

import jax
import jax.numpy as jnp
from jax import lax
from jax.experimental import pallas as pl
from jax.experimental.pallas import tpu as pltpu

F32 = jnp.float32
BF16 = jnp.bfloat16
MESH = pl.DeviceIdType.MESH

EPS = 1e-6
GRID_W = 64
CONV_WIDTH = 31
CONV_HALF = CONV_WIDTH // 2
CONV_PAD = 16
POOL_WINDOWS = (2, 4, 8, 16)
POOL_HALF = max(POOL_WINDOWS) // 2
HEADS = 8
NOPE = 128
ROPE = 64
HEAD_W = 256
VDIM = 128
KV_RANK = 256
Q_RANK = 384
ATT_SCALE = (NOPE + ROPE) ** -0.5
ROPE_THETA = 10000.0
CHUNK = 128
CHUNK_GROUPS = 8
LANES = 128
TM = 256
TQ = 256
VMEM_LIMIT = 56 * 1024 * 1024

ADAM_LR = 0.001
ADAM_B1 = 0.9
ADAM_B2 = 0.999
ADAM_EPS = 1e-08
ADAM_WD = 0.01
ADAM_STEP = 10


def _dot(a, b):
    return jnp.dot(a.astype(BF16), b.astype(BF16), preferred_element_type=F32)


def _dot_nt(a, b):
    return lax.dot_general(a.astype(BF16), b.astype(BF16), (((1,), (1,)), ((), ())), preferred_element_type=F32)


def _dot_tn(a, b):
    return lax.dot_general(a.astype(BF16), b.astype(BF16), (((0,), (0,)), ((), ())), preferred_element_type=F32)


@jax.custom_vjp
def _mm(a, w):
    return _dot(a, w)


def _mm_fwd(a, w):
    return _dot(a, w), (a, w)


def _mm_bwd(res, ct):
    a, w = res
    return _dot_nt(ct, w), _dot_tn(a, ct)


_mm.defvjp(_mm_fwd, _mm_bwd)


def _swap16_impl(x):
    n = x.shape[-1]
    ax = x.ndim - 1
    lane = lax.broadcasted_iota(jnp.int32, x.shape, ax)
    up = pltpu.roll(x, n - 16, ax)
    dn = pltpu.roll(x, 16, ax)
    return jnp.where((lane % 32) < 16, up, dn)


@jax.custom_vjp
def _swap16(x):
    return _swap16_impl(x)


_swap16.defvjp(lambda x: (_swap16_impl(x), None), lambda _, ct: (_swap16_impl(ct),))


def _rms(x, g, n=None):
    n = x.shape[-1] if n is None else n
    return x * lax.rsqrt(jnp.sum(x * x, axis=-1, keepdims=True) * (1.0 / n) + EPS) * g


def _layernorm(x, g, b):
    mu = jnp.mean(x, axis=-1, keepdims=True)
    xc = x - mu
    var = jnp.mean(xc * xc, axis=-1, keepdims=True)
    return xc * lax.rsqrt(var + EPS) * g + b


def _silu(x):
    return x * jax.nn.sigmoid(x)


def _rope(x, cos, sin):
    return x * cos + _swap16(x) * sin


ANY = pl.BlockSpec(memory_space=pl.ANY)


class _Hosted:
    def __init__(self, arrays, out_shapes, sems, start, wait, aliases=None):
        self.arrays, self.out_shapes, self.sems = list(arrays), list(out_shapes), list(sems)
        self.start, self.wait, self.aliases = start, wait, dict(aliases or {})
        self.results = None


def _merge_hosted(parts):
    parts = [p for p in parts if p is not None]
    if not parts:
        return None
    if len(parts) == 1:
        return parts[0]
    offs, a0, o0, s0 = [], 0, 0, 0
    for p in parts:
        offs.append((a0, o0, s0))
        a0, o0, s0 = a0 + len(p.arrays), o0 + len(p.out_shapes), s0 + len(p.sems)

    def run(which):
        def f(ins, outs, sems):
            for p, (a, o, s) in zip(parts, offs):
                getattr(p, which)(ins[a:a + len(p.arrays)], outs[o:o + len(p.out_shapes)], sems[s:s + len(p.sems)])
        return f

    aliases = {}
    for p, (a, o, _) in zip(parts, offs):
        aliases.update({a + i: o + j for i, j in p.aliases.items()})
    merged = _Hosted(sum((p.arrays for p in parts), []), sum((p.out_shapes for p in parts), []),
                     sum((p.sems for p in parts), []), run("start"), run("wait"), aliases)
    merged.parts, merged.offs = parts, offs
    return merged


def _deliver(hosted, results):
    hosted.results = list(results)
    for p, (_, o, _) in zip(getattr(hosted, "parts", []), getattr(hosted, "offs", [])):
        p.results = list(results[o:o + len(p.out_shapes)])


def _pcall(body, *, name, grid, in_specs, out_specs, out_shape, args, hosted=None, vmem_limit=True):
    n_in, n_out = len(args), len(out_shape)
    kwargs = {}
    if hosted is not None:
        nhi, nho, inner = len(hosted.arrays), len(hosted.out_shapes), body

        def body(*refs):
            ins, hin = refs[:n_in], refs[n_in:n_in + nhi]
            outs, hout = refs[n_in + nhi:n_in + nhi + n_out], refs[n_in + nhi + n_out:n_in + nhi + n_out + nho]
            sems = refs[n_in + nhi + n_out + nho:]
            first, last = None, None
            for k, g in enumerate(grid):
                f, l = pl.program_id(k) == 0, pl.program_id(k) == g - 1
                first = f if first is None else jnp.logical_and(first, f)
                last = l if last is None else jnp.logical_and(last, l)

            @pl.when(first)
            def _():
                hosted.start(hin, hout, sems)

            inner(*ins, *outs)

            @pl.when(last)
            def _():
                hosted.wait(hin, hout, sems)

        in_specs = list(in_specs) + [ANY] * nhi
        out_specs = list(out_specs) + [ANY] * nho
        out_shape = list(out_shape) + hosted.out_shapes
        args = list(args) + hosted.arrays
        kwargs = dict(scratch_shapes=hosted.sems,
                      input_output_aliases={n_in + i: n_out + j for i, j in hosted.aliases.items()})
    params = dict(dimension_semantics=("arbitrary",) * len(grid))
    if vmem_limit:
        params["vmem_limit_bytes"] = VMEM_LIMIT
    res = pl.pallas_call(body, name=name, grid=grid, in_specs=list(in_specs), out_specs=list(out_specs),
                         out_shape=list(out_shape), compiler_params=pltpu.CompilerParams(**params), **kwargs)(*args)
    if hosted is not None:
        _deliver(hosted, res[n_out:])
    return list(res[:n_out])


def _run_hosted(name, hosted):
    nhi, nho = len(hosted.arrays), len(hosted.out_shapes)

    def body(*refs):
        ins, outs, sems = refs[:nhi], refs[nhi:nhi + nho], refs[nhi + nho:]
        hosted.start(ins, outs, sems)
        hosted.wait(ins, outs, sems)

    res = pl.pallas_call(body, name=name, in_specs=[ANY] * nhi, out_specs=[ANY] * nho, out_shape=hosted.out_shapes,
                         scratch_shapes=hosted.sems, input_output_aliases=hosted.aliases)(*hosted.arrays)
    _deliver(hosted, res)
    return list(res)


def _const_spec(shape, single=False):
    nd = len(shape)
    if single:
        return pl.BlockSpec(shape, lambda b, i: (0,) * nd, pipeline_mode=pl.Buffered(1))
    return pl.BlockSpec(shape, lambda b, i: (0,) * nd)


def _tile_spec(arr, n_lat_tiles, lat_only=False):
    bt, _, cw = arr.shape
    if lat_only:
        return pl.BlockSpec((1, TM, cw), lambda b, i: (b if bt > 1 else 0, jnp.minimum(i, n_lat_tiles - 1), 0))
    return pl.BlockSpec((1, TM, cw), lambda b, i: (b if bt > 1 else 0, i, 0))


def _eparam_spec(arr, n_lat_tiles):
    cw = arr.shape[-1]
    return pl.BlockSpec((1, 1, 1, cw), lambda b, i: (b, (i >= n_lat_tiles).astype(jnp.int32), 0, 0))


def _stage_fwd(name, *, pre, post, wsel, splits, tiles, eparams, sparams, weights, out_widths, out_dtypes,
               batch, n_tiles, n_lat_tiles, hosted=None):
    nt, ne, ns, nw = len(tiles), len(eparams), len(sparams), len(weights)

    def body(*refs):
        t_refs = refs[:nt]
        e_refs = refs[nt:nt + ne]
        s_refs = refs[nt + ne:nt + ne + ns]
        w_refs = refs[nt + ne + ns:nt + ne + ns + nw]
        o_refs = refs[nt + ne + ns + nw:]
        tv = [r[0].astype(F32) for r in t_refs]
        ev = [r[0, 0] for r in e_refs]
        sv = [r[...] for r in s_refs]
        a = pre(tv, ev, sv)
        z = [_dot(a[wsel[j]], w_refs[j][...]) for j in range(nw)]
        if post is None:
            outs = [z[j][:, s:s + w] for (j, s, w) in splits]
        else:
            outs = post(z, tv, ev, sv)
        for o_ref, o in zip(o_refs, outs):
            o_ref[0] = o.astype(o_ref.dtype)

    in_specs = ([_tile_spec(t, n_lat_tiles) for t in tiles] + [_eparam_spec(e, n_lat_tiles) for e in eparams]
                + [_const_spec(s.shape) for s in sparams] + [_const_spec(w.shape, single=True) for w in weights])
    out_shape = [jax.ShapeDtypeStruct((batch, n_tiles * TM, w), dt) for w, dt in zip(out_widths, out_dtypes)]
    out_specs = [pl.BlockSpec((1, TM, w), lambda b, i: (b, i, 0)) for w in out_widths]
    return _pcall(body, name=name, grid=(batch, n_tiles), in_specs=in_specs, out_specs=out_specs,
                  out_shape=out_shape, args=[*tiles, *eparams, *sparams, *weights], hosted=hosted)


def _stage_bwd(name, *, pre, post, wsel, splits, tiles, tile_diff, eparams, sparams, weights, cots, cot_lat_only,
               batch, n_tiles, n_lat_tiles, add=None, add_lat_only=False, hosted=None):
    nt, ne, ns, nw, nc = len(tiles), len(eparams), len(sparams), len(weights), len(cots)
    diff_idx = [k for k in range(nt) if tile_diff[k]]
    nd = len(diff_idx)
    has_add = add is not None

    def body(*refs):
        pos = 0
        t_refs = refs[pos:pos + nt]; pos += nt
        e_refs = refs[pos:pos + ne]; pos += ne
        s_refs = refs[pos:pos + ns]; pos += ns
        w_refs = refs[pos:pos + nw]; pos += nw
        c_refs = refs[pos:pos + nc]; pos += nc
        if has_add:
            add_ref = refs[pos]; pos += 1
        dt_refs = refs[pos:pos + nd]; pos += nd
        de_refs = refs[pos:pos + ne]; pos += ne
        ds_refs = refs[pos:pos + ns]; pos += ns
        dw_refs = refs[pos:pos + nw]; pos += nw

        b = pl.program_id(0)
        i = pl.program_id(1)
        is_lat = i < n_lat_tiles
        tv = [r[0].astype(F32) for r in t_refs]
        ev = tuple(r[0, 0] for r in e_refs)
        sv = tuple(r[...] for r in s_refs)
        dv0 = tuple(tv[k] for k in diff_idx)

        def merge(dv):
            full = list(tv)
            for k, v in zip(diff_idx, dv):
                full[k] = v
            return full

        def pre_f(dv, ev_, sv_):
            return tuple(pre(merge(dv), list(ev_), list(sv_)))

        a, vjp_pre = jax.vjp(pre_f, dv0, ev, sv)
        cv = []
        for c_ref, lat in zip(c_refs, cot_lat_only):
            c = c_ref[0].astype(F32)
            cv.append(jnp.where(is_lat, c, 0.0) if lat else c)
        if post is None:
            dz = []
            for j in range(nw):
                parts = [cv[k] for k, (jj, _, _) in enumerate(splits) if jj == j]
                dz.append(parts[0] if len(parts) == 1 else jnp.concatenate(parts, axis=1))
            dt2 = de2 = ds2 = None
        else:
            z = tuple(_dot(a[wsel[j]], w_refs[j][...]) for j in range(nw))

            def post_f(z_, dv, ev_, sv_):
                return tuple(post(list(z_), merge(dv), list(ev_), list(sv_)))

            _, vjp_post = jax.vjp(post_f, z, dv0, ev, sv)
            dz, dt2, de2, ds2 = vjp_post(tuple(cv))
        da = [None] * len(a)
        dws = []
        for j in range(nw):
            g = _dot_nt(dz[j], w_refs[j][...])
            da[wsel[j]] = g if da[wsel[j]] is None else da[wsel[j]] + g
            dws.append(_dot_tn(a[wsel[j]], dz[j]))
        da = tuple(jnp.zeros_like(a[k]) if da[k] is None else da[k] for k in range(len(a)))
        dt1, de1, ds1 = vjp_pre(da)

        def plus(u, v):
            return u if v is None else u + v

        for k in range(nd):
            val = plus(dt1[k], None if dt2 is None else dt2[k])
            if has_add and k == 0:
                addv = add_ref[0].astype(F32)
                val = val + (jnp.where(is_lat, addv, 0.0) if add_lat_only else addv)
            dt_refs[k][0] = val.astype(dt_refs[k].dtype)

        seg_first = jnp.logical_or(i == 0, i == n_lat_tiles)
        for k in range(ne):
            val = plus(de1[k], None if de2 is None else de2[k])

            @pl.when(seg_first)
            def _(k=k, val=val):
                de_refs[k][0, 0] = val

            @pl.when(jnp.logical_not(seg_first))
            def _(k=k, val=val):
                de_refs[k][0, 0] += val

        first = jnp.logical_and(b == 0, i == 0)
        acc = [(ds_refs[k], plus(ds1[k], None if ds2 is None else ds2[k])) for k in range(ns)]
        acc += [(dw_refs[j], dws[j]) for j in range(nw)]
        for ref, val in acc:
            @pl.when(first)
            def _(ref=ref, val=val):
                ref[...] = val

            @pl.when(jnp.logical_not(first))
            def _(ref=ref, val=val):
                ref[...] += val

    in_specs = ([_tile_spec(t, n_lat_tiles) for t in tiles] + [_eparam_spec(e, n_lat_tiles) for e in eparams]
                + [_const_spec(s.shape) for s in sparams] + [_const_spec(w.shape, single=True) for w in weights]
                + [_tile_spec(c, n_lat_tiles, lat) for c, lat in zip(cots, cot_lat_only)])
    args = [*tiles, *eparams, *sparams, *weights, *cots]
    if has_add:
        in_specs.append(_tile_spec(add, n_lat_tiles, add_lat_only))
        args.append(add)
    out_shape = [jax.ShapeDtypeStruct((batch, n_tiles * TM, tiles[k].shape[-1]), F32) for k in diff_idx]
    out_specs = [pl.BlockSpec((1, TM, tiles[k].shape[-1]), lambda b, i: (b, i, 0)) for k in diff_idx]
    out_shape += [jax.ShapeDtypeStruct(e.shape, F32) for e in eparams]
    out_specs += [_eparam_spec(e, n_lat_tiles) for e in eparams]
    out_shape += [jax.ShapeDtypeStruct(s.shape, F32) for s in sparams]
    out_specs += [_const_spec(s.shape) for s in sparams]
    out_shape += [jax.ShapeDtypeStruct(w.shape, F32) for w in weights]
    out_specs += [_const_spec(w.shape, single=True) for w in weights]
    res = _pcall(body, name=name, grid=(batch, n_tiles), in_specs=in_specs, out_specs=out_specs,
                 out_shape=out_shape, args=args, hosted=hosted)
    return res[:nd], res[nd:nd + ne], res[nd + ne:nd + ne + ns], res[nd + ne + ns:]


def _pre_adaln(tv, ev, sv):
    x = tv[0]
    sh, sc = ev[0], ev[1]
    return [_rms(x, sv[0]) * (1.0 + sc) + sh]


def _post_residual(x_index):
    def post(z, tv, ev, sv):
        return [tv[x_index] + ev[-1] * z[0]]
    return post


def _pre_conv_out(tv, ev, sv):
    c1, gg = tv[0], tv[1]
    return [_silu(_layernorm(c1, sv[0], sv[1])) * _silu(gg)]


def _pre_pool_out(tv, ev, sv):
    pooled, gg = tv[0], tv[1]
    w_grp, scale = sv[0], sv[1]
    gw = w_grp.shape[-1]
    y = jnp.concatenate([_mm(pooled[:, k * gw:(k + 1) * gw], w_grp[k]) for k in range(w_grp.shape[0])], axis=1)
    return [y * scale * _silu(gg)]


def _pre_rms_only(tv, ev, sv):
    return [_rms(tv[0], sv[0])]


def _post_mla_keys(z, tv, ev, sv):
    krp, cos, sin = tv[1], tv[2], tv[3]
    nope_g, rope_g = sv[1], sv[2]
    kv = z[0]
    kr = _rope(_rms(krp, rope_g, ROPE), cos, sin)
    ks, vs = [], []
    for h in range(HEADS):
        ks.append(_rms(kv[:, h * 2 * NOPE:h * 2 * NOPE + NOPE], nope_g))
        ks.append(kr)
        vs.append(kv[:, h * 2 * NOPE + NOPE:(h + 1) * 2 * NOPE])
    return [jnp.concatenate(ks, axis=1), jnp.concatenate(vs, axis=1)]


def _post_mla_queries(z, tv, ev, sv):
    cos, sin = tv[1], tv[2]
    nope_g, rope_g = sv[1], sv[2]
    q = z[0]
    qs = []
    for h in range(HEADS):
        qs.append(_rms(q[:, h * HEAD_W:h * HEAD_W + NOPE], nope_g))
        qs.append(_rope(_rms(q[:, h * HEAD_W + NOPE:(h + 1) * HEAD_W], rope_g, ROPE), cos, sin))
    return [jnp.concatenate(qs, axis=1)]


def _pre_mla_out(tv, ev, sv):
    return [tv[0] * _silu(tv[1])]


def _pre_chunk_out(tv, ev, sv):
    u, v, gg = tv[0], tv[1], tv[2]
    ln_g, ln_b, w_s, b_s = sv
    vn = _layernorm(v, ln_g, ln_b)
    rows = []
    for n in range(vn.shape[0] // CHUNK):
        blk = vn[n * CHUNK:(n + 1) * CHUNK]
        cols = [_mm(w_s[g], blk[:, g * LANES:(g + 1) * LANES]) + b_s[:, g:g + 1] for g in range(CHUNK_GROUPS)]
        rows.append(jnp.concatenate(cols, axis=1))
    s = jnp.concatenate(rows, axis=0)
    return [u * s * _silu(gg)]


def _segments(lat_len, tot_len):
    segs = [(0, lat_len)]
    if tot_len > lat_len:
        segs.append((lat_len, tot_len - lat_len))
    return segs


def _shifted(x, j):
    if j == 0:
        return x
    n = x.shape[0]
    rows = lax.broadcasted_iota(jnp.int32, x.shape, 0)
    r = pltpu.roll(x, (-j) % n, 0)
    return jnp.where(jnp.logical_and(rows + j >= 0, rows + j < n), r, 0.0)


def _conv_fwd(a, bgate, dw, db, lat_len):
    batch, tot, e = a.shape
    segs = _segments(lat_len, tot)

    def body(a_ref, b_ref, dw_ref, db_ref, o_ref):
        w = dw_ref[...]
        for (s0, n) in segs:
            y = a_ref[0, s0:s0 + n, :] * jax.nn.sigmoid(b_ref[0, s0:s0 + n, :])
            acc = jnp.zeros_like(y) + db_ref[...]
            for k in range(CONV_WIDTH):
                acc = acc + _shifted(y, k - CONV_HALF) * w[k:k + 1, :]
            o_ref[0, s0:s0 + n, :] = acc

    blk = pl.BlockSpec((1, tot, LANES), lambda b, cb: (b, 0, cb))
    return pl.pallas_call(
        body, name="conv_fwd", grid=(batch, e // LANES),
        in_specs=[blk, blk, pl.BlockSpec((CONV_WIDTH, LANES), lambda b, cb: (0, cb)),
                  pl.BlockSpec((1, LANES), lambda b, cb: (0, cb))],
        out_specs=blk, out_shape=jax.ShapeDtypeStruct(a.shape, F32),
        compiler_params=pltpu.CompilerParams(dimension_semantics=("arbitrary", "arbitrary"),
                                             vmem_limit_bytes=VMEM_LIMIT),
    )(a, bgate, dw, db)


def _conv_bwd(a, bgate, dw, dc1, lat_len, hosted=None):
    batch, tot, e = a.shape
    segs = _segments(lat_len, tot)

    def body(a_ref, b_ref, dw_ref, dc_ref, da_ref, dg_ref, ddw_ref, ddb_ref):
        b = pl.program_id(1)
        w = dw_ref[...]
        ddw_rows = [None] * CONV_WIDTH
        ddb = None
        for (s0, n) in segs:
            av = a_ref[0, s0:s0 + n, :]
            sg = jax.nn.sigmoid(b_ref[0, s0:s0 + n, :])
            y = av * sg
            dc = dc_ref[0, s0:s0 + n, :]
            dy = jnp.zeros_like(y)
            for k in range(CONV_WIDTH):
                j = k - CONV_HALF
                dy = dy + _shifted(dc, -j) * w[k:k + 1, :]
                r = jnp.sum(dc * _shifted(y, j), axis=0, keepdims=True)
                ddw_rows[k] = r if ddw_rows[k] is None else ddw_rows[k] + r
            r = jnp.sum(dc, axis=0, keepdims=True)
            ddb = r if ddb is None else ddb + r
            da_ref[0, s0:s0 + n, :] = dy * sg
            dg_ref[0, s0:s0 + n, :] = dy * av * sg * (1.0 - sg)

        @pl.when(b == 0)
        def _():
            ddw_ref[...] = jnp.zeros_like(ddw_ref)
            ddb_ref[...] = jnp.zeros_like(ddb_ref)

        for k in range(CONV_WIDTH):
            ddw_ref[k:k + 1, :] += ddw_rows[k]
        ddb_ref[...] += ddb

    blk = pl.BlockSpec((1, tot, LANES), lambda cb, b: (b, 0, cb))
    wspec = pl.BlockSpec((CONV_WIDTH, LANES), lambda cb, b: (0, cb))
    bspec = pl.BlockSpec((1, LANES), lambda cb, b: (0, cb))
    return _pcall(
        body, name="conv_bwd", grid=(e // LANES, batch),
        in_specs=[blk, blk, wspec, blk],
        out_specs=[blk, blk, wspec, bspec],
        out_shape=[jax.ShapeDtypeStruct(a.shape, F32), jax.ShapeDtypeStruct(a.shape, F32),
                   jax.ShapeDtypeStruct((CONV_WIDTH, e), F32), jax.ShapeDtypeStruct((1, e), F32)],
        args=[a, bgate, dw, dc1], hosted=hosted)


def _pool_taps(group):
    half = lax.shift_left(jnp.int32(1), group)
    taps = []
    for j in range(-POOL_HALF, POOL_HALF):
        inside = jnp.logical_and(j >= -half, j < half)
        taps.append(jnp.where(inside, 1.0, 0.0).astype(F32))
    return taps, half


def _pool_counts(n, half, shape):
    t = lax.broadcasted_iota(jnp.int32, shape, 0)
    cnt = jnp.minimum(t + half, n) - jnp.maximum(t - half, 0)
    return cnt.astype(F32)


def _pool_fwd(v, lat_len):
    batch, tot, e = v.shape
    gw = e // len(POOL_WINDOWS)
    segs = _segments(lat_len, tot)

    def body(v_ref, o_ref):
        taps, half = _pool_taps(pl.program_id(1))
        for (s0, n) in segs:
            x = v_ref[0, s0:s0 + n, :]
            acc = jnp.zeros_like(x)
            for idx, j in enumerate(range(-POOL_HALF, POOL_HALF)):
                acc = acc + _shifted(x, j) * taps[idx]
            o_ref[0, s0:s0 + n, :] = acc / _pool_counts(n, half, x.shape) - x

    blk = pl.BlockSpec((1, tot, gw), lambda b, g: (b, 0, g))
    return pl.pallas_call(
        body, name="pool_fwd", grid=(batch, len(POOL_WINDOWS)), in_specs=[blk], out_specs=blk,
        out_shape=jax.ShapeDtypeStruct(v.shape, F32),
        compiler_params=pltpu.CompilerParams(dimension_semantics=("arbitrary", "arbitrary"),
                                             vmem_limit_bytes=VMEM_LIMIT),
    )(v)


def _pool_bwd(dp, lat_len):
    batch, tot, e = dp.shape
    gw = e // len(POOL_WINDOWS)
    segs = _segments(lat_len, tot)

    def body(d_ref, o_ref):
        taps, half = _pool_taps(pl.program_id(1))
        for (s0, n) in segs:
            d = d_ref[0, s0:s0 + n, :]
            dn = d / _pool_counts(n, half, d.shape)
            acc = jnp.zeros_like(d)
            for idx, j in enumerate(range(-POOL_HALF, POOL_HALF)):
                acc = acc + _shifted(dn, -j) * taps[idx]
            o_ref[0, s0:s0 + n, :] = acc - d

    blk = pl.BlockSpec((1, tot, gw), lambda b, g: (b, 0, g))
    return pl.pallas_call(
        body, name="pool_bwd", grid=(batch, len(POOL_WINDOWS)), in_specs=[blk], out_specs=blk,
        out_shape=jax.ShapeDtypeStruct(dp.shape, F32),
        compiler_params=pltpu.CompilerParams(dimension_semantics=("arbitrary", "arbitrary"),
                                             vmem_limit_bytes=VMEM_LIMIT),
    )(dp)


def _softmax_rows(q, k):
    s = _dot_nt(q, k) * ATT_SCALE
    m = jnp.max(s, axis=-1, keepdims=True)
    e = jnp.exp(s - m)
    return e / jnp.sum(e, axis=-1, keepdims=True)


def _attn_fwd(q, k, v):
    batch, lq, _ = q.shape
    tk = k.shape[1]

    def body(q_ref, k_ref, v_ref, o_ref):
        p = _softmax_rows(q_ref[0], k_ref[0])
        o_ref[0] = _dot(p, v_ref[0])

    return pl.pallas_call(
        body, name="attn_fwd", grid=(batch, HEADS, lq // TQ),
        in_specs=[pl.BlockSpec((1, TQ, HEAD_W), lambda b, h, i: (b, i, h)),
                  pl.BlockSpec((1, tk, HEAD_W), lambda b, h, i: (b, 0, h)),
                  pl.BlockSpec((1, tk, VDIM), lambda b, h, i: (b, 0, h))],
        out_specs=pl.BlockSpec((1, TQ, VDIM), lambda b, h, i: (b, i, h)),
        out_shape=jax.ShapeDtypeStruct((batch, lq, HEADS * VDIM), F32),
        compiler_params=pltpu.CompilerParams(dimension_semantics=("arbitrary",) * 3, vmem_limit_bytes=VMEM_LIMIT),
    )(q, k, v)


def _attn_bwd(q, k, v, do, hosted=None):
    batch, lq, _ = q.shape
    tk = k.shape[1]

    def body(q_ref, k_ref, v_ref, do_ref, dq_ref, dk_ref, dv_ref):
        i = pl.program_id(2)
        qv, kv, vv, dov = q_ref[0], k_ref[0], v_ref[0], do_ref[0]
        p = _softmax_rows(qv, kv)
        dp = _dot_nt(dov, vv)
        ds = p * (dp - jnp.sum(p * dp, axis=-1, keepdims=True)) * ATT_SCALE
        dq_ref[0] = _dot(ds, kv)
        dk = _dot_tn(ds, qv)
        dv = _dot_tn(p, dov)

        @pl.when(i == 0)
        def _():
            dk_ref[0] = dk
            dv_ref[0] = dv

        @pl.when(i != 0)
        def _():
            dk_ref[0] += dk
            dv_ref[0] += dv

    return _pcall(
        body, name="attn_bwd", grid=(batch, HEADS, lq // TQ),
        in_specs=[pl.BlockSpec((1, TQ, HEAD_W), lambda b, h, i: (b, i, h)),
                  pl.BlockSpec((1, tk, HEAD_W), lambda b, h, i: (b, 0, h)),
                  pl.BlockSpec((1, tk, VDIM), lambda b, h, i: (b, 0, h)),
                  pl.BlockSpec((1, TQ, VDIM), lambda b, h, i: (b, i, h))],
        out_specs=[pl.BlockSpec((1, TQ, HEAD_W), lambda b, h, i: (b, i, h)),
                   pl.BlockSpec((1, tk, HEAD_W), lambda b, h, i: (b, 0, h)),
                   pl.BlockSpec((1, tk, VDIM), lambda b, h, i: (b, 0, h))],
        out_shape=[jax.ShapeDtypeStruct(q.shape, F32), jax.ShapeDtypeStruct(k.shape, F32),
                   jax.ShapeDtypeStruct(v.shape, F32)],
        args=[q, k, v, do], hosted=hosted)


def _loss_kernel(y, target):
    batch, lq, d = y.shape

    def body(y_ref, t_ref, l_ref, dy_ref):
        first = jnp.logical_and(pl.program_id(0) == 0, pl.program_id(1) == 0)
        err = y_ref[0] - t_ref[0]
        dy_ref[0] = err * (1.0 / d)
        part = jnp.zeros((1, LANES), F32) + jnp.sum(err * err) * (0.5 / d)

        @pl.when(first)
        def _():
            l_ref[...] = part

        @pl.when(jnp.logical_not(first))
        def _():
            l_ref[...] += part

    blk = pl.BlockSpec((1, TM, d), lambda b, i: (b, i, 0))
    return pl.pallas_call(
        body, name="loss_head", grid=(batch, lq // TM), in_specs=[blk, blk],
        out_specs=[pl.BlockSpec((1, LANES), lambda b, i: (0, 0)), blk],
        out_shape=[jax.ShapeDtypeStruct((1, LANES), F32), jax.ShapeDtypeStruct(y.shape, F32)],
        compiler_params=pltpu.CompilerParams(dimension_semantics=("arbitrary", "arbitrary")),
    )(y, target)


def _rope_tables(lat_len, ctx_len):
    rows = lat_len // GRID_W
    row_id = jnp.repeat(jnp.arange(rows), GRID_W).astype(F32)
    col_id = jnp.tile(jnp.arange(GRID_W), rows).astype(F32)
    axis_dim = ROPE // 2
    freqs = ROPE_THETA ** (-jnp.arange(0, axis_dim, 2, dtype=F32) / axis_dim)
    ar = row_id[:, None] * freqs
    ac = col_id[:, None] * freqs
    cr, sr, cc, sc = jnp.cos(ar), jnp.sin(ar), jnp.cos(ac), jnp.sin(ac)
    pad = jnp.zeros((lat_len, LANES - ROPE), F32)
    cos = jnp.concatenate([cr, cr, cc, cc, pad], axis=1)
    sin = jnp.concatenate([-sr, sr, -sc, sc, pad], axis=1)
    ident = jnp.concatenate([jnp.ones((ctx_len, ROPE), F32), jnp.zeros((ctx_len, LANES - ROPE), F32)], axis=1)
    cos = jnp.concatenate([cos, ident], axis=0)
    sin = jnp.concatenate([sin, jnp.zeros((ctx_len, LANES), F32)], axis=0)
    return cos[None], sin[None]


def _prep_weights(w):
    p = dict(w)
    kvc = KV_RANK + ROPE
    if "ml_w_in" in w:
        wi = w["ml_w_in"]
        p["ml_w_in"] = jnp.concatenate(
            [wi[:, :kvc], jnp.zeros((wi.shape[0], LANES - ROPE), wi.dtype), wi[:, kvc:]], axis=1)
    if "ml_w_uq" in w:
        uq = w["ml_w_uq"].reshape(Q_RANK, HEADS, NOPE + ROPE)
        p["ml_w_uq"] = jnp.pad(uq, ((0, 0), (0, 0), (0, HEAD_W - NOPE - ROPE))).reshape(Q_RANK, HEADS * HEAD_W)
    if "ml_rope_norm" in w:
        p["ml_rope_norm"] = jnp.pad(w["ml_rope_norm"], ((0, 0), (0, LANES - ROPE)))
    return p


def _unprep_grads(g):
    out = dict(g)
    kvc = KV_RANK + ROPE
    if "ml_w_in" in g:
        wi = g["ml_w_in"]
        out["ml_w_in"] = jnp.concatenate([wi[:, :kvc], wi[:, kvc + LANES - ROPE:]], axis=1)
    if "ml_w_uq" in g:
        uq = g["ml_w_uq"].reshape(Q_RANK, HEADS, HEAD_W)
        out["ml_w_uq"] = uq[:, :, :NOPE + ROPE].reshape(Q_RANK, HEADS * (NOPE + ROPE))
    if "ml_rope_norm" in g:
        out["ml_rope_norm"] = g["ml_rope_norm"][:, :ROPE]
    return out


LAYER_WEIGHTS = (("cv_w_in", "cv_w_out"), ("pl_w_in", "pl_w_grp", "pl_w_out"),
                 ("ml_w_in", "ml_w_uq", "ml_w_ukv", "ml_w_out"), ("ch_w_in", "ch_w_out"))


class _LocalPlan:
    def __init__(self, w):
        self.small = w
        self.grads = {}

    def weights(self, layer):
        return {n: self.small[n] for n in LAYER_WEIGHTS[layer]}

    def hosted(self, tag):
        return None

    def after(self, tag):
        pass

    def layer_grads(self, layer, grads):
        self.grads.update(grads)


def _local_step(xm, target, mods, plan, lat_len):
    batch, tot, d = xm.shape
    e = d
    n_all, n_lat = tot // TM, lat_len // TM
    cos, sin = _rope_tables(lat_len, tot - lat_len)
    g = {}
    w = dict(plan.small)

    def hosting(tag, fn, *args, **kwargs):
        out = fn(*args, hosted=plan.hosted(tag), **kwargs)
        plan.after(tag)
        return out

    def s1_splits(widths):
        out, s = [], 0
        for wd in widths:
            out.append((0, s, wd))
            s += wd
        return out

    def fwd_in(name, x, mod, gi, wname, widths, n_tiles):
        return hosting(name, _stage_fwd, name, pre=_pre_adaln, post=None, wsel=[0], splits=s1_splits(widths),
                       tiles=[x], eparams=[mod[0], mod[1]], sparams=[w["norm_g"][gi:gi + 1]], weights=[w[wname]],
                       out_widths=widths, out_dtypes=[F32] * len(widths), batch=batch, n_tiles=n_tiles,
                       n_lat_tiles=n_lat)

    def bwd_in(name, x, mod, gi, wname, widths, n_tiles, cots, lat_only, add, add_lat_only):
        (dx,), (dsh, dsc), (dg,), (dw,) = hosting(
            name, _stage_bwd, name, pre=_pre_adaln, post=None, wsel=[0], splits=s1_splits(widths), tiles=[x],
            tile_diff=[True], eparams=[mod[0], mod[1]], sparams=[w["norm_g"][gi:gi + 1]], weights=[w[wname]],
            cots=cots, cot_lat_only=lat_only, batch=batch, n_tiles=n_tiles, n_lat_tiles=n_lat, add=add,
            add_lat_only=add_lat_only)
        return dx, dsh, dsc, dg, dw

    def fwd_out(name, pre, tiles, mod, sparams, wname, n_tiles):
        return hosting(name, _stage_fwd, name, pre=pre, post=_post_residual(len(tiles) - 1), wsel=[0], splits=None,
                       tiles=tiles, eparams=[mod[2]], sparams=sparams, weights=[w[wname]], out_widths=[d],
                       out_dtypes=[F32], batch=batch, n_tiles=n_tiles, n_lat_tiles=n_lat)[0]

    def bwd_out(name, pre, tiles, mod, sparams, wname, n_tiles, cot):
        diff = [True] * (len(tiles) - 1) + [False]
        dts, (dgt,), dss, (dw,) = hosting(
            name, _stage_bwd, name, pre=pre, post=_post_residual(len(tiles) - 1), wsel=[0], splits=None, tiles=tiles,
            tile_diff=diff, eparams=[mod[2]], sparams=sparams, weights=[w[wname]], cots=[cot], cot_lat_only=[False],
            batch=batch, n_tiles=n_tiles, n_lat_tiles=n_lat)
        return dts, dgt, dss, dw

    w.update(plan.weights(0))
    cv_s = [w["cv_ln_g"], w["cv_ln_b"]]
    a0, b0, g0 = fwd_in("cv_in_fwd", xm, mods[0], 0, "cv_w_in", [e, e, e], n_all)
    c1 = _conv_fwd(a0, b0, w["cv_dw"], w["cv_db"], lat_len)
    x1 = fwd_out("cv_out_fwd", _pre_conv_out, [c1, g0, xm], mods[0], cv_s, "cv_w_out", n_all)

    w.update(plan.weights(1))
    pl_s = [w["pl_w_grp"], w["pl_scale"]]
    v1, g1 = fwd_in("pl_in_fwd", x1, mods[1], 1, "pl_w_in", [e, e], n_all)
    pooled = _pool_fwd(v1, lat_len)
    x2 = fwd_out("pl_out_fwd", _pre_pool_out, [pooled, g1, x1], mods[1], pl_s, "pl_w_out", n_all)

    w.update(plan.weights(2))
    ml_widths = [KV_RANK, LANES, Q_RANK, HEADS * VDIM]
    ckv, krp, cq, g2 = fwd_in("ml_in_fwd", x2, mods[2], 2, "ml_w_in", ml_widths, n_all)
    k_s = [w["ml_kv_norm"], w["ml_nope_norm"][1:2], w["ml_rope_norm"][1:2]]
    q_s = [w["ml_q_norm"], w["ml_nope_norm"][0:1], w["ml_rope_norm"][0:1]]
    kk, vv = hosting("ml_keys_fwd", _stage_fwd, "ml_keys_fwd", pre=_pre_rms_only, post=_post_mla_keys, wsel=[0],
                     splits=None, tiles=[ckv, krp, cos, sin], eparams=[], sparams=k_s, weights=[w["ml_w_ukv"]],
                     out_widths=[HEADS * HEAD_W, HEADS * VDIM], out_dtypes=[BF16, BF16], batch=batch,
                     n_tiles=n_all, n_lat_tiles=n_lat)
    (qq,) = _stage_fwd("ml_queries_fwd", pre=_pre_rms_only, post=_post_mla_queries, wsel=[0], splits=None,
                       tiles=[cq, cos, sin], eparams=[], sparams=q_s, weights=[w["ml_w_uq"]],
                       out_widths=[HEADS * HEAD_W], out_dtypes=[BF16], batch=batch, n_tiles=n_lat,
                       n_lat_tiles=n_lat)
    att = _attn_fwd(qq, kk, vv)
    x3 = fwd_out("ml_out_fwd", _pre_mla_out, [att, g2, x2], mods[2], [], "ml_w_out", n_lat)

    w.update(plan.weights(3))
    ch_s = [w["ch_ln_g"], w["ch_ln_b"], w["ch_w_s"], w["ch_b_s"]]
    u3, v3, g3 = fwd_in("ch_in_fwd", x3, mods[3], 3, "ch_w_in", [e, e, e], n_lat)
    x4 = fwd_out("ch_out_fwd", _pre_chunk_out, [u3, v3, g3, x3], mods[3], ch_s, "ch_w_out", n_lat)

    loss_part, dy = _loss_kernel(x4, target)

    dmods = [None] * 4
    dnorm = [None] * 4
    big = {}
    (du, dv, dg), dgt, (g["ch_ln_g"], g["ch_ln_b"], g["ch_w_s"], g["ch_b_s"]), big["ch_w_out"] = bwd_out(
        "ch_out_bwd", _pre_chunk_out, [u3, v3, g3, x3], mods[3], ch_s, "ch_w_out", n_lat, dy)
    dx3, dsh, dsc, dnorm[3], big["ch_w_in"] = bwd_in("ch_in_bwd", x3, mods[3], 3, "ch_w_in", [e, e, e], n_lat,
                                                     [du, dv, dg], [False] * 3, dy, False)
    dmods[3] = (dsh, dsc, dgt)
    plan.layer_grads(3, big)

    big = {}
    (datt, dg), dgt, _, big["ml_w_out"] = bwd_out("ml_out_bwd", _pre_mla_out, [att, g2, x2], mods[2], [],
                                                  "ml_w_out", n_lat, dx3)
    dq, dk, dvv = hosting("attn_bwd", _attn_bwd, qq, kk, vv, datt)
    (dcq,), _, (g["ml_q_norm"], dnope0, drope0), (big["ml_w_uq"],) = hosting(
        "ml_queries_bwd", _stage_bwd, "ml_queries_bwd", pre=_pre_rms_only, post=_post_mla_queries, wsel=[0],
        splits=None, tiles=[cq, cos, sin], tile_diff=[True, False, False], eparams=[], sparams=q_s,
        weights=[w["ml_w_uq"]], cots=[dq], cot_lat_only=[False], batch=batch, n_tiles=n_lat, n_lat_tiles=n_lat)
    (dckv, dkrp), _, (g["ml_kv_norm"], dnope1, drope1), (big["ml_w_ukv"],) = hosting(
        "ml_keys_bwd", _stage_bwd, "ml_keys_bwd", pre=_pre_rms_only, post=_post_mla_keys, wsel=[0], splits=None,
        tiles=[ckv, krp, cos, sin], tile_diff=[True, True, False, False], eparams=[], sparams=k_s,
        weights=[w["ml_w_ukv"]], cots=[dk, dvv], cot_lat_only=[False, False], batch=batch, n_tiles=n_all,
        n_lat_tiles=n_lat)
    g["ml_nope_norm"] = jnp.concatenate([dnope0, dnope1], axis=0)
    g["ml_rope_norm"] = jnp.concatenate([drope0, drope1], axis=0)
    dx2, dsh, dsc, dnorm[2], big["ml_w_in"] = bwd_in("ml_in_bwd", x2, mods[2], 2, "ml_w_in", ml_widths, n_all,
                                                     [dckv, dkrp, dcq, dg], [False, False, True, True], dx3, True)
    dmods[2] = (dsh, dsc, dgt)
    plan.layer_grads(2, big)

    big = {}
    (dpooled, dg), dgt, (big["pl_w_grp"], g["pl_scale"]), big["pl_w_out"] = bwd_out(
        "pl_out_bwd", _pre_pool_out, [pooled, g1, x1], mods[1], pl_s, "pl_w_out", n_all, dx2)
    dv1 = _pool_bwd(dpooled, lat_len)
    dx1, dsh, dsc, dnorm[1], big["pl_w_in"] = bwd_in("pl_in_bwd", x1, mods[1], 1, "pl_w_in", [e, e], n_all,
                                                     [dv1, dg], [False] * 2, dx2, False)
    dmods[1] = (dsh, dsc, dgt)
    plan.layer_grads(1, big)

    big = {}
    (dc1, dg), dgt, (g["cv_ln_g"], g["cv_ln_b"]), big["cv_w_out"] = bwd_out(
        "cv_out_bwd", _pre_conv_out, [c1, g0, xm], mods[0], cv_s, "cv_w_out", n_all, dx1)
    da, db, g["cv_dw"], g["cv_db"] = hosting("conv_bwd", _conv_bwd, a0, b0, w["cv_dw"], dc1, lat_len)
    dx0, dsh, dsc, dnorm[0], big["cv_w_in"] = bwd_in("cv_in_bwd", xm, mods[0], 0, "cv_w_in", [e, e, e], n_all,
                                                     [da, db, dg], [False] * 3, dx1, False)
    dmods[0] = (dsh, dsc, dgt)
    plan.layer_grads(0, big)
    g["norm_g"] = jnp.concatenate(dnorm, axis=0)
    return loss_part, dx0[:, :lat_len], dmods, g


N_DEV = 8
N_CHIP = 4
ANY = pl.BlockSpec(memory_space=pl.ANY)


def _my_place():
    return lax.axis_index("x"), lax.axis_index("y"), lax.axis_index("c")


def _flip(v, f):
    return 1 - v if f else v


def _ag8(name, x):
    r, cw = x.shape

    def body(x_ref, o_ref, send_sems, recv_sems, local_sem):
        mx, my, mc = _my_place()
        me = 4 * mx + 2 * my + mc
        mine = pltpu.make_async_copy(x_ref, o_ref.at[me], local_sem)
        mine.start()
        sends = []
        for rel in range(1, N_DEV):
            peer = (_flip(mx, rel & 4), _flip(my, rel & 2), _flip(mc, rel & 1))
            cp = pltpu.make_async_remote_copy(src_ref=x_ref, dst_ref=o_ref.at[me], send_sem=send_sems.at[rel - 1],
                                              recv_sem=recv_sems.at[rel - 1], device_id=peer, device_id_type=MESH)
            cp.start()
            sends.append(cp)
        for rel in range(1, N_DEV):
            peer = (_flip(mx, rel & 4), _flip(my, rel & 2), _flip(mc, rel & 1))
            src_dev = 4 * peer[0] + 2 * peer[1] + peer[2]
            pltpu.make_async_remote_copy(src_ref=x_ref, dst_ref=o_ref.at[src_dev], send_sem=send_sems.at[rel - 1],
                                         recv_sem=recv_sems.at[rel - 1], device_id=peer,
                                         device_id_type=MESH).wait_recv()
        for cp in sends:
            cp.wait_send()
        mine.wait()

    return pl.pallas_call(
        body, name=name, out_shape=jax.ShapeDtypeStruct((N_DEV, r, cw), x.dtype),
        in_specs=[pl.BlockSpec(memory_space=pltpu.VMEM)], out_specs=pl.BlockSpec(memory_space=pltpu.VMEM),
        scratch_shapes=[pltpu.SemaphoreType.DMA((N_DEV - 1,)), pltpu.SemaphoreType.DMA((N_DEV - 1,)),
                        pltpu.SemaphoreType.DMA],
        compiler_params=pltpu.CompilerParams(vmem_limit_bytes=VMEM_LIMIT),
    )(x)


def _chip_peers(mx, my, mc):
    out = []
    for rel in range(1, N_CHIP):
        px, py = _flip(mx, rel & 2), _flip(my, rel & 1)
        out.append((rel - 1, (px, py, mc), 2 * px + py))
    return out


def _half(mc, rows):
    return pl.ds(pl.multiple_of(mc * (rows // 2), 8), rows // 2)


def _copies_hosted(arrays, out_shapes, n_sems, plan, aliases=None):
    def start(ins, outs, sems):
        sends, _, locals_ = plan(ins, outs, sems)
        for cp in locals_ + sends:
            cp.start()

    def wait(ins, outs, sems):
        sends, recvs, locals_ = plan(ins, outs, sems)
        for cp in recvs:
            cp.wait_recv()
        for cp in sends:
            cp.wait_send()
        for cp in locals_:
            cp.wait()

    return _Hosted(arrays, out_shapes, [pltpu.SemaphoreType.DMA((k,)) for k in n_sems], start, wait, aliases)


def _remote(src, dst, sems, k, peer):
    return pltpu.make_async_remote_copy(src_ref=src, dst_ref=dst, send_sem=sems[0].at[k], recv_sem=sems[1].at[k],
                                        device_id=peer, device_id_type=MESH)


def _gather_ici(shards):
    n = len(shards)

    def plan(ins, outs, sems):
        mx, my, mc = _my_place()
        chip = 2 * mx + my
        sends, recvs, locals_ = [], [], []
        for a in range(n):
            rows = ins[a].shape[0]
            locals_.append(pltpu.make_async_copy(ins[a], outs[a].at[chip], sems[2].at[a]))
            for k, peer, pchip in _chip_peers(mx, my, mc):
                src = ins[a].at[_half(mc, rows)]
                sends.append(_remote(src, outs[a].at[chip, _half(mc, rows)], sems, 3 * a + k, peer))
                recvs.append(_remote(src, outs[a].at[pchip, _half(mc, rows)], sems, 3 * a + k, peer))
        return sends, recvs, locals_

    return _copies_hosted(shards, [jax.ShapeDtypeStruct((N_CHIP,) + s.shape, s.dtype) for s in shards],
                          (3 * n, 3 * n, n), plan)


def _sibling_fill(arrays, row_axis, chips_only_other):
    n = len(arrays)
    per = 3 if chips_only_other else 1

    def plan(ins, outs, sems):
        mx, my, mc = _my_place()
        sibling = (mx, my, 1 - mc)

        def views(a, core):
            rows = outs[a].shape[row_axis]
            if chips_only_other:
                return [outs[a].at[pchip, _half(core, rows)] for _, _, pchip in _chip_peers(mx, my, mc)]
            return [outs[a].at[_half(core, rows)]]

        sends, recvs = [], []
        for a in range(n):
            for k, v in enumerate(views(a, mc)):
                sends.append(_remote(v, v, sems, per * a + k, sibling))
            for k, v in enumerate(views(a, 1 - mc)):
                recvs.append(_remote(v, v, sems, per * a + k, sibling))
        return sends, recvs, []

    return _copies_hosted(arrays, [jax.ShapeDtypeStruct(s.shape, s.dtype) for s in arrays], (per * n, per * n), plan,
                          aliases={a: a for a in range(n)})


def _grad_swap_d2d(stacks):
    n = len(stacks)

    def plan(ins, outs, sems):
        mx, my, mc = _my_place()
        sibling = (mx, my, 1 - mc)
        sends = [_remote(ins[a].at[:, _half(1 - mc, ins[a].shape[1])], outs[a], sems, a, sibling) for a in range(n)]
        return sends, sends, []

    return _copies_hosted(stacks, [jax.ShapeDtypeStruct((N_CHIP, s.shape[1] // 2, s.shape[2]), s.dtype)
                                   for s in stacks], (n, n), plan)


def _grad_exchange_ici(parts):
    n = len(parts)

    def plan(ins, outs, sems):
        mx, my, mc = _my_place()
        chip = 2 * mx + my
        sends, recvs, locals_ = [], [], []
        for a in range(n):
            locals_.append(pltpu.make_async_copy(ins[a].at[chip], outs[a].at[chip], sems[2].at[a]))
            for k, peer, pchip in _chip_peers(mx, my, mc):
                sends.append(_remote(ins[a].at[pchip], outs[a].at[chip], sems, 3 * a + k, peer))
                recvs.append(_remote(ins[a].at[pchip], outs[a].at[pchip], sems, 3 * a + k, peer))
        return sends, recvs, locals_

    return _copies_hosted(parts, [jax.ShapeDtypeStruct(s.shape, s.dtype) for s in parts], (3 * n, 3 * n, n), plan)


def _row_block(rows, limit=256):
    for t in range(min(rows, limit), 7, -8):
        if rows % t == 0 and t % 8 == 0:
            return t
    return rows


def _grad_add_half(core, stack, received):
    _, rows, cw = stack.shape
    rh = rows // 2
    tr = _row_block(rh)

    def body(s_ref, a_ref, b_ref, o_ref):
        o_ref[...] = a_ref[...] + b_ref[...]

    grid_spec = pltpu.PrefetchScalarGridSpec(
        num_scalar_prefetch=1, grid=(rh // tr,),
        in_specs=[pl.BlockSpec((N_CHIP, tr, cw), lambda i, s: (0, s[0] * (rh // tr) + i, 0)),
                  pl.BlockSpec((N_CHIP, tr, cw), lambda i, s: (0, i, 0))],
        out_specs=pl.BlockSpec((N_CHIP, tr, cw), lambda i, s: (0, i, 0)))
    return pl.pallas_call(
        body, name="grad_add_half", grid_spec=grid_spec, out_shape=jax.ShapeDtypeStruct(received.shape, F32),
        compiler_params=pltpu.CompilerParams(dimension_semantics=("arbitrary",), vmem_limit_bytes=VMEM_LIMIT),
    )(core, stack, received)


def _adamw(name, row_off, parts, w, m, v, rows):
    n, _, cw = parts.shape
    tr = _row_block(rows, 128)
    c1 = 1.0 / (1.0 - ADAM_B1 ** ADAM_STEP)
    c2 = 1.0 / (1.0 - ADAM_B2 ** ADAM_STEP)

    def body(s_ref, p_ref, w_ref, m_ref, v_ref, g_ref, d_ref, nm_ref, nv_ref):
        g = p_ref[0]
        for k in range(1, n):
            g = g + p_ref[k]
        nm = ADAM_B1 * m_ref[...] + (1.0 - ADAM_B1) * g
        nv = ADAM_B2 * v_ref[...] + (1.0 - ADAM_B2) * (g * g)
        m_hat = nm / (1.0 - ADAM_B1 ** ADAM_STEP)
        v_hat = nv / (1.0 - ADAM_B2 ** ADAM_STEP)
        g_ref[...] = g
        d_ref[...] = -ADAM_LR * (m_hat / (jnp.sqrt(v_hat) + ADAM_EPS) + ADAM_WD * w_ref[...])
        nm_ref[...] = nm
        nv_ref[...] = nv

    del c1, c2
    full = pl.BlockSpec((tr, cw), lambda i, s: (s[0] // tr + i, 0))
    grid_spec = pltpu.PrefetchScalarGridSpec(
        num_scalar_prefetch=1, grid=(rows // tr,),
        in_specs=[pl.BlockSpec((n, tr, cw), lambda i, s: (0, i, 0)), full, full, full],
        out_specs=[full, full, full, full])
    return pl.pallas_call(
        body, name=name, grid_spec=grid_spec, out_shape=[jax.ShapeDtypeStruct(w.shape, F32)] * 4,
        compiler_params=pltpu.CompilerParams(dimension_semantics=("arbitrary",), vmem_limit_bytes=VMEM_LIMIT),
    )(row_off, parts, w, m, v)


def _sum8(x):
    _, r, cw = x.shape
    tr = _row_block(r, 64)

    def body(x_ref, o_ref):
        acc = x_ref[0]
        for k in range(1, N_DEV):
            acc = acc + x_ref[k]
        o_ref[...] = acc

    return pl.pallas_call(
        body, name="sum8", grid=(r // tr,), in_specs=[pl.BlockSpec((N_DEV, tr, cw), lambda i: (0, i, 0))],
        out_specs=pl.BlockSpec((tr, cw), lambda i: (i, 0)), out_shape=jax.ShapeDtypeStruct((r, cw), F32),
        compiler_params=pltpu.CompilerParams(dimension_semantics=("arbitrary",)),
    )(x)


MOD_ROWS = 24
CTX_ROW = 16


def _mod_fwd(c_rows, w_mod, b_mod):
    nl, d, nn = w_mod.shape

    def body(c_ref, w_ref, b_ref, o_ref):
        o_ref[0] = _dot(_silu(c_ref[...]), w_ref[0]) + b_ref[0]

    return pl.pallas_call(
        body, name="mod_fwd", grid=(nl,),
        in_specs=[pl.BlockSpec((MOD_ROWS, d), lambda i: (0, 0)), pl.BlockSpec((1, d, nn), lambda i: (i, 0, 0)),
                  pl.BlockSpec((1, 1, nn), lambda i: (i, 0, 0))],
        out_specs=pl.BlockSpec((1, MOD_ROWS, nn), lambda i: (i, 0, 0)),
        out_shape=jax.ShapeDtypeStruct((nl, MOD_ROWS, nn), F32),
        compiler_params=pltpu.CompilerParams(dimension_semantics=("arbitrary",), vmem_limit_bytes=VMEM_LIMIT),
    )(c_rows, w_mod, b_mod)


def _mod_bwd_rows(dlat, dctx_parts):
    nl, ne, nn = dlat.shape

    def body(l_ref, c_ref, db_ref, dc_ref):
        dc = c_ref[0, 0:1, :]
        for k in range(1, N_DEV):
            dc = dc + c_ref[0, k:k + 1, :]
        db = dc
        for k in range(ne):
            db = db + l_ref[0, k:k + 1, :]
        db_ref[0] = db
        dc_ref[0] = dc

    return pl.pallas_call(
        body, name="mod_bwd_rows", grid=(nl,),
        in_specs=[pl.BlockSpec((1, ne, nn), lambda i: (i, 0, 0)), pl.BlockSpec((1, N_DEV, nn), lambda i: (i, 0, 0))],
        out_specs=[pl.BlockSpec((1, 1, nn), lambda i: (i, 0, 0))] * 2,
        out_shape=[jax.ShapeDtypeStruct((nl, 1, nn), F32)] * 2,
        compiler_params=pltpu.CompilerParams(dimension_semantics=("arbitrary",)),
    )(dlat, dctx_parts)


def _mod_bwd_w(c_cols, d_rows, w_mod):
    nl, d, nn = w_mod.shape

    def body(c_ref, d_ref, w_ref, dw_ref, dc_ref):
        i = pl.program_id(0)
        c = c_ref[...]
        sg = jax.nn.sigmoid(c)
        s = c * sg
        dv = d_ref[0]
        acc = s[:, 0:1] * dv[0:1, :]
        for r in range(1, CTX_ROW + 1):
            acc = acc + s[:, r:r + 1] * dv[r:r + 1, :]
        dw_ref[0] = acc
        ds_ctx = jnp.sum(w_ref[0] * dv[CTX_ROW:CTX_ROW + 1, :], axis=1, keepdims=True)
        cc, sc = c[:, CTX_ROW:CTX_ROW + 1], sg[:, CTX_ROW:CTX_ROW + 1]
        part = ds_ctx * (sc * (1.0 + cc * (1.0 - sc)))

        @pl.when(i == 0)
        def _():
            dc_ref[...] = part

        @pl.when(i != 0)
        def _():
            dc_ref[...] += part

    return pl.pallas_call(
        body, name="mod_bwd_w", grid=(nl,),
        in_specs=[pl.BlockSpec((d, MOD_ROWS), lambda i: (0, 0)), pl.BlockSpec((1, MOD_ROWS, nn), lambda i: (i, 0, 0)),
                  pl.BlockSpec((1, d, nn), lambda i: (i, 0, 0))],
        out_specs=[pl.BlockSpec((1, d, nn), lambda i: (i, 0, 0)), pl.BlockSpec((d, 1), lambda i: (0, 0))],
        out_shape=[jax.ShapeDtypeStruct((nl, d, nn), F32), jax.ShapeDtypeStruct((d, 1), F32)],
        compiler_params=pltpu.CompilerParams(dimension_semantics=("arbitrary",), vmem_limit_bytes=VMEM_LIMIT),
    )(c_cols, d_rows, w_mod)


def _pack_rows(arrays, width, row_multiple=8):
    rows, spans, r0 = [], [], 0
    for a in arrays:
        flat = a.reshape(-1)
        nr = -(-flat.shape[0] // width)
        held = -(-nr // 8) * 8
        flat = jnp.pad(flat, (0, held * width - flat.shape[0]))
        rows.append(flat.reshape(held, width))
        spans.append((r0, nr, a.shape))
        r0 += held
    if r0 % row_multiple:
        rows.append(jnp.zeros((row_multiple - r0 % row_multiple, width), F32))
    return jnp.concatenate(rows, axis=0), spans


def _unpack_rows(packed, spans):
    out = []
    for r0, nr, shape in spans:
        out.append(packed[r0:r0 + nr].reshape(-1)[:math.prod(shape)].reshape(shape))
    return out


BIG = {"cv_w_in": 1, "cv_w_out": 0, "pl_w_in": 1, "pl_w_grp": None, "pl_w_out": 0, "ml_w_in": 1, "ml_w_uq": 1,
       "ml_w_ukv": 1, "ml_w_out": 0, "ch_w_in": 1, "ch_w_out": 0}
SMALL_SHARDED = ["cv_dw", "pl_scale", "ml_q_norm", "ml_kv_norm", "ch_ln_g", "ch_ln_b"]
SMALL_REPLICATED = ["c_ctx", "norm_g", "b_mod", "cv_db", "cv_ln_g", "cv_ln_b", "ml_nope_norm", "ml_rope_norm",
                    "ch_w_s", "ch_b_s"]
WEIGHTS = ['c_ctx', 'norm_g', 'w_mod', 'b_mod', 'cv_w_in', 'cv_dw', 'cv_db', 'cv_ln_g', 'cv_ln_b', 'cv_w_out',
           'pl_w_in', 'pl_w_grp', 'pl_scale', 'pl_w_out', 'ml_w_in', 'ml_q_norm', 'ml_kv_norm', 'ml_w_uq', 'ml_w_ukv',
           'ml_nope_norm', 'ml_rope_norm', 'ml_w_out', 'ch_w_in', 'ch_ln_g', 'ch_ln_b', 'ch_w_s', 'ch_b_s', 'ch_w_out']


def _shard2d(name, a):
    if name == "pl_w_grp":
        return a.reshape(a.shape[-3] * a.shape[-2], a.shape[-1])
    return a.reshape(a.shape[-2], a.shape[-1])


def _unstack(name, s):
    if name == "pl_w_grp":
        ng = len(POOL_WINDOWS)
        return s.reshape(N_CHIP, ng, s.shape[1] // ng, s.shape[2]).transpose(1, 0, 2, 3).reshape(ng, -1, s.shape[2])
    if BIG[name] == 0:
        return s.reshape(-1, s.shape[2])
    return s.transpose(1, 0, 2).reshape(s.shape[1], -1)


def _stack(name, g):
    if name == "pl_w_grp":
        ng = len(POOL_WINDOWS)
        return g.reshape(ng, N_CHIP, -1, g.shape[2]).transpose(1, 0, 2, 3).reshape(N_CHIP, -1, g.shape[2])
    if BIG[name] == 0:
        return g.reshape(N_CHIP, -1, g.shape[1])
    return g.reshape(g.shape[0], N_CHIP, -1).transpose(1, 0, 2)


SCHEDULE = {
    "cv_in_fwd": [("gather", 1)], "cv_out_fwd": [("gfill", 1)],
    "pl_in_fwd": [("gather", 2)], "pl_out_fwd": [("gfill", 2)],
    "ml_in_fwd": [("gather", 3)], "ml_keys_fwd": [("gfill", 3)],
    "ml_out_bwd": [("swap", 3)], "attn_bwd": [("exch", 3)], "ml_queries_bwd": [("ofill", 3)],
    "pl_out_bwd": [("swap", 2)], "pl_in_bwd": [("exch", 2)],
    "cv_out_bwd": [("swap", 1), ("ofill", 2)], "conv_bwd": [("exch", 1)], "cv_in_bwd": [("ofill", 1)],
}


class _DistPlan:
    def __init__(self, weights, m, v, small, core):
        self.W, self.M, self.V, self.small, self.core = weights, m, v, small, core
        self.stacks, self.gstacks, self.parts, self.halves, self.out = {}, {}, {}, {}, {}
        self.live, self.done = {}, set()
        self._alone("gather", 0)
        self._alone("gfill", 0)

    def _make(self, op, layer):
        if op == "gather":
            return _gather_ici([_shard2d(n, self.W[n]).astype(BF16) for n in LAYER_WEIGHTS[layer]])
        if op == "gfill":
            return _sibling_fill(self.stacks[layer], 1, True)
        if op == "swap":
            return _grad_swap_d2d(self.gstacks[layer])
        if op == "exch":
            return _grad_exchange_ici(self.parts[layer])
        return _sibling_fill(self.halves[layer], 0, False)

    def _finish_op(self, op, layer, hosted):
        names = LAYER_WEIGHTS[layer]
        self.done.add((op, layer))
        if op in ("gather", "gfill"):
            self.stacks[layer] = hosted.results
        elif op == "swap":
            self.parts[layer] = [_grad_add_half(self.core, s, r) for s, r in zip(self.gstacks[layer], hosted.results)]
        elif op == "exch":
            halves = []
            for n, q in zip(names, hosted.results):
                rh = q.shape[1]
                halves.extend(_adamw("adamw_" + n, self.core * rh, q, _shard2d(n, self.W[n]), _shard2d(n, self.M[n]),
                                     _shard2d(n, self.V[n]), rh))
            self.halves[layer] = halves
        else:
            for k, n in enumerate(names):
                self.out[n] = tuple(r.reshape(self.W[n].shape) for r in hosted.results[4 * k:4 * k + 4])

    def _alone(self, op, layer):
        hosted = self._make(op, layer)
        _run_hosted("%s_%d" % (op, layer), hosted)
        self._finish_op(op, layer, hosted)

    def weights(self, layer):
        wk = {n: _unstack(n, s) for n, s in zip(LAYER_WEIGHTS[layer], self.stacks[layer])}
        if "pl_w_grp" in wk:
            wk["pl_w_grp"] = wk["pl_w_grp"].astype(F32)
        return _prep_weights(wk)

    def hosted(self, tag):
        self.live[tag] = [(op, layer, self._make(op, layer)) for op, layer in SCHEDULE.get(tag, [])]
        return _merge_hosted([h for _, _, h in self.live[tag]])

    def after(self, tag):
        for op, layer, hosted in self.live.pop(tag, []):
            self._finish_op(op, layer, hosted)

    def layer_grads(self, layer, grads):
        g = _unprep_grads(grads)
        self.gstacks[layer] = [_stack(n, g[n]) for n in LAYER_WEIGHTS[layer]]

    def finish(self):
        for layer in (3, 2, 1, 0):
            for op in ("swap", "exch", "ofill"):
                if (op, layer) not in self.done:
                    self._alone(op, layer)
        return self.out


def kernel(x, c, ctx, c_ctx, norm_g, w_mod, b_mod, cv_w_in, cv_dw, cv_db, cv_ln_g, cv_ln_b, cv_w_out, pl_w_in, pl_w_grp, pl_scale, pl_w_out, ml_w_in, ml_q_norm, ml_kv_norm, ml_w_uq, ml_w_ukv, ml_nope_norm, ml_rope_norm, ml_w_out, ch_w_in, ch_ln_g, ch_ln_b, ch_w_s, ch_b_s, ch_w_out, loss_target, m_c_ctx, m_norm_g, m_w_mod, m_b_mod, m_cv_w_in, m_cv_dw, m_cv_db, m_cv_ln_g, m_cv_ln_b, m_cv_w_out, m_pl_w_in, m_pl_w_grp, m_pl_scale, m_pl_w_out, m_ml_w_in, m_ml_q_norm, m_ml_kv_norm, m_ml_w_uq, m_ml_w_ukv, m_ml_nope_norm, m_ml_rope_norm, m_ml_w_out, m_ch_w_in, m_ch_ln_g, m_ch_ln_b, m_ch_w_s, m_ch_b_s, m_ch_w_out, v_c_ctx, v_norm_g, v_w_mod, v_b_mod, v_cv_w_in, v_cv_dw, v_cv_db, v_cv_ln_g, v_cv_ln_b, v_cv_w_out, v_pl_w_in, v_pl_w_grp, v_pl_scale, v_pl_w_out, v_ml_w_in, v_ml_q_norm, v_ml_kv_norm, v_ml_w_uq, v_ml_w_ukv, v_ml_nope_norm, v_ml_rope_norm, v_ml_w_out, v_ch_w_in, v_ch_ln_g, v_ch_ln_b, v_ch_w_s, v_ch_b_s, v_ch_w_out):
    W = dict(c_ctx=c_ctx, norm_g=norm_g, w_mod=w_mod, b_mod=b_mod, cv_w_in=cv_w_in, cv_dw=cv_dw, cv_db=cv_db, cv_ln_g=cv_ln_g, cv_ln_b=cv_ln_b, cv_w_out=cv_w_out, pl_w_in=pl_w_in, pl_w_grp=pl_w_grp, pl_scale=pl_scale, pl_w_out=pl_w_out, ml_w_in=ml_w_in, ml_q_norm=ml_q_norm, ml_kv_norm=ml_kv_norm, ml_w_uq=ml_w_uq, ml_w_ukv=ml_w_ukv, ml_nope_norm=ml_nope_norm, ml_rope_norm=ml_rope_norm, ml_w_out=ml_w_out, ch_w_in=ch_w_in, ch_ln_g=ch_ln_g, ch_ln_b=ch_ln_b, ch_w_s=ch_w_s, ch_b_s=ch_b_s, ch_w_out=ch_w_out)
    M = dict(c_ctx=m_c_ctx, norm_g=m_norm_g, w_mod=m_w_mod, b_mod=m_b_mod, cv_w_in=m_cv_w_in, cv_dw=m_cv_dw, cv_db=m_cv_db, cv_ln_g=m_cv_ln_g, cv_ln_b=m_cv_ln_b, cv_w_out=m_cv_w_out, pl_w_in=m_pl_w_in, pl_w_grp=m_pl_w_grp, pl_scale=m_pl_scale, pl_w_out=m_pl_w_out, ml_w_in=m_ml_w_in, ml_q_norm=m_ml_q_norm, ml_kv_norm=m_ml_kv_norm, ml_w_uq=m_ml_w_uq, ml_w_ukv=m_ml_w_ukv, ml_nope_norm=m_ml_nope_norm, ml_rope_norm=m_ml_rope_norm, ml_w_out=m_ml_w_out, ch_w_in=m_ch_w_in, ch_ln_g=m_ch_ln_g, ch_ln_b=m_ch_ln_b, ch_w_s=m_ch_w_s, ch_b_s=m_ch_b_s, ch_w_out=m_ch_w_out)
    V = dict(c_ctx=v_c_ctx, norm_g=v_norm_g, w_mod=v_w_mod, b_mod=v_b_mod, cv_w_in=v_cv_w_in, cv_dw=v_cv_dw, cv_db=v_cv_db, cv_ln_g=v_cv_ln_g, cv_ln_b=v_cv_ln_b, cv_w_out=v_cv_w_out, pl_w_in=v_pl_w_in, pl_w_grp=v_pl_w_grp, pl_scale=v_pl_scale, pl_w_out=v_pl_w_out, ml_w_in=v_ml_w_in, ml_q_norm=v_ml_q_norm, ml_kv_norm=v_ml_kv_norm, ml_w_uq=v_ml_w_uq, ml_w_ukv=v_ml_w_ukv, ml_nope_norm=v_ml_nope_norm, ml_rope_norm=v_ml_rope_norm, ml_w_out=v_ml_w_out, ch_w_in=v_ch_w_in, ch_ln_g=v_ch_ln_g, ch_ln_b=v_ch_ln_b, ch_w_s=v_ch_w_s, ch_b_s=v_ch_b_s, ch_w_out=v_ch_w_out)

    batch, lat_len, d = x.shape
    mx, my, mc = _my_place()
    chip = 2 * mx + my
    dev = 2 * chip + mc
    core = jnp.reshape(mc, (1,)).astype(jnp.int32)
    zero_off = jnp.zeros((1,), jnp.int32)
    big_names = list(BIG)

    sw = d // N_CHIP
    small_in = [c] + [jnp.pad(W[n].reshape(-1, W[n].shape[-1]), ((0, 0), (0, sw - W[n].shape[-1])))
                      for n in SMALL_SHARDED]
    pack1, spans1 = _pack_rows(small_in, sw)
    got1 = _ag8("ag8_inputs", pack1)
    c_all = got1[:, spans1[0][0]:spans1[0][0] + spans1[0][1]].reshape(N_DEV * batch, d)
    full_small = {}
    for n, (r0, nr, _) in zip(SMALL_SHARDED, spans1[1:]):
        blk = got1[0::2, r0:r0 + nr, :W[n].shape[-1]]
        full_small[n] = blk.transpose(1, 0, 2).reshape(nr, -1)

    c_rows = jnp.concatenate([c_all, c_ctx[None], jnp.zeros((MOD_ROWS - CTX_ROW - 1, d), F32)], axis=0)
    nmod = w_mod.shape[2]
    b_shard = lax.dynamic_slice(b_mod, (0, chip * nmod), (b_mod.shape[0], nmod))[:, None, :]
    mod_shard = _mod_fwd(c_rows, w_mod, b_shard)
    got2 = _ag8("ag8_mod", mod_shard.reshape(-1, nmod))
    mod_full = got2[0::2].reshape(N_CHIP, 4, MOD_ROWS, nmod).transpose(1, 2, 0, 3).reshape(4, MOD_ROWS, 3 * d)
    mod_lat = lax.dynamic_slice(mod_full, (0, dev * batch, 0), (4, batch, 3 * d))
    mod_ctx = mod_full[:, CTX_ROW]
    mods = []
    for i in range(4):
        mods.append(tuple(
            jnp.stack([mod_lat[i, :, j * d:(j + 1) * d], jnp.broadcast_to(mod_ctx[i, j * d:(j + 1) * d], (batch, d))],
                      axis=1)[:, :, None, :] for j in range(3)))

    wk = dict(full_small)
    wk.update(norm_g=norm_g, cv_db=cv_db, cv_ln_g=cv_ln_g, cv_ln_b=cv_ln_b, ml_nope_norm=ml_nope_norm[0],
              ml_rope_norm=ml_rope_norm[0], ch_w_s=ch_w_s[0], ch_b_s=ch_b_s[0])
    plan = _DistPlan(W, M, V, _prep_weights(wk), core)
    xm = jnp.concatenate([x, ctx], axis=1)
    loss_part, grad_x, dmods, g = _local_step(xm, loss_target, mods, plan, lat_len)
    g = _unprep_grads(g)
    out = plan.finish()
    loss = lax.psum(loss_part[0, 0], ("x", "y", "c"))

    lat_rows, ctx_rows = [], []
    for i in range(4):
        dsh, dsc, dgt = dmods[i]
        lat_rows.append(jnp.concatenate([dsh[:, 0, 0], dsc[:, 0, 0], dgt[:, 0, 0]], axis=1))
        zero = jnp.zeros((d,), F32)
        cs = [jnp.sum(t[:, 1, 0], axis=0) if ok else zero
              for t, ok in zip((dsh, dsc, dgt), (i <= 2, i <= 2, i <= 1))]
        ctx_rows.append(jnp.concatenate(cs, axis=0)[None])
    dmod_dev = jnp.concatenate(lat_rows + ctx_rows, axis=0)
    dmod_dev = jnp.pad(dmod_dev, ((0, (-dmod_dev.shape[0]) % 8), (0, 0)))
    got3 = _ag8("ag8_dmod", dmod_dev)
    dlat = got3[:, :4 * batch].reshape(N_DEV, 4, batch, 3 * d).transpose(1, 0, 2, 3).reshape(4, N_DEV * batch, 3 * d)
    dctx_parts = got3[:, 4 * batch:4 * batch + 4].transpose(1, 0, 2)
    g_b_mod, dctx = _mod_bwd_rows(dlat, dctx_parts)
    d_rows = jnp.concatenate([dlat, dctx, jnp.zeros((4, MOD_ROWS - CTX_ROW - 1, 3 * d), F32)], axis=1)
    d_rows = lax.dynamic_slice_in_dim(d_rows, chip * nmod, nmod, axis=2)
    g_w_mod, dcc_part = _mod_bwd_w(c_rows.T, d_rows, w_mod)

    g_small_in = {n: g[n] for n in SMALL_SHARDED}
    g_small_in.update(norm_g=g["norm_g"], cv_db=g["cv_db"], cv_ln_g=g["cv_ln_g"], cv_ln_b=g["cv_ln_b"],
                      ml_nope_norm=g["ml_nope_norm"], ml_rope_norm=g["ml_rope_norm"], ch_w_s=g["ch_w_s"],
                      ch_b_s=g["ch_b_s"],
                      c_ctx=dcc_part.reshape(-1) * (mc == 0).astype(F32))
    small_names = list(g_small_in)
    pack4, spans4 = _pack_rows([g_small_in[n] for n in small_names], LANES, 128)
    summed = _sum8(_ag8("ag8_small_grads", pack4))
    gs = dict(zip(small_names, _unpack_rows(summed, spans4)))
    gs["b_mod"] = g_b_mod[:, 0]
    for n in SMALL_SHARDED:
        wd = W[n].shape[-1]
        gs[n] = lax.dynamic_slice_in_dim(gs[n], chip * wd, wd, axis=1)
    upd_names = SMALL_REPLICATED + SMALL_SHARDED
    pw, spans_u = _pack_rows([W[n] for n in upd_names], LANES, 128)
    pm, _ = _pack_rows([M[n] for n in upd_names], LANES, 128)
    pv, _ = _pack_rows([V[n] for n in upd_names], LANES, 128)
    pg, _ = _pack_rows([gs[n].reshape(W[n].shape) for n in upd_names], LANES, 128)
    res_small = _adamw("adamw_small", zero_off, pg[None], pw, pm, pv, pw.shape[0])
    for n, vals in zip(upd_names, zip(*[_unpack_rows(r, spans_u) for r in res_small])):
        out[n] = vals

    wm2 = w_mod.reshape(-1, nmod)
    res_mod = _adamw("adamw_w_mod", zero_off, g_w_mod.reshape(1, -1, nmod), wm2, M["w_mod"].reshape(-1, nmod),
                     V["w_mod"].reshape(-1, nmod), wm2.shape[0])
    out["w_mod"] = tuple(r.reshape(w_mod.shape) for r in res_mod)

    outs = [loss, grad_x]
    for j in range(4):
        outs.extend(out[n][j] for n in WEIGHTS)
    return tuple(outs)
```

```python
import functools
import math

import jax
import jax.numpy as jnp
from jax import lax
from jax.experimental import pallas as pl
from jax.experimental.pallas import tpu as pltpu

F32 = jnp.float32
BF16 = jnp.bfloat16
MESH = pl.DeviceIdType.MESH

EPS = 1e-6
GRID_W = 64
CONV_WIDTH = 31
CONV_HALF = CONV_WIDTH // 2
CONV_PAD = 16
POOL_WINDOWS = (2, 4, 8, 16)
POOL_HALF = max(POOL_WINDOWS) // 2
HEADS = 8
NOPE = 128
ROPE = 64
HEAD_W = 256
VDIM = 128
KV_RANK = 256
Q_RANK = 384
ATT_SCALE = (NOPE + ROPE) ** -0.5
ROPE_THETA = 10000.0
CHUNK = 128
CHUNK_GROUPS = 8
LANES = 128
TM = 256
TQ = 512
VMEM_LIMIT = 56 * 1024 * 1024

ADAM_LR = 0.001
ADAM_B1 = 0.9
ADAM_B2 = 0.999
ADAM_EPS = 1e-08
ADAM_WD = 0.01
ADAM_STEP = 10


def _dot(a, b):
    return jnp.dot(a.astype(BF16), b.astype(BF16), preferred_element_type=F32)


def _dot_nt(a, b):
    return lax.dot_general(a.astype(BF16), b.astype(BF16), (((1,), (1,)), ((), ())), preferred_element_type=F32)


def _dot_tn(a, b):
    return lax.dot_general(a.astype(BF16), b.astype(BF16), (((0,), (0,)), ((), ())), preferred_element_type=F32)


@jax.custom_vjp
def _mm(a, w):
    return _dot(a, w)


def _mm_fwd(a, w):
    return _dot(a, w), (a, w)


def _mm_bwd(res, ct):
    a, w = res
    return _dot_nt(ct, w), _dot_tn(a, ct)


_mm.defvjp(_mm_fwd, _mm_bwd)


def _swap16_impl(x):
    n = x.shape[-1]
    ax = x.ndim - 1
    lane = lax.broadcasted_iota(jnp.int32, x.shape, ax)
    up = pltpu.roll(x, n - 16, ax)
    dn = pltpu.roll(x, 16, ax)
    return jnp.where((lane % 32) < 16, up, dn)


@jax.custom_vjp
def _swap16(x):
    return _swap16_impl(x)


_swap16.defvjp(lambda x: (_swap16_impl(x), None), lambda _, ct: (_swap16_impl(ct),))


def _rms(x, g, n=None):
    n = x.shape[-1] if n is None else n
    return x * lax.rsqrt(jnp.sum(x * x, axis=-1, keepdims=True) * (1.0 / n) + EPS) * g


def _layernorm(x, g, b):
    mu = jnp.mean(x, axis=-1, keepdims=True)
    xc = x - mu
    var = jnp.mean(xc * xc, axis=-1, keepdims=True)
    return xc * lax.rsqrt(var + EPS) * g + b


def _silu(x):
    return x * jax.nn.sigmoid(x)


def _rope(x, cos, sin):
    return x * cos + _swap16(x) * sin


ANY = pl.BlockSpec(memory_space=pl.ANY)


class _Hosted:
    def __init__(self, arrays, out_shapes, sems, start, wait, aliases=None):
        self.arrays, self.out_shapes, self.sems = list(arrays), list(out_shapes), list(sems)
        self.start, self.wait, self.aliases = start, wait, dict(aliases or {})
        self.results = None


def _merge_hosted(parts):
    parts = [p for p in parts if p is not None]
    if not parts:
        return None
    if len(parts) == 1:
        return parts[0]
    offs, a0, o0, s0 = [], 0, 0, 0
    for p in parts:
        offs.append((a0, o0, s0))
        a0, o0, s0 = a0 + len(p.arrays), o0 + len(p.out_shapes), s0 + len(p.sems)

    def run(which):
        def f(ins, outs, sems):
            for p, (a, o, s) in zip(parts, offs):
                getattr(p, which)(ins[a:a + len(p.arrays)], outs[o:o + len(p.out_shapes)], sems[s:s + len(p.sems)])
        return f

    aliases = {}
    for p, (a, o, _) in zip(parts, offs):
        aliases.update({a + i: o + j for i, j in p.aliases.items()})
    merged = _Hosted(sum((p.arrays for p in parts), []), sum((p.out_shapes for p in parts), []),
                     sum((p.sems for p in parts), []), run("start"), run("wait"), aliases)
    merged.parts, merged.offs = parts, offs
    return merged


def _deliver(hosted, results):
    hosted.results = list(results)
    for p, (_, o, _) in zip(getattr(hosted, "parts", []), getattr(hosted, "offs", [])):
        p.results = list(results[o:o + len(p.out_shapes)])


def _pcall(body, *, name, grid, in_specs, out_specs, out_shape, args, hosted=None, vmem_limit=True):
    n_in, n_out = len(args), len(out_shape)
    kwargs = {}
    if hosted is not None:
        nhi, nho, inner = len(hosted.arrays), len(hosted.out_shapes), body

        def body(*refs):
            ins, hin = refs[:n_in], refs[n_in:n_in + nhi]
            outs, hout = refs[n_in + nhi:n_in + nhi + n_out], refs[n_in + nhi + n_out:n_in + nhi + n_out + nho]
            sems = refs[n_in + nhi + n_out + nho:]
            first, last = None, None
            for k, g in enumerate(grid):
                f, l = pl.program_id(k) == 0, pl.program_id(k) == g - 1
                first = f if first is None else jnp.logical_and(first, f)
                last = l if last is None else jnp.logical_and(last, l)

            @pl.when(first)
            def _():
                hosted.start(hin, hout, sems)

            inner(*ins, *outs)

            @pl.when(last)
            def _():
                hosted.wait(hin, hout, sems)

        in_specs = list(in_specs) + [ANY] * nhi
        out_specs = list(out_specs) + [ANY] * nho
        out_shape = list(out_shape) + hosted.out_shapes
        args = list(args) + hosted.arrays
        kwargs = dict(scratch_shapes=hosted.sems,
                      input_output_aliases={n_in + i: n_out + j for i, j in hosted.aliases.items()})
    params = dict(dimension_semantics=("arbitrary",) * len(grid))
    if vmem_limit:
        params["vmem_limit_bytes"] = VMEM_LIMIT
    res = pl.pallas_call(body, name=name, grid=grid, in_specs=list(in_specs), out_specs=list(out_specs),
                         out_shape=list(out_shape), compiler_params=pltpu.CompilerParams(**params), **kwargs)(*args)
    if hosted is not None:
        _deliver(hosted, res[n_out:])
    return list(res[:n_out])


def _run_hosted(name, hosted):
    nhi, nho = len(hosted.arrays), len(hosted.out_shapes)

    def body(*refs):
        ins, outs, sems = refs[:nhi], refs[nhi:nhi + nho], refs[nhi + nho:]
        hosted.start(ins, outs, sems)
        hosted.wait(ins, outs, sems)

    res = pl.pallas_call(body, name=name, in_specs=[ANY] * nhi, out_specs=[ANY] * nho, out_shape=hosted.out_shapes,
                         scratch_shapes=hosted.sems, input_output_aliases=hosted.aliases)(*hosted.arrays)
    _deliver(hosted, res)
    return list(res)


def _const_spec(shape, single=False):
    nd = len(shape)
    if single:
        return pl.BlockSpec(shape, lambda b, i: (0,) * nd, pipeline_mode=pl.Buffered(1))
    return pl.BlockSpec(shape, lambda b, i: (0,) * nd)


def _tile_spec(arr, n_lat_tiles, lat_only=False):
    bt, _, cw = arr.shape
    if lat_only:
        return pl.BlockSpec((1, TM, cw), lambda b, i: (b if bt > 1 else 0, jnp.minimum(i, n_lat_tiles - 1), 0))
    return pl.BlockSpec((1, TM, cw), lambda b, i: (b if bt > 1 else 0, i, 0))


def _eparam_spec(arr, n_lat_tiles):
    cw = arr.shape[-1]
    return pl.BlockSpec((1, 1, 1, cw), lambda b, i: (b, (i >= n_lat_tiles).astype(jnp.int32), 0, 0))


def _stage_fwd(name, *, pre, post, wsel, splits, tiles, eparams, sparams, weights, out_widths, out_dtypes,
               batch, n_tiles, n_lat_tiles, hosted=None):
    nt, ne, ns, nw = len(tiles), len(eparams), len(sparams), len(weights)

    def body(*refs):
        t_refs = refs[:nt]
        e_refs = refs[nt:nt + ne]
        s_refs = refs[nt + ne:nt + ne + ns]
        w_refs = refs[nt + ne + ns:nt + ne + ns + nw]
        o_refs = refs[nt + ne + ns + nw:]
        tv = [r[0].astype(F32) for r in t_refs]
        ev = [r[0, 0] for r in e_refs]
        sv = [r[...] for r in s_refs]
        a = pre(tv, ev, sv)
        z = [_dot(a[wsel[j]], w_refs[j][...]) for j in range(nw)]
        if post is None:
            outs = [z[j][:, s:s + w] for (j, s, w) in splits]
        else:
            outs = post(z, tv, ev, sv)
        for o_ref, o in zip(o_refs, outs):
            o_ref[0] = o.astype(o_ref.dtype)

    in_specs = ([_tile_spec(t, n_lat_tiles) for t in tiles] + [_eparam_spec(e, n_lat_tiles) for e in eparams]
                + [_const_spec(s.shape) for s in sparams] + [_const_spec(w.shape, single=True) for w in weights])
    out_shape = [jax.ShapeDtypeStruct((batch, n_tiles * TM, w), dt) for w, dt in zip(out_widths, out_dtypes)]
    out_specs = [pl.BlockSpec((1, TM, w), lambda b, i: (b, i, 0)) for w in out_widths]
    return _pcall(body, name=name, grid=(batch, n_tiles), in_specs=in_specs, out_specs=out_specs,
                  out_shape=out_shape, args=[*tiles, *eparams, *sparams, *weights], hosted=hosted)


def _stage_bwd(name, *, pre, post, wsel, splits, tiles, tile_diff, eparams, sparams, weights, cots, cot_lat_only,
               batch, n_tiles, n_lat_tiles, add=None, add_lat_only=False, hosted=None):
    nt, ne, ns, nw, nc = len(tiles), len(eparams), len(sparams), len(weights), len(cots)
    diff_idx = [k for k in range(nt) if tile_diff[k]]
    nd = len(diff_idx)
    has_add = add is not None

    def body(*refs):
        pos = 0
        t_refs = refs[pos:pos + nt]; pos += nt
        e_refs = refs[pos:pos + ne]; pos += ne
        s_refs = refs[pos:pos + ns]; pos += ns
        w_refs = refs[pos:pos + nw]; pos += nw
        c_refs = refs[pos:pos + nc]; pos += nc
        if has_add:
            add_ref = refs[pos]; pos += 1
        dt_refs = refs[pos:pos + nd]; pos += nd
        de_refs = refs[pos:pos + ne]; pos += ne
        ds_refs = refs[pos:pos + ns]; pos += ns
        dw_refs = refs[pos:pos + nw]; pos += nw

        b = pl.program_id(0)
        i = pl.program_id(1)
        is_lat = i < n_lat_tiles
        tv = [r[0].astype(F32) for r in t_refs]
        ev = tuple(r[0, 0] for r in e_refs)
        sv = tuple(r[...] for r in s_refs)
        dv0 = tuple(tv[k] for k in diff_idx)

        def merge(dv):
            full = list(tv)
            for k, v in zip(diff_idx, dv):
                full[k] = v
            return full

        def pre_f(dv, ev_, sv_):
            return tuple(pre(merge(dv), list(ev_), list(sv_)))

        a, vjp_pre = jax.vjp(pre_f, dv0, ev, sv)
        cv = []
        for c_ref, lat in zip(c_refs, cot_lat_only):
            c = c_ref[0].astype(F32)
            cv.append(jnp.where(is_lat, c, 0.0) if lat else c)
        if post is None:
            dz = []
            for j in range(nw):
                parts = [cv[k] for k, (jj, _, _) in enumerate(splits) if jj == j]
                dz.append(parts[0] if len(parts) == 1 else jnp.concatenate(parts, axis=1))
            dt2 = de2 = ds2 = None
        else:
            z = tuple(_dot(a[wsel[j]], w_refs[j][...]) for j in range(nw))

            def post_f(z_, dv, ev_, sv_):
                return tuple(post(list(z_), merge(dv), list(ev_), list(sv_)))

            _, vjp_post = jax.vjp(post_f, z, dv0, ev, sv)
            dz, dt2, de2, ds2 = vjp_post(tuple(cv))
        da = [None] * len(a)
        dws = []
        for j in range(nw):
            g = _dot_nt(dz[j], w_refs[j][...])
            da[wsel[j]] = g if da[wsel[j]] is None else da[wsel[j]] + g
            dws.append(_dot_tn(a[wsel[j]], dz[j]))
        da = tuple(jnp.zeros_like(a[k]) if da[k] is None else da[k] for k in range(len(a)))
        dt1, de1, ds1 = vjp_pre(da)

        def plus(u, v):
            return u if v is None else u + v

        for k in range(nd):
            val = plus(dt1[k], None if dt2 is None else dt2[k])
            if has_add and k == 0:
                addv = add_ref[0].astype(F32)
                val = val + (jnp.where(is_lat, addv, 0.0) if add_lat_only else addv)
            dt_refs[k][0] = val.astype(dt_refs[k].dtype)

        seg_first = jnp.logical_or(i == 0, i == n_lat_tiles)
        for k in range(ne):
            val = plus(de1[k], None if de2 is None else de2[k])

            @pl.when(seg_first)
            def _(k=k, val=val):
                de_refs[k][0, 0] = val

            @pl.when(jnp.logical_not(seg_first))
            def _(k=k, val=val):
                de_refs[k][0, 0] += val

        first = jnp.logical_and(b == 0, i == 0)
        acc = [(ds_refs[k], plus(ds1[k], None if ds2 is None else ds2[k])) for k in range(ns)]
        acc += [(dw_refs[j], dws[j]) for j in range(nw)]
        for ref, val in acc:
            @pl.when(first)
            def _(ref=ref, val=val):
                ref[...] = val

            @pl.when(jnp.logical_not(first))
            def _(ref=ref, val=val):
                ref[...] += val

    in_specs = ([_tile_spec(t, n_lat_tiles) for t in tiles] + [_eparam_spec(e, n_lat_tiles) for e in eparams]
                + [_const_spec(s.shape) for s in sparams] + [_const_spec(w.shape, single=True) for w in weights]
                + [_tile_spec(c, n_lat_tiles, lat) for c, lat in zip(cots, cot_lat_only)])
    args = [*tiles, *eparams, *sparams, *weights, *cots]
    if has_add:
        in_specs.append(_tile_spec(add, n_lat_tiles, add_lat_only))
        args.append(add)
    out_shape = [jax.ShapeDtypeStruct((batch, n_tiles * TM, tiles[k].shape[-1]), F32) for k in diff_idx]
    out_specs = [pl.BlockSpec((1, TM, tiles[k].shape[-1]), lambda b, i: (b, i, 0)) for k in diff_idx]
    out_shape += [jax.ShapeDtypeStruct(e.shape, F32) for e in eparams]
    out_specs += [_eparam_spec(e, n_lat_tiles) for e in eparams]
    out_shape += [jax.ShapeDtypeStruct(s.shape, F32) for s in sparams]
    out_specs += [_const_spec(s.shape) for s in sparams]
    out_shape += [jax.ShapeDtypeStruct(w.shape, F32) for w in weights]
    out_specs += [_const_spec(w.shape, single=True) for w in weights]
    res = _pcall(body, name=name, grid=(batch, n_tiles), in_specs=in_specs, out_specs=out_specs,
                 out_shape=out_shape, args=args, hosted=hosted)
    return res[:nd], res[nd:nd + ne], res[nd + ne:nd + ne + ns], res[nd + ne + ns:]


def _pre_adaln(tv, ev, sv):
    x = tv[0]
    sh, sc = ev[0], ev[1]
    return [_rms(x, sv[0]) * (1.0 + sc) + sh]


def _post_residual(x_index):
    def post(z, tv, ev, sv):
        return [tv[x_index] + ev[-1] * z[0]]
    return post


def _pre_conv_out(tv, ev, sv):
    c1, gg = tv[0], tv[1]
    return [_silu(_layernorm(c1, sv[0], sv[1])) * _silu(gg)]


def _pre_pool_out(tv, ev, sv):
    pooled, gg = tv[0], tv[1]
    w_grp, scale = sv[0], sv[1]
    gw = w_grp.shape[-1]
    y = jnp.concatenate([_mm(pooled[:, k * gw:(k + 1) * gw], w_grp[k]) for k in range(w_grp.shape[0])], axis=1)
    return [y * scale * _silu(gg)]


def _pre_rms_only(tv, ev, sv):
    return [_rms(tv[0], sv[0])]


def _post_mla_keys(z, tv, ev, sv):
    krp, cos, sin = tv[1], tv[2], tv[3]
    nope_g, rope_g = sv[1], sv[2]
    kv = z[0]
    kr = _rope(_rms(krp, rope_g, ROPE), cos, sin)
    ks, vs = [], []
    for h in range(HEADS):
        ks.append(_rms(kv[:, h * 2 * NOPE:h * 2 * NOPE + NOPE], nope_g))
        ks.append(kr)
        vs.append(kv[:, h * 2 * NOPE + NOPE:(h + 1) * 2 * NOPE])
    return [jnp.concatenate(ks, axis=1), jnp.concatenate(vs, axis=1)]


def _post_mla_queries(z, tv, ev, sv):
    cos, sin = tv[1], tv[2]
    nope_g, rope_g = sv[1], sv[2]
    q = z[0]
    qs = []
    for h in range(HEADS):
        qs.append(_rms(q[:, h * HEAD_W:h * HEAD_W + NOPE], nope_g))
        qs.append(_rope(_rms(q[:, h * HEAD_W + NOPE:(h + 1) * HEAD_W], rope_g, ROPE), cos, sin))
    return [jnp.concatenate(qs, axis=1)]


def _pre_mla_out(tv, ev, sv):
    return [tv[0] * _silu(tv[1])]


def _pre_chunk_out(tv, ev, sv):
    u, v, gg = tv[0], tv[1], tv[2]
    ln_g, ln_b, w_s, b_s = sv
    vn = _layernorm(v, ln_g, ln_b)
    rows = []
    for n in range(vn.shape[0] // CHUNK):
        blk = vn[n * CHUNK:(n + 1) * CHUNK]
        cols = [_mm(w_s[g], blk[:, g * LANES:(g + 1) * LANES]) + b_s[:, g:g + 1] for g in range(CHUNK_GROUPS)]
        rows.append(jnp.concatenate(cols, axis=1))
    s = jnp.concatenate(rows, axis=0)
    return [u * s * _silu(gg)]


def _segments(lat_len, tot_len):
    segs = [(0, lat_len)]
    if tot_len > lat_len:
        segs.append((lat_len, tot_len - lat_len))
    return segs


def _shifted(x, j):
    if j == 0:
        return x
    n = x.shape[0]
    rows = lax.broadcasted_iota(jnp.int32, x.shape, 0)
    r = pltpu.roll(x, (-j) % n, 0)
    return jnp.where(jnp.logical_and(rows + j >= 0, rows + j < n), r, 0.0)


def _conv_fwd(a, bgate, dw, db, lat_len, hosted=None):
    batch, tot, e = a.shape
    segs = _segments(lat_len, tot)

    def body(a_ref, b_ref, dw_ref, db_ref, o_ref):
        w = dw_ref[...]
        for (s0, n) in segs:
            y = a_ref[0, s0:s0 + n, :] * jax.nn.sigmoid(b_ref[0, s0:s0 + n, :])
            acc = jnp.zeros_like(y) + db_ref[...]
            for k in range(CONV_WIDTH):
                acc = acc + _shifted(y, k - CONV_HALF) * w[k:k + 1, :]
            o_ref[0, s0:s0 + n, :] = acc

    blk = pl.BlockSpec((1, tot, LANES), lambda b, cb: (b, 0, cb))
    return _pcall(
        body, name="conv_fwd", grid=(batch, e // LANES),
        in_specs=[blk, blk, pl.BlockSpec((CONV_WIDTH, LANES), lambda b, cb: (0, cb)),
                  pl.BlockSpec((1, LANES), lambda b, cb: (0, cb))],
        out_specs=[blk], out_shape=[jax.ShapeDtypeStruct(a.shape, F32)], args=[a, bgate, dw, db], hosted=hosted)[0]


def _conv_bwd(a, bgate, dw, dc1, lat_len, hosted=None):
    batch, tot, e = a.shape
    segs = _segments(lat_len, tot)

    def body(a_ref, b_ref, dw_ref, dc_ref, da_ref, dg_ref, ddw_ref, ddb_ref):
        b = pl.program_id(1)
        w = dw_ref[...]
        ddw_rows = [None] * CONV_WIDTH
        ddb = None
        for (s0, n) in segs:
            av = a_ref[0, s0:s0 + n, :]
            sg = jax.nn.sigmoid(b_ref[0, s0:s0 + n, :])
            y = av * sg
            dc = dc_ref[0, s0:s0 + n, :]
            dy = jnp.zeros_like(y)
            for k in range(CONV_WIDTH):
                j = k - CONV_HALF
                dy = dy + _shifted(dc, -j) * w[k:k + 1, :]
                r = jnp.sum(dc * _shifted(y, j), axis=0, keepdims=True)
                ddw_rows[k] = r if ddw_rows[k] is None else ddw_rows[k] + r
            r = jnp.sum(dc, axis=0, keepdims=True)
            ddb = r if ddb is None else ddb + r
            da_ref[0, s0:s0 + n, :] = dy * sg
            dg_ref[0, s0:s0 + n, :] = dy * av * sg * (1.0 - sg)

        @pl.when(b == 0)
        def _():
            ddw_ref[...] = jnp.zeros_like(ddw_ref)
            ddb_ref[...] = jnp.zeros_like(ddb_ref)

        for k in range(CONV_WIDTH):
            ddw_ref[k:k + 1, :] += ddw_rows[k]
        ddb_ref[...] += ddb

    blk = pl.BlockSpec((1, tot, LANES), lambda cb, b: (b, 0, cb))
    wspec = pl.BlockSpec((CONV_WIDTH, LANES), lambda cb, b: (0, cb))
    bspec = pl.BlockSpec((1, LANES), lambda cb, b: (0, cb))
    return _pcall(
        body, name="conv_bwd", grid=(e // LANES, batch),
        in_specs=[blk, blk, wspec, blk],
        out_specs=[blk, blk, wspec, bspec],
        out_shape=[jax.ShapeDtypeStruct(a.shape, F32), jax.ShapeDtypeStruct(a.shape, F32),
                   jax.ShapeDtypeStruct((CONV_WIDTH, e), F32), jax.ShapeDtypeStruct((1, e), F32)],
        args=[a, bgate, dw, dc1], hosted=hosted)


def _pool_taps(group):
    half = lax.shift_left(jnp.int32(1), group)
    taps = []
    for j in range(-POOL_HALF, POOL_HALF):
        inside = jnp.logical_and(j >= -half, j < half)
        taps.append(jnp.where(inside, 1.0, 0.0).astype(F32))
    return taps, half


def _pool_counts(n, half, shape):
    t = lax.broadcasted_iota(jnp.int32, shape, 0)
    cnt = jnp.minimum(t + half, n) - jnp.maximum(t - half, 0)
    return cnt.astype(F32)


def _pool_fwd(v, lat_len, hosted=None):
    batch, tot, e = v.shape
    gw = e // len(POOL_WINDOWS)
    segs = _segments(lat_len, tot)

    def body(v_ref, o_ref):
        taps, half = _pool_taps(pl.program_id(1))
        for (s0, n) in segs:
            x = v_ref[0, s0:s0 + n, :]
            acc = jnp.zeros_like(x)
            for idx, j in enumerate(range(-POOL_HALF, POOL_HALF)):
                acc = acc + _shifted(x, j) * taps[idx]
            o_ref[0, s0:s0 + n, :] = acc / _pool_counts(n, half, x.shape) - x

    blk = pl.BlockSpec((1, tot, gw), lambda b, g: (b, 0, g))
    return _pcall(body, name="pool_fwd", grid=(batch, len(POOL_WINDOWS)), in_specs=[blk], out_specs=[blk],
                  out_shape=[jax.ShapeDtypeStruct(v.shape, F32)], args=[v], hosted=hosted)[0]


def _pool_bwd(dp, lat_len):
    batch, tot, e = dp.shape
    gw = e // len(POOL_WINDOWS)
    segs = _segments(lat_len, tot)

    def body(d_ref, o_ref):
        taps, half = _pool_taps(pl.program_id(1))
        for (s0, n) in segs:
            d = d_ref[0, s0:s0 + n, :]
            dn = d / _pool_counts(n, half, d.shape)
            acc = jnp.zeros_like(d)
            for idx, j in enumerate(range(-POOL_HALF, POOL_HALF)):
                acc = acc + _shifted(dn, -j) * taps[idx]
            o_ref[0, s0:s0 + n, :] = acc - d

    blk = pl.BlockSpec((1, tot, gw), lambda b, g: (b, 0, g))
    return pl.pallas_call(
        body, name="pool_bwd", grid=(batch, len(POOL_WINDOWS)), in_specs=[blk], out_specs=blk,
        out_shape=jax.ShapeDtypeStruct(dp.shape, F32),
        compiler_params=pltpu.CompilerParams(dimension_semantics=("arbitrary", "arbitrary"),
                                             vmem_limit_bytes=VMEM_LIMIT),
    )(dp)


def _softmax_rows(q, k):
    s = _dot_nt(q, k) * ATT_SCALE
    m = jnp.max(s, axis=-1, keepdims=True)
    e = jnp.exp(s - m)
    return e / jnp.sum(e, axis=-1, keepdims=True)


def _attn_fwd(q, k, v, hosted=None):
    batch, lq, _ = q.shape
    tk = k.shape[1]
    tq = min(TQ, lq)

    def body(q_ref, k_ref, v_ref, o_ref):
        p = _softmax_rows(q_ref[0], k_ref[0])
        o_ref[0] = _dot(p, v_ref[0])

    return _pcall(
        body, name="attn_fwd", grid=(batch, HEADS, lq // tq),
        in_specs=[pl.BlockSpec((1, tq, HEAD_W), lambda b, h, i: (b, i, h)),
                  pl.BlockSpec((1, tk, HEAD_W), lambda b, h, i: (b, 0, h)),
                  pl.BlockSpec((1, tk, VDIM), lambda b, h, i: (b, 0, h))],
        out_specs=[pl.BlockSpec((1, tq, VDIM), lambda b, h, i: (b, i, h))],
        out_shape=[jax.ShapeDtypeStruct((batch, lq, HEADS * VDIM), F32)], args=[q, k, v], hosted=hosted)[0]


def _attn_bwd(q, k, v, do, hosted=None):
    batch, lq, _ = q.shape
    tk = k.shape[1]

    def body(q_ref, k_ref, v_ref, do_ref, dq_ref, dk_ref, dv_ref):
        i = pl.program_id(2)
        qv, kv, vv, dov = q_ref[0], k_ref[0], v_ref[0], do_ref[0]
        p = _softmax_rows(qv, kv)
        dp = _dot_nt(dov, vv)
        ds = p * (dp - jnp.sum(p * dp, axis=-1, keepdims=True)) * ATT_SCALE
        dq_ref[0] = _dot(ds, kv)
        dk = _dot_tn(ds, qv)
        dv = _dot_tn(p, dov)

        @pl.when(i == 0)
        def _():
            dk_ref[0] = dk
            dv_ref[0] = dv

        @pl.when(i != 0)
        def _():
            dk_ref[0] += dk
            dv_ref[0] += dv

    tq = min(TQ, lq)
    return _pcall(
        body, name="attn_bwd", grid=(batch, HEADS, lq // tq),
        in_specs=[pl.BlockSpec((1, tq, HEAD_W), lambda b, h, i: (b, i, h)),
                  pl.BlockSpec((1, tk, HEAD_W), lambda b, h, i: (b, 0, h)),
                  pl.BlockSpec((1, tk, VDIM), lambda b, h, i: (b, 0, h)),
                  pl.BlockSpec((1, tq, VDIM), lambda b, h, i: (b, i, h))],
        out_specs=[pl.BlockSpec((1, tq, HEAD_W), lambda b, h, i: (b, i, h)),
                   pl.BlockSpec((1, tk, HEAD_W), lambda b, h, i: (b, 0, h)),
                   pl.BlockSpec((1, tk, VDIM), lambda b, h, i: (b, 0, h))],
        out_shape=[jax.ShapeDtypeStruct(q.shape, F32), jax.ShapeDtypeStruct(k.shape, F32),
                   jax.ShapeDtypeStruct(v.shape, F32)],
        args=[q, k, v, do], hosted=hosted)


def _loss_kernel(y, target):
    batch, lq, d = y.shape

    def body(y_ref, t_ref, l_ref, dy_ref):
        first = jnp.logical_and(pl.program_id(0) == 0, pl.program_id(1) == 0)
        err = y_ref[0] - t_ref[0]
        dy_ref[0] = err * (1.0 / d)
        part = jnp.zeros((1, LANES), F32) + jnp.sum(err * err) * (0.5 / d)

        @pl.when(first)
        def _():
            l_ref[...] = part

        @pl.when(jnp.logical_not(first))
        def _():
            l_ref[...] += part

    blk = pl.BlockSpec((1, TM, d), lambda b, i: (b, i, 0))
    return pl.pallas_call(
        body, name="loss_head", grid=(batch, lq // TM), in_specs=[blk, blk],
        out_specs=[pl.BlockSpec((1, LANES), lambda b, i: (0, 0)), blk],
        out_shape=[jax.ShapeDtypeStruct((1, LANES), F32), jax.ShapeDtypeStruct(y.shape, F32)],
        compiler_params=pltpu.CompilerParams(dimension_semantics=("arbitrary", "arbitrary")),
    )(y, target)


def _rope_tables(lat_len, ctx_len):
    rows = lat_len // GRID_W
    row_id = jnp.repeat(jnp.arange(rows), GRID_W).astype(F32)
    col_id = jnp.tile(jnp.arange(GRID_W), rows).astype(F32)
    axis_dim = ROPE // 2
    freqs = ROPE_THETA ** (-jnp.arange(0, axis_dim, 2, dtype=F32) / axis_dim)
    ar = row_id[:, None] * freqs
    ac = col_id[:, None] * freqs
    cr, sr, cc, sc = jnp.cos(ar), jnp.sin(ar), jnp.cos(ac), jnp.sin(ac)
    pad = jnp.zeros((lat_len, LANES - ROPE), F32)
    cos = jnp.concatenate([cr, cr, cc, cc, pad], axis=1)
    sin = jnp.concatenate([-sr, sr, -sc, sc, pad], axis=1)
    ident = jnp.concatenate([jnp.ones((ctx_len, ROPE), F32), jnp.zeros((ctx_len, LANES - ROPE), F32)], axis=1)
    cos = jnp.concatenate([cos, ident], axis=0)
    sin = jnp.concatenate([sin, jnp.zeros((ctx_len, LANES), F32)], axis=0)
    return cos[None], sin[None]


def _prep_weights(w):
    p = dict(w)
    kvc = KV_RANK + ROPE
    if "ml_w_in" in w:
        wi = w["ml_w_in"]
        p["ml_w_in"] = jnp.concatenate(
            [wi[:, :kvc], jnp.zeros((wi.shape[0], LANES - ROPE), wi.dtype), wi[:, kvc:]], axis=1)
    if "ml_w_uq" in w:
        uq = w["ml_w_uq"].reshape(Q_RANK, HEADS, NOPE + ROPE)
        p["ml_w_uq"] = jnp.pad(uq, ((0, 0), (0, 0), (0, HEAD_W - NOPE - ROPE))).reshape(Q_RANK, HEADS * HEAD_W)
    if "ml_rope_norm" in w:
        p["ml_rope_norm"] = jnp.pad(w["ml_rope_norm"], ((0, 0), (0, LANES - ROPE)))
    return p


def _unprep_grads(g):
    out = dict(g)
    kvc = KV_RANK + ROPE
    if "ml_w_in" in g:
        wi = g["ml_w_in"]
        out["ml_w_in"] = jnp.concatenate([wi[:, :kvc], wi[:, kvc + LANES - ROPE:]], axis=1)
    if "ml_w_uq" in g:
        uq = g["ml_w_uq"].reshape(Q_RANK, HEADS, HEAD_W)
        out["ml_w_uq"] = uq[:, :, :NOPE + ROPE].reshape(Q_RANK, HEADS * (NOPE + ROPE))
    if "ml_rope_norm" in g:
        out["ml_rope_norm"] = g["ml_rope_norm"][:, :ROPE]
    return out


LAYER_WEIGHTS = (("cv_w_in", "cv_w_out"), ("pl_w_in", "pl_w_grp", "pl_w_out"),
                 ("ml_w_in", "ml_w_uq", "ml_w_ukv", "ml_w_out"), ("ch_w_in", "ch_w_out"))


class _LocalPlan:
    def __init__(self, w):
        self.small = w
        self.grads = {}

    def weights(self, layer):
        return {n: self.small[n] for n in LAYER_WEIGHTS[layer]}

    def hosted(self, tag):
        return None

    def after(self, tag):
        pass

    def note(self, values):
        pass

    def layer_grads(self, layer, grads):
        self.grads.update(grads)


def _local_step(xm, target, mods, plan, lat_len):
    batch, tot, d = xm.shape
    e = d
    n_all, n_lat = tot // TM, lat_len // TM
    cos, sin = _rope_tables(lat_len, tot - lat_len)
    g = {}
    w = dict(plan.small)

    def hosting(tag, fn, *args, **kwargs):
        out = fn(*args, hosted=plan.hosted(tag), **kwargs)
        plan.after(tag)
        return out

    def s1_splits(widths):
        out, s = [], 0
        for wd in widths:
            out.append((0, s, wd))
            s += wd
        return out

    def fwd_in(name, x, mod, gi, wname, widths, n_tiles):
        return hosting(name, _stage_fwd, name, pre=_pre_adaln, post=None, wsel=[0], splits=s1_splits(widths),
                       tiles=[x], eparams=[mod[0], mod[1]], sparams=[w["norm_g"][gi:gi + 1]], weights=[w[wname]],
                       out_widths=widths, out_dtypes=[F32] * len(widths), batch=batch, n_tiles=n_tiles,
                       n_lat_tiles=n_lat)

    def bwd_in(name, x, mod, gi, wname, widths, n_tiles, cots, lat_only, add, add_lat_only):
        (dx,), (dsh, dsc), (dg,), (dw,) = hosting(
            name, _stage_bwd, name, pre=_pre_adaln, post=None, wsel=[0], splits=s1_splits(widths), tiles=[x],
            tile_diff=[True], eparams=[mod[0], mod[1]], sparams=[w["norm_g"][gi:gi + 1]], weights=[w[wname]],
            cots=cots, cot_lat_only=lat_only, batch=batch, n_tiles=n_tiles, n_lat_tiles=n_lat, add=add,
            add_lat_only=add_lat_only)
        return dx, dsh, dsc, dg, dw

    def fwd_out(name, pre, tiles, mod, sparams, wname, n_tiles):
        return hosting(name, _stage_fwd, name, pre=pre, post=_post_residual(len(tiles) - 1), wsel=[0], splits=None,
                       tiles=tiles, eparams=[mod[2]], sparams=sparams, weights=[w[wname]], out_widths=[d],
                       out_dtypes=[F32], batch=batch, n_tiles=n_tiles, n_lat_tiles=n_lat)[0]

    def bwd_out(name, pre, tiles, mod, sparams, wname, n_tiles, cot):
        diff = [True] * (len(tiles) - 1) + [False]
        dts, (dgt,), dss, (dw,) = hosting(
            name, _stage_bwd, name, pre=pre, post=_post_residual(len(tiles) - 1), wsel=[0], splits=None, tiles=tiles,
            tile_diff=diff, eparams=[mod[2]], sparams=sparams, weights=[w[wname]], cots=[cot], cot_lat_only=[False],
            batch=batch, n_tiles=n_tiles, n_lat_tiles=n_lat)
        return dts, dgt, dss, dw

    w.update(plan.weights(0))
    cv_s = [w["cv_ln_g"], w["cv_ln_b"]]
    a0, b0, g0 = fwd_in("cv_in_fwd", xm, mods[0], 0, "cv_w_in", [e, e, e], n_all)
    c1 = hosting("conv_fwd", _conv_fwd, a0, b0, w["cv_dw"], w["cv_db"], lat_len)
    x1 = fwd_out("cv_out_fwd", _pre_conv_out, [c1, g0, xm], mods[0], cv_s, "cv_w_out", n_all)

    w.update(plan.weights(1))
    pl_s = [w["pl_w_grp"], w["pl_scale"]]
    v1, g1 = fwd_in("pl_in_fwd", x1, mods[1], 1, "pl_w_in", [e, e], n_all)
    pooled = hosting("pool_fwd", _pool_fwd, v1, lat_len)
    x2 = fwd_out("pl_out_fwd", _pre_pool_out, [pooled, g1, x1], mods[1], pl_s, "pl_w_out", n_all)

    w.update(plan.weights(2))
    ml_widths = [KV_RANK, LANES, Q_RANK, HEADS * VDIM]
    ckv, krp, cq, g2 = fwd_in("ml_in_fwd", x2, mods[2], 2, "ml_w_in", ml_widths, n_all)
    k_s = [w["ml_kv_norm"], w["ml_nope_norm"][1:2], w["ml_rope_norm"][1:2]]
    q_s = [w["ml_q_norm"], w["ml_nope_norm"][0:1], w["ml_rope_norm"][0:1]]
    kk, vv = hosting("ml_keys_fwd", _stage_fwd, "ml_keys_fwd", pre=_pre_rms_only, post=_post_mla_keys, wsel=[0],
                     splits=None, tiles=[ckv, krp, cos, sin], eparams=[], sparams=k_s, weights=[w["ml_w_ukv"]],
                     out_widths=[HEADS * HEAD_W, HEADS * VDIM], out_dtypes=[BF16, BF16], batch=batch,
                     n_tiles=n_all, n_lat_tiles=n_lat)
    (qq,) = _stage_fwd("ml_queries_fwd", pre=_pre_rms_only, post=_post_mla_queries, wsel=[0], splits=None,
                       tiles=[cq, cos, sin], eparams=[], sparams=q_s, weights=[w["ml_w_uq"]],
                       out_widths=[HEADS * HEAD_W], out_dtypes=[BF16], batch=batch, n_tiles=n_lat,
                       n_lat_tiles=n_lat)
    att = hosting("attn_fwd", _attn_fwd, qq, kk, vv)
    x3 = fwd_out("ml_out_fwd", _pre_mla_out, [att, g2, x2], mods[2], [], "ml_w_out", n_lat)

    w.update(plan.weights(3))
    ch_s = [w["ch_ln_g"], w["ch_ln_b"], w["ch_w_s"], w["ch_b_s"]]
    u3, v3, g3 = fwd_in("ch_in_fwd", x3, mods[3], 3, "ch_w_in", [e, e, e], n_lat)
    x4 = fwd_out("ch_out_fwd", _pre_chunk_out, [u3, v3, g3, x3], mods[3], ch_s, "ch_w_out", n_lat)

    loss_part, dy = _loss_kernel(x4, target)

    dmods = [None] * 4
    dnorm = [None] * 4
    big = {}
    (du, dv, dg), dgt, (g["ch_ln_g"], g["ch_ln_b"], g["ch_w_s"], g["ch_b_s"]), big["ch_w_out"] = bwd_out(
        "ch_out_bwd", _pre_chunk_out, [u3, v3, g3, x3], mods[3], ch_s, "ch_w_out", n_lat, dy)
    plan.note({n: g[n] for n in ("ch_ln_g", "ch_ln_b", "ch_w_s", "ch_b_s")})
    dx3, dsh, dsc, dnorm[3], big["ch_w_in"] = bwd_in("ch_in_bwd", x3, mods[3], 3, "ch_w_in", [e, e, e], n_lat,
                                                     [du, dv, dg], [False] * 3, dy, False)
    dmods[3] = (dsh, dsc, dgt)
    plan.layer_grads(3, big)

    big = {}
    (datt, dg), dgt, _, big["ml_w_out"] = bwd_out("ml_out_bwd", _pre_mla_out, [att, g2, x2], mods[2], [],
                                                  "ml_w_out", n_lat, dx3)
    dq, dk, dvv = hosting("attn_bwd", _attn_bwd, qq, kk, vv, datt)
    (dcq,), _, (g["ml_q_norm"], dnope0, drope0), (big["ml_w_uq"],) = hosting(
        "ml_queries_bwd", _stage_bwd, "ml_queries_bwd", pre=_pre_rms_only, post=_post_mla_queries, wsel=[0],
        splits=None, tiles=[cq, cos, sin], tile_diff=[True, False, False], eparams=[], sparams=q_s,
        weights=[w["ml_w_uq"]], cots=[dq], cot_lat_only=[False], batch=batch, n_tiles=n_lat, n_lat_tiles=n_lat)
    (dckv, dkrp), _, (g["ml_kv_norm"], dnope1, drope1), (big["ml_w_ukv"],) = hosting(
        "ml_keys_bwd", _stage_bwd, "ml_keys_bwd", pre=_pre_rms_only, post=_post_mla_keys, wsel=[0], splits=None,
        tiles=[ckv, krp, cos, sin], tile_diff=[True, True, False, False], eparams=[], sparams=k_s,
        weights=[w["ml_w_ukv"]], cots=[dk, dvv], cot_lat_only=[False, False], batch=batch, n_tiles=n_all,
        n_lat_tiles=n_lat)
    g["ml_nope_norm"] = jnp.concatenate([dnope0, dnope1], axis=0)
    g["ml_rope_norm"] = jnp.concatenate([drope0, drope1], axis=0)
    dx2, dsh, dsc, dnorm[2], big["ml_w_in"] = bwd_in("ml_in_bwd", x2, mods[2], 2, "ml_w_in", ml_widths, n_all,
                                                     [dckv, dkrp, dcq, dg], [False, False, True, True], dx3, True)
    dmods[2] = (dsh, dsc, dgt)
    plan.layer_grads(2, big)

    big = {}
    (dpooled, dg), dgt, (big["pl_w_grp"], g["pl_scale"]), big["pl_w_out"] = bwd_out(
        "pl_out_bwd", _pre_pool_out, [pooled, g1, x1], mods[1], pl_s, "pl_w_out", n_all, dx2)
    dv1 = _pool_bwd(dpooled, lat_len)
    dx1, dsh, dsc, dnorm[1], big["pl_w_in"] = bwd_in("pl_in_bwd", x1, mods[1], 1, "pl_w_in", [e, e], n_all,
                                                     [dv1, dg], [False] * 2, dx2, False)
    dmods[1] = (dsh, dsc, dgt)
    plan.layer_grads(1, big)

    big = {}
    (dc1, dg), dgt, (g["cv_ln_g"], g["cv_ln_b"]), big["cv_w_out"] = bwd_out(
        "cv_out_bwd", _pre_conv_out, [c1, g0, xm], mods[0], cv_s, "cv_w_out", n_all, dx1)
    da, db, g["cv_dw"], g["cv_db"] = hosting("conv_bwd", _conv_bwd, a0, b0, w["cv_dw"], dc1, lat_len)
    dx0, dsh, dsc, dnorm[0], big["cv_w_in"] = bwd_in("cv_in_bwd", xm, mods[0], 0, "cv_w_in", [e, e, e], n_all,
                                                     [da, db, dg], [False] * 3, dx1, False)
    dmods[0] = (dsh, dsc, dgt)
    plan.layer_grads(0, big)
    g["norm_g"] = jnp.concatenate(dnorm, axis=0)
    return loss_part, dx0[:, :lat_len], dmods, g


N_DEV = 8
N_CHIP = 4
ANY = pl.BlockSpec(memory_space=pl.ANY)


def _my_place():
    return lax.axis_index("x"), lax.axis_index("y"), lax.axis_index("c")


def _flip(v, f):
    return 1 - v if f else v


def _ag8_copies(x):
    def plan(ins, outs, sems):
        mx, my, mc = _my_place()
        me = 4 * mx + 2 * my + mc
        sends, recvs = [], []
        for rel in range(1, N_DEV):
            peer = (_flip(mx, rel & 4), _flip(my, rel & 2), _flip(mc, rel & 1))
            src_dev = 4 * peer[0] + 2 * peer[1] + peer[2]
            sends.append(_remote(ins[0], outs[0].at[me], sems, rel - 1, peer))
            recvs.append(_remote(ins[0], outs[0].at[src_dev], sems, rel - 1, peer))
        return sends, recvs, [pltpu.make_async_copy(ins[0], outs[0].at[me], sems[2].at[0])]

    return _copies_hosted([x], [jax.ShapeDtypeStruct((N_DEV,) + x.shape, x.dtype)], (N_DEV - 1, N_DEV - 1, 1), plan)


def _ag8(name, x):
    return _run_hosted(name, _ag8_copies(x))[0]


def _chip_peers(mx, my, mc):
    out = []
    for rel in range(1, N_CHIP):
        px, py = _flip(mx, rel & 2), _flip(my, rel & 1)
        out.append((rel - 1, (px, py, mc), 2 * px + py))
    return out


def _half(mc, rows):
    return pl.ds(pl.multiple_of(mc * (rows // 2), 8), rows // 2)


def _copies_hosted(arrays, out_shapes, n_sems, plan, aliases=None):
    def start(ins, outs, sems):
        sends, _, locals_ = plan(ins, outs, sems)
        for cp in locals_ + sends:
            cp.start()

    def wait(ins, outs, sems):
        sends, recvs, locals_ = plan(ins, outs, sems)
        for cp in recvs:
            cp.wait_recv()
        for cp in sends:
            cp.wait_send()
        for cp in locals_:
            cp.wait()

    return _Hosted(arrays, out_shapes, [pltpu.SemaphoreType.DMA((k,)) for k in n_sems], start, wait, aliases)


def _remote(src, dst, sems, k, peer):
    return pltpu.make_async_remote_copy(src_ref=src, dst_ref=dst, send_sem=sems[0].at[k], recv_sem=sems[1].at[k],
                                        device_id=peer, device_id_type=MESH)


def _gather_ici(shards):
    n = len(shards)

    def plan(ins, outs, sems):
        mx, my, mc = _my_place()
        chip = 2 * mx + my
        sends, recvs, locals_ = [], [], []
        for a in range(n):
            rows = ins[a].shape[0]
            locals_.append(pltpu.make_async_copy(ins[a], outs[a].at[chip], sems[2].at[a]))
            for k, peer, pchip in _chip_peers(mx, my, mc):
                src = ins[a].at[_half(mc, rows)]
                sends.append(_remote(src, outs[a].at[chip, _half(mc, rows)], sems, 3 * a + k, peer))
                recvs.append(_remote(src, outs[a].at[pchip, _half(mc, rows)], sems, 3 * a + k, peer))
        return sends, recvs, locals_

    return _copies_hosted(shards, [jax.ShapeDtypeStruct((N_CHIP,) + s.shape, s.dtype) for s in shards],
                          (3 * n, 3 * n, n), plan)


def _sibling_fill(arrays, row_axis, chips_only_other):
    n = len(arrays)
    per = 3 if chips_only_other else 1

    def plan(ins, outs, sems):
        mx, my, mc = _my_place()
        sibling = (mx, my, 1 - mc)

        def views(a, core):
            rows = outs[a].shape[row_axis]
            if chips_only_other:
                return [outs[a].at[pchip, _half(core, rows)] for _, _, pchip in _chip_peers(mx, my, mc)]
            return [outs[a].at[_half(core, rows)]]

        sends, recvs = [], []
        for a in range(n):
            for k, v in enumerate(views(a, mc)):
                sends.append(_remote(v, v, sems, per * a + k, sibling))
            for k, v in enumerate(views(a, 1 - mc)):
                recvs.append(_remote(v, v, sems, per * a + k, sibling))
        return sends, recvs, []

    return _copies_hosted(arrays, [jax.ShapeDtypeStruct(s.shape, s.dtype) for s in arrays], (per * n, per * n), plan,
                          aliases={a: a for a in range(n)})


def _grad_swap_d2d(stacks):
    n = len(stacks)

    def plan(ins, outs, sems):
        mx, my, mc = _my_place()
        sibling = (mx, my, 1 - mc)
        sends = [_remote(ins[a].at[:, _half(1 - mc, ins[a].shape[1])], outs[a], sems, a, sibling) for a in range(n)]
        return sends, sends, []

    return _copies_hosted(stacks, [jax.ShapeDtypeStruct((N_CHIP, s.shape[1] // 2, s.shape[2]), s.dtype)
                                   for s in stacks], (n, n), plan)


def _grad_exchange_ici(parts):
    n = len(parts)

    def plan(ins, outs, sems):
        mx, my, mc = _my_place()
        chip = 2 * mx + my
        sends, recvs, locals_ = [], [], []
        for a in range(n):
            locals_.append(pltpu.make_async_copy(ins[a].at[chip], outs[a].at[chip], sems[2].at[a]))
            for k, peer, pchip in _chip_peers(mx, my, mc):
                sends.append(_remote(ins[a].at[pchip], outs[a].at[chip], sems, 3 * a + k, peer))
                recvs.append(_remote(ins[a].at[pchip], outs[a].at[pchip], sems, 3 * a + k, peer))
        return sends, recvs, locals_

    return _copies_hosted(parts, [jax.ShapeDtypeStruct(s.shape, s.dtype) for s in parts], (3 * n, 3 * n, n), plan)


def _row_block(rows, limit=256):
    for t in range(min(rows, limit), 7, -8):
        if rows % t == 0 and t % 8 == 0:
            return t
    return rows


def _grad_add_half(core, stack, received):
    _, rows, cw = stack.shape
    rh = rows // 2
    tr = _row_block(rh)

    def body(s_ref, a_ref, b_ref, o_ref):
        o_ref[...] = (a_ref[...] + b_ref[...]).astype(o_ref.dtype)

    grid_spec = pltpu.PrefetchScalarGridSpec(
        num_scalar_prefetch=1, grid=(rh // tr,),
        in_specs=[pl.BlockSpec((N_CHIP, tr, cw), lambda i, s: (0, s[0] * (rh // tr) + i, 0)),
                  pl.BlockSpec((N_CHIP, tr, cw), lambda i, s: (0, i, 0))],
        out_specs=pl.BlockSpec((N_CHIP, tr, cw), lambda i, s: (0, i, 0)))
    return pl.pallas_call(
        body, name="grad_add_half", grid_spec=grid_spec, out_shape=jax.ShapeDtypeStruct(received.shape, BF16),
        compiler_params=pltpu.CompilerParams(dimension_semantics=("arbitrary",), vmem_limit_bytes=VMEM_LIMIT),
    )(core, stack, received)


def _adamw(name, row_off, parts, w, m, v, rows, hosted=None):
    n, _, cw = parts.shape
    tr = _row_block(rows, 128)

    def update(p_ref, w_ref, m_ref, v_ref, g_ref, d_ref, nm_ref, nv_ref):
        g = p_ref[0].astype(F32)
        for k in range(1, n):
            g = g + p_ref[k].astype(F32)
        nm = ADAM_B1 * m_ref[...] + (1.0 - ADAM_B1) * g
        nv = ADAM_B2 * v_ref[...] + (1.0 - ADAM_B2) * (g * g)
        m_hat = nm / (1.0 - ADAM_B1 ** ADAM_STEP)
        v_hat = nv / (1.0 - ADAM_B2 ** ADAM_STEP)
        g_ref[...] = g
        d_ref[...] = -ADAM_LR * (m_hat / (jnp.sqrt(v_hat) + ADAM_EPS) + ADAM_WD * w_ref[...])
        nm_ref[...] = nm
        nv_ref[...] = nv

    out_shape = [jax.ShapeDtypeStruct(w.shape, F32)] * 4
    if row_off is None:
        blk = pl.BlockSpec((tr, cw), lambda i: (i, 0))
        return _pcall(update, name=name, grid=(rows // tr,), out_specs=[blk] * 4, out_shape=out_shape,
                      in_specs=[pl.BlockSpec((n, tr, cw), lambda i: (0, i, 0)), blk, blk, blk],
                      args=[parts, w, m, v], hosted=hosted)

    def body(s_ref, *refs):
        update(*refs)

    full = pl.BlockSpec((tr, cw), lambda i, s: (s[0] // tr + i, 0))
    grid_spec = pltpu.PrefetchScalarGridSpec(
        num_scalar_prefetch=1, grid=(rows // tr,),
        in_specs=[pl.BlockSpec((n, tr, cw), lambda i, s: (0, i, 0)), full, full, full],
        out_specs=[full, full, full, full])
    return pl.pallas_call(
        body, name=name, grid_spec=grid_spec, out_shape=out_shape,
        compiler_params=pltpu.CompilerParams(dimension_semantics=("arbitrary",), vmem_limit_bytes=VMEM_LIMIT),
    )(row_off, parts, w, m, v)


def _sum8(x):
    _, r, cw = x.shape
    tr = _row_block(r, 64)

    def body(x_ref, o_ref):
        acc = x_ref[0]
        for k in range(1, N_DEV):
            acc = acc + x_ref[k]
        o_ref[...] = acc

    return pl.pallas_call(
        body, name="sum8", grid=(r // tr,), in_specs=[pl.BlockSpec((N_DEV, tr, cw), lambda i: (0, i, 0))],
        out_specs=pl.BlockSpec((tr, cw), lambda i: (i, 0)), out_shape=jax.ShapeDtypeStruct((r, cw), F32),
        compiler_params=pltpu.CompilerParams(dimension_semantics=("arbitrary",)),
    )(x)


MOD_ROWS = 24
CTX_ROW = 16


def _mod_fwd(c_rows, w_mod, b_mod, hosted=None):
    nl, d, nn = w_mod.shape

    def body(c_ref, w_ref, b_ref, o_ref):
        o_ref[0] = _dot(_silu(c_ref[...]), w_ref[0]) + b_ref[0]

    return _pcall(
        body, name="mod_fwd", grid=(nl,),
        in_specs=[pl.BlockSpec((MOD_ROWS, d), lambda i: (0, 0)), pl.BlockSpec((1, d, nn), lambda i: (i, 0, 0)),
                  pl.BlockSpec((1, 1, nn), lambda i: (i, 0, 0))],
        out_specs=[pl.BlockSpec((1, MOD_ROWS, nn), lambda i: (i, 0, 0))],
        out_shape=[jax.ShapeDtypeStruct((nl, MOD_ROWS, nn), F32)], args=[c_rows, w_mod, b_mod], hosted=hosted)[0]


def _mod_bwd_rows(dlat, dctx_parts):
    nl, ne, nn = dlat.shape

    def body(l_ref, c_ref, db_ref, dc_ref):
        dc = c_ref[0, 0:1, :]
        for k in range(1, N_DEV):
            dc = dc + c_ref[0, k:k + 1, :]
        db = dc
        for k in range(ne):
            db = db + l_ref[0, k:k + 1, :]
        db_ref[0] = db
        dc_ref[0] = dc

    return pl.pallas_call(
        body, name="mod_bwd_rows", grid=(nl,),
        in_specs=[pl.BlockSpec((1, ne, nn), lambda i: (i, 0, 0)), pl.BlockSpec((1, N_DEV, nn), lambda i: (i, 0, 0))],
        out_specs=[pl.BlockSpec((1, 1, nn), lambda i: (i, 0, 0))] * 2,
        out_shape=[jax.ShapeDtypeStruct((nl, 1, nn), F32)] * 2,
        compiler_params=pltpu.CompilerParams(dimension_semantics=("arbitrary",)),
    )(dlat, dctx_parts)


def _mod_bwd_w(c_cols, d_rows, w_mod):
    nl, d, nn = w_mod.shape

    def body(c_ref, d_ref, w_ref, dw_ref, dc_ref):
        i = pl.program_id(0)
        c = c_ref[...]
        sg = jax.nn.sigmoid(c)
        s = c * sg
        dv = d_ref[0]
        acc = s[:, 0:1] * dv[0:1, :]
        for r in range(1, CTX_ROW + 1):
            acc = acc + s[:, r:r + 1] * dv[r:r + 1, :]
        dw_ref[0] = acc
        ds_ctx = jnp.sum(w_ref[0] * dv[CTX_ROW:CTX_ROW + 1, :], axis=1, keepdims=True)
        cc, sc = c[:, CTX_ROW:CTX_ROW + 1], sg[:, CTX_ROW:CTX_ROW + 1]
        part = ds_ctx * (sc * (1.0 + cc * (1.0 - sc)))

        @pl.when(i == 0)
        def _():
            dc_ref[...] = part

        @pl.when(i != 0)
        def _():
            dc_ref[...] += part

    return pl.pallas_call(
        body, name="mod_bwd_w", grid=(nl,),
        in_specs=[pl.BlockSpec((d, MOD_ROWS), lambda i: (0, 0)), pl.BlockSpec((1, MOD_ROWS, nn), lambda i: (i, 0, 0)),
                  pl.BlockSpec((1, d, nn), lambda i: (i, 0, 0))],
        out_specs=[pl.BlockSpec((1, d, nn), lambda i: (i, 0, 0)), pl.BlockSpec((d, 1), lambda i: (0, 0))],
        out_shape=[jax.ShapeDtypeStruct((nl, d, nn), F32), jax.ShapeDtypeStruct((d, 1), F32)],
        compiler_params=pltpu.CompilerParams(dimension_semantics=("arbitrary",), vmem_limit_bytes=VMEM_LIMIT),
    )(c_cols, d_rows, w_mod)


def _pack_rows(arrays, width, row_multiple=8):
    rows, spans, r0 = [], [], 0
    for a in arrays:
        flat = a.reshape(-1)
        nr = -(-flat.shape[0] // width)
        held = -(-nr // 8) * 8
        flat = jnp.pad(flat, (0, held * width - flat.shape[0]))
        rows.append(flat.reshape(held, width))
        spans.append((r0, nr, a.shape))
        r0 += held
    if r0 % row_multiple:
        rows.append(jnp.zeros((row_multiple - r0 % row_multiple, width), F32))
    return jnp.concatenate(rows, axis=0), spans


def _unpack_rows(packed, spans):
    out = []
    for r0, nr, shape in spans:
        out.append(packed[r0:r0 + nr].reshape(-1)[:math.prod(shape)].reshape(shape))
    return out


BIG = {"cv_w_in": 1, "cv_w_out": 0, "pl_w_in": 1, "pl_w_grp": None, "pl_w_out": 0, "ml_w_in": 1, "ml_w_uq": 1,
       "ml_w_ukv": 1, "ml_w_out": 0, "ch_w_in": 1, "ch_w_out": 0}
SMALL_SHARDED = ["cv_dw", "pl_scale", "ml_q_norm", "ml_kv_norm", "ch_ln_g", "ch_ln_b"]
SMALL_REPLICATED = ["c_ctx", "norm_g", "b_mod", "cv_db", "cv_ln_g", "cv_ln_b", "ml_nope_norm", "ml_rope_norm",
                    "ch_w_s", "ch_b_s"]
WEIGHTS = ['c_ctx', 'norm_g', 'w_mod', 'b_mod', 'cv_w_in', 'cv_dw', 'cv_db', 'cv_ln_g', 'cv_ln_b', 'cv_w_out',
           'pl_w_in', 'pl_w_grp', 'pl_scale', 'pl_w_out', 'ml_w_in', 'ml_q_norm', 'ml_kv_norm', 'ml_w_uq', 'ml_w_ukv',
           'ml_nope_norm', 'ml_rope_norm', 'ml_w_out', 'ch_w_in', 'ch_ln_g', 'ch_ln_b', 'ch_w_s', 'ch_b_s', 'ch_w_out']


def _shard2d(name, a):
    if name == "pl_w_grp":
        return a.reshape(a.shape[-3] * a.shape[-2], a.shape[-1])
    return a.reshape(a.shape[-2], a.shape[-1])


def _unstack(name, s):
    if name == "pl_w_grp":
        ng = len(POOL_WINDOWS)
        return s.reshape(N_CHIP, ng, s.shape[1] // ng, s.shape[2]).transpose(1, 0, 2, 3).reshape(ng, -1, s.shape[2])
    if BIG[name] == 0:
        return s.reshape(-1, s.shape[2])
    return s.transpose(1, 0, 2).reshape(s.shape[1], -1)


def _stack(name, g):
    if name == "pl_w_grp":
        ng = len(POOL_WINDOWS)
        return g.reshape(ng, N_CHIP, -1, g.shape[2]).transpose(1, 0, 2, 3).reshape(N_CHIP, -1, g.shape[2])
    if BIG[name] == 0:
        return g.reshape(N_CHIP, -1, g.shape[1])
    return g.reshape(g.shape[0], N_CHIP, -1).transpose(1, 0, 2)


L0, L1, L2, L3 = LAYER_WEIGHTS
EARLY_SMALL = ("ch_w_s", "ch_b_s", "ch_ln_g", "ch_ln_b")
MESH_SCHEDULE = {
    "ag8_inputs": [("gather", L0)], "mod_fwd": [("gfill", L0)],
    "cv_in_fwd": [("gather", L1[:1])], "conv_fwd": [("gather", L1[1:])], "cv_out_fwd": [("gfill", L1)],
    "pl_in_fwd": [("gather", L2[:1])], "pool_fwd": [("gather", L2[1:])], "pl_out_fwd": [("gfill", L2)],
    "attn_fwd": [("gather", L3)], "ml_out_fwd": [("gfill", L3)],
    "ch_in_bwd": [("small", EARLY_SMALL)],
    "ml_out_bwd": [("swap", L3)], "attn_bwd": [("exch", L3)], "ml_queries_bwd": [("ofill", L3)],
    "pl_out_bwd": [("swap", L2)], "pl_in_bwd": [("exch", L2)],
    "cv_out_bwd": [("swap", L1), ("ofill", L2)], "conv_bwd": [("exch", L1)], "cv_in_bwd": [("ofill", L1)],
    "adamw_w_mod": [("exch", L0)], "adamw_small": [("ofill", L0)],
}


class _MeshPlan:
    def __init__(self, weights, m, v, core):
        self.W, self.M, self.V, self.core = weights, m, v, core
        self.small = None
        self.stack, self.gstack, self.part, self.half, self.out = {}, {}, {}, {}, {}
        self.notes, self.early = {}, {}
        self.live, self.done = {}, set()

    def _make(self, op, names):
        if op == "gather":
            return _gather_ici([_shard2d(n, self.W[n]).astype(BF16) for n in names])
        if op == "gfill":
            return _sibling_fill([self.stack[n] for n in names], 1, True)
        if op == "swap":
            return _grad_swap_d2d([self.gstack[n] for n in names])
        if op == "exch":
            return _grad_exchange_ici([self.part[n] for n in names])
        if op == "ofill":
            return _sibling_fill([t for n in names for t in self.half[n]], 0, False)
        pack, self.early_spans = _pack_rows([self.notes[n] for n in names], LANES, 128)
        return _ag8_copies(pack)

    def _finish_op(self, op, names, hosted):
        self.done.add((op, names))
        res = hosted.results
        if op in ("gather", "gfill"):
            self.stack.update(zip(names, res))
        elif op == "swap":
            for n, r in zip(names, res):
                self.part[n] = _grad_add_half(self.core, self.gstack[n], r)
        elif op == "exch":
            for n, q in zip(names, res):
                rh = q.shape[1]
                self.half[n] = _adamw("adamw_" + n, self.core * rh, q, _shard2d(n, self.W[n]),
                                      _shard2d(n, self.M[n]), _shard2d(n, self.V[n]), rh)
        elif op == "ofill":
            for k, n in enumerate(names):
                self.out[n] = tuple(r.reshape(self.W[n].shape) for r in res[4 * k:4 * k + 4])
        else:
            self.early.update(zip(names, _unpack_rows(_sum8(res[0]), self.early_spans)))

    def alone(self, op, names):
        hosted = self._make(op, names)
        _run_hosted("%s_%s" % (op, names[0]), hosted)
        self._finish_op(op, names, hosted)

    def weights(self, layer):
        wk = {n: _unstack(n, self.stack[n]) for n in LAYER_WEIGHTS[layer]}
        if "pl_w_grp" in wk:
            wk["pl_w_grp"] = wk["pl_w_grp"].astype(F32)
        return _prep_weights(wk)

    def hosted(self, tag):
        self.live[tag] = [(op, names, self._make(op, names)) for op, names in MESH_SCHEDULE.get(tag, [])]
        return _merge_hosted([h for _, _, h in self.live[tag]])

    def after(self, tag):
        for op, names, hosted in self.live.pop(tag, []):
            self._finish_op(op, names, hosted)

    def note(self, values):
        self.notes.update(values)

    def layer_grads(self, layer, grads):
        g = _unprep_grads(grads)
        for n in LAYER_WEIGHTS[layer]:
            self.gstack[n] = _stack(n, g[n])

    def finish(self):
        for names in (L3, L2, L1, L0):
            for op in ("swap", "exch", "ofill"):
                if (op, names) not in self.done:
                    self.alone(op, names)
        return self.out


def kernel(x, c, ctx, c_ctx, norm_g, w_mod, b_mod, cv_w_in, cv_dw, cv_db, cv_ln_g, cv_ln_b, cv_w_out, pl_w_in, pl_w_grp, pl_scale, pl_w_out, ml_w_in, ml_q_norm, ml_kv_norm, ml_w_uq, ml_w_ukv, ml_nope_norm, ml_rope_norm, ml_w_out, ch_w_in, ch_ln_g, ch_ln_b, ch_w_s, ch_b_s, ch_w_out, loss_target, m_c_ctx, m_norm_g, m_w_mod, m_b_mod, m_cv_w_in, m_cv_dw, m_cv_db, m_cv_ln_g, m_cv_ln_b, m_cv_w_out, m_pl_w_in, m_pl_w_grp, m_pl_scale, m_pl_w_out, m_ml_w_in, m_ml_q_norm, m_ml_kv_norm, m_ml_w_uq, m_ml_w_ukv, m_ml_nope_norm, m_ml_rope_norm, m_ml_w_out, m_ch_w_in, m_ch_ln_g, m_ch_ln_b, m_ch_w_s, m_ch_b_s, m_ch_w_out, v_c_ctx, v_norm_g, v_w_mod, v_b_mod, v_cv_w_in, v_cv_dw, v_cv_db, v_cv_ln_g, v_cv_ln_b, v_cv_w_out, v_pl_w_in, v_pl_w_grp, v_pl_scale, v_pl_w_out, v_ml_w_in, v_ml_q_norm, v_ml_kv_norm, v_ml_w_uq, v_ml_w_ukv, v_ml_nope_norm, v_ml_rope_norm, v_ml_w_out, v_ch_w_in, v_ch_ln_g, v_ch_ln_b, v_ch_w_s, v_ch_b_s, v_ch_w_out):
    W = dict(c_ctx=c_ctx, norm_g=norm_g, w_mod=w_mod, b_mod=b_mod, cv_w_in=cv_w_in, cv_dw=cv_dw, cv_db=cv_db, cv_ln_g=cv_ln_g, cv_ln_b=cv_ln_b, cv_w_out=cv_w_out, pl_w_in=pl_w_in, pl_w_grp=pl_w_grp, pl_scale=pl_scale, pl_w_out=pl_w_out, ml_w_in=ml_w_in, ml_q_norm=ml_q_norm, ml_kv_norm=ml_kv_norm, ml_w_uq=ml_w_uq, ml_w_ukv=ml_w_ukv, ml_nope_norm=ml_nope_norm, ml_rope_norm=ml_rope_norm, ml_w_out=ml_w_out, ch_w_in=ch_w_in, ch_ln_g=ch_ln_g, ch_ln_b=ch_ln_b, ch_w_s=ch_w_s, ch_b_s=ch_b_s, ch_w_out=ch_w_out)
    M = dict(c_ctx=m_c_ctx, norm_g=m_norm_g, w_mod=m_w_mod, b_mod=m_b_mod, cv_w_in=m_cv_w_in, cv_dw=m_cv_dw, cv_db=m_cv_db, cv_ln_g=m_cv_ln_g, cv_ln_b=m_cv_ln_b, cv_w_out=m_cv_w_out, pl_w_in=m_pl_w_in, pl_w_grp=m_pl_w_grp, pl_scale=m_pl_scale, pl_w_out=m_pl_w_out, ml_w_in=m_ml_w_in, ml_q_norm=m_ml_q_norm, ml_kv_norm=m_ml_kv_norm, ml_w_uq=m_ml_w_uq, ml_w_ukv=m_ml_w_ukv, ml_nope_norm=m_ml_nope_norm, ml_rope_norm=m_ml_rope_norm, ml_w_out=m_ml_w_out, ch_w_in=m_ch_w_in, ch_ln_g=m_ch_ln_g, ch_ln_b=m_ch_ln_b, ch_w_s=m_ch_w_s, ch_b_s=m_ch_b_s, ch_w_out=m_ch_w_out)
    V = dict(c_ctx=v_c_ctx, norm_g=v_norm_g, w_mod=v_w_mod, b_mod=v_b_mod, cv_w_in=v_cv_w_in, cv_dw=v_cv_dw, cv_db=v_cv_db, cv_ln_g=v_cv_ln_g, cv_ln_b=v_cv_ln_b, cv_w_out=v_cv_w_out, pl_w_in=v_pl_w_in, pl_w_grp=v_pl_w_grp, pl_scale=v_pl_scale, pl_w_out=v_pl_w_out, ml_w_in=v_ml_w_in, ml_q_norm=v_ml_q_norm, ml_kv_norm=v_ml_kv_norm, ml_w_uq=v_ml_w_uq, ml_w_ukv=v_ml_w_ukv, ml_nope_norm=v_ml_nope_norm, ml_rope_norm=v_ml_rope_norm, ml_w_out=v_ml_w_out, ch_w_in=v_ch_w_in, ch_ln_g=v_ch_ln_g, ch_ln_b=v_ch_ln_b, ch_w_s=v_ch_w_s, ch_b_s=v_ch_b_s, ch_w_out=v_ch_w_out)

    batch, lat_len, d = x.shape
    mx, my, mc = _my_place()
    chip = 2 * mx + my
    dev = 2 * chip + mc
    core = jnp.reshape(mc, (1,)).astype(jnp.int32)
    zero_off = jnp.zeros((1,), jnp.int32)
    big_names = list(BIG)

    sw = d // N_CHIP
    small_in = [c] + [jnp.pad(W[n].reshape(-1, W[n].shape[-1]), ((0, 0), (0, sw - W[n].shape[-1])))
                      for n in SMALL_SHARDED]
    pack1, spans1 = _pack_rows(small_in, sw)
    plan = _MeshPlan(W, M, V, core)
    gather1 = _ag8_copies(pack1)
    _run_hosted("ag8_inputs", _merge_hosted([gather1, plan.hosted("ag8_inputs")]))
    plan.after("ag8_inputs")
    got1 = gather1.results[0]
    c_all = got1[:, spans1[0][0]:spans1[0][0] + spans1[0][1]].reshape(N_DEV * batch, d)
    full_small = {}
    for n, (r0, nr, _) in zip(SMALL_SHARDED, spans1[1:]):
        blk = got1[0::2, r0:r0 + nr, :W[n].shape[-1]]
        full_small[n] = blk.transpose(1, 0, 2).reshape(nr, -1)

    c_rows = jnp.concatenate([c_all, c_ctx[None], jnp.zeros((MOD_ROWS - CTX_ROW - 1, d), F32)], axis=0)
    nmod = w_mod.shape[2]
    b_shard = lax.dynamic_slice(b_mod, (0, chip * nmod), (b_mod.shape[0], nmod))[:, None, :]
    mod_shard = _mod_fwd(c_rows, w_mod, b_shard, hosted=plan.hosted("mod_fwd"))
    plan.after("mod_fwd")
    got2 = _ag8("ag8_mod", mod_shard.reshape(-1, nmod))
    mod_full = got2[0::2].reshape(N_CHIP, 4, MOD_ROWS, nmod).transpose(1, 2, 0, 3).reshape(4, MOD_ROWS, 3 * d)
    mod_lat = lax.dynamic_slice(mod_full, (0, dev * batch, 0), (4, batch, 3 * d))
    mod_ctx = mod_full[:, CTX_ROW]
    mods = []
    for i in range(4):
        mods.append(tuple(
            jnp.stack([mod_lat[i, :, j * d:(j + 1) * d], jnp.broadcast_to(mod_ctx[i, j * d:(j + 1) * d], (batch, d))],
                      axis=1)[:, :, None, :] for j in range(3)))

    wk = dict(full_small)
    wk.update(norm_g=norm_g, cv_db=cv_db, cv_ln_g=cv_ln_g, cv_ln_b=cv_ln_b, ml_nope_norm=ml_nope_norm[0],
              ml_rope_norm=ml_rope_norm[0], ch_w_s=ch_w_s[0], ch_b_s=ch_b_s[0])
    plan.small = _prep_weights(wk)
    xm = jnp.concatenate([x, ctx], axis=1)
    loss_part, grad_x, dmods, g = _local_step(xm, loss_target, mods, plan, lat_len)
    g = _unprep_grads(g)
    plan.alone("swap", L0)
    loss = lax.psum(loss_part[0, 0], ("x", "y", "c"))

    lat_rows, ctx_rows = [], []
    for i in range(4):
        dsh, dsc, dgt = dmods[i]
        lat_rows.append(jnp.concatenate([dsh[:, 0, 0], dsc[:, 0, 0], dgt[:, 0, 0]], axis=1))
        zero = jnp.zeros((d,), F32)
        cs = [jnp.sum(t[:, 1, 0], axis=0) if ok else zero
              for t, ok in zip((dsh, dsc, dgt), (i <= 2, i <= 2, i <= 1))]
        ctx_rows.append(jnp.concatenate(cs, axis=0)[None])
    dmod_dev = jnp.concatenate(lat_rows + ctx_rows, axis=0)
    dmod_dev = jnp.pad(dmod_dev, ((0, (-dmod_dev.shape[0]) % 8), (0, 0)))
    got3 = _ag8("ag8_dmod", dmod_dev)
    dlat = got3[:, :4 * batch].reshape(N_DEV, 4, batch, 3 * d).transpose(1, 0, 2, 3).reshape(4, N_DEV * batch, 3 * d)
    dctx_parts = got3[:, 4 * batch:4 * batch + 4].transpose(1, 0, 2)
    g_b_mod, dctx = _mod_bwd_rows(dlat, dctx_parts)
    d_rows = jnp.concatenate([dlat, dctx, jnp.zeros((4, MOD_ROWS - CTX_ROW - 1, 3 * d), F32)], axis=1)
    d_rows = lax.dynamic_slice_in_dim(d_rows, chip * nmod, nmod, axis=2)
    g_w_mod, dcc_part = _mod_bwd_w(c_rows.T, d_rows, w_mod)

    wm2 = w_mod.reshape(-1, nmod)
    res_mod = _adamw("adamw_w_mod", None, g_w_mod.reshape(1, -1, nmod), wm2, M["w_mod"].reshape(-1, nmod),
                     V["w_mod"].reshape(-1, nmod), wm2.shape[0], hosted=plan.hosted("adamw_w_mod"))
    plan.after("adamw_w_mod")
    out = {"w_mod": tuple(r.reshape(w_mod.shape) for r in res_mod)}

    g_small_in = {n: g[n] for n in SMALL_SHARDED if n not in EARLY_SMALL}
    g_small_in.update(norm_g=g["norm_g"], cv_db=g["cv_db"], cv_ln_g=g["cv_ln_g"], cv_ln_b=g["cv_ln_b"],
                      ml_nope_norm=g["ml_nope_norm"], ml_rope_norm=g["ml_rope_norm"],
                      c_ctx=dcc_part.reshape(-1) * (mc == 0).astype(F32))
    small_names = list(g_small_in)
    pack4, spans4 = _pack_rows([g_small_in[n] for n in small_names], LANES, 128)
    summed = _sum8(_ag8("ag8_small_grads", pack4))
    gs = dict(zip(small_names, _unpack_rows(summed, spans4)))
    gs.update(plan.early)
    gs["b_mod"] = g_b_mod[:, 0]
    for n in SMALL_SHARDED:
        wd = W[n].shape[-1]
        gs[n] = lax.dynamic_slice_in_dim(gs[n], chip * wd, wd, axis=1)
    upd_names = SMALL_REPLICATED + SMALL_SHARDED
    pw, spans_u = _pack_rows([W[n] for n in upd_names], LANES, 128)
    pm, _ = _pack_rows([M[n] for n in upd_names], LANES, 128)
    pv, _ = _pack_rows([V[n] for n in upd_names], LANES, 128)
    pg, _ = _pack_rows([gs[n].reshape(W[n].shape) for n in upd_names], LANES, 128)
    res_small = _adamw("adamw_small", None, pg[None], pw, pm, pv, pw.shape[0], hosted=plan.hosted("adamw_small"))
    plan.after("adamw_small")
    for n, vals in zip(upd_names, zip(*[_unpack_rows(r, spans_u) for r in res_small])):
        out[n] = vals
    out.update(plan.finish())

    outs = [loss, grad_x]
    for j in range(4):
        outs.extend(out[n][j] for n in WEIGHTS)
    return tuple(outs)
```

```python
import functools
import math

import jax
import jax.numpy as jnp
from jax import lax
from jax.experimental import pallas as pl
from jax.experimental.pallas import tpu as pltpu

F32 = jnp.float32
BF16 = jnp.bfloat16
MESH = pl.DeviceIdType.MESH

EPS = 1e-6
GRID_W = 64
CONV_WIDTH = 31
CONV_HALF = CONV_WIDTH // 2
CONV_PAD = 16
POOL_WINDOWS = (2, 4, 8, 16)
POOL_HALF = max(POOL_WINDOWS) // 2
HEADS = 8
NOPE = 128
ROPE = 64
HEAD_W = 256
VDIM = 128
KV_RANK = 256
Q_RANK = 384
ATT_SCALE = (NOPE + ROPE) ** -0.5
ROPE_THETA = 10000.0
CHUNK = 128
CHUNK_GROUPS = 8
LANES = 128
TM = 256
TQ = 512
ATT_SUB = 512
VMEM_LIMIT = 56 * 1024 * 1024

ADAM_LR = 0.001
ADAM_B1 = 0.9
ADAM_B2 = 0.999
ADAM_EPS = 1e-08
ADAM_WD = 0.01
ADAM_STEP = 10


def _dot(a, b):
    return jnp.dot(a.astype(BF16), b.astype(BF16), preferred_element_type=F32)


def _dot_nt(a, b):
    return lax.dot_general(a.astype(BF16), b.astype(BF16), (((1,), (1,)), ((), ())), preferred_element_type=F32)


def _dot_tn(a, b):
    return lax.dot_general(a.astype(BF16), b.astype(BF16), (((0,), (0,)), ((), ())), preferred_element_type=F32)


@jax.custom_vjp
def _mm(a, w):
    return _dot(a, w)


def _mm_fwd(a, w):
    return _dot(a, w), (a, w)


def _mm_bwd(res, ct):
    a, w = res
    return _dot_nt(ct, w), _dot_tn(a, ct)


_mm.defvjp(_mm_fwd, _mm_bwd)


def _swap16_impl(x):
    n = x.shape[-1]
    ax = x.ndim - 1
    lane = lax.broadcasted_iota(jnp.int32, x.shape, ax)
    up = pltpu.roll(x, n - 16, ax)
    dn = pltpu.roll(x, 16, ax)
    return jnp.where((lane % 32) < 16, up, dn)


@jax.custom_vjp
def _swap16(x):
    return _swap16_impl(x)


_swap16.defvjp(lambda x: (_swap16_impl(x), None), lambda _, ct: (_swap16_impl(ct),))


def _rms(x, g, n=None):
    n = x.shape[-1] if n is None else n
    return x * lax.rsqrt(jnp.sum(x * x, axis=-1, keepdims=True) * (1.0 / n) + EPS) * g


def _layernorm(x, g, b):
    mu = jnp.mean(x, axis=-1, keepdims=True)
    xc = x - mu
    var = jnp.mean(xc * xc, axis=-1, keepdims=True)
    return xc * lax.rsqrt(var + EPS) * g + b


def _silu(x):
    return x * jax.nn.sigmoid(x)


def _rope(x, cos, sin):
    return x * cos + _swap16(x) * sin


ANY = pl.BlockSpec(memory_space=pl.ANY)


class _Hosted:
    def __init__(self, arrays, out_shapes, sems, start, wait, aliases=None):
        self.arrays, self.out_shapes, self.sems = list(arrays), list(out_shapes), list(sems)
        self.start, self.wait, self.aliases = start, wait, dict(aliases or {})
        self.results = None


def _merge_hosted(parts):
    parts = [p for p in parts if p is not None]
    if not parts:
        return None
    if len(parts) == 1:
        return parts[0]
    offs, a0, o0, s0 = [], 0, 0, 0
    for p in parts:
        offs.append((a0, o0, s0))
        a0, o0, s0 = a0 + len(p.arrays), o0 + len(p.out_shapes), s0 + len(p.sems)

    def run(which):
        def f(ins, outs, sems):
            for p, (a, o, s) in zip(parts, offs):
                getattr(p, which)(ins[a:a + len(p.arrays)], outs[o:o + len(p.out_shapes)], sems[s:s + len(p.sems)])
        return f

    aliases = {}
    for p, (a, o, _) in zip(parts, offs):
        aliases.update({a + i: o + j for i, j in p.aliases.items()})
    merged = _Hosted(sum((p.arrays for p in parts), []), sum((p.out_shapes for p in parts), []),
                     sum((p.sems for p in parts), []), run("start"), run("wait"), aliases)
    merged.parts, merged.offs = parts, offs
    return merged


def _deliver(hosted, results):
    hosted.results = list(results)
    for p, (_, o, _) in zip(getattr(hosted, "parts", []), getattr(hosted, "offs", [])):
        p.results = list(results[o:o + len(p.out_shapes)])


def _pcall(body, *, name, grid, in_specs, out_specs, out_shape, args, hosted=None, vmem_limit=True):
    n_in, n_out = len(args), len(out_shape)
    kwargs = {}
    if hosted is not None:
        nhi, nho, inner = len(hosted.arrays), len(hosted.out_shapes), body

        def body(*refs):
            ins, hin = refs[:n_in], refs[n_in:n_in + nhi]
            outs, hout = refs[n_in + nhi:n_in + nhi + n_out], refs[n_in + nhi + n_out:n_in + nhi + n_out + nho]
            sems = refs[n_in + nhi + n_out + nho:]
            first, last = None, None
            for k, g in enumerate(grid):
                f, l = pl.program_id(k) == 0, pl.program_id(k) == g - 1
                first = f if first is None else jnp.logical_and(first, f)
                last = l if last is None else jnp.logical_and(last, l)

            @pl.when(first)
            def _():
                hosted.start(hin, hout, sems)

            inner(*ins, *outs)

            @pl.when(last)
            def _():
                hosted.wait(hin, hout, sems)

        in_specs = list(in_specs) + [ANY] * nhi
        out_specs = list(out_specs) + [ANY] * nho
        out_shape = list(out_shape) + hosted.out_shapes
        args = list(args) + hosted.arrays
        kwargs = dict(scratch_shapes=hosted.sems,
                      input_output_aliases={n_in + i: n_out + j for i, j in hosted.aliases.items()})
    params = dict(dimension_semantics=("arbitrary",) * len(grid))
    if vmem_limit:
        params["vmem_limit_bytes"] = VMEM_LIMIT
    res = pl.pallas_call(body, name=name, grid=grid, in_specs=list(in_specs), out_specs=list(out_specs),
                         out_shape=list(out_shape), compiler_params=pltpu.CompilerParams(**params), **kwargs)(*args)
    if hosted is not None:
        _deliver(hosted, res[n_out:])
    return list(res[:n_out])


def _run_hosted(name, hosted):
    nhi, nho = len(hosted.arrays), len(hosted.out_shapes)

    def body(*refs):
        ins, outs, sems = refs[:nhi], refs[nhi:nhi + nho], refs[nhi + nho:]
        hosted.start(ins, outs, sems)
        hosted.wait(ins, outs, sems)

    res = pl.pallas_call(body, name=name, in_specs=[ANY] * nhi, out_specs=[ANY] * nho, out_shape=hosted.out_shapes,
                         scratch_shapes=hosted.sems, input_output_aliases=hosted.aliases)(*hosted.arrays)
    _deliver(hosted, res)
    return list(res)


def _const_spec(shape, single=False):
    nd = len(shape)
    if single:
        return pl.BlockSpec(shape, lambda b, i: (0,) * nd, pipeline_mode=pl.Buffered(1))
    return pl.BlockSpec(shape, lambda b, i: (0,) * nd)


def _tile_spec(arr, n_lat_tiles, lat_only=False):
    bt, _, cw = arr.shape
    if lat_only:
        return pl.BlockSpec((1, TM, cw), lambda b, i: (b if bt > 1 else 0, jnp.minimum(i, n_lat_tiles - 1), 0))
    return pl.BlockSpec((1, TM, cw), lambda b, i: (b if bt > 1 else 0, i, 0))


def _eparam_spec(arr, n_lat_tiles):
    cw = arr.shape[-1]
    return pl.BlockSpec((1, 1, 1, cw), lambda b, i: (b, (i >= n_lat_tiles).astype(jnp.int32), 0, 0))


def _stage_fwd(name, *, pre, post, wsel, splits, tiles, eparams, sparams, weights, out_widths, out_dtypes,
               batch, n_tiles, n_lat_tiles, hosted=None):
    nt, ne, ns, nw = len(tiles), len(eparams), len(sparams), len(weights)

    def body(*refs):
        t_refs = refs[:nt]
        e_refs = refs[nt:nt + ne]
        s_refs = refs[nt + ne:nt + ne + ns]
        w_refs = refs[nt + ne + ns:nt + ne + ns + nw]
        o_refs = refs[nt + ne + ns + nw:]
        tv = [r[0].astype(F32) for r in t_refs]
        ev = [r[0, 0] for r in e_refs]
        sv = [r[...] for r in s_refs]
        a = pre(tv, ev, sv)
        z = [_dot(a[wsel[j]], w_refs[j][...]) for j in range(nw)]
        if post is None:
            outs = [z[j][:, s:s + w] for (j, s, w) in splits]
        else:
            outs = post(z, tv, ev, sv)
        for o_ref, o in zip(o_refs, outs):
            o_ref[0] = o.astype(o_ref.dtype)

    in_specs = ([_tile_spec(t, n_lat_tiles) for t in tiles] + [_eparam_spec(e, n_lat_tiles) for e in eparams]
                + [_const_spec(s.shape) for s in sparams] + [_const_spec(w.shape, single=True) for w in weights])
    out_shape = [jax.ShapeDtypeStruct((batch, n_tiles * TM, w), dt) for w, dt in zip(out_widths, out_dtypes)]
    out_specs = [pl.BlockSpec((1, TM, w), lambda b, i: (b, i, 0)) for w in out_widths]
    return _pcall(body, name=name, grid=(batch, n_tiles), in_specs=in_specs, out_specs=out_specs,
                  out_shape=out_shape, args=[*tiles, *eparams, *sparams, *weights], hosted=hosted)


def _stage_bwd(name, *, pre, post, wsel, splits, tiles, tile_diff, eparams, sparams, weights, cots, cot_lat_only,
               batch, n_tiles, n_lat_tiles, add=None, add_lat_only=False, hosted=None, w_col_stack=None,
               dt_lat_only=False):
    nt, ne, ns, nw, nc = len(tiles), len(eparams), len(sparams), len(weights), len(cots)
    diff_idx = [k for k in range(nt) if tile_diff[k]]
    nd = len(diff_idx)
    has_add = add is not None
    w_col_stack = w_col_stack or [None] * nw

    def body(*refs):
        pos = 0
        t_refs = refs[pos:pos + nt]; pos += nt
        e_refs = refs[pos:pos + ne]; pos += ne
        s_refs = refs[pos:pos + ns]; pos += ns
        w_refs = refs[pos:pos + nw]; pos += nw
        c_refs = refs[pos:pos + nc]; pos += nc
        if has_add:
            add_ref = refs[pos]; pos += 1
        dt_refs = refs[pos:pos + nd]; pos += nd
        de_refs = refs[pos:pos + ne]; pos += ne
        ds_refs = refs[pos:pos + ns]; pos += ns
        dw_refs = refs[pos:pos + nw]; pos += nw

        b = pl.program_id(0)
        i = pl.program_id(1)
        is_lat = i < n_lat_tiles
        tv = [r[0].astype(F32) for r in t_refs]
        ev = tuple(r[0, 0] for r in e_refs)
        sv = tuple(r[...] for r in s_refs)
        dv0 = tuple(tv[k] for k in diff_idx)

        def merge(dv):
            full = list(tv)
            for k, v in zip(diff_idx, dv):
                full[k] = v
            return full

        def pre_f(dv, ev_, sv_):
            return tuple(pre(merge(dv), list(ev_), list(sv_)))

        a, vjp_pre = jax.vjp(pre_f, dv0, ev, sv)
        cv = []
        for c_ref, lat in zip(c_refs, cot_lat_only):
            c = c_ref[0].astype(F32)
            cv.append(jnp.where(is_lat, c, 0.0) if lat else c)
        if post is None:
            dz = []
            for j in range(nw):
                parts = [cv[k] for k, (jj, _, _) in enumerate(splits) if jj == j]
                dz.append(parts[0] if len(parts) == 1 else jnp.concatenate(parts, axis=1))
            dt2 = de2 = ds2 = None
        else:
            z = tuple(_dot(a[wsel[j]], w_refs[j][...]) for j in range(nw))

            def post_f(z_, dv, ev_, sv_):
                return tuple(post(list(z_), merge(dv), list(ev_), list(sv_)))

            _, vjp_post = jax.vjp(post_f, z, dv0, ev, sv)
            dz, dt2, de2, ds2 = vjp_post(tuple(cv))
        da = [None] * len(a)
        dws = []
        for j in range(nw):
            g = _dot_nt(dz[j], w_refs[j][...])
            da[wsel[j]] = g if da[wsel[j]] is None else da[wsel[j]] + g
            dws.append(_dot_tn(a[wsel[j]], dz[j]))
        da = tuple(jnp.zeros_like(a[k]) if da[k] is None else da[k] for k in range(len(a)))
        dt1, de1, ds1 = vjp_pre(da)

        def plus(u, v):
            return u if v is None else u + v

        for k in range(nd):
            val = plus(dt1[k], None if dt2 is None else dt2[k])
            if has_add and k == 0:
                addv = add_ref[0].astype(F32)
                val = val + (jnp.where(is_lat, addv, 0.0) if add_lat_only else addv)
            if dt_lat_only:
                @pl.when(is_lat)
                def _(k=k, val=val):
                    dt_refs[k][0] = val.astype(dt_refs[k].dtype)
            else:
                dt_refs[k][0] = val.astype(dt_refs[k].dtype)

        seg_first = jnp.logical_or(i == 0, i == n_lat_tiles)
        for k in range(ne):
            val = plus(de1[k], None if de2 is None else de2[k])

            @pl.when(seg_first)
            def _(k=k, val=val):
                de_refs[k][0, 0] = val

            @pl.when(jnp.logical_not(seg_first))
            def _(k=k, val=val):
                de_refs[k][0, 0] += val

        first = jnp.logical_and(b == 0, i == 0)
        acc = [(ds_refs[k], plus(ds1[k], None if ds2 is None else ds2[k])) for k in range(ns)]
        for j in range(nw):
            if w_col_stack[j]:
                cw = dws[j].shape[1] // w_col_stack[j]
                acc += [(dw_refs[j].at[c], dws[j][:, c * cw:(c + 1) * cw]) for c in range(w_col_stack[j])]
            else:
                acc.append((dw_refs[j], dws[j]))
        for ref, val in acc:
            @pl.when(first)
            def _(ref=ref, val=val):
                ref[...] = val

            @pl.when(jnp.logical_not(first))
            def _(ref=ref, val=val):
                ref[...] += val

    in_specs = ([_tile_spec(t, n_lat_tiles) for t in tiles] + [_eparam_spec(e, n_lat_tiles) for e in eparams]
                + [_const_spec(s.shape) for s in sparams] + [_const_spec(w.shape, single=True) for w in weights]
                + [_tile_spec(c, n_lat_tiles, lat) for c, lat in zip(cots, cot_lat_only)])
    args = [*tiles, *eparams, *sparams, *weights, *cots]
    if has_add:
        in_specs.append(_tile_spec(add, n_lat_tiles, add_lat_only))
        args.append(add)
    dt_tiles = n_lat_tiles if dt_lat_only else n_tiles
    out_shape = [jax.ShapeDtypeStruct((batch, dt_tiles * TM, tiles[k].shape[-1]), F32) for k in diff_idx]
    out_specs = [pl.BlockSpec((1, TM, tiles[k].shape[-1]), lambda b, i: (b, jnp.minimum(i, dt_tiles - 1), 0))
                 for k in diff_idx]
    out_shape += [jax.ShapeDtypeStruct(e.shape, F32) for e in eparams]
    out_specs += [_eparam_spec(e, n_lat_tiles) for e in eparams]
    out_shape += [jax.ShapeDtypeStruct(s.shape, F32) for s in sparams]
    out_specs += [_const_spec(s.shape) for s in sparams]
    dw_shapes = [(n, w.shape[0], w.shape[1] // n) if n else w.shape for w, n in zip(weights, w_col_stack)]
    out_shape += [jax.ShapeDtypeStruct(s, F32) for s in dw_shapes]
    out_specs += [_const_spec(s, single=True) for s in dw_shapes]
    res = _pcall(body, name=name, grid=(batch, n_tiles), in_specs=in_specs, out_specs=out_specs,
                 out_shape=out_shape, args=args, hosted=hosted)
    return res[:nd], res[nd:nd + ne], res[nd + ne:nd + ne + ns], res[nd + ne + ns:]


def _pre_adaln(tv, ev, sv):
    x = tv[0]
    sh, sc = ev[0], ev[1]
    return [_rms(x, sv[0]) * (1.0 + sc) + sh]


def _post_residual(x_index):
    def post(z, tv, ev, sv):
        return [tv[x_index] + ev[-1] * z[0]]
    return post


def _pre_conv_out(tv, ev, sv):
    c1, gg = tv[0], tv[1]
    return [_silu(_layernorm(c1, sv[0], sv[1])) * _silu(gg)]


def _pre_pool_out(tv, ev, sv):
    pooled, gg = tv[0], tv[1]
    w_grp, scale = sv[0], sv[1]
    gw = w_grp.shape[-1]
    y = jnp.concatenate([_mm(pooled[:, k * gw:(k + 1) * gw], w_grp[k]) for k in range(w_grp.shape[0])], axis=1)
    return [y * scale * _silu(gg)]


def _pre_rms_only(tv, ev, sv):
    return [_rms(tv[0], sv[0])]


def _post_mla_keys(z, tv, ev, sv):
    krp, cos, sin = tv[1], tv[2], tv[3]
    nope_g, rope_g = sv[1], sv[2]
    kv = z[0]
    kr = _rope(_rms(krp, rope_g, ROPE), cos, sin)
    ks, vs = [], []
    for h in range(HEADS):
        ks.append(_rms(kv[:, h * 2 * NOPE:h * 2 * NOPE + NOPE], nope_g))
        ks.append(kr)
        vs.append(kv[:, h * 2 * NOPE + NOPE:(h + 1) * 2 * NOPE])
    return [jnp.concatenate(ks, axis=1), jnp.concatenate(vs, axis=1)]


def _post_mla_queries(z, tv, ev, sv):
    cos, sin = tv[1], tv[2]
    nope_g, rope_g = sv[1], sv[2]
    q = z[0]
    qs = []
    for h in range(HEADS):
        qs.append(_rms(q[:, h * HEAD_W:h * HEAD_W + NOPE], nope_g))
        qs.append(_rope(_rms(q[:, h * HEAD_W + NOPE:(h + 1) * HEAD_W], rope_g, ROPE), cos, sin))
    return [jnp.concatenate(qs, axis=1)]


def _pre_mla_out(tv, ev, sv):
    return [tv[0] * _silu(tv[1])]


def _pre_chunk_out(tv, ev, sv):
    u, v, gg = tv[0], tv[1], tv[2]
    ln_g, ln_b, w_s, b_s = sv
    vn = _layernorm(v, ln_g, ln_b)
    rows = []
    for n in range(vn.shape[0] // CHUNK):
        blk = vn[n * CHUNK:(n + 1) * CHUNK]
        cols = [_mm(w_s[g], blk[:, g * LANES:(g + 1) * LANES]) + b_s[:, g:g + 1] for g in range(CHUNK_GROUPS)]
        rows.append(jnp.concatenate(cols, axis=1))
    s = jnp.concatenate(rows, axis=0)
    return [u * s * _silu(gg)]


def _segments(lat_len, tot_len):
    segs = [(0, lat_len)]
    if tot_len > lat_len:
        segs.append((lat_len, tot_len - lat_len))
    return segs


def _pad_rows(x):
    z = jnp.zeros((CONV_PAD, x.shape[1]), x.dtype)
    return jnp.concatenate([z, x, z], axis=0)


def _shifted(xp, j):
    n = xp.shape[0] - 2 * CONV_PAD
    if j != 0:
        xp = pltpu.roll(xp, (-j) % xp.shape[0], 0)
    return xp[CONV_PAD:CONV_PAD + n]


def _conv_fwd(a, bgate, dw, db, lat_len, hosted=None):
    batch, tot, e = a.shape
    segs = _segments(lat_len, tot)

    def body(a_ref, b_ref, dw_ref, db_ref, o_ref):
        w = dw_ref[...]
        for (s0, n) in segs:
            y = a_ref[0, s0:s0 + n, :] * jax.nn.sigmoid(b_ref[0, s0:s0 + n, :])
            yp = _pad_rows(y)
            acc = jnp.zeros_like(y) + db_ref[...]
            for k in range(CONV_WIDTH):
                acc = acc + _shifted(yp, k - CONV_HALF) * w[k:k + 1, :]
            o_ref[0, s0:s0 + n, :] = acc

    blk = pl.BlockSpec((1, tot, LANES), lambda b, cb: (b, 0, cb))
    return _pcall(
        body, name="conv_fwd", grid=(batch, e // LANES),
        in_specs=[blk, blk, pl.BlockSpec((CONV_WIDTH, LANES), lambda b, cb: (0, cb)),
                  pl.BlockSpec((1, LANES), lambda b, cb: (0, cb))],
        out_specs=[blk], out_shape=[jax.ShapeDtypeStruct(a.shape, F32)], args=[a, bgate, dw, db], hosted=hosted)[0]


def _conv_bwd(a, bgate, dw, dc1, lat_len, hosted=None):
    batch, tot, e = a.shape
    segs = _segments(lat_len, tot)

    def body(a_ref, b_ref, dw_ref, dc_ref, da_ref, dg_ref, ddw_ref, ddb_ref):
        b = pl.program_id(1)
        w = dw_ref[...]
        ddw_rows = [None] * CONV_WIDTH
        ddb = None
        for (s0, n) in segs:
            av = a_ref[0, s0:s0 + n, :]
            sg = jax.nn.sigmoid(b_ref[0, s0:s0 + n, :])
            y = av * sg
            dc = dc_ref[0, s0:s0 + n, :]
            yp, dcp = _pad_rows(y), _pad_rows(dc)
            dy = jnp.zeros_like(y)
            for k in range(CONV_WIDTH):
                j = k - CONV_HALF
                dy = dy + _shifted(dcp, -j) * w[k:k + 1, :]
                r = jnp.sum(dc * _shifted(yp, j), axis=0, keepdims=True)
                ddw_rows[k] = r if ddw_rows[k] is None else ddw_rows[k] + r
            r = jnp.sum(dc, axis=0, keepdims=True)
            ddb = r if ddb is None else ddb + r
            da_ref[0, s0:s0 + n, :] = dy * sg
            dg_ref[0, s0:s0 + n, :] = dy * av * sg * (1.0 - sg)

        @pl.when(b == 0)
        def _():
            ddw_ref[...] = jnp.zeros_like(ddw_ref)
            ddb_ref[...] = jnp.zeros_like(ddb_ref)

        for k in range(CONV_WIDTH):
            ddw_ref[k:k + 1, :] += ddw_rows[k]
        ddb_ref[...] += ddb

    blk = pl.BlockSpec((1, tot, LANES), lambda cb, b: (b, 0, cb))
    wspec = pl.BlockSpec((CONV_WIDTH, LANES), lambda cb, b: (0, cb))
    bspec = pl.BlockSpec((1, LANES), lambda cb, b: (0, cb))
    return _pcall(
        body, name="conv_bwd", grid=(e // LANES, batch),
        in_specs=[blk, blk, wspec, blk],
        out_specs=[blk, blk, wspec, bspec],
        out_shape=[jax.ShapeDtypeStruct(a.shape, F32), jax.ShapeDtypeStruct(a.shape, F32),
                   jax.ShapeDtypeStruct((CONV_WIDTH, e), F32), jax.ShapeDtypeStruct((1, e), F32)],
        args=[a, bgate, dw, dc1], hosted=hosted)


def _pool_taps(group):
    half = lax.shift_left(jnp.int32(1), group)
    taps = []
    for j in range(-POOL_HALF, POOL_HALF):
        inside = jnp.logical_and(j >= -half, j < half)
        taps.append(jnp.where(inside, 1.0, 0.0).astype(F32))
    return taps, half


def _pool_counts(n, half, shape):
    t = lax.broadcasted_iota(jnp.int32, shape, 0)
    cnt = jnp.minimum(t + half, n) - jnp.maximum(t - half, 0)
    return cnt.astype(F32)


def _pool_fwd(v, lat_len, hosted=None):
    batch, tot, e = v.shape
    gw = e // len(POOL_WINDOWS)
    segs = _segments(lat_len, tot)

    def body(v_ref, o_ref):
        taps, half = _pool_taps(pl.program_id(1))
        for (s0, n) in segs:
            x = v_ref[0, s0:s0 + n, :]
            xp = _pad_rows(x)
            acc = jnp.zeros_like(x)
            for idx, j in enumerate(range(-POOL_HALF, POOL_HALF)):
                acc = acc + _shifted(xp, j) * taps[idx]
            o_ref[0, s0:s0 + n, :] = acc / _pool_counts(n, half, x.shape) - x

    blk = pl.BlockSpec((1, tot, gw), lambda b, g: (b, 0, g))
    return _pcall(body, name="pool_fwd", grid=(batch, len(POOL_WINDOWS)), in_specs=[blk], out_specs=[blk],
                  out_shape=[jax.ShapeDtypeStruct(v.shape, F32)], args=[v], hosted=hosted)[0]


def _pool_bwd(dp, lat_len):
    batch, tot, e = dp.shape
    gw = e // len(POOL_WINDOWS)
    segs = _segments(lat_len, tot)

    def body(d_ref, o_ref):
        taps, half = _pool_taps(pl.program_id(1))
        for (s0, n) in segs:
            d = d_ref[0, s0:s0 + n, :]
            dnp = _pad_rows(d / _pool_counts(n, half, d.shape))
            acc = jnp.zeros_like(d)
            for idx, j in enumerate(range(-POOL_HALF, POOL_HALF)):
                acc = acc + _shifted(dnp, -j) * taps[idx]
            o_ref[0, s0:s0 + n, :] = acc - d

    blk = pl.BlockSpec((1, tot, gw), lambda b, g: (b, 0, g))
    return pl.pallas_call(
        body, name="pool_bwd", grid=(batch, len(POOL_WINDOWS)), in_specs=[blk], out_specs=blk,
        out_shape=jax.ShapeDtypeStruct(dp.shape, F32),
        compiler_params=pltpu.CompilerParams(dimension_semantics=("arbitrary", "arbitrary"),
                                             vmem_limit_bytes=VMEM_LIMIT),
    )(dp)


def _softmax_rows(q, k):
    s = _dot_nt(q, k) * ATT_SCALE
    m = jnp.max(s, axis=-1, keepdims=True)
    e = jnp.exp(s - m)
    return e / jnp.sum(e, axis=-1, keepdims=True)


def _attn_fwd(q, k, v, hosted=None):
    batch, lq, _ = q.shape
    tk = k.shape[1]
    tq = min(TQ, lq)
    sub = min(ATT_SUB, tq)

    def body(q_ref, k_ref, v_ref, o_ref):
        kv, vv = k_ref[0], v_ref[0]
        for r in range(tq // sub):
            rows = slice(r * sub, (r + 1) * sub)
            o_ref[0, rows, :] = _dot(_softmax_rows(q_ref[0, rows, :], kv), vv)

    return _pcall(
        body, name="attn_fwd", grid=(batch, HEADS, lq // tq),
        in_specs=[pl.BlockSpec((1, tq, HEAD_W), lambda b, h, i: (b, i, h)),
                  pl.BlockSpec((1, tk, HEAD_W), lambda b, h, i: (b, 0, h)),
                  pl.BlockSpec((1, tk, VDIM), lambda b, h, i: (b, 0, h))],
        out_specs=[pl.BlockSpec((1, tq, VDIM), lambda b, h, i: (b, i, h))],
        out_shape=[jax.ShapeDtypeStruct((batch, lq, HEADS * VDIM), F32)], args=[q, k, v], hosted=hosted)[0]


def _attn_bwd(q, k, v, do, hosted=None):
    batch, lq, _ = q.shape
    tk = k.shape[1]

    tq = min(TQ, lq)
    sub = min(ATT_SUB, tq)

    def body(q_ref, k_ref, v_ref, do_ref, dq_ref, dk_ref, dv_ref):
        i = pl.program_id(2)
        kv, vv = k_ref[0], v_ref[0]
        ps, dss = [], []
        for r in range(tq // sub):
            rows = slice(r * sub, (r + 1) * sub)
            p = _softmax_rows(q_ref[0, rows, :], kv)
            dp = _dot_nt(do_ref[0, rows, :], vv)
            ds = p * (dp - jnp.sum(p * dp, axis=-1, keepdims=True)) * ATT_SCALE
            dq_ref[0, rows, :] = _dot(ds, kv)
            ps.append(p.astype(BF16))
            dss.append(ds.astype(BF16))
        dk = _dot_tn(jnp.concatenate(dss, axis=0), q_ref[0])
        dv = _dot_tn(jnp.concatenate(ps, axis=0), do_ref[0])

        @pl.when(i == 0)
        def _():
            dk_ref[0] = dk
            dv_ref[0] = dv

        @pl.when(i != 0)
        def _():
            dk_ref[0] += dk
            dv_ref[0] += dv

    return _pcall(
        body, name="attn_bwd", grid=(batch, HEADS, lq // tq),
        in_specs=[pl.BlockSpec((1, tq, HEAD_W), lambda b, h, i: (b, i, h)),
                  pl.BlockSpec((1, tk, HEAD_W), lambda b, h, i: (b, 0, h)),
                  pl.BlockSpec((1, tk, VDIM), lambda b, h, i: (b, 0, h)),
                  pl.BlockSpec((1, tq, VDIM), lambda b, h, i: (b, i, h))],
        out_specs=[pl.BlockSpec((1, tq, HEAD_W), lambda b, h, i: (b, i, h)),
                   pl.BlockSpec((1, tk, HEAD_W), lambda b, h, i: (b, 0, h)),
                   pl.BlockSpec((1, tk, VDIM), lambda b, h, i: (b, 0, h))],
        out_shape=[jax.ShapeDtypeStruct(q.shape, F32), jax.ShapeDtypeStruct(k.shape, F32),
                   jax.ShapeDtypeStruct(v.shape, F32)],
        args=[q, k, v, do], hosted=hosted)


def _loss_kernel(y, target):
    batch, lq, d = y.shape

    def body(y_ref, t_ref, l_ref, dy_ref):
        first = jnp.logical_and(pl.program_id(0) == 0, pl.program_id(1) == 0)
        err = y_ref[0] - t_ref[0]
        dy_ref[0] = err * (1.0 / d)
        part = jnp.zeros((1, LANES), F32) + jnp.sum(err * err) * (0.5 / d)

        @pl.when(first)
        def _():
            l_ref[...] = part

        @pl.when(jnp.logical_not(first))
        def _():
            l_ref[...] += part

    blk = pl.BlockSpec((1, TM, d), lambda b, i: (b, i, 0))
    return pl.pallas_call(
        body, name="loss_head", grid=(batch, lq // TM), in_specs=[blk, blk],
        out_specs=[pl.BlockSpec((1, LANES), lambda b, i: (0, 0)), blk],
        out_shape=[jax.ShapeDtypeStruct((1, LANES), F32), jax.ShapeDtypeStruct(y.shape, F32)],
        compiler_params=pltpu.CompilerParams(dimension_semantics=("arbitrary", "arbitrary")),
    )(y, target)


def _rope_tables(lat_len, ctx_len):
    rows = lat_len // GRID_W
    row_id = jnp.repeat(jnp.arange(rows), GRID_W).astype(F32)
    col_id = jnp.tile(jnp.arange(GRID_W), rows).astype(F32)
    axis_dim = ROPE // 2
    freqs = ROPE_THETA ** (-jnp.arange(0, axis_dim, 2, dtype=F32) / axis_dim)
    ar = row_id[:, None] * freqs
    ac = col_id[:, None] * freqs
    cr, sr, cc, sc = jnp.cos(ar), jnp.sin(ar), jnp.cos(ac), jnp.sin(ac)
    pad = jnp.zeros((lat_len, LANES - ROPE), F32)
    cos = jnp.concatenate([cr, cr, cc, cc, pad], axis=1)
    sin = jnp.concatenate([-sr, sr, -sc, sc, pad], axis=1)
    ident = jnp.concatenate([jnp.ones((ctx_len, ROPE), F32), jnp.zeros((ctx_len, LANES - ROPE), F32)], axis=1)
    cos = jnp.concatenate([cos, ident], axis=0)
    sin = jnp.concatenate([sin, jnp.zeros((ctx_len, LANES), F32)], axis=0)
    return cos[None], sin[None]


def _prep_weights(w):
    p = dict(w)
    kvc = KV_RANK + ROPE
    if "ml_w_in" in w:
        wi = w["ml_w_in"]
        p["ml_w_in"] = jnp.concatenate(
            [wi[:, :kvc], jnp.zeros((wi.shape[0], LANES - ROPE), wi.dtype), wi[:, kvc:]], axis=1)
    if "ml_w_uq" in w:
        uq = w["ml_w_uq"].reshape(Q_RANK, HEADS, NOPE + ROPE)
        p["ml_w_uq"] = jnp.pad(uq, ((0, 0), (0, 0), (0, HEAD_W - NOPE - ROPE))).reshape(Q_RANK, HEADS * HEAD_W)
    if "ml_rope_norm" in w:
        p["ml_rope_norm"] = jnp.pad(w["ml_rope_norm"], ((0, 0), (0, LANES - ROPE)))
    return p


def _unprep_grads(g):
    out = dict(g)
    kvc = KV_RANK + ROPE
    if "ml_w_in" in g:
        wi = g["ml_w_in"]
        out["ml_w_in"] = jnp.concatenate([wi[:, :kvc], wi[:, kvc + LANES - ROPE:]], axis=1)
    if "ml_w_uq" in g:
        uq = g["ml_w_uq"].reshape(Q_RANK, HEADS, HEAD_W)
        out["ml_w_uq"] = uq[:, :, :NOPE + ROPE].reshape(Q_RANK, HEADS * (NOPE + ROPE))
    if "ml_rope_norm" in g:
        out["ml_rope_norm"] = g["ml_rope_norm"][:, :ROPE]
    return out


LAYER_WEIGHTS = (("cv_w_in", "cv_w_out"), ("pl_w_in", "pl_w_grp", "pl_w_out"),
                 ("ml_w_in", "ml_w_uq", "ml_w_ukv", "ml_w_out"), ("ch_w_in", "ch_w_out"))


class _LocalPlan:
    def __init__(self, w):
        self.small = w
        self.grads = {}

    def weights(self, layer):
        return {n: self.small[n] for n in LAYER_WEIGHTS[layer]}

    def hosted(self, tag):
        return None

    def after(self, tag):
        pass

    def note(self, values):
        pass

    def layer_grads(self, layer, grads):
        self.grads.update(grads)


def _local_step(xm, target, mods, plan, lat_len):
    batch, tot, d = xm.shape
    e = d
    n_all, n_lat = tot // TM, lat_len // TM
    cos, sin = _rope_tables(lat_len, tot - lat_len)
    g = {}
    w = dict(plan.small)

    def hosting(tag, fn, *args, **kwargs):
        out = fn(*args, hosted=plan.hosted(tag), **kwargs)
        plan.after(tag)
        return out

    def s1_splits(widths):
        out, s = [], 0
        for wd in widths:
            out.append((0, s, wd))
            s += wd
        return out

    def fwd_in(name, x, mod, gi, wname, widths, n_tiles):
        return hosting(name, _stage_fwd, name, pre=_pre_adaln, post=None, wsel=[0], splits=s1_splits(widths),
                       tiles=[x], eparams=[mod[0], mod[1]], sparams=[w["norm_g"][gi:gi + 1]], weights=[w[wname]],
                       out_widths=widths, out_dtypes=[F32] * len(widths), batch=batch, n_tiles=n_tiles,
                       n_lat_tiles=n_lat)

    def bwd_in(name, x, mod, gi, wname, widths, n_tiles, cots, lat_only, add, add_lat_only, stack=None,
               dx_lat_only=False):
        (dx,), (dsh, dsc), (dg,), (dw,) = hosting(
            name, _stage_bwd, name, pre=_pre_adaln, post=None, wsel=[0], splits=s1_splits(widths), tiles=[x],
            tile_diff=[True], eparams=[mod[0], mod[1]], sparams=[w["norm_g"][gi:gi + 1]], weights=[w[wname]],
            cots=cots, cot_lat_only=lat_only, batch=batch, n_tiles=n_tiles, n_lat_tiles=n_lat, add=add,
            add_lat_only=add_lat_only, w_col_stack=[stack], dt_lat_only=dx_lat_only)
        return dx, dsh, dsc, dg, dw

    def fwd_out(name, pre, tiles, mod, sparams, wname, n_tiles):
        return hosting(name, _stage_fwd, name, pre=pre, post=_post_residual(len(tiles) - 1), wsel=[0], splits=None,
                       tiles=tiles, eparams=[mod[2]], sparams=sparams, weights=[w[wname]], out_widths=[d],
                       out_dtypes=[F32], batch=batch, n_tiles=n_tiles, n_lat_tiles=n_lat)[0]

    def bwd_out(name, pre, tiles, mod, sparams, wname, n_tiles, cot):
        diff = [True] * (len(tiles) - 1) + [False]
        dts, (dgt,), dss, (dw,) = hosting(
            name, _stage_bwd, name, pre=pre, post=_post_residual(len(tiles) - 1), wsel=[0], splits=None, tiles=tiles,
            tile_diff=diff, eparams=[mod[2]], sparams=sparams, weights=[w[wname]], cots=[cot], cot_lat_only=[False],
            batch=batch, n_tiles=n_tiles, n_lat_tiles=n_lat)
        return dts, dgt, dss, dw

    w.update(plan.weights(0))
    cv_s = [w["cv_ln_g"], w["cv_ln_b"]]
    a0, b0, g0 = fwd_in("cv_in_fwd", xm, mods[0], 0, "cv_w_in", [e, e, e], n_all)
    c1 = hosting("conv_fwd", _conv_fwd, a0, b0, w["cv_dw"], w["cv_db"], lat_len)
    x1 = fwd_out("cv_out_fwd", _pre_conv_out, [c1, g0, xm], mods[0], cv_s, "cv_w_out", n_all)

    w.update(plan.weights(1))
    pl_s = [w["pl_w_grp"], w["pl_scale"]]
    v1, g1 = fwd_in("pl_in_fwd", x1, mods[1], 1, "pl_w_in", [e, e], n_all)
    pooled = hosting("pool_fwd", _pool_fwd, v1, lat_len)
    x2 = fwd_out("pl_out_fwd", _pre_pool_out, [pooled, g1, x1], mods[1], pl_s, "pl_w_out", n_all)

    w.update(plan.weights(2))
    ml_widths = [KV_RANK, LANES, Q_RANK, HEADS * VDIM]
    ckv, krp, cq, g2 = fwd_in("ml_in_fwd", x2, mods[2], 2, "ml_w_in", ml_widths, n_all)
    k_s = [w["ml_kv_norm"], w["ml_nope_norm"][1:2], w["ml_rope_norm"][1:2]]
    q_s = [w["ml_q_norm"], w["ml_nope_norm"][0:1], w["ml_rope_norm"][0:1]]
    kk, vv = hosting("ml_keys_fwd", _stage_fwd, "ml_keys_fwd", pre=_pre_rms_only, post=_post_mla_keys, wsel=[0],
                     splits=None, tiles=[ckv, krp, cos, sin], eparams=[], sparams=k_s, weights=[w["ml_w_ukv"]],
                     out_widths=[HEADS * HEAD_W, HEADS * VDIM], out_dtypes=[BF16, BF16], batch=batch,
                     n_tiles=n_all, n_lat_tiles=n_lat)
    (qq,) = _stage_fwd("ml_queries_fwd", pre=_pre_rms_only, post=_post_mla_queries, wsel=[0], splits=None,
                       tiles=[cq, cos, sin], eparams=[], sparams=q_s, weights=[w["ml_w_uq"]],
                       out_widths=[HEADS * HEAD_W], out_dtypes=[BF16], batch=batch, n_tiles=n_lat,
                       n_lat_tiles=n_lat)
    att = hosting("attn_fwd", _attn_fwd, qq, kk, vv)
    x3 = fwd_out("ml_out_fwd", _pre_mla_out, [att, g2, x2], mods[2], [], "ml_w_out", n_lat)

    w.update(plan.weights(3))
    ch_s = [w["ch_ln_g"], w["ch_ln_b"], w["ch_w_s"], w["ch_b_s"]]
    u3, v3, g3 = fwd_in("ch_in_fwd", x3, mods[3], 3, "ch_w_in", [e, e, e], n_lat)
    x4 = fwd_out("ch_out_fwd", _pre_chunk_out, [u3, v3, g3, x3], mods[3], ch_s, "ch_w_out", n_lat)

    loss_part, dy = _loss_kernel(x4, target)

    dmods = [None] * 4
    dnorm = [None] * 4
    big = {}
    (du, dv, dg), dgt, (g["ch_ln_g"], g["ch_ln_b"], g["ch_w_s"], g["ch_b_s"]), big["ch_w_out"] = bwd_out(
        "ch_out_bwd", _pre_chunk_out, [u3, v3, g3, x3], mods[3], ch_s, "ch_w_out", n_lat, dy)
    plan.note({n: g[n] for n in ("ch_ln_g", "ch_ln_b", "ch_w_s", "ch_b_s")})
    dx3, dsh, dsc, dnorm[3], big["ch_w_in"] = bwd_in("ch_in_bwd", x3, mods[3], 3, "ch_w_in", [e, e, e], n_lat,
                                                     [du, dv, dg], [False] * 3, dy, False, stack=N_CHIP)
    dmods[3] = (dsh, dsc, dgt)
    plan.layer_grads(3, big)

    big = {}
    (datt, dg), dgt, _, big["ml_w_out"] = bwd_out("ml_out_bwd", _pre_mla_out, [att, g2, x2], mods[2], [],
                                                  "ml_w_out", n_lat, dx3)
    dq, dk, dvv = hosting("attn_bwd", _attn_bwd, qq, kk, vv, datt)
    (dcq,), _, (g["ml_q_norm"], dnope0, drope0), (big["ml_w_uq"],) = hosting(
        "ml_queries_bwd", _stage_bwd, "ml_queries_bwd", pre=_pre_rms_only, post=_post_mla_queries, wsel=[0],
        splits=None, tiles=[cq, cos, sin], tile_diff=[True, False, False], eparams=[], sparams=q_s,
        weights=[w["ml_w_uq"]], cots=[dq], cot_lat_only=[False], batch=batch, n_tiles=n_lat, n_lat_tiles=n_lat)
    (dckv, dkrp), _, (g["ml_kv_norm"], dnope1, drope1), (big["ml_w_ukv"],) = hosting(
        "ml_keys_bwd", _stage_bwd, "ml_keys_bwd", pre=_pre_rms_only, post=_post_mla_keys, wsel=[0], splits=None,
        tiles=[ckv, krp, cos, sin], tile_diff=[True, True, False, False], eparams=[], sparams=k_s,
        weights=[w["ml_w_ukv"]], cots=[dk, dvv], cot_lat_only=[False, False], batch=batch, n_tiles=n_all,
        n_lat_tiles=n_lat, w_col_stack=[N_CHIP])
    g["ml_nope_norm"] = jnp.concatenate([dnope0, dnope1], axis=0)
    g["ml_rope_norm"] = jnp.concatenate([drope0, drope1], axis=0)
    dx2, dsh, dsc, dnorm[2], big["ml_w_in"] = bwd_in("ml_in_bwd", x2, mods[2], 2, "ml_w_in", ml_widths, n_all,
                                                     [dckv, dkrp, dcq, dg], [False, False, True, True], dx3, True)
    dmods[2] = (dsh, dsc, dgt)
    plan.layer_grads(2, big)

    big = {}
    (dpooled, dg), dgt, (big["pl_w_grp"], g["pl_scale"]), big["pl_w_out"] = bwd_out(
        "pl_out_bwd", _pre_pool_out, [pooled, g1, x1], mods[1], pl_s, "pl_w_out", n_all, dx2)
    dv1 = _pool_bwd(dpooled, lat_len)
    dx1, dsh, dsc, dnorm[1], big["pl_w_in"] = bwd_in("pl_in_bwd", x1, mods[1], 1, "pl_w_in", [e, e], n_all,
                                                     [dv1, dg], [False] * 2, dx2, False, stack=N_CHIP)
    dmods[1] = (dsh, dsc, dgt)
    plan.layer_grads(1, big)

    big = {}
    (dc1, dg), dgt, (g["cv_ln_g"], g["cv_ln_b"]), big["cv_w_out"] = bwd_out(
        "cv_out_bwd", _pre_conv_out, [c1, g0, xm], mods[0], cv_s, "cv_w_out", n_all, dx1)
    da, db, g["cv_dw"], g["cv_db"] = hosting("conv_bwd", _conv_bwd, a0, b0, w["cv_dw"], dc1, lat_len)
    dx0, dsh, dsc, dnorm[0], big["cv_w_in"] = bwd_in("cv_in_bwd", xm, mods[0], 0, "cv_w_in", [e, e, e], n_all,
                                                     [da, db, dg], [False] * 3, dx1, False, stack=N_CHIP,
                                                     dx_lat_only=True)
    dmods[0] = (dsh, dsc, dgt)
    plan.layer_grads(0, big)
    g["norm_g"] = jnp.concatenate(dnorm, axis=0)
    return loss_part, dx0, dmods, g


N_DEV = 8
N_CHIP = 4
ANY = pl.BlockSpec(memory_space=pl.ANY)


def _my_place():
    return lax.axis_index("x"), lax.axis_index("y"), lax.axis_index("c")


def _flip(v, f):
    return 1 - v if f else v


def _ag8_copies(x):
    def plan(ins, outs, sems):
        mx, my, mc = _my_place()
        me = 4 * mx + 2 * my + mc
        sends, recvs = [], []
        for rel in range(1, N_DEV):
            peer = (_flip(mx, rel & 4), _flip(my, rel & 2), _flip(mc, rel & 1))
            src_dev = 4 * peer[0] + 2 * peer[1] + peer[2]
            sends.append(_remote(ins[0], outs[0].at[me], sems, rel - 1, peer))
            recvs.append(_remote(ins[0], outs[0].at[src_dev], sems, rel - 1, peer))
        return sends, recvs, [pltpu.make_async_copy(ins[0], outs[0].at[me], sems[2].at[0])]

    return _copies_hosted([x], [jax.ShapeDtypeStruct((N_DEV,) + x.shape, x.dtype)], (N_DEV - 1, N_DEV - 1, 1), plan)


def _ag8(name, x):
    return _run_hosted(name, _ag8_copies(x))[0]


def _chip_peers(mx, my, mc):
    out = []
    for rel in range(1, N_CHIP):
        px, py = _flip(mx, rel & 2), _flip(my, rel & 1)
        out.append((rel - 1, (px, py, mc), 2 * px + py))
    return out


def _half(mc, rows):
    return pl.ds(pl.multiple_of(mc * (rows // 2), 8), rows // 2)


def _copies_hosted(arrays, out_shapes, n_sems, plan, aliases=None):
    def start(ins, outs, sems):
        sends, _, locals_ = plan(ins, outs, sems)
        for cp in locals_ + sends:
            cp.start()

    def wait(ins, outs, sems):
        sends, recvs, locals_ = plan(ins, outs, sems)
        for cp in recvs:
            cp.wait_recv()
        for cp in sends:
            cp.wait_send()
        for cp in locals_:
            cp.wait()

    return _Hosted(arrays, out_shapes, [pltpu.SemaphoreType.DMA((k,)) for k in n_sems], start, wait, aliases)


def _remote(src, dst, sems, k, peer):
    return pltpu.make_async_remote_copy(src_ref=src, dst_ref=dst, send_sem=sems[0].at[k], recv_sem=sems[1].at[k],
                                        device_id=peer, device_id_type=MESH)


def _gather_ici(shards):
    n = len(shards)

    def plan(ins, outs, sems):
        mx, my, mc = _my_place()
        chip = 2 * mx + my
        sends, recvs, locals_ = [], [], []
        for a in range(n):
            rows = ins[a].shape[0]
            locals_.append(pltpu.make_async_copy(ins[a], outs[a].at[chip], sems[2].at[a]))
            for k, peer, pchip in _chip_peers(mx, my, mc):
                src = ins[a].at[_half(mc, rows)]
                sends.append(_remote(src, outs[a].at[chip, _half(mc, rows)], sems, 3 * a + k, peer))
                recvs.append(_remote(src, outs[a].at[pchip, _half(mc, rows)], sems, 3 * a + k, peer))
        return sends, recvs, locals_

    return _copies_hosted(shards, [jax.ShapeDtypeStruct((N_CHIP,) + s.shape, s.dtype) for s in shards],
                          (3 * n, 3 * n, n), plan)


def _sibling_fill(arrays, row_axis, chips_only_other):
    n = len(arrays)
    per = 3 if chips_only_other else 1

    def plan(ins, outs, sems):
        mx, my, mc = _my_place()
        sibling = (mx, my, 1 - mc)

        def views(a, core):
            rows = outs[a].shape[row_axis]
            if chips_only_other:
                return [outs[a].at[pchip, _half(core, rows)] for _, _, pchip in _chip_peers(mx, my, mc)]
            return [outs[a].at[_half(core, rows)]]

        sends, recvs = [], []
        for a in range(n):
            for k, v in enumerate(views(a, mc)):
                sends.append(_remote(v, v, sems, per * a + k, sibling))
            for k, v in enumerate(views(a, 1 - mc)):
                recvs.append(_remote(v, v, sems, per * a + k, sibling))
        return sends, recvs, []

    return _copies_hosted(arrays, [jax.ShapeDtypeStruct(s.shape, s.dtype) for s in arrays], (per * n, per * n), plan,
                          aliases={a: a for a in range(n)})


def _grad_swap_d2d(stacks):
    n = len(stacks)

    def plan(ins, outs, sems):
        mx, my, mc = _my_place()
        sibling = (mx, my, 1 - mc)
        sends = [_remote(ins[a].at[:, _half(1 - mc, ins[a].shape[1])], outs[a], sems, a, sibling) for a in range(n)]
        return sends, sends, []

    return _copies_hosted(stacks, [jax.ShapeDtypeStruct((N_CHIP, s.shape[1] // 2, s.shape[2]), s.dtype)
                                   for s in stacks], (n, n), plan)


def _grad_exchange_ici(parts):
    n = len(parts)

    def plan(ins, outs, sems):
        mx, my, mc = _my_place()
        chip = 2 * mx + my
        sends, recvs, locals_ = [], [], []
        for a in range(n):
            locals_.append(pltpu.make_async_copy(ins[a].at[chip], outs[a].at[chip], sems[2].at[a]))
            for k, peer, pchip in _chip_peers(mx, my, mc):
                sends.append(_remote(ins[a].at[pchip], outs[a].at[chip], sems, 3 * a + k, peer))
                recvs.append(_remote(ins[a].at[pchip], outs[a].at[pchip], sems, 3 * a + k, peer))
        return sends, recvs, locals_

    return _copies_hosted(parts, [jax.ShapeDtypeStruct(s.shape, s.dtype) for s in parts], (3 * n, 3 * n, n), plan)


def _row_block(rows, limit=256):
    for t in range(min(rows, limit), 7, -8):
        if rows % t == 0 and t % 8 == 0:
            return t
    return rows


def _grad_add_half(core, stack, received):
    _, rows, cw = stack.shape
    rh = rows // 2
    tr = _row_block(rh)

    def body(s_ref, a_ref, b_ref, o_ref):
        o_ref[...] = (a_ref[...] + b_ref[...]).astype(o_ref.dtype)

    grid_spec = pltpu.PrefetchScalarGridSpec(
        num_scalar_prefetch=1, grid=(rh // tr,),
        in_specs=[pl.BlockSpec((N_CHIP, tr, cw), lambda i, s: (0, s[0] * (rh // tr) + i, 0)),
                  pl.BlockSpec((N_CHIP, tr, cw), lambda i, s: (0, i, 0))],
        out_specs=pl.BlockSpec((N_CHIP, tr, cw), lambda i, s: (0, i, 0)))
    return pl.pallas_call(
        body, name="grad_add_half", grid_spec=grid_spec, out_shape=jax.ShapeDtypeStruct(received.shape, BF16),
        compiler_params=pltpu.CompilerParams(dimension_semantics=("arbitrary",), vmem_limit_bytes=VMEM_LIMIT),
    )(core, stack, received)


def _adamw(name, row_off, parts, w, m, v, rows, hosted=None):
    n, _, cw = parts.shape
    tr = _row_block(rows, 128)

    def update(p_ref, w_ref, m_ref, v_ref, g_ref, d_ref, nm_ref, nv_ref):
        g = p_ref[0].astype(F32)
        for k in range(1, n):
            g = g + p_ref[k].astype(F32)
        nm = ADAM_B1 * m_ref[...] + (1.0 - ADAM_B1) * g
        nv = ADAM_B2 * v_ref[...] + (1.0 - ADAM_B2) * (g * g)
        m_hat = nm / (1.0 - ADAM_B1 ** ADAM_STEP)
        v_hat = nv / (1.0 - ADAM_B2 ** ADAM_STEP)
        g_ref[...] = g
        d_ref[...] = -ADAM_LR * (m_hat / (jnp.sqrt(v_hat) + ADAM_EPS) + ADAM_WD * w_ref[...])
        nm_ref[...] = nm
        nv_ref[...] = nv

    out_shape = [jax.ShapeDtypeStruct(w.shape, F32)] * 4
    if row_off is None:
        blk = pl.BlockSpec((tr, cw), lambda i: (i, 0))
        return _pcall(update, name=name, grid=(rows // tr,), out_specs=[blk] * 4, out_shape=out_shape,
                      in_specs=[pl.BlockSpec((n, tr, cw), lambda i: (0, i, 0)), blk, blk, blk],
                      args=[parts, w, m, v], hosted=hosted)

    def body(s_ref, *refs):
        update(*refs)

    full = pl.BlockSpec((tr, cw), lambda i, s: (s[0] // tr + i, 0))
    grid_spec = pltpu.PrefetchScalarGridSpec(
        num_scalar_prefetch=1, grid=(rows // tr,),
        in_specs=[pl.BlockSpec((n, tr, cw), lambda i, s: (0, i, 0)), full, full, full],
        out_specs=[full, full, full, full])
    return pl.pallas_call(
        body, name=name, grid_spec=grid_spec, out_shape=out_shape,
        compiler_params=pltpu.CompilerParams(dimension_semantics=("arbitrary",), vmem_limit_bytes=VMEM_LIMIT),
    )(row_off, parts, w, m, v)


def _sum8(x):
    _, r, cw = x.shape
    tr = _row_block(r, 64)

    def body(x_ref, o_ref):
        acc = x_ref[0]
        for k in range(1, N_DEV):
            acc = acc + x_ref[k]
        o_ref[...] = acc

    return pl.pallas_call(
        body, name="sum8", grid=(r // tr,), in_specs=[pl.BlockSpec((N_DEV, tr, cw), lambda i: (0, i, 0))],
        out_specs=pl.BlockSpec((tr, cw), lambda i: (i, 0)), out_shape=jax.ShapeDtypeStruct((r, cw), F32),
        compiler_params=pltpu.CompilerParams(dimension_semantics=("arbitrary",)),
    )(x)


MOD_ROWS = 24
CTX_ROW = 16


def _mod_fwd(c_rows, w_mod, b_mod, hosted=None):
    nl, d, nn = w_mod.shape

    def body(c_ref, w_ref, b_ref, o_ref):
        o_ref[0] = _dot(_silu(c_ref[...]), w_ref[0]) + b_ref[0]

    return _pcall(
        body, name="mod_fwd", grid=(nl,),
        in_specs=[pl.BlockSpec((MOD_ROWS, d), lambda i: (0, 0)), pl.BlockSpec((1, d, nn), lambda i: (i, 0, 0)),
                  pl.BlockSpec((1, 1, nn), lambda i: (i, 0, 0))],
        out_specs=[pl.BlockSpec((1, MOD_ROWS, nn), lambda i: (i, 0, 0))],
        out_shape=[jax.ShapeDtypeStruct((nl, MOD_ROWS, nn), F32)], args=[c_rows, w_mod, b_mod], hosted=hosted)[0]


def _mod_bwd_rows(dlat, dctx_parts):
    nl, ne, nn = dlat.shape

    def body(l_ref, c_ref, db_ref, dc_ref):
        dc = c_ref[0, 0:1, :]
        for k in range(1, N_DEV):
            dc = dc + c_ref[0, k:k + 1, :]
        db = dc
        for k in range(ne):
            db = db + l_ref[0, k:k + 1, :]
        db_ref[0] = db
        dc_ref[0] = dc

    return pl.pallas_call(
        body, name="mod_bwd_rows", grid=(nl,),
        in_specs=[pl.BlockSpec((1, ne, nn), lambda i: (i, 0, 0)), pl.BlockSpec((1, N_DEV, nn), lambda i: (i, 0, 0))],
        out_specs=[pl.BlockSpec((1, 1, nn), lambda i: (i, 0, 0))] * 2,
        out_shape=[jax.ShapeDtypeStruct((nl, 1, nn), F32)] * 2,
        compiler_params=pltpu.CompilerParams(dimension_semantics=("arbitrary",)),
    )(dlat, dctx_parts)


def _mod_bwd_w(c_cols, d_rows, w_mod):
    nl, d, nn = w_mod.shape

    def body(c_ref, d_ref, w_ref, dw_ref, dc_ref):
        i = pl.program_id(0)
        c = c_ref[...]
        sg = jax.nn.sigmoid(c)
        s = c * sg
        dv = d_ref[0]
        acc = s[:, 0:1] * dv[0:1, :]
        for r in range(1, CTX_ROW + 1):
            acc = acc + s[:, r:r + 1] * dv[r:r + 1, :]
        dw_ref[0] = acc
        ds_ctx = jnp.sum(w_ref[0] * dv[CTX_ROW:CTX_ROW + 1, :], axis=1, keepdims=True)
        cc, sc = c[:, CTX_ROW:CTX_ROW + 1], sg[:, CTX_ROW:CTX_ROW + 1]
        part = ds_ctx * (sc * (1.0 + cc * (1.0 - sc)))

        @pl.when(i == 0)
        def _():
            dc_ref[...] = part

        @pl.when(i != 0)
        def _():
            dc_ref[...] += part

    return pl.pallas_call(
        body, name="mod_bwd_w", grid=(nl,),
        in_specs=[pl.BlockSpec((d, MOD_ROWS), lambda i: (0, 0)), pl.BlockSpec((1, MOD_ROWS, nn), lambda i: (i, 0, 0)),
                  pl.BlockSpec((1, d, nn), lambda i: (i, 0, 0))],
        out_specs=[pl.BlockSpec((1, d, nn), lambda i: (i, 0, 0)), pl.BlockSpec((d, 1), lambda i: (0, 0))],
        out_shape=[jax.ShapeDtypeStruct((nl, d, nn), F32), jax.ShapeDtypeStruct((d, 1), F32)],
        compiler_params=pltpu.CompilerParams(dimension_semantics=("arbitrary",), vmem_limit_bytes=VMEM_LIMIT),
    )(c_cols, d_rows, w_mod)


def _pack_rows(arrays, width, row_multiple=8):
    rows, spans, r0 = [], [], 0
    for a in arrays:
        flat = a.reshape(-1)
        nr = -(-flat.shape[0] // width)
        held = -(-nr // 8) * 8
        flat = jnp.pad(flat, (0, held * width - flat.shape[0]))
        rows.append(flat.reshape(held, width))
        spans.append((r0, nr, a.shape))
        r0 += held
    if r0 % row_multiple:
        rows.append(jnp.zeros((row_multiple - r0 % row_multiple, width), F32))
    return jnp.concatenate(rows, axis=0), spans


def _unpack_rows(packed, spans):
    out = []
    for r0, nr, shape in spans:
        out.append(packed[r0:r0 + nr].reshape(-1)[:math.prod(shape)].reshape(shape))
    return out


BIG = {"cv_w_in": 1, "cv_w_out": 0, "pl_w_in": 1, "pl_w_grp": None, "pl_w_out": 0, "ml_w_in": 1, "ml_w_uq": 1,
       "ml_w_ukv": 1, "ml_w_out": 0, "ch_w_in": 1, "ch_w_out": 0}
SMALL_SHARDED = ["cv_dw", "pl_scale", "ml_q_norm", "ml_kv_norm", "ch_ln_g", "ch_ln_b"]
SMALL_REPLICATED = ["c_ctx", "norm_g", "b_mod", "cv_db", "cv_ln_g", "cv_ln_b", "ml_nope_norm", "ml_rope_norm",
                    "ch_w_s", "ch_b_s"]
WEIGHTS = ['c_ctx', 'norm_g', 'w_mod', 'b_mod', 'cv_w_in', 'cv_dw', 'cv_db', 'cv_ln_g', 'cv_ln_b', 'cv_w_out',
           'pl_w_in', 'pl_w_grp', 'pl_scale', 'pl_w_out', 'ml_w_in', 'ml_q_norm', 'ml_kv_norm', 'ml_w_uq', 'ml_w_ukv',
           'ml_nope_norm', 'ml_rope_norm', 'ml_w_out', 'ch_w_in', 'ch_ln_g', 'ch_ln_b', 'ch_w_s', 'ch_b_s', 'ch_w_out']


def _shard2d(name, a):
    if name == "pl_w_grp":
        return a.reshape(a.shape[-3] * a.shape[-2], a.shape[-1])
    return a.reshape(a.shape[-2], a.shape[-1])


def _unstack(name, s):
    if name == "pl_w_grp":
        ng = len(POOL_WINDOWS)
        return s.reshape(N_CHIP, ng, s.shape[1] // ng, s.shape[2]).transpose(1, 0, 2, 3).reshape(ng, -1, s.shape[2])
    if BIG[name] == 0:
        return s.reshape(-1, s.shape[2])
    return s.transpose(1, 0, 2).reshape(s.shape[1], -1)


def _stack(name, g):
    if g.ndim == 3 and name != "pl_w_grp":
        return g
    if name == "pl_w_grp":
        ng = len(POOL_WINDOWS)
        return g.reshape(ng, N_CHIP, -1, g.shape[2]).transpose(1, 0, 2, 3).reshape(N_CHIP, -1, g.shape[2])
    if BIG[name] == 0:
        return g.reshape(N_CHIP, -1, g.shape[1])
    return g.reshape(g.shape[0], N_CHIP, -1).transpose(1, 0, 2)


L0, L1, L2, L3 = LAYER_WEIGHTS
EARLY_SMALL = ("ch_w_s", "ch_b_s", "ch_ln_g", "ch_ln_b")
MESH_SCHEDULE = {
    "ag8_inputs": [("gather", L0)], "mod_fwd": [("gfill", L0)],
    "cv_in_fwd": [("gather", L1[:1])], "conv_fwd": [("gather", L1[1:])], "cv_out_fwd": [("gfill", L1)],
    "pl_in_fwd": [("gather", L2[:1])], "pool_fwd": [("gather", L2[1:])], "pl_out_fwd": [("gfill", L2)],
    "attn_fwd": [("gather", L3)], "ml_out_fwd": [("gfill", L3)],
    "ch_in_bwd": [("small", EARLY_SMALL)],
    "ml_out_bwd": [("swap", L3)], "attn_bwd": [("exch", L3)], "ml_queries_bwd": [("ofill", L3)],
    "pl_out_bwd": [("swap", L2)], "pl_in_bwd": [("exch", L2)],
    "cv_out_bwd": [("swap", L1), ("ofill", L2)], "conv_bwd": [("exch", L1)], "cv_in_bwd": [("ofill", L1)],
    "ag8_dmod": [("exch", L0)], "ag8_small_grads": [("ofill", L0)],
}


class _MeshPlan:
    def __init__(self, weights, m, v, core):
        self.W, self.M, self.V, self.core = weights, m, v, core
        self.small = None
        self.stack, self.gstack, self.part, self.half, self.out = {}, {}, {}, {}, {}
        self.notes, self.early = {}, {}
        self.live, self.done = {}, set()

    def _make(self, op, names):
        if op == "gather":
            return _gather_ici([_shard2d(n, self.W[n]).astype(BF16) for n in names])
        if op == "gfill":
            return _sibling_fill([self.stack[n] for n in names], 1, True)
        if op == "swap":
            return _grad_swap_d2d([self.gstack[n] for n in names])
        if op == "exch":
            return _grad_exchange_ici([self.part[n] for n in names])
        if op == "ofill":
            return _sibling_fill([t for n in names for t in self.half[n]], 0, False)
        pack, self.early_spans = _pack_rows([self.notes[n] for n in names], LANES, 128)
        return _ag8_copies(pack)

    def _finish_op(self, op, names, hosted):
        self.done.add((op, names))
        res = hosted.results
        if op in ("gather", "gfill"):
            self.stack.update(zip(names, res))
        elif op == "swap":
            for n, r in zip(names, res):
                self.part[n] = _grad_add_half(self.core, self.gstack[n], r)
        elif op == "exch":
            for n, q in zip(names, res):
                rh = q.shape[1]
                self.half[n] = _adamw("adamw_" + n, self.core * rh, q, _shard2d(n, self.W[n]),
                                      _shard2d(n, self.M[n]), _shard2d(n, self.V[n]), rh)
        elif op == "ofill":
            for k, n in enumerate(names):
                self.out[n] = tuple(r.reshape(self.W[n].shape) for r in res[4 * k:4 * k + 4])
        else:
            self.early.update(zip(names, _unpack_rows(_sum8(res[0]), self.early_spans)))

    def alone(self, op, names):
        hosted = self._make(op, names)
        _run_hosted("%s_%s" % (op, names[0]), hosted)
        self._finish_op(op, names, hosted)

    def weights(self, layer):
        wk = {n: _unstack(n, self.stack[n]) for n in LAYER_WEIGHTS[layer]}
        if "pl_w_grp" in wk:
            wk["pl_w_grp"] = wk["pl_w_grp"].astype(F32)
        return _prep_weights(wk)

    def hosted(self, tag):
        self.live[tag] = [(op, names, self._make(op, names)) for op, names in MESH_SCHEDULE.get(tag, [])]
        return _merge_hosted([h for _, _, h in self.live[tag]])

    def after(self, tag):
        for op, names, hosted in self.live.pop(tag, []):
            self._finish_op(op, names, hosted)

    def note(self, values):
        self.notes.update(values)

    def layer_grads(self, layer, grads):
        g = _unprep_grads(grads)
        for n in LAYER_WEIGHTS[layer]:
            self.gstack[n] = _stack(n, g[n])

    def finish(self):
        for names in (L3, L2, L1, L0):
            for op in ("swap", "exch", "ofill"):
                if (op, names) not in self.done:
                    self.alone(op, names)
        return self.out


def kernel(x, c, ctx, c_ctx, norm_g, w_mod, b_mod, cv_w_in, cv_dw, cv_db, cv_ln_g, cv_ln_b, cv_w_out, pl_w_in, pl_w_grp, pl_scale, pl_w_out, ml_w_in, ml_q_norm, ml_kv_norm, ml_w_uq, ml_w_ukv, ml_nope_norm, ml_rope_norm, ml_w_out, ch_w_in, ch_ln_g, ch_ln_b, ch_w_s, ch_b_s, ch_w_out, loss_target, m_c_ctx, m_norm_g, m_w_mod, m_b_mod, m_cv_w_in, m_cv_dw, m_cv_db, m_cv_ln_g, m_cv_ln_b, m_cv_w_out, m_pl_w_in, m_pl_w_grp, m_pl_scale, m_pl_w_out, m_ml_w_in, m_ml_q_norm, m_ml_kv_norm, m_ml_w_uq, m_ml_w_ukv, m_ml_nope_norm, m_ml_rope_norm, m_ml_w_out, m_ch_w_in, m_ch_ln_g, m_ch_ln_b, m_ch_w_s, m_ch_b_s, m_ch_w_out, v_c_ctx, v_norm_g, v_w_mod, v_b_mod, v_cv_w_in, v_cv_dw, v_cv_db, v_cv_ln_g, v_cv_ln_b, v_cv_w_out, v_pl_w_in, v_pl_w_grp, v_pl_scale, v_pl_w_out, v_ml_w_in, v_ml_q_norm, v_ml_kv_norm, v_ml_w_uq, v_ml_w_ukv, v_ml_nope_norm, v_ml_rope_norm, v_ml_w_out, v_ch_w_in, v_ch_ln_g, v_ch_ln_b, v_ch_w_s, v_ch_b_s, v_ch_w_out):
    W = dict(c_ctx=c_ctx, norm_g=norm_g, w_mod=w_mod, b_mod=b_mod, cv_w_in=cv_w_in, cv_dw=cv_dw, cv_db=cv_db, cv_ln_g=cv_ln_g, cv_ln_b=cv_ln_b, cv_w_out=cv_w_out, pl_w_in=pl_w_in, pl_w_grp=pl_w_grp, pl_scale=pl_scale, pl_w_out=pl_w_out, ml_w_in=ml_w_in, ml_q_norm=ml_q_norm, ml_kv_norm=ml_kv_norm, ml_w_uq=ml_w_uq, ml_w_ukv=ml_w_ukv, ml_nope_norm=ml_nope_norm, ml_rope_norm=ml_rope_norm, ml_w_out=ml_w_out, ch_w_in=ch_w_in, ch_ln_g=ch_ln_g, ch_ln_b=ch_ln_b, ch_w_s=ch_w_s, ch_b_s=ch_b_s, ch_w_out=ch_w_out)
    M = dict(c_ctx=m_c_ctx, norm_g=m_norm_g, w_mod=m_w_mod, b_mod=m_b_mod, cv_w_in=m_cv_w_in, cv_dw=m_cv_dw, cv_db=m_cv_db, cv_ln_g=m_cv_ln_g, cv_ln_b=m_cv_ln_b, cv_w_out=m_cv_w_out, pl_w_in=m_pl_w_in, pl_w_grp=m_pl_w_grp, pl_scale=m_pl_scale, pl_w_out=m_pl_w_out, ml_w_in=m_ml_w_in, ml_q_norm=m_ml_q_norm, ml_kv_norm=m_ml_kv_norm, ml_w_uq=m_ml_w_uq, ml_w_ukv=m_ml_w_ukv, ml_nope_norm=m_ml_nope_norm, ml_rope_norm=m_ml_rope_norm, ml_w_out=m_ml_w_out, ch_w_in=m_ch_w_in, ch_ln_g=m_ch_ln_g, ch_ln_b=m_ch_ln_b, ch_w_s=m_ch_w_s, ch_b_s=m_ch_b_s, ch_w_out=m_ch_w_out)
    V = dict(c_ctx=v_c_ctx, norm_g=v_norm_g, w_mod=v_w_mod, b_mod=v_b_mod, cv_w_in=v_cv_w_in, cv_dw=v_cv_dw, cv_db=v_cv_db, cv_ln_g=v_cv_ln_g, cv_ln_b=v_cv_ln_b, cv_w_out=v_cv_w_out, pl_w_in=v_pl_w_in, pl_w_grp=v_pl_w_grp, pl_scale=v_pl_scale, pl_w_out=v_pl_w_out, ml_w_in=v_ml_w_in, ml_q_norm=v_ml_q_norm, ml_kv_norm=v_ml_kv_norm, ml_w_uq=v_ml_w_uq, ml_w_ukv=v_ml_w_ukv, ml_nope_norm=v_ml_nope_norm, ml_rope_norm=v_ml_rope_norm, ml_w_out=v_ml_w_out, ch_w_in=v_ch_w_in, ch_ln_g=v_ch_ln_g, ch_ln_b=v_ch_ln_b, ch_w_s=v_ch_w_s, ch_b_s=v_ch_b_s, ch_w_out=v_ch_w_out)

    batch, lat_len, d = x.shape
    mx, my, mc = _my_place()
    chip = 2 * mx + my
    dev = 2 * chip + mc
    core = jnp.reshape(mc, (1,)).astype(jnp.int32)
    zero_off = jnp.zeros((1,), jnp.int32)
    big_names = list(BIG)

    sw = d // N_CHIP
    small_in = [c] + [jnp.pad(W[n].reshape(-1, W[n].shape[-1]), ((0, 0), (0, sw - W[n].shape[-1])))
                      for n in SMALL_SHARDED]
    pack1, spans1 = _pack_rows(small_in, sw)
    plan = _MeshPlan(W, M, V, core)
    gather1 = _ag8_copies(pack1)
    _run_hosted("ag8_inputs", _merge_hosted([gather1, plan.hosted("ag8_inputs")]))
    plan.after("ag8_inputs")
    got1 = gather1.results[0]
    c_all = got1[:, spans1[0][0]:spans1[0][0] + spans1[0][1]].reshape(N_DEV * batch, d)
    full_small = {}
    for n, (r0, nr, _) in zip(SMALL_SHARDED, spans1[1:]):
        blk = got1[0::2, r0:r0 + nr, :W[n].shape[-1]]
        full_small[n] = blk.transpose(1, 0, 2).reshape(nr, -1)

    c_rows = jnp.concatenate([c_all, c_ctx[None], jnp.zeros((MOD_ROWS - CTX_ROW - 1, d), F32)], axis=0)
    nmod = w_mod.shape[2]
    b_shard = lax.dynamic_slice(b_mod, (0, chip * nmod), (b_mod.shape[0], nmod))[:, None, :]
    mod_shard = _mod_fwd(c_rows, w_mod, b_shard, hosted=plan.hosted("mod_fwd"))
    plan.after("mod_fwd")
    got2 = _ag8("ag8_mod", mod_shard.reshape(-1, nmod))
    mod_full = got2[0::2].reshape(N_CHIP, 4, MOD_ROWS, nmod).transpose(1, 2, 0, 3).reshape(4, MOD_ROWS, 3 * d)
    mod_lat = lax.dynamic_slice(mod_full, (0, dev * batch, 0), (4, batch, 3 * d))
    mod_ctx = mod_full[:, CTX_ROW]
    mods = []
    for i in range(4):
        mods.append(tuple(
            jnp.stack([mod_lat[i, :, j * d:(j + 1) * d], jnp.broadcast_to(mod_ctx[i, j * d:(j + 1) * d], (batch, d))],
                      axis=1)[:, :, None, :] for j in range(3)))

    wk = dict(full_small)
    wk.update(norm_g=norm_g, cv_db=cv_db, cv_ln_g=cv_ln_g, cv_ln_b=cv_ln_b, ml_nope_norm=ml_nope_norm[0],
              ml_rope_norm=ml_rope_norm[0], ch_w_s=ch_w_s[0], ch_b_s=ch_b_s[0])
    plan.small = _prep_weights(wk)
    xm = jnp.concatenate([x, ctx], axis=1)
    loss_part, grad_x, dmods, g = _local_step(xm, loss_target, mods, plan, lat_len)
    g = _unprep_grads(g)
    plan.alone("swap", L0)

    lat_rows, ctx_rows = [], []
    for i in range(4):
        dsh, dsc, dgt = dmods[i]
        lat_rows.append(jnp.concatenate([dsh[:, 0, 0], dsc[:, 0, 0], dgt[:, 0, 0]], axis=1))
        zero = jnp.zeros((d,), F32)
        cs = [jnp.sum(t[:, 1, 0], axis=0) if ok else zero
              for t, ok in zip((dsh, dsc, dgt), (i <= 2, i <= 2, i <= 1))]
        ctx_rows.append(jnp.concatenate(cs, axis=0)[None])
    dmod_dev = jnp.concatenate(lat_rows + ctx_rows, axis=0)
    dmod_dev = jnp.pad(dmod_dev, ((0, (-dmod_dev.shape[0]) % 8), (0, 0)))
    gather3 = _ag8_copies(dmod_dev)
    _run_hosted("ag8_dmod", _merge_hosted([gather3, plan.hosted("ag8_dmod")]))
    plan.after("ag8_dmod")
    got3 = gather3.results[0]
    dlat = got3[:, :4 * batch].reshape(N_DEV, 4, batch, 3 * d).transpose(1, 0, 2, 3).reshape(4, N_DEV * batch, 3 * d)
    dctx_parts = got3[:, 4 * batch:4 * batch + 4].transpose(1, 0, 2)
    g_b_mod, dctx = _mod_bwd_rows(dlat, dctx_parts)
    d_rows = jnp.concatenate([dlat, dctx, jnp.zeros((4, MOD_ROWS - CTX_ROW - 1, 3 * d), F32)], axis=1)
    d_rows = lax.dynamic_slice_in_dim(d_rows, chip * nmod, nmod, axis=2)
    g_w_mod, dcc_part = _mod_bwd_w(c_rows.T, d_rows, w_mod)

    wm2 = w_mod.reshape(-1, nmod)
    res_mod = _adamw("adamw_w_mod", None, g_w_mod.reshape(1, -1, nmod), wm2, M["w_mod"].reshape(-1, nmod),
                     V["w_mod"].reshape(-1, nmod), wm2.shape[0], hosted=plan.hosted("adamw_w_mod"))
    plan.after("adamw_w_mod")
    out = {"w_mod": tuple(r.reshape(w_mod.shape) for r in res_mod)}

    g_small_in = {n: g[n] for n in SMALL_SHARDED if n not in EARLY_SMALL}
    g_small_in.update(norm_g=g["norm_g"], cv_db=g["cv_db"], cv_ln_g=g["cv_ln_g"], cv_ln_b=g["cv_ln_b"],
                      ml_nope_norm=g["ml_nope_norm"], ml_rope_norm=g["ml_rope_norm"],
                      c_ctx=dcc_part.reshape(-1) * (mc == 0).astype(F32), loss=loss_part)
    small_names = list(g_small_in)
    pack4, spans4 = _pack_rows([g_small_in[n] for n in small_names], LANES, 128)
    gather4 = _ag8_copies(pack4)
    _run_hosted("ag8_small_grads", _merge_hosted([gather4, plan.hosted("ag8_small_grads")]))
    plan.after("ag8_small_grads")
    gs = dict(zip(small_names, _unpack_rows(_sum8(gather4.results[0]), spans4)))
    loss = gs["loss"][0, 0]
    gs.update(plan.early)
    gs["b_mod"] = g_b_mod[:, 0]
    for n in SMALL_SHARDED:
        wd = W[n].shape[-1]
        gs[n] = lax.dynamic_slice_in_dim(gs[n], chip * wd, wd, axis=1)
    upd_names = SMALL_REPLICATED + SMALL_SHARDED
    pw, spans_u = _pack_rows([W[n] for n in upd_names], LANES, 128)
    pm, _ = _pack_rows([M[n] for n in upd_names], LANES, 128)
    pv, _ = _pack_rows([V[n] for n in upd_names], LANES, 128)
    pg, _ = _pack_rows([gs[n].reshape(W[n].shape) for n in upd_names], LANES, 128)
    res_small = _adamw("adamw_small", None, pg[None], pw, pm, pv, pw.shape[0], hosted=plan.hosted("adamw_small"))
    plan.after("adamw_small")
    for n, vals in zip(upd_names, zip(*[_unpack_rows(r, spans_u) for r in res_small])):
        out[n] = vals
    out.update(plan.finish())

    outs = [loss, grad_x]
    for j in range(4):
        outs.extend(out[n][j] for n in WEIGHTS)
    return tuple(outs)
```

```python
import functools
import math

import jax
import jax.numpy as jnp
from jax import lax
from jax.experimental import pallas as pl
from jax.experimental.pallas import tpu as pltpu

F32 = jnp.float32
BF16 = jnp.bfloat16
MESH = pl.DeviceIdType.MESH

EPS = 1e-6
GRID_W = 64
CONV_WIDTH = 31
CONV_HALF = CONV_WIDTH // 2
CONV_PAD = 16
POOL_WINDOWS = (2, 4, 8, 16)
POOL_HALF = max(POOL_WINDOWS) // 2
HEADS = 8
NOPE = 128
ROPE = 64
HEAD_W = 256
VDIM = 128
KV_RANK = 256
Q_RANK = 384
ATT_SCALE = (NOPE + ROPE) ** -0.5
LN2 = math.log(2.0)
Q_PRESCALE = ATT_SCALE / LN2
ROPE_THETA = 10000.0
CHUNK = 128
CHUNK_GROUPS = 8
LANES = 128
TM = 256
TQ = 512
ATT_RQ = 128
ATT_KC = 256
VMEM_LIMIT = 56 * 1024 * 1024

ADAM_LR = 0.001
ADAM_B1 = 0.9
ADAM_B2 = 0.999
ADAM_EPS = 1e-08
ADAM_WD = 0.01
ADAM_STEP = 10


def _dot(a, b):
    return jnp.dot(a.astype(BF16), b.astype(BF16), preferred_element_type=F32)


def _dot_nt(a, b):
    return lax.dot_general(a.astype(BF16), b.astype(BF16), (((1,), (1,)), ((), ())), preferred_element_type=F32)


def _dot_tn(a, b):
    return lax.dot_general(a.astype(BF16), b.astype(BF16), (((0,), (0,)), ((), ())), preferred_element_type=F32)


@jax.custom_vjp
def _mm(a, w):
    return _dot(a, w)


def _mm_fwd(a, w):
    return _dot(a, w), (a, w)


def _mm_bwd(res, ct):
    a, w = res
    return _dot_nt(ct, w), _dot_tn(a, ct)


_mm.defvjp(_mm_fwd, _mm_bwd)


def _swap16_impl(x):
    n = x.shape[-1]
    ax = x.ndim - 1
    lane = lax.broadcasted_iota(jnp.int32, x.shape, ax)
    up = pltpu.roll(x, n - 16, ax)
    dn = pltpu.roll(x, 16, ax)
    return jnp.where((lane % 32) < 16, up, dn)


@jax.custom_vjp
def _swap16(x):
    return _swap16_impl(x)


_swap16.defvjp(lambda x: (_swap16_impl(x), None), lambda _, ct: (_swap16_impl(ct),))


def _rms(x, g, n=None):
    n = x.shape[-1] if n is None else n
    return x * lax.rsqrt(jnp.sum(x * x, axis=-1, keepdims=True) * (1.0 / n) + EPS) * g


def _layernorm(x, g, b):
    mu = jnp.mean(x, axis=-1, keepdims=True)
    xc = x - mu
    var = jnp.mean(xc * xc, axis=-1, keepdims=True)
    return xc * lax.rsqrt(var + EPS) * g + b


def _silu(x):
    return x * jax.nn.sigmoid(x)


def _rope(x, cos, sin):
    return x * cos + _swap16(x) * sin


ANY = pl.BlockSpec(memory_space=pl.ANY)


class _Hosted:
    def __init__(self, arrays, out_shapes, sems, start, wait, aliases=None):
        self.arrays, self.out_shapes, self.sems = list(arrays), list(out_shapes), list(sems)
        self.start, self.wait, self.aliases = start, wait, dict(aliases or {})
        self.results = None


def _merge_hosted(parts):
    parts = [p for p in parts if p is not None]
    if not parts:
        return None
    if len(parts) == 1:
        return parts[0]
    offs, a0, o0, s0 = [], 0, 0, 0
    for p in parts:
        offs.append((a0, o0, s0))
        a0, o0, s0 = a0 + len(p.arrays), o0 + len(p.out_shapes), s0 + len(p.sems)

    def run(which):
        def f(ins, outs, sems):
            for p, (a, o, s) in zip(parts, offs):
                getattr(p, which)(ins[a:a + len(p.arrays)], outs[o:o + len(p.out_shapes)], sems[s:s + len(p.sems)])
        return f

    aliases = {}
    for p, (a, o, _) in zip(parts, offs):
        aliases.update({a + i: o + j for i, j in p.aliases.items()})
    merged = _Hosted(sum((p.arrays for p in parts), []), sum((p.out_shapes for p in parts), []),
                     sum((p.sems for p in parts), []), run("start"), run("wait"), aliases)
    merged.parts, merged.offs = parts, offs
    return merged


def _deliver(hosted, results):
    hosted.results = list(results)
    for p, (_, o, _) in zip(getattr(hosted, "parts", []), getattr(hosted, "offs", [])):
        p.results = list(results[o:o + len(p.out_shapes)])


def _pcall(body, *, name, grid, in_specs, out_specs, out_shape, args, hosted=None, vmem_limit=True, scratch=()):
    n_in, n_out, n_scr = len(args), len(out_shape), len(scratch)
    kwargs = dict(scratch_shapes=list(scratch)) if scratch else {}
    if hosted is not None:
        nhi, nho, inner = len(hosted.arrays), len(hosted.out_shapes), body

        def body(*refs):
            ins, hin = refs[:n_in], refs[n_in:n_in + nhi]
            outs, hout = refs[n_in + nhi:n_in + nhi + n_out], refs[n_in + nhi + n_out:n_in + nhi + n_out + nho]
            own = refs[n_in + nhi + n_out + nho:n_in + nhi + n_out + nho + n_scr]
            sems = refs[n_in + nhi + n_out + nho + n_scr:]
            first, last = None, None
            for k, g in enumerate(grid):
                f, l = pl.program_id(k) == 0, pl.program_id(k) == g - 1
                first = f if first is None else jnp.logical_and(first, f)
                last = l if last is None else jnp.logical_and(last, l)

            @pl.when(first)
            def _():
                hosted.start(hin, hout, sems)

            inner(*ins, *outs, *own)

            @pl.when(last)
            def _():
                hosted.wait(hin, hout, sems)

        in_specs = list(in_specs) + [ANY] * nhi
        out_specs = list(out_specs) + [ANY] * nho
        out_shape = list(out_shape) + hosted.out_shapes
        args = list(args) + hosted.arrays
        kwargs = dict(scratch_shapes=list(scratch) + hosted.sems,
                      input_output_aliases={n_in + i: n_out + j for i, j in hosted.aliases.items()})
    params = dict(dimension_semantics=("arbitrary",) * len(grid))
    if vmem_limit:
        params["vmem_limit_bytes"] = VMEM_LIMIT
    res = pl.pallas_call(body, name=name, grid=grid, in_specs=list(in_specs), out_specs=list(out_specs),
                         out_shape=list(out_shape), compiler_params=pltpu.CompilerParams(**params), **kwargs)(*args)
    if hosted is not None:
        _deliver(hosted, res[n_out:])
    return list(res[:n_out])


def _run_hosted(name, hosted):
    nhi, nho = len(hosted.arrays), len(hosted.out_shapes)

    def body(*refs):
        ins, outs, sems = refs[:nhi], refs[nhi:nhi + nho], refs[nhi + nho:]
        hosted.start(ins, outs, sems)
        hosted.wait(ins, outs, sems)

    res = pl.pallas_call(body, name=name, in_specs=[ANY] * nhi, out_specs=[ANY] * nho, out_shape=hosted.out_shapes,
                         scratch_shapes=hosted.sems, input_output_aliases=hosted.aliases)(*hosted.arrays)
    _deliver(hosted, res)
    return list(res)


def _const_spec(shape, single=False):
    nd = len(shape)
    if single:
        return pl.BlockSpec(shape, lambda b, i: (0,) * nd, pipeline_mode=pl.Buffered(1))
    return pl.BlockSpec(shape, lambda b, i: (0,) * nd)


def _tile_spec(arr, n_lat_tiles, lat_only=False):
    bt, _, cw = arr.shape
    if lat_only:
        return pl.BlockSpec((1, TM, cw), lambda b, i: (b if bt > 1 else 0, jnp.minimum(i, n_lat_tiles - 1), 0))
    return pl.BlockSpec((1, TM, cw), lambda b, i: (b if bt > 1 else 0, i, 0))


def _eparam_spec(arr, n_lat_tiles):
    cw = arr.shape[-1]
    return pl.BlockSpec((1, 1, 1, cw), lambda b, i: (b, (i >= n_lat_tiles).astype(jnp.int32), 0, 0))


def _stage_fwd(name, *, pre, post, wsel, splits, tiles, eparams, sparams, weights, out_widths, out_dtypes,
               batch, n_tiles, n_lat_tiles, hosted=None):
    nt, ne, ns, nw = len(tiles), len(eparams), len(sparams), len(weights)

    def body(*refs):
        t_refs = refs[:nt]
        e_refs = refs[nt:nt + ne]
        s_refs = refs[nt + ne:nt + ne + ns]
        w_refs = refs[nt + ne + ns:nt + ne + ns + nw]
        o_refs = refs[nt + ne + ns + nw:]
        tv = [r[0].astype(F32) for r in t_refs]
        ev = [r[0, 0] for r in e_refs]
        sv = [r[...] for r in s_refs]
        a = pre(tv, ev, sv)
        z = [_dot(a[wsel[j]], w_refs[j][...]) for j in range(nw)]
        if post is None:
            outs = [z[j][:, s:s + w] for (j, s, w) in splits]
        else:
            outs = post(z, tv, ev, sv)
        for o_ref, o in zip(o_refs, outs):
            o_ref[0] = o.astype(o_ref.dtype)

    in_specs = ([_tile_spec(t, n_lat_tiles) for t in tiles] + [_eparam_spec(e, n_lat_tiles) for e in eparams]
                + [_const_spec(s.shape) for s in sparams] + [_const_spec(w.shape, single=True) for w in weights])
    out_shape = [jax.ShapeDtypeStruct((batch, n_tiles * TM, w), dt) for w, dt in zip(out_widths, out_dtypes)]
    out_specs = [pl.BlockSpec((1, TM, w), lambda b, i: (b, i, 0)) for w in out_widths]
    return _pcall(body, name=name, grid=(batch, n_tiles), in_specs=in_specs, out_specs=out_specs,
                  out_shape=out_shape, args=[*tiles, *eparams, *sparams, *weights], hosted=hosted)


def _stage_bwd(name, *, pre, post, wsel, splits, tiles, tile_diff, eparams, sparams, weights, cots, cot_lat_only,
               batch, n_tiles, n_lat_tiles, add=None, add_lat_only=False, hosted=None, w_col_stack=None,
               dt_lat_only=False):
    nt, ne, ns, nw, nc = len(tiles), len(eparams), len(sparams), len(weights), len(cots)
    diff_idx = [k for k in range(nt) if tile_diff[k]]
    nd = len(diff_idx)
    has_add = add is not None
    w_col_stack = w_col_stack or [None] * nw

    def body(*refs):
        pos = 0
        t_refs = refs[pos:pos + nt]; pos += nt
        e_refs = refs[pos:pos + ne]; pos += ne
        s_refs = refs[pos:pos + ns]; pos += ns
        w_refs = refs[pos:pos + nw]; pos += nw
        c_refs = refs[pos:pos + nc]; pos += nc
        if has_add:
            add_ref = refs[pos]; pos += 1
        dt_refs = refs[pos:pos + nd]; pos += nd
        de_refs = refs[pos:pos + ne]; pos += ne
        ds_refs = refs[pos:pos + ns]; pos += ns
        dw_refs = refs[pos:pos + nw]; pos += nw

        b = pl.program_id(0)
        i = pl.program_id(1)
        is_lat = i < n_lat_tiles
        tv = [r[0].astype(F32) for r in t_refs]
        ev = tuple(r[0, 0] for r in e_refs)
        sv = tuple(r[...] for r in s_refs)
        dv0 = tuple(tv[k] for k in diff_idx)

        def merge(dv):
            full = list(tv)
            for k, v in zip(diff_idx, dv):
                full[k] = v
            return full

        def pre_f(dv, ev_, sv_):
            return tuple(pre(merge(dv), list(ev_), list(sv_)))

        a, vjp_pre = jax.vjp(pre_f, dv0, ev, sv)
        cv = []
        for c_ref, lat in zip(c_refs, cot_lat_only):
            c = c_ref[0].astype(F32)
            cv.append(jnp.where(is_lat, c, 0.0) if lat else c)
        if post is None:
            dz = []
            for j in range(nw):
                parts = [cv[k] for k, (jj, _, _) in enumerate(splits) if jj == j]
                dz.append(parts[0] if len(parts) == 1 else jnp.concatenate(parts, axis=1))
            dt2 = de2 = ds2 = None
        else:
            z = tuple(_dot(a[wsel[j]], w_refs[j][...]) for j in range(nw))

            def post_f(z_, dv, ev_, sv_):
                return tuple(post(list(z_), merge(dv), list(ev_), list(sv_)))

            _, vjp_post = jax.vjp(post_f, z, dv0, ev, sv)
            dz, dt2, de2, ds2 = vjp_post(tuple(cv))
        da = [None] * len(a)
        dws = []
        for j in range(nw):
            g = _dot_nt(dz[j], w_refs[j][...])
            da[wsel[j]] = g if da[wsel[j]] is None else da[wsel[j]] + g
            dws.append(_dot_tn(a[wsel[j]], dz[j]))
        da = tuple(jnp.zeros_like(a[k]) if da[k] is None else da[k] for k in range(len(a)))
        dt1, de1, ds1 = vjp_pre(da)

        def plus(u, v):
            return u if v is None else u + v

        for k in range(nd):
            val = plus(dt1[k], None if dt2 is None else dt2[k])
            if has_add and k == 0:
                addv = add_ref[0].astype(F32)
                val = val + (jnp.where(is_lat, addv, 0.0) if add_lat_only else addv)
            if dt_lat_only:
                @pl.when(is_lat)
                def _(k=k, val=val):
                    dt_refs[k][0] = val.astype(dt_refs[k].dtype)
            else:
                dt_refs[k][0] = val.astype(dt_refs[k].dtype)

        seg_first = jnp.logical_or(i == 0, i == n_lat_tiles)
        for k in range(ne):
            val = plus(de1[k], None if de2 is None else de2[k])

            @pl.when(seg_first)
            def _(k=k, val=val):
                de_refs[k][0, 0] = val

            @pl.when(jnp.logical_not(seg_first))
            def _(k=k, val=val):
                de_refs[k][0, 0] += val

        first = jnp.logical_and(b == 0, i == 0)
        acc = [(ds_refs[k], plus(ds1[k], None if ds2 is None else ds2[k])) for k in range(ns)]
        for j in range(nw):
            if w_col_stack[j]:
                cw = dws[j].shape[1] // w_col_stack[j]
                acc += [(dw_refs[j].at[c], dws[j][:, c * cw:(c + 1) * cw]) for c in range(w_col_stack[j])]
            else:
                acc.append((dw_refs[j], dws[j]))
        for ref, val in acc:
            @pl.when(first)
            def _(ref=ref, val=val):
                ref[...] = val

            @pl.when(jnp.logical_not(first))
            def _(ref=ref, val=val):
                ref[...] += val

    in_specs = ([_tile_spec(t, n_lat_tiles) for t in tiles] + [_eparam_spec(e, n_lat_tiles) for e in eparams]
                + [_const_spec(s.shape) for s in sparams] + [_const_spec(w.shape, single=True) for w in weights]
                + [_tile_spec(c, n_lat_tiles, lat) for c, lat in zip(cots, cot_lat_only)])
    args = [*tiles, *eparams, *sparams, *weights, *cots]
    if has_add:
        in_specs.append(_tile_spec(add, n_lat_tiles, add_lat_only))
        args.append(add)
    dt_tiles = n_lat_tiles if dt_lat_only else n_tiles
    out_shape = [jax.ShapeDtypeStruct((batch, dt_tiles * TM, tiles[k].shape[-1]), F32) for k in diff_idx]
    out_specs = [pl.BlockSpec((1, TM, tiles[k].shape[-1]), lambda b, i: (b, jnp.minimum(i, dt_tiles - 1), 0))
                 for k in diff_idx]
    out_shape += [jax.ShapeDtypeStruct(e.shape, F32) for e in eparams]
    out_specs += [_eparam_spec(e, n_lat_tiles) for e in eparams]
    out_shape += [jax.ShapeDtypeStruct(s.shape, F32) for s in sparams]
    out_specs += [_const_spec(s.shape) for s in sparams]
    dw_shapes = [(n, w.shape[0], w.shape[1] // n) if n else w.shape for w, n in zip(weights, w_col_stack)]
    out_shape += [jax.ShapeDtypeStruct(s, F32) for s in dw_shapes]
    out_specs += [_const_spec(s, single=True) for s in dw_shapes]
    res = _pcall(body, name=name, grid=(batch, n_tiles), in_specs=in_specs, out_specs=out_specs,
                 out_shape=out_shape, args=args, hosted=hosted)
    return res[:nd], res[nd:nd + ne], res[nd + ne:nd + ne + ns], res[nd + ne + ns:]


def _pre_adaln(tv, ev, sv):
    x = tv[0]
    sh, sc = ev[0], ev[1]
    return [_rms(x, sv[0]) * (1.0 + sc) + sh]


def _post_residual(x_index):
    def post(z, tv, ev, sv):
        return [tv[x_index] + ev[-1] * z[0]]
    return post


def _pre_conv_out(tv, ev, sv):
    c1, gg = tv[0], tv[1]
    return [_silu(_layernorm(c1, sv[0], sv[1])) * _silu(gg)]


def _pre_pool_out(tv, ev, sv):
    pooled, gg = tv[0], tv[1]
    w_grp, scale = sv[0], sv[1]
    gw = w_grp.shape[-1]
    y = jnp.concatenate([_mm(pooled[:, k * gw:(k + 1) * gw], w_grp[k]) for k in range(w_grp.shape[0])], axis=1)
    return [y * scale * _silu(gg)]


def _pre_rms_only(tv, ev, sv):
    return [_rms(tv[0], sv[0])]


def _post_mla_keys(z, tv, ev, sv):
    krp, cos, sin = tv[1], tv[2], tv[3]
    nope_g, rope_g = sv[1], sv[2]
    kv = z[0]
    kr = _rope(_rms(krp, rope_g, ROPE), cos, sin)
    ks, vs = [], []
    for h in range(HEADS):
        ks.append(_rms(kv[:, h * 2 * NOPE:h * 2 * NOPE + NOPE], nope_g))
        ks.append(kr)
        vs.append(kv[:, h * 2 * NOPE + NOPE:(h + 1) * 2 * NOPE])
    return [jnp.concatenate(ks, axis=1), jnp.concatenate(vs, axis=1)]


def _post_mla_queries(z, tv, ev, sv):
    cos, sin = tv[1], tv[2]
    nope_g, rope_g = sv[1], sv[2]
    q = z[0]
    qs = []
    for h in range(HEADS):
        qs.append(_rms(q[:, h * HEAD_W:h * HEAD_W + NOPE], nope_g))
        qs.append(_rope(_rms(q[:, h * HEAD_W + NOPE:(h + 1) * HEAD_W], rope_g, ROPE), cos, sin))
    return [jnp.concatenate(qs, axis=1) * Q_PRESCALE]


def _pre_mla_out(tv, ev, sv):
    return [tv[0] * _silu(tv[1])]


def _pre_chunk_out(tv, ev, sv):
    u, v, gg = tv[0], tv[1], tv[2]
    ln_g, ln_b, w_s, b_s = sv
    vn = _layernorm(v, ln_g, ln_b)
    rows = []
    for n in range(vn.shape[0] // CHUNK):
        blk = vn[n * CHUNK:(n + 1) * CHUNK]
        cols = [_mm(w_s[g], blk[:, g * LANES:(g + 1) * LANES]) + b_s[:, g:g + 1] for g in range(CHUNK_GROUPS)]
        rows.append(jnp.concatenate(cols, axis=1))
    s = jnp.concatenate(rows, axis=0)
    return [u * s * _silu(gg)]


def _segments(lat_len, tot_len):
    segs = [(0, lat_len)]
    if tot_len > lat_len:
        segs.append((lat_len, tot_len - lat_len))
    return segs


def _pad_rows(x):
    z = jnp.zeros((CONV_PAD, x.shape[1]), x.dtype)
    return jnp.concatenate([z, x, z], axis=0)


def _shifted(xp, j):
    n = xp.shape[0] - 2 * CONV_PAD
    if j != 0:
        xp = pltpu.roll(xp, (-j) % xp.shape[0], 0)
    return xp[CONV_PAD:CONV_PAD + n]


def _conv_fwd(a, bgate, dw, db, lat_len, hosted=None):
    batch, tot, e = a.shape
    segs = _segments(lat_len, tot)

    def body(a_ref, b_ref, dw_ref, db_ref, o_ref):
        w = dw_ref[...]
        for (s0, n) in segs:
            y = a_ref[0, s0:s0 + n, :] * jax.nn.sigmoid(b_ref[0, s0:s0 + n, :])
            yp = _pad_rows(y)
            acc = jnp.zeros_like(y) + db_ref[...]
            for k in range(CONV_WIDTH):
                acc = acc + _shifted(yp, k - CONV_HALF) * w[k:k + 1, :]
            o_ref[0, s0:s0 + n, :] = acc

    blk = pl.BlockSpec((1, tot, LANES), lambda b, cb: (b, 0, cb))
    return _pcall(
        body, name="conv_fwd", grid=(batch, e // LANES),
        in_specs=[blk, blk, pl.BlockSpec((CONV_WIDTH, LANES), lambda b, cb: (0, cb)),
                  pl.BlockSpec((1, LANES), lambda b, cb: (0, cb))],
        out_specs=[blk], out_shape=[jax.ShapeDtypeStruct(a.shape, F32)], args=[a, bgate, dw, db], hosted=hosted)[0]


def _conv_bwd(a, bgate, dw, dc1, lat_len, hosted=None):
    batch, tot, e = a.shape
    segs = _segments(lat_len, tot)

    def body(a_ref, b_ref, dw_ref, dc_ref, da_ref, dg_ref, ddw_ref, ddb_ref):
        b = pl.program_id(1)
        w = dw_ref[...]
        ddw_rows = [None] * CONV_WIDTH
        ddb = None
        for (s0, n) in segs:
            av = a_ref[0, s0:s0 + n, :]
            sg = jax.nn.sigmoid(b_ref[0, s0:s0 + n, :])
            y = av * sg
            dc = dc_ref[0, s0:s0 + n, :]
            yp, dcp = _pad_rows(y), _pad_rows(dc)
            dy = jnp.zeros_like(y)
            for k in range(CONV_WIDTH):
                j = k - CONV_HALF
                dy = dy + _shifted(dcp, -j) * w[k:k + 1, :]
                r = jnp.sum(dc * _shifted(yp, j), axis=0, keepdims=True)
                ddw_rows[k] = r if ddw_rows[k] is None else ddw_rows[k] + r
            r = jnp.sum(dc, axis=0, keepdims=True)
            ddb = r if ddb is None else ddb + r
            da_ref[0, s0:s0 + n, :] = dy * sg
            dg_ref[0, s0:s0 + n, :] = dy * av * sg * (1.0 - sg)

        @pl.when(b == 0)
        def _():
            ddw_ref[...] = jnp.zeros_like(ddw_ref)
            ddb_ref[...] = jnp.zeros_like(ddb_ref)

        for k in range(CONV_WIDTH):
            ddw_ref[k:k + 1, :] += ddw_rows[k]
        ddb_ref[...] += ddb

    blk = pl.BlockSpec((1, tot, LANES), lambda cb, b: (b, 0, cb))
    wspec = pl.BlockSpec((CONV_WIDTH, LANES), lambda cb, b: (0, cb))
    bspec = pl.BlockSpec((1, LANES), lambda cb, b: (0, cb))
    return _pcall(
        body, name="conv_bwd", grid=(e // LANES, batch),
        in_specs=[blk, blk, wspec, blk],
        out_specs=[blk, blk, wspec, bspec],
        out_shape=[jax.ShapeDtypeStruct(a.shape, F32), jax.ShapeDtypeStruct(a.shape, F32),
                   jax.ShapeDtypeStruct((CONV_WIDTH, e), F32), jax.ShapeDtypeStruct((1, e), F32)],
        args=[a, bgate, dw, dc1], hosted=hosted)


def _pool_taps(group):
    half = lax.shift_left(jnp.int32(1), group)
    taps = []
    for j in range(-POOL_HALF, POOL_HALF):
        inside = jnp.logical_and(j >= -half, j < half)
        taps.append(jnp.where(inside, 1.0, 0.0).astype(F32))
    return taps, half


def _pool_counts(n, half, shape):
    t = lax.broadcasted_iota(jnp.int32, shape, 0)
    cnt = jnp.minimum(t + half, n) - jnp.maximum(t - half, 0)
    return cnt.astype(F32)


def _pool_fwd(v, lat_len, hosted=None):
    batch, tot, e = v.shape
    gw = e // len(POOL_WINDOWS)
    segs = _segments(lat_len, tot)

    def body(v_ref, o_ref):
        taps, half = _pool_taps(pl.program_id(1))
        for (s0, n) in segs:
            x = v_ref[0, s0:s0 + n, :]
            xp = _pad_rows(x)
            acc = jnp.zeros_like(x)
            for idx, j in enumerate(range(-POOL_HALF, POOL_HALF)):
                acc = acc + _shifted(xp, j) * taps[idx]
            o_ref[0, s0:s0 + n, :] = acc / _pool_counts(n, half, x.shape) - x

    blk = pl.BlockSpec((1, tot, gw), lambda b, g: (b, 0, g))
    return _pcall(body, name="pool_fwd", grid=(batch, len(POOL_WINDOWS)), in_specs=[blk], out_specs=[blk],
                  out_shape=[jax.ShapeDtypeStruct(v.shape, F32)], args=[v], hosted=hosted)[0]


def _pool_bwd(dp, lat_len):
    batch, tot, e = dp.shape
    gw = e // len(POOL_WINDOWS)
    segs = _segments(lat_len, tot)

    def body(d_ref, o_ref):
        taps, half = _pool_taps(pl.program_id(1))
        for (s0, n) in segs:
            d = d_ref[0, s0:s0 + n, :]
            dnp = _pad_rows(d / _pool_counts(n, half, d.shape))
            acc = jnp.zeros_like(d)
            for idx, j in enumerate(range(-POOL_HALF, POOL_HALF)):
                acc = acc + _shifted(dnp, -j) * taps[idx]
            o_ref[0, s0:s0 + n, :] = acc - d

    blk = pl.BlockSpec((1, tot, gw), lambda b, g: (b, 0, g))
    return pl.pallas_call(
        body, name="pool_bwd", grid=(batch, len(POOL_WINDOWS)), in_specs=[blk], out_specs=blk,
        out_shape=jax.ShapeDtypeStruct(dp.shape, F32),
        compiler_params=pltpu.CompilerParams(dimension_semantics=("arbitrary", "arbitrary"),
                                             vmem_limit_bytes=VMEM_LIMIT),
    )(dp)


def _attn_fwd(q, k, v, hosted=None):
    batch, lq, _ = q.shape
    tk = k.shape[1]
    tq = min(TQ, lq)

    def body(q_ref, k_ref, v_ref, o_ref, lse_ref):
        s2 = _dot_nt(q_ref[0], k_ref[0])
        m2 = jnp.max(s2, axis=-1, keepdims=True)
        e = jnp.exp2(s2 - m2)
        l = jnp.sum(e, axis=-1, keepdims=True)
        o_ref[0] = _dot(e, v_ref[0]) / l
        lse_ref[0, 0] = m2 + jnp.log2(l)

    return _pcall(
        body, name="attn_fwd", grid=(batch, HEADS, lq // tq),
        in_specs=[pl.BlockSpec((1, tq, HEAD_W), lambda b, h, i: (b, i, h)),
                  pl.BlockSpec((1, tk, HEAD_W), lambda b, h, i: (b, 0, h)),
                  pl.BlockSpec((1, tk, VDIM), lambda b, h, i: (b, 0, h))],
        out_specs=[pl.BlockSpec((1, tq, VDIM), lambda b, h, i: (b, i, h)),
                   pl.BlockSpec((1, 1, tq, 1), lambda b, h, i: (b, h, i, 0))],
        out_shape=[jax.ShapeDtypeStruct((batch, lq, HEADS * VDIM), F32),
                   jax.ShapeDtypeStruct((batch, HEADS, lq, 1), F32)], args=[q, k, v], hosted=hosted)


def _attn_bwd(q, k, v, o, lse, do, hosted=None):
    batch, lq, _ = q.shape
    tk = k.shape[1]
    tq = min(TQ, lq)

    def body(q_ref, k_ref, v_ref, o_ref, lse_ref, do_ref, dq_ref, dk_ref, dv_ref, p_scr, ds_scr):
        i = pl.program_id(2)
        nr = tq // ATT_RQ
        rows = [slice(r * ATT_RQ, (r + 1) * ATT_RQ) for r in range(nr)]
        qv = [q_ref[0, rw, :] for rw in rows]
        dob = [do_ref[0, rw, :].astype(BF16) for rw in rows]
        row_lse = [lse_ref[0, 0, rw, :] for rw in rows]
        delta = [jnp.sum(do_ref[0, rw, :] * o_ref[0, rw, :], axis=-1, keepdims=True) for rw in rows]
        for c in range(tk // ATT_KC):
            keys = slice(c * ATT_KC, (c + 1) * ATT_KC)
            kc, vc = k_ref[0, keys, :], v_ref[0, keys, :]
            for r in range(nr):
                p = jnp.exp2(_dot_nt(qv[r], kc) - row_lse[r])
                dp = _dot_nt(dob[r], vc)
                p_scr[rows[r], keys] = p.astype(BF16)
                ds_scr[rows[r], keys] = (p * (dp - delta[r]) * LN2).astype(BF16)
        dq_ref[0] = _dot(ds_scr[...], k_ref[0])
        dk = _dot_tn(ds_scr[...], q_ref[0])
        dv = _dot_tn(p_scr[...], do_ref[0])

        @pl.when(i == 0)
        def _():
            dk_ref[0] = dk
            dv_ref[0] = dv

        @pl.when(i != 0)
        def _():
            dk_ref[0] += dk
            dv_ref[0] += dv

    return _pcall(
        body, name="attn_bwd", grid=(batch, HEADS, lq // tq),
        in_specs=[pl.BlockSpec((1, tq, HEAD_W), lambda b, h, i: (b, i, h)),
                  pl.BlockSpec((1, tk, HEAD_W), lambda b, h, i: (b, 0, h)),
                  pl.BlockSpec((1, tk, VDIM), lambda b, h, i: (b, 0, h)),
                  pl.BlockSpec((1, tq, VDIM), lambda b, h, i: (b, i, h)),
                  pl.BlockSpec((1, 1, tq, 1), lambda b, h, i: (b, h, i, 0)),
                  pl.BlockSpec((1, tq, VDIM), lambda b, h, i: (b, i, h))],
        out_specs=[pl.BlockSpec((1, tq, HEAD_W), lambda b, h, i: (b, i, h)),
                   pl.BlockSpec((1, tk, HEAD_W), lambda b, h, i: (b, 0, h)),
                   pl.BlockSpec((1, tk, VDIM), lambda b, h, i: (b, 0, h))],
        out_shape=[jax.ShapeDtypeStruct(q.shape, F32), jax.ShapeDtypeStruct(k.shape, F32),
                   jax.ShapeDtypeStruct(v.shape, F32)],
        args=[q, k, v, o, lse, do], hosted=hosted,
        scratch=[pltpu.VMEM((tq, tk), BF16), pltpu.VMEM((tq, tk), BF16)])


def _loss_kernel(y, target):
    batch, lq, d = y.shape

    def body(y_ref, t_ref, l_ref, dy_ref):
        first = jnp.logical_and(pl.program_id(0) == 0, pl.program_id(1) == 0)
        err = y_ref[0] - t_ref[0]
        dy_ref[0] = err * (1.0 / d)
        part = jnp.zeros((1, LANES), F32) + jnp.sum(err * err) * (0.5 / d)

        @pl.when(first)
        def _():
            l_ref[...] = part

        @pl.when(jnp.logical_not(first))
        def _():
            l_ref[...] += part

    blk = pl.BlockSpec((1, TM, d), lambda b, i: (b, i, 0))
    return pl.pallas_call(
        body, name="loss_head", grid=(batch, lq // TM), in_specs=[blk, blk],
        out_specs=[pl.BlockSpec((1, LANES), lambda b, i: (0, 0)), blk],
        out_shape=[jax.ShapeDtypeStruct((1, LANES), F32), jax.ShapeDtypeStruct(y.shape, F32)],
        compiler_params=pltpu.CompilerParams(dimension_semantics=("arbitrary", "arbitrary")),
    )(y, target)


def _rope_tables(lat_len, ctx_len):
    rows = lat_len // GRID_W
    row_id = jnp.repeat(jnp.arange(rows), GRID_W).astype(F32)
    col_id = jnp.tile(jnp.arange(GRID_W), rows).astype(F32)
    axis_dim = ROPE // 2
    freqs = ROPE_THETA ** (-jnp.arange(0, axis_dim, 2, dtype=F32) / axis_dim)
    ar = row_id[:, None] * freqs
    ac = col_id[:, None] * freqs
    cr, sr, cc, sc = jnp.cos(ar), jnp.sin(ar), jnp.cos(ac), jnp.sin(ac)
    pad = jnp.zeros((lat_len, LANES - ROPE), F32)
    cos = jnp.concatenate([cr, cr, cc, cc, pad], axis=1)
    sin = jnp.concatenate([-sr, sr, -sc, sc, pad], axis=1)
    ident = jnp.concatenate([jnp.ones((ctx_len, ROPE), F32), jnp.zeros((ctx_len, LANES - ROPE), F32)], axis=1)
    cos = jnp.concatenate([cos, ident], axis=0)
    sin = jnp.concatenate([sin, jnp.zeros((ctx_len, LANES), F32)], axis=0)
    return cos[None], sin[None]


def _prep_weights(w):
    p = dict(w)
    kvc = KV_RANK + ROPE
    if "ml_w_in" in w:
        wi = w["ml_w_in"]
        p["ml_w_in"] = jnp.concatenate(
            [wi[:, :kvc], jnp.zeros((wi.shape[0], LANES - ROPE), wi.dtype), wi[:, kvc:]], axis=1)
    if "ml_w_uq" in w:
        uq = w["ml_w_uq"].reshape(Q_RANK, HEADS, NOPE + ROPE)
        p["ml_w_uq"] = jnp.pad(uq, ((0, 0), (0, 0), (0, HEAD_W - NOPE - ROPE))).reshape(Q_RANK, HEADS * HEAD_W)
    if "ml_rope_norm" in w:
        p["ml_rope_norm"] = jnp.pad(w["ml_rope_norm"], ((0, 0), (0, LANES - ROPE)))
    return p


def _unprep_grads(g):
    out = dict(g)
    kvc = KV_RANK + ROPE
    if "ml_w_in" in g:
        wi = g["ml_w_in"]
        out["ml_w_in"] = jnp.concatenate([wi[:, :kvc], wi[:, kvc + LANES - ROPE:]], axis=1)
    if "ml_w_uq" in g:
        uq = g["ml_w_uq"].reshape(Q_RANK, HEADS, HEAD_W)
        out["ml_w_uq"] = uq[:, :, :NOPE + ROPE].reshape(Q_RANK, HEADS * (NOPE + ROPE))
    if "ml_rope_norm" in g:
        out["ml_rope_norm"] = g["ml_rope_norm"][:, :ROPE]
    return out


LAYER_WEIGHTS = (("cv_w_in", "cv_w_out"), ("pl_w_in", "pl_w_grp", "pl_w_out"),
                 ("ml_w_in", "ml_w_uq", "ml_w_ukv", "ml_w_out"), ("ch_w_in", "ch_w_out"))


class _LocalPlan:
    def __init__(self, w):
        self.small = w
        self.grads = {}

    def weights(self, layer):
        return {n: self.small[n] for n in LAYER_WEIGHTS[layer]}

    def hosted(self, tag):
        return None

    def after(self, tag):
        pass

    def note(self, values):
        pass

    def layer_grads(self, layer, grads):
        self.grads.update(grads)


def _local_step(xm, target, mods, plan, lat_len):
    batch, tot, d = xm.shape
    e = d
    n_all, n_lat = tot // TM, lat_len // TM
    cos, sin = _rope_tables(lat_len, tot - lat_len)
    g = {}
    w = dict(plan.small)

    def hosting(tag, fn, *args, **kwargs):
        out = fn(*args, hosted=plan.hosted(tag), **kwargs)
        plan.after(tag)
        return out

    def s1_splits(widths):
        out, s = [], 0
        for wd in widths:
            out.append((0, s, wd))
            s += wd
        return out

    def fwd_in(name, x, mod, gi, wname, widths, n_tiles):
        return hosting(name, _stage_fwd, name, pre=_pre_adaln, post=None, wsel=[0], splits=s1_splits(widths),
                       tiles=[x], eparams=[mod[0], mod[1]], sparams=[w["norm_g"][gi:gi + 1]], weights=[w[wname]],
                       out_widths=widths, out_dtypes=[F32] * len(widths), batch=batch, n_tiles=n_tiles,
                       n_lat_tiles=n_lat)

    def bwd_in(name, x, mod, gi, wname, widths, n_tiles, cots, lat_only, add, add_lat_only, stack=None,
               dx_lat_only=False):
        (dx,), (dsh, dsc), (dg,), (dw,) = hosting(
            name, _stage_bwd, name, pre=_pre_adaln, post=None, wsel=[0], splits=s1_splits(widths), tiles=[x],
            tile_diff=[True], eparams=[mod[0], mod[1]], sparams=[w["norm_g"][gi:gi + 1]], weights=[w[wname]],
            cots=cots, cot_lat_only=lat_only, batch=batch, n_tiles=n_tiles, n_lat_tiles=n_lat, add=add,
            add_lat_only=add_lat_only, w_col_stack=[stack], dt_lat_only=dx_lat_only)
        return dx, dsh, dsc, dg, dw

    def fwd_out(name, pre, tiles, mod, sparams, wname, n_tiles):
        return hosting(name, _stage_fwd, name, pre=pre, post=_post_residual(len(tiles) - 1), wsel=[0], splits=None,
                       tiles=tiles, eparams=[mod[2]], sparams=sparams, weights=[w[wname]], out_widths=[d],
                       out_dtypes=[F32], batch=batch, n_tiles=n_tiles, n_lat_tiles=n_lat)[0]

    def bwd_out(name, pre, tiles, mod, sparams, wname, n_tiles, cot):
        diff = [True] * (len(tiles) - 1) + [False]
        dts, (dgt,), dss, (dw,) = hosting(
            name, _stage_bwd, name, pre=pre, post=_post_residual(len(tiles) - 1), wsel=[0], splits=None, tiles=tiles,
            tile_diff=diff, eparams=[mod[2]], sparams=sparams, weights=[w[wname]], cots=[cot], cot_lat_only=[False],
            batch=batch, n_tiles=n_tiles, n_lat_tiles=n_lat)
        return dts, dgt, dss, dw

    w.update(plan.weights(0))
    cv_s = [w["cv_ln_g"], w["cv_ln_b"]]
    a0, b0, g0 = fwd_in("cv_in_fwd", xm, mods[0], 0, "cv_w_in", [e, e, e], n_all)
    c1 = hosting("conv_fwd", _conv_fwd, a0, b0, w["cv_dw"], w["cv_db"], lat_len)
    x1 = fwd_out("cv_out_fwd", _pre_conv_out, [c1, g0, xm], mods[0], cv_s, "cv_w_out", n_all)

    w.update(plan.weights(1))
    pl_s = [w["pl_w_grp"], w["pl_scale"]]
    v1, g1 = fwd_in("pl_in_fwd", x1, mods[1], 1, "pl_w_in", [e, e], n_all)
    pooled = hosting("pool_fwd", _pool_fwd, v1, lat_len)
    x2 = fwd_out("pl_out_fwd", _pre_pool_out, [pooled, g1, x1], mods[1], pl_s, "pl_w_out", n_all)

    w.update(plan.weights(2))
    ml_widths = [KV_RANK, LANES, Q_RANK, HEADS * VDIM]
    ckv, krp, cq, g2 = fwd_in("ml_in_fwd", x2, mods[2], 2, "ml_w_in", ml_widths, n_all)
    k_s = [w["ml_kv_norm"], w["ml_nope_norm"][1:2], w["ml_rope_norm"][1:2]]
    q_s = [w["ml_q_norm"], w["ml_nope_norm"][0:1], w["ml_rope_norm"][0:1]]
    kk, vv = hosting("ml_keys_fwd", _stage_fwd, "ml_keys_fwd", pre=_pre_rms_only, post=_post_mla_keys, wsel=[0],
                     splits=None, tiles=[ckv, krp, cos, sin], eparams=[], sparams=k_s, weights=[w["ml_w_ukv"]],
                     out_widths=[HEADS * HEAD_W, HEADS * VDIM], out_dtypes=[BF16, BF16], batch=batch,
                     n_tiles=n_all, n_lat_tiles=n_lat)
    (qq,) = _stage_fwd("ml_queries_fwd", pre=_pre_rms_only, post=_post_mla_queries, wsel=[0], splits=None,
                       tiles=[cq, cos, sin], eparams=[], sparams=q_s, weights=[w["ml_w_uq"]],
                       out_widths=[HEADS * HEAD_W], out_dtypes=[BF16], batch=batch, n_tiles=n_lat,
                       n_lat_tiles=n_lat)
    att, lse = hosting("attn_fwd", _attn_fwd, qq, kk, vv)
    x3 = fwd_out("ml_out_fwd", _pre_mla_out, [att, g2, x2], mods[2], [], "ml_w_out", n_lat)

    w.update(plan.weights(3))
    ch_s = [w["ch_ln_g"], w["ch_ln_b"], w["ch_w_s"], w["ch_b_s"]]
    u3, v3, g3 = fwd_in("ch_in_fwd", x3, mods[3], 3, "ch_w_in", [e, e, e], n_lat)
    x4 = fwd_out("ch_out_fwd", _pre_chunk_out, [u3, v3, g3, x3], mods[3], ch_s, "ch_w_out", n_lat)

    loss_part, dy = _loss_kernel(x4, target)

    dmods = [None] * 4
    dnorm = [None] * 4
    big = {}
    (du, dv, dg), dgt, (g["ch_ln_g"], g["ch_ln_b"], g["ch_w_s"], g["ch_b_s"]), big["ch_w_out"] = bwd_out(
        "ch_out_bwd", _pre_chunk_out, [u3, v3, g3, x3], mods[3], ch_s, "ch_w_out", n_lat, dy)
    plan.note({n: g[n] for n in ("ch_ln_g", "ch_ln_b", "ch_w_s", "ch_b_s")})
    dx3, dsh, dsc, dnorm[3], big["ch_w_in"] = bwd_in("ch_in_bwd", x3, mods[3], 3, "ch_w_in", [e, e, e], n_lat,
                                                     [du, dv, dg], [False] * 3, dy, False, stack=N_CHIP)
    dmods[3] = (dsh, dsc, dgt)
    plan.layer_grads(3, big)

    big = {}
    (datt, dg), dgt, _, big["ml_w_out"] = bwd_out("ml_out_bwd", _pre_mla_out, [att, g2, x2], mods[2], [],
                                                  "ml_w_out", n_lat, dx3)
    dq, dk, dvv = hosting("attn_bwd", _attn_bwd, qq, kk, vv, att, lse, datt)
    (dcq,), _, (g["ml_q_norm"], dnope0, drope0), (big["ml_w_uq"],) = hosting(
        "ml_queries_bwd", _stage_bwd, "ml_queries_bwd", pre=_pre_rms_only, post=_post_mla_queries, wsel=[0],
        splits=None, tiles=[cq, cos, sin], tile_diff=[True, False, False], eparams=[], sparams=q_s,
        weights=[w["ml_w_uq"]], cots=[dq], cot_lat_only=[False], batch=batch, n_tiles=n_lat, n_lat_tiles=n_lat)
    (dckv, dkrp), _, (g["ml_kv_norm"], dnope1, drope1), (big["ml_w_ukv"],) = hosting(
        "ml_keys_bwd", _stage_bwd, "ml_keys_bwd", pre=_pre_rms_only, post=_post_mla_keys, wsel=[0], splits=None,
        tiles=[ckv, krp, cos, sin], tile_diff=[True, True, False, False], eparams=[], sparams=k_s,
        weights=[w["ml_w_ukv"]], cots=[dk, dvv], cot_lat_only=[False, False], batch=batch, n_tiles=n_all,
        n_lat_tiles=n_lat, w_col_stack=[N_CHIP])
    g["ml_nope_norm"] = jnp.concatenate([dnope0, dnope1], axis=0)
    g["ml_rope_norm"] = jnp.concatenate([drope0, drope1], axis=0)
    dx2, dsh, dsc, dnorm[2], big["ml_w_in"] = bwd_in("ml_in_bwd", x2, mods[2], 2, "ml_w_in", ml_widths, n_all,
                                                     [dckv, dkrp, dcq, dg], [False, False, True, True], dx3, True)
    dmods[2] = (dsh, dsc, dgt)
    plan.layer_grads(2, big)

    big = {}
    (dpooled, dg), dgt, (big["pl_w_grp"], g["pl_scale"]), big["pl_w_out"] = bwd_out(
        "pl_out_bwd", _pre_pool_out, [pooled, g1, x1], mods[1], pl_s, "pl_w_out", n_all, dx2)
    dv1 = _pool_bwd(dpooled, lat_len)
    dx1, dsh, dsc, dnorm[1], big["pl_w_in"] = bwd_in("pl_in_bwd", x1, mods[1], 1, "pl_w_in", [e, e], n_all,
                                                     [dv1, dg], [False] * 2, dx2, False, stack=N_CHIP)
    dmods[1] = (dsh, dsc, dgt)
    plan.layer_grads(1, big)

    big = {}
    (dc1, dg), dgt, (g["cv_ln_g"], g["cv_ln_b"]), big["cv_w_out"] = bwd_out(
        "cv_out_bwd", _pre_conv_out, [c1, g0, xm], mods[0], cv_s, "cv_w_out", n_all, dx1)
    da, db, g["cv_dw"], g["cv_db"] = hosting("conv_bwd", _conv_bwd, a0, b0, w["cv_dw"], dc1, lat_len)
    dx0, dsh, dsc, dnorm[0], big["cv_w_in"] = bwd_in("cv_in_bwd", xm, mods[0], 0, "cv_w_in", [e, e, e], n_all,
                                                     [da, db, dg], [False] * 3, dx1, False, stack=N_CHIP,
                                                     dx_lat_only=True)
    dmods[0] = (dsh, dsc, dgt)
    plan.layer_grads(0, big)
    g["norm_g"] = jnp.concatenate(dnorm, axis=0)
    return loss_part, dx0, dmods, g


N_DEV = 8
N_CHIP = 4
ANY = pl.BlockSpec(memory_space=pl.ANY)


def _my_place():
    return lax.axis_index("x"), lax.axis_index("y"), lax.axis_index("c")


def _flip(v, f):
    return 1 - v if f else v


def _ag8_copies(x):
    def plan(ins, outs, sems):
        mx, my, mc = _my_place()
        me = 4 * mx + 2 * my + mc
        sends, recvs = [], []
        for rel in range(1, N_DEV):
            peer = (_flip(mx, rel & 4), _flip(my, rel & 2), _flip(mc, rel & 1))
            src_dev = 4 * peer[0] + 2 * peer[1] + peer[2]
            sends.append(_remote(ins[0], outs[0].at[me], sems, rel - 1, peer))
            recvs.append(_remote(ins[0], outs[0].at[src_dev], sems, rel - 1, peer))
        return sends, recvs, [pltpu.make_async_copy(ins[0], outs[0].at[me], sems[2].at[0])]

    return _copies_hosted([x], [jax.ShapeDtypeStruct((N_DEV,) + x.shape, x.dtype)], (N_DEV - 1, N_DEV - 1, 1), plan)


def _ag8(name, x):
    return _run_hosted(name, _ag8_copies(x))[0]


def _chip_peers(mx, my, mc):
    out = []
    for rel in range(1, N_CHIP):
        px, py = _flip(mx, rel & 2), _flip(my, rel & 1)
        out.append((rel - 1, (px, py, mc), 2 * px + py))
    return out


def _half(mc, rows):
    return pl.ds(pl.multiple_of(mc * (rows // 2), 8), rows // 2)


def _copies_hosted(arrays, out_shapes, n_sems, plan, aliases=None):
    def start(ins, outs, sems):
        sends, _, locals_ = plan(ins, outs, sems)
        for cp in locals_ + sends:
            cp.start()

    def wait(ins, outs, sems):
        sends, recvs, locals_ = plan(ins, outs, sems)
        for cp in recvs:
            cp.wait_recv()
        for cp in sends:
            cp.wait_send()
        for cp in locals_:
            cp.wait()

    return _Hosted(arrays, out_shapes, [pltpu.SemaphoreType.DMA((k,)) for k in n_sems], start, wait, aliases)


def _remote(src, dst, sems, k, peer):
    return pltpu.make_async_remote_copy(src_ref=src, dst_ref=dst, send_sem=sems[0].at[k], recv_sem=sems[1].at[k],
                                        device_id=peer, device_id_type=MESH)


def _gather_ici(shards):
    n = len(shards)

    def plan(ins, outs, sems):
        mx, my, mc = _my_place()
        chip = 2 * mx + my
        sends, recvs, locals_ = [], [], []
        for a in range(n):
            rows = ins[a].shape[0]
            locals_.append(pltpu.make_async_copy(ins[a], outs[a].at[chip], sems[2].at[a]))
            for k, peer, pchip in _chip_peers(mx, my, mc):
                src = ins[a].at[_half(mc, rows)]
                sends.append(_remote(src, outs[a].at[chip, _half(mc, rows)], sems, 3 * a + k, peer))
                recvs.append(_remote(src, outs[a].at[pchip, _half(mc, rows)], sems, 3 * a + k, peer))
        return sends, recvs, locals_

    return _copies_hosted(shards, [jax.ShapeDtypeStruct((N_CHIP,) + s.shape, s.dtype) for s in shards],
                          (3 * n, 3 * n, n), plan)


def _sibling_fill(arrays, row_axis, chips_only_other):
    n = len(arrays)
    per = 3 if chips_only_other else 1

    def plan(ins, outs, sems):
        mx, my, mc = _my_place()
        sibling = (mx, my, 1 - mc)

        def views(a, core):
            rows = outs[a].shape[row_axis]
            if chips_only_other:
                return [outs[a].at[pchip, _half(core, rows)] for _, _, pchip in _chip_peers(mx, my, mc)]
            return [outs[a].at[_half(core, rows)]]

        sends, recvs = [], []
        for a in range(n):
            for k, v in enumerate(views(a, mc)):
                sends.append(_remote(v, v, sems, per * a + k, sibling))
            for k, v in enumerate(views(a, 1 - mc)):
                recvs.append(_remote(v, v, sems, per * a + k, sibling))
        return sends, recvs, []

    return _copies_hosted(arrays, [jax.ShapeDtypeStruct(s.shape, s.dtype) for s in arrays], (per * n, per * n), plan,
                          aliases={a: a for a in range(n)})


def _grad_swap_d2d(stacks):
    n = len(stacks)

    def plan(ins, outs, sems):
        mx, my, mc = _my_place()
        sibling = (mx, my, 1 - mc)
        sends = [_remote(ins[a].at[:, _half(1 - mc, ins[a].shape[1])], outs[a], sems, a, sibling) for a in range(n)]
        return sends, sends, []

    return _copies_hosted(stacks, [jax.ShapeDtypeStruct((N_CHIP, s.shape[1] // 2, s.shape[2]), s.dtype)
                                   for s in stacks], (n, n), plan)


def _grad_exchange_ici(parts):
    n = len(parts)

    def plan(ins, outs, sems):
        mx, my, mc = _my_place()
        chip = 2 * mx + my
        sends, recvs, locals_ = [], [], []
        for a in range(n):
            locals_.append(pltpu.make_async_copy(ins[a].at[chip], outs[a].at[chip], sems[2].at[a]))
            for k, peer, pchip in _chip_peers(mx, my, mc):
                sends.append(_remote(ins[a].at[pchip], outs[a].at[chip], sems, 3 * a + k, peer))
                recvs.append(_remote(ins[a].at[pchip], outs[a].at[pchip], sems, 3 * a + k, peer))
        return sends, recvs, locals_

    return _copies_hosted(parts, [jax.ShapeDtypeStruct(s.shape, s.dtype) for s in parts], (3 * n, 3 * n, n), plan)


def _row_block(rows, limit=256):
    for t in range(min(rows, limit), 7, -8):
        if rows % t == 0 and t % 8 == 0:
            return t
    return rows


def _grad_add_half(core, stack, received):
    _, rows, cw = stack.shape
    rh = rows // 2
    tr = _row_block(rh)

    def body(s_ref, a_ref, b_ref, o_ref):
        o_ref[...] = (a_ref[...] + b_ref[...]).astype(o_ref.dtype)

    grid_spec = pltpu.PrefetchScalarGridSpec(
        num_scalar_prefetch=1, grid=(rh // tr,),
        in_specs=[pl.BlockSpec((N_CHIP, tr, cw), lambda i, s: (0, s[0] * (rh // tr) + i, 0)),
                  pl.BlockSpec((N_CHIP, tr, cw), lambda i, s: (0, i, 0))],
        out_specs=pl.BlockSpec((N_CHIP, tr, cw), lambda i, s: (0, i, 0)))
    return pl.pallas_call(
        body, name="grad_add_half", grid_spec=grid_spec, out_shape=jax.ShapeDtypeStruct(received.shape, BF16),
        compiler_params=pltpu.CompilerParams(dimension_semantics=("arbitrary",), vmem_limit_bytes=VMEM_LIMIT),
    )(core, stack, received)


def _adamw(name, row_off, parts, w, m, v, rows, hosted=None):
    n, _, cw = parts.shape
    tr = _row_block(rows, 128)

    def update(p_ref, w_ref, m_ref, v_ref, g_ref, d_ref, nm_ref, nv_ref):
        g = p_ref[0].astype(F32)
        for k in range(1, n):
            g = g + p_ref[k].astype(F32)
        nm = ADAM_B1 * m_ref[...] + (1.0 - ADAM_B1) * g
        nv = ADAM_B2 * v_ref[...] + (1.0 - ADAM_B2) * (g * g)
        m_hat = nm / (1.0 - ADAM_B1 ** ADAM_STEP)
        v_hat = nv / (1.0 - ADAM_B2 ** ADAM_STEP)
        g_ref[...] = g
        d_ref[...] = -ADAM_LR * (m_hat / (jnp.sqrt(v_hat) + ADAM_EPS) + ADAM_WD * w_ref[...])
        nm_ref[...] = nm
        nv_ref[...] = nv

    out_shape = [jax.ShapeDtypeStruct(w.shape, F32)] * 4
    if row_off is None:
        blk = pl.BlockSpec((tr, cw), lambda i: (i, 0))
        return _pcall(update, name=name, grid=(rows // tr,), out_specs=[blk] * 4, out_shape=out_shape,
                      in_specs=[pl.BlockSpec((n, tr, cw), lambda i: (0, i, 0)), blk, blk, blk],
                      args=[parts, w, m, v], hosted=hosted)

    def body(s_ref, *refs):
        update(*refs)

    full = pl.BlockSpec((tr, cw), lambda i, s: (s[0] // tr + i, 0))
    grid_spec = pltpu.PrefetchScalarGridSpec(
        num_scalar_prefetch=1, grid=(rows // tr,),
        in_specs=[pl.BlockSpec((n, tr, cw), lambda i, s: (0, i, 0)), full, full, full],
        out_specs=[full, full, full, full])
    return pl.pallas_call(
        body, name=name, grid_spec=grid_spec, out_shape=out_shape,
        compiler_params=pltpu.CompilerParams(dimension_semantics=("arbitrary",), vmem_limit_bytes=VMEM_LIMIT),
    )(row_off, parts, w, m, v)


def _sum8(x):
    _, r, cw = x.shape
    tr = _row_block(r, 64)

    def body(x_ref, o_ref):
        acc = x_ref[0]
        for k in range(1, N_DEV):
            acc = acc + x_ref[k]
        o_ref[...] = acc

    return pl.pallas_call(
        body, name="sum8", grid=(r // tr,), in_specs=[pl.BlockSpec((N_DEV, tr, cw), lambda i: (0, i, 0))],
        out_specs=pl.BlockSpec((tr, cw), lambda i: (i, 0)), out_shape=jax.ShapeDtypeStruct((r, cw), F32),
        compiler_params=pltpu.CompilerParams(dimension_semantics=("arbitrary",)),
    )(x)


MOD_ROWS = 24
CTX_ROW = 16


def _mod_fwd(c_rows, w_mod, b_mod, hosted=None):
    nl, d, nn = w_mod.shape

    def body(c_ref, w_ref, b_ref, o_ref):
        o_ref[0] = _dot(_silu(c_ref[...]), w_ref[0]) + b_ref[0]

    return _pcall(
        body, name="mod_fwd", grid=(nl,),
        in_specs=[pl.BlockSpec((MOD_ROWS, d), lambda i: (0, 0)), pl.BlockSpec((1, d, nn), lambda i: (i, 0, 0)),
                  pl.BlockSpec((1, 1, nn), lambda i: (i, 0, 0))],
        out_specs=[pl.BlockSpec((1, MOD_ROWS, nn), lambda i: (i, 0, 0))],
        out_shape=[jax.ShapeDtypeStruct((nl, MOD_ROWS, nn), F32)], args=[c_rows, w_mod, b_mod], hosted=hosted)[0]


def _mod_bwd_rows(dlat, dctx_parts):
    nl, ne, nn = dlat.shape

    def body(l_ref, c_ref, db_ref, dc_ref):
        dc = c_ref[0, 0:1, :]
        for k in range(1, N_DEV):
            dc = dc + c_ref[0, k:k + 1, :]
        db = dc
        for k in range(ne):
            db = db + l_ref[0, k:k + 1, :]
        db_ref[0] = db
        dc_ref[0] = dc

    return pl.pallas_call(
        body, name="mod_bwd_rows", grid=(nl,),
        in_specs=[pl.BlockSpec((1, ne, nn), lambda i: (i, 0, 0)), pl.BlockSpec((1, N_DEV, nn), lambda i: (i, 0, 0))],
        out_specs=[pl.BlockSpec((1, 1, nn), lambda i: (i, 0, 0))] * 2,
        out_shape=[jax.ShapeDtypeStruct((nl, 1, nn), F32)] * 2,
        compiler_params=pltpu.CompilerParams(dimension_semantics=("arbitrary",)),
    )(dlat, dctx_parts)


def _mod_bwd_w(c_cols, d_rows, w_mod):
    nl, d, nn = w_mod.shape

    def body(c_ref, d_ref, w_ref, dw_ref, dc_ref):
        i = pl.program_id(0)
        c = c_ref[...]
        sg = jax.nn.sigmoid(c)
        s = c * sg
        dv = d_ref[0]
        acc = s[:, 0:1] * dv[0:1, :]
        for r in range(1, CTX_ROW + 1):
            acc = acc + s[:, r:r + 1] * dv[r:r + 1, :]
        dw_ref[0] = acc
        ds_ctx = jnp.sum(w_ref[0] * dv[CTX_ROW:CTX_ROW + 1, :], axis=1, keepdims=True)
        cc, sc = c[:, CTX_ROW:CTX_ROW + 1], sg[:, CTX_ROW:CTX_ROW + 1]
        part = ds_ctx * (sc * (1.0 + cc * (1.0 - sc)))

        @pl.when(i == 0)
        def _():
            dc_ref[...] = part

        @pl.when(i != 0)
        def _():
            dc_ref[...] += part

    return pl.pallas_call(
        body, name="mod_bwd_w", grid=(nl,),
        in_specs=[pl.BlockSpec((d, MOD_ROWS), lambda i: (0, 0)), pl.BlockSpec((1, MOD_ROWS, nn), lambda i: (i, 0, 0)),
                  pl.BlockSpec((1, d, nn), lambda i: (i, 0, 0))],
        out_specs=[pl.BlockSpec((1, d, nn), lambda i: (i, 0, 0)), pl.BlockSpec((d, 1), lambda i: (0, 0))],
        out_shape=[jax.ShapeDtypeStruct((nl, d, nn), F32), jax.ShapeDtypeStruct((d, 1), F32)],
        compiler_params=pltpu.CompilerParams(dimension_semantics=("arbitrary",), vmem_limit_bytes=VMEM_LIMIT),
    )(c_cols, d_rows, w_mod)


def _pack_rows(arrays, width, row_multiple=8):
    rows, spans, r0 = [], [], 0
    for a in arrays:
        flat = a.reshape(-1)
        nr = -(-flat.shape[0] // width)
        held = -(-nr // 8) * 8
        flat = jnp.pad(flat, (0, held * width - flat.shape[0]))
        rows.append(flat.reshape(held, width))
        spans.append((r0, nr, a.shape))
        r0 += held
    if r0 % row_multiple:
        rows.append(jnp.zeros((row_multiple - r0 % row_multiple, width), F32))
    return jnp.concatenate(rows, axis=0), spans


def _unpack_rows(packed, spans):
    out = []
    for r0, nr, shape in spans:
        out.append(packed[r0:r0 + nr].reshape(-1)[:math.prod(shape)].reshape(shape))
    return out


BIG = {"cv_w_in": 1, "cv_w_out": 0, "pl_w_in": 1, "pl_w_grp": None, "pl_w_out": 0, "ml_w_in": 1, "ml_w_uq": 1,
       "ml_w_ukv": 1, "ml_w_out": 0, "ch_w_in": 1, "ch_w_out": 0}
SMALL_SHARDED = ["cv_dw", "pl_scale", "ml_q_norm", "ml_kv_norm", "ch_ln_g", "ch_ln_b"]
SMALL_REPLICATED = ["c_ctx", "norm_g", "b_mod", "cv_db", "cv_ln_g", "cv_ln_b", "ml_nope_norm", "ml_rope_norm",
                    "ch_w_s", "ch_b_s"]
WEIGHTS = ['c_ctx', 'norm_g', 'w_mod', 'b_mod', 'cv_w_in', 'cv_dw', 'cv_db', 'cv_ln_g', 'cv_ln_b', 'cv_w_out',
           'pl_w_in', 'pl_w_grp', 'pl_scale', 'pl_w_out', 'ml_w_in', 'ml_q_norm', 'ml_kv_norm', 'ml_w_uq', 'ml_w_ukv',
           'ml_nope_norm', 'ml_rope_norm', 'ml_w_out', 'ch_w_in', 'ch_ln_g', 'ch_ln_b', 'ch_w_s', 'ch_b_s', 'ch_w_out']


def _shard2d(name, a):
    if name == "pl_w_grp":
        return a.reshape(a.shape[-3] * a.shape[-2], a.shape[-1])
    return a.reshape(a.shape[-2], a.shape[-1])


def _unstack(name, s):
    if name == "pl_w_grp":
        ng = len(POOL_WINDOWS)
        return s.reshape(N_CHIP, ng, s.shape[1] // ng, s.shape[2]).transpose(1, 0, 2, 3).reshape(ng, -1, s.shape[2])
    if BIG[name] == 0:
        return s.reshape(-1, s.shape[2])
    return s.transpose(1, 0, 2).reshape(s.shape[1], -1)


def _stack(name, g):
    if g.ndim == 3 and name != "pl_w_grp":
        return g
    if name == "pl_w_grp":
        ng = len(POOL_WINDOWS)
        return g.reshape(ng, N_CHIP, -1, g.shape[2]).transpose(1, 0, 2, 3).reshape(N_CHIP, -1, g.shape[2])
    if BIG[name] == 0:
        return g.reshape(N_CHIP, -1, g.shape[1])
    return g.reshape(g.shape[0], N_CHIP, -1).transpose(1, 0, 2)


L0, L1, L2, L3 = LAYER_WEIGHTS
EARLY_SMALL = ("ch_w_s", "ch_b_s", "ch_ln_g", "ch_ln_b")
MESH_SCHEDULE = {
    "ag8_inputs": [("gather", L0)], "mod_fwd": [("gfill", L0)],
    "cv_in_fwd": [("gather", L1[:1])], "conv_fwd": [("gather", L1[1:])], "cv_out_fwd": [("gfill", L1)],
    "pl_in_fwd": [("gather", L2[:1])], "pool_fwd": [("gather", L2[1:])], "pl_out_fwd": [("gfill", L2)],
    "attn_fwd": [("gather", L3)], "ml_out_fwd": [("gfill", L3)],
    "ch_in_bwd": [("small", EARLY_SMALL)],
    "ml_out_bwd": [("swap", L3)], "attn_bwd": [("exch", L3)], "ml_queries_bwd": [("ofill", L3)],
    "pl_out_bwd": [("swap", L2)], "pl_in_bwd": [("exch", L2)],
    "cv_out_bwd": [("swap", L1), ("ofill", L2)], "conv_bwd": [("exch", L1)], "cv_in_bwd": [("ofill", L1)],
    "ag8_dmod": [("exch", L0)], "ag8_small_grads": [("ofill", L0)],
}


class _MeshPlan:
    def __init__(self, weights, m, v, core):
        self.W, self.M, self.V, self.core = weights, m, v, core
        self.small = None
        self.stack, self.gstack, self.part, self.half, self.out = {}, {}, {}, {}, {}
        self.notes, self.early = {}, {}
        self.live, self.done = {}, set()

    def _make(self, op, names):
        if op == "gather":
            return _gather_ici([_shard2d(n, self.W[n]).astype(BF16) for n in names])
        if op == "gfill":
            return _sibling_fill([self.stack[n] for n in names], 1, True)
        if op == "swap":
            return _grad_swap_d2d([self.gstack[n] for n in names])
        if op == "exch":
            return _grad_exchange_ici([self.part[n] for n in names])
        if op == "ofill":
            return _sibling_fill([t for n in names for t in self.half[n]], 0, False)
        pack, self.early_spans = _pack_rows([self.notes[n] for n in names], LANES, 128)
        return _ag8_copies(pack)

    def _finish_op(self, op, names, hosted):
        self.done.add((op, names))
        res = hosted.results
        if op in ("gather", "gfill"):
            self.stack.update(zip(names, res))
        elif op == "swap":
            for n, r in zip(names, res):
                self.part[n] = _grad_add_half(self.core, self.gstack[n], r)
        elif op == "exch":
            for n, q in zip(names, res):
                rh = q.shape[1]
                self.half[n] = _adamw("adamw_" + n, self.core * rh, q, _shard2d(n, self.W[n]),
                                      _shard2d(n, self.M[n]), _shard2d(n, self.V[n]), rh)
        elif op == "ofill":
            for k, n in enumerate(names):
                self.out[n] = tuple(r.reshape(self.W[n].shape) for r in res[4 * k:4 * k + 4])
        else:
            self.early.update(zip(names, _unpack_rows(_sum8(res[0]), self.early_spans)))

    def alone(self, op, names):
        hosted = self._make(op, names)
        _run_hosted("%s_%s" % (op, names[0]), hosted)
        self._finish_op(op, names, hosted)

    def weights(self, layer):
        wk = {n: _unstack(n, self.stack[n]) for n in LAYER_WEIGHTS[layer]}
        if "pl_w_grp" in wk:
            wk["pl_w_grp"] = wk["pl_w_grp"].astype(F32)
        return _prep_weights(wk)

    def hosted(self, tag):
        self.live[tag] = [(op, names, self._make(op, names)) for op, names in MESH_SCHEDULE.get(tag, [])]
        return _merge_hosted([h for _, _, h in self.live[tag]])

    def after(self, tag):
        for op, names, hosted in self.live.pop(tag, []):
            self._finish_op(op, names, hosted)

    def note(self, values):
        self.notes.update(values)

    def layer_grads(self, layer, grads):
        g = _unprep_grads(grads)
        for n in LAYER_WEIGHTS[layer]:
            self.gstack[n] = _stack(n, g[n])

    def finish(self):
        for names in (L3, L2, L1, L0):
            for op in ("swap", "exch", "ofill"):
                if (op, names) not in self.done:
                    self.alone(op, names)
        return self.out


def kernel(x, c, ctx, c_ctx, norm_g, w_mod, b_mod, cv_w_in, cv_dw, cv_db, cv_ln_g, cv_ln_b, cv_w_out, pl_w_in, pl_w_grp, pl_scale, pl_w_out, ml_w_in, ml_q_norm, ml_kv_norm, ml_w_uq, ml_w_ukv, ml_nope_norm, ml_rope_norm, ml_w_out, ch_w_in, ch_ln_g, ch_ln_b, ch_w_s, ch_b_s, ch_w_out, loss_target, m_c_ctx, m_norm_g, m_w_mod, m_b_mod, m_cv_w_in, m_cv_dw, m_cv_db, m_cv_ln_g, m_cv_ln_b, m_cv_w_out, m_pl_w_in, m_pl_w_grp, m_pl_scale, m_pl_w_out, m_ml_w_in, m_ml_q_norm, m_ml_kv_norm, m_ml_w_uq, m_ml_w_ukv, m_ml_nope_norm, m_ml_rope_norm, m_ml_w_out, m_ch_w_in, m_ch_ln_g, m_ch_ln_b, m_ch_w_s, m_ch_b_s, m_ch_w_out, v_c_ctx, v_norm_g, v_w_mod, v_b_mod, v_cv_w_in, v_cv_dw, v_cv_db, v_cv_ln_g, v_cv_ln_b, v_cv_w_out, v_pl_w_in, v_pl_w_grp, v_pl_scale, v_pl_w_out, v_ml_w_in, v_ml_q_norm, v_ml_kv_norm, v_ml_w_uq, v_ml_w_ukv, v_ml_nope_norm, v_ml_rope_norm, v_ml_w_out, v_ch_w_in, v_ch_ln_g, v_ch_ln_b, v_ch_w_s, v_ch_b_s, v_ch_w_out):
    W = dict(c_ctx=c_ctx, norm_g=norm_g, w_mod=w_mod, b_mod=b_mod, cv_w_in=cv_w_in, cv_dw=cv_dw, cv_db=cv_db, cv_ln_g=cv_ln_g, cv_ln_b=cv_ln_b, cv_w_out=cv_w_out, pl_w_in=pl_w_in, pl_w_grp=pl_w_grp, pl_scale=pl_scale, pl_w_out=pl_w_out, ml_w_in=ml_w_in, ml_q_norm=ml_q_norm, ml_kv_norm=ml_kv_norm, ml_w_uq=ml_w_uq, ml_w_ukv=ml_w_ukv, ml_nope_norm=ml_nope_norm, ml_rope_norm=ml_rope_norm, ml_w_out=ml_w_out, ch_w_in=ch_w_in, ch_ln_g=ch_ln_g, ch_ln_b=ch_ln_b, ch_w_s=ch_w_s, ch_b_s=ch_b_s, ch_w_out=ch_w_out)
    M = dict(c_ctx=m_c_ctx, norm_g=m_norm_g, w_mod=m_w_mod, b_mod=m_b_mod, cv_w_in=m_cv_w_in, cv_dw=m_cv_dw, cv_db=m_cv_db, cv_ln_g=m_cv_ln_g, cv_ln_b=m_cv_ln_b, cv_w_out=m_cv_w_out, pl_w_in=m_pl_w_in, pl_w_grp=m_pl_w_grp, pl_scale=m_pl_scale, pl_w_out=m_pl_w_out, ml_w_in=m_ml_w_in, ml_q_norm=m_ml_q_norm, ml_kv_norm=m_ml_kv_norm, ml_w_uq=m_ml_w_uq, ml_w_ukv=m_ml_w_ukv, ml_nope_norm=m_ml_nope_norm, ml_rope_norm=m_ml_rope_norm, ml_w_out=m_ml_w_out, ch_w_in=m_ch_w_in, ch_ln_g=m_ch_ln_g, ch_ln_b=m_ch_ln_b, ch_w_s=m_ch_w_s, ch_b_s=m_ch_b_s, ch_w_out=m_ch_w_out)
    V = dict(c_ctx=v_c_ctx, norm_g=v_norm_g, w_mod=v_w_mod, b_mod=v_b_mod, cv_w_in=v_cv_w_in, cv_dw=v_cv_dw, cv_db=v_cv_db, cv_ln_g=v_cv_ln_g, cv_ln_b=v_cv_ln_b, cv_w_out=v_cv_w_out, pl_w_in=v_pl_w_in, pl_w_grp=v_pl_w_grp, pl_scale=v_pl_scale, pl_w_out=v_pl_w_out, ml_w_in=v_ml_w_in, ml_q_norm=v_ml_q_norm, ml_kv_norm=v_ml_kv_norm, ml_w_uq=v_ml_w_uq, ml_w_ukv=v_ml_w_ukv, ml_nope_norm=v_ml_nope_norm, ml_rope_norm=v_ml_rope_norm, ml_w_out=v_ml_w_out, ch_w_in=v_ch_w_in, ch_ln_g=v_ch_ln_g, ch_ln_b=v_ch_ln_b, ch_w_s=v_ch_w_s, ch_b_s=v_ch_b_s, ch_w_out=v_ch_w_out)

    batch, lat_len, d = x.shape
    mx, my, mc = _my_place()
    chip = 2 * mx + my
    dev = 2 * chip + mc
    core = jnp.reshape(mc, (1,)).astype(jnp.int32)
    zero_off = jnp.zeros((1,), jnp.int32)
    big_names = list(BIG)

    sw = d // N_CHIP
    small_in = [c] + [jnp.pad(W[n].reshape(-1, W[n].shape[-1]), ((0, 0), (0, sw - W[n].shape[-1])))
                      for n in SMALL_SHARDED]
    pack1, spans1 = _pack_rows(small_in, sw)
    plan = _MeshPlan(W, M, V, core)
    gather1 = _ag8_copies(pack1)
    _run_hosted("ag8_inputs", _merge_hosted([gather1, plan.hosted("ag8_inputs")]))
    plan.after("ag8_inputs")
    got1 = gather1.results[0]
    c_all = got1[:, spans1[0][0]:spans1[0][0] + spans1[0][1]].reshape(N_DEV * batch, d)
    full_small = {}
    for n, (r0, nr, _) in zip(SMALL_SHARDED, spans1[1:]):
        blk = got1[0::2, r0:r0 + nr, :W[n].shape[-1]]
        full_small[n] = blk.transpose(1, 0, 2).reshape(nr, -1)

    c_rows = jnp.concatenate([c_all, c_ctx[None], jnp.zeros((MOD_ROWS - CTX_ROW - 1, d), F32)], axis=0)
    nmod = w_mod.shape[2]
    b_shard = lax.dynamic_slice(b_mod, (0, chip * nmod), (b_mod.shape[0], nmod))[:, None, :]
    mod_shard = _mod_fwd(c_rows, w_mod, b_shard, hosted=plan.hosted("mod_fwd"))
    plan.after("mod_fwd")
    got2 = _ag8("ag8_mod", mod_shard.reshape(-1, nmod))
    mod_full = got2[0::2].reshape(N_CHIP, 4, MOD_ROWS, nmod).transpose(1, 2, 0, 3).reshape(4, MOD_ROWS, 3 * d)
    mod_lat = lax.dynamic_slice(mod_full, (0, dev * batch, 0), (4, batch, 3 * d))
    mod_ctx = mod_full[:, CTX_ROW]
    mods = []
    for i in range(4):
        mods.append(tuple(
            jnp.stack([mod_lat[i, :, j * d:(j + 1) * d], jnp.broadcast_to(mod_ctx[i, j * d:(j + 1) * d], (batch, d))],
                      axis=1)[:, :, None, :] for j in range(3)))

    wk = dict(full_small)
    wk.update(norm_g=norm_g, cv_db=cv_db, cv_ln_g=cv_ln_g, cv_ln_b=cv_ln_b, ml_nope_norm=ml_nope_norm[0],
              ml_rope_norm=ml_rope_norm[0], ch_w_s=ch_w_s[0], ch_b_s=ch_b_s[0])
    plan.small = _prep_weights(wk)
    xm = jnp.concatenate([x, ctx], axis=1)
    loss_part, grad_x, dmods, g = _local_step(xm, loss_target, mods, plan, lat_len)
    g = _unprep_grads(g)
    plan.alone("swap", L0)

    lat_rows, ctx_rows = [], []
    for i in range(4):
        dsh, dsc, dgt = dmods[i]
        lat_rows.append(jnp.concatenate([dsh[:, 0, 0], dsc[:, 0, 0], dgt[:, 0, 0]], axis=1))
        zero = jnp.zeros((d,), F32)
        cs = [jnp.sum(t[:, 1, 0], axis=0) if ok else zero
              for t, ok in zip((dsh, dsc, dgt), (i <= 2, i <= 2, i <= 1))]
        ctx_rows.append(jnp.concatenate(cs, axis=0)[None])
    dmod_dev = jnp.concatenate(lat_rows + ctx_rows, axis=0)
    dmod_dev = jnp.pad(dmod_dev, ((0, (-dmod_dev.shape[0]) % 8), (0, 0)))
    gather3 = _ag8_copies(dmod_dev)
    _run_hosted("ag8_dmod", _merge_hosted([gather3, plan.hosted("ag8_dmod")]))
    plan.after("ag8_dmod")
    got3 = gather3.results[0]
    dlat = got3[:, :4 * batch].reshape(N_DEV, 4, batch, 3 * d).transpose(1, 0, 2, 3).reshape(4, N_DEV * batch, 3 * d)
    dctx_parts = got3[:, 4 * batch:4 * batch + 4].transpose(1, 0, 2)
    g_b_mod, dctx = _mod_bwd_rows(dlat, dctx_parts)
    d_rows = jnp.concatenate([dlat, dctx, jnp.zeros((4, MOD_ROWS - CTX_ROW - 1, 3 * d), F32)], axis=1)
    d_rows = lax.dynamic_slice_in_dim(d_rows, chip * nmod, nmod, axis=2)
    g_w_mod, dcc_part = _mod_bwd_w(c_rows.T, d_rows, w_mod)

    wm2 = w_mod.reshape(-1, nmod)
    res_mod = _adamw("adamw_w_mod", None, g_w_mod.reshape(1, -1, nmod), wm2, M["w_mod"].reshape(-1, nmod),
                     V["w_mod"].reshape(-1, nmod), wm2.shape[0], hosted=plan.hosted("adamw_w_mod"))
    plan.after("adamw_w_mod")
    out = {"w_mod": tuple(r.reshape(w_mod.shape) for r in res_mod)}

    g_small_in = {n: g[n] for n in SMALL_SHARDED if n not in EARLY_SMALL}
    g_small_in.update(norm_g=g["norm_g"], cv_db=g["cv_db"], cv_ln_g=g["cv_ln_g"], cv_ln_b=g["cv_ln_b"],
                      ml_nope_norm=g["ml_nope_norm"], ml_rope_norm=g["ml_rope_norm"],
                      c_ctx=dcc_part.reshape(-1) * (mc == 0).astype(F32), loss=loss_part)
    small_names = list(g_small_in)
    pack4, spans4 = _pack_rows([g_small_in[n] for n in small_names], LANES, 128)
    gather4 = _ag8_copies(pack4)
    _run_hosted("ag8_small_grads", _merge_hosted([gather4, plan.hosted("ag8_small_grads")]))
    plan.after("ag8_small_grads")
    gs = dict(zip(small_names, _unpack_rows(_sum8(gather4.results[0]), spans4)))
    loss = gs["loss"][0, 0]
    gs.update(plan.early)
    gs["b_mod"] = g_b_mod[:, 0]
    for n in SMALL_SHARDED:
        wd = W[n].shape[-1]
        gs[n] = lax.dynamic_slice_in_dim(gs[n], chip * wd, wd, axis=1)
    upd_names = SMALL_REPLICATED + SMALL_SHARDED
    pw, spans_u = _pack_rows([W[n] for n in upd_names], LANES, 128)
    pm, _ = _pack_rows([M[n] for n in upd_names], LANES, 128)
    pv, _ = _pack_rows([V[n] for n in upd_names], LANES, 128)
    pg, _ = _pack_rows([gs[n].reshape(W[n].shape) for n in upd_names], LANES, 128)
    res_small = _adamw("adamw_small", None, pg[None], pw, pm, pv, pw.shape[0], hosted=plan.hosted("adamw_small"))
    plan.after("adamw_small")
    for n, vals in zip(upd_names, zip(*[_unpack_rows(r, spans_u) for r in res_small])):
        out[n] = vals
    out.update(plan.finish())

    outs = [loss, grad_x]
    for j in range(4):
        outs.extend(out[n][j] for n in WEIGHTS)
    return tuple(outs)
```

```python
import functools
import math

import jax
import jax.numpy as jnp
from jax import lax
from jax.experimental import pallas as pl
from jax.experimental.pallas import tpu as pltpu

F32 = jnp.float32
BF16 = jnp.bfloat16
MESH = pl.DeviceIdType.MESH

EPS = 1e-6
GRID_W = 64
CONV_WIDTH = 31
CONV_HALF = CONV_WIDTH // 2
CONV_PAD = 16
POOL_WINDOWS = (2, 4, 8, 16)
POOL_HALF = max(POOL_WINDOWS) // 2
HEADS = 8
NOPE = 128
ROPE = 64
HEAD_W = 256
VDIM = 128
KV_RANK = 256
Q_RANK = 384
ATT_SCALE = (NOPE + ROPE) ** -0.5
LN2 = math.log(2.0)
Q_PRESCALE = ATT_SCALE / LN2
ROPE_THETA = 10000.0
CHUNK = 128
CHUNK_GROUPS = 8
LANES = 128
TM = 256
TQ = 512
ATT_RQ = 128
ATT_KC = 256
VMEM_LIMIT = 56 * 1024 * 1024

ADAM_LR = 0.001
ADAM_B1 = 0.9
ADAM_B2 = 0.999
ADAM_EPS = 1e-08
ADAM_WD = 0.01
ADAM_STEP = 10


def _dot(a, b):
    return jnp.dot(a.astype(BF16), b.astype(BF16), preferred_element_type=F32)


def _dot_nt(a, b):
    return lax.dot_general(a.astype(BF16), b.astype(BF16), (((1,), (1,)), ((), ())), preferred_element_type=F32)


def _dot_tn(a, b):
    return lax.dot_general(a.astype(BF16), b.astype(BF16), (((0,), (0,)), ((), ())), preferred_element_type=F32)


@jax.custom_vjp
def _mm(a, w):
    return _dot(a, w)


def _mm_fwd(a, w):
    return _dot(a, w), (a, w)


def _mm_bwd(res, ct):
    a, w = res
    return _dot_nt(ct, w), _dot_tn(a, ct)


_mm.defvjp(_mm_fwd, _mm_bwd)


def _swap16_impl(x):
    n = x.shape[-1]
    ax = x.ndim - 1
    lane = lax.broadcasted_iota(jnp.int32, x.shape, ax)
    up = pltpu.roll(x, n - 16, ax)
    dn = pltpu.roll(x, 16, ax)
    return jnp.where((lane % 32) < 16, up, dn)


@jax.custom_vjp
def _swap16(x):
    return _swap16_impl(x)


_swap16.defvjp(lambda x: (_swap16_impl(x), None), lambda _, ct: (_swap16_impl(ct),))


def _rms(x, g, n=None):
    n = x.shape[-1] if n is None else n
    return x * lax.rsqrt(jnp.sum(x * x, axis=-1, keepdims=True) * (1.0 / n) + EPS) * g


def _layernorm(x, g, b):
    mu = jnp.mean(x, axis=-1, keepdims=True)
    xc = x - mu
    var = jnp.mean(xc * xc, axis=-1, keepdims=True)
    return xc * lax.rsqrt(var + EPS) * g + b


def _silu(x):
    return x * jax.nn.sigmoid(x)


def _rope(x, cos, sin):
    return x * cos + _swap16(x) * sin


ANY = pl.BlockSpec(memory_space=pl.ANY)


class _Hosted:
    def __init__(self, arrays, out_shapes, sems, start, wait, aliases=None):
        self.arrays, self.out_shapes, self.sems = list(arrays), list(out_shapes), list(sems)
        self.start, self.wait, self.aliases = start, wait, dict(aliases or {})
        self.results = None


def _merge_hosted(parts):
    parts = [p for p in parts if p is not None]
    if not parts:
        return None
    if len(parts) == 1:
        return parts[0]
    offs, a0, o0, s0 = [], 0, 0, 0
    for p in parts:
        offs.append((a0, o0, s0))
        a0, o0, s0 = a0 + len(p.arrays), o0 + len(p.out_shapes), s0 + len(p.sems)

    def run(which):
        def f(ins, outs, sems):
            for p, (a, o, s) in zip(parts, offs):
                getattr(p, which)(ins[a:a + len(p.arrays)], outs[o:o + len(p.out_shapes)], sems[s:s + len(p.sems)])
        return f

    aliases = {}
    for p, (a, o, _) in zip(parts, offs):
        aliases.update({a + i: o + j for i, j in p.aliases.items()})
    merged = _Hosted(sum((p.arrays for p in parts), []), sum((p.out_shapes for p in parts), []),
                     sum((p.sems for p in parts), []), run("start"), run("wait"), aliases)
    merged.parts, merged.offs = parts, offs
    return merged


def _deliver(hosted, results):
    hosted.results = list(results)
    for p, (_, o, _) in zip(getattr(hosted, "parts", []), getattr(hosted, "offs", [])):
        p.results = list(results[o:o + len(p.out_shapes)])


def _pcall(body, *, name, grid, in_specs, out_specs, out_shape, args, hosted=None, vmem_limit=True, scratch=()):
    n_in, n_out, n_scr = len(args), len(out_shape), len(scratch)
    kwargs = dict(scratch_shapes=list(scratch)) if scratch else {}
    if hosted is not None:
        nhi, nho, inner = len(hosted.arrays), len(hosted.out_shapes), body

        def body(*refs):
            ins, hin = refs[:n_in], refs[n_in:n_in + nhi]
            outs, hout = refs[n_in + nhi:n_in + nhi + n_out], refs[n_in + nhi + n_out:n_in + nhi + n_out + nho]
            own = refs[n_in + nhi + n_out + nho:n_in + nhi + n_out + nho + n_scr]
            sems = refs[n_in + nhi + n_out + nho + n_scr:]
            first, last = None, None
            for k, g in enumerate(grid):
                f, l = pl.program_id(k) == 0, pl.program_id(k) == g - 1
                first = f if first is None else jnp.logical_and(first, f)
                last = l if last is None else jnp.logical_and(last, l)

            @pl.when(first)
            def _():
                hosted.start(hin, hout, sems)

            inner(*ins, *outs, *own)

            @pl.when(last)
            def _():
                hosted.wait(hin, hout, sems)

        in_specs = list(in_specs) + [ANY] * nhi
        out_specs = list(out_specs) + [ANY] * nho
        out_shape = list(out_shape) + hosted.out_shapes
        args = list(args) + hosted.arrays
        kwargs = dict(scratch_shapes=list(scratch) + hosted.sems,
                      input_output_aliases={n_in + i: n_out + j for i, j in hosted.aliases.items()})
    params = dict(dimension_semantics=("arbitrary",) * len(grid))
    if vmem_limit:
        params["vmem_limit_bytes"] = VMEM_LIMIT
    res = pl.pallas_call(body, name=name, grid=grid, in_specs=list(in_specs), out_specs=list(out_specs),
                         out_shape=list(out_shape), compiler_params=pltpu.CompilerParams(**params), **kwargs)(*args)
    if hosted is not None:
        _deliver(hosted, res[n_out:])
    return list(res[:n_out])


def _run_hosted(name, hosted):
    nhi, nho = len(hosted.arrays), len(hosted.out_shapes)

    def body(*refs):
        ins, outs, sems = refs[:nhi], refs[nhi:nhi + nho], refs[nhi + nho:]
        hosted.start(ins, outs, sems)
        hosted.wait(ins, outs, sems)

    res = pl.pallas_call(body, name=name, in_specs=[ANY] * nhi, out_specs=[ANY] * nho, out_shape=hosted.out_shapes,
                         scratch_shapes=hosted.sems, input_output_aliases=hosted.aliases)(*hosted.arrays)
    _deliver(hosted, res)
    return list(res)


def _const_spec(shape, single=False):
    nd = len(shape)
    if single:
        return pl.BlockSpec(shape, lambda b, i: (0,) * nd, pipeline_mode=pl.Buffered(1))
    return pl.BlockSpec(shape, lambda b, i: (0,) * nd)


def _tile_spec(arr, n_lat_tiles, lat_only=False):
    bt, _, cw = arr.shape
    if lat_only:
        return pl.BlockSpec((1, TM, cw), lambda b, i: (b if bt > 1 else 0, jnp.minimum(i, n_lat_tiles - 1), 0))
    return pl.BlockSpec((1, TM, cw), lambda b, i: (b if bt > 1 else 0, i, 0))


def _eparam_spec(arr, n_lat_tiles):
    cw = arr.shape[-1]
    return pl.BlockSpec((1, 1, 1, cw), lambda b, i: (b, (i >= n_lat_tiles).astype(jnp.int32), 0, 0))


def _stage_fwd(name, *, pre, post, wsel, splits, tiles, eparams, sparams, weights, out_widths, out_dtypes,
               batch, n_tiles, n_lat_tiles, hosted=None):
    nt, ne, ns, nw = len(tiles), len(eparams), len(sparams), len(weights)

    def body(*refs):
        t_refs = refs[:nt]
        e_refs = refs[nt:nt + ne]
        s_refs = refs[nt + ne:nt + ne + ns]
        w_refs = refs[nt + ne + ns:nt + ne + ns + nw]
        o_refs = refs[nt + ne + ns + nw:]
        tv = [r[0].astype(F32) for r in t_refs]
        ev = [r[0, 0] for r in e_refs]
        sv = [r[...] for r in s_refs]
        a = pre(tv, ev, sv)
        z = [_dot(a[wsel[j]], w_refs[j][...]) for j in range(nw)]
        if post is None:
            outs = [z[j][:, s:s + w] for (j, s, w) in splits]
        else:
            outs = post(z, tv, ev, sv)
        for o_ref, o in zip(o_refs, outs):
            o_ref[0] = o.astype(o_ref.dtype)

    in_specs = ([_tile_spec(t, n_lat_tiles) for t in tiles] + [_eparam_spec(e, n_lat_tiles) for e in eparams]
                + [_const_spec(s.shape) for s in sparams] + [_const_spec(w.shape, single=True) for w in weights])
    out_shape = [jax.ShapeDtypeStruct((batch, n_tiles * TM, w), dt) for w, dt in zip(out_widths, out_dtypes)]
    out_specs = [pl.BlockSpec((1, TM, w), lambda b, i: (b, i, 0)) for w in out_widths]
    return _pcall(body, name=name, grid=(batch, n_tiles), in_specs=in_specs, out_specs=out_specs,
                  out_shape=out_shape, args=[*tiles, *eparams, *sparams, *weights], hosted=hosted)


def _stage_bwd(name, *, pre, post, wsel, splits, tiles, tile_diff, eparams, sparams, weights, cots, cot_lat_only,
               batch, n_tiles, n_lat_tiles, add=None, add_lat_only=False, hosted=None, w_col_stack=None,
               dt_lat_only=False):
    nt, ne, ns, nw, nc = len(tiles), len(eparams), len(sparams), len(weights), len(cots)
    diff_idx = [k for k in range(nt) if tile_diff[k]]
    nd = len(diff_idx)
    has_add = add is not None
    w_col_stack = w_col_stack or [None] * nw

    def body(*refs):
        pos = 0
        t_refs = refs[pos:pos + nt]; pos += nt
        e_refs = refs[pos:pos + ne]; pos += ne
        s_refs = refs[pos:pos + ns]; pos += ns
        w_refs = refs[pos:pos + nw]; pos += nw
        c_refs = refs[pos:pos + nc]; pos += nc
        if has_add:
            add_ref = refs[pos]; pos += 1
        dt_refs = refs[pos:pos + nd]; pos += nd
        de_refs = refs[pos:pos + ne]; pos += ne
        ds_refs = refs[pos:pos + ns]; pos += ns
        dw_refs = refs[pos:pos + nw]; pos += nw

        b = pl.program_id(0)
        i = pl.program_id(1)
        is_lat = i < n_lat_tiles
        tv = [r[0].astype(F32) for r in t_refs]
        ev = tuple(r[0, 0] for r in e_refs)
        sv = tuple(r[...] for r in s_refs)
        dv0 = tuple(tv[k] for k in diff_idx)

        def merge(dv):
            full = list(tv)
            for k, v in zip(diff_idx, dv):
                full[k] = v
            return full

        def pre_f(dv, ev_, sv_):
            return tuple(pre(merge(dv), list(ev_), list(sv_)))

        a, vjp_pre = jax.vjp(pre_f, dv0, ev, sv)
        cv = []
        for c_ref, lat in zip(c_refs, cot_lat_only):
            c = c_ref[0].astype(F32)
            cv.append(jnp.where(is_lat, c, 0.0) if lat else c)
        if post is None:
            dz = []
            for j in range(nw):
                parts = [cv[k] for k, (jj, _, _) in enumerate(splits) if jj == j]
                dz.append(parts[0] if len(parts) == 1 else jnp.concatenate(parts, axis=1))
            dt2 = de2 = ds2 = None
        else:
            z = tuple(_dot(a[wsel[j]], w_refs[j][...]) for j in range(nw))

            def post_f(z_, dv, ev_, sv_):
                return tuple(post(list(z_), merge(dv), list(ev_), list(sv_)))

            _, vjp_post = jax.vjp(post_f, z, dv0, ev, sv)
            dz, dt2, de2, ds2 = vjp_post(tuple(cv))
        da = [None] * len(a)
        dws = []
        for j in range(nw):
            g = _dot_nt(dz[j], w_refs[j][...])
            da[wsel[j]] = g if da[wsel[j]] is None else da[wsel[j]] + g
            dws.append(_dot_tn(a[wsel[j]], dz[j]))
        da = tuple(jnp.zeros_like(a[k]) if da[k] is None else da[k] for k in range(len(a)))
        dt1, de1, ds1 = vjp_pre(da)

        def plus(u, v):
            return u if v is None else u + v

        for k in range(nd):
            val = plus(dt1[k], None if dt2 is None else dt2[k])
            if has_add and k == 0:
                addv = add_ref[0].astype(F32)
                val = val + (jnp.where(is_lat, addv, 0.0) if add_lat_only else addv)
            if dt_lat_only:
                @pl.when(is_lat)
                def _(k=k, val=val):
                    dt_refs[k][0] = val.astype(dt_refs[k].dtype)
            else:
                dt_refs[k][0] = val.astype(dt_refs[k].dtype)

        seg_first = jnp.logical_or(i == 0, i == n_lat_tiles)
        for k in range(ne):
            val = plus(de1[k], None if de2 is None else de2[k])

            @pl.when(seg_first)
            def _(k=k, val=val):
                de_refs[k][0, 0] = val

            @pl.when(jnp.logical_not(seg_first))
            def _(k=k, val=val):
                de_refs[k][0, 0] += val

        first = jnp.logical_and(b == 0, i == 0)
        acc = [(ds_refs[k], plus(ds1[k], None if ds2 is None else ds2[k])) for k in range(ns)]
        for j in range(nw):
            if w_col_stack[j]:
                cw = dws[j].shape[1] // w_col_stack[j]
                acc += [(dw_refs[j].at[c], dws[j][:, c * cw:(c + 1) * cw]) for c in range(w_col_stack[j])]
            else:
                acc.append((dw_refs[j], dws[j]))
        for ref, val in acc:
            @pl.when(first)
            def _(ref=ref, val=val):
                ref[...] = val

            @pl.when(jnp.logical_not(first))
            def _(ref=ref, val=val):
                ref[...] += val

    in_specs = ([_tile_spec(t, n_lat_tiles) for t in tiles] + [_eparam_spec(e, n_lat_tiles) for e in eparams]
                + [_const_spec(s.shape) for s in sparams] + [_const_spec(w.shape, single=True) for w in weights]
                + [_tile_spec(c, n_lat_tiles, lat) for c, lat in zip(cots, cot_lat_only)])
    args = [*tiles, *eparams, *sparams, *weights, *cots]
    if has_add:
        in_specs.append(_tile_spec(add, n_lat_tiles, add_lat_only))
        args.append(add)
    dt_tiles = n_lat_tiles if dt_lat_only else n_tiles
    out_shape = [jax.ShapeDtypeStruct((batch, dt_tiles * TM, tiles[k].shape[-1]), F32) for k in diff_idx]
    out_specs = [pl.BlockSpec((1, TM, tiles[k].shape[-1]), lambda b, i: (b, jnp.minimum(i, dt_tiles - 1), 0))
                 for k in diff_idx]
    out_shape += [jax.ShapeDtypeStruct(e.shape, F32) for e in eparams]
    out_specs += [_eparam_spec(e, n_lat_tiles) for e in eparams]
    out_shape += [jax.ShapeDtypeStruct(s.shape, F32) for s in sparams]
    out_specs += [_const_spec(s.shape) for s in sparams]
    dw_shapes = [(n, w.shape[0], w.shape[1] // n) if n else w.shape for w, n in zip(weights, w_col_stack)]
    out_shape += [jax.ShapeDtypeStruct(s, F32) for s in dw_shapes]
    out_specs += [_const_spec(s, single=True) for s in dw_shapes]
    res = _pcall(body, name=name, grid=(batch, n_tiles), in_specs=in_specs, out_specs=out_specs,
                 out_shape=out_shape, args=args, hosted=hosted)
    return res[:nd], res[nd:nd + ne], res[nd + ne:nd + ne + ns], res[nd + ne + ns:]


def _pre_adaln(tv, ev, sv):
    x = tv[0]
    sh, sc = ev[0], ev[1]
    return [_rms(x, sv[0]) * (1.0 + sc) + sh]


def _post_residual(x_index):
    def post(z, tv, ev, sv):
        return [tv[x_index] + ev[-1] * z[0]]
    return post


def _pre_conv_out(tv, ev, sv):
    c1, gg = tv[0], tv[1]
    return [_silu(_layernorm(c1, sv[0], sv[1])) * _silu(gg)]


def _pre_pool_out(tv, ev, sv):
    pooled, gg = tv[0], tv[1]
    w_grp, scale = sv[0], sv[1]
    gw = w_grp.shape[-1]
    y = jnp.concatenate([_mm(pooled[:, k * gw:(k + 1) * gw], w_grp[k]) for k in range(w_grp.shape[0])], axis=1)
    return [y * scale * _silu(gg)]


def _pre_rms_only(tv, ev, sv):
    return [_rms(tv[0], sv[0])]


def _post_mla_keys(z, tv, ev, sv):
    krp, cos, sin = tv[1], tv[2], tv[3]
    nope_g, rope_g = sv[1], sv[2]
    kv = z[0]
    kr = _rope(_rms(krp, rope_g, ROPE), cos, sin)
    ks, vs = [], []
    for h in range(HEADS):
        ks.append(_rms(kv[:, h * 2 * NOPE:h * 2 * NOPE + NOPE], nope_g))
        ks.append(kr)
        vs.append(kv[:, h * 2 * NOPE + NOPE:(h + 1) * 2 * NOPE])
    return [jnp.concatenate(ks, axis=1), jnp.concatenate(vs, axis=1)]


def _post_mla_queries(z, tv, ev, sv):
    cos, sin = tv[1], tv[2]
    nope_g, rope_g = sv[1], sv[2]
    q = z[0]
    qs = []
    for h in range(HEADS):
        qs.append(_rms(q[:, h * HEAD_W:h * HEAD_W + NOPE], nope_g))
        qs.append(_rope(_rms(q[:, h * HEAD_W + NOPE:(h + 1) * HEAD_W], rope_g, ROPE), cos, sin))
    return [jnp.concatenate(qs, axis=1) * Q_PRESCALE]


def _pre_mla_out(tv, ev, sv):
    return [tv[0] * _silu(tv[1])]


def _pre_chunk_out(tv, ev, sv):
    u, v, gg = tv[0], tv[1], tv[2]
    ln_g, ln_b, w_s, b_s = sv
    vn = _layernorm(v, ln_g, ln_b)
    rows = []
    for n in range(vn.shape[0] // CHUNK):
        blk = vn[n * CHUNK:(n + 1) * CHUNK]
        cols = [_mm(w_s[g], blk[:, g * LANES:(g + 1) * LANES]) + b_s[:, g:g + 1] for g in range(CHUNK_GROUPS)]
        rows.append(jnp.concatenate(cols, axis=1))
    s = jnp.concatenate(rows, axis=0)
    return [u * s * _silu(gg)]


def _segments(lat_len, tot_len):
    segs = [(0, lat_len)]
    if tot_len > lat_len:
        segs.append((lat_len, tot_len - lat_len))
    return segs


def _pad_rows(x):
    z = jnp.zeros((CONV_PAD, x.shape[1]), x.dtype)
    return jnp.concatenate([z, x, z], axis=0)


def _shifted(xp, j):
    n = xp.shape[0] - 2 * CONV_PAD
    if j != 0:
        xp = pltpu.roll(xp, (-j) % xp.shape[0], 0)
    return xp[CONV_PAD:CONV_PAD + n]


def _conv_fwd(a, bgate, dw, db, lat_len, hosted=None):
    batch, tot, e = a.shape
    segs = _segments(lat_len, tot)

    def body(a_ref, b_ref, dw_ref, db_ref, o_ref):
        w = dw_ref[...]
        for (s0, n) in segs:
            y = a_ref[0, s0:s0 + n, :] * jax.nn.sigmoid(b_ref[0, s0:s0 + n, :])
            yp = _pad_rows(y)
            acc = jnp.zeros_like(y) + db_ref[...]
            for k in range(CONV_WIDTH):
                acc = acc + _shifted(yp, k - CONV_HALF) * w[k:k + 1, :]
            o_ref[0, s0:s0 + n, :] = acc

    blk = pl.BlockSpec((1, tot, LANES), lambda b, cb: (b, 0, cb))
    return _pcall(
        body, name="conv_fwd", grid=(batch, e // LANES),
        in_specs=[blk, blk, pl.BlockSpec((CONV_WIDTH, LANES), lambda b, cb: (0, cb)),
                  pl.BlockSpec((1, LANES), lambda b, cb: (0, cb))],
        out_specs=[blk], out_shape=[jax.ShapeDtypeStruct(a.shape, F32)], args=[a, bgate, dw, db], hosted=hosted)[0]


def _conv_bwd(a, bgate, dw, dc1, lat_len, hosted=None):
    batch, tot, e = a.shape
    segs = _segments(lat_len, tot)

    def body(a_ref, b_ref, dw_ref, dc_ref, da_ref, dg_ref, ddw_ref, ddb_ref):
        b = pl.program_id(1)
        w = dw_ref[...]
        ddw_rows = [None] * CONV_WIDTH
        ddb = None
        for (s0, n) in segs:
            av = a_ref[0, s0:s0 + n, :]
            sg = jax.nn.sigmoid(b_ref[0, s0:s0 + n, :])
            y = av * sg
            dc = dc_ref[0, s0:s0 + n, :]
            yp, dcp = _pad_rows(y), _pad_rows(dc)
            dy = jnp.zeros_like(y)
            for k in range(CONV_WIDTH):
                j = k - CONV_HALF
                dy = dy + _shifted(dcp, -j) * w[k:k + 1, :]
                r = jnp.sum(dc * _shifted(yp, j), axis=0, keepdims=True)
                ddw_rows[k] = r if ddw_rows[k] is None else ddw_rows[k] + r
            r = jnp.sum(dc, axis=0, keepdims=True)
            ddb = r if ddb is None else ddb + r
            da_ref[0, s0:s0 + n, :] = dy * sg
            dg_ref[0, s0:s0 + n, :] = dy * av * sg * (1.0 - sg)

        @pl.when(b == 0)
        def _():
            ddw_ref[...] = jnp.zeros_like(ddw_ref)
            ddb_ref[...] = jnp.zeros_like(ddb_ref)

        for k in range(CONV_WIDTH):
            ddw_ref[k:k + 1, :] += ddw_rows[k]
        ddb_ref[...] += ddb

    blk = pl.BlockSpec((1, tot, LANES), lambda cb, b: (b, 0, cb))
    wspec = pl.BlockSpec((CONV_WIDTH, LANES), lambda cb, b: (0, cb))
    bspec = pl.BlockSpec((1, LANES), lambda cb, b: (0, cb))
    return _pcall(
        body, name="conv_bwd", grid=(e // LANES, batch),
        in_specs=[blk, blk, wspec, blk],
        out_specs=[blk, blk, wspec, bspec],
        out_shape=[jax.ShapeDtypeStruct(a.shape, F32), jax.ShapeDtypeStruct(a.shape, F32),
                   jax.ShapeDtypeStruct((CONV_WIDTH, e), F32), jax.ShapeDtypeStruct((1, e), F32)],
        args=[a, bgate, dw, dc1], hosted=hosted)


def _pool_taps(group):
    half = lax.shift_left(jnp.int32(1), group)
    taps = []
    for j in range(-POOL_HALF, POOL_HALF):
        inside = jnp.logical_and(j >= -half, j < half)
        taps.append(jnp.where(inside, 1.0, 0.0).astype(F32))
    return taps, half


def _pool_counts(n, half, shape):
    t = lax.broadcasted_iota(jnp.int32, shape, 0)
    cnt = jnp.minimum(t + half, n) - jnp.maximum(t - half, 0)
    return cnt.astype(F32)


def _pool_fwd(v, lat_len, hosted=None):
    batch, tot, e = v.shape
    gw = e // len(POOL_WINDOWS)
    segs = _segments(lat_len, tot)

    def body(v_ref, o_ref):
        taps, half = _pool_taps(pl.program_id(1))
        for (s0, n) in segs:
            x = v_ref[0, s0:s0 + n, :]
            xp = _pad_rows(x)
            acc = jnp.zeros_like(x)
            for idx, j in enumerate(range(-POOL_HALF, POOL_HALF)):
                acc = acc + _shifted(xp, j) * taps[idx]
            o_ref[0, s0:s0 + n, :] = acc / _pool_counts(n, half, x.shape) - x

    blk = pl.BlockSpec((1, tot, gw), lambda b, g: (b, 0, g))
    return _pcall(body, name="pool_fwd", grid=(batch, len(POOL_WINDOWS)), in_specs=[blk], out_specs=[blk],
                  out_shape=[jax.ShapeDtypeStruct(v.shape, F32)], args=[v], hosted=hosted)[0]


def _pool_bwd(dp, lat_len):
    batch, tot, e = dp.shape
    gw = e // len(POOL_WINDOWS)
    segs = _segments(lat_len, tot)

    def body(d_ref, o_ref):
        taps, half = _pool_taps(pl.program_id(1))
        for (s0, n) in segs:
            d = d_ref[0, s0:s0 + n, :]
            dnp = _pad_rows(d / _pool_counts(n, half, d.shape))
            acc = jnp.zeros_like(d)
            for idx, j in enumerate(range(-POOL_HALF, POOL_HALF)):
                acc = acc + _shifted(dnp, -j) * taps[idx]
            o_ref[0, s0:s0 + n, :] = acc - d

    blk = pl.BlockSpec((1, tot, gw), lambda b, g: (b, 0, g))
    return pl.pallas_call(
        body, name="pool_bwd", grid=(batch, len(POOL_WINDOWS)), in_specs=[blk], out_specs=blk,
        out_shape=jax.ShapeDtypeStruct(dp.shape, F32),
        compiler_params=pltpu.CompilerParams(dimension_semantics=("arbitrary", "arbitrary"),
                                             vmem_limit_bytes=VMEM_LIMIT),
    )(dp)


def _attn_fwd(q, k, v, hosted=None):
    batch, lq, _ = q.shape
    tk = k.shape[1]
    tq = min(TQ, lq)

    def body(q_ref, k_ref, v_ref, o_ref, lse_ref):
        s2 = _dot_nt(q_ref[0], k_ref[0])
        m2 = jnp.max(s2, axis=-1, keepdims=True)
        e = jnp.exp2(s2 - m2)
        l = jnp.sum(e, axis=-1, keepdims=True)
        o_ref[0] = _dot(e, v_ref[0]) / l
        lse_ref[0, 0] = m2 + jnp.log2(l)

    return _pcall(
        body, name="attn_fwd", grid=(batch, HEADS, lq // tq),
        in_specs=[pl.BlockSpec((1, tq, HEAD_W), lambda b, h, i: (b, i, h)),
                  pl.BlockSpec((1, tk, HEAD_W), lambda b, h, i: (b, 0, h)),
                  pl.BlockSpec((1, tk, VDIM), lambda b, h, i: (b, 0, h))],
        out_specs=[pl.BlockSpec((1, tq, VDIM), lambda b, h, i: (b, i, h)),
                   pl.BlockSpec((1, 1, tq, 1), lambda b, h, i: (b, h, i, 0))],
        out_shape=[jax.ShapeDtypeStruct((batch, lq, HEADS * VDIM), F32),
                   jax.ShapeDtypeStruct((batch, HEADS, lq, 1), F32)], args=[q, k, v], hosted=hosted)


def _attn_bwd(q, k, v, o, lse, do, hosted=None):
    batch, lq, _ = q.shape
    tk = k.shape[1]
    tq = min(TQ, lq)

    def body(q_ref, k_ref, v_ref, o_ref, lse_ref, do_ref, dq_ref, dk_ref, dv_ref, p_scr, ds_scr):
        i = pl.program_id(2)
        nr = tq // ATT_RQ
        rows = [slice(r * ATT_RQ, (r + 1) * ATT_RQ) for r in range(nr)]
        qv = [q_ref[0, rw, :] for rw in rows]
        dob = [do_ref[0, rw, :].astype(BF16) for rw in rows]
        row_lse = [lse_ref[0, 0, rw, :] for rw in rows]
        delta = [jnp.sum(do_ref[0, rw, :] * o_ref[0, rw, :], axis=-1, keepdims=True) for rw in rows]
        for c in range(tk // ATT_KC):
            keys = slice(c * ATT_KC, (c + 1) * ATT_KC)
            kc, vc = k_ref[0, keys, :], v_ref[0, keys, :]
            for r in range(nr):
                p = jnp.exp2(_dot_nt(qv[r], kc) - row_lse[r])
                dp = _dot_nt(dob[r], vc)
                p_scr[rows[r], keys] = p.astype(BF16)
                ds_scr[rows[r], keys] = (p * (dp - delta[r]) * LN2).astype(BF16)
        dq_ref[0] = _dot(ds_scr[...], k_ref[0])
        dk = _dot_tn(ds_scr[...], q_ref[0])
        dv = _dot_tn(p_scr[...], do_ref[0])

        @pl.when(i == 0)
        def _():
            dk_ref[0] = dk
            dv_ref[0] = dv

        @pl.when(i != 0)
        def _():
            dk_ref[0] += dk
            dv_ref[0] += dv

    return _pcall(
        body, name="attn_bwd", grid=(batch, HEADS, lq // tq),
        in_specs=[pl.BlockSpec((1, tq, HEAD_W), lambda b, h, i: (b, i, h)),
                  pl.BlockSpec((1, tk, HEAD_W), lambda b, h, i: (b, 0, h)),
                  pl.BlockSpec((1, tk, VDIM), lambda b, h, i: (b, 0, h)),
                  pl.BlockSpec((1, tq, VDIM), lambda b, h, i: (b, i, h)),
                  pl.BlockSpec((1, 1, tq, 1), lambda b, h, i: (b, h, i, 0)),
                  pl.BlockSpec((1, tq, VDIM), lambda b, h, i: (b, i, h))],
        out_specs=[pl.BlockSpec((1, tq, HEAD_W), lambda b, h, i: (b, i, h)),
                   pl.BlockSpec((1, tk, HEAD_W), lambda b, h, i: (b, 0, h)),
                   pl.BlockSpec((1, tk, VDIM), lambda b, h, i: (b, 0, h))],
        out_shape=[jax.ShapeDtypeStruct(q.shape, F32), jax.ShapeDtypeStruct(k.shape, F32),
                   jax.ShapeDtypeStruct(v.shape, F32)],
        args=[q, k, v, o, lse, do], hosted=hosted,
        scratch=[pltpu.VMEM((tq, tk), BF16), pltpu.VMEM((tq, tk), BF16)])


def _loss_kernel(y, target):
    batch, lq, d = y.shape

    def body(y_ref, t_ref, l_ref, dy_ref):
        first = jnp.logical_and(pl.program_id(0) == 0, pl.program_id(1) == 0)
        err = y_ref[0] - t_ref[0]
        dy_ref[0] = err * (1.0 / d)
        part = jnp.zeros((1, LANES), F32) + jnp.sum(err * err) * (0.5 / d)

        @pl.when(first)
        def _():
            l_ref[...] = part

        @pl.when(jnp.logical_not(first))
        def _():
            l_ref[...] += part

    blk = pl.BlockSpec((1, TM, d), lambda b, i: (b, i, 0))
    return pl.pallas_call(
        body, name="loss_head", grid=(batch, lq // TM), in_specs=[blk, blk],
        out_specs=[pl.BlockSpec((1, LANES), lambda b, i: (0, 0)), blk],
        out_shape=[jax.ShapeDtypeStruct((1, LANES), F32), jax.ShapeDtypeStruct(y.shape, F32)],
        compiler_params=pltpu.CompilerParams(dimension_semantics=("arbitrary", "arbitrary")),
    )(y, target)


def _rope_tables(lat_len, ctx_len):
    rows = lat_len // GRID_W
    row_id = jnp.repeat(jnp.arange(rows), GRID_W).astype(F32)
    col_id = jnp.tile(jnp.arange(GRID_W), rows).astype(F32)
    axis_dim = ROPE // 2
    freqs = ROPE_THETA ** (-jnp.arange(0, axis_dim, 2, dtype=F32) / axis_dim)
    ar = row_id[:, None] * freqs
    ac = col_id[:, None] * freqs
    cr, sr, cc, sc = jnp.cos(ar), jnp.sin(ar), jnp.cos(ac), jnp.sin(ac)
    pad = jnp.zeros((lat_len, LANES - ROPE), F32)
    cos = jnp.concatenate([cr, cr, cc, cc, pad], axis=1)
    sin = jnp.concatenate([-sr, sr, -sc, sc, pad], axis=1)
    ident = jnp.concatenate([jnp.ones((ctx_len, ROPE), F32), jnp.zeros((ctx_len, LANES - ROPE), F32)], axis=1)
    cos = jnp.concatenate([cos, ident], axis=0)
    sin = jnp.concatenate([sin, jnp.zeros((ctx_len, LANES), F32)], axis=0)
    return cos[None], sin[None]


def _prep_weights(w):
    p = dict(w)
    kvc = KV_RANK + ROPE
    if "ml_w_in" in w:
        wi = w["ml_w_in"]
        p["ml_w_in"] = jnp.concatenate(
            [wi[:, :kvc], jnp.zeros((wi.shape[0], LANES - ROPE), wi.dtype), wi[:, kvc:]], axis=1)
    if "ml_w_uq" in w:
        uq = w["ml_w_uq"].reshape(Q_RANK, HEADS, NOPE + ROPE)
        p["ml_w_uq"] = jnp.pad(uq, ((0, 0), (0, 0), (0, HEAD_W - NOPE - ROPE))).reshape(Q_RANK, HEADS * HEAD_W)
    if "ml_rope_norm" in w:
        p["ml_rope_norm"] = jnp.pad(w["ml_rope_norm"], ((0, 0), (0, LANES - ROPE)))
    return p


def _unprep_grads(g):
    out = dict(g)
    kvc = KV_RANK + ROPE
    if "ml_w_in" in g:
        wi = g["ml_w_in"]
        out["ml_w_in"] = jnp.concatenate([wi[:, :kvc], wi[:, kvc + LANES - ROPE:]], axis=1)
    if "ml_w_uq" in g:
        uq = g["ml_w_uq"].reshape(Q_RANK, HEADS, HEAD_W)
        out["ml_w_uq"] = uq[:, :, :NOPE + ROPE].reshape(Q_RANK, HEADS * (NOPE + ROPE))
    if "ml_rope_norm" in g:
        out["ml_rope_norm"] = g["ml_rope_norm"][:, :ROPE]
    return out


LAYER_WEIGHTS = (("cv_w_in", "cv_w_out"), ("pl_w_in", "pl_w_grp", "pl_w_out"),
                 ("ml_w_in", "ml_w_uq", "ml_w_ukv", "ml_w_out"), ("ch_w_in", "ch_w_out"))


class _LocalPlan:
    def __init__(self, w):
        self.small = w
        self.grads = {}

    def weights(self, names):
        return {n: self.small[n] for n in names}

    def hosted(self, tag):
        return None

    def after(self, tag):
        pass

    def note(self, values):
        pass

    def layer_grads(self, layer, grads):
        self.grads.update(grads)


def _local_step(xm, target, mods, plan, lat_len):
    batch, tot, d = xm.shape
    e = d
    n_all, n_lat = tot // TM, lat_len // TM
    cos, sin = _rope_tables(lat_len, tot - lat_len)
    g = {}
    w = dict(plan.small)

    def hosting(tag, fn, *args, **kwargs):
        out = fn(*args, hosted=plan.hosted(tag), **kwargs)
        plan.after(tag)
        return out

    def s1_splits(widths):
        out, s = [], 0
        for wd in widths:
            out.append((0, s, wd))
            s += wd
        return out

    def fwd_in(name, x, mod, gi, wname, widths, n_tiles):
        return hosting(name, _stage_fwd, name, pre=_pre_adaln, post=None, wsel=[0], splits=s1_splits(widths),
                       tiles=[x], eparams=[mod[0], mod[1]], sparams=[w["norm_g"][gi:gi + 1]], weights=[w[wname]],
                       out_widths=widths, out_dtypes=[F32] * len(widths), batch=batch, n_tiles=n_tiles,
                       n_lat_tiles=n_lat)

    def bwd_in(name, x, mod, gi, wname, widths, n_tiles, cots, lat_only, add, add_lat_only, stack=None,
               dx_lat_only=False):
        (dx,), (dsh, dsc), (dg,), (dw,) = hosting(
            name, _stage_bwd, name, pre=_pre_adaln, post=None, wsel=[0], splits=s1_splits(widths), tiles=[x],
            tile_diff=[True], eparams=[mod[0], mod[1]], sparams=[w["norm_g"][gi:gi + 1]], weights=[w[wname]],
            cots=cots, cot_lat_only=lat_only, batch=batch, n_tiles=n_tiles, n_lat_tiles=n_lat, add=add,
            add_lat_only=add_lat_only, w_col_stack=[stack], dt_lat_only=dx_lat_only)
        return dx, dsh, dsc, dg, dw

    def fwd_out(name, pre, tiles, mod, sparams, wname, n_tiles):
        return hosting(name, _stage_fwd, name, pre=pre, post=_post_residual(len(tiles) - 1), wsel=[0], splits=None,
                       tiles=tiles, eparams=[mod[2]], sparams=sparams, weights=[w[wname]], out_widths=[d],
                       out_dtypes=[F32], batch=batch, n_tiles=n_tiles, n_lat_tiles=n_lat)[0]

    def bwd_out(name, pre, tiles, mod, sparams, wname, n_tiles, cot):
        diff = [True] * (len(tiles) - 1) + [False]
        dts, (dgt,), dss, (dw,) = hosting(
            name, _stage_bwd, name, pre=pre, post=_post_residual(len(tiles) - 1), wsel=[0], splits=None, tiles=tiles,
            tile_diff=diff, eparams=[mod[2]], sparams=sparams, weights=[w[wname]], cots=[cot], cot_lat_only=[False],
            batch=batch, n_tiles=n_tiles, n_lat_tiles=n_lat)
        return dts, dgt, dss, dw

    w.update(plan.weights(("cv_w_in",)))
    cv_s = [w["cv_ln_g"], w["cv_ln_b"]]
    a0, b0, g0 = fwd_in("cv_in_fwd", xm, mods[0], 0, "cv_w_in", [e, e, e], n_all)
    c1 = hosting("conv_fwd", _conv_fwd, a0, b0, w["cv_dw"], w["cv_db"], lat_len)
    w.update(plan.weights(("cv_w_out",)))
    x1 = fwd_out("cv_out_fwd", _pre_conv_out, [c1, g0, xm], mods[0], cv_s, "cv_w_out", n_all)

    w.update(plan.weights(LAYER_WEIGHTS[1]))
    pl_s = [w["pl_w_grp"], w["pl_scale"]]
    v1, g1 = fwd_in("pl_in_fwd", x1, mods[1], 1, "pl_w_in", [e, e], n_all)
    pooled = hosting("pool_fwd", _pool_fwd, v1, lat_len)
    x2 = fwd_out("pl_out_fwd", _pre_pool_out, [pooled, g1, x1], mods[1], pl_s, "pl_w_out", n_all)

    w.update(plan.weights(LAYER_WEIGHTS[2]))
    ml_widths = [KV_RANK, LANES, Q_RANK, HEADS * VDIM]
    ckv, krp, cq, g2 = fwd_in("ml_in_fwd", x2, mods[2], 2, "ml_w_in", ml_widths, n_all)
    k_s = [w["ml_kv_norm"], w["ml_nope_norm"][1:2], w["ml_rope_norm"][1:2]]
    q_s = [w["ml_q_norm"], w["ml_nope_norm"][0:1], w["ml_rope_norm"][0:1]]
    kk, vv = hosting("ml_keys_fwd", _stage_fwd, "ml_keys_fwd", pre=_pre_rms_only, post=_post_mla_keys, wsel=[0],
                     splits=None, tiles=[ckv, krp, cos, sin], eparams=[], sparams=k_s, weights=[w["ml_w_ukv"]],
                     out_widths=[HEADS * HEAD_W, HEADS * VDIM], out_dtypes=[BF16, BF16], batch=batch,
                     n_tiles=n_all, n_lat_tiles=n_lat)
    (qq,) = _stage_fwd("ml_queries_fwd", pre=_pre_rms_only, post=_post_mla_queries, wsel=[0], splits=None,
                       tiles=[cq, cos, sin], eparams=[], sparams=q_s, weights=[w["ml_w_uq"]],
                       out_widths=[HEADS * HEAD_W], out_dtypes=[BF16], batch=batch, n_tiles=n_lat,
                       n_lat_tiles=n_lat)
    att, lse = hosting("attn_fwd", _attn_fwd, qq, kk, vv)
    x3 = fwd_out("ml_out_fwd", _pre_mla_out, [att, g2, x2], mods[2], [], "ml_w_out", n_lat)

    w.update(plan.weights(LAYER_WEIGHTS[3]))
    ch_s = [w["ch_ln_g"], w["ch_ln_b"], w["ch_w_s"], w["ch_b_s"]]
    u3, v3, g3 = fwd_in("ch_in_fwd", x3, mods[3], 3, "ch_w_in", [e, e, e], n_lat)
    x4 = fwd_out("ch_out_fwd", _pre_chunk_out, [u3, v3, g3, x3], mods[3], ch_s, "ch_w_out", n_lat)

    loss_part, dy = _loss_kernel(x4, target)

    dmods = [None] * 4
    dnorm = [None] * 4
    big = {}
    (du, dv, dg), dgt, (g["ch_ln_g"], g["ch_ln_b"], g["ch_w_s"], g["ch_b_s"]), big["ch_w_out"] = bwd_out(
        "ch_out_bwd", _pre_chunk_out, [u3, v3, g3, x3], mods[3], ch_s, "ch_w_out", n_lat, dy)
    plan.note({n: g[n] for n in ("ch_ln_g", "ch_ln_b", "ch_w_s", "ch_b_s")})
    dx3, dsh, dsc, dnorm[3], big["ch_w_in"] = bwd_in("ch_in_bwd", x3, mods[3], 3, "ch_w_in", [e, e, e], n_lat,
                                                     [du, dv, dg], [False] * 3, dy, False, stack=N_CHIP)
    dmods[3] = (dsh, dsc, dgt)
    plan.layer_grads(3, big)

    big = {}
    (datt, dg), dgt, _, big["ml_w_out"] = bwd_out("ml_out_bwd", _pre_mla_out, [att, g2, x2], mods[2], [],
                                                  "ml_w_out", n_lat, dx3)
    dq, dk, dvv = hosting("attn_bwd", _attn_bwd, qq, kk, vv, att, lse, datt)
    (dcq,), _, (g["ml_q_norm"], dnope0, drope0), (big["ml_w_uq"],) = hosting(
        "ml_queries_bwd", _stage_bwd, "ml_queries_bwd", pre=_pre_rms_only, post=_post_mla_queries, wsel=[0],
        splits=None, tiles=[cq, cos, sin], tile_diff=[True, False, False], eparams=[], sparams=q_s,
        weights=[w["ml_w_uq"]], cots=[dq], cot_lat_only=[False], batch=batch, n_tiles=n_lat, n_lat_tiles=n_lat)
    (dckv, dkrp), _, (g["ml_kv_norm"], dnope1, drope1), (big["ml_w_ukv"],) = hosting(
        "ml_keys_bwd", _stage_bwd, "ml_keys_bwd", pre=_pre_rms_only, post=_post_mla_keys, wsel=[0], splits=None,
        tiles=[ckv, krp, cos, sin], tile_diff=[True, True, False, False], eparams=[], sparams=k_s,
        weights=[w["ml_w_ukv"]], cots=[dk, dvv], cot_lat_only=[False, False], batch=batch, n_tiles=n_all,
        n_lat_tiles=n_lat, w_col_stack=[N_CHIP])
    g["ml_nope_norm"] = jnp.concatenate([dnope0, dnope1], axis=0)
    g["ml_rope_norm"] = jnp.concatenate([drope0, drope1], axis=0)
    dx2, dsh, dsc, dnorm[2], big["ml_w_in"] = bwd_in("ml_in_bwd", x2, mods[2], 2, "ml_w_in", ml_widths, n_all,
                                                     [dckv, dkrp, dcq, dg], [False, False, True, True], dx3, True)
    dmods[2] = (dsh, dsc, dgt)
    plan.layer_grads(2, big)

    big = {}
    (dpooled, dg), dgt, (big["pl_w_grp"], g["pl_scale"]), big["pl_w_out"] = bwd_out(
        "pl_out_bwd", _pre_pool_out, [pooled, g1, x1], mods[1], pl_s, "pl_w_out", n_all, dx2)
    dv1 = _pool_bwd(dpooled, lat_len)
    dx1, dsh, dsc, dnorm[1], big["pl_w_in"] = bwd_in("pl_in_bwd", x1, mods[1], 1, "pl_w_in", [e, e], n_all,
                                                     [dv1, dg], [False] * 2, dx2, False, stack=N_CHIP)
    dmods[1] = (dsh, dsc, dgt)
    plan.layer_grads(1, big)

    big = {}
    (dc1, dg), dgt, (g["cv_ln_g"], g["cv_ln_b"]), big["cv_w_out"] = bwd_out(
        "cv_out_bwd", _pre_conv_out, [c1, g0, xm], mods[0], cv_s, "cv_w_out", n_all, dx1)
    da, db, g["cv_dw"], g["cv_db"] = hosting("conv_bwd", _conv_bwd, a0, b0, w["cv_dw"], dc1, lat_len)
    dx0, dsh, dsc, dnorm[0], big["cv_w_in"] = bwd_in("cv_in_bwd", xm, mods[0], 0, "cv_w_in", [e, e, e], n_all,
                                                     [da, db, dg], [False] * 3, dx1, False, stack=N_CHIP,
                                                     dx_lat_only=True)
    dmods[0] = (dsh, dsc, dgt)
    plan.layer_grads(0, big)
    g["norm_g"] = jnp.concatenate(dnorm, axis=0)
    return loss_part, dx0, dmods, g


N_DEV = 8
N_CHIP = 4
ANY = pl.BlockSpec(memory_space=pl.ANY)


def _my_place():
    return lax.axis_index("x"), lax.axis_index("y"), lax.axis_index("c")


def _flip(v, f):
    return 1 - v if f else v


def _ag8_copies(x):
    def plan(ins, outs, sems):
        mx, my, mc = _my_place()
        me = 4 * mx + 2 * my + mc
        sends, recvs = [], []
        for rel in range(1, N_DEV):
            peer = (_flip(mx, rel & 4), _flip(my, rel & 2), _flip(mc, rel & 1))
            src_dev = 4 * peer[0] + 2 * peer[1] + peer[2]
            sends.append(_remote(ins[0], outs[0].at[me], sems, rel - 1, peer))
            recvs.append(_remote(ins[0], outs[0].at[src_dev], sems, rel - 1, peer))
        return sends, recvs, [pltpu.make_async_copy(ins[0], outs[0].at[me], sems[2].at[0])]

    return _copies_hosted([x], [jax.ShapeDtypeStruct((N_DEV,) + x.shape, x.dtype)], (N_DEV - 1, N_DEV - 1, 1), plan)


def _ag8(name, x):
    return _run_hosted(name, _ag8_copies(x))[0]


def _ag8_column_copies(x, width):
    def plan(ins, outs, sems):
        mx, my, mc = _my_place()
        me = 4 * mx + 2 * my + mc
        sends, recvs = [], []
        for rel in range(1, N_DEV):
            peer = (_flip(mx, rel & 4), _flip(my, rel & 2), _flip(mc, rel & 1))
            src_dev = 4 * peer[0] + 2 * peer[1] + peer[2]
            cols = pl.ds(pl.multiple_of((2 * peer[0] + peer[1]) * width, LANES), width)
            sends.append(_remote(ins[0].at[:, cols], outs[0].at[me], sems, rel - 1, peer))
            recvs.append(_remote(ins[0].at[:, cols], outs[0].at[src_dev], sems, rel - 1, peer))
        mine = pl.ds(pl.multiple_of((2 * mx + my) * width, LANES), width)
        return sends, recvs, [pltpu.make_async_copy(ins[0].at[:, mine], outs[0].at[me], sems[2].at[0])]

    return _copies_hosted([x], [jax.ShapeDtypeStruct((N_DEV, x.shape[0], width), x.dtype)],
                          (N_DEV - 1, N_DEV - 1, 1), plan)


def _chip_rows_copies(x, rows_per_dev, shared_row):
    n_out = rows_per_dev + 1

    def plan(ins, outs, sems):
        mx, my, mc = _my_place()
        chip = 2 * mx + my
        sends, recvs = [], []

        def pieces(dev):
            return [(ins[0].at[pl.ds(dev * rows_per_dev, rows_per_dev)], slice(0, rows_per_dev)),
                    (ins[0].at[pl.ds(shared_row, 1)], slice(rows_per_dev, n_out))]

        for k, peer, pchip in _chip_peers(mx, my, mc):
            for t, (src, where) in enumerate(pieces(2 * pchip + mc)):
                sends.append(_remote(src, outs[0].at[chip, where], sems, 2 * k + t, peer))
                recvs.append(_remote(src, outs[0].at[pchip, where], sems, 2 * k + t, peer))
        locals_ = [pltpu.make_async_copy(src, outs[0].at[chip, where], sems[2].at[t])
                   for t, (src, where) in enumerate(pieces(2 * chip + mc))]
        return sends, recvs, locals_

    return _copies_hosted([x], [jax.ShapeDtypeStruct((N_CHIP, n_out) + x.shape[1:], x.dtype)], (6, 6, 2), plan)


def _chip_peers(mx, my, mc):
    out = []
    for rel in range(1, N_CHIP):
        px, py = _flip(mx, rel & 2), _flip(my, rel & 1)
        out.append((rel - 1, (px, py, mc), 2 * px + py))
    return out


def _half(mc, rows):
    return pl.ds(pl.multiple_of(mc * (rows // 2), 8), rows // 2)


def _copies_hosted(arrays, out_shapes, n_sems, plan, aliases=None):
    def start(ins, outs, sems):
        sends, _, locals_ = plan(ins, outs, sems)
        for cp in locals_ + sends:
            cp.start()

    def wait(ins, outs, sems):
        sends, recvs, locals_ = plan(ins, outs, sems)
        for cp in recvs:
            cp.wait_recv()
        for cp in sends:
            cp.wait_send()
        for cp in locals_:
            cp.wait()

    return _Hosted(arrays, out_shapes, [pltpu.SemaphoreType.DMA((k,)) for k in n_sems], start, wait, aliases)


def _remote(src, dst, sems, k, peer):
    return pltpu.make_async_remote_copy(src_ref=src, dst_ref=dst, send_sem=sems[0].at[k], recv_sem=sems[1].at[k],
                                        device_id=peer, device_id_type=MESH)


def _gather_ici(shards):
    n = len(shards)

    def plan(ins, outs, sems):
        mx, my, mc = _my_place()
        chip = 2 * mx + my
        sends, recvs, locals_ = [], [], []
        for a in range(n):
            rows = ins[a].shape[0]
            locals_.append(pltpu.make_async_copy(ins[a], outs[a].at[chip], sems[2].at[a]))
            for k, peer, pchip in _chip_peers(mx, my, mc):
                src = ins[a].at[_half(mc, rows)]
                sends.append(_remote(src, outs[a].at[chip, _half(mc, rows)], sems, 3 * a + k, peer))
                recvs.append(_remote(src, outs[a].at[pchip, _half(mc, rows)], sems, 3 * a + k, peer))
        return sends, recvs, locals_

    return _copies_hosted(shards, [jax.ShapeDtypeStruct((N_CHIP,) + s.shape, s.dtype) for s in shards],
                          (3 * n, 3 * n, n), plan)


def _sibling_fill(arrays, row_axis, chips_only_other):
    n = len(arrays)
    per = 3 if chips_only_other else 1

    def plan(ins, outs, sems):
        mx, my, mc = _my_place()
        sibling = (mx, my, 1 - mc)

        def views(a, core):
            rows = outs[a].shape[row_axis]
            if chips_only_other:
                return [outs[a].at[pchip, _half(core, rows)] for _, _, pchip in _chip_peers(mx, my, mc)]
            return [outs[a].at[_half(core, rows)]]

        sends, recvs = [], []
        for a in range(n):
            for k, v in enumerate(views(a, mc)):
                sends.append(_remote(v, v, sems, per * a + k, sibling))
            for k, v in enumerate(views(a, 1 - mc)):
                recvs.append(_remote(v, v, sems, per * a + k, sibling))
        return sends, recvs, []

    return _copies_hosted(arrays, [jax.ShapeDtypeStruct(s.shape, s.dtype) for s in arrays], (per * n, per * n), plan,
                          aliases={a: a for a in range(n)})


def _grad_swap_d2d(stacks):
    n = len(stacks)

    def plan(ins, outs, sems):
        mx, my, mc = _my_place()
        sibling = (mx, my, 1 - mc)
        sends = [_remote(ins[a].at[:, _half(1 - mc, ins[a].shape[1])], outs[a], sems, a, sibling) for a in range(n)]
        return sends, sends, []

    return _copies_hosted(stacks, [jax.ShapeDtypeStruct((N_CHIP, s.shape[1] // 2, s.shape[2]), s.dtype)
                                   for s in stacks], (n, n), plan)


def _grad_exchange_ici(parts):
    n = len(parts)

    def plan(ins, outs, sems):
        mx, my, mc = _my_place()
        chip = 2 * mx + my
        sends, recvs, locals_ = [], [], []
        for a in range(n):
            locals_.append(pltpu.make_async_copy(ins[a].at[chip], outs[a].at[chip], sems[2].at[a]))
            for k, peer, pchip in _chip_peers(mx, my, mc):
                sends.append(_remote(ins[a].at[pchip], outs[a].at[chip], sems, 3 * a + k, peer))
                recvs.append(_remote(ins[a].at[pchip], outs[a].at[pchip], sems, 3 * a + k, peer))
        return sends, recvs, locals_

    return _copies_hosted(parts, [jax.ShapeDtypeStruct(s.shape, s.dtype) for s in parts], (3 * n, 3 * n, n), plan)


def _row_block(rows, limit=256):
    for t in range(min(rows, limit), 7, -8):
        if rows % t == 0 and t % 8 == 0:
            return t
    return rows


def _grad_add_half(core, stack, received):
    _, rows, cw = stack.shape
    rh = rows // 2
    tr = _row_block(rh)

    def body(s_ref, a_ref, b_ref, o_ref):
        o_ref[...] = (a_ref[...] + b_ref[...]).astype(o_ref.dtype)

    grid_spec = pltpu.PrefetchScalarGridSpec(
        num_scalar_prefetch=1, grid=(rh // tr,),
        in_specs=[pl.BlockSpec((N_CHIP, tr, cw), lambda i, s: (0, s[0] * (rh // tr) + i, 0)),
                  pl.BlockSpec((N_CHIP, tr, cw), lambda i, s: (0, i, 0))],
        out_specs=pl.BlockSpec((N_CHIP, tr, cw), lambda i, s: (0, i, 0)))
    return pl.pallas_call(
        body, name="grad_add_half", grid_spec=grid_spec, out_shape=jax.ShapeDtypeStruct(received.shape, BF16),
        compiler_params=pltpu.CompilerParams(dimension_semantics=("arbitrary",), vmem_limit_bytes=VMEM_LIMIT),
    )(core, stack, received)


def _adamw(name, row_off, parts, w, m, v, rows, hosted=None):
    n, _, cw = parts.shape
    tr = _row_block(rows, 128)

    def update(p_ref, w_ref, m_ref, v_ref, g_ref, d_ref, nm_ref, nv_ref):
        g = p_ref[0].astype(F32)
        for k in range(1, n):
            g = g + p_ref[k].astype(F32)
        nm = ADAM_B1 * m_ref[...] + (1.0 - ADAM_B1) * g
        nv = ADAM_B2 * v_ref[...] + (1.0 - ADAM_B2) * (g * g)
        m_hat = nm / (1.0 - ADAM_B1 ** ADAM_STEP)
        v_hat = nv / (1.0 - ADAM_B2 ** ADAM_STEP)
        g_ref[...] = g
        d_ref[...] = -ADAM_LR * (m_hat / (jnp.sqrt(v_hat) + ADAM_EPS) + ADAM_WD * w_ref[...])
        nm_ref[...] = nm
        nv_ref[...] = nv

    out_shape = [jax.ShapeDtypeStruct(w.shape, F32)] * 4
    if row_off is None:
        blk = pl.BlockSpec((tr, cw), lambda i: (i, 0))
        return _pcall(update, name=name, grid=(rows // tr,), out_specs=[blk] * 4, out_shape=out_shape,
                      in_specs=[pl.BlockSpec((n, tr, cw), lambda i: (0, i, 0)), blk, blk, blk],
                      args=[parts, w, m, v], hosted=hosted)

    def body(s_ref, *refs):
        update(*refs)

    full = pl.BlockSpec((tr, cw), lambda i, s: (s[0] // tr + i, 0))
    grid_spec = pltpu.PrefetchScalarGridSpec(
        num_scalar_prefetch=1, grid=(rows // tr,),
        in_specs=[pl.BlockSpec((n, tr, cw), lambda i, s: (0, i, 0)), full, full, full],
        out_specs=[full, full, full, full])
    return pl.pallas_call(
        body, name=name, grid_spec=grid_spec, out_shape=out_shape,
        compiler_params=pltpu.CompilerParams(dimension_semantics=("arbitrary",), vmem_limit_bytes=VMEM_LIMIT),
    )(row_off, parts, w, m, v)


def _sum8(x):
    _, r, cw = x.shape
    tr = _row_block(r, 64)

    def body(x_ref, o_ref):
        acc = x_ref[0]
        for k in range(1, N_DEV):
            acc = acc + x_ref[k]
        o_ref[...] = acc

    return pl.pallas_call(
        body, name="sum8", grid=(r // tr,), in_specs=[pl.BlockSpec((N_DEV, tr, cw), lambda i: (0, i, 0))],
        out_specs=pl.BlockSpec((tr, cw), lambda i: (i, 0)), out_shape=jax.ShapeDtypeStruct((r, cw), F32),
        compiler_params=pltpu.CompilerParams(dimension_semantics=("arbitrary",)),
    )(x)


MOD_ROWS = 24
CTX_ROW = 16


def _mod_fwd(c_rows, w_mod, b_mod, hosted=None):
    nl, d, nn = w_mod.shape

    def body(c_ref, w_ref, b_ref, o_ref):
        o_ref[0] = _dot(_silu(c_ref[...]), w_ref[0]) + b_ref[0]

    return _pcall(
        body, name="mod_fwd", grid=(nl,),
        in_specs=[pl.BlockSpec((MOD_ROWS, d), lambda i: (0, 0)), pl.BlockSpec((1, d, nn), lambda i: (i, 0, 0)),
                  pl.BlockSpec((1, 1, nn), lambda i: (i, 0, 0))],
        out_specs=[pl.BlockSpec((1, MOD_ROWS, nn), lambda i: (i, 0, 0))],
        out_shape=[jax.ShapeDtypeStruct((nl, MOD_ROWS, nn), F32)], args=[c_rows, w_mod, b_mod], hosted=hosted)[0]


def _mod_bwd_rows(dlat, dctx_parts):
    nl, ne, nn = dlat.shape

    def body(l_ref, c_ref, db_ref, dc_ref):
        dc = c_ref[0, 0:1, :]
        for k in range(1, N_DEV):
            dc = dc + c_ref[0, k:k + 1, :]
        db = dc
        for k in range(ne):
            db = db + l_ref[0, k:k + 1, :]
        db_ref[0] = db
        dc_ref[0] = dc

    return pl.pallas_call(
        body, name="mod_bwd_rows", grid=(nl,),
        in_specs=[pl.BlockSpec((1, ne, nn), lambda i: (i, 0, 0)), pl.BlockSpec((1, N_DEV, nn), lambda i: (i, 0, 0))],
        out_specs=[pl.BlockSpec((1, 1, nn), lambda i: (i, 0, 0))] * 2,
        out_shape=[jax.ShapeDtypeStruct((nl, 1, nn), F32)] * 2,
        compiler_params=pltpu.CompilerParams(dimension_semantics=("arbitrary",)),
    )(dlat, dctx_parts)


def _mod_bwd_w(c_cols, d_rows, w_mod, hosted=None):
    nl, d, nn = w_mod.shape

    def body(c_ref, d_ref, w_ref, dw_ref, dc_ref):
        i = pl.program_id(0)
        c = c_ref[...]
        sg = jax.nn.sigmoid(c)
        s = c * sg
        dv = d_ref[0]
        acc = s[:, 0:1] * dv[0:1, :]
        for r in range(1, CTX_ROW + 1):
            acc = acc + s[:, r:r + 1] * dv[r:r + 1, :]
        dw_ref[0] = acc
        ds_ctx = jnp.sum(w_ref[0] * dv[CTX_ROW:CTX_ROW + 1, :], axis=1, keepdims=True)
        cc, sc = c[:, CTX_ROW:CTX_ROW + 1], sg[:, CTX_ROW:CTX_ROW + 1]
        part = ds_ctx * (sc * (1.0 + cc * (1.0 - sc)))

        @pl.when(i == 0)
        def _():
            dc_ref[...] = part

        @pl.when(i != 0)
        def _():
            dc_ref[...] += part

    return _pcall(
        body, name="mod_bwd_w", grid=(nl,),
        in_specs=[pl.BlockSpec((d, MOD_ROWS), lambda i: (0, 0)), pl.BlockSpec((1, MOD_ROWS, nn), lambda i: (i, 0, 0)),
                  pl.BlockSpec((1, d, nn), lambda i: (i, 0, 0))],
        out_specs=[pl.BlockSpec((1, d, nn), lambda i: (i, 0, 0)), pl.BlockSpec((d, 1), lambda i: (0, 0))],
        out_shape=[jax.ShapeDtypeStruct((nl, d, nn), F32), jax.ShapeDtypeStruct((d, 1), F32)],
        args=[c_cols, d_rows, w_mod], hosted=hosted)


def _pack_rows(arrays, width, row_multiple=8):
    rows, spans, r0 = [], [], 0
    for a in arrays:
        flat = a.reshape(-1)
        nr = -(-flat.shape[0] // width)
        held = -(-nr // 8) * 8
        flat = jnp.pad(flat, (0, held * width - flat.shape[0]))
        rows.append(flat.reshape(held, width))
        spans.append((r0, nr, a.shape))
        r0 += held
    if r0 % row_multiple:
        rows.append(jnp.zeros((row_multiple - r0 % row_multiple, width), F32))
    return jnp.concatenate(rows, axis=0), spans


def _unpack_rows(packed, spans):
    out = []
    for r0, nr, shape in spans:
        out.append(packed[r0:r0 + nr].reshape(-1)[:math.prod(shape)].reshape(shape))
    return out


BIG = {"cv_w_in": 1, "cv_w_out": 0, "pl_w_in": 1, "pl_w_grp": None, "pl_w_out": 0, "ml_w_in": 1, "ml_w_uq": 1,
       "ml_w_ukv": 1, "ml_w_out": 0, "ch_w_in": 1, "ch_w_out": 0}
SMALL_SHARDED = ["cv_dw", "pl_scale", "ml_q_norm", "ml_kv_norm", "ch_ln_g", "ch_ln_b"]
SMALL_REPLICATED = ["c_ctx", "norm_g", "b_mod", "cv_db", "cv_ln_g", "cv_ln_b", "ml_nope_norm", "ml_rope_norm",
                    "ch_w_s", "ch_b_s"]
WEIGHTS = ['c_ctx', 'norm_g', 'w_mod', 'b_mod', 'cv_w_in', 'cv_dw', 'cv_db', 'cv_ln_g', 'cv_ln_b', 'cv_w_out',
           'pl_w_in', 'pl_w_grp', 'pl_scale', 'pl_w_out', 'ml_w_in', 'ml_q_norm', 'ml_kv_norm', 'ml_w_uq', 'ml_w_ukv',
           'ml_nope_norm', 'ml_rope_norm', 'ml_w_out', 'ch_w_in', 'ch_ln_g', 'ch_ln_b', 'ch_w_s', 'ch_b_s', 'ch_w_out']


def _shard2d(name, a):
    if name == "pl_w_grp":
        return a.reshape(a.shape[-3] * a.shape[-2], a.shape[-1])
    return a.reshape(a.shape[-2], a.shape[-1])


def _unstack(name, s):
    if name == "pl_w_grp":
        ng = len(POOL_WINDOWS)
        return s.reshape(N_CHIP, ng, s.shape[1] // ng, s.shape[2]).transpose(1, 0, 2, 3).reshape(ng, -1, s.shape[2])
    if BIG[name] == 0:
        return s.reshape(-1, s.shape[2])
    return s.transpose(1, 0, 2).reshape(s.shape[1], -1)


def _stack(name, g):
    if g.ndim == 3 and name != "pl_w_grp":
        return g
    if name == "pl_w_grp":
        ng = len(POOL_WINDOWS)
        return g.reshape(ng, N_CHIP, -1, g.shape[2]).transpose(1, 0, 2, 3).reshape(N_CHIP, -1, g.shape[2])
    if BIG[name] == 0:
        return g.reshape(N_CHIP, -1, g.shape[1])
    return g.reshape(g.shape[0], N_CHIP, -1).transpose(1, 0, 2)


L0, L1, L2, L3 = LAYER_WEIGHTS
EARLY_SMALL = ("ch_w_s", "ch_b_s", "ch_ln_g", "ch_ln_b")
MESH_SCHEDULE = {
    "ag8_inputs": [("gather", L0[:1])], "mod_fwd": [("gfill", L0[:1])],
    "cv_in_fwd": [("gather", L0[1:]), ("gather", L1[:1])], "conv_fwd": [("gfill", L0[1:]), ("gather", L1[1:])],
    "cv_out_fwd": [("gfill", L1)],
    "pl_in_fwd": [("gather", L2[:1])], "pool_fwd": [("gather", L2[1:])], "pl_out_fwd": [("gfill", L2)],
    "attn_fwd": [("gather", L3)], "ml_out_fwd": [("gfill", L3)],
    "ch_in_bwd": [("small", EARLY_SMALL)],
    "ml_out_bwd": [("swap", L3)], "attn_bwd": [("exch", L3)], "ml_queries_bwd": [("ofill", L3)],
    "pl_out_bwd": [("swap", L2)], "pl_in_bwd": [("exch", L2)],
    "cv_out_bwd": [("swap", L1), ("ofill", L2)], "conv_bwd": [("exch", L1)], "cv_in_bwd": [("ofill", L1)],
    "ag8_dmod": [("swap", L0)], "mod_bwd_w": [("exch", L0)], "ag8_small_grads": [("ofill", L0)],
}


class _MeshPlan:
    def __init__(self, weights, m, v, core):
        self.W, self.M, self.V, self.core = weights, m, v, core
        self.small = None
        self.stack, self.gstack, self.part, self.half, self.out = {}, {}, {}, {}, {}
        self.notes, self.early = {}, {}
        self.live, self.done = {}, set()

    def _make(self, op, names):
        if op == "gather":
            return _gather_ici([_shard2d(n, self.W[n]).astype(BF16) for n in names])
        if op == "gfill":
            return _sibling_fill([self.stack[n] for n in names], 1, True)
        if op == "swap":
            return _grad_swap_d2d([self.gstack[n] for n in names])
        if op == "exch":
            return _grad_exchange_ici([self.part[n] for n in names])
        if op == "ofill":
            return _sibling_fill([t for n in names for t in self.half[n]], 0, False)
        pack, self.early_spans = _pack_rows([self.notes[n] for n in names], LANES, 128)
        return _ag8_copies(pack)

    def _finish_op(self, op, names, hosted):
        self.done.add((op, names))
        res = hosted.results
        if op in ("gather", "gfill"):
            self.stack.update(zip(names, res))
        elif op == "swap":
            for n, r in zip(names, res):
                self.part[n] = _grad_add_half(self.core, self.gstack[n], r)
        elif op == "exch":
            for n, q in zip(names, res):
                rh = q.shape[1]
                self.half[n] = _adamw("adamw_" + n, self.core * rh, q, _shard2d(n, self.W[n]),
                                      _shard2d(n, self.M[n]), _shard2d(n, self.V[n]), rh)
        elif op == "ofill":
            for k, n in enumerate(names):
                self.out[n] = tuple(r.reshape(self.W[n].shape) for r in res[4 * k:4 * k + 4])
        else:
            self.early.update(zip(names, _unpack_rows(_sum8(res[0]), self.early_spans)))

    def alone(self, op, names):
        hosted = self._make(op, names)
        _run_hosted("%s_%s" % (op, names[0]), hosted)
        self._finish_op(op, names, hosted)

    def weights(self, names):
        wk = {n: _unstack(n, self.stack[n]) for n in names}
        if "pl_w_grp" in wk:
            wk["pl_w_grp"] = wk["pl_w_grp"].astype(F32)
        return _prep_weights(wk)

    def hosted(self, tag):
        self.live[tag] = [(op, names, self._make(op, names)) for op, names in MESH_SCHEDULE.get(tag, [])]
        return _merge_hosted([h for _, _, h in self.live[tag]])

    def after(self, tag):
        for op, names, hosted in self.live.pop(tag, []):
            self._finish_op(op, names, hosted)

    def note(self, values):
        self.notes.update(values)

    def layer_grads(self, layer, grads):
        g = _unprep_grads(grads)
        for n in LAYER_WEIGHTS[layer]:
            self.gstack[n] = _stack(n, g[n])

    def finish(self):
        for names in (L3, L2, L1, L0):
            for op in ("swap", "exch", "ofill"):
                if (op, names) not in self.done:
                    self.alone(op, names)
        return self.out


def kernel(x, c, ctx, c_ctx, norm_g, w_mod, b_mod, cv_w_in, cv_dw, cv_db, cv_ln_g, cv_ln_b, cv_w_out, pl_w_in, pl_w_grp, pl_scale, pl_w_out, ml_w_in, ml_q_norm, ml_kv_norm, ml_w_uq, ml_w_ukv, ml_nope_norm, ml_rope_norm, ml_w_out, ch_w_in, ch_ln_g, ch_ln_b, ch_w_s, ch_b_s, ch_w_out, loss_target, m_c_ctx, m_norm_g, m_w_mod, m_b_mod, m_cv_w_in, m_cv_dw, m_cv_db, m_cv_ln_g, m_cv_ln_b, m_cv_w_out, m_pl_w_in, m_pl_w_grp, m_pl_scale, m_pl_w_out, m_ml_w_in, m_ml_q_norm, m_ml_kv_norm, m_ml_w_uq, m_ml_w_ukv, m_ml_nope_norm, m_ml_rope_norm, m_ml_w_out, m_ch_w_in, m_ch_ln_g, m_ch_ln_b, m_ch_w_s, m_ch_b_s, m_ch_w_out, v_c_ctx, v_norm_g, v_w_mod, v_b_mod, v_cv_w_in, v_cv_dw, v_cv_db, v_cv_ln_g, v_cv_ln_b, v_cv_w_out, v_pl_w_in, v_pl_w_grp, v_pl_scale, v_pl_w_out, v_ml_w_in, v_ml_q_norm, v_ml_kv_norm, v_ml_w_uq, v_ml_w_ukv, v_ml_nope_norm, v_ml_rope_norm, v_ml_w_out, v_ch_w_in, v_ch_ln_g, v_ch_ln_b, v_ch_w_s, v_ch_b_s, v_ch_w_out):
    W = dict(c_ctx=c_ctx, norm_g=norm_g, w_mod=w_mod, b_mod=b_mod, cv_w_in=cv_w_in, cv_dw=cv_dw, cv_db=cv_db, cv_ln_g=cv_ln_g, cv_ln_b=cv_ln_b, cv_w_out=cv_w_out, pl_w_in=pl_w_in, pl_w_grp=pl_w_grp, pl_scale=pl_scale, pl_w_out=pl_w_out, ml_w_in=ml_w_in, ml_q_norm=ml_q_norm, ml_kv_norm=ml_kv_norm, ml_w_uq=ml_w_uq, ml_w_ukv=ml_w_ukv, ml_nope_norm=ml_nope_norm, ml_rope_norm=ml_rope_norm, ml_w_out=ml_w_out, ch_w_in=ch_w_in, ch_ln_g=ch_ln_g, ch_ln_b=ch_ln_b, ch_w_s=ch_w_s, ch_b_s=ch_b_s, ch_w_out=ch_w_out)
    M = dict(c_ctx=m_c_ctx, norm_g=m_norm_g, w_mod=m_w_mod, b_mod=m_b_mod, cv_w_in=m_cv_w_in, cv_dw=m_cv_dw, cv_db=m_cv_db, cv_ln_g=m_cv_ln_g, cv_ln_b=m_cv_ln_b, cv_w_out=m_cv_w_out, pl_w_in=m_pl_w_in, pl_w_grp=m_pl_w_grp, pl_scale=m_pl_scale, pl_w_out=m_pl_w_out, ml_w_in=m_ml_w_in, ml_q_norm=m_ml_q_norm, ml_kv_norm=m_ml_kv_norm, ml_w_uq=m_ml_w_uq, ml_w_ukv=m_ml_w_ukv, ml_nope_norm=m_ml_nope_norm, ml_rope_norm=m_ml_rope_norm, ml_w_out=m_ml_w_out, ch_w_in=m_ch_w_in, ch_ln_g=m_ch_ln_g, ch_ln_b=m_ch_ln_b, ch_w_s=m_ch_w_s, ch_b_s=m_ch_b_s, ch_w_out=m_ch_w_out)
    V = dict(c_ctx=v_c_ctx, norm_g=v_norm_g, w_mod=v_w_mod, b_mod=v_b_mod, cv_w_in=v_cv_w_in, cv_dw=v_cv_dw, cv_db=v_cv_db, cv_ln_g=v_cv_ln_g, cv_ln_b=v_cv_ln_b, cv_w_out=v_cv_w_out, pl_w_in=v_pl_w_in, pl_w_grp=v_pl_w_grp, pl_scale=v_pl_scale, pl_w_out=v_pl_w_out, ml_w_in=v_ml_w_in, ml_q_norm=v_ml_q_norm, ml_kv_norm=v_ml_kv_norm, ml_w_uq=v_ml_w_uq, ml_w_ukv=v_ml_w_ukv, ml_nope_norm=v_ml_nope_norm, ml_rope_norm=v_ml_rope_norm, ml_w_out=v_ml_w_out, ch_w_in=v_ch_w_in, ch_ln_g=v_ch_ln_g, ch_ln_b=v_ch_ln_b, ch_w_s=v_ch_w_s, ch_b_s=v_ch_b_s, ch_w_out=v_ch_w_out)

    batch, lat_len, d = x.shape
    mx, my, mc = _my_place()
    chip = 2 * mx + my
    dev = 2 * chip + mc
    core = jnp.reshape(mc, (1,)).astype(jnp.int32)
    zero_off = jnp.zeros((1,), jnp.int32)
    big_names = list(BIG)

    sw = d // N_CHIP
    small_in = [c] + [jnp.pad(W[n].reshape(-1, W[n].shape[-1]), ((0, 0), (0, sw - W[n].shape[-1])))
                      for n in SMALL_SHARDED]
    pack1, spans1 = _pack_rows(small_in, sw)
    plan = _MeshPlan(W, M, V, core)
    gather1 = _ag8_copies(pack1)
    _run_hosted("ag8_inputs", _merge_hosted([gather1, plan.hosted("ag8_inputs")]))
    plan.after("ag8_inputs")
    got1 = gather1.results[0]
    c_all = got1[:, spans1[0][0]:spans1[0][0] + spans1[0][1]].reshape(N_DEV * batch, d)
    full_small = {}
    for n, (r0, nr, _) in zip(SMALL_SHARDED, spans1[1:]):
        blk = got1[0::2, r0:r0 + nr, :W[n].shape[-1]]
        full_small[n] = blk.transpose(1, 0, 2).reshape(nr, -1)

    c_rows = jnp.concatenate([c_all, c_ctx[None], jnp.zeros((MOD_ROWS - CTX_ROW - 1, d), F32)], axis=0)
    nmod = w_mod.shape[2]
    b_shard = lax.dynamic_slice(b_mod, (0, chip * nmod), (b_mod.shape[0], nmod))[:, None, :]
    mod_shard = _mod_fwd(c_rows, w_mod, b_shard, hosted=plan.hosted("mod_fwd"))
    plan.after("mod_fwd")
    mod_rows = mod_shard.transpose(1, 0, 2).reshape(MOD_ROWS, 1, 4 * nmod)
    got2 = _run_hosted("mod_exchange", _chip_rows_copies(mod_rows, batch, CTX_ROW))[0]
    mod_mine = got2.reshape(N_CHIP, batch + 1, 4, nmod).transpose(2, 1, 0, 3).reshape(4, batch + 1, 3 * d)
    mod_lat, mod_ctx = mod_mine[:, :batch], mod_mine[:, batch]
    mods = []
    for i in range(4):
        mods.append(tuple(
            jnp.stack([mod_lat[i, :, j * d:(j + 1) * d], jnp.broadcast_to(mod_ctx[i, j * d:(j + 1) * d], (batch, d))],
                      axis=1)[:, :, None, :] for j in range(3)))

    wk = dict(full_small)
    wk.update(norm_g=norm_g, cv_db=cv_db, cv_ln_g=cv_ln_g, cv_ln_b=cv_ln_b, ml_nope_norm=ml_nope_norm[0],
              ml_rope_norm=ml_rope_norm[0], ch_w_s=ch_w_s[0], ch_b_s=ch_b_s[0])
    plan.small = _prep_weights(wk)
    xm = jnp.concatenate([x, ctx], axis=1)
    loss_part, grad_x, dmods, g = _local_step(xm, loss_target, mods, plan, lat_len)
    g = _unprep_grads(g)

    lat_rows, ctx_rows = [], []
    for i in range(4):
        dsh, dsc, dgt = dmods[i]
        lat_rows.append(jnp.concatenate([dsh[:, 0, 0], dsc[:, 0, 0], dgt[:, 0, 0]], axis=1))
        zero = jnp.zeros((d,), F32)
        cs = [jnp.sum(t[:, 1, 0], axis=0) if ok else zero
              for t, ok in zip((dsh, dsc, dgt), (i <= 2, i <= 2, i <= 1))]
        ctx_rows.append(jnp.concatenate(cs, axis=0)[None])
    dmod_dev = jnp.concatenate(lat_rows + ctx_rows, axis=0)
    dmod_dev = jnp.pad(dmod_dev, ((0, (-dmod_dev.shape[0]) % 8), (0, 0)))
    gather3 = _ag8_column_copies(dmod_dev, nmod)
    _run_hosted("ag8_dmod", _merge_hosted([gather3, plan.hosted("ag8_dmod")]))
    plan.after("ag8_dmod")
    got3 = gather3.results[0]
    dlat = got3[:, :4 * batch].reshape(N_DEV, 4, batch, nmod).transpose(1, 0, 2, 3).reshape(4, N_DEV * batch, nmod)
    dctx_parts = got3[:, 4 * batch:4 * batch + 4].transpose(1, 0, 2)
    g_b_shard, dctx = _mod_bwd_rows(dlat, dctx_parts)
    d_rows = jnp.concatenate([dlat, dctx, jnp.zeros((4, MOD_ROWS - CTX_ROW - 1, nmod), F32)], axis=1)
    g_w_mod, dcc_part = _mod_bwd_w(c_rows.T, d_rows, w_mod, hosted=plan.hosted("mod_bwd_w"))
    plan.after("mod_bwd_w")

    wm2 = w_mod.reshape(-1, nmod)
    res_mod = _adamw("adamw_w_mod", None, g_w_mod.reshape(1, -1, nmod), wm2, M["w_mod"].reshape(-1, nmod),
                     V["w_mod"].reshape(-1, nmod), wm2.shape[0], hosted=plan.hosted("adamw_w_mod"))
    plan.after("adamw_w_mod")
    out = {"w_mod": tuple(r.reshape(w_mod.shape) for r in res_mod)}

    g_small_in = {n: g[n] for n in SMALL_SHARDED if n not in EARLY_SMALL}
    g_small_in.update(norm_g=g["norm_g"], cv_db=g["cv_db"], cv_ln_g=g["cv_ln_g"], cv_ln_b=g["cv_ln_b"],
                      ml_nope_norm=g["ml_nope_norm"], ml_rope_norm=g["ml_rope_norm"],
                      c_ctx=dcc_part.reshape(-1) * (mc == 0).astype(F32), loss=loss_part,
                      b_mod=lax.dynamic_update_slice(jnp.zeros((N_CHIP, 4, nmod), F32),
                                                     g_b_shard[None, :, 0] * (mc == 0).astype(F32), (chip, 0, 0)))
    small_names = list(g_small_in)
    pack4, spans4 = _pack_rows([g_small_in[n] for n in small_names], LANES, 128)
    gather4 = _ag8_copies(pack4)
    _run_hosted("ag8_small_grads", _merge_hosted([gather4, plan.hosted("ag8_small_grads")]))
    plan.after("ag8_small_grads")
    gs = dict(zip(small_names, _unpack_rows(_sum8(gather4.results[0]), spans4)))
    loss = gs["loss"][0, 0]
    gs.update(plan.early)
    gs["b_mod"] = gs["b_mod"].transpose(1, 0, 2).reshape(4, N_CHIP * nmod)
    for n in SMALL_SHARDED:
        wd = W[n].shape[-1]
        gs[n] = lax.dynamic_slice_in_dim(gs[n], chip * wd, wd, axis=1)
    upd_names = SMALL_REPLICATED + SMALL_SHARDED
    pw, spans_u = _pack_rows([W[n] for n in upd_names], LANES, 128)
    pm, _ = _pack_rows([M[n] for n in upd_names], LANES, 128)
    pv, _ = _pack_rows([V[n] for n in upd_names], LANES, 128)
    pg, _ = _pack_rows([gs[n].reshape(W[n].shape) for n in upd_names], LANES, 128)
    res_small = _adamw("adamw_small", None, pg[None], pw, pm, pv, pw.shape[0], hosted=plan.hosted("adamw_small"))
    plan.after("adamw_small")
    for n, vals in zip(upd_names, zip(*[_unpack_rows(r, spans_u) for r in res_small])):
        out[n] = vals
    out.update(plan.finish())

    outs = [loss, grad_x]
    for j in range(4):
        outs.extend(out[n][j] for n in WEIGHTS)
    return tuple(outs)
```

```python
import functools
import math

import jax
import jax.numpy as jnp
from jax import lax
from jax.experimental import pallas as pl
from jax.experimental.pallas import tpu as pltpu

F32 = jnp.float32
BF16 = jnp.bfloat16
ACT = jnp.bfloat16
MESH = pl.DeviceIdType.MESH

EPS = 1e-6
GRID_W = 64
CONV_WIDTH = 31
CONV_HALF = CONV_WIDTH // 2
CONV_PAD = 16
POOL_WINDOWS = (2, 4, 8, 16)
POOL_HALF = max(POOL_WINDOWS) // 2
HEADS = 8
NOPE = 128
ROPE = 64
HEAD_W = 256
VDIM = 128
KV_RANK = 256
Q_RANK = 384
ATT_SCALE = (NOPE + ROPE) ** -0.5
LN2 = math.log(2.0)
Q_PRESCALE = ATT_SCALE / LN2
ROPE_THETA = 10000.0
CHUNK = 128
CHUNK_GROUPS = 8
LANES = 128
TM = 256
TQ = 512
ATT_RQ = 128
ATT_KC = 256
VMEM_LIMIT = 56 * 1024 * 1024

ADAM_LR = 0.001
ADAM_B1 = 0.9
ADAM_B2 = 0.999
ADAM_EPS = 1e-08
ADAM_WD = 0.01
ADAM_STEP = 10


def _dot(a, b):
    return jnp.dot(a.astype(BF16), b.astype(BF16), preferred_element_type=F32)


def _dot_nt(a, b):
    return lax.dot_general(a.astype(BF16), b.astype(BF16), (((1,), (1,)), ((), ())), preferred_element_type=F32)


def _dot_tn(a, b):
    return lax.dot_general(a.astype(BF16), b.astype(BF16), (((0,), (0,)), ((), ())), preferred_element_type=F32)


@jax.custom_vjp
def _mm(a, w):
    return _dot(a, w)


def _mm_fwd(a, w):
    return _dot(a, w), (a, w)


def _mm_bwd(res, ct):
    a, w = res
    return _dot_nt(ct, w), _dot_tn(a, ct)


_mm.defvjp(_mm_fwd, _mm_bwd)


def _swap16_impl(x):
    n = x.shape[-1]
    ax = x.ndim - 1
    lane = lax.broadcasted_iota(jnp.int32, x.shape, ax)
    up = pltpu.roll(x, n - 16, ax)
    dn = pltpu.roll(x, 16, ax)
    return jnp.where((lane % 32) < 16, up, dn)


@jax.custom_vjp
def _swap16(x):
    return _swap16_impl(x)


_swap16.defvjp(lambda x: (_swap16_impl(x), None), lambda _, ct: (_swap16_impl(ct),))


def _rms(x, g, n=None):
    n = x.shape[-1] if n is None else n
    return x * lax.rsqrt(jnp.sum(x * x, axis=-1, keepdims=True) * (1.0 / n) + EPS) * g


def _layernorm(x, g, b):
    mu = jnp.mean(x, axis=-1, keepdims=True)
    xc = x - mu
    var = jnp.mean(xc * xc, axis=-1, keepdims=True)
    return xc * lax.rsqrt(var + EPS) * g + b


def _silu(x):
    return x * jax.nn.sigmoid(x)


def _rope(x, cos, sin):
    return x * cos + _swap16(x) * sin


ANY = pl.BlockSpec(memory_space=pl.ANY)


class _Hosted:
    def __init__(self, arrays, out_shapes, sems, start, wait, aliases=None):
        self.arrays, self.out_shapes, self.sems = list(arrays), list(out_shapes), list(sems)
        self.start, self.wait, self.aliases = start, wait, dict(aliases or {})
        self.results = None


def _merge_hosted(parts):
    parts = [p for p in parts if p is not None]
    if not parts:
        return None
    if len(parts) == 1:
        return parts[0]
    offs, a0, o0, s0 = [], 0, 0, 0
    for p in parts:
        offs.append((a0, o0, s0))
        a0, o0, s0 = a0 + len(p.arrays), o0 + len(p.out_shapes), s0 + len(p.sems)

    def run(which):
        def f(ins, outs, sems):
            for p, (a, o, s) in zip(parts, offs):
                getattr(p, which)(ins[a:a + len(p.arrays)], outs[o:o + len(p.out_shapes)], sems[s:s + len(p.sems)])
        return f

    aliases = {}
    for p, (a, o, _) in zip(parts, offs):
        aliases.update({a + i: o + j for i, j in p.aliases.items()})
    merged = _Hosted(sum((p.arrays for p in parts), []), sum((p.out_shapes for p in parts), []),
                     sum((p.sems for p in parts), []), run("start"), run("wait"), aliases)
    merged.parts, merged.offs = parts, offs
    return merged


def _deliver(hosted, results):
    hosted.results = list(results)
    for p, (_, o, _) in zip(getattr(hosted, "parts", []), getattr(hosted, "offs", [])):
        p.results = list(results[o:o + len(p.out_shapes)])


def _pcall(body, *, name, grid, in_specs, out_specs, out_shape, args, hosted=None, vmem_limit=True, scratch=()):
    n_in, n_out, n_scr = len(args), len(out_shape), len(scratch)
    kwargs = dict(scratch_shapes=list(scratch)) if scratch else {}
    if hosted is not None:
        nhi, nho, inner = len(hosted.arrays), len(hosted.out_shapes), body

        def body(*refs):
            ins, hin = refs[:n_in], refs[n_in:n_in + nhi]
            outs, hout = refs[n_in + nhi:n_in + nhi + n_out], refs[n_in + nhi + n_out:n_in + nhi + n_out + nho]
            own = refs[n_in + nhi + n_out + nho:n_in + nhi + n_out + nho + n_scr]
            sems = refs[n_in + nhi + n_out + nho + n_scr:]
            first, last = None, None
            for k, g in enumerate(grid):
                f, l = pl.program_id(k) == 0, pl.program_id(k) == g - 1
                first = f if first is None else jnp.logical_and(first, f)
                last = l if last is None else jnp.logical_and(last, l)

            @pl.when(first)
            def _():
                hosted.start(hin, hout, sems)

            inner(*ins, *outs, *own)

            @pl.when(last)
            def _():
                hosted.wait(hin, hout, sems)

        in_specs = list(in_specs) + [ANY] * nhi
        out_specs = list(out_specs) + [ANY] * nho
        out_shape = list(out_shape) + hosted.out_shapes
        args = list(args) + hosted.arrays
        kwargs = dict(scratch_shapes=list(scratch) + hosted.sems,
                      input_output_aliases={n_in + i: n_out + j for i, j in hosted.aliases.items()})
    params = dict(dimension_semantics=("arbitrary",) * len(grid))
    if vmem_limit:
        params["vmem_limit_bytes"] = VMEM_LIMIT
    res = pl.pallas_call(body, name=name, grid=grid, in_specs=list(in_specs), out_specs=list(out_specs),
                         out_shape=list(out_shape), compiler_params=pltpu.CompilerParams(**params), **kwargs)(*args)
    if hosted is not None:
        _deliver(hosted, res[n_out:])
    return list(res[:n_out])


def _run_hosted(name, hosted):
    nhi, nho = len(hosted.arrays), len(hosted.out_shapes)

    def body(*refs):
        ins, outs, sems = refs[:nhi], refs[nhi:nhi + nho], refs[nhi + nho:]
        hosted.start(ins, outs, sems)
        hosted.wait(ins, outs, sems)

    res = pl.pallas_call(body, name=name, in_specs=[ANY] * nhi, out_specs=[ANY] * nho, out_shape=hosted.out_shapes,
                         scratch_shapes=hosted.sems, input_output_aliases=hosted.aliases)(*hosted.arrays)
    _deliver(hosted, res)
    return list(res)


def _const_spec(shape, single=False):
    nd = len(shape)
    if single:
        return pl.BlockSpec(shape, lambda b, i: (0,) * nd, pipeline_mode=pl.Buffered(1))
    return pl.BlockSpec(shape, lambda b, i: (0,) * nd)


def _tile_spec(arr, n_lat_tiles, lat_only=False):
    bt, _, cw = arr.shape
    if lat_only:
        return pl.BlockSpec((1, TM, cw), lambda b, i: (b if bt > 1 else 0, jnp.minimum(i, n_lat_tiles - 1), 0))
    return pl.BlockSpec((1, TM, cw), lambda b, i: (b if bt > 1 else 0, i, 0))


def _eparam_spec(arr, n_lat_tiles):
    cw = arr.shape[-1]
    return pl.BlockSpec((1, 1, 1, cw), lambda b, i: (b, (i >= n_lat_tiles).astype(jnp.int32), 0, 0))


def _stage_fwd(name, *, pre, post, wsel, splits, tiles, eparams, sparams, weights, out_widths, out_dtypes,
               batch, n_tiles, n_lat_tiles, hosted=None):
    nt, ne, ns, nw = len(tiles), len(eparams), len(sparams), len(weights)

    def body(*refs):
        t_refs = refs[:nt]
        e_refs = refs[nt:nt + ne]
        s_refs = refs[nt + ne:nt + ne + ns]
        w_refs = refs[nt + ne + ns:nt + ne + ns + nw]
        o_refs = refs[nt + ne + ns + nw:]
        tv = [r[0].astype(F32) for r in t_refs]
        ev = [r[0, 0] for r in e_refs]
        sv = [r[...] for r in s_refs]
        a = pre(tv, ev, sv)
        z = [_dot(a[wsel[j]], w_refs[j][...]) for j in range(nw)]
        if post is None:
            outs = [z[j][:, s:s + w] for (j, s, w) in splits]
        else:
            outs = post(z, tv, ev, sv)
        for o_ref, o in zip(o_refs, outs):
            o_ref[0] = o.astype(o_ref.dtype)

    in_specs = ([_tile_spec(t, n_lat_tiles) for t in tiles] + [_eparam_spec(e, n_lat_tiles) for e in eparams]
                + [_const_spec(s.shape) for s in sparams] + [_const_spec(w.shape, single=True) for w in weights])
    out_shape = [jax.ShapeDtypeStruct((batch, n_tiles * TM, w), dt) for w, dt in zip(out_widths, out_dtypes)]
    out_specs = [pl.BlockSpec((1, TM, w), lambda b, i: (b, i, 0)) for w in out_widths]
    return _pcall(body, name=name, grid=(batch, n_tiles), in_specs=in_specs, out_specs=out_specs,
                  out_shape=out_shape, args=[*tiles, *eparams, *sparams, *weights], hosted=hosted)


def _stage_bwd(name, *, pre, post, wsel, splits, tiles, tile_diff, eparams, sparams, weights, cots, cot_lat_only,
               batch, n_tiles, n_lat_tiles, add=None, add_lat_only=False, hosted=None, w_col_stack=None,
               dt_lat_only=False):
    nt, ne, ns, nw, nc = len(tiles), len(eparams), len(sparams), len(weights), len(cots)
    diff_idx = [k for k in range(nt) if tile_diff[k]]
    nd = len(diff_idx)
    has_add = add is not None
    w_col_stack = w_col_stack or [None] * nw

    def body(*refs):
        pos = 0
        t_refs = refs[pos:pos + nt]; pos += nt
        e_refs = refs[pos:pos + ne]; pos += ne
        s_refs = refs[pos:pos + ns]; pos += ns
        w_refs = refs[pos:pos + nw]; pos += nw
        c_refs = refs[pos:pos + nc]; pos += nc
        if has_add:
            add_ref = refs[pos]; pos += 1
        dt_refs = refs[pos:pos + nd]; pos += nd
        de_refs = refs[pos:pos + ne]; pos += ne
        ds_refs = refs[pos:pos + ns]; pos += ns
        dw_refs = refs[pos:pos + nw]; pos += nw

        b = pl.program_id(0)
        i = pl.program_id(1)
        is_lat = i < n_lat_tiles
        tv = [r[0].astype(F32) for r in t_refs]
        ev = tuple(r[0, 0] for r in e_refs)
        sv = tuple(r[...] for r in s_refs)
        dv0 = tuple(tv[k] for k in diff_idx)

        def merge(dv):
            full = list(tv)
            for k, v in zip(diff_idx, dv):
                full[k] = v
            return full

        def pre_f(dv, ev_, sv_):
            return tuple(pre(merge(dv), list(ev_), list(sv_)))

        a, vjp_pre = jax.vjp(pre_f, dv0, ev, sv)
        cv = []
        for c_ref, lat in zip(c_refs, cot_lat_only):
            c = c_ref[0].astype(F32)
            cv.append(jnp.where(is_lat, c, 0.0) if lat else c)
        if post is None:
            dz = []
            for j in range(nw):
                parts = [cv[k] for k, (jj, _, _) in enumerate(splits) if jj == j]
                dz.append(parts[0] if len(parts) == 1 else jnp.concatenate(parts, axis=1))
            dt2 = de2 = ds2 = None
        else:
            z = tuple(_dot(a[wsel[j]], w_refs[j][...]) for j in range(nw))

            def post_f(z_, dv, ev_, sv_):
                return tuple(post(list(z_), merge(dv), list(ev_), list(sv_)))

            _, vjp_post = jax.vjp(post_f, z, dv0, ev, sv)
            dz, dt2, de2, ds2 = vjp_post(tuple(cv))
        da = [None] * len(a)
        dws = []
        for j in range(nw):
            g = _dot_nt(dz[j], w_refs[j][...])
            da[wsel[j]] = g if da[wsel[j]] is None else da[wsel[j]] + g
            dws.append(_dot_tn(a[wsel[j]], dz[j]))
        da = tuple(jnp.zeros_like(a[k]) if da[k] is None else da[k] for k in range(len(a)))
        dt1, de1, ds1 = vjp_pre(da)

        def plus(u, v):
            return u if v is None else u + v

        for k in range(nd):
            val = plus(dt1[k], None if dt2 is None else dt2[k])
            if has_add and k == 0:
                addv = add_ref[0].astype(F32)
                val = val + (jnp.where(is_lat, addv, 0.0) if add_lat_only else addv)
            if dt_lat_only:
                @pl.when(is_lat)
                def _(k=k, val=val):
                    dt_refs[k][0] = val.astype(dt_refs[k].dtype)
            else:
                dt_refs[k][0] = val.astype(dt_refs[k].dtype)

        seg_first = jnp.logical_or(i == 0, i == n_lat_tiles)
        for k in range(ne):
            val = plus(de1[k], None if de2 is None else de2[k])

            @pl.when(seg_first)
            def _(k=k, val=val):
                de_refs[k][0, 0] = val

            @pl.when(jnp.logical_not(seg_first))
            def _(k=k, val=val):
                de_refs[k][0, 0] += val

        first = jnp.logical_and(b == 0, i == 0)
        acc = [(ds_refs[k], plus(ds1[k], None if ds2 is None else ds2[k])) for k in range(ns)]
        for j in range(nw):
            if w_col_stack[j]:
                cw = dws[j].shape[1] // w_col_stack[j]
                acc += [(dw_refs[j].at[c], dws[j][:, c * cw:(c + 1) * cw]) for c in range(w_col_stack[j])]
            else:
                acc.append((dw_refs[j], dws[j]))
        for ref, val in acc:
            @pl.when(first)
            def _(ref=ref, val=val):
                ref[...] = val

            @pl.when(jnp.logical_not(first))
            def _(ref=ref, val=val):
                ref[...] += val

    in_specs = ([_tile_spec(t, n_lat_tiles) for t in tiles] + [_eparam_spec(e, n_lat_tiles) for e in eparams]
                + [_const_spec(s.shape) for s in sparams] + [_const_spec(w.shape, single=True) for w in weights]
                + [_tile_spec(c, n_lat_tiles, lat) for c, lat in zip(cots, cot_lat_only)])
    args = [*tiles, *eparams, *sparams, *weights, *cots]
    if has_add:
        in_specs.append(_tile_spec(add, n_lat_tiles, add_lat_only))
        args.append(add)
    dt_tiles = n_lat_tiles if dt_lat_only else n_tiles
    out_shape = [jax.ShapeDtypeStruct((batch, dt_tiles * TM, tiles[k].shape[-1]), F32) for k in diff_idx]
    out_specs = [pl.BlockSpec((1, TM, tiles[k].shape[-1]), lambda b, i: (b, jnp.minimum(i, dt_tiles - 1), 0))
                 for k in diff_idx]
    out_shape += [jax.ShapeDtypeStruct(e.shape, F32) for e in eparams]
    out_specs += [_eparam_spec(e, n_lat_tiles) for e in eparams]
    out_shape += [jax.ShapeDtypeStruct(s.shape, F32) for s in sparams]
    out_specs += [_const_spec(s.shape) for s in sparams]
    dw_shapes = [(n, w.shape[0], w.shape[1] // n) if n else w.shape for w, n in zip(weights, w_col_stack)]
    out_shape += [jax.ShapeDtypeStruct(s, F32) for s in dw_shapes]
    out_specs += [_const_spec(s, single=True) for s in dw_shapes]
    res = _pcall(body, name=name, grid=(batch, n_tiles), in_specs=in_specs, out_specs=out_specs,
                 out_shape=out_shape, args=args, hosted=hosted)
    return res[:nd], res[nd:nd + ne], res[nd + ne:nd + ne + ns], res[nd + ne + ns:]


def _pre_adaln(tv, ev, sv):
    x = tv[0]
    sh, sc = ev[0], ev[1]
    return [_rms(x, sv[0]) * (1.0 + sc) + sh]


def _post_residual(x_index):
    def post(z, tv, ev, sv):
        return [tv[x_index] + ev[-1] * z[0]]
    return post


def _pre_conv_out(tv, ev, sv):
    c1, gg = tv[0], tv[1]
    return [_silu(_layernorm(c1, sv[0], sv[1])) * _silu(gg)]


def _pre_pool_out(tv, ev, sv):
    pooled, gg = tv[0], tv[1]
    w_grp, scale = sv[0], sv[1]
    gw = w_grp.shape[-1]
    y = jnp.concatenate([_mm(pooled[:, k * gw:(k + 1) * gw], w_grp[k]) for k in range(w_grp.shape[0])], axis=1)
    return [y * scale * _silu(gg)]


def _pre_rms_only(tv, ev, sv):
    return [_rms(tv[0], sv[0])]


def _post_mla_keys(z, tv, ev, sv):
    krp, cos, sin = tv[1], tv[2], tv[3]
    nope_g, rope_g = sv[1], sv[2]
    kv = z[0]
    kr = _rope(_rms(krp, rope_g, ROPE), cos, sin)
    ks, vs = [], []
    for h in range(HEADS):
        ks.append(_rms(kv[:, h * 2 * NOPE:h * 2 * NOPE + NOPE], nope_g))
        ks.append(kr)
        vs.append(kv[:, h * 2 * NOPE + NOPE:(h + 1) * 2 * NOPE])
    return [jnp.concatenate(ks, axis=1), jnp.concatenate(vs, axis=1)]


def _post_mla_queries(z, tv, ev, sv):
    cos, sin = tv[1], tv[2]
    nope_g, rope_g = sv[1], sv[2]
    q = z[0]
    qs = []
    for h in range(HEADS):
        qs.append(_rms(q[:, h * HEAD_W:h * HEAD_W + NOPE], nope_g))
        qs.append(_rope(_rms(q[:, h * HEAD_W + NOPE:(h + 1) * HEAD_W], rope_g, ROPE), cos, sin))
    return [jnp.concatenate(qs, axis=1) * Q_PRESCALE]


def _pre_mla_out(tv, ev, sv):
    return [tv[0] * _silu(tv[1])]


def _pre_chunk_out(tv, ev, sv):
    u, v, gg = tv[0], tv[1], tv[2]
    ln_g, ln_b, w_s, b_s = sv
    vn = _layernorm(v, ln_g, ln_b)
    rows = []
    for n in range(vn.shape[0] // CHUNK):
        blk = vn[n * CHUNK:(n + 1) * CHUNK]
        cols = [_mm(w_s[g], blk[:, g * LANES:(g + 1) * LANES]) + b_s[:, g:g + 1] for g in range(CHUNK_GROUPS)]
        rows.append(jnp.concatenate(cols, axis=1))
    s = jnp.concatenate(rows, axis=0)
    return [u * s * _silu(gg)]


def _segments(lat_len, tot_len):
    segs = [(0, lat_len)]
    if tot_len > lat_len:
        segs.append((lat_len, tot_len - lat_len))
    return segs


def _pad_rows(x):
    z = jnp.zeros((CONV_PAD, x.shape[1]), x.dtype)
    return jnp.concatenate([z, x, z], axis=0)


def _shifted(xp, j):
    n = xp.shape[0] - 2 * CONV_PAD
    if j != 0:
        xp = pltpu.roll(xp, (-j) % xp.shape[0], 0)
    return xp[CONV_PAD:CONV_PAD + n]


def _conv_fwd(a, bgate, dw, db, lat_len, hosted=None):
    batch, tot, e = a.shape
    segs = _segments(lat_len, tot)

    def body(a_ref, b_ref, dw_ref, db_ref, o_ref):
        w = dw_ref[...]
        for (s0, n) in segs:
            y = a_ref[0, s0:s0 + n, :].astype(F32) * jax.nn.sigmoid(b_ref[0, s0:s0 + n, :].astype(F32))
            yp = _pad_rows(y)
            acc = jnp.zeros_like(y) + db_ref[...]
            for k in range(CONV_WIDTH):
                acc = acc + _shifted(yp, k - CONV_HALF) * w[k:k + 1, :]
            o_ref[0, s0:s0 + n, :] = acc.astype(o_ref.dtype)

    blk = pl.BlockSpec((1, tot, LANES), lambda b, cb: (b, 0, cb))
    return _pcall(
        body, name="conv_fwd", grid=(batch, e // LANES),
        in_specs=[blk, blk, pl.BlockSpec((CONV_WIDTH, LANES), lambda b, cb: (0, cb)),
                  pl.BlockSpec((1, LANES), lambda b, cb: (0, cb))],
        out_specs=[blk], out_shape=[jax.ShapeDtypeStruct(a.shape, ACT)], args=[a, bgate, dw, db], hosted=hosted)[0]


def _conv_bwd(a, bgate, dw, dc1, lat_len, hosted=None):
    batch, tot, e = a.shape
    segs = _segments(lat_len, tot)

    def body(a_ref, b_ref, dw_ref, dc_ref, da_ref, dg_ref, ddw_ref, ddb_ref):
        b = pl.program_id(1)
        w = dw_ref[...]
        ddw_rows = [None] * CONV_WIDTH
        ddb = None
        for (s0, n) in segs:
            av = a_ref[0, s0:s0 + n, :].astype(F32)
            sg = jax.nn.sigmoid(b_ref[0, s0:s0 + n, :].astype(F32))
            y = av * sg
            dc = dc_ref[0, s0:s0 + n, :]
            yp, dcp = _pad_rows(y), _pad_rows(dc)
            dy = jnp.zeros_like(y)
            for k in range(CONV_WIDTH):
                j = k - CONV_HALF
                dy = dy + _shifted(dcp, -j) * w[k:k + 1, :]
                r = jnp.sum(dc * _shifted(yp, j), axis=0, keepdims=True)
                ddw_rows[k] = r if ddw_rows[k] is None else ddw_rows[k] + r
            r = jnp.sum(dc, axis=0, keepdims=True)
            ddb = r if ddb is None else ddb + r
            da_ref[0, s0:s0 + n, :] = dy * sg
            dg_ref[0, s0:s0 + n, :] = dy * av * sg * (1.0 - sg)

        @pl.when(b == 0)
        def _():
            ddw_ref[...] = jnp.zeros_like(ddw_ref)
            ddb_ref[...] = jnp.zeros_like(ddb_ref)

        for k in range(CONV_WIDTH):
            ddw_ref[k:k + 1, :] += ddw_rows[k]
        ddb_ref[...] += ddb

    blk = pl.BlockSpec((1, tot, LANES), lambda cb, b: (b, 0, cb))
    wspec = pl.BlockSpec((CONV_WIDTH, LANES), lambda cb, b: (0, cb))
    bspec = pl.BlockSpec((1, LANES), lambda cb, b: (0, cb))
    return _pcall(
        body, name="conv_bwd", grid=(e // LANES, batch),
        in_specs=[blk, blk, wspec, blk],
        out_specs=[blk, blk, wspec, bspec],
        out_shape=[jax.ShapeDtypeStruct(a.shape, F32), jax.ShapeDtypeStruct(a.shape, F32),
                   jax.ShapeDtypeStruct((CONV_WIDTH, e), F32), jax.ShapeDtypeStruct((1, e), F32)],
        args=[a, bgate, dw, dc1], hosted=hosted)


def _pool_taps(group):
    half = lax.shift_left(jnp.int32(1), group)
    taps = []
    for j in range(-POOL_HALF, POOL_HALF):
        inside = jnp.logical_and(j >= -half, j < half)
        taps.append(jnp.where(inside, 1.0, 0.0).astype(F32))
    return taps, half


def _pool_counts(n, half, shape):
    t = lax.broadcasted_iota(jnp.int32, shape, 0)
    cnt = jnp.minimum(t + half, n) - jnp.maximum(t - half, 0)
    return cnt.astype(F32)


def _pool_fwd(v, lat_len, hosted=None):
    batch, tot, e = v.shape
    gw = e // len(POOL_WINDOWS)
    segs = _segments(lat_len, tot)

    def body(v_ref, o_ref):
        taps, half = _pool_taps(pl.program_id(1))
        for (s0, n) in segs:
            x = v_ref[0, s0:s0 + n, :]
            xp = _pad_rows(x)
            acc = jnp.zeros_like(x)
            for idx, j in enumerate(range(-POOL_HALF, POOL_HALF)):
                acc = acc + _shifted(xp, j) * taps[idx]
            o_ref[0, s0:s0 + n, :] = (acc / _pool_counts(n, half, x.shape) - x).astype(o_ref.dtype)

    blk = pl.BlockSpec((1, tot, gw), lambda b, g: (b, 0, g))
    return _pcall(body, name="pool_fwd", grid=(batch, len(POOL_WINDOWS)), in_specs=[blk], out_specs=[blk],
                  out_shape=[jax.ShapeDtypeStruct(v.shape, ACT)], args=[v], hosted=hosted)[0]


def _pool_bwd(dp, lat_len):
    batch, tot, e = dp.shape
    gw = e // len(POOL_WINDOWS)
    segs = _segments(lat_len, tot)

    def body(d_ref, o_ref):
        taps, half = _pool_taps(pl.program_id(1))
        for (s0, n) in segs:
            d = d_ref[0, s0:s0 + n, :]
            dnp = _pad_rows(d / _pool_counts(n, half, d.shape))
            acc = jnp.zeros_like(d)
            for idx, j in enumerate(range(-POOL_HALF, POOL_HALF)):
                acc = acc + _shifted(dnp, -j) * taps[idx]
            o_ref[0, s0:s0 + n, :] = acc - d

    blk = pl.BlockSpec((1, tot, gw), lambda b, g: (b, 0, g))
    return pl.pallas_call(
        body, name="pool_bwd", grid=(batch, len(POOL_WINDOWS)), in_specs=[blk], out_specs=blk,
        out_shape=jax.ShapeDtypeStruct(dp.shape, F32),
        compiler_params=pltpu.CompilerParams(dimension_semantics=("arbitrary", "arbitrary"),
                                             vmem_limit_bytes=VMEM_LIMIT),
    )(dp)


def _attn_fwd(q, k, v, hosted=None):
    batch, lq, _ = q.shape
    tk = k.shape[1]
    tq = min(TQ, lq)

    def body(q_ref, k_ref, v_ref, o_ref, lse_ref):
        s2 = _dot_nt(q_ref[0], k_ref[0])
        m2 = jnp.max(s2, axis=-1, keepdims=True)
        e = jnp.exp2(s2 - m2)
        l = jnp.sum(e, axis=-1, keepdims=True)
        o_ref[0] = (_dot(e, v_ref[0]) / l).astype(o_ref.dtype)
        lse_ref[0, 0] = m2 + jnp.log2(l)

    return _pcall(
        body, name="attn_fwd", grid=(batch, HEADS, lq // tq),
        in_specs=[pl.BlockSpec((1, tq, HEAD_W), lambda b, h, i: (b, i, h)),
                  pl.BlockSpec((1, tk, HEAD_W), lambda b, h, i: (b, 0, h)),
                  pl.BlockSpec((1, tk, VDIM), lambda b, h, i: (b, 0, h))],
        out_specs=[pl.BlockSpec((1, tq, VDIM), lambda b, h, i: (b, i, h)),
                   pl.BlockSpec((1, 1, tq, 1), lambda b, h, i: (b, h, i, 0))],
        out_shape=[jax.ShapeDtypeStruct((batch, lq, HEADS * VDIM), ACT),
                   jax.ShapeDtypeStruct((batch, HEADS, lq, 1), F32)], args=[q, k, v], hosted=hosted)


def _attn_bwd(q, k, v, o, lse, do, hosted=None):
    batch, lq, _ = q.shape
    tk = k.shape[1]
    tq = min(TQ, lq)

    def body(q_ref, k_ref, v_ref, o_ref, lse_ref, do_ref, dq_ref, dk_ref, dv_ref, p_scr, ds_scr):
        i = pl.program_id(2)
        nr = tq // ATT_RQ
        rows = [slice(r * ATT_RQ, (r + 1) * ATT_RQ) for r in range(nr)]
        qv = [q_ref[0, rw, :] for rw in rows]
        dob = [do_ref[0, rw, :].astype(BF16) for rw in rows]
        row_lse = [lse_ref[0, 0, rw, :] for rw in rows]
        delta = [jnp.sum(do_ref[0, rw, :] * o_ref[0, rw, :], axis=-1, keepdims=True) for rw in rows]
        for c in range(tk // ATT_KC):
            keys = slice(c * ATT_KC, (c + 1) * ATT_KC)
            kc, vc = k_ref[0, keys, :], v_ref[0, keys, :]
            for r in range(nr):
                p = jnp.exp2(_dot_nt(qv[r], kc) - row_lse[r])
                dp = _dot_nt(dob[r], vc)
                p_scr[rows[r], keys] = p.astype(BF16)
                ds_scr[rows[r], keys] = (p * (dp - delta[r]) * LN2).astype(BF16)
        dq_ref[0] = _dot(ds_scr[...], k_ref[0])
        dk = _dot_tn(ds_scr[...], q_ref[0])
        dv = _dot_tn(p_scr[...], do_ref[0])

        @pl.when(i == 0)
        def _():
            dk_ref[0] = dk
            dv_ref[0] = dv

        @pl.when(i != 0)
        def _():
            dk_ref[0] += dk
            dv_ref[0] += dv

    return _pcall(
        body, name="attn_bwd", grid=(batch, HEADS, lq // tq),
        in_specs=[pl.BlockSpec((1, tq, HEAD_W), lambda b, h, i: (b, i, h)),
                  pl.BlockSpec((1, tk, HEAD_W), lambda b, h, i: (b, 0, h)),
                  pl.BlockSpec((1, tk, VDIM), lambda b, h, i: (b, 0, h)),
                  pl.BlockSpec((1, tq, VDIM), lambda b, h, i: (b, i, h)),
                  pl.BlockSpec((1, 1, tq, 1), lambda b, h, i: (b, h, i, 0)),
                  pl.BlockSpec((1, tq, VDIM), lambda b, h, i: (b, i, h))],
        out_specs=[pl.BlockSpec((1, tq, HEAD_W), lambda b, h, i: (b, i, h)),
                   pl.BlockSpec((1, tk, HEAD_W), lambda b, h, i: (b, 0, h)),
                   pl.BlockSpec((1, tk, VDIM), lambda b, h, i: (b, 0, h))],
        out_shape=[jax.ShapeDtypeStruct(q.shape, F32), jax.ShapeDtypeStruct(k.shape, F32),
                   jax.ShapeDtypeStruct(v.shape, F32)],
        args=[q, k, v, o, lse, do], hosted=hosted,
        scratch=[pltpu.VMEM((tq, tk), BF16), pltpu.VMEM((tq, tk), BF16)])


def _loss_kernel(y, target):
    batch, lq, d = y.shape

    def body(y_ref, t_ref, l_ref, dy_ref):
        first = jnp.logical_and(pl.program_id(0) == 0, pl.program_id(1) == 0)
        err = y_ref[0] - t_ref[0]
        dy_ref[0] = err * (1.0 / d)
        part = jnp.zeros((1, LANES), F32) + jnp.sum(err * err) * (0.5 / d)

        @pl.when(first)
        def _():
            l_ref[...] = part

        @pl.when(jnp.logical_not(first))
        def _():
            l_ref[...] += part

    blk = pl.BlockSpec((1, TM, d), lambda b, i: (b, i, 0))
    return pl.pallas_call(
        body, name="loss_head", grid=(batch, lq // TM), in_specs=[blk, blk],
        out_specs=[pl.BlockSpec((1, LANES), lambda b, i: (0, 0)), blk],
        out_shape=[jax.ShapeDtypeStruct((1, LANES), F32), jax.ShapeDtypeStruct(y.shape, F32)],
        compiler_params=pltpu.CompilerParams(dimension_semantics=("arbitrary", "arbitrary")),
    )(y, target)


def _rope_tables(lat_len, ctx_len):
    rows = lat_len // GRID_W
    row_id = jnp.repeat(jnp.arange(rows), GRID_W).astype(F32)
    col_id = jnp.tile(jnp.arange(GRID_W), rows).astype(F32)
    axis_dim = ROPE // 2
    freqs = ROPE_THETA ** (-jnp.arange(0, axis_dim, 2, dtype=F32) / axis_dim)
    ar = row_id[:, None] * freqs
    ac = col_id[:, None] * freqs
    cr, sr, cc, sc = jnp.cos(ar), jnp.sin(ar), jnp.cos(ac), jnp.sin(ac)
    pad = jnp.zeros((lat_len, LANES - ROPE), F32)
    cos = jnp.concatenate([cr, cr, cc, cc, pad], axis=1)
    sin = jnp.concatenate([-sr, sr, -sc, sc, pad], axis=1)
    ident = jnp.concatenate([jnp.ones((ctx_len, ROPE), F32), jnp.zeros((ctx_len, LANES - ROPE), F32)], axis=1)
    cos = jnp.concatenate([cos, ident], axis=0)
    sin = jnp.concatenate([sin, jnp.zeros((ctx_len, LANES), F32)], axis=0)
    return cos[None], sin[None]


def _prep_weights(w):
    p = dict(w)
    kvc = KV_RANK + ROPE
    if "ml_w_in" in w:
        wi = w["ml_w_in"]
        p["ml_w_in"] = jnp.concatenate(
            [wi[:, :kvc], jnp.zeros((wi.shape[0], LANES - ROPE), wi.dtype), wi[:, kvc:]], axis=1)
    if "ml_w_uq" in w:
        uq = w["ml_w_uq"].reshape(Q_RANK, HEADS, NOPE + ROPE)
        p["ml_w_uq"] = jnp.pad(uq, ((0, 0), (0, 0), (0, HEAD_W - NOPE - ROPE))).reshape(Q_RANK, HEADS * HEAD_W)
    if "ml_rope_norm" in w:
        p["ml_rope_norm"] = jnp.pad(w["ml_rope_norm"], ((0, 0), (0, LANES - ROPE)))
    return p


def _unprep_grads(g):
    out = dict(g)
    kvc = KV_RANK + ROPE
    if "ml_w_in" in g:
        wi = g["ml_w_in"]
        out["ml_w_in"] = jnp.concatenate([wi[:, :kvc], wi[:, kvc + LANES - ROPE:]], axis=1)
    if "ml_w_uq" in g:
        uq = g["ml_w_uq"].reshape(Q_RANK, HEADS, HEAD_W)
        out["ml_w_uq"] = uq[:, :, :NOPE + ROPE].reshape(Q_RANK, HEADS * (NOPE + ROPE))
    if "ml_rope_norm" in g:
        out["ml_rope_norm"] = g["ml_rope_norm"][:, :ROPE]
    return out


LAYER_WEIGHTS = (("cv_w_in", "cv_w_out"), ("pl_w_in", "pl_w_grp", "pl_w_out"),
                 ("ml_w_in", "ml_w_uq", "ml_w_ukv", "ml_w_out"), ("ch_w_in", "ch_w_out"))


class _LocalPlan:
    def __init__(self, w):
        self.small = w
        self.grads = {}

    def weights(self, names):
        return {n: self.small[n] for n in names}

    def hosted(self, tag):
        return None

    def after(self, tag):
        pass

    def note(self, values):
        pass

    def layer_grads(self, layer, grads):
        self.grads.update(grads)


def _local_step(xm, target, mods, plan, lat_len):
    batch, tot, d = xm.shape
    e = d
    n_all, n_lat = tot // TM, lat_len // TM
    cos, sin = _rope_tables(lat_len, tot - lat_len)
    g = {}
    w = dict(plan.small)

    def hosting(tag, fn, *args, **kwargs):
        out = fn(*args, hosted=plan.hosted(tag), **kwargs)
        plan.after(tag)
        return out

    def s1_splits(widths):
        out, s = [], 0
        for wd in widths:
            out.append((0, s, wd))
            s += wd
        return out

    def fwd_in(name, x, mod, gi, wname, widths, n_tiles, dtypes=None):
        return hosting(name, _stage_fwd, name, pre=_pre_adaln, post=None, wsel=[0], splits=s1_splits(widths),
                       tiles=[x], eparams=[mod[0], mod[1]], sparams=[w["norm_g"][gi:gi + 1]], weights=[w[wname]],
                       out_widths=widths, out_dtypes=dtypes or [ACT] * len(widths), batch=batch, n_tiles=n_tiles,
                       n_lat_tiles=n_lat)

    def bwd_in(name, x, mod, gi, wname, widths, n_tiles, cots, lat_only, add, add_lat_only, stack=None,
               dx_lat_only=False):
        (dx,), (dsh, dsc), (dg,), (dw,) = hosting(
            name, _stage_bwd, name, pre=_pre_adaln, post=None, wsel=[0], splits=s1_splits(widths), tiles=[x],
            tile_diff=[True], eparams=[mod[0], mod[1]], sparams=[w["norm_g"][gi:gi + 1]], weights=[w[wname]],
            cots=cots, cot_lat_only=lat_only, batch=batch, n_tiles=n_tiles, n_lat_tiles=n_lat, add=add,
            add_lat_only=add_lat_only, w_col_stack=[stack], dt_lat_only=dx_lat_only)
        return dx, dsh, dsc, dg, dw

    def fwd_out(name, pre, tiles, mod, sparams, wname, n_tiles):
        return hosting(name, _stage_fwd, name, pre=pre, post=_post_residual(len(tiles) - 1), wsel=[0], splits=None,
                       tiles=tiles, eparams=[mod[2]], sparams=sparams, weights=[w[wname]], out_widths=[d],
                       out_dtypes=[F32], batch=batch, n_tiles=n_tiles, n_lat_tiles=n_lat)[0]

    def bwd_out(name, pre, tiles, mod, sparams, wname, n_tiles, cot):
        diff = [True] * (len(tiles) - 1) + [False]
        dts, (dgt,), dss, (dw,) = hosting(
            name, _stage_bwd, name, pre=pre, post=_post_residual(len(tiles) - 1), wsel=[0], splits=None, tiles=tiles,
            tile_diff=diff, eparams=[mod[2]], sparams=sparams, weights=[w[wname]], cots=[cot], cot_lat_only=[False],
            batch=batch, n_tiles=n_tiles, n_lat_tiles=n_lat)
        return dts, dgt, dss, dw

    w.update(plan.weights(("cv_w_in",)))
    cv_s = [w["cv_ln_g"], w["cv_ln_b"]]
    a0, b0, g0 = fwd_in("cv_in_fwd", xm, mods[0], 0, "cv_w_in", [e, e, e], n_all)
    c1 = hosting("conv_fwd", _conv_fwd, a0, b0, w["cv_dw"], w["cv_db"], lat_len)
    w.update(plan.weights(("cv_w_out",)))
    x1 = fwd_out("cv_out_fwd", _pre_conv_out, [c1, g0, xm], mods[0], cv_s, "cv_w_out", n_all)

    w.update(plan.weights(LAYER_WEIGHTS[1]))
    pl_s = [w["pl_w_grp"], w["pl_scale"]]
    v1, g1 = fwd_in("pl_in_fwd", x1, mods[1], 1, "pl_w_in", [e, e], n_all, dtypes=[F32, ACT])
    pooled = hosting("pool_fwd", _pool_fwd, v1, lat_len)
    x2 = fwd_out("pl_out_fwd", _pre_pool_out, [pooled, g1, x1], mods[1], pl_s, "pl_w_out", n_all)

    w.update(plan.weights(LAYER_WEIGHTS[2]))
    ml_widths = [KV_RANK, LANES, Q_RANK, HEADS * VDIM]
    ckv, krp, cq, g2 = fwd_in("ml_in_fwd", x2, mods[2], 2, "ml_w_in", ml_widths, n_all)
    k_s = [w["ml_kv_norm"], w["ml_nope_norm"][1:2], w["ml_rope_norm"][1:2]]
    q_s = [w["ml_q_norm"], w["ml_nope_norm"][0:1], w["ml_rope_norm"][0:1]]
    kk, vv = hosting("ml_keys_fwd", _stage_fwd, "ml_keys_fwd", pre=_pre_rms_only, post=_post_mla_keys, wsel=[0],
                     splits=None, tiles=[ckv, krp, cos, sin], eparams=[], sparams=k_s, weights=[w["ml_w_ukv"]],
                     out_widths=[HEADS * HEAD_W, HEADS * VDIM], out_dtypes=[BF16, BF16], batch=batch,
                     n_tiles=n_all, n_lat_tiles=n_lat)
    (qq,) = _stage_fwd("ml_queries_fwd", pre=_pre_rms_only, post=_post_mla_queries, wsel=[0], splits=None,
                       tiles=[cq, cos, sin], eparams=[], sparams=q_s, weights=[w["ml_w_uq"]],
                       out_widths=[HEADS * HEAD_W], out_dtypes=[BF16], batch=batch, n_tiles=n_lat,
                       n_lat_tiles=n_lat)
    att, lse = hosting("attn_fwd", _attn_fwd, qq, kk, vv)
    x3 = fwd_out("ml_out_fwd", _pre_mla_out, [att, g2, x2], mods[2], [], "ml_w_out", n_lat)

    w.update(plan.weights(LAYER_WEIGHTS[3]))
    ch_s = [w["ch_ln_g"], w["ch_ln_b"], w["ch_w_s"], w["ch_b_s"]]
    u3, v3, g3 = fwd_in("ch_in_fwd", x3, mods[3], 3, "ch_w_in", [e, e, e], n_lat)
    x4 = fwd_out("ch_out_fwd", _pre_chunk_out, [u3, v3, g3, x3], mods[3], ch_s, "ch_w_out", n_lat)

    loss_part, dy = _loss_kernel(x4, target)

    dmods = [None] * 4
    dnorm = [None] * 4
    big = {}
    (du, dv, dg), dgt, (g["ch_ln_g"], g["ch_ln_b"], g["ch_w_s"], g["ch_b_s"]), big["ch_w_out"] = bwd_out(
        "ch_out_bwd", _pre_chunk_out, [u3, v3, g3, x3], mods[3], ch_s, "ch_w_out", n_lat, dy)
    plan.note({n: g[n] for n in ("ch_ln_g", "ch_ln_b", "ch_w_s", "ch_b_s")})
    dx3, dsh, dsc, dnorm[3], big["ch_w_in"] = bwd_in("ch_in_bwd", x3, mods[3], 3, "ch_w_in", [e, e, e], n_lat,
                                                     [du, dv, dg], [False] * 3, dy, False, stack=N_CHIP)
    dmods[3] = (dsh, dsc, dgt)
    plan.layer_grads(3, big)

    big = {}
    (datt, dg), dgt, _, big["ml_w_out"] = bwd_out("ml_out_bwd", _pre_mla_out, [att, g2, x2], mods[2], [],
                                                  "ml_w_out", n_lat, dx3)
    dq, dk, dvv = hosting("attn_bwd", _attn_bwd, qq, kk, vv, att, lse, datt)
    (dcq,), _, (g["ml_q_norm"], dnope0, drope0), (big["ml_w_uq"],) = hosting(
        "ml_queries_bwd", _stage_bwd, "ml_queries_bwd", pre=_pre_rms_only, post=_post_mla_queries, wsel=[0],
        splits=None, tiles=[cq, cos, sin], tile_diff=[True, False, False], eparams=[], sparams=q_s,
        weights=[w["ml_w_uq"]], cots=[dq], cot_lat_only=[False], batch=batch, n_tiles=n_lat, n_lat_tiles=n_lat)
    (dckv, dkrp), _, (g["ml_kv_norm"], dnope1, drope1), (big["ml_w_ukv"],) = hosting(
        "ml_keys_bwd", _stage_bwd, "ml_keys_bwd", pre=_pre_rms_only, post=_post_mla_keys, wsel=[0], splits=None,
        tiles=[ckv, krp, cos, sin], tile_diff=[True, True, False, False], eparams=[], sparams=k_s,
        weights=[w["ml_w_ukv"]], cots=[dk, dvv], cot_lat_only=[False, False], batch=batch, n_tiles=n_all,
        n_lat_tiles=n_lat, w_col_stack=[N_CHIP])
    g["ml_nope_norm"] = jnp.concatenate([dnope0, dnope1], axis=0)
    g["ml_rope_norm"] = jnp.concatenate([drope0, drope1], axis=0)
    dx2, dsh, dsc, dnorm[2], big["ml_w_in"] = bwd_in("ml_in_bwd", x2, mods[2], 2, "ml_w_in", ml_widths, n_all,
                                                     [dckv, dkrp, dcq, dg], [False, False, True, True], dx3, True)
    dmods[2] = (dsh, dsc, dgt)
    plan.layer_grads(2, big)

    big = {}
    (dpooled, dg), dgt, (big["pl_w_grp"], g["pl_scale"]), big["pl_w_out"] = bwd_out(
        "pl_out_bwd", _pre_pool_out, [pooled, g1, x1], mods[1], pl_s, "pl_w_out", n_all, dx2)
    dv1 = _pool_bwd(dpooled, lat_len)
    dx1, dsh, dsc, dnorm[1], big["pl_w_in"] = bwd_in("pl_in_bwd", x1, mods[1], 1, "pl_w_in", [e, e], n_all,
                                                     [dv1, dg], [False] * 2, dx2, False, stack=N_CHIP)
    dmods[1] = (dsh, dsc, dgt)
    plan.layer_grads(1, big)

    big = {}
    (dc1, dg), dgt, (g["cv_ln_g"], g["cv_ln_b"]), big["cv_w_out"] = bwd_out(
        "cv_out_bwd", _pre_conv_out, [c1, g0, xm], mods[0], cv_s, "cv_w_out", n_all, dx1)
    da, db, g["cv_dw"], g["cv_db"] = hosting("conv_bwd", _conv_bwd, a0, b0, w["cv_dw"], dc1, lat_len)
    dx0, dsh, dsc, dnorm[0], big["cv_w_in"] = bwd_in("cv_in_bwd", xm, mods[0], 0, "cv_w_in", [e, e, e], n_all,
                                                     [da, db, dg], [False] * 3, dx1, False, stack=N_CHIP,
                                                     dx_lat_only=True)
    dmods[0] = (dsh, dsc, dgt)
    plan.layer_grads(0, big)
    g["norm_g"] = jnp.concatenate(dnorm, axis=0)
    return loss_part, dx0, dmods, g


N_DEV = 8
N_CHIP = 4
ANY = pl.BlockSpec(memory_space=pl.ANY)


def _my_place():
    return lax.axis_index("x"), lax.axis_index("y"), lax.axis_index("c")


def _flip(v, f):
    return 1 - v if f else v


def _ag8_copies(x):
    def plan(ins, outs, sems):
        mx, my, mc = _my_place()
        me = 4 * mx + 2 * my + mc
        sends, recvs = [], []
        for rel in range(1, N_DEV):
            peer = (_flip(mx, rel & 4), _flip(my, rel & 2), _flip(mc, rel & 1))
            src_dev = 4 * peer[0] + 2 * peer[1] + peer[2]
            sends.append(_remote(ins[0], outs[0].at[me], sems, rel - 1, peer))
            recvs.append(_remote(ins[0], outs[0].at[src_dev], sems, rel - 1, peer))
        return sends, recvs, [pltpu.make_async_copy(ins[0], outs[0].at[me], sems[2].at[0])]

    return _copies_hosted([x], [jax.ShapeDtypeStruct((N_DEV,) + x.shape, x.dtype)], (N_DEV - 1, N_DEV - 1, 1), plan)


def _ag8(name, x):
    return _run_hosted(name, _ag8_copies(x))[0]


def _ag8_column_copies(x, width):
    def plan(ins, outs, sems):
        mx, my, mc = _my_place()
        me = 4 * mx + 2 * my + mc
        sends, recvs = [], []
        for rel in range(1, N_DEV):
            peer = (_flip(mx, rel & 4), _flip(my, rel & 2), _flip(mc, rel & 1))
            src_dev = 4 * peer[0] + 2 * peer[1] + peer[2]
            cols = pl.ds(pl.multiple_of((2 * peer[0] + peer[1]) * width, LANES), width)
            sends.append(_remote(ins[0].at[:, cols], outs[0].at[me], sems, rel - 1, peer))
            recvs.append(_remote(ins[0].at[:, cols], outs[0].at[src_dev], sems, rel - 1, peer))
        mine = pl.ds(pl.multiple_of((2 * mx + my) * width, LANES), width)
        return sends, recvs, [pltpu.make_async_copy(ins[0].at[:, mine], outs[0].at[me], sems[2].at[0])]

    return _copies_hosted([x], [jax.ShapeDtypeStruct((N_DEV, x.shape[0], width), x.dtype)],
                          (N_DEV - 1, N_DEV - 1, 1), plan)


def _chip_rows_copies(x, rows_per_dev, shared_row):
    n_out = rows_per_dev + 1

    def plan(ins, outs, sems):
        mx, my, mc = _my_place()
        chip = 2 * mx + my
        sends, recvs = [], []

        def pieces(dev):
            return [(ins[0].at[pl.ds(dev * rows_per_dev, rows_per_dev)], slice(0, rows_per_dev)),
                    (ins[0].at[pl.ds(shared_row, 1)], slice(rows_per_dev, n_out))]

        for k, peer, pchip in _chip_peers(mx, my, mc):
            for t, (src, where) in enumerate(pieces(2 * pchip + mc)):
                sends.append(_remote(src, outs[0].at[chip, where], sems, 2 * k + t, peer))
                recvs.append(_remote(src, outs[0].at[pchip, where], sems, 2 * k + t, peer))
        locals_ = [pltpu.make_async_copy(src, outs[0].at[chip, where], sems[2].at[t])
                   for t, (src, where) in enumerate(pieces(2 * chip + mc))]
        return sends, recvs, locals_

    return _copies_hosted([x], [jax.ShapeDtypeStruct((N_CHIP, n_out) + x.shape[1:], x.dtype)], (6, 6, 2), plan)


def _chip_peers(mx, my, mc):
    out = []
    for rel in range(1, N_CHIP):
        px, py = _flip(mx, rel & 2), _flip(my, rel & 1)
        out.append((rel - 1, (px, py, mc), 2 * px + py))
    return out


def _half(mc, rows):
    return pl.ds(pl.multiple_of(mc * (rows // 2), 8), rows // 2)


def _copies_hosted(arrays, out_shapes, n_sems, plan, aliases=None):
    def start(ins, outs, sems):
        sends, _, locals_ = plan(ins, outs, sems)
        for cp in locals_ + sends:
            cp.start()

    def wait(ins, outs, sems):
        sends, recvs, locals_ = plan(ins, outs, sems)
        for cp in recvs:
            cp.wait_recv()
        for cp in sends:
            cp.wait_send()
        for cp in locals_:
            cp.wait()

    return _Hosted(arrays, out_shapes, [pltpu.SemaphoreType.DMA((k,)) for k in n_sems], start, wait, aliases)


def _remote(src, dst, sems, k, peer):
    return pltpu.make_async_remote_copy(src_ref=src, dst_ref=dst, send_sem=sems[0].at[k], recv_sem=sems[1].at[k],
                                        device_id=peer, device_id_type=MESH)


def _gather_ici(shards):
    n = len(shards)

    def plan(ins, outs, sems):
        mx, my, mc = _my_place()
        chip = 2 * mx + my
        sends, recvs, locals_ = [], [], []
        for a in range(n):
            rows = ins[a].shape[0]
            locals_.append(pltpu.make_async_copy(ins[a], outs[a].at[chip], sems[2].at[a]))
            for k, peer, pchip in _chip_peers(mx, my, mc):
                src = ins[a].at[_half(mc, rows)]
                sends.append(_remote(src, outs[a].at[chip, _half(mc, rows)], sems, 3 * a + k, peer))
                recvs.append(_remote(src, outs[a].at[pchip, _half(mc, rows)], sems, 3 * a + k, peer))
        return sends, recvs, locals_

    return _copies_hosted(shards, [jax.ShapeDtypeStruct((N_CHIP,) + s.shape, s.dtype) for s in shards],
                          (3 * n, 3 * n, n), plan)


def _sibling_fill(arrays, row_axis, chips_only_other):
    n = len(arrays)
    per = 3 if chips_only_other else 1

    def plan(ins, outs, sems):
        mx, my, mc = _my_place()
        sibling = (mx, my, 1 - mc)

        def views(a, core):
            rows = outs[a].shape[row_axis]
            if chips_only_other:
                return [outs[a].at[pchip, _half(core, rows)] for _, _, pchip in _chip_peers(mx, my, mc)]
            return [outs[a].at[_half(core, rows)]]

        sends, recvs = [], []
        for a in range(n):
            for k, v in enumerate(views(a, mc)):
                sends.append(_remote(v, v, sems, per * a + k, sibling))
            for k, v in enumerate(views(a, 1 - mc)):
                recvs.append(_remote(v, v, sems, per * a + k, sibling))
        return sends, recvs, []

    return _copies_hosted(arrays, [jax.ShapeDtypeStruct(s.shape, s.dtype) for s in arrays], (per * n, per * n), plan,
                          aliases={a: a for a in range(n)})


def _grad_swap_d2d(stacks):
    n = len(stacks)

    def plan(ins, outs, sems):
        mx, my, mc = _my_place()
        sibling = (mx, my, 1 - mc)
        sends = [_remote(ins[a].at[:, _half(1 - mc, ins[a].shape[1])], outs[a], sems, a, sibling) for a in range(n)]
        return sends, sends, []

    return _copies_hosted(stacks, [jax.ShapeDtypeStruct((N_CHIP, s.shape[1] // 2, s.shape[2]), s.dtype)
                                   for s in stacks], (n, n), plan)


def _grad_exchange_ici(parts):
    n = len(parts)

    def plan(ins, outs, sems):
        mx, my, mc = _my_place()
        chip = 2 * mx + my
        sends, recvs, locals_ = [], [], []
        for a in range(n):
            locals_.append(pltpu.make_async_copy(ins[a].at[chip], outs[a].at[chip], sems[2].at[a]))
            for k, peer, pchip in _chip_peers(mx, my, mc):
                sends.append(_remote(ins[a].at[pchip], outs[a].at[chip], sems, 3 * a + k, peer))
                recvs.append(_remote(ins[a].at[pchip], outs[a].at[pchip], sems, 3 * a + k, peer))
        return sends, recvs, locals_

    return _copies_hosted(parts, [jax.ShapeDtypeStruct(s.shape, s.dtype) for s in parts], (3 * n, 3 * n, n), plan)


def _row_block(rows, limit=256):
    for t in range(min(rows, limit), 7, -8):
        if rows % t == 0 and t % 8 == 0:
            return t
    return rows


def _grad_add_half(core, stack, received):
    _, rows, cw = stack.shape
    rh = rows // 2
    tr = _row_block(rh)

    def body(s_ref, a_ref, b_ref, o_ref):
        o_ref[...] = (a_ref[...] + b_ref[...]).astype(o_ref.dtype)

    grid_spec = pltpu.PrefetchScalarGridSpec(
        num_scalar_prefetch=1, grid=(rh // tr,),
        in_specs=[pl.BlockSpec((N_CHIP, tr, cw), lambda i, s: (0, s[0] * (rh // tr) + i, 0)),
                  pl.BlockSpec((N_CHIP, tr, cw), lambda i, s: (0, i, 0))],
        out_specs=pl.BlockSpec((N_CHIP, tr, cw), lambda i, s: (0, i, 0)))
    return pl.pallas_call(
        body, name="grad_add_half", grid_spec=grid_spec, out_shape=jax.ShapeDtypeStruct(received.shape, BF16),
        compiler_params=pltpu.CompilerParams(dimension_semantics=("arbitrary",), vmem_limit_bytes=VMEM_LIMIT),
    )(core, stack, received)


def _adamw(name, row_off, parts, w, m, v, rows, hosted=None):
    n, _, cw = parts.shape
    tr = _row_block(rows, 128)

    def update(p_ref, w_ref, m_ref, v_ref, g_ref, d_ref, nm_ref, nv_ref):
        g = p_ref[0].astype(F32)
        for k in range(1, n):
            g = g + p_ref[k].astype(F32)
        nm = ADAM_B1 * m_ref[...] + (1.0 - ADAM_B1) * g
        nv = ADAM_B2 * v_ref[...] + (1.0 - ADAM_B2) * (g * g)
        m_hat = nm / (1.0 - ADAM_B1 ** ADAM_STEP)
        v_hat = nv / (1.0 - ADAM_B2 ** ADAM_STEP)
        g_ref[...] = g
        d_ref[...] = -ADAM_LR * (m_hat / (jnp.sqrt(v_hat) + ADAM_EPS) + ADAM_WD * w_ref[...])
        nm_ref[...] = nm
        nv_ref[...] = nv

    out_shape = [jax.ShapeDtypeStruct(w.shape, F32)] * 4
    if row_off is None:
        blk = pl.BlockSpec((tr, cw), lambda i: (i, 0))
        return _pcall(update, name=name, grid=(rows // tr,), out_specs=[blk] * 4, out_shape=out_shape,
                      in_specs=[pl.BlockSpec((n, tr, cw), lambda i: (0, i, 0)), blk, blk, blk],
                      args=[parts, w, m, v], hosted=hosted)

    def body(s_ref, *refs):
        update(*refs)

    full = pl.BlockSpec((tr, cw), lambda i, s: (s[0] // tr + i, 0))
    grid_spec = pltpu.PrefetchScalarGridSpec(
        num_scalar_prefetch=1, grid=(rows // tr,),
        in_specs=[pl.BlockSpec((n, tr, cw), lambda i, s: (0, i, 0)), full, full, full],
        out_specs=[full, full, full, full])
    return pl.pallas_call(
        body, name=name, grid_spec=grid_spec, out_shape=out_shape,
        compiler_params=pltpu.CompilerParams(dimension_semantics=("arbitrary",), vmem_limit_bytes=VMEM_LIMIT),
    )(row_off, parts, w, m, v)


def _sum8(x):
    _, r, cw = x.shape
    tr = _row_block(r, 64)

    def body(x_ref, o_ref):
        acc = x_ref[0]
        for k in range(1, N_DEV):
            acc = acc + x_ref[k]
        o_ref[...] = acc

    return pl.pallas_call(
        body, name="sum8", grid=(r // tr,), in_specs=[pl.BlockSpec((N_DEV, tr, cw), lambda i: (0, i, 0))],
        out_specs=pl.BlockSpec((tr, cw), lambda i: (i, 0)), out_shape=jax.ShapeDtypeStruct((r, cw), F32),
        compiler_params=pltpu.CompilerParams(dimension_semantics=("arbitrary",)),
    )(x)


MOD_ROWS = 24
CTX_ROW = 16


def _mod_fwd(c_rows, w_mod, b_mod, hosted=None):
    nl, d, nn = w_mod.shape

    def body(c_ref, w_ref, b_ref, o_ref):
        o_ref[0] = _dot(_silu(c_ref[...]), w_ref[0]) + b_ref[0]

    return _pcall(
        body, name="mod_fwd", grid=(nl,),
        in_specs=[pl.BlockSpec((MOD_ROWS, d), lambda i: (0, 0)), pl.BlockSpec((1, d, nn), lambda i: (i, 0, 0)),
                  pl.BlockSpec((1, 1, nn), lambda i: (i, 0, 0))],
        out_specs=[pl.BlockSpec((1, MOD_ROWS, nn), lambda i: (i, 0, 0))],
        out_shape=[jax.ShapeDtypeStruct((nl, MOD_ROWS, nn), F32)], args=[c_rows, w_mod, b_mod], hosted=hosted)[0]


def _mod_bwd_rows(dlat, dctx_parts):
    nl, ne, nn = dlat.shape

    def body(l_ref, c_ref, db_ref, dc_ref):
        dc = c_ref[0, 0:1, :]
        for k in range(1, N_DEV):
            dc = dc + c_ref[0, k:k + 1, :]
        db = dc
        for k in range(ne):
            db = db + l_ref[0, k:k + 1, :]
        db_ref[0] = db
        dc_ref[0] = dc

    return pl.pallas_call(
        body, name="mod_bwd_rows", grid=(nl,),
        in_specs=[pl.BlockSpec((1, ne, nn), lambda i: (i, 0, 0)), pl.BlockSpec((1, N_DEV, nn), lambda i: (i, 0, 0))],
        out_specs=[pl.BlockSpec((1, 1, nn), lambda i: (i, 0, 0))] * 2,
        out_shape=[jax.ShapeDtypeStruct((nl, 1, nn), F32)] * 2,
        compiler_params=pltpu.CompilerParams(dimension_semantics=("arbitrary",)),
    )(dlat, dctx_parts)


def _mod_bwd_w(c_cols, d_rows, w_mod, hosted=None):
    nl, d, nn = w_mod.shape

    def body(c_ref, d_ref, w_ref, dw_ref, dc_ref):
        i = pl.program_id(0)
        c = c_ref[...]
        sg = jax.nn.sigmoid(c)
        s = c * sg
        dv = d_ref[0]
        acc = s[:, 0:1] * dv[0:1, :]
        for r in range(1, CTX_ROW + 1):
            acc = acc + s[:, r:r + 1] * dv[r:r + 1, :]
        dw_ref[0] = acc
        ds_ctx = jnp.sum(w_ref[0] * dv[CTX_ROW:CTX_ROW + 1, :], axis=1, keepdims=True)
        cc, sc = c[:, CTX_ROW:CTX_ROW + 1], sg[:, CTX_ROW:CTX_ROW + 1]
        part = ds_ctx * (sc * (1.0 + cc * (1.0 - sc)))

        @pl.when(i == 0)
        def _():
            dc_ref[...] = part

        @pl.when(i != 0)
        def _():
            dc_ref[...] += part

    return _pcall(
        body, name="mod_bwd_w", grid=(nl,),
        in_specs=[pl.BlockSpec((d, MOD_ROWS), lambda i: (0, 0)), pl.BlockSpec((1, MOD_ROWS, nn), lambda i: (i, 0, 0)),
                  pl.BlockSpec((1, d, nn), lambda i: (i, 0, 0))],
        out_specs=[pl.BlockSpec((1, d, nn), lambda i: (i, 0, 0)), pl.BlockSpec((d, 1), lambda i: (0, 0))],
        out_shape=[jax.ShapeDtypeStruct((nl, d, nn), F32), jax.ShapeDtypeStruct((d, 1), F32)],
        args=[c_cols, d_rows, w_mod], hosted=hosted)


def _pack_rows(arrays, width, row_multiple=8):
    rows, spans, r0 = [], [], 0
    for a in arrays:
        flat = a.reshape(-1)
        nr = -(-flat.shape[0] // width)
        held = -(-nr // 8) * 8
        flat = jnp.pad(flat, (0, held * width - flat.shape[0]))
        rows.append(flat.reshape(held, width))
        spans.append((r0, nr, a.shape))
        r0 += held
    if r0 % row_multiple:
        rows.append(jnp.zeros((row_multiple - r0 % row_multiple, width), F32))
    return jnp.concatenate(rows, axis=0), spans


def _unpack_rows(packed, spans):
    out = []
    for r0, nr, shape in spans:
        out.append(packed[r0:r0 + nr].reshape(-1)[:math.prod(shape)].reshape(shape))
    return out


BIG = {"cv_w_in": 1, "cv_w_out": 0, "pl_w_in": 1, "pl_w_grp": None, "pl_w_out": 0, "ml_w_in": 1, "ml_w_uq": 1,
       "ml_w_ukv": 1, "ml_w_out": 0, "ch_w_in": 1, "ch_w_out": 0}
SMALL_SHARDED = ["cv_dw", "pl_scale", "ml_q_norm", "ml_kv_norm", "ch_ln_g", "ch_ln_b"]
SMALL_REPLICATED = ["c_ctx", "norm_g", "b_mod", "cv_db", "cv_ln_g", "cv_ln_b", "ml_nope_norm", "ml_rope_norm",
                    "ch_w_s", "ch_b_s"]
WEIGHTS = ['c_ctx', 'norm_g', 'w_mod', 'b_mod', 'cv_w_in', 'cv_dw', 'cv_db', 'cv_ln_g', 'cv_ln_b', 'cv_w_out',
           'pl_w_in', 'pl_w_grp', 'pl_scale', 'pl_w_out', 'ml_w_in', 'ml_q_norm', 'ml_kv_norm', 'ml_w_uq', 'ml_w_ukv',
           'ml_nope_norm', 'ml_rope_norm', 'ml_w_out', 'ch_w_in', 'ch_ln_g', 'ch_ln_b', 'ch_w_s', 'ch_b_s', 'ch_w_out']


def _shard2d(name, a):
    if name == "pl_w_grp":
        return a.reshape(a.shape[-3] * a.shape[-2], a.shape[-1])
    return a.reshape(a.shape[-2], a.shape[-1])


def _unstack(name, s):
    if name == "pl_w_grp":
        ng = len(POOL_WINDOWS)
        return s.reshape(N_CHIP, ng, s.shape[1] // ng, s.shape[2]).transpose(1, 0, 2, 3).reshape(ng, -1, s.shape[2])
    if BIG[name] == 0:
        return s.reshape(-1, s.shape[2])
    return s.transpose(1, 0, 2).reshape(s.shape[1], -1)


def _stack(name, g):
    if g.ndim == 3 and name != "pl_w_grp":
        return g
    if name == "pl_w_grp":
        ng = len(POOL_WINDOWS)
        return g.reshape(ng, N_CHIP, -1, g.shape[2]).transpose(1, 0, 2, 3).reshape(N_CHIP, -1, g.shape[2])
    if BIG[name] == 0:
        return g.reshape(N_CHIP, -1, g.shape[1])
    return g.reshape(g.shape[0], N_CHIP, -1).transpose(1, 0, 2)


L0, L1, L2, L3 = LAYER_WEIGHTS
EARLY_SMALL = ("ch_w_s", "ch_b_s", "ch_ln_g", "ch_ln_b")
MESH_SCHEDULE = {
    "ag8_inputs": [("gather", L0[:1])], "mod_fwd": [("gfill", L0[:1])],
    "cv_in_fwd": [("gather", L0[1:]), ("gather", L1[:1])], "conv_fwd": [("gfill", L0[1:]), ("gather", L1[1:])],
    "cv_out_fwd": [("gfill", L1)],
    "pl_in_fwd": [("gather", L2[:1])], "pool_fwd": [("gather", L2[1:])], "pl_out_fwd": [("gfill", L2)],
    "attn_fwd": [("gather", L3)], "ml_out_fwd": [("gfill", L3)],
    "ch_in_bwd": [("small", EARLY_SMALL)],
    "ml_out_bwd": [("swap", L3)], "attn_bwd": [("exch", L3)], "ml_queries_bwd": [("ofill", L3)],
    "pl_out_bwd": [("swap", L2)], "pl_in_bwd": [("exch", L2)],
    "cv_out_bwd": [("swap", L1), ("ofill", L2)], "conv_bwd": [("exch", L1)], "cv_in_bwd": [("ofill", L1)],
    "ag8_dmod": [("swap", L0)], "mod_bwd_w": [("exch", L0)], "ag8_small_grads": [("ofill", L0)],
}


class _MeshPlan:
    def __init__(self, weights, m, v, core):
        self.W, self.M, self.V, self.core = weights, m, v, core
        self.small = None
        self.stack, self.gstack, self.part, self.half, self.out = {}, {}, {}, {}, {}
        self.notes, self.early = {}, {}
        self.live, self.done = {}, set()

    def _make(self, op, names):
        if op == "gather":
            return _gather_ici([_shard2d(n, self.W[n]).astype(BF16) for n in names])
        if op == "gfill":
            return _sibling_fill([self.stack[n] for n in names], 1, True)
        if op == "swap":
            return _grad_swap_d2d([self.gstack[n] for n in names])
        if op == "exch":
            return _grad_exchange_ici([self.part[n] for n in names])
        if op == "ofill":
            return _sibling_fill([t for n in names for t in self.half[n]], 0, False)
        pack, self.early_spans = _pack_rows([self.notes[n] for n in names], LANES, 128)
        return _ag8_copies(pack)

    def _finish_op(self, op, names, hosted):
        self.done.add((op, names))
        res = hosted.results
        if op in ("gather", "gfill"):
            self.stack.update(zip(names, res))
        elif op == "swap":
            for n, r in zip(names, res):
                self.part[n] = _grad_add_half(self.core, self.gstack[n], r)
        elif op == "exch":
            for n, q in zip(names, res):
                rh = q.shape[1]
                self.half[n] = _adamw("adamw_" + n, self.core * rh, q, _shard2d(n, self.W[n]),
                                      _shard2d(n, self.M[n]), _shard2d(n, self.V[n]), rh)
        elif op == "ofill":
            for k, n in enumerate(names):
                self.out[n] = tuple(r.reshape(self.W[n].shape) for r in res[4 * k:4 * k + 4])
        else:
            self.early.update(zip(names, _unpack_rows(_sum8(res[0]), self.early_spans)))

    def alone(self, op, names):
        hosted = self._make(op, names)
        _run_hosted("%s_%s" % (op, names[0]), hosted)
        self._finish_op(op, names, hosted)

    def weights(self, names):
        wk = {n: _unstack(n, self.stack[n]) for n in names}
        if "pl_w_grp" in wk:
            wk["pl_w_grp"] = wk["pl_w_grp"].astype(F32)
        return _prep_weights(wk)

    def hosted(self, tag):
        self.live[tag] = [(op, names, self._make(op, names)) for op, names in MESH_SCHEDULE.get(tag, [])]
        return _merge_hosted([h for _, _, h in self.live[tag]])

    def after(self, tag):
        for op, names, hosted in self.live.pop(tag, []):
            self._finish_op(op, names, hosted)

    def note(self, values):
        self.notes.update(values)

    def layer_grads(self, layer, grads):
        g = _unprep_grads(grads)
        for n in LAYER_WEIGHTS[layer]:
            self.gstack[n] = _stack(n, g[n])

    def finish(self):
        for names in (L3, L2, L1, L0):
            for op in ("swap", "exch", "ofill"):
                if (op, names) not in self.done:
                    self.alone(op, names)
        return self.out


def kernel(x, c, ctx, c_ctx, norm_g, w_mod, b_mod, cv_w_in, cv_dw, cv_db, cv_ln_g, cv_ln_b, cv_w_out, pl_w_in, pl_w_grp, pl_scale, pl_w_out, ml_w_in, ml_q_norm, ml_kv_norm, ml_w_uq, ml_w_ukv, ml_nope_norm, ml_rope_norm, ml_w_out, ch_w_in, ch_ln_g, ch_ln_b, ch_w_s, ch_b_s, ch_w_out, loss_target, m_c_ctx, m_norm_g, m_w_mod, m_b_mod, m_cv_w_in, m_cv_dw, m_cv_db, m_cv_ln_g, m_cv_ln_b, m_cv_w_out, m_pl_w_in, m_pl_w_grp, m_pl_scale, m_pl_w_out, m_ml_w_in, m_ml_q_norm, m_ml_kv_norm, m_ml_w_uq, m_ml_w_ukv, m_ml_nope_norm, m_ml_rope_norm, m_ml_w_out, m_ch_w_in, m_ch_ln_g, m_ch_ln_b, m_ch_w_s, m_ch_b_s, m_ch_w_out, v_c_ctx, v_norm_g, v_w_mod, v_b_mod, v_cv_w_in, v_cv_dw, v_cv_db, v_cv_ln_g, v_cv_ln_b, v_cv_w_out, v_pl_w_in, v_pl_w_grp, v_pl_scale, v_pl_w_out, v_ml_w_in, v_ml_q_norm, v_ml_kv_norm, v_ml_w_uq, v_ml_w_ukv, v_ml_nope_norm, v_ml_rope_norm, v_ml_w_out, v_ch_w_in, v_ch_ln_g, v_ch_ln_b, v_ch_w_s, v_ch_b_s, v_ch_w_out):
    W = dict(c_ctx=c_ctx, norm_g=norm_g, w_mod=w_mod, b_mod=b_mod, cv_w_in=cv_w_in, cv_dw=cv_dw, cv_db=cv_db, cv_ln_g=cv_ln_g, cv_ln_b=cv_ln_b, cv_w_out=cv_w_out, pl_w_in=pl_w_in, pl_w_grp=pl_w_grp, pl_scale=pl_scale, pl_w_out=pl_w_out, ml_w_in=ml_w_in, ml_q_norm=ml_q_norm, ml_kv_norm=ml_kv_norm, ml_w_uq=ml_w_uq, ml_w_ukv=ml_w_ukv, ml_nope_norm=ml_nope_norm, ml_rope_norm=ml_rope_norm, ml_w_out=ml_w_out, ch_w_in=ch_w_in, ch_ln_g=ch_ln_g, ch_ln_b=ch_ln_b, ch_w_s=ch_w_s, ch_b_s=ch_b_s, ch_w_out=ch_w_out)
    M = dict(c_ctx=m_c_ctx, norm_g=m_norm_g, w_mod=m_w_mod, b_mod=m_b_mod, cv_w_in=m_cv_w_in, cv_dw=m_cv_dw, cv_db=m_cv_db, cv_ln_g=m_cv_ln_g, cv_ln_b=m_cv_ln_b, cv_w_out=m_cv_w_out, pl_w_in=m_pl_w_in, pl_w_grp=m_pl_w_grp, pl_scale=m_pl_scale, pl_w_out=m_pl_w_out, ml_w_in=m_ml_w_in, ml_q_norm=m_ml_q_norm, ml_kv_norm=m_ml_kv_norm, ml_w_uq=m_ml_w_uq, ml_w_ukv=m_ml_w_ukv, ml_nope_norm=m_ml_nope_norm, ml_rope_norm=m_ml_rope_norm, ml_w_out=m_ml_w_out, ch_w_in=m_ch_w_in, ch_ln_g=m_ch_ln_g, ch_ln_b=m_ch_ln_b, ch_w_s=m_ch_w_s, ch_b_s=m_ch_b_s, ch_w_out=m_ch_w_out)
    V = dict(c_ctx=v_c_ctx, norm_g=v_norm_g, w_mod=v_w_mod, b_mod=v_b_mod, cv_w_in=v_cv_w_in, cv_dw=v_cv_dw, cv_db=v_cv_db, cv_ln_g=v_cv_ln_g, cv_ln_b=v_cv_ln_b, cv_w_out=v_cv_w_out, pl_w_in=v_pl_w_in, pl_w_grp=v_pl_w_grp, pl_scale=v_pl_scale, pl_w_out=v_pl_w_out, ml_w_in=v_ml_w_in, ml_q_norm=v_ml_q_norm, ml_kv_norm=v_ml_kv_norm, ml_w_uq=v_ml_w_uq, ml_w_ukv=v_ml_w_ukv, ml_nope_norm=v_ml_nope_norm, ml_rope_norm=v_ml_rope_norm, ml_w_out=v_ml_w_out, ch_w_in=v_ch_w_in, ch_ln_g=v_ch_ln_g, ch_ln_b=v_ch_ln_b, ch_w_s=v_ch_w_s, ch_b_s=v_ch_b_s, ch_w_out=v_ch_w_out)

    batch, lat_len, d = x.shape
    mx, my, mc = _my_place()
    chip = 2 * mx + my
    dev = 2 * chip + mc
    core = jnp.reshape(mc, (1,)).astype(jnp.int32)
    zero_off = jnp.zeros((1,), jnp.int32)
    big_names = list(BIG)

    sw = d // N_CHIP
    small_in = [c] + [jnp.pad(W[n].reshape(-1, W[n].shape[-1]), ((0, 0), (0, sw - W[n].shape[-1])))
                      for n in SMALL_SHARDED]
    pack1, spans1 = _pack_rows(small_in, sw)
    plan = _MeshPlan(W, M, V, core)
    gather1 = _ag8_copies(pack1)
    _run_hosted("ag8_inputs", _merge_hosted([gather1, plan.hosted("ag8_inputs")]))
    plan.after("ag8_inputs")
    got1 = gather1.results[0]
    c_all = got1[:, spans1[0][0]:spans1[0][0] + spans1[0][1]].reshape(N_DEV * batch, d)
    full_small = {}
    for n, (r0, nr, _) in zip(SMALL_SHARDED, spans1[1:]):
        blk = got1[0::2, r0:r0 + nr, :W[n].shape[-1]]
        full_small[n] = blk.transpose(1, 0, 2).reshape(nr, -1)

    c_rows = jnp.concatenate([c_all, c_ctx[None], jnp.zeros((MOD_ROWS - CTX_ROW - 1, d), F32)], axis=0)
    nmod = w_mod.shape[2]
    b_shard = lax.dynamic_slice(b_mod, (0, chip * nmod), (b_mod.shape[0], nmod))[:, None, :]
    mod_shard = _mod_fwd(c_rows, w_mod, b_shard, hosted=plan.hosted("mod_fwd"))
    plan.after("mod_fwd")
    mod_rows = mod_shard.transpose(1, 0, 2).reshape(MOD_ROWS, 1, 4 * nmod)
    got2 = _run_hosted("mod_exchange", _chip_rows_copies(mod_rows, batch, CTX_ROW))[0]
    mod_mine = got2.reshape(N_CHIP, batch + 1, 4, nmod).transpose(2, 1, 0, 3).reshape(4, batch + 1, 3 * d)
    mod_lat, mod_ctx = mod_mine[:, :batch], mod_mine[:, batch]
    mods = []
    for i in range(4):
        mods.append(tuple(
            jnp.stack([mod_lat[i, :, j * d:(j + 1) * d], jnp.broadcast_to(mod_ctx[i, j * d:(j + 1) * d], (batch, d))],
                      axis=1)[:, :, None, :] for j in range(3)))

    wk = dict(full_small)
    wk.update(norm_g=norm_g, cv_db=cv_db, cv_ln_g=cv_ln_g, cv_ln_b=cv_ln_b, ml_nope_norm=ml_nope_norm[0],
              ml_rope_norm=ml_rope_norm[0], ch_w_s=ch_w_s[0], ch_b_s=ch_b_s[0])
    plan.small = _prep_weights(wk)
    xm = jnp.concatenate([x, ctx], axis=1)
    loss_part, grad_x, dmods, g = _local_step(xm, loss_target, mods, plan, lat_len)
    g = _unprep_grads(g)

    lat_rows, ctx_rows = [], []
    for i in range(4):
        dsh, dsc, dgt = dmods[i]
        lat_rows.append(jnp.concatenate([dsh[:, 0, 0], dsc[:, 0, 0], dgt[:, 0, 0]], axis=1))
        zero = jnp.zeros((d,), F32)
        cs = [jnp.sum(t[:, 1, 0], axis=0) if ok else zero
              for t, ok in zip((dsh, dsc, dgt), (i <= 2, i <= 2, i <= 1))]
        ctx_rows.append(jnp.concatenate(cs, axis=0)[None])
    dmod_dev = jnp.concatenate(lat_rows + ctx_rows, axis=0)
    dmod_dev = jnp.pad(dmod_dev, ((0, (-dmod_dev.shape[0]) % 8), (0, 0)))
    gather3 = _ag8_column_copies(dmod_dev, nmod)
    _run_hosted("ag8_dmod", _merge_hosted([gather3, plan.hosted("ag8_dmod")]))
    plan.after("ag8_dmod")
    got3 = gather3.results[0]
    dlat = got3[:, :4 * batch].reshape(N_DEV, 4, batch, nmod).transpose(1, 0, 2, 3).reshape(4, N_DEV * batch, nmod)
    dctx_parts = got3[:, 4 * batch:4 * batch + 4].transpose(1, 0, 2)
    g_b_shard, dctx = _mod_bwd_rows(dlat, dctx_parts)
    d_rows = jnp.concatenate([dlat, dctx, jnp.zeros((4, MOD_ROWS - CTX_ROW - 1, nmod), F32)], axis=1)
    g_w_mod, dcc_part = _mod_bwd_w(c_rows.T, d_rows, w_mod, hosted=plan.hosted("mod_bwd_w"))
    plan.after("mod_bwd_w")

    wm2 = w_mod.reshape(-1, nmod)
    res_mod = _adamw("adamw_w_mod", None, g_w_mod.reshape(1, -1, nmod), wm2, M["w_mod"].reshape(-1, nmod),
                     V["w_mod"].reshape(-1, nmod), wm2.shape[0], hosted=plan.hosted("adamw_w_mod"))
    plan.after("adamw_w_mod")
    out = {"w_mod": tuple(r.reshape(w_mod.shape) for r in res_mod)}

    g_small_in = {n: g[n] for n in SMALL_SHARDED if n not in EARLY_SMALL}
    g_small_in.update(norm_g=g["norm_g"], cv_db=g["cv_db"], cv_ln_g=g["cv_ln_g"], cv_ln_b=g["cv_ln_b"],
                      ml_nope_norm=g["ml_nope_norm"], ml_rope_norm=g["ml_rope_norm"],
                      c_ctx=dcc_part.reshape(-1) * (mc == 0).astype(F32), loss=loss_part,
                      b_mod=lax.dynamic_update_slice(jnp.zeros((N_CHIP, 4, nmod), F32),
                                                     g_b_shard[None, :, 0] * (mc == 0).astype(F32), (chip, 0, 0)))
    small_names = list(g_small_in)
    pack4, spans4 = _pack_rows([g_small_in[n] for n in small_names], LANES, 128)
    gather4 = _ag8_copies(pack4)
    _run_hosted("ag8_small_grads", _merge_hosted([gather4, plan.hosted("ag8_small_grads")]))
    plan.after("ag8_small_grads")
    gs = dict(zip(small_names, _unpack_rows(_sum8(gather4.results[0]), spans4)))
    loss = gs["loss"][0, 0]
    gs.update(plan.early)
    gs["b_mod"] = gs["b_mod"].transpose(1, 0, 2).reshape(4, N_CHIP * nmod)
    for n in SMALL_SHARDED:
        wd = W[n].shape[-1]
        gs[n] = lax.dynamic_slice_in_dim(gs[n], chip * wd, wd, axis=1)
    upd_names = SMALL_REPLICATED + SMALL_SHARDED
    pw, spans_u = _pack_rows([W[n] for n in upd_names], LANES, 128)
    pm, _ = _pack_rows([M[n] for n in upd_names], LANES, 128)
    pv, _ = _pack_rows([V[n] for n in upd_names], LANES, 128)
    pg, _ = _pack_rows([gs[n].reshape(W[n].shape) for n in upd_names], LANES, 128)
    res_small = _adamw("adamw_small", None, pg[None], pw, pm, pv, pw.shape[0], hosted=plan.hosted("adamw_small"))
    plan.after("adamw_small")
    for n, vals in zip(upd_names, zip(*[_unpack_rows(r, spans_u) for r in res_small])):
        out[n] = vals
    out.update(plan.finish())

    outs = [loss, grad_x]
    for j in range(4):
        outs.extend(out[n][j] for n in WEIGHTS)
    return tuple(outs)
```

```python
import functools
import math

import jax
import jax.numpy as jnp
from jax import lax
from jax.experimental import pallas as pl
from jax.experimental.pallas import tpu as pltpu

F32 = jnp.float32
BF16 = jnp.bfloat16
ACT = jnp.float32
MESH = pl.DeviceIdType.MESH

EPS = 1e-6
GRID_W = 64
CONV_WIDTH = 31
CONV_HALF = CONV_WIDTH // 2
CONV_PAD = 16
POOL_WINDOWS = (2, 4, 8, 16)
POOL_HALF = max(POOL_WINDOWS) // 2
HEADS = 8
NOPE = 128
ROPE = 64
HEAD_W = 256
VDIM = 128
KV_RANK = 256
Q_RANK = 384
ATT_SCALE = (NOPE + ROPE) ** -0.5
LN2 = math.log(2.0)
Q_PRESCALE = ATT_SCALE / LN2
ROPE_THETA = 10000.0
CHUNK = 128
CHUNK_GROUPS = 8
LANES = 128
TM = 256
TM_LATENT = 512
TQ = 512
ATT_RQ = 128
ATT_KC = 256
VMEM_LIMIT = 56 * 1024 * 1024

ADAM_LR = 0.001
ADAM_B1 = 0.9
ADAM_B2 = 0.999
ADAM_EPS = 1e-08
ADAM_WD = 0.01
ADAM_STEP = 10


def _dot(a, b):
    return jnp.dot(a.astype(BF16), b.astype(BF16), preferred_element_type=F32)


def _dot_nt(a, b):
    return lax.dot_general(a.astype(BF16), b.astype(BF16), (((1,), (1,)), ((), ())), preferred_element_type=F32)


def _dot_tn(a, b):
    return lax.dot_general(a.astype(BF16), b.astype(BF16), (((0,), (0,)), ((), ())), preferred_element_type=F32)


@jax.custom_vjp
def _mm(a, w):
    return _dot(a, w)


def _mm_fwd(a, w):
    return _dot(a, w), (a, w)


def _mm_bwd(res, ct):
    a, w = res
    return _dot_nt(ct, w), _dot_tn(a, ct)


_mm.defvjp(_mm_fwd, _mm_bwd)


def _swap16_impl(x):
    n = x.shape[-1]
    ax = x.ndim - 1
    lane = lax.broadcasted_iota(jnp.int32, x.shape, ax)
    up = pltpu.roll(x, n - 16, ax)
    dn = pltpu.roll(x, 16, ax)
    return jnp.where((lane % 32) < 16, up, dn)


@jax.custom_vjp
def _swap16(x):
    return _swap16_impl(x)


_swap16.defvjp(lambda x: (_swap16_impl(x), None), lambda _, ct: (_swap16_impl(ct),))


def _rms(x, g, n=None):
    n = x.shape[-1] if n is None else n
    return x * lax.rsqrt(jnp.sum(x * x, axis=-1, keepdims=True) * (1.0 / n) + EPS) * g


def _layernorm(x, g, b):
    mu = jnp.mean(x, axis=-1, keepdims=True)
    xc = x - mu
    var = jnp.mean(xc * xc, axis=-1, keepdims=True)
    return xc * lax.rsqrt(var + EPS) * g + b


def _silu(x):
    return x * jax.nn.sigmoid(x)


def _rope(x, cos, sin):
    return x * cos + _swap16(x) * sin


ANY = pl.BlockSpec(memory_space=pl.ANY)


class _Hosted:
    def __init__(self, arrays, out_shapes, sems, start, wait, aliases=None):
        self.arrays, self.out_shapes, self.sems = list(arrays), list(out_shapes), list(sems)
        self.start, self.wait, self.aliases = start, wait, dict(aliases or {})
        self.results = None


def _merge_hosted(parts):
    parts = [p for p in parts if p is not None]
    if not parts:
        return None
    if len(parts) == 1:
        return parts[0]
    offs, a0, o0, s0 = [], 0, 0, 0
    for p in parts:
        offs.append((a0, o0, s0))
        a0, o0, s0 = a0 + len(p.arrays), o0 + len(p.out_shapes), s0 + len(p.sems)

    def run(which):
        def f(ins, outs, sems):
            for p, (a, o, s) in zip(parts, offs):
                getattr(p, which)(ins[a:a + len(p.arrays)], outs[o:o + len(p.out_shapes)], sems[s:s + len(p.sems)])
        return f

    aliases = {}
    for p, (a, o, _) in zip(parts, offs):
        aliases.update({a + i: o + j for i, j in p.aliases.items()})
    merged = _Hosted(sum((p.arrays for p in parts), []), sum((p.out_shapes for p in parts), []),
                     sum((p.sems for p in parts), []), run("start"), run("wait"), aliases)
    merged.parts, merged.offs = parts, offs
    return merged


def _deliver(hosted, results):
    hosted.results = list(results)
    for p, (_, o, _) in zip(getattr(hosted, "parts", []), getattr(hosted, "offs", [])):
        p.results = list(results[o:o + len(p.out_shapes)])


def _pcall(body, *, name, grid, in_specs, out_specs, out_shape, args, hosted=None, vmem_limit=True, scratch=()):
    n_in, n_out, n_scr = len(args), len(out_shape), len(scratch)
    kwargs = dict(scratch_shapes=list(scratch)) if scratch else {}
    if hosted is not None:
        nhi, nho, inner = len(hosted.arrays), len(hosted.out_shapes), body

        def body(*refs):
            ins, hin = refs[:n_in], refs[n_in:n_in + nhi]
            outs, hout = refs[n_in + nhi:n_in + nhi + n_out], refs[n_in + nhi + n_out:n_in + nhi + n_out + nho]
            own = refs[n_in + nhi + n_out + nho:n_in + nhi + n_out + nho + n_scr]
            sems = refs[n_in + nhi + n_out + nho + n_scr:]
            first, last = None, None
            for k, g in enumerate(grid):
                f, l = pl.program_id(k) == 0, pl.program_id(k) == g - 1
                first = f if first is None else jnp.logical_and(first, f)
                last = l if last is None else jnp.logical_and(last, l)

            @pl.when(first)
            def _():
                hosted.start(hin, hout, sems)

            inner(*ins, *outs, *own)

            @pl.when(last)
            def _():
                hosted.wait(hin, hout, sems)

        in_specs = list(in_specs) + [ANY] * nhi
        out_specs = list(out_specs) + [ANY] * nho
        out_shape = list(out_shape) + hosted.out_shapes
        args = list(args) + hosted.arrays
        kwargs = dict(scratch_shapes=list(scratch) + hosted.sems,
                      input_output_aliases={n_in + i: n_out + j for i, j in hosted.aliases.items()})
    params = dict(dimension_semantics=("arbitrary",) * len(grid))
    if vmem_limit:
        params["vmem_limit_bytes"] = VMEM_LIMIT
    res = pl.pallas_call(body, name=name, grid=grid, in_specs=list(in_specs), out_specs=list(out_specs),
                         out_shape=list(out_shape), compiler_params=pltpu.CompilerParams(**params), **kwargs)(*args)
    if hosted is not None:
        _deliver(hosted, res[n_out:])
    return list(res[:n_out])


def _run_hosted(name, hosted):
    nhi, nho = len(hosted.arrays), len(hosted.out_shapes)

    def body(*refs):
        ins, outs, sems = refs[:nhi], refs[nhi:nhi + nho], refs[nhi + nho:]
        hosted.start(ins, outs, sems)
        hosted.wait(ins, outs, sems)

    res = pl.pallas_call(body, name=name, in_specs=[ANY] * nhi, out_specs=[ANY] * nho, out_shape=hosted.out_shapes,
                         scratch_shapes=hosted.sems, input_output_aliases=hosted.aliases)(*hosted.arrays)
    _deliver(hosted, res)
    return list(res)


def _const_spec(shape, single=False):
    nd = len(shape)
    if single:
        return pl.BlockSpec(shape, lambda b, i: (0,) * nd, pipeline_mode=pl.Buffered(1))
    return pl.BlockSpec(shape, lambda b, i: (0,) * nd)


def _tile_spec(arr, n_lat_tiles, lat_only=False, tm=TM):
    bt, _, cw = arr.shape
    if lat_only:
        return pl.BlockSpec((1, tm, cw), lambda b, i: (b if bt > 1 else 0, jnp.minimum(i, n_lat_tiles - 1), 0))
    return pl.BlockSpec((1, tm, cw), lambda b, i: (b if bt > 1 else 0, i, 0))


def _eparam_spec(arr, n_lat_tiles):
    cw = arr.shape[-1]
    return pl.BlockSpec((1, 1, 1, cw), lambda b, i: (b, (i >= n_lat_tiles).astype(jnp.int32), 0, 0))


def _stage_fwd(name, *, pre, post, wsel, splits, tiles, eparams, sparams, weights, out_widths, out_dtypes,
               batch, n_tiles, n_lat_tiles, hosted=None, tm=TM):
    nt, ne, ns, nw = len(tiles), len(eparams), len(sparams), len(weights)

    def body(*refs):
        t_refs = refs[:nt]
        e_refs = refs[nt:nt + ne]
        s_refs = refs[nt + ne:nt + ne + ns]
        w_refs = refs[nt + ne + ns:nt + ne + ns + nw]
        o_refs = refs[nt + ne + ns + nw:]
        tv = [r[0].astype(F32) for r in t_refs]
        ev = [r[0, 0] for r in e_refs]
        sv = [r[...] for r in s_refs]
        a = pre(tv, ev, sv)
        z = [_dot(a[wsel[j]], w_refs[j][...]) for j in range(nw)]
        if post is None:
            outs = [z[j][:, s:s + w] for (j, s, w) in splits]
        else:
            outs = post(z, tv, ev, sv)
        for o_ref, o in zip(o_refs, outs):
            o_ref[0] = o.astype(o_ref.dtype)

    in_specs = ([_tile_spec(t, n_lat_tiles, tm=tm) for t in tiles] + [_eparam_spec(e, n_lat_tiles) for e in eparams]
                + [_const_spec(s.shape) for s in sparams] + [_const_spec(w.shape, single=True) for w in weights])
    out_shape = [jax.ShapeDtypeStruct((batch, n_tiles * tm, w), dt) for w, dt in zip(out_widths, out_dtypes)]
    out_specs = [pl.BlockSpec((1, tm, w), lambda b, i: (b, i, 0)) for w in out_widths]
    return _pcall(body, name=name, grid=(batch, n_tiles), in_specs=in_specs, out_specs=out_specs,
                  out_shape=out_shape, args=[*tiles, *eparams, *sparams, *weights], hosted=hosted)


def _stage_bwd(name, *, pre, post, wsel, splits, tiles, tile_diff, eparams, sparams, weights, cots, cot_lat_only,
               batch, n_tiles, n_lat_tiles, add=None, add_lat_only=False, hosted=None, w_col_stack=None,
               dt_lat_only=False, tm=TM):
    nt, ne, ns, nw, nc = len(tiles), len(eparams), len(sparams), len(weights), len(cots)
    diff_idx = [k for k in range(nt) if tile_diff[k]]
    nd = len(diff_idx)
    has_add = add is not None
    w_col_stack = w_col_stack or [None] * nw

    def body(*refs):
        pos = 0
        t_refs = refs[pos:pos + nt]; pos += nt
        e_refs = refs[pos:pos + ne]; pos += ne
        s_refs = refs[pos:pos + ns]; pos += ns
        w_refs = refs[pos:pos + nw]; pos += nw
        c_refs = refs[pos:pos + nc]; pos += nc
        if has_add:
            add_ref = refs[pos]; pos += 1
        dt_refs = refs[pos:pos + nd]; pos += nd
        de_refs = refs[pos:pos + ne]; pos += ne
        ds_refs = refs[pos:pos + ns]; pos += ns
        dw_refs = refs[pos:pos + nw]; pos += nw

        b = pl.program_id(0)
        i = pl.program_id(1)
        is_lat = i < n_lat_tiles
        tv = [r[0].astype(F32) for r in t_refs]
        ev = tuple(r[0, 0] for r in e_refs)
        sv = tuple(r[...] for r in s_refs)
        dv0 = tuple(tv[k] for k in diff_idx)

        def merge(dv):
            full = list(tv)
            for k, v in zip(diff_idx, dv):
                full[k] = v
            return full

        def pre_f(dv, ev_, sv_):
            return tuple(pre(merge(dv), list(ev_), list(sv_)))

        a, vjp_pre = jax.vjp(pre_f, dv0, ev, sv)
        cv = []
        for c_ref, lat in zip(c_refs, cot_lat_only):
            c = c_ref[0].astype(F32)
            cv.append(jnp.where(is_lat, c, 0.0) if lat else c)
        if post is None:
            dz = []
            for j in range(nw):
                parts = [cv[k] for k, (jj, _, _) in enumerate(splits) if jj == j]
                dz.append(parts[0] if len(parts) == 1 else jnp.concatenate(parts, axis=1))
            dt2 = de2 = ds2 = None
        else:
            z = tuple(_dot(a[wsel[j]], w_refs[j][...]) for j in range(nw))

            def post_f(z_, dv, ev_, sv_):
                return tuple(post(list(z_), merge(dv), list(ev_), list(sv_)))

            _, vjp_post = jax.vjp(post_f, z, dv0, ev, sv)
            dz, dt2, de2, ds2 = vjp_post(tuple(cv))
        da = [None] * len(a)
        dws = []
        for j in range(nw):
            g = _dot_nt(dz[j], w_refs[j][...])
            da[wsel[j]] = g if da[wsel[j]] is None else da[wsel[j]] + g
            dws.append(_dot_tn(a[wsel[j]], dz[j]))
        da = tuple(jnp.zeros_like(a[k]) if da[k] is None else da[k] for k in range(len(a)))
        dt1, de1, ds1 = vjp_pre(da)

        def plus(u, v):
            return u if v is None else u + v

        for k in range(nd):
            val = plus(dt1[k], None if dt2 is None else dt2[k])
            if has_add and k == 0:
                addv = add_ref[0].astype(F32)
                val = val + (jnp.where(is_lat, addv, 0.0) if add_lat_only else addv)
            if dt_lat_only:
                @pl.when(is_lat)
                def _(k=k, val=val):
                    dt_refs[k][0] = val.astype(dt_refs[k].dtype)
            else:
                dt_refs[k][0] = val.astype(dt_refs[k].dtype)

        seg_first = jnp.logical_or(i == 0, i == n_lat_tiles)
        for k in range(ne):
            val = plus(de1[k], None if de2 is None else de2[k])

            @pl.when(seg_first)
            def _(k=k, val=val):
                de_refs[k][0, 0] = val

            @pl.when(jnp.logical_not(seg_first))
            def _(k=k, val=val):
                de_refs[k][0, 0] += val

        first = jnp.logical_and(b == 0, i == 0)
        acc = [(ds_refs[k], plus(ds1[k], None if ds2 is None else ds2[k])) for k in range(ns)]
        for j in range(nw):
            if w_col_stack[j]:
                cw = dws[j].shape[1] // w_col_stack[j]
                acc += [(dw_refs[j].at[c], dws[j][:, c * cw:(c + 1) * cw]) for c in range(w_col_stack[j])]
            else:
                acc.append((dw_refs[j], dws[j]))
        for ref, val in acc:
            @pl.when(first)
            def _(ref=ref, val=val):
                ref[...] = val

            @pl.when(jnp.logical_not(first))
            def _(ref=ref, val=val):
                ref[...] += val

    in_specs = ([_tile_spec(t, n_lat_tiles, tm=tm) for t in tiles] + [_eparam_spec(e, n_lat_tiles) for e in eparams]
                + [_const_spec(s.shape) for s in sparams] + [_const_spec(w.shape, single=True) for w in weights]
                + [_tile_spec(c, n_lat_tiles, lat, tm) for c, lat in zip(cots, cot_lat_only)])
    args = [*tiles, *eparams, *sparams, *weights, *cots]
    if has_add:
        in_specs.append(_tile_spec(add, n_lat_tiles, add_lat_only, tm))
        args.append(add)
    dt_tiles = n_lat_tiles if dt_lat_only else n_tiles
    out_shape = [jax.ShapeDtypeStruct((batch, dt_tiles * tm, tiles[k].shape[-1]), F32) for k in diff_idx]
    out_specs = [pl.BlockSpec((1, tm, tiles[k].shape[-1]), lambda b, i: (b, jnp.minimum(i, dt_tiles - 1), 0))
                 for k in diff_idx]
    out_shape += [jax.ShapeDtypeStruct(e.shape, F32) for e in eparams]
    out_specs += [_eparam_spec(e, n_lat_tiles) for e in eparams]
    out_shape += [jax.ShapeDtypeStruct(s.shape, F32) for s in sparams]
    out_specs += [_const_spec(s.shape) for s in sparams]
    dw_shapes = [(n, w.shape[0], w.shape[1] // n) if n else w.shape for w, n in zip(weights, w_col_stack)]
    out_shape += [jax.ShapeDtypeStruct(s, F32) for s in dw_shapes]
    out_specs += [_const_spec(s, single=True) for s in dw_shapes]
    res = _pcall(body, name=name, grid=(batch, n_tiles), in_specs=in_specs, out_specs=out_specs,
                 out_shape=out_shape, args=args, hosted=hosted)
    return res[:nd], res[nd:nd + ne], res[nd + ne:nd + ne + ns], res[nd + ne + ns:]


def _pre_adaln(tv, ev, sv):
    x = tv[0]
    sh, sc = ev[0], ev[1]
    return [_rms(x, sv[0]) * (1.0 + sc) + sh]


def _post_residual(x_index):
    def post(z, tv, ev, sv):
        return [tv[x_index] + ev[-1] * z[0]]
    return post


def _pre_conv_out(tv, ev, sv):
    c1, gg = tv[0], tv[1]
    return [_silu(_layernorm(c1, sv[0], sv[1])) * _silu(gg)]


def _pre_pool_out(tv, ev, sv):
    pooled, gg = tv[0], tv[1]
    w_grp, scale = sv[0], sv[1]
    gw = w_grp.shape[-1]
    y = jnp.concatenate([_mm(pooled[:, k * gw:(k + 1) * gw], w_grp[k]) for k in range(w_grp.shape[0])], axis=1)
    return [y * scale * _silu(gg)]


def _pre_rms_only(tv, ev, sv):
    return [_rms(tv[0], sv[0])]


def _post_mla_keys(z, tv, ev, sv):
    krp, cos, sin = tv[1], tv[2], tv[3]
    nope_g, rope_g = sv[1], sv[2]
    kv = z[0]
    kr = _rope(_rms(krp, rope_g, ROPE), cos, sin)
    ks, vs = [], []
    for h in range(HEADS):
        ks.append(_rms(kv[:, h * 2 * NOPE:h * 2 * NOPE + NOPE], nope_g))
        ks.append(kr)
        vs.append(kv[:, h * 2 * NOPE + NOPE:(h + 1) * 2 * NOPE])
    return [jnp.concatenate(ks, axis=1), jnp.concatenate(vs, axis=1)]


def _post_mla_queries(z, tv, ev, sv):
    cos, sin = tv[1], tv[2]
    nope_g, rope_g = sv[1], sv[2]
    q = z[0]
    qs = []
    for h in range(HEADS):
        qs.append(_rms(q[:, h * HEAD_W:h * HEAD_W + NOPE], nope_g))
        qs.append(_rope(_rms(q[:, h * HEAD_W + NOPE:(h + 1) * HEAD_W], rope_g, ROPE), cos, sin))
    return [jnp.concatenate(qs, axis=1) * Q_PRESCALE]


def _pre_mla_out(tv, ev, sv):
    return [tv[0] * _silu(tv[1])]


def _pre_chunk_out(tv, ev, sv):
    u, v, gg = tv[0], tv[1], tv[2]
    ln_g, ln_b, w_s, b_s = sv
    vn = _layernorm(v, ln_g, ln_b)
    rows = []
    for n in range(vn.shape[0] // CHUNK):
        blk = vn[n * CHUNK:(n + 1) * CHUNK]
        cols = [_mm(w_s[g], blk[:, g * LANES:(g + 1) * LANES]) + b_s[:, g:g + 1] for g in range(CHUNK_GROUPS)]
        rows.append(jnp.concatenate(cols, axis=1))
    s = jnp.concatenate(rows, axis=0)
    return [u * s * _silu(gg)]


def _segments(lat_len, tot_len):
    segs = [(0, lat_len)]
    if tot_len > lat_len:
        segs.append((lat_len, tot_len - lat_len))
    return segs


def _pad_rows(x):
    z = jnp.zeros((CONV_PAD, x.shape[1]), x.dtype)
    return jnp.concatenate([z, x, z], axis=0)


def _shifted(xp, j):
    n = xp.shape[0] - 2 * CONV_PAD
    if j != 0:
        xp = pltpu.roll(xp, (-j) % xp.shape[0], 0)
    return xp[CONV_PAD:CONV_PAD + n]


def _conv_fwd(a, bgate, dw, db, lat_len, hosted=None):
    batch, tot, e = a.shape
    segs = _segments(lat_len, tot)

    def body(a_ref, b_ref, dw_ref, db_ref, o_ref):
        w = dw_ref[...]
        for (s0, n) in segs:
            y = a_ref[0, s0:s0 + n, :].astype(F32) * jax.nn.sigmoid(b_ref[0, s0:s0 + n, :].astype(F32))
            yp = _pad_rows(y)
            acc = jnp.zeros_like(y) + db_ref[...]
            for k in range(CONV_WIDTH):
                acc = acc + _shifted(yp, k - CONV_HALF) * w[k:k + 1, :]
            o_ref[0, s0:s0 + n, :] = acc.astype(o_ref.dtype)

    blk = pl.BlockSpec((1, tot, LANES), lambda b, cb: (b, 0, cb))
    return _pcall(
        body, name="conv_fwd", grid=(batch, e // LANES),
        in_specs=[blk, blk, pl.BlockSpec((CONV_WIDTH, LANES), lambda b, cb: (0, cb)),
                  pl.BlockSpec((1, LANES), lambda b, cb: (0, cb))],
        out_specs=[blk], out_shape=[jax.ShapeDtypeStruct(a.shape, ACT)], args=[a, bgate, dw, db], hosted=hosted)[0]


def _conv_bwd(a, bgate, dw, dc1, lat_len, hosted=None):
    batch, tot, e = a.shape
    segs = _segments(lat_len, tot)

    def body(a_ref, b_ref, dw_ref, dc_ref, da_ref, dg_ref, ddw_ref, ddb_ref):
        b = pl.program_id(1)
        w = dw_ref[...]
        ddw_rows = [None] * CONV_WIDTH
        ddb = None
        for (s0, n) in segs:
            av = a_ref[0, s0:s0 + n, :].astype(F32)
            sg = jax.nn.sigmoid(b_ref[0, s0:s0 + n, :].astype(F32))
            y = av * sg
            dc = dc_ref[0, s0:s0 + n, :]
            yp, dcp = _pad_rows(y), _pad_rows(dc)
            dy = jnp.zeros_like(y)
            for k in range(CONV_WIDTH):
                j = k - CONV_HALF
                dy = dy + _shifted(dcp, -j) * w[k:k + 1, :]
                r = jnp.sum(dc * _shifted(yp, j), axis=0, keepdims=True)
                ddw_rows[k] = r if ddw_rows[k] is None else ddw_rows[k] + r
            r = jnp.sum(dc, axis=0, keepdims=True)
            ddb = r if ddb is None else ddb + r
            da_ref[0, s0:s0 + n, :] = dy * sg
            dg_ref[0, s0:s0 + n, :] = dy * av * sg * (1.0 - sg)

        @pl.when(b == 0)
        def _():
            ddw_ref[...] = jnp.zeros_like(ddw_ref)
            ddb_ref[...] = jnp.zeros_like(ddb_ref)

        for k in range(CONV_WIDTH):
            ddw_ref[k:k + 1, :] += ddw_rows[k]
        ddb_ref[...] += ddb

    blk = pl.BlockSpec((1, tot, LANES), lambda cb, b: (b, 0, cb))
    wspec = pl.BlockSpec((CONV_WIDTH, LANES), lambda cb, b: (0, cb))
    bspec = pl.BlockSpec((1, LANES), lambda cb, b: (0, cb))
    return _pcall(
        body, name="conv_bwd", grid=(e // LANES, batch),
        in_specs=[blk, blk, wspec, blk],
        out_specs=[blk, blk, wspec, bspec],
        out_shape=[jax.ShapeDtypeStruct(a.shape, F32), jax.ShapeDtypeStruct(a.shape, F32),
                   jax.ShapeDtypeStruct((CONV_WIDTH, e), F32), jax.ShapeDtypeStruct((1, e), F32)],
        args=[a, bgate, dw, dc1], hosted=hosted)


def _pool_taps(group):
    half = lax.shift_left(jnp.int32(1), group)
    taps = []
    for j in range(-POOL_HALF, POOL_HALF):
        inside = jnp.logical_and(j >= -half, j < half)
        taps.append(jnp.where(inside, 1.0, 0.0).astype(F32))
    return taps, half


def _pool_counts(n, half, shape):
    t = lax.broadcasted_iota(jnp.int32, shape, 0)
    cnt = jnp.minimum(t + half, n) - jnp.maximum(t - half, 0)
    return cnt.astype(F32)


def _pool_fwd(v, lat_len, hosted=None):
    batch, tot, e = v.shape
    gw = e // len(POOL_WINDOWS)
    segs = _segments(lat_len, tot)

    def body(v_ref, o_ref):
        taps, half = _pool_taps(pl.program_id(1))
        for (s0, n) in segs:
            x = v_ref[0, s0:s0 + n, :]
            xp = _pad_rows(x)
            acc = jnp.zeros_like(x)
            for idx, j in enumerate(range(-POOL_HALF, POOL_HALF)):
                acc = acc + _shifted(xp, j) * taps[idx]
            o_ref[0, s0:s0 + n, :] = (acc / _pool_counts(n, half, x.shape) - x).astype(o_ref.dtype)

    blk = pl.BlockSpec((1, tot, gw), lambda b, g: (b, 0, g))
    return _pcall(body, name="pool_fwd", grid=(batch, len(POOL_WINDOWS)), in_specs=[blk], out_specs=[blk],
                  out_shape=[jax.ShapeDtypeStruct(v.shape, ACT)], args=[v], hosted=hosted)[0]


def _pool_bwd(dp, lat_len):
    batch, tot, e = dp.shape
    gw = e // len(POOL_WINDOWS)
    segs = _segments(lat_len, tot)

    def body(d_ref, o_ref):
        taps, half = _pool_taps(pl.program_id(1))
        for (s0, n) in segs:
            d = d_ref[0, s0:s0 + n, :]
            dnp = _pad_rows(d / _pool_counts(n, half, d.shape))
            acc = jnp.zeros_like(d)
            for idx, j in enumerate(range(-POOL_HALF, POOL_HALF)):
                acc = acc + _shifted(dnp, -j) * taps[idx]
            o_ref[0, s0:s0 + n, :] = acc - d

    blk = pl.BlockSpec((1, tot, gw), lambda b, g: (b, 0, g))
    return pl.pallas_call(
        body, name="pool_bwd", grid=(batch, len(POOL_WINDOWS)), in_specs=[blk], out_specs=blk,
        out_shape=jax.ShapeDtypeStruct(dp.shape, F32),
        compiler_params=pltpu.CompilerParams(dimension_semantics=("arbitrary", "arbitrary"),
                                             vmem_limit_bytes=VMEM_LIMIT),
    )(dp)


def _attn_fwd(q, k, v, hosted=None):
    batch, lq, _ = q.shape
    tk = k.shape[1]
    tq = min(TQ, lq)

    def body(q_ref, k_ref, v_ref, o_ref, lse_ref):
        s2 = _dot_nt(q_ref[0], k_ref[0])
        m2 = jnp.max(s2, axis=-1, keepdims=True)
        e = jnp.exp2(s2 - m2)
        l = jnp.sum(e, axis=-1, keepdims=True)
        o_ref[0] = (_dot(e, v_ref[0]) / l).astype(o_ref.dtype)
        lse_ref[0, 0] = m2 + jnp.log2(l)

    return _pcall(
        body, name="attn_fwd", grid=(batch, HEADS, lq // tq),
        in_specs=[pl.BlockSpec((1, tq, HEAD_W), lambda b, h, i: (b, i, h)),
                  pl.BlockSpec((1, tk, HEAD_W), lambda b, h, i: (b, 0, h)),
                  pl.BlockSpec((1, tk, VDIM), lambda b, h, i: (b, 0, h))],
        out_specs=[pl.BlockSpec((1, tq, VDIM), lambda b, h, i: (b, i, h)),
                   pl.BlockSpec((1, 1, tq, 1), lambda b, h, i: (b, h, i, 0))],
        out_shape=[jax.ShapeDtypeStruct((batch, lq, HEADS * VDIM), ACT),
                   jax.ShapeDtypeStruct((batch, HEADS, lq, 1), F32)], args=[q, k, v], hosted=hosted)


def _attn_bwd(q, k, v, o, lse, do, hosted=None):
    batch, lq, _ = q.shape
    tk = k.shape[1]
    tq = min(TQ, lq)

    def body(q_ref, k_ref, v_ref, o_ref, lse_ref, do_ref, dq_ref, dk_ref, dv_ref, p_scr, ds_scr):
        i = pl.program_id(2)
        nr = tq // ATT_RQ
        rows = [slice(r * ATT_RQ, (r + 1) * ATT_RQ) for r in range(nr)]
        qv = [q_ref[0, rw, :] for rw in rows]
        dob = [do_ref[0, rw, :].astype(BF16) for rw in rows]
        row_lse = [lse_ref[0, 0, rw, :] for rw in rows]
        delta = [jnp.sum(do_ref[0, rw, :] * o_ref[0, rw, :], axis=-1, keepdims=True) for rw in rows]
        for c in range(tk // ATT_KC):
            keys = slice(c * ATT_KC, (c + 1) * ATT_KC)
            kc, vc = k_ref[0, keys, :], v_ref[0, keys, :]
            for r in range(nr):
                p = jnp.exp2(_dot_nt(qv[r], kc) - row_lse[r])
                dp = _dot_nt(dob[r], vc)
                p_scr[rows[r], keys] = p.astype(BF16)
                ds_scr[rows[r], keys] = (p * (dp - delta[r]) * LN2).astype(BF16)
        dq_ref[0] = _dot(ds_scr[...], k_ref[0])
        dk = _dot_tn(ds_scr[...], q_ref[0])
        dv = _dot_tn(p_scr[...], do_ref[0])

        @pl.when(i == 0)
        def _():
            dk_ref[0] = dk
            dv_ref[0] = dv

        @pl.when(i != 0)
        def _():
            dk_ref[0] += dk
            dv_ref[0] += dv

    return _pcall(
        body, name="attn_bwd", grid=(batch, HEADS, lq // tq),
        in_specs=[pl.BlockSpec((1, tq, HEAD_W), lambda b, h, i: (b, i, h)),
                  pl.BlockSpec((1, tk, HEAD_W), lambda b, h, i: (b, 0, h)),
                  pl.BlockSpec((1, tk, VDIM), lambda b, h, i: (b, 0, h)),
                  pl.BlockSpec((1, tq, VDIM), lambda b, h, i: (b, i, h)),
                  pl.BlockSpec((1, 1, tq, 1), lambda b, h, i: (b, h, i, 0)),
                  pl.BlockSpec((1, tq, VDIM), lambda b, h, i: (b, i, h))],
        out_specs=[pl.BlockSpec((1, tq, HEAD_W), lambda b, h, i: (b, i, h)),
                   pl.BlockSpec((1, tk, HEAD_W), lambda b, h, i: (b, 0, h)),
                   pl.BlockSpec((1, tk, VDIM), lambda b, h, i: (b, 0, h))],
        out_shape=[jax.ShapeDtypeStruct(q.shape, F32), jax.ShapeDtypeStruct(k.shape, F32),
                   jax.ShapeDtypeStruct(v.shape, F32)],
        args=[q, k, v, o, lse, do], hosted=hosted,
        scratch=[pltpu.VMEM((tq, tk), BF16), pltpu.VMEM((tq, tk), BF16)])


def _loss_kernel(y, target):
    batch, lq, d = y.shape

    def body(y_ref, t_ref, l_ref, dy_ref):
        first = jnp.logical_and(pl.program_id(0) == 0, pl.program_id(1) == 0)
        err = y_ref[0] - t_ref[0]
        dy_ref[0] = err * (1.0 / d)
        part = jnp.zeros((1, LANES), F32) + jnp.sum(err * err) * (0.5 / d)

        @pl.when(first)
        def _():
            l_ref[...] = part

        @pl.when(jnp.logical_not(first))
        def _():
            l_ref[...] += part

    blk = pl.BlockSpec((1, TM, d), lambda b, i: (b, i, 0))
    return pl.pallas_call(
        body, name="loss_head", grid=(batch, lq // TM), in_specs=[blk, blk],
        out_specs=[pl.BlockSpec((1, LANES), lambda b, i: (0, 0)), blk],
        out_shape=[jax.ShapeDtypeStruct((1, LANES), F32), jax.ShapeDtypeStruct(y.shape, F32)],
        compiler_params=pltpu.CompilerParams(dimension_semantics=("arbitrary", "arbitrary")),
    )(y, target)


def _rope_tables(lat_len, ctx_len):
    rows = lat_len // GRID_W
    row_id = jnp.repeat(jnp.arange(rows), GRID_W).astype(F32)
    col_id = jnp.tile(jnp.arange(GRID_W), rows).astype(F32)
    axis_dim = ROPE // 2
    freqs = ROPE_THETA ** (-jnp.arange(0, axis_dim, 2, dtype=F32) / axis_dim)
    ar = row_id[:, None] * freqs
    ac = col_id[:, None] * freqs
    cr, sr, cc, sc = jnp.cos(ar), jnp.sin(ar), jnp.cos(ac), jnp.sin(ac)
    pad = jnp.zeros((lat_len, LANES - ROPE), F32)
    cos = jnp.concatenate([cr, cr, cc, cc, pad], axis=1)
    sin = jnp.concatenate([-sr, sr, -sc, sc, pad], axis=1)
    ident = jnp.concatenate([jnp.ones((ctx_len, ROPE), F32), jnp.zeros((ctx_len, LANES - ROPE), F32)], axis=1)
    cos = jnp.concatenate([cos, ident], axis=0)
    sin = jnp.concatenate([sin, jnp.zeros((ctx_len, LANES), F32)], axis=0)
    return cos[None], sin[None]


def _prep_weights(w):
    p = dict(w)
    kvc = KV_RANK + ROPE
    if "ml_w_in" in w:
        wi = w["ml_w_in"]
        p["ml_w_in"] = jnp.concatenate(
            [wi[:, :kvc], jnp.zeros((wi.shape[0], LANES - ROPE), wi.dtype), wi[:, kvc:]], axis=1)
    if "ml_w_uq" in w:
        uq = w["ml_w_uq"].reshape(Q_RANK, HEADS, NOPE + ROPE)
        p["ml_w_uq"] = jnp.pad(uq, ((0, 0), (0, 0), (0, HEAD_W - NOPE - ROPE))).reshape(Q_RANK, HEADS * HEAD_W)
    if "ml_rope_norm" in w:
        p["ml_rope_norm"] = jnp.pad(w["ml_rope_norm"], ((0, 0), (0, LANES - ROPE)))
    return p


def _unprep_grads(g):
    out = dict(g)
    kvc = KV_RANK + ROPE
    if "ml_w_in" in g:
        wi = g["ml_w_in"]
        out["ml_w_in"] = jnp.concatenate([wi[:, :kvc], wi[:, kvc + LANES - ROPE:]], axis=1)
    if "ml_w_uq" in g:
        uq = g["ml_w_uq"].reshape(Q_RANK, HEADS, HEAD_W)
        out["ml_w_uq"] = uq[:, :, :NOPE + ROPE].reshape(Q_RANK, HEADS * (NOPE + ROPE))
    if "ml_rope_norm" in g:
        out["ml_rope_norm"] = g["ml_rope_norm"][:, :ROPE]
    return out


LAYER_WEIGHTS = (("cv_w_in", "cv_w_out"), ("pl_w_in", "pl_w_grp", "pl_w_out"),
                 ("ml_w_in", "ml_w_uq", "ml_w_ukv", "ml_w_out"), ("ch_w_in", "ch_w_out"))


class _LocalPlan:
    def __init__(self, w):
        self.small = w
        self.grads = {}

    def weights(self, names):
        return {n: self.small[n] for n in names}

    def hosted(self, tag):
        return None

    def after(self, tag):
        pass

    def note(self, values):
        pass

    def layer_grads(self, layer, grads):
        self.grads.update(grads)


def _local_step(xm, target, mods, plan, lat_len):
    batch, tot, d = xm.shape
    e = d
    n_all, n_lat = tot // TM, lat_len // TM
    cos, sin = _rope_tables(lat_len, tot - lat_len)
    g = {}
    w = dict(plan.small)

    def hosting(tag, fn, *args, **kwargs):
        out = fn(*args, hosted=plan.hosted(tag), **kwargs)
        plan.after(tag)
        return out

    def s1_splits(widths):
        out, s = [], 0
        for wd in widths:
            out.append((0, s, wd))
            s += wd
        return out

    tml = TM_LATENT if lat_len % TM_LATENT == 0 else TM
    n_big = lat_len // tml

    def lat_tiles(n_tiles, tm):
        return n_lat if tm == TM else n_tiles

    def fwd_in(name, x, mod, gi, wname, widths, n_tiles, dtypes=None, tm=TM):
        return hosting(name, _stage_fwd, name, pre=_pre_adaln, post=None, wsel=[0], splits=s1_splits(widths),
                       tiles=[x], eparams=[mod[0], mod[1]], sparams=[w["norm_g"][gi:gi + 1]], weights=[w[wname]],
                       out_widths=widths, out_dtypes=dtypes or [ACT] * len(widths), batch=batch, n_tiles=n_tiles,
                       n_lat_tiles=lat_tiles(n_tiles, tm), tm=tm)

    def bwd_in(name, x, mod, gi, wname, widths, n_tiles, cots, lat_only, add, add_lat_only, stack=None,
               dx_lat_only=False):
        (dx,), (dsh, dsc), (dg,), (dw,) = hosting(
            name, _stage_bwd, name, pre=_pre_adaln, post=None, wsel=[0], splits=s1_splits(widths), tiles=[x],
            tile_diff=[True], eparams=[mod[0], mod[1]], sparams=[w["norm_g"][gi:gi + 1]], weights=[w[wname]],
            cots=cots, cot_lat_only=lat_only, batch=batch, n_tiles=n_tiles, n_lat_tiles=n_lat, add=add,
            add_lat_only=add_lat_only, w_col_stack=[stack], dt_lat_only=dx_lat_only)
        return dx, dsh, dsc, dg, dw

    def fwd_out(name, pre, tiles, mod, sparams, wname, n_tiles, tm=TM):
        return hosting(name, _stage_fwd, name, pre=pre, post=_post_residual(len(tiles) - 1), wsel=[0], splits=None,
                       tiles=tiles, eparams=[mod[2]], sparams=sparams, weights=[w[wname]], out_widths=[d],
                       out_dtypes=[F32], batch=batch, n_tiles=n_tiles, n_lat_tiles=lat_tiles(n_tiles, tm), tm=tm)[0]

    def bwd_out(name, pre, tiles, mod, sparams, wname, n_tiles, cot, tm=TM):
        diff = [True] * (len(tiles) - 1) + [False]
        dts, (dgt,), dss, (dw,) = hosting(
            name, _stage_bwd, name, pre=pre, post=_post_residual(len(tiles) - 1), wsel=[0], splits=None, tiles=tiles,
            tile_diff=diff, eparams=[mod[2]], sparams=sparams, weights=[w[wname]], cots=[cot], cot_lat_only=[False],
            batch=batch, n_tiles=n_tiles, n_lat_tiles=lat_tiles(n_tiles, tm), tm=tm)
        return dts, dgt, dss, dw

    w.update(plan.weights(("cv_w_in",)))
    cv_s = [w["cv_ln_g"], w["cv_ln_b"]]
    a0, b0, g0 = fwd_in("cv_in_fwd", xm, mods[0], 0, "cv_w_in", [e, e, e], n_all)
    c1 = hosting("conv_fwd", _conv_fwd, a0, b0, w["cv_dw"], w["cv_db"], lat_len)
    w.update(plan.weights(("cv_w_out",)))
    x1 = fwd_out("cv_out_fwd", _pre_conv_out, [c1, g0, xm], mods[0], cv_s, "cv_w_out", n_all)

    w.update(plan.weights(LAYER_WEIGHTS[1]))
    pl_s = [w["pl_w_grp"], w["pl_scale"]]
    v1, g1 = fwd_in("pl_in_fwd", x1, mods[1], 1, "pl_w_in", [e, e], n_all, dtypes=[F32, ACT])
    pooled = hosting("pool_fwd", _pool_fwd, v1, lat_len)
    x2 = fwd_out("pl_out_fwd", _pre_pool_out, [pooled, g1, x1], mods[1], pl_s, "pl_w_out", n_all)

    w.update(plan.weights(LAYER_WEIGHTS[2]))
    ml_widths = [KV_RANK, LANES, Q_RANK, HEADS * VDIM]
    ckv, krp, cq, g2 = fwd_in("ml_in_fwd", x2, mods[2], 2, "ml_w_in", ml_widths, n_all)
    k_s = [w["ml_kv_norm"], w["ml_nope_norm"][1:2], w["ml_rope_norm"][1:2]]
    q_s = [w["ml_q_norm"], w["ml_nope_norm"][0:1], w["ml_rope_norm"][0:1]]
    kk, vv = hosting("ml_keys_fwd", _stage_fwd, "ml_keys_fwd", pre=_pre_rms_only, post=_post_mla_keys, wsel=[0],
                     splits=None, tiles=[ckv, krp, cos, sin], eparams=[], sparams=k_s, weights=[w["ml_w_ukv"]],
                     out_widths=[HEADS * HEAD_W, HEADS * VDIM], out_dtypes=[BF16, BF16], batch=batch,
                     n_tiles=n_all, n_lat_tiles=n_lat)
    (qq,) = _stage_fwd("ml_queries_fwd", pre=_pre_rms_only, post=_post_mla_queries, wsel=[0], splits=None,
                       tiles=[cq, cos, sin], eparams=[], sparams=q_s, weights=[w["ml_w_uq"]],
                       out_widths=[HEADS * HEAD_W], out_dtypes=[BF16], batch=batch, n_tiles=n_big,
                       n_lat_tiles=n_big, tm=tml)
    att, lse = hosting("attn_fwd", _attn_fwd, qq, kk, vv)
    x3 = fwd_out("ml_out_fwd", _pre_mla_out, [att, g2, x2], mods[2], [], "ml_w_out", n_big, tm=tml)

    w.update(plan.weights(LAYER_WEIGHTS[3]))
    ch_s = [w["ch_ln_g"], w["ch_ln_b"], w["ch_w_s"], w["ch_b_s"]]
    u3, v3, g3 = fwd_in("ch_in_fwd", x3, mods[3], 3, "ch_w_in", [e, e, e], n_big, tm=tml)
    x4 = fwd_out("ch_out_fwd", _pre_chunk_out, [u3, v3, g3, x3], mods[3], ch_s, "ch_w_out", n_big, tm=tml)

    loss_part, dy = _loss_kernel(x4, target)

    dmods = [None] * 4
    dnorm = [None] * 4
    big = {}
    (du, dv, dg), dgt, (g["ch_ln_g"], g["ch_ln_b"], g["ch_w_s"], g["ch_b_s"]), big["ch_w_out"] = bwd_out(
        "ch_out_bwd", _pre_chunk_out, [u3, v3, g3, x3], mods[3], ch_s, "ch_w_out", n_big, dy, tm=tml)
    plan.note({n: g[n] for n in ("ch_ln_g", "ch_ln_b", "ch_w_s", "ch_b_s")})
    dx3, dsh, dsc, dnorm[3], big["ch_w_in"] = bwd_in("ch_in_bwd", x3, mods[3], 3, "ch_w_in", [e, e, e], n_lat,
                                                     [du, dv, dg], [False] * 3, dy, False, stack=N_CHIP)
    dmods[3] = (dsh, dsc, dgt)
    plan.layer_grads(3, big)

    big = {}
    (datt, dg), dgt, _, big["ml_w_out"] = bwd_out("ml_out_bwd", _pre_mla_out, [att, g2, x2], mods[2], [],
                                                  "ml_w_out", n_big, dx3, tm=tml)
    dq, dk, dvv = hosting("attn_bwd", _attn_bwd, qq, kk, vv, att, lse, datt)
    (dcq,), _, (g["ml_q_norm"], dnope0, drope0), (big["ml_w_uq"],) = hosting(
        "ml_queries_bwd", _stage_bwd, "ml_queries_bwd", pre=_pre_rms_only, post=_post_mla_queries, wsel=[0],
        splits=None, tiles=[cq, cos, sin], tile_diff=[True, False, False], eparams=[], sparams=q_s,
        weights=[w["ml_w_uq"]], cots=[dq], cot_lat_only=[False], batch=batch, n_tiles=n_big, n_lat_tiles=n_big,
        tm=tml)
    (dckv, dkrp), _, (g["ml_kv_norm"], dnope1, drope1), (big["ml_w_ukv"],) = hosting(
        "ml_keys_bwd", _stage_bwd, "ml_keys_bwd", pre=_pre_rms_only, post=_post_mla_keys, wsel=[0], splits=None,
        tiles=[ckv, krp, cos, sin], tile_diff=[True, True, False, False], eparams=[], sparams=k_s,
        weights=[w["ml_w_ukv"]], cots=[dk, dvv], cot_lat_only=[False, False], batch=batch, n_tiles=n_all,
        n_lat_tiles=n_lat, w_col_stack=[N_CHIP])
    g["ml_nope_norm"] = jnp.concatenate([dnope0, dnope1], axis=0)
    g["ml_rope_norm"] = jnp.concatenate([drope0, drope1], axis=0)
    dx2, dsh, dsc, dnorm[2], big["ml_w_in"] = bwd_in("ml_in_bwd", x2, mods[2], 2, "ml_w_in", ml_widths, n_all,
                                                     [dckv, dkrp, dcq, dg], [False, False, True, True], dx3, True)
    dmods[2] = (dsh, dsc, dgt)
    plan.layer_grads(2, big)

    big = {}
    (dpooled, dg), dgt, (big["pl_w_grp"], g["pl_scale"]), big["pl_w_out"] = bwd_out(
        "pl_out_bwd", _pre_pool_out, [pooled, g1, x1], mods[1], pl_s, "pl_w_out", n_all, dx2)
    dv1 = _pool_bwd(dpooled, lat_len)
    dx1, dsh, dsc, dnorm[1], big["pl_w_in"] = bwd_in("pl_in_bwd", x1, mods[1], 1, "pl_w_in", [e, e], n_all,
                                                     [dv1, dg], [False] * 2, dx2, False, stack=N_CHIP)
    dmods[1] = (dsh, dsc, dgt)
    plan.layer_grads(1, big)

    big = {}
    (dc1, dg), dgt, (g["cv_ln_g"], g["cv_ln_b"]), big["cv_w_out"] = bwd_out(
        "cv_out_bwd", _pre_conv_out, [c1, g0, xm], mods[0], cv_s, "cv_w_out", n_all, dx1)
    da, db, g["cv_dw"], g["cv_db"] = hosting("conv_bwd", _conv_bwd, a0, b0, w["cv_dw"], dc1, lat_len)
    dx0, dsh, dsc, dnorm[0], big["cv_w_in"] = bwd_in("cv_in_bwd", xm, mods[0], 0, "cv_w_in", [e, e, e], n_all,
                                                     [da, db, dg], [False] * 3, dx1, False, stack=N_CHIP,
                                                     dx_lat_only=True)
    dmods[0] = (dsh, dsc, dgt)
    plan.layer_grads(0, big)
    g["norm_g"] = jnp.concatenate(dnorm, axis=0)
    return loss_part, dx0, dmods, g


N_DEV = 8
N_CHIP = 4
ANY = pl.BlockSpec(memory_space=pl.ANY)


def _my_place():
    return lax.axis_index("x"), lax.axis_index("y"), lax.axis_index("c")


def _flip(v, f):
    return 1 - v if f else v


def _ag8_copies(x):
    def plan(ins, outs, sems):
        mx, my, mc = _my_place()
        me = 4 * mx + 2 * my + mc
        sends, recvs = [], []
        for rel in range(1, N_DEV):
            peer = (_flip(mx, rel & 4), _flip(my, rel & 2), _flip(mc, rel & 1))
            src_dev = 4 * peer[0] + 2 * peer[1] + peer[2]
            sends.append(_remote(ins[0], outs[0].at[me], sems, rel - 1, peer))
            recvs.append(_remote(ins[0], outs[0].at[src_dev], sems, rel - 1, peer))
        return sends, recvs, [pltpu.make_async_copy(ins[0], outs[0].at[me], sems[2].at[0])]

    return _copies_hosted([x], [jax.ShapeDtypeStruct((N_DEV,) + x.shape, x.dtype)], (N_DEV - 1, N_DEV - 1, 1), plan)


def _ag8(name, x):
    return _run_hosted(name, _ag8_copies(x))[0]


def _ag8_column_copies(x, width):
    def plan(ins, outs, sems):
        mx, my, mc = _my_place()
        me = 4 * mx + 2 * my + mc
        sends, recvs = [], []
        for rel in range(1, N_DEV):
            peer = (_flip(mx, rel & 4), _flip(my, rel & 2), _flip(mc, rel & 1))
            src_dev = 4 * peer[0] + 2 * peer[1] + peer[2]
            cols = pl.ds(pl.multiple_of((2 * peer[0] + peer[1]) * width, LANES), width)
            sends.append(_remote(ins[0].at[:, cols], outs[0].at[me], sems, rel - 1, peer))
            recvs.append(_remote(ins[0].at[:, cols], outs[0].at[src_dev], sems, rel - 1, peer))
        mine = pl.ds(pl.multiple_of((2 * mx + my) * width, LANES), width)
        return sends, recvs, [pltpu.make_async_copy(ins[0].at[:, mine], outs[0].at[me], sems[2].at[0])]

    return _copies_hosted([x], [jax.ShapeDtypeStruct((N_DEV, x.shape[0], width), x.dtype)],
                          (N_DEV - 1, N_DEV - 1, 1), plan)


def _chip_rows_copies(x, rows_per_dev, shared_row):
    n_out = rows_per_dev + 1

    def plan(ins, outs, sems):
        mx, my, mc = _my_place()
        chip = 2 * mx + my
        sends, recvs = [], []

        def pieces(dev):
            return [(ins[0].at[pl.ds(dev * rows_per_dev, rows_per_dev)], slice(0, rows_per_dev)),
                    (ins[0].at[pl.ds(shared_row, 1)], slice(rows_per_dev, n_out))]

        for k, peer, pchip in _chip_peers(mx, my, mc):
            for t, (src, where) in enumerate(pieces(2 * pchip + mc)):
                sends.append(_remote(src, outs[0].at[chip, where], sems, 2 * k + t, peer))
                recvs.append(_remote(src, outs[0].at[pchip, where], sems, 2 * k + t, peer))
        locals_ = [pltpu.make_async_copy(src, outs[0].at[chip, where], sems[2].at[t])
                   for t, (src, where) in enumerate(pieces(2 * chip + mc))]
        return sends, recvs, locals_

    return _copies_hosted([x], [jax.ShapeDtypeStruct((N_CHIP, n_out) + x.shape[1:], x.dtype)], (6, 6, 2), plan)


def _chip_peers(mx, my, mc):
    out = []
    for rel in range(1, N_CHIP):
        px, py = _flip(mx, rel & 2), _flip(my, rel & 1)
        out.append((rel - 1, (px, py, mc), 2 * px + py))
    return out


def _half(mc, rows):
    return pl.ds(pl.multiple_of(mc * (rows // 2), 8), rows // 2)


def _copies_hosted(arrays, out_shapes, n_sems, plan, aliases=None):
    def start(ins, outs, sems):
        sends, _, locals_ = plan(ins, outs, sems)
        for cp in locals_ + sends:
            cp.start()

    def wait(ins, outs, sems):
        sends, recvs, locals_ = plan(ins, outs, sems)
        for cp in recvs:
            cp.wait_recv()
        for cp in sends:
            cp.wait_send()
        for cp in locals_:
            cp.wait()

    return _Hosted(arrays, out_shapes, [pltpu.SemaphoreType.DMA((k,)) for k in n_sems], start, wait, aliases)


def _remote(src, dst, sems, k, peer):
    return pltpu.make_async_remote_copy(src_ref=src, dst_ref=dst, send_sem=sems[0].at[k], recv_sem=sems[1].at[k],
                                        device_id=peer, device_id_type=MESH)


def _gather_ici(shards):
    n = len(shards)

    def plan(ins, outs, sems):
        mx, my, mc = _my_place()
        chip = 2 * mx + my
        sends, recvs, locals_ = [], [], []
        for a in range(n):
            rows = ins[a].shape[0]
            locals_.append(pltpu.make_async_copy(ins[a], outs[a].at[chip], sems[2].at[a]))
            for k, peer, pchip in _chip_peers(mx, my, mc):
                src = ins[a].at[_half(mc, rows)]
                sends.append(_remote(src, outs[a].at[chip, _half(mc, rows)], sems, 3 * a + k, peer))
                recvs.append(_remote(src, outs[a].at[pchip, _half(mc, rows)], sems, 3 * a + k, peer))
        return sends, recvs, locals_

    return _copies_hosted(shards, [jax.ShapeDtypeStruct((N_CHIP,) + s.shape, s.dtype) for s in shards],
                          (3 * n, 3 * n, n), plan)


def _sibling_fill(arrays, row_axis, chips_only_other):
    n = len(arrays)
    per = 3 if chips_only_other else 1

    def plan(ins, outs, sems):
        mx, my, mc = _my_place()
        sibling = (mx, my, 1 - mc)

        def views(a, core):
            rows = outs[a].shape[row_axis]
            if chips_only_other:
                return [outs[a].at[pchip, _half(core, rows)] for _, _, pchip in _chip_peers(mx, my, mc)]
            return [outs[a].at[_half(core, rows)]]

        sends, recvs = [], []
        for a in range(n):
            for k, v in enumerate(views(a, mc)):
                sends.append(_remote(v, v, sems, per * a + k, sibling))
            for k, v in enumerate(views(a, 1 - mc)):
                recvs.append(_remote(v, v, sems, per * a + k, sibling))
        return sends, recvs, []

    return _copies_hosted(arrays, [jax.ShapeDtypeStruct(s.shape, s.dtype) for s in arrays], (per * n, per * n), plan,
                          aliases={a: a for a in range(n)})


def _grad_swap_d2d(stacks):
    n = len(stacks)

    def plan(ins, outs, sems):
        mx, my, mc = _my_place()
        sibling = (mx, my, 1 - mc)
        sends = [_remote(ins[a].at[:, _half(1 - mc, ins[a].shape[1])], outs[a], sems, a, sibling) for a in range(n)]
        return sends, sends, []

    return _copies_hosted(stacks, [jax.ShapeDtypeStruct((N_CHIP, s.shape[1] // 2, s.shape[2]), s.dtype)
                                   for s in stacks], (n, n), plan)


def _grad_exchange_ici(parts):
    n = len(parts)

    def plan(ins, outs, sems):
        mx, my, mc = _my_place()
        chip = 2 * mx + my
        sends, recvs, locals_ = [], [], []
        for a in range(n):
            locals_.append(pltpu.make_async_copy(ins[a].at[chip], outs[a].at[chip], sems[2].at[a]))
            for k, peer, pchip in _chip_peers(mx, my, mc):
                sends.append(_remote(ins[a].at[pchip], outs[a].at[chip], sems, 3 * a + k, peer))
                recvs.append(_remote(ins[a].at[pchip], outs[a].at[pchip], sems, 3 * a + k, peer))
        return sends, recvs, locals_

    return _copies_hosted(parts, [jax.ShapeDtypeStruct(s.shape, s.dtype) for s in parts], (3 * n, 3 * n, n), plan)


def _row_block(rows, limit=256):
    for t in range(min(rows, limit), 7, -8):
        if rows % t == 0 and t % 8 == 0:
            return t
    return rows


def _grad_add_half(core, stack, received):
    _, rows, cw = stack.shape
    rh = rows // 2
    tr = _row_block(rh)

    def body(s_ref, a_ref, b_ref, o_ref):
        o_ref[...] = (a_ref[...] + b_ref[...]).astype(o_ref.dtype)

    grid_spec = pltpu.PrefetchScalarGridSpec(
        num_scalar_prefetch=1, grid=(rh // tr,),
        in_specs=[pl.BlockSpec((N_CHIP, tr, cw), lambda i, s: (0, s[0] * (rh // tr) + i, 0)),
                  pl.BlockSpec((N_CHIP, tr, cw), lambda i, s: (0, i, 0))],
        out_specs=pl.BlockSpec((N_CHIP, tr, cw), lambda i, s: (0, i, 0)))
    return pl.pallas_call(
        body, name="grad_add_half", grid_spec=grid_spec, out_shape=jax.ShapeDtypeStruct(received.shape, BF16),
        compiler_params=pltpu.CompilerParams(dimension_semantics=("arbitrary",), vmem_limit_bytes=VMEM_LIMIT),
    )(core, stack, received)


def _adamw(name, row_off, parts, w, m, v, rows, hosted=None):
    n, _, cw = parts.shape
    tr = _row_block(rows, 128)

    def update(p_ref, w_ref, m_ref, v_ref, g_ref, d_ref, nm_ref, nv_ref):
        g = p_ref[0].astype(F32)
        for k in range(1, n):
            g = g + p_ref[k].astype(F32)
        nm = ADAM_B1 * m_ref[...] + (1.0 - ADAM_B1) * g
        nv = ADAM_B2 * v_ref[...] + (1.0 - ADAM_B2) * (g * g)
        m_hat = nm / (1.0 - ADAM_B1 ** ADAM_STEP)
        v_hat = nv / (1.0 - ADAM_B2 ** ADAM_STEP)
        g_ref[...] = g
        d_ref[...] = -ADAM_LR * (m_hat / (jnp.sqrt(v_hat) + ADAM_EPS) + ADAM_WD * w_ref[...])
        nm_ref[...] = nm
        nv_ref[...] = nv

    out_shape = [jax.ShapeDtypeStruct(w.shape, F32)] * 4
    if row_off is None:
        blk = pl.BlockSpec((tr, cw), lambda i: (i, 0))
        return _pcall(update, name=name, grid=(rows // tr,), out_specs=[blk] * 4, out_shape=out_shape,
                      in_specs=[pl.BlockSpec((n, tr, cw), lambda i: (0, i, 0)), blk, blk, blk],
                      args=[parts, w, m, v], hosted=hosted)

    def body(s_ref, *refs):
        update(*refs)

    full = pl.BlockSpec((tr, cw), lambda i, s: (s[0] // tr + i, 0))
    grid_spec = pltpu.PrefetchScalarGridSpec(
        num_scalar_prefetch=1, grid=(rows // tr,),
        in_specs=[pl.BlockSpec((n, tr, cw), lambda i, s: (0, i, 0)), full, full, full],
        out_specs=[full, full, full, full])
    return pl.pallas_call(
        body, name=name, grid_spec=grid_spec, out_shape=out_shape,
        compiler_params=pltpu.CompilerParams(dimension_semantics=("arbitrary",), vmem_limit_bytes=VMEM_LIMIT),
    )(row_off, parts, w, m, v)


def _sum8(x):
    _, r, cw = x.shape
    tr = _row_block(r, 64)

    def body(x_ref, o_ref):
        acc = x_ref[0]
        for k in range(1, N_DEV):
            acc = acc + x_ref[k]
        o_ref[...] = acc

    return pl.pallas_call(
        body, name="sum8", grid=(r // tr,), in_specs=[pl.BlockSpec((N_DEV, tr, cw), lambda i: (0, i, 0))],
        out_specs=pl.BlockSpec((tr, cw), lambda i: (i, 0)), out_shape=jax.ShapeDtypeStruct((r, cw), F32),
        compiler_params=pltpu.CompilerParams(dimension_semantics=("arbitrary",)),
    )(x)


MOD_ROWS = 24
CTX_ROW = 16


def _mod_fwd(c_rows, w_mod, b_mod, hosted=None):
    nl, d, nn = w_mod.shape

    def body(c_ref, w_ref, b_ref, o_ref):
        o_ref[0] = _dot(_silu(c_ref[...]), w_ref[0]) + b_ref[0]

    return _pcall(
        body, name="mod_fwd", grid=(nl,),
        in_specs=[pl.BlockSpec((MOD_ROWS, d), lambda i: (0, 0)), pl.BlockSpec((1, d, nn), lambda i: (i, 0, 0)),
                  pl.BlockSpec((1, 1, nn), lambda i: (i, 0, 0))],
        out_specs=[pl.BlockSpec((1, MOD_ROWS, nn), lambda i: (i, 0, 0))],
        out_shape=[jax.ShapeDtypeStruct((nl, MOD_ROWS, nn), F32)], args=[c_rows, w_mod, b_mod], hosted=hosted)[0]


def _mod_bwd_rows(dlat, dctx_parts):
    nl, ne, nn = dlat.shape

    def body(l_ref, c_ref, db_ref, dc_ref):
        dc = c_ref[0, 0:1, :]
        for k in range(1, N_DEV):
            dc = dc + c_ref[0, k:k + 1, :]
        db = dc
        for k in range(ne):
            db = db + l_ref[0, k:k + 1, :]
        db_ref[0] = db
        dc_ref[0] = dc

    return pl.pallas_call(
        body, name="mod_bwd_rows", grid=(nl,),
        in_specs=[pl.BlockSpec((1, ne, nn), lambda i: (i, 0, 0)), pl.BlockSpec((1, N_DEV, nn), lambda i: (i, 0, 0))],
        out_specs=[pl.BlockSpec((1, 1, nn), lambda i: (i, 0, 0))] * 2,
        out_shape=[jax.ShapeDtypeStruct((nl, 1, nn), F32)] * 2,
        compiler_params=pltpu.CompilerParams(dimension_semantics=("arbitrary",)),
    )(dlat, dctx_parts)


def _mod_bwd_w(c_cols, d_rows, w_mod, hosted=None):
    nl, d, nn = w_mod.shape

    def body(c_ref, d_ref, w_ref, dw_ref, dc_ref):
        i = pl.program_id(0)
        c = c_ref[...]
        sg = jax.nn.sigmoid(c)
        s = c * sg
        dv = d_ref[0]
        acc = s[:, 0:1] * dv[0:1, :]
        for r in range(1, CTX_ROW + 1):
            acc = acc + s[:, r:r + 1] * dv[r:r + 1, :]
        dw_ref[0] = acc
        ds_ctx = jnp.sum(w_ref[0] * dv[CTX_ROW:CTX_ROW + 1, :], axis=1, keepdims=True)
        cc, sc = c[:, CTX_ROW:CTX_ROW + 1], sg[:, CTX_ROW:CTX_ROW + 1]
        part = ds_ctx * (sc * (1.0 + cc * (1.0 - sc)))

        @pl.when(i == 0)
        def _():
            dc_ref[...] = part

        @pl.when(i != 0)
        def _():
            dc_ref[...] += part

    return _pcall(
        body, name="mod_bwd_w", grid=(nl,),
        in_specs=[pl.BlockSpec((d, MOD_ROWS), lambda i: (0, 0)), pl.BlockSpec((1, MOD_ROWS, nn), lambda i: (i, 0, 0)),
                  pl.BlockSpec((1, d, nn), lambda i: (i, 0, 0))],
        out_specs=[pl.BlockSpec((1, d, nn), lambda i: (i, 0, 0)), pl.BlockSpec((d, 1), lambda i: (0, 0))],
        out_shape=[jax.ShapeDtypeStruct((nl, d, nn), F32), jax.ShapeDtypeStruct((d, 1), F32)],
        args=[c_cols, d_rows, w_mod], hosted=hosted)


def _pack_rows(arrays, width, row_multiple=8):
    rows, spans, r0 = [], [], 0
    for a in arrays:
        flat = a.reshape(-1)
        nr = -(-flat.shape[0] // width)
        held = -(-nr // 8) * 8
        flat = jnp.pad(flat, (0, held * width - flat.shape[0]))
        rows.append(flat.reshape(held, width))
        spans.append((r0, nr, a.shape))
        r0 += held
    if r0 % row_multiple:
        rows.append(jnp.zeros((row_multiple - r0 % row_multiple, width), F32))
    return jnp.concatenate(rows, axis=0), spans


def _unpack_rows(packed, spans):
    out = []
    for r0, nr, shape in spans:
        out.append(packed[r0:r0 + nr].reshape(-1)[:math.prod(shape)].reshape(shape))
    return out


BIG = {"cv_w_in": 1, "cv_w_out": 0, "pl_w_in": 1, "pl_w_grp": None, "pl_w_out": 0, "ml_w_in": 1, "ml_w_uq": 1,
       "ml_w_ukv": 1, "ml_w_out": 0, "ch_w_in": 1, "ch_w_out": 0}
SMALL_SHARDED = ["cv_dw", "pl_scale", "ml_q_norm", "ml_kv_norm", "ch_ln_g", "ch_ln_b"]
SMALL_REPLICATED = ["c_ctx", "norm_g", "b_mod", "cv_db", "cv_ln_g", "cv_ln_b", "ml_nope_norm", "ml_rope_norm",
                    "ch_w_s", "ch_b_s"]
WEIGHTS = ['c_ctx', 'norm_g', 'w_mod', 'b_mod', 'cv_w_in', 'cv_dw', 'cv_db', 'cv_ln_g', 'cv_ln_b', 'cv_w_out',
           'pl_w_in', 'pl_w_grp', 'pl_scale', 'pl_w_out', 'ml_w_in', 'ml_q_norm', 'ml_kv_norm', 'ml_w_uq', 'ml_w_ukv',
           'ml_nope_norm', 'ml_rope_norm', 'ml_w_out', 'ch_w_in', 'ch_ln_g', 'ch_ln_b', 'ch_w_s', 'ch_b_s', 'ch_w_out']


def _shard2d(name, a):
    if name == "pl_w_grp":
        return a.reshape(a.shape[-3] * a.shape[-2], a.shape[-1])
    return a.reshape(a.shape[-2], a.shape[-1])


def _unstack(name, s):
    if name == "pl_w_grp":
        ng = len(POOL_WINDOWS)
        return s.reshape(N_CHIP, ng, s.shape[1] // ng, s.shape[2]).transpose(1, 0, 2, 3).reshape(ng, -1, s.shape[2])
    if BIG[name] == 0:
        return s.reshape(-1, s.shape[2])
    return s.transpose(1, 0, 2).reshape(s.shape[1], -1)


def _stack(name, g):
    if g.ndim == 3 and name != "pl_w_grp":
        return g
    if name == "pl_w_grp":
        ng = len(POOL_WINDOWS)
        return g.reshape(ng, N_CHIP, -1, g.shape[2]).transpose(1, 0, 2, 3).reshape(N_CHIP, -1, g.shape[2])
    if BIG[name] == 0:
        return g.reshape(N_CHIP, -1, g.shape[1])
    return g.reshape(g.shape[0], N_CHIP, -1).transpose(1, 0, 2)


L0, L1, L2, L3 = LAYER_WEIGHTS
EARLY_SMALL = ("ch_w_s", "ch_b_s", "ch_ln_g", "ch_ln_b")
MESH_SCHEDULE = {
    "ag8_inputs": [("gather", L0[:1])], "mod_fwd": [("gfill", L0[:1])],
    "cv_in_fwd": [("gather", L0[1:]), ("gather", L1[:1])], "conv_fwd": [("gfill", L0[1:]), ("gather", L1[1:])],
    "cv_out_fwd": [("gfill", L1)],
    "pl_in_fwd": [("gather", L2[:1])], "pool_fwd": [("gather", L2[1:])], "pl_out_fwd": [("gfill", L2)],
    "attn_fwd": [("gather", L3)], "ml_out_fwd": [("gfill", L3)],
    "ch_in_bwd": [("small", EARLY_SMALL)],
    "ml_out_bwd": [("swap", L3)], "attn_bwd": [("exch", L3)], "ml_queries_bwd": [("ofill", L3)],
    "pl_out_bwd": [("swap", L2)], "pl_in_bwd": [("exch", L2)],
    "cv_out_bwd": [("swap", L1), ("ofill", L2)], "conv_bwd": [("exch", L1)], "cv_in_bwd": [("ofill", L1)],
    "ag8_dmod": [("swap", L0)], "mod_bwd_w": [("exch", L0)], "ag8_small_grads": [("ofill", L0)],
}


class _MeshPlan:
    def __init__(self, weights, m, v, core):
        self.W, self.M, self.V, self.core = weights, m, v, core
        self.small = None
        self.stack, self.gstack, self.part, self.half, self.out = {}, {}, {}, {}, {}
        self.notes, self.early = {}, {}
        self.live, self.done = {}, set()

    def _make(self, op, names):
        if op == "gather":
            return _gather_ici([_shard2d(n, self.W[n]).astype(BF16) for n in names])
        if op == "gfill":
            return _sibling_fill([self.stack[n] for n in names], 1, True)
        if op == "swap":
            return _grad_swap_d2d([self.gstack[n] for n in names])
        if op == "exch":
            return _grad_exchange_ici([self.part[n] for n in names])
        if op == "ofill":
            return _sibling_fill([t for n in names for t in self.half[n]], 0, False)
        pack, self.early_spans = _pack_rows([self.notes[n] for n in names], LANES, 128)
        return _ag8_copies(pack)

    def _finish_op(self, op, names, hosted):
        self.done.add((op, names))
        res = hosted.results
        if op in ("gather", "gfill"):
            self.stack.update(zip(names, res))
        elif op == "swap":
            for n, r in zip(names, res):
                self.part[n] = _grad_add_half(self.core, self.gstack[n], r)
        elif op == "exch":
            for n, q in zip(names, res):
                rh = q.shape[1]
                self.half[n] = _adamw("adamw_" + n, self.core * rh, q, _shard2d(n, self.W[n]),
                                      _shard2d(n, self.M[n]), _shard2d(n, self.V[n]), rh)
        elif op == "ofill":
            for k, n in enumerate(names):
                self.out[n] = tuple(r.reshape(self.W[n].shape) for r in res[4 * k:4 * k + 4])
        else:
            self.early.update(zip(names, _unpack_rows(_sum8(res[0]), self.early_spans)))

    def alone(self, op, names):
        hosted = self._make(op, names)
        _run_hosted("%s_%s" % (op, names[0]), hosted)
        self._finish_op(op, names, hosted)

    def weights(self, names):
        wk = {n: _unstack(n, self.stack[n]) for n in names}
        if "pl_w_grp" in wk:
            wk["pl_w_grp"] = wk["pl_w_grp"].astype(F32)
        return _prep_weights(wk)

    def hosted(self, tag):
        self.live[tag] = [(op, names, self._make(op, names)) for op, names in MESH_SCHEDULE.get(tag, [])]
        return _merge_hosted([h for _, _, h in self.live[tag]])

    def after(self, tag):
        for op, names, hosted in self.live.pop(tag, []):
            self._finish_op(op, names, hosted)

    def note(self, values):
        self.notes.update(values)

    def layer_grads(self, layer, grads):
        g = _unprep_grads(grads)
        for n in LAYER_WEIGHTS[layer]:
            self.gstack[n] = _stack(n, g[n])

    def finish(self):
        for names in (L3, L2, L1, L0):
            for op in ("swap", "exch", "ofill"):
                if (op, names) not in self.done:
                    self.alone(op, names)
        return self.out


def kernel(x, c, ctx, c_ctx, norm_g, w_mod, b_mod, cv_w_in, cv_dw, cv_db, cv_ln_g, cv_ln_b, cv_w_out, pl_w_in, pl_w_grp, pl_scale, pl_w_out, ml_w_in, ml_q_norm, ml_kv_norm, ml_w_uq, ml_w_ukv, ml_nope_norm, ml_rope_norm, ml_w_out, ch_w_in, ch_ln_g, ch_ln_b, ch_w_s, ch_b_s, ch_w_out, loss_target, m_c_ctx, m_norm_g, m_w_mod, m_b_mod, m_cv_w_in, m_cv_dw, m_cv_db, m_cv_ln_g, m_cv_ln_b, m_cv_w_out, m_pl_w_in, m_pl_w_grp, m_pl_scale, m_pl_w_out, m_ml_w_in, m_ml_q_norm, m_ml_kv_norm, m_ml_w_uq, m_ml_w_ukv, m_ml_nope_norm, m_ml_rope_norm, m_ml_w_out, m_ch_w_in, m_ch_ln_g, m_ch_ln_b, m_ch_w_s, m_ch_b_s, m_ch_w_out, v_c_ctx, v_norm_g, v_w_mod, v_b_mod, v_cv_w_in, v_cv_dw, v_cv_db, v_cv_ln_g, v_cv_ln_b, v_cv_w_out, v_pl_w_in, v_pl_w_grp, v_pl_scale, v_pl_w_out, v_ml_w_in, v_ml_q_norm, v_ml_kv_norm, v_ml_w_uq, v_ml_w_ukv, v_ml_nope_norm, v_ml_rope_norm, v_ml_w_out, v_ch_w_in, v_ch_ln_g, v_ch_ln_b, v_ch_w_s, v_ch_b_s, v_ch_w_out):
    W = dict(c_ctx=c_ctx, norm_g=norm_g, w_mod=w_mod, b_mod=b_mod, cv_w_in=cv_w_in, cv_dw=cv_dw, cv_db=cv_db, cv_ln_g=cv_ln_g, cv_ln_b=cv_ln_b, cv_w_out=cv_w_out, pl_w_in=pl_w_in, pl_w_grp=pl_w_grp, pl_scale=pl_scale, pl_w_out=pl_w_out, ml_w_in=ml_w_in, ml_q_norm=ml_q_norm, ml_kv_norm=ml_kv_norm, ml_w_uq=ml_w_uq, ml_w_ukv=ml_w_ukv, ml_nope_norm=ml_nope_norm, ml_rope_norm=ml_rope_norm, ml_w_out=ml_w_out, ch_w_in=ch_w_in, ch_ln_g=ch_ln_g, ch_ln_b=ch_ln_b, ch_w_s=ch_w_s, ch_b_s=ch_b_s, ch_w_out=ch_w_out)
    M = dict(c_ctx=m_c_ctx, norm_g=m_norm_g, w_mod=m_w_mod, b_mod=m_b_mod, cv_w_in=m_cv_w_in, cv_dw=m_cv_dw, cv_db=m_cv_db, cv_ln_g=m_cv_ln_g, cv_ln_b=m_cv_ln_b, cv_w_out=m_cv_w_out, pl_w_in=m_pl_w_in, pl_w_grp=m_pl_w_grp, pl_scale=m_pl_scale, pl_w_out=m_pl_w_out, ml_w_in=m_ml_w_in, ml_q_norm=m_ml_q_norm, ml_kv_norm=m_ml_kv_norm, ml_w_uq=m_ml_w_uq, ml_w_ukv=m_ml_w_ukv, ml_nope_norm=m_ml_nope_norm, ml_rope_norm=m_ml_rope_norm, ml_w_out=m_ml_w_out, ch_w_in=m_ch_w_in, ch_ln_g=m_ch_ln_g, ch_ln_b=m_ch_ln_b, ch_w_s=m_ch_w_s, ch_b_s=m_ch_b_s, ch_w_out=m_ch_w_out)
    V = dict(c_ctx=v_c_ctx, norm_g=v_norm_g, w_mod=v_w_mod, b_mod=v_b_mod, cv_w_in=v_cv_w_in, cv_dw=v_cv_dw, cv_db=v_cv_db, cv_ln_g=v_cv_ln_g, cv_ln_b=v_cv_ln_b, cv_w_out=v_cv_w_out, pl_w_in=v_pl_w_in, pl_w_grp=v_pl_w_grp, pl_scale=v_pl_scale, pl_w_out=v_pl_w_out, ml_w_in=v_ml_w_in, ml_q_norm=v_ml_q_norm, ml_kv_norm=v_ml_kv_norm, ml_w_uq=v_ml_w_uq, ml_w_ukv=v_ml_w_ukv, ml_nope_norm=v_ml_nope_norm, ml_rope_norm=v_ml_rope_norm, ml_w_out=v_ml_w_out, ch_w_in=v_ch_w_in, ch_ln_g=v_ch_ln_g, ch_ln_b=v_ch_ln_b, ch_w_s=v_ch_w_s, ch_b_s=v_ch_b_s, ch_w_out=v_ch_w_out)

    batch, lat_len, d = x.shape
    mx, my, mc = _my_place()
    chip = 2 * mx + my
    dev = 2 * chip + mc
    core = jnp.reshape(mc, (1,)).astype(jnp.int32)
    zero_off = jnp.zeros((1,), jnp.int32)
    big_names = list(BIG)

    sw = d // N_CHIP
    small_in = [c] + [jnp.pad(W[n].reshape(-1, W[n].shape[-1]), ((0, 0), (0, sw - W[n].shape[-1])))
                      for n in SMALL_SHARDED]
    pack1, spans1 = _pack_rows(small_in, sw)
    plan = _MeshPlan(W, M, V, core)
    gather1 = _ag8_copies(pack1)
    _run_hosted("ag8_inputs", _merge_hosted([gather1, plan.hosted("ag8_inputs")]))
    plan.after("ag8_inputs")
    got1 = gather1.results[0]
    c_all = got1[:, spans1[0][0]:spans1[0][0] + spans1[0][1]].reshape(N_DEV * batch, d)
    full_small = {}
    for n, (r0, nr, _) in zip(SMALL_SHARDED, spans1[1:]):
        blk = got1[0::2, r0:r0 + nr, :W[n].shape[-1]]
        full_small[n] = blk.transpose(1, 0, 2).reshape(nr, -1)

    c_rows = jnp.concatenate([c_all, c_ctx[None], jnp.zeros((MOD_ROWS - CTX_ROW - 1, d), F32)], axis=0)
    nmod = w_mod.shape[2]
    b_shard = lax.dynamic_slice(b_mod, (0, chip * nmod), (b_mod.shape[0], nmod))[:, None, :]
    mod_shard = _mod_fwd(c_rows, w_mod, b_shard, hosted=plan.hosted("mod_fwd"))
    plan.after("mod_fwd")
    mod_rows = mod_shard.transpose(1, 0, 2).reshape(MOD_ROWS, 1, 4 * nmod)
    got2 = _run_hosted("mod_exchange", _chip_rows_copies(mod_rows, batch, CTX_ROW))[0]
    mod_mine = got2.reshape(N_CHIP, batch + 1, 4, nmod).transpose(2, 1, 0, 3).reshape(4, batch + 1, 3 * d)
    mod_lat, mod_ctx = mod_mine[:, :batch], mod_mine[:, batch]
    mods = []
    for i in range(4):
        mods.append(tuple(
            jnp.stack([mod_lat[i, :, j * d:(j + 1) * d], jnp.broadcast_to(mod_ctx[i, j * d:(j + 1) * d], (batch, d))],
                      axis=1)[:, :, None, :] for j in range(3)))

    wk = dict(full_small)
    wk.update(norm_g=norm_g, cv_db=cv_db, cv_ln_g=cv_ln_g, cv_ln_b=cv_ln_b, ml_nope_norm=ml_nope_norm[0],
              ml_rope_norm=ml_rope_norm[0], ch_w_s=ch_w_s[0], ch_b_s=ch_b_s[0])
    plan.small = _prep_weights(wk)
    xm = jnp.concatenate([x, ctx], axis=1)
    loss_part, grad_x, dmods, g = _local_step(xm, loss_target, mods, plan, lat_len)
    g = _unprep_grads(g)

    lat_rows, ctx_rows = [], []
    for i in range(4):
        dsh, dsc, dgt = dmods[i]
        lat_rows.append(jnp.concatenate([dsh[:, 0, 0], dsc[:, 0, 0], dgt[:, 0, 0]], axis=1))
        zero = jnp.zeros((d,), F32)
        cs = [jnp.sum(t[:, 1, 0], axis=0) if ok else zero
              for t, ok in zip((dsh, dsc, dgt), (i <= 2, i <= 2, i <= 1))]
        ctx_rows.append(jnp.concatenate(cs, axis=0)[None])
    dmod_dev = jnp.concatenate(lat_rows + ctx_rows, axis=0)
    dmod_dev = jnp.pad(dmod_dev, ((0, (-dmod_dev.shape[0]) % 8), (0, 0)))
    gather3 = _ag8_column_copies(dmod_dev, nmod)
    _run_hosted("ag8_dmod", _merge_hosted([gather3, plan.hosted("ag8_dmod")]))
    plan.after("ag8_dmod")
    got3 = gather3.results[0]
    dlat = got3[:, :4 * batch].reshape(N_DEV, 4, batch, nmod).transpose(1, 0, 2, 3).reshape(4, N_DEV * batch, nmod)
    dctx_parts = got3[:, 4 * batch:4 * batch + 4].transpose(1, 0, 2)
    g_b_shard, dctx = _mod_bwd_rows(dlat, dctx_parts)
    d_rows = jnp.concatenate([dlat, dctx, jnp.zeros((4, MOD_ROWS - CTX_ROW - 1, nmod), F32)], axis=1)
    g_w_mod, dcc_part = _mod_bwd_w(c_rows.T, d_rows, w_mod, hosted=plan.hosted("mod_bwd_w"))
    plan.after("mod_bwd_w")

    wm2 = w_mod.reshape(-1, nmod)
    res_mod = _adamw("adamw_w_mod", None, g_w_mod.reshape(1, -1, nmod), wm2, M["w_mod"].reshape(-1, nmod),
                     V["w_mod"].reshape(-1, nmod), wm2.shape[0], hosted=plan.hosted("adamw_w_mod"))
    plan.after("adamw_w_mod")
    out = {"w_mod": tuple(r.reshape(w_mod.shape) for r in res_mod)}

    g_small_in = {n: g[n] for n in SMALL_SHARDED if n not in EARLY_SMALL}
    g_small_in.update(norm_g=g["norm_g"], cv_db=g["cv_db"], cv_ln_g=g["cv_ln_g"], cv_ln_b=g["cv_ln_b"],
                      ml_nope_norm=g["ml_nope_norm"], ml_rope_norm=g["ml_rope_norm"],
                      c_ctx=dcc_part.reshape(-1) * (mc == 0).astype(F32), loss=loss_part,
                      b_mod=lax.dynamic_update_slice(jnp.zeros((N_CHIP, 4, nmod), F32),
                                                     g_b_shard[None, :, 0] * (mc == 0).astype(F32), (chip, 0, 0)))
    small_names = list(g_small_in)
    pack4, spans4 = _pack_rows([g_small_in[n] for n in small_names], LANES, 128)
    gather4 = _ag8_copies(pack4)
    _run_hosted("ag8_small_grads", _merge_hosted([gather4, plan.hosted("ag8_small_grads")]))
    plan.after("ag8_small_grads")
    gs = dict(zip(small_names, _unpack_rows(_sum8(gather4.results[0]), spans4)))
    loss = gs["loss"][0, 0]
    gs.update(plan.early)
    gs["b_mod"] = gs["b_mod"].transpose(1, 0, 2).reshape(4, N_CHIP * nmod)
    for n in SMALL_SHARDED:
        wd = W[n].shape[-1]
        gs[n] = lax.dynamic_slice_in_dim(gs[n], chip * wd, wd, axis=1)
    upd_names = SMALL_REPLICATED + SMALL_SHARDED
    pw, spans_u = _pack_rows([W[n] for n in upd_names], LANES, 128)
    pm, _ = _pack_rows([M[n] for n in upd_names], LANES, 128)
    pv, _ = _pack_rows([V[n] for n in upd_names], LANES, 128)
    pg, _ = _pack_rows([gs[n].reshape(W[n].shape) for n in upd_names], LANES, 128)
    res_small = _adamw("adamw_small", None, pg[None], pw, pm, pv, pw.shape[0], hosted=plan.hosted("adamw_small"))
    plan.after("adamw_small")
    for n, vals in zip(upd_names, zip(*[_unpack_rows(r, spans_u) for r in res_small])):
        out[n] = vals
    out.update(plan.finish())

    outs = [loss, grad_x]
    for j in range(4):
        outs.extend(out[n][j] for n in WEIGHTS)
    return tuple(outs)
```

```python
import functools
import math

import jax
import jax.numpy as jnp
from jax import lax
from jax.experimental import pallas as pl
from jax.experimental.pallas import tpu as pltpu

F32 = jnp.float32
BF16 = jnp.bfloat16
ACT = jnp.float32
MESH = pl.DeviceIdType.MESH

EPS = 1e-6
GRID_W = 64
CONV_WIDTH = 31
CONV_HALF = CONV_WIDTH // 2
CONV_PAD = 16
POOL_WINDOWS = (2, 4, 8, 16)
POOL_HALF = max(POOL_WINDOWS) // 2
HEADS = 8
NOPE = 128
ROPE = 64
HEAD_W = 256
VDIM = 128
KV_RANK = 256
Q_RANK = 384
ATT_SCALE = (NOPE + ROPE) ** -0.5
LN2 = math.log(2.0)
Q_PRESCALE = ATT_SCALE / LN2
ROPE_THETA = 10000.0
CHUNK = 128
CHUNK_GROUPS = 8
LANES = 128
TM = 256
TM_LATENT = 512
TQ = 1024
ATT_RQ = 128
ATT_KC = 256
VMEM_LIMIT = 56 * 1024 * 1024

ADAM_LR = 0.001
ADAM_B1 = 0.9
ADAM_B2 = 0.999
ADAM_EPS = 1e-08
ADAM_WD = 0.01
ADAM_STEP = 10


def _dot(a, b):
    return jnp.dot(a.astype(BF16), b.astype(BF16), preferred_element_type=F32)


def _dot_nt(a, b):
    return lax.dot_general(a.astype(BF16), b.astype(BF16), (((1,), (1,)), ((), ())), preferred_element_type=F32)


def _dot_tn(a, b):
    return lax.dot_general(a.astype(BF16), b.astype(BF16), (((0,), (0,)), ((), ())), preferred_element_type=F32)


@jax.custom_vjp
def _mm(a, w):
    return _dot(a, w)


def _mm_fwd(a, w):
    return _dot(a, w), (a, w)


def _mm_bwd(res, ct):
    a, w = res
    return _dot_nt(ct, w), _dot_tn(a, ct)


_mm.defvjp(_mm_fwd, _mm_bwd)


def _swap16_impl(x):
    n = x.shape[-1]
    ax = x.ndim - 1
    lane = lax.broadcasted_iota(jnp.int32, x.shape, ax)
    up = pltpu.roll(x, n - 16, ax)
    dn = pltpu.roll(x, 16, ax)
    return jnp.where((lane % 32) < 16, up, dn)


@jax.custom_vjp
def _swap16(x):
    return _swap16_impl(x)


_swap16.defvjp(lambda x: (_swap16_impl(x), None), lambda _, ct: (_swap16_impl(ct),))


def _rms(x, g, n=None):
    n = x.shape[-1] if n is None else n
    return x * lax.rsqrt(jnp.sum(x * x, axis=-1, keepdims=True) * (1.0 / n) + EPS) * g


def _layernorm(x, g, b):
    mu = jnp.mean(x, axis=-1, keepdims=True)
    xc = x - mu
    var = jnp.mean(xc * xc, axis=-1, keepdims=True)
    return xc * lax.rsqrt(var + EPS) * g + b


def _silu(x):
    return x * jax.nn.sigmoid(x)


def _rope(x, cos, sin):
    return x * cos + _swap16(x) * sin


ANY = pl.BlockSpec(memory_space=pl.ANY)


class _Hosted:
    def __init__(self, arrays, out_shapes, sems, start, wait, aliases=None):
        self.arrays, self.out_shapes, self.sems = list(arrays), list(out_shapes), list(sems)
        self.start, self.wait, self.aliases = start, wait, dict(aliases or {})
        self.results = None


def _merge_hosted(parts):
    parts = [p for p in parts if p is not None]
    if not parts:
        return None
    if len(parts) == 1:
        return parts[0]
    offs, a0, o0, s0 = [], 0, 0, 0
    for p in parts:
        offs.append((a0, o0, s0))
        a0, o0, s0 = a0 + len(p.arrays), o0 + len(p.out_shapes), s0 + len(p.sems)

    def run(which):
        def f(ins, outs, sems):
            for p, (a, o, s) in zip(parts, offs):
                getattr(p, which)(ins[a:a + len(p.arrays)], outs[o:o + len(p.out_shapes)], sems[s:s + len(p.sems)])
        return f

    aliases = {}
    for p, (a, o, _) in zip(parts, offs):
        aliases.update({a + i: o + j for i, j in p.aliases.items()})
    merged = _Hosted(sum((p.arrays for p in parts), []), sum((p.out_shapes for p in parts), []),
                     sum((p.sems for p in parts), []), run("start"), run("wait"), aliases)
    merged.parts, merged.offs = parts, offs
    return merged


def _deliver(hosted, results):
    hosted.results = list(results)
    for p, (_, o, _) in zip(getattr(hosted, "parts", []), getattr(hosted, "offs", [])):
        p.results = list(results[o:o + len(p.out_shapes)])


def _pcall(body, *, name, grid, in_specs, out_specs, out_shape, args, hosted=None, vmem_limit=True, scratch=()):
    n_in, n_out, n_scr = len(args), len(out_shape), len(scratch)
    kwargs = dict(scratch_shapes=list(scratch)) if scratch else {}
    if hosted is not None:
        nhi, nho, inner = len(hosted.arrays), len(hosted.out_shapes), body

        def body(*refs):
            ins, hin = refs[:n_in], refs[n_in:n_in + nhi]
            outs, hout = refs[n_in + nhi:n_in + nhi + n_out], refs[n_in + nhi + n_out:n_in + nhi + n_out + nho]
            own = refs[n_in + nhi + n_out + nho:n_in + nhi + n_out + nho + n_scr]
            sems = refs[n_in + nhi + n_out + nho + n_scr:]
            first, last = None, None
            for k, g in enumerate(grid):
                f, l = pl.program_id(k) == 0, pl.program_id(k) == g - 1
                first = f if first is None else jnp.logical_and(first, f)
                last = l if last is None else jnp.logical_and(last, l)

            @pl.when(first)
            def _():
                hosted.start(hin, hout, sems)

            inner(*ins, *outs, *own)

            @pl.when(last)
            def _():
                hosted.wait(hin, hout, sems)

        in_specs = list(in_specs) + [ANY] * nhi
        out_specs = list(out_specs) + [ANY] * nho
        out_shape = list(out_shape) + hosted.out_shapes
        args = list(args) + hosted.arrays
        kwargs = dict(scratch_shapes=list(scratch) + hosted.sems,
                      input_output_aliases={n_in + i: n_out + j for i, j in hosted.aliases.items()})
    params = dict(dimension_semantics=("arbitrary",) * len(grid))
    if vmem_limit:
        params["vmem_limit_bytes"] = VMEM_LIMIT
    res = pl.pallas_call(body, name=name, grid=grid, in_specs=list(in_specs), out_specs=list(out_specs),
                         out_shape=list(out_shape), compiler_params=pltpu.CompilerParams(**params), **kwargs)(*args)
    if hosted is not None:
        _deliver(hosted, res[n_out:])
    return list(res[:n_out])


def _run_hosted(name, hosted):
    nhi, nho = len(hosted.arrays), len(hosted.out_shapes)

    def body(*refs):
        ins, outs, sems = refs[:nhi], refs[nhi:nhi + nho], refs[nhi + nho:]
        hosted.start(ins, outs, sems)
        hosted.wait(ins, outs, sems)

    res = pl.pallas_call(body, name=name, in_specs=[ANY] * nhi, out_specs=[ANY] * nho, out_shape=hosted.out_shapes,
                         scratch_shapes=hosted.sems, input_output_aliases=hosted.aliases)(*hosted.arrays)
    _deliver(hosted, res)
    return list(res)


def _const_spec(shape, single=False):
    nd = len(shape)
    if single:
        return pl.BlockSpec(shape, lambda b, i: (0,) * nd, pipeline_mode=pl.Buffered(1))
    return pl.BlockSpec(shape, lambda b, i: (0,) * nd)


def _tile_spec(arr, n_lat_tiles, lat_only=False, tm=TM):
    bt, _, cw = arr.shape
    if lat_only:
        return pl.BlockSpec((1, tm, cw), lambda b, i: (b if bt > 1 else 0, jnp.minimum(i, n_lat_tiles - 1), 0))
    return pl.BlockSpec((1, tm, cw), lambda b, i: (b if bt > 1 else 0, i, 0))


def _eparam_spec(arr, n_lat_tiles):
    cw = arr.shape[-1]
    return pl.BlockSpec((1, 1, 1, cw), lambda b, i: (b, (i >= n_lat_tiles).astype(jnp.int32), 0, 0))


def _stage_fwd(name, *, pre, post, wsel, splits, tiles, eparams, sparams, weights, out_widths, out_dtypes,
               batch, n_tiles, n_lat_tiles, hosted=None, tm=TM):
    nt, ne, ns, nw = len(tiles), len(eparams), len(sparams), len(weights)

    def body(*refs):
        t_refs = refs[:nt]
        e_refs = refs[nt:nt + ne]
        s_refs = refs[nt + ne:nt + ne + ns]
        w_refs = refs[nt + ne + ns:nt + ne + ns + nw]
        o_refs = refs[nt + ne + ns + nw:]
        tv = [r[0].astype(F32) for r in t_refs]
        ev = [r[0, 0] for r in e_refs]
        sv = [r[...] for r in s_refs]
        a = pre(tv, ev, sv)
        z = [_dot(a[wsel[j]], w_refs[j][...]) for j in range(nw)]
        if post is None:
            outs = [z[j][:, s:s + w] for (j, s, w) in splits]
        else:
            outs = post(z, tv, ev, sv)
        for o_ref, o in zip(o_refs, outs):
            o_ref[0] = o.astype(o_ref.dtype)

    in_specs = ([_tile_spec(t, n_lat_tiles, tm=tm) for t in tiles] + [_eparam_spec(e, n_lat_tiles) for e in eparams]
                + [_const_spec(s.shape) for s in sparams] + [_const_spec(w.shape, single=True) for w in weights])
    out_shape = [jax.ShapeDtypeStruct((batch, n_tiles * tm, w), dt) for w, dt in zip(out_widths, out_dtypes)]
    out_specs = [pl.BlockSpec((1, tm, w), lambda b, i: (b, i, 0)) for w in out_widths]
    return _pcall(body, name=name, grid=(batch, n_tiles), in_specs=in_specs, out_specs=out_specs,
                  out_shape=out_shape, args=[*tiles, *eparams, *sparams, *weights], hosted=hosted)


def _stage_bwd(name, *, pre, post, wsel, splits, tiles, tile_diff, eparams, sparams, weights, cots, cot_lat_only,
               batch, n_tiles, n_lat_tiles, add=None, add_lat_only=False, hosted=None, w_col_stack=None,
               dt_lat_only=False, tm=TM):
    nt, ne, ns, nw, nc = len(tiles), len(eparams), len(sparams), len(weights), len(cots)
    diff_idx = [k for k in range(nt) if tile_diff[k]]
    nd = len(diff_idx)
    has_add = add is not None
    w_col_stack = w_col_stack or [None] * nw

    def body(*refs):
        pos = 0
        t_refs = refs[pos:pos + nt]; pos += nt
        e_refs = refs[pos:pos + ne]; pos += ne
        s_refs = refs[pos:pos + ns]; pos += ns
        w_refs = refs[pos:pos + nw]; pos += nw
        c_refs = refs[pos:pos + nc]; pos += nc
        if has_add:
            add_ref = refs[pos]; pos += 1
        dt_refs = refs[pos:pos + nd]; pos += nd
        de_refs = refs[pos:pos + ne]; pos += ne
        ds_refs = refs[pos:pos + ns]; pos += ns
        dw_refs = refs[pos:pos + nw]; pos += nw

        b = pl.program_id(0)
        i = pl.program_id(1)
        is_lat = i < n_lat_tiles
        tv = [r[0].astype(F32) for r in t_refs]
        ev = tuple(r[0, 0] for r in e_refs)
        sv = tuple(r[...] for r in s_refs)
        dv0 = tuple(tv[k] for k in diff_idx)

        def merge(dv):
            full = list(tv)
            for k, v in zip(diff_idx, dv):
                full[k] = v
            return full

        def pre_f(dv, ev_, sv_):
            return tuple(pre(merge(dv), list(ev_), list(sv_)))

        a, vjp_pre = jax.vjp(pre_f, dv0, ev, sv)
        cv = []
        for c_ref, lat in zip(c_refs, cot_lat_only):
            c = c_ref[0].astype(F32)
            cv.append(jnp.where(is_lat, c, 0.0) if lat else c)
        if post is None:
            dz = []
            for j in range(nw):
                parts = [cv[k] for k, (jj, _, _) in enumerate(splits) if jj == j]
                dz.append(parts[0] if len(parts) == 1 else jnp.concatenate(parts, axis=1))
            dt2 = de2 = ds2 = None
        else:
            z = tuple(_dot(a[wsel[j]], w_refs[j][...]) for j in range(nw))

            def post_f(z_, dv, ev_, sv_):
                return tuple(post(list(z_), merge(dv), list(ev_), list(sv_)))

            _, vjp_post = jax.vjp(post_f, z, dv0, ev, sv)
            dz, dt2, de2, ds2 = vjp_post(tuple(cv))
        da = [None] * len(a)
        dws = []
        for j in range(nw):
            g = _dot_nt(dz[j], w_refs[j][...])
            da[wsel[j]] = g if da[wsel[j]] is None else da[wsel[j]] + g
            dws.append(_dot_tn(a[wsel[j]], dz[j]))
        da = tuple(jnp.zeros_like(a[k]) if da[k] is None else da[k] for k in range(len(a)))
        dt1, de1, ds1 = vjp_pre(da)

        def plus(u, v):
            return u if v is None else u + v

        for k in range(nd):
            val = plus(dt1[k], None if dt2 is None else dt2[k])
            if has_add and k == 0:
                addv = add_ref[0].astype(F32)
                val = val + (jnp.where(is_lat, addv, 0.0) if add_lat_only else addv)
            if dt_lat_only:
                @pl.when(is_lat)
                def _(k=k, val=val):
                    dt_refs[k][0] = val.astype(dt_refs[k].dtype)
            else:
                dt_refs[k][0] = val.astype(dt_refs[k].dtype)

        seg_first = jnp.logical_or(i == 0, i == n_lat_tiles)
        for k in range(ne):
            val = plus(de1[k], None if de2 is None else de2[k])

            @pl.when(seg_first)
            def _(k=k, val=val):
                de_refs[k][0, 0] = val

            @pl.when(jnp.logical_not(seg_first))
            def _(k=k, val=val):
                de_refs[k][0, 0] += val

        first = jnp.logical_and(b == 0, i == 0)
        acc = [(ds_refs[k], plus(ds1[k], None if ds2 is None else ds2[k])) for k in range(ns)]
        for j in range(nw):
            if w_col_stack[j]:
                cw = dws[j].shape[1] // w_col_stack[j]
                acc += [(dw_refs[j].at[c], dws[j][:, c * cw:(c + 1) * cw]) for c in range(w_col_stack[j])]
            else:
                acc.append((dw_refs[j], dws[j]))
        for ref, val in acc:
            @pl.when(first)
            def _(ref=ref, val=val):
                ref[...] = val

            @pl.when(jnp.logical_not(first))
            def _(ref=ref, val=val):
                ref[...] += val

    in_specs = ([_tile_spec(t, n_lat_tiles, tm=tm) for t in tiles] + [_eparam_spec(e, n_lat_tiles) for e in eparams]
                + [_const_spec(s.shape) for s in sparams] + [_const_spec(w.shape, single=True) for w in weights]
                + [_tile_spec(c, n_lat_tiles, lat, tm) for c, lat in zip(cots, cot_lat_only)])
    args = [*tiles, *eparams, *sparams, *weights, *cots]
    if has_add:
        in_specs.append(_tile_spec(add, n_lat_tiles, add_lat_only, tm))
        args.append(add)
    dt_tiles = n_lat_tiles if dt_lat_only else n_tiles
    out_shape = [jax.ShapeDtypeStruct((batch, dt_tiles * tm, tiles[k].shape[-1]), F32) for k in diff_idx]
    out_specs = [pl.BlockSpec((1, tm, tiles[k].shape[-1]), lambda b, i: (b, jnp.minimum(i, dt_tiles - 1), 0))
                 for k in diff_idx]
    out_shape += [jax.ShapeDtypeStruct(e.shape, F32) for e in eparams]
    out_specs += [_eparam_spec(e, n_lat_tiles) for e in eparams]
    out_shape += [jax.ShapeDtypeStruct(s.shape, F32) for s in sparams]
    out_specs += [_const_spec(s.shape) for s in sparams]
    dw_shapes = [(n, w.shape[0], w.shape[1] // n) if n else w.shape for w, n in zip(weights, w_col_stack)]
    out_shape += [jax.ShapeDtypeStruct(s, F32) for s in dw_shapes]
    out_specs += [_const_spec(s, single=True) for s in dw_shapes]
    res = _pcall(body, name=name, grid=(batch, n_tiles), in_specs=in_specs, out_specs=out_specs,
                 out_shape=out_shape, args=args, hosted=hosted)
    return res[:nd], res[nd:nd + ne], res[nd + ne:nd + ne + ns], res[nd + ne + ns:]


def _pre_adaln(tv, ev, sv):
    x = tv[0]
    sh, sc = ev[0], ev[1]
    return [_rms(x, sv[0]) * (1.0 + sc) + sh]


def _post_residual(x_index):
    def post(z, tv, ev, sv):
        return [tv[x_index] + ev[-1] * z[0]]
    return post


def _pre_conv_out(tv, ev, sv):
    c1, gg = tv[0], tv[1]
    return [_silu(_layernorm(c1, sv[0], sv[1])) * _silu(gg)]


def _pre_pool_out(tv, ev, sv):
    pooled, gg = tv[0], tv[1]
    w_grp, scale = sv[0], sv[1]
    gw = w_grp.shape[-1]
    y = jnp.concatenate([_mm(pooled[:, k * gw:(k + 1) * gw], w_grp[k]) for k in range(w_grp.shape[0])], axis=1)
    return [y * scale * _silu(gg)]


def _pre_rms_only(tv, ev, sv):
    return [_rms(tv[0], sv[0])]


def _post_mla_keys(z, tv, ev, sv):
    krp, cos, sin = tv[1], tv[2], tv[3]
    nope_g, rope_g = sv[1], sv[2]
    kv = z[0]
    kr = _rope(_rms(krp, rope_g, ROPE), cos, sin)
    ks, vs = [], []
    for h in range(HEADS):
        ks.append(_rms(kv[:, h * 2 * NOPE:h * 2 * NOPE + NOPE], nope_g))
        ks.append(kr)
        vs.append(kv[:, h * 2 * NOPE + NOPE:(h + 1) * 2 * NOPE])
    return [jnp.concatenate(ks, axis=1), jnp.concatenate(vs, axis=1)]


def _post_mla_queries(z, tv, ev, sv):
    cos, sin = tv[1], tv[2]
    nope_g, rope_g = sv[1], sv[2]
    q = z[0]
    qs = []
    for h in range(HEADS):
        qs.append(_rms(q[:, h * HEAD_W:h * HEAD_W + NOPE], nope_g))
        qs.append(_rope(_rms(q[:, h * HEAD_W + NOPE:(h + 1) * HEAD_W], rope_g, ROPE), cos, sin))
    return [jnp.concatenate(qs, axis=1) * Q_PRESCALE]


def _pre_mla_out(tv, ev, sv):
    return [tv[0] * _silu(tv[1])]


def _pre_chunk_out(tv, ev, sv):
    u, v, gg = tv[0], tv[1], tv[2]
    ln_g, ln_b, w_s, b_s = sv
    vn = _layernorm(v, ln_g, ln_b)
    rows = []
    for n in range(vn.shape[0] // CHUNK):
        blk = vn[n * CHUNK:(n + 1) * CHUNK]
        cols = [_mm(w_s[g], blk[:, g * LANES:(g + 1) * LANES]) + b_s[:, g:g + 1] for g in range(CHUNK_GROUPS)]
        rows.append(jnp.concatenate(cols, axis=1))
    s = jnp.concatenate(rows, axis=0)
    return [u * s * _silu(gg)]


def _segments(lat_len, tot_len):
    segs = [(0, lat_len)]
    if tot_len > lat_len:
        segs.append((lat_len, tot_len - lat_len))
    return segs


def _pad_rows(x):
    z = jnp.zeros((CONV_PAD, x.shape[1]), x.dtype)
    return jnp.concatenate([z, x, z], axis=0)


def _shifted(xp, j):
    n = xp.shape[0] - 2 * CONV_PAD
    if j != 0:
        xp = pltpu.roll(xp, (-j) % xp.shape[0], 0)
    return xp[CONV_PAD:CONV_PAD + n]


def _conv_fwd(a, bgate, dw, db, lat_len, hosted=None):
    batch, tot, e = a.shape
    segs = _segments(lat_len, tot)

    def body(a_ref, b_ref, dw_ref, db_ref, o_ref):
        w = dw_ref[...]
        for (s0, n) in segs:
            y = a_ref[0, s0:s0 + n, :].astype(F32) * jax.nn.sigmoid(b_ref[0, s0:s0 + n, :].astype(F32))
            yp = _pad_rows(y)
            acc = jnp.zeros_like(y) + db_ref[...]
            for k in range(CONV_WIDTH):
                acc = acc + _shifted(yp, k - CONV_HALF) * w[k:k + 1, :]
            o_ref[0, s0:s0 + n, :] = acc.astype(o_ref.dtype)

    blk = pl.BlockSpec((1, tot, LANES), lambda b, cb: (b, 0, cb))
    return _pcall(
        body, name="conv_fwd", grid=(batch, e // LANES),
        in_specs=[blk, blk, pl.BlockSpec((CONV_WIDTH, LANES), lambda b, cb: (0, cb)),
                  pl.BlockSpec((1, LANES), lambda b, cb: (0, cb))],
        out_specs=[blk], out_shape=[jax.ShapeDtypeStruct(a.shape, ACT)], args=[a, bgate, dw, db], hosted=hosted)[0]


def _conv_bwd(a, bgate, dw, dc1, lat_len, hosted=None):
    batch, tot, e = a.shape
    segs = _segments(lat_len, tot)

    def body(a_ref, b_ref, dw_ref, dc_ref, da_ref, dg_ref, ddw_ref, ddb_ref):
        b = pl.program_id(1)
        w = dw_ref[...]
        ddw_rows = [None] * CONV_WIDTH
        ddb = None
        for (s0, n) in segs:
            av = a_ref[0, s0:s0 + n, :].astype(F32)
            sg = jax.nn.sigmoid(b_ref[0, s0:s0 + n, :].astype(F32))
            y = av * sg
            dc = dc_ref[0, s0:s0 + n, :]
            yp, dcp = _pad_rows(y), _pad_rows(dc)
            dy = jnp.zeros_like(y)
            for k in range(CONV_WIDTH):
                j = k - CONV_HALF
                dy = dy + _shifted(dcp, -j) * w[k:k + 1, :]
                r = jnp.sum(dc * _shifted(yp, j), axis=0, keepdims=True)
                ddw_rows[k] = r if ddw_rows[k] is None else ddw_rows[k] + r
            r = jnp.sum(dc, axis=0, keepdims=True)
            ddb = r if ddb is None else ddb + r
            da_ref[0, s0:s0 + n, :] = dy * sg
            dg_ref[0, s0:s0 + n, :] = dy * av * sg * (1.0 - sg)

        @pl.when(b == 0)
        def _():
            ddw_ref[...] = jnp.zeros_like(ddw_ref)
            ddb_ref[...] = jnp.zeros_like(ddb_ref)

        for k in range(CONV_WIDTH):
            ddw_ref[k:k + 1, :] += ddw_rows[k]
        ddb_ref[...] += ddb

    blk = pl.BlockSpec((1, tot, LANES), lambda cb, b: (b, 0, cb))
    wspec = pl.BlockSpec((CONV_WIDTH, LANES), lambda cb, b: (0, cb))
    bspec = pl.BlockSpec((1, LANES), lambda cb, b: (0, cb))
    return _pcall(
        body, name="conv_bwd", grid=(e // LANES, batch),
        in_specs=[blk, blk, wspec, blk],
        out_specs=[blk, blk, wspec, bspec],
        out_shape=[jax.ShapeDtypeStruct(a.shape, F32), jax.ShapeDtypeStruct(a.shape, F32),
                   jax.ShapeDtypeStruct((CONV_WIDTH, e), F32), jax.ShapeDtypeStruct((1, e), F32)],
        args=[a, bgate, dw, dc1], hosted=hosted)


def _pool_taps(group):
    half = lax.shift_left(jnp.int32(1), group)
    taps = []
    for j in range(-POOL_HALF, POOL_HALF):
        inside = jnp.logical_and(j >= -half, j < half)
        taps.append(jnp.where(inside, 1.0, 0.0).astype(F32))
    return taps, half


def _pool_counts(n, half, shape):
    t = lax.broadcasted_iota(jnp.int32, shape, 0)
    cnt = jnp.minimum(t + half, n) - jnp.maximum(t - half, 0)
    return cnt.astype(F32)


def _pool_fwd(v, lat_len, hosted=None):
    batch, tot, e = v.shape
    gw = e // len(POOL_WINDOWS)
    segs = _segments(lat_len, tot)

    def body(v_ref, o_ref):
        taps, half = _pool_taps(pl.program_id(1))
        for (s0, n) in segs:
            x = v_ref[0, s0:s0 + n, :]
            xp = _pad_rows(x)
            acc = jnp.zeros_like(x)
            for idx, j in enumerate(range(-POOL_HALF, POOL_HALF)):
                acc = acc + _shifted(xp, j) * taps[idx]
            o_ref[0, s0:s0 + n, :] = (acc / _pool_counts(n, half, x.shape) - x).astype(o_ref.dtype)

    blk = pl.BlockSpec((1, tot, gw), lambda b, g: (b, 0, g))
    return _pcall(body, name="pool_fwd", grid=(batch, len(POOL_WINDOWS)), in_specs=[blk], out_specs=[blk],
                  out_shape=[jax.ShapeDtypeStruct(v.shape, ACT)], args=[v], hosted=hosted)[0]


def _pool_bwd(dp, lat_len):
    batch, tot, e = dp.shape
    gw = e // len(POOL_WINDOWS)
    segs = _segments(lat_len, tot)

    def body(d_ref, o_ref):
        taps, half = _pool_taps(pl.program_id(1))
        for (s0, n) in segs:
            d = d_ref[0, s0:s0 + n, :]
            dnp = _pad_rows(d / _pool_counts(n, half, d.shape))
            acc = jnp.zeros_like(d)
            for idx, j in enumerate(range(-POOL_HALF, POOL_HALF)):
                acc = acc + _shifted(dnp, -j) * taps[idx]
            o_ref[0, s0:s0 + n, :] = acc - d

    blk = pl.BlockSpec((1, tot, gw), lambda b, g: (b, 0, g))
    return pl.pallas_call(
        body, name="pool_bwd", grid=(batch, len(POOL_WINDOWS)), in_specs=[blk], out_specs=blk,
        out_shape=jax.ShapeDtypeStruct(dp.shape, F32),
        compiler_params=pltpu.CompilerParams(dimension_semantics=("arbitrary", "arbitrary"),
                                             vmem_limit_bytes=VMEM_LIMIT),
    )(dp)


def _attn_fwd(q, k, v, hosted=None):
    batch, lq, _ = q.shape
    tk = k.shape[1]
    tq = min(TQ, lq)

    def body(q_ref, k_ref, v_ref, o_ref, lse_ref):
        s2 = _dot_nt(q_ref[0], k_ref[0])
        m2 = jnp.max(s2, axis=-1, keepdims=True)
        e = jnp.exp2(s2 - m2)
        l = jnp.sum(e, axis=-1, keepdims=True)
        o_ref[0] = (_dot(e, v_ref[0]) / l).astype(o_ref.dtype)
        lse_ref[0, 0] = m2 + jnp.log2(l)

    return _pcall(
        body, name="attn_fwd", grid=(batch, HEADS, lq // tq),
        in_specs=[pl.BlockSpec((1, tq, HEAD_W), lambda b, h, i: (b, i, h)),
                  pl.BlockSpec((1, tk, HEAD_W), lambda b, h, i: (b, 0, h)),
                  pl.BlockSpec((1, tk, VDIM), lambda b, h, i: (b, 0, h))],
        out_specs=[pl.BlockSpec((1, tq, VDIM), lambda b, h, i: (b, i, h)),
                   pl.BlockSpec((1, 1, tq, 1), lambda b, h, i: (b, h, i, 0))],
        out_shape=[jax.ShapeDtypeStruct((batch, lq, HEADS * VDIM), ACT),
                   jax.ShapeDtypeStruct((batch, HEADS, lq, 1), F32)], args=[q, k, v], hosted=hosted)


def _attn_bwd(q, k, v, o, lse, do, hosted=None):
    batch, lq, _ = q.shape
    tk = k.shape[1]
    tq = min(TQ, lq)

    def body(q_ref, k_ref, v_ref, o_ref, lse_ref, do_ref, dq_ref, dk_ref, dv_ref, p_scr, ds_scr):
        i = pl.program_id(2)
        nr = tq // ATT_RQ
        rows = [slice(r * ATT_RQ, (r + 1) * ATT_RQ) for r in range(nr)]
        qv = [q_ref[0, rw, :] for rw in rows]
        dob = [do_ref[0, rw, :].astype(BF16) for rw in rows]
        row_lse = [lse_ref[0, 0, rw, :] for rw in rows]
        delta = [jnp.sum(do_ref[0, rw, :] * o_ref[0, rw, :], axis=-1, keepdims=True) for rw in rows]
        for c in range(tk // ATT_KC):
            keys = slice(c * ATT_KC, (c + 1) * ATT_KC)
            kc, vc = k_ref[0, keys, :], v_ref[0, keys, :]
            for r in range(nr):
                p = jnp.exp2(_dot_nt(qv[r], kc) - row_lse[r])
                dp = _dot_nt(dob[r], vc)
                p_scr[rows[r], keys] = p.astype(BF16)
                ds_scr[rows[r], keys] = (p * (dp - delta[r]) * LN2).astype(BF16)
        dq_ref[0] = _dot(ds_scr[...], k_ref[0])
        dk = _dot_tn(ds_scr[...], q_ref[0])
        dv = _dot_tn(p_scr[...], do_ref[0])

        @pl.when(i == 0)
        def _():
            dk_ref[0] = dk
            dv_ref[0] = dv

        @pl.when(i != 0)
        def _():
            dk_ref[0] += dk
            dv_ref[0] += dv

    return _pcall(
        body, name="attn_bwd", grid=(batch, HEADS, lq // tq),
        in_specs=[pl.BlockSpec((1, tq, HEAD_W), lambda b, h, i: (b, i, h)),
                  pl.BlockSpec((1, tk, HEAD_W), lambda b, h, i: (b, 0, h)),
                  pl.BlockSpec((1, tk, VDIM), lambda b, h, i: (b, 0, h)),
                  pl.BlockSpec((1, tq, VDIM), lambda b, h, i: (b, i, h)),
                  pl.BlockSpec((1, 1, tq, 1), lambda b, h, i: (b, h, i, 0)),
                  pl.BlockSpec((1, tq, VDIM), lambda b, h, i: (b, i, h))],
        out_specs=[pl.BlockSpec((1, tq, HEAD_W), lambda b, h, i: (b, i, h)),
                   pl.BlockSpec((1, tk, HEAD_W), lambda b, h, i: (b, 0, h)),
                   pl.BlockSpec((1, tk, VDIM), lambda b, h, i: (b, 0, h))],
        out_shape=[jax.ShapeDtypeStruct(q.shape, F32), jax.ShapeDtypeStruct(k.shape, F32),
                   jax.ShapeDtypeStruct(v.shape, F32)],
        args=[q, k, v, o, lse, do], hosted=hosted,
        scratch=[pltpu.VMEM((tq, tk), BF16), pltpu.VMEM((tq, tk), BF16)])


def _loss_kernel(y, target):
    batch, lq, d = y.shape

    def body(y_ref, t_ref, l_ref, dy_ref):
        first = jnp.logical_and(pl.program_id(0) == 0, pl.program_id(1) == 0)
        err = y_ref[0] - t_ref[0]
        dy_ref[0] = err * (1.0 / d)
        part = jnp.zeros((1, LANES), F32) + jnp.sum(err * err) * (0.5 / d)

        @pl.when(first)
        def _():
            l_ref[...] = part

        @pl.when(jnp.logical_not(first))
        def _():
            l_ref[...] += part

    blk = pl.BlockSpec((1, TM, d), lambda b, i: (b, i, 0))
    return pl.pallas_call(
        body, name="loss_head", grid=(batch, lq // TM), in_specs=[blk, blk],
        out_specs=[pl.BlockSpec((1, LANES), lambda b, i: (0, 0)), blk],
        out_shape=[jax.ShapeDtypeStruct((1, LANES), F32), jax.ShapeDtypeStruct(y.shape, F32)],
        compiler_params=pltpu.CompilerParams(dimension_semantics=("arbitrary", "arbitrary")),
    )(y, target)


def _rope_tables(lat_len, ctx_len):
    rows = lat_len // GRID_W
    row_id = jnp.repeat(jnp.arange(rows), GRID_W).astype(F32)
    col_id = jnp.tile(jnp.arange(GRID_W), rows).astype(F32)
    axis_dim = ROPE // 2
    freqs = ROPE_THETA ** (-jnp.arange(0, axis_dim, 2, dtype=F32) / axis_dim)
    ar = row_id[:, None] * freqs
    ac = col_id[:, None] * freqs
    cr, sr, cc, sc = jnp.cos(ar), jnp.sin(ar), jnp.cos(ac), jnp.sin(ac)
    pad = jnp.zeros((lat_len, LANES - ROPE), F32)
    cos = jnp.concatenate([cr, cr, cc, cc, pad], axis=1)
    sin = jnp.concatenate([-sr, sr, -sc, sc, pad], axis=1)
    ident = jnp.concatenate([jnp.ones((ctx_len, ROPE), F32), jnp.zeros((ctx_len, LANES - ROPE), F32)], axis=1)
    cos = jnp.concatenate([cos, ident], axis=0)
    sin = jnp.concatenate([sin, jnp.zeros((ctx_len, LANES), F32)], axis=0)
    return cos[None], sin[None]


def _prep_weights(w):
    p = dict(w)
    kvc = KV_RANK + ROPE
    if "ml_w_in" in w:
        wi = w["ml_w_in"]
        p["ml_w_in"] = jnp.concatenate(
            [wi[:, :kvc], jnp.zeros((wi.shape[0], LANES - ROPE), wi.dtype), wi[:, kvc:]], axis=1)
    if "ml_w_uq" in w:
        uq = w["ml_w_uq"].reshape(Q_RANK, HEADS, NOPE + ROPE)
        p["ml_w_uq"] = jnp.pad(uq, ((0, 0), (0, 0), (0, HEAD_W - NOPE - ROPE))).reshape(Q_RANK, HEADS * HEAD_W)
    if "ml_rope_norm" in w:
        p["ml_rope_norm"] = jnp.pad(w["ml_rope_norm"], ((0, 0), (0, LANES - ROPE)))
    return p


def _unprep_grads(g):
    out = dict(g)
    kvc = KV_RANK + ROPE
    if "ml_w_in" in g:
        wi = g["ml_w_in"]
        out["ml_w_in"] = jnp.concatenate([wi[:, :kvc], wi[:, kvc + LANES - ROPE:]], axis=1)
    if "ml_w_uq" in g:
        uq = g["ml_w_uq"].reshape(Q_RANK, HEADS, HEAD_W)
        out["ml_w_uq"] = uq[:, :, :NOPE + ROPE].reshape(Q_RANK, HEADS * (NOPE + ROPE))
    if "ml_rope_norm" in g:
        out["ml_rope_norm"] = g["ml_rope_norm"][:, :ROPE]
    return out


LAYER_WEIGHTS = (("cv_w_in", "cv_w_out"), ("pl_w_in", "pl_w_grp", "pl_w_out"),
                 ("ml_w_in", "ml_w_uq", "ml_w_ukv", "ml_w_out"), ("ch_w_in", "ch_w_out"))


class _LocalPlan:
    def __init__(self, w):
        self.small = w
        self.grads = {}

    def weights(self, names):
        return {n: self.small[n] for n in names}

    def hosted(self, tag):
        return None

    def after(self, tag):
        pass

    def note(self, values):
        pass

    def layer_grads(self, layer, grads):
        self.grads.update(grads)


def _local_step(xm, target, mods, plan, lat_len):
    batch, tot, d = xm.shape
    e = d
    n_all, n_lat = tot // TM, lat_len // TM
    cos, sin = _rope_tables(lat_len, tot - lat_len)
    g = {}
    w = dict(plan.small)

    def hosting(tag, fn, *args, **kwargs):
        out = fn(*args, hosted=plan.hosted(tag), **kwargs)
        plan.after(tag)
        return out

    def s1_splits(widths):
        out, s = [], 0
        for wd in widths:
            out.append((0, s, wd))
            s += wd
        return out

    tml = TM_LATENT if lat_len % TM_LATENT == 0 else TM
    n_big = lat_len // tml

    def lat_tiles(n_tiles, tm):
        return n_lat if tm == TM else n_tiles

    def fwd_in(name, x, mod, gi, wname, widths, n_tiles, dtypes=None, tm=TM):
        return hosting(name, _stage_fwd, name, pre=_pre_adaln, post=None, wsel=[0], splits=s1_splits(widths),
                       tiles=[x], eparams=[mod[0], mod[1]], sparams=[w["norm_g"][gi:gi + 1]], weights=[w[wname]],
                       out_widths=widths, out_dtypes=dtypes or [ACT] * len(widths), batch=batch, n_tiles=n_tiles,
                       n_lat_tiles=lat_tiles(n_tiles, tm), tm=tm)

    def bwd_in(name, x, mod, gi, wname, widths, n_tiles, cots, lat_only, add, add_lat_only, stack=None,
               dx_lat_only=False):
        (dx,), (dsh, dsc), (dg,), (dw,) = hosting(
            name, _stage_bwd, name, pre=_pre_adaln, post=None, wsel=[0], splits=s1_splits(widths), tiles=[x],
            tile_diff=[True], eparams=[mod[0], mod[1]], sparams=[w["norm_g"][gi:gi + 1]], weights=[w[wname]],
            cots=cots, cot_lat_only=lat_only, batch=batch, n_tiles=n_tiles, n_lat_tiles=n_lat, add=add,
            add_lat_only=add_lat_only, w_col_stack=[stack], dt_lat_only=dx_lat_only)
        return dx, dsh, dsc, dg, dw

    def fwd_out(name, pre, tiles, mod, sparams, wname, n_tiles, tm=TM):
        return hosting(name, _stage_fwd, name, pre=pre, post=_post_residual(len(tiles) - 1), wsel=[0], splits=None,
                       tiles=tiles, eparams=[mod[2]], sparams=sparams, weights=[w[wname]], out_widths=[d],
                       out_dtypes=[F32], batch=batch, n_tiles=n_tiles, n_lat_tiles=lat_tiles(n_tiles, tm), tm=tm)[0]

    def bwd_out(name, pre, tiles, mod, sparams, wname, n_tiles, cot, tm=TM):
        diff = [True] * (len(tiles) - 1) + [False]
        dts, (dgt,), dss, (dw,) = hosting(
            name, _stage_bwd, name, pre=pre, post=_post_residual(len(tiles) - 1), wsel=[0], splits=None, tiles=tiles,
            tile_diff=diff, eparams=[mod[2]], sparams=sparams, weights=[w[wname]], cots=[cot], cot_lat_only=[False],
            batch=batch, n_tiles=n_tiles, n_lat_tiles=lat_tiles(n_tiles, tm), tm=tm)
        return dts, dgt, dss, dw

    w.update(plan.weights(("cv_w_in",)))
    cv_s = [w["cv_ln_g"], w["cv_ln_b"]]
    a0, b0, g0 = fwd_in("cv_in_fwd", xm, mods[0], 0, "cv_w_in", [e, e, e], n_all)
    c1 = hosting("conv_fwd", _conv_fwd, a0, b0, w["cv_dw"], w["cv_db"], lat_len)
    w.update(plan.weights(("cv_w_out",)))
    x1 = fwd_out("cv_out_fwd", _pre_conv_out, [c1, g0, xm], mods[0], cv_s, "cv_w_out", n_all)

    w.update(plan.weights(LAYER_WEIGHTS[1]))
    pl_s = [w["pl_w_grp"], w["pl_scale"]]
    v1, g1 = fwd_in("pl_in_fwd", x1, mods[1], 1, "pl_w_in", [e, e], n_all, dtypes=[F32, ACT])
    pooled = hosting("pool_fwd", _pool_fwd, v1, lat_len)
    x2 = fwd_out("pl_out_fwd", _pre_pool_out, [pooled, g1, x1], mods[1], pl_s, "pl_w_out", n_all)

    w.update(plan.weights(LAYER_WEIGHTS[2]))
    ml_widths = [KV_RANK, LANES, Q_RANK, HEADS * VDIM]
    ckv, krp, cq, g2 = fwd_in("ml_in_fwd", x2, mods[2], 2, "ml_w_in", ml_widths, n_all)
    k_s = [w["ml_kv_norm"], w["ml_nope_norm"][1:2], w["ml_rope_norm"][1:2]]
    q_s = [w["ml_q_norm"], w["ml_nope_norm"][0:1], w["ml_rope_norm"][0:1]]
    kk, vv = hosting("ml_keys_fwd", _stage_fwd, "ml_keys_fwd", pre=_pre_rms_only, post=_post_mla_keys, wsel=[0],
                     splits=None, tiles=[ckv, krp, cos, sin], eparams=[], sparams=k_s, weights=[w["ml_w_ukv"]],
                     out_widths=[HEADS * HEAD_W, HEADS * VDIM], out_dtypes=[BF16, BF16], batch=batch,
                     n_tiles=n_all, n_lat_tiles=n_lat)
    (qq,) = _stage_fwd("ml_queries_fwd", pre=_pre_rms_only, post=_post_mla_queries, wsel=[0], splits=None,
                       tiles=[cq, cos, sin], eparams=[], sparams=q_s, weights=[w["ml_w_uq"]],
                       out_widths=[HEADS * HEAD_W], out_dtypes=[BF16], batch=batch, n_tiles=n_big,
                       n_lat_tiles=n_big, tm=tml)
    att, lse = hosting("attn_fwd", _attn_fwd, qq, kk, vv)
    x3 = fwd_out("ml_out_fwd", _pre_mla_out, [att, g2, x2], mods[2], [], "ml_w_out", n_big, tm=tml)

    w.update(plan.weights(LAYER_WEIGHTS[3]))
    ch_s = [w["ch_ln_g"], w["ch_ln_b"], w["ch_w_s"], w["ch_b_s"]]
    u3, v3, g3 = fwd_in("ch_in_fwd", x3, mods[3], 3, "ch_w_in", [e, e, e], n_big, tm=tml)
    x4 = fwd_out("ch_out_fwd", _pre_chunk_out, [u3, v3, g3, x3], mods[3], ch_s, "ch_w_out", n_big, tm=tml)

    loss_part, dy = _loss_kernel(x4, target)

    dmods = [None] * 4
    dnorm = [None] * 4
    big = {}
    (du, dv, dg), dgt, (g["ch_ln_g"], g["ch_ln_b"], g["ch_w_s"], g["ch_b_s"]), big["ch_w_out"] = bwd_out(
        "ch_out_bwd", _pre_chunk_out, [u3, v3, g3, x3], mods[3], ch_s, "ch_w_out", n_big, dy, tm=tml)
    plan.note({n: g[n] for n in ("ch_ln_g", "ch_ln_b", "ch_w_s", "ch_b_s")})
    dx3, dsh, dsc, dnorm[3], big["ch_w_in"] = bwd_in("ch_in_bwd", x3, mods[3], 3, "ch_w_in", [e, e, e], n_lat,
                                                     [du, dv, dg], [False] * 3, dy, False, stack=N_CHIP)
    dmods[3] = (dsh, dsc, dgt)
    plan.layer_grads(3, big)

    big = {}
    (datt, dg), dgt, _, big["ml_w_out"] = bwd_out("ml_out_bwd", _pre_mla_out, [att, g2, x2], mods[2], [],
                                                  "ml_w_out", n_big, dx3, tm=tml)
    dq, dk, dvv = hosting("attn_bwd", _attn_bwd, qq, kk, vv, att, lse, datt)
    (dcq,), _, (g["ml_q_norm"], dnope0, drope0), (big["ml_w_uq"],) = hosting(
        "ml_queries_bwd", _stage_bwd, "ml_queries_bwd", pre=_pre_rms_only, post=_post_mla_queries, wsel=[0],
        splits=None, tiles=[cq, cos, sin], tile_diff=[True, False, False], eparams=[], sparams=q_s,
        weights=[w["ml_w_uq"]], cots=[dq], cot_lat_only=[False], batch=batch, n_tiles=n_big, n_lat_tiles=n_big,
        tm=tml)
    (dckv, dkrp), _, (g["ml_kv_norm"], dnope1, drope1), (big["ml_w_ukv"],) = hosting(
        "ml_keys_bwd", _stage_bwd, "ml_keys_bwd", pre=_pre_rms_only, post=_post_mla_keys, wsel=[0], splits=None,
        tiles=[ckv, krp, cos, sin], tile_diff=[True, True, False, False], eparams=[], sparams=k_s,
        weights=[w["ml_w_ukv"]], cots=[dk, dvv], cot_lat_only=[False, False], batch=batch, n_tiles=n_all,
        n_lat_tiles=n_lat, w_col_stack=[N_CHIP])
    g["ml_nope_norm"] = jnp.concatenate([dnope0, dnope1], axis=0)
    g["ml_rope_norm"] = jnp.concatenate([drope0, drope1], axis=0)
    dx2, dsh, dsc, dnorm[2], big["ml_w_in"] = bwd_in("ml_in_bwd", x2, mods[2], 2, "ml_w_in", ml_widths, n_all,
                                                     [dckv, dkrp, dcq, dg], [False, False, True, True], dx3, True)
    dmods[2] = (dsh, dsc, dgt)
    plan.layer_grads(2, big)

    big = {}
    (dpooled, dg), dgt, (big["pl_w_grp"], g["pl_scale"]), big["pl_w_out"] = bwd_out(
        "pl_out_bwd", _pre_pool_out, [pooled, g1, x1], mods[1], pl_s, "pl_w_out", n_all, dx2)
    dv1 = _pool_bwd(dpooled, lat_len)
    dx1, dsh, dsc, dnorm[1], big["pl_w_in"] = bwd_in("pl_in_bwd", x1, mods[1], 1, "pl_w_in", [e, e], n_all,
                                                     [dv1, dg], [False] * 2, dx2, False, stack=N_CHIP)
    dmods[1] = (dsh, dsc, dgt)
    plan.layer_grads(1, big)

    big = {}
    (dc1, dg), dgt, (g["cv_ln_g"], g["cv_ln_b"]), big["cv_w_out"] = bwd_out(
        "cv_out_bwd", _pre_conv_out, [c1, g0, xm], mods[0], cv_s, "cv_w_out", n_all, dx1)
    da, db, g["cv_dw"], g["cv_db"] = hosting("conv_bwd", _conv_bwd, a0, b0, w["cv_dw"], dc1, lat_len)
    dx0, dsh, dsc, dnorm[0], big["cv_w_in"] = bwd_in("cv_in_bwd", xm, mods[0], 0, "cv_w_in", [e, e, e], n_all,
                                                     [da, db, dg], [False] * 3, dx1, False, stack=N_CHIP,
                                                     dx_lat_only=True)
    dmods[0] = (dsh, dsc, dgt)
    plan.layer_grads(0, big)
    g["norm_g"] = jnp.concatenate(dnorm, axis=0)
    return loss_part, dx0, dmods, g


N_DEV = 8
N_CHIP = 4
ANY = pl.BlockSpec(memory_space=pl.ANY)


def _my_place():
    return lax.axis_index("x"), lax.axis_index("y"), lax.axis_index("c")


def _flip(v, f):
    return 1 - v if f else v


def _ag8_copies(x):
    def plan(ins, outs, sems):
        mx, my, mc = _my_place()
        me = 4 * mx + 2 * my + mc
        sends, recvs = [], []
        for rel in range(1, N_DEV):
            peer = (_flip(mx, rel & 4), _flip(my, rel & 2), _flip(mc, rel & 1))
            src_dev = 4 * peer[0] + 2 * peer[1] + peer[2]
            sends.append(_remote(ins[0], outs[0].at[me], sems, rel - 1, peer))
            recvs.append(_remote(ins[0], outs[0].at[src_dev], sems, rel - 1, peer))
        return sends, recvs, [pltpu.make_async_copy(ins[0], outs[0].at[me], sems[2].at[0])]

    return _copies_hosted([x], [jax.ShapeDtypeStruct((N_DEV,) + x.shape, x.dtype)], (N_DEV - 1, N_DEV - 1, 1), plan)


def _ag8(name, x):
    return _run_hosted(name, _ag8_copies(x))[0]


def _ag8_column_copies(x, width):
    def plan(ins, outs, sems):
        mx, my, mc = _my_place()
        me = 4 * mx + 2 * my + mc
        sends, recvs = [], []
        for rel in range(1, N_DEV):
            peer = (_flip(mx, rel & 4), _flip(my, rel & 2), _flip(mc, rel & 1))
            src_dev = 4 * peer[0] + 2 * peer[1] + peer[2]
            cols = pl.ds(pl.multiple_of((2 * peer[0] + peer[1]) * width, LANES), width)
            sends.append(_remote(ins[0].at[:, cols], outs[0].at[me], sems, rel - 1, peer))
            recvs.append(_remote(ins[0].at[:, cols], outs[0].at[src_dev], sems, rel - 1, peer))
        mine = pl.ds(pl.multiple_of((2 * mx + my) * width, LANES), width)
        return sends, recvs, [pltpu.make_async_copy(ins[0].at[:, mine], outs[0].at[me], sems[2].at[0])]

    return _copies_hosted([x], [jax.ShapeDtypeStruct((N_DEV, x.shape[0], width), x.dtype)],
                          (N_DEV - 1, N_DEV - 1, 1), plan)


def _chip_rows_copies(x, rows_per_dev, shared_row):
    n_out = rows_per_dev + 1

    def plan(ins, outs, sems):
        mx, my, mc = _my_place()
        chip = 2 * mx + my
        sends, recvs = [], []

        def pieces(dev):
            return [(ins[0].at[pl.ds(dev * rows_per_dev, rows_per_dev)], slice(0, rows_per_dev)),
                    (ins[0].at[pl.ds(shared_row, 1)], slice(rows_per_dev, n_out))]

        for k, peer, pchip in _chip_peers(mx, my, mc):
            for t, (src, where) in enumerate(pieces(2 * pchip + mc)):
                sends.append(_remote(src, outs[0].at[chip, where], sems, 2 * k + t, peer))
                recvs.append(_remote(src, outs[0].at[pchip, where], sems, 2 * k + t, peer))
        locals_ = [pltpu.make_async_copy(src, outs[0].at[chip, where], sems[2].at[t])
                   for t, (src, where) in enumerate(pieces(2 * chip + mc))]
        return sends, recvs, locals_

    return _copies_hosted([x], [jax.ShapeDtypeStruct((N_CHIP, n_out) + x.shape[1:], x.dtype)], (6, 6, 2), plan)


def _chip_peers(mx, my, mc):
    out = []
    for rel in range(1, N_CHIP):
        px, py = _flip(mx, rel & 2), _flip(my, rel & 1)
        out.append((rel - 1, (px, py, mc), 2 * px + py))
    return out


def _half(mc, rows):
    return pl.ds(pl.multiple_of(mc * (rows // 2), 8), rows // 2)


def _copies_hosted(arrays, out_shapes, n_sems, plan, aliases=None):
    def start(ins, outs, sems):
        sends, _, locals_ = plan(ins, outs, sems)
        for cp in locals_ + sends:
            cp.start()

    def wait(ins, outs, sems):
        sends, recvs, locals_ = plan(ins, outs, sems)
        for cp in recvs:
            cp.wait_recv()
        for cp in sends:
            cp.wait_send()
        for cp in locals_:
            cp.wait()

    return _Hosted(arrays, out_shapes, [pltpu.SemaphoreType.DMA((k,)) for k in n_sems], start, wait, aliases)


def _remote(src, dst, sems, k, peer):
    return pltpu.make_async_remote_copy(src_ref=src, dst_ref=dst, send_sem=sems[0].at[k], recv_sem=sems[1].at[k],
                                        device_id=peer, device_id_type=MESH)


def _gather_ici(shards):
    n = len(shards)

    def plan(ins, outs, sems):
        mx, my, mc = _my_place()
        chip = 2 * mx + my
        sends, recvs, locals_ = [], [], []
        for a in range(n):
            rows = ins[a].shape[0]
            locals_.append(pltpu.make_async_copy(ins[a], outs[a].at[chip], sems[2].at[a]))
            for k, peer, pchip in _chip_peers(mx, my, mc):
                src = ins[a].at[_half(mc, rows)]
                sends.append(_remote(src, outs[a].at[chip, _half(mc, rows)], sems, 3 * a + k, peer))
                recvs.append(_remote(src, outs[a].at[pchip, _half(mc, rows)], sems, 3 * a + k, peer))
        return sends, recvs, locals_

    return _copies_hosted(shards, [jax.ShapeDtypeStruct((N_CHIP,) + s.shape, s.dtype) for s in shards],
                          (3 * n, 3 * n, n), plan)


def _sibling_fill(arrays, row_axis, chips_only_other):
    n = len(arrays)
    per = 3 if chips_only_other else 1

    def plan(ins, outs, sems):
        mx, my, mc = _my_place()
        sibling = (mx, my, 1 - mc)

        def views(a, core):
            rows = outs[a].shape[row_axis]
            if chips_only_other:
                return [outs[a].at[pchip, _half(core, rows)] for _, _, pchip in _chip_peers(mx, my, mc)]
            return [outs[a].at[_half(core, rows)]]

        sends, recvs = [], []
        for a in range(n):
            for k, v in enumerate(views(a, mc)):
                sends.append(_remote(v, v, sems, per * a + k, sibling))
            for k, v in enumerate(views(a, 1 - mc)):
                recvs.append(_remote(v, v, sems, per * a + k, sibling))
        return sends, recvs, []

    return _copies_hosted(arrays, [jax.ShapeDtypeStruct(s.shape, s.dtype) for s in arrays], (per * n, per * n), plan,
                          aliases={a: a for a in range(n)})


def _grad_swap_d2d(stacks):
    n = len(stacks)

    def plan(ins, outs, sems):
        mx, my, mc = _my_place()
        sibling = (mx, my, 1 - mc)
        sends = [_remote(ins[a].at[:, _half(1 - mc, ins[a].shape[1])], outs[a], sems, a, sibling) for a in range(n)]
        return sends, sends, []

    return _copies_hosted(stacks, [jax.ShapeDtypeStruct((N_CHIP, s.shape[1] // 2, s.shape[2]), s.dtype)
                                   for s in stacks], (n, n), plan)


def _grad_exchange_ici(parts):
    n = len(parts)

    def plan(ins, outs, sems):
        mx, my, mc = _my_place()
        chip = 2 * mx + my
        sends, recvs, locals_ = [], [], []
        for a in range(n):
            locals_.append(pltpu.make_async_copy(ins[a].at[chip], outs[a].at[chip], sems[2].at[a]))
            for k, peer, pchip in _chip_peers(mx, my, mc):
                sends.append(_remote(ins[a].at[pchip], outs[a].at[chip], sems, 3 * a + k, peer))
                recvs.append(_remote(ins[a].at[pchip], outs[a].at[pchip], sems, 3 * a + k, peer))
        return sends, recvs, locals_

    return _copies_hosted(parts, [jax.ShapeDtypeStruct(s.shape, s.dtype) for s in parts], (3 * n, 3 * n, n), plan)


def _row_block(rows, limit=256):
    for t in range(min(rows, limit), 7, -8):
        if rows % t == 0 and t % 8 == 0:
            return t
    return rows


def _grad_add_half(core, stack, received):
    _, rows, cw = stack.shape
    rh = rows // 2
    tr = _row_block(rh)

    def body(s_ref, a_ref, b_ref, o_ref):
        o_ref[...] = (a_ref[...] + b_ref[...]).astype(o_ref.dtype)

    grid_spec = pltpu.PrefetchScalarGridSpec(
        num_scalar_prefetch=1, grid=(rh // tr,),
        in_specs=[pl.BlockSpec((N_CHIP, tr, cw), lambda i, s: (0, s[0] * (rh // tr) + i, 0)),
                  pl.BlockSpec((N_CHIP, tr, cw), lambda i, s: (0, i, 0))],
        out_specs=pl.BlockSpec((N_CHIP, tr, cw), lambda i, s: (0, i, 0)))
    return pl.pallas_call(
        body, name="grad_add_half", grid_spec=grid_spec, out_shape=jax.ShapeDtypeStruct(received.shape, BF16),
        compiler_params=pltpu.CompilerParams(dimension_semantics=("arbitrary",), vmem_limit_bytes=VMEM_LIMIT),
    )(core, stack, received)


def _adamw(name, row_off, parts, w, m, v, rows, hosted=None):
    n, _, cw = parts.shape
    tr = _row_block(rows, 128)

    def update(p_ref, w_ref, m_ref, v_ref, g_ref, d_ref, nm_ref, nv_ref):
        g = p_ref[0].astype(F32)
        for k in range(1, n):
            g = g + p_ref[k].astype(F32)
        nm = ADAM_B1 * m_ref[...] + (1.0 - ADAM_B1) * g
        nv = ADAM_B2 * v_ref[...] + (1.0 - ADAM_B2) * (g * g)
        m_hat = nm / (1.0 - ADAM_B1 ** ADAM_STEP)
        v_hat = nv / (1.0 - ADAM_B2 ** ADAM_STEP)
        g_ref[...] = g
        d_ref[...] = -ADAM_LR * (m_hat / (jnp.sqrt(v_hat) + ADAM_EPS) + ADAM_WD * w_ref[...])
        nm_ref[...] = nm
        nv_ref[...] = nv

    out_shape = [jax.ShapeDtypeStruct(w.shape, F32)] * 4
    if row_off is None:
        blk = pl.BlockSpec((tr, cw), lambda i: (i, 0))
        return _pcall(update, name=name, grid=(rows // tr,), out_specs=[blk] * 4, out_shape=out_shape,
                      in_specs=[pl.BlockSpec((n, tr, cw), lambda i: (0, i, 0)), blk, blk, blk],
                      args=[parts, w, m, v], hosted=hosted)

    def body(s_ref, *refs):
        update(*refs)

    full = pl.BlockSpec((tr, cw), lambda i, s: (s[0] // tr + i, 0))
    grid_spec = pltpu.PrefetchScalarGridSpec(
        num_scalar_prefetch=1, grid=(rows // tr,),
        in_specs=[pl.BlockSpec((n, tr, cw), lambda i, s: (0, i, 0)), full, full, full],
        out_specs=[full, full, full, full])
    return pl.pallas_call(
        body, name=name, grid_spec=grid_spec, out_shape=out_shape,
        compiler_params=pltpu.CompilerParams(dimension_semantics=("arbitrary",), vmem_limit_bytes=VMEM_LIMIT),
    )(row_off, parts, w, m, v)


def _sum8(x):
    _, r, cw = x.shape
    tr = _row_block(r, 64)

    def body(x_ref, o_ref):
        acc = x_ref[0]
        for k in range(1, N_DEV):
            acc = acc + x_ref[k]
        o_ref[...] = acc

    return pl.pallas_call(
        body, name="sum8", grid=(r // tr,), in_specs=[pl.BlockSpec((N_DEV, tr, cw), lambda i: (0, i, 0))],
        out_specs=pl.BlockSpec((tr, cw), lambda i: (i, 0)), out_shape=jax.ShapeDtypeStruct((r, cw), F32),
        compiler_params=pltpu.CompilerParams(dimension_semantics=("arbitrary",)),
    )(x)


MOD_ROWS = 24
CTX_ROW = 16


def _mod_fwd(c_rows, w_mod, b_mod, hosted=None):
    nl, d, nn = w_mod.shape

    def body(c_ref, w_ref, b_ref, o_ref):
        o_ref[0] = _dot(_silu(c_ref[...]), w_ref[0]) + b_ref[0]

    return _pcall(
        body, name="mod_fwd", grid=(nl,),
        in_specs=[pl.BlockSpec((MOD_ROWS, d), lambda i: (0, 0)), pl.BlockSpec((1, d, nn), lambda i: (i, 0, 0)),
                  pl.BlockSpec((1, 1, nn), lambda i: (i, 0, 0))],
        out_specs=[pl.BlockSpec((1, MOD_ROWS, nn), lambda i: (i, 0, 0))],
        out_shape=[jax.ShapeDtypeStruct((nl, MOD_ROWS, nn), F32)], args=[c_rows, w_mod, b_mod], hosted=hosted)[0]


def _mod_bwd_rows(dlat, dctx_parts):
    nl, ne, nn = dlat.shape

    def body(l_ref, c_ref, db_ref, dc_ref):
        dc = c_ref[0, 0:1, :]
        for k in range(1, N_DEV):
            dc = dc + c_ref[0, k:k + 1, :]
        db = dc
        for k in range(ne):
            db = db + l_ref[0, k:k + 1, :]
        db_ref[0] = db
        dc_ref[0] = dc

    return pl.pallas_call(
        body, name="mod_bwd_rows", grid=(nl,),
        in_specs=[pl.BlockSpec((1, ne, nn), lambda i: (i, 0, 0)), pl.BlockSpec((1, N_DEV, nn), lambda i: (i, 0, 0))],
        out_specs=[pl.BlockSpec((1, 1, nn), lambda i: (i, 0, 0))] * 2,
        out_shape=[jax.ShapeDtypeStruct((nl, 1, nn), F32)] * 2,
        compiler_params=pltpu.CompilerParams(dimension_semantics=("arbitrary",)),
    )(dlat, dctx_parts)


def _mod_bwd_w(c_cols, d_rows, w_mod, hosted=None):
    nl, d, nn = w_mod.shape

    def body(c_ref, d_ref, w_ref, dw_ref, dc_ref):
        i = pl.program_id(0)
        c = c_ref[...]
        sg = jax.nn.sigmoid(c)
        s = c * sg
        dv = d_ref[0]
        acc = s[:, 0:1] * dv[0:1, :]
        for r in range(1, CTX_ROW + 1):
            acc = acc + s[:, r:r + 1] * dv[r:r + 1, :]
        dw_ref[0] = acc
        ds_ctx = jnp.sum(w_ref[0] * dv[CTX_ROW:CTX_ROW + 1, :], axis=1, keepdims=True)
        cc, sc = c[:, CTX_ROW:CTX_ROW + 1], sg[:, CTX_ROW:CTX_ROW + 1]
        part = ds_ctx * (sc * (1.0 + cc * (1.0 - sc)))

        @pl.when(i == 0)
        def _():
            dc_ref[...] = part

        @pl.when(i != 0)
        def _():
            dc_ref[...] += part

    return _pcall(
        body, name="mod_bwd_w", grid=(nl,),
        in_specs=[pl.BlockSpec((d, MOD_ROWS), lambda i: (0, 0)), pl.BlockSpec((1, MOD_ROWS, nn), lambda i: (i, 0, 0)),
                  pl.BlockSpec((1, d, nn), lambda i: (i, 0, 0))],
        out_specs=[pl.BlockSpec((1, d, nn), lambda i: (i, 0, 0)), pl.BlockSpec((d, 1), lambda i: (0, 0))],
        out_shape=[jax.ShapeDtypeStruct((nl, d, nn), F32), jax.ShapeDtypeStruct((d, 1), F32)],
        args=[c_cols, d_rows, w_mod], hosted=hosted)


def _pack_rows(arrays, width, row_multiple=8):
    rows, spans, r0 = [], [], 0
    for a in arrays:
        flat = a.reshape(-1)
        nr = -(-flat.shape[0] // width)
        held = -(-nr // 8) * 8
        flat = jnp.pad(flat, (0, held * width - flat.shape[0]))
        rows.append(flat.reshape(held, width))
        spans.append((r0, nr, a.shape))
        r0 += held
    if r0 % row_multiple:
        rows.append(jnp.zeros((row_multiple - r0 % row_multiple, width), F32))
    return jnp.concatenate(rows, axis=0), spans


def _unpack_rows(packed, spans):
    out = []
    for r0, nr, shape in spans:
        out.append(packed[r0:r0 + nr].reshape(-1)[:math.prod(shape)].reshape(shape))
    return out


BIG = {"cv_w_in": 1, "cv_w_out": 0, "pl_w_in": 1, "pl_w_grp": None, "pl_w_out": 0, "ml_w_in": 1, "ml_w_uq": 1,
       "ml_w_ukv": 1, "ml_w_out": 0, "ch_w_in": 1, "ch_w_out": 0}
SMALL_SHARDED = ["cv_dw", "pl_scale", "ml_q_norm", "ml_kv_norm", "ch_ln_g", "ch_ln_b"]
SMALL_REPLICATED = ["c_ctx", "norm_g", "b_mod", "cv_db", "cv_ln_g", "cv_ln_b", "ml_nope_norm", "ml_rope_norm",
                    "ch_w_s", "ch_b_s"]
WEIGHTS = ['c_ctx', 'norm_g', 'w_mod', 'b_mod', 'cv_w_in', 'cv_dw', 'cv_db', 'cv_ln_g', 'cv_ln_b', 'cv_w_out',
           'pl_w_in', 'pl_w_grp', 'pl_scale', 'pl_w_out', 'ml_w_in', 'ml_q_norm', 'ml_kv_norm', 'ml_w_uq', 'ml_w_ukv',
           'ml_nope_norm', 'ml_rope_norm', 'ml_w_out', 'ch_w_in', 'ch_ln_g', 'ch_ln_b', 'ch_w_s', 'ch_b_s', 'ch_w_out']


def _shard2d(name, a):
    if name == "pl_w_grp":
        return a.reshape(a.shape[-3] * a.shape[-2], a.shape[-1])
    return a.reshape(a.shape[-2], a.shape[-1])


def _unstack(name, s):
    if name == "pl_w_grp":
        ng = len(POOL_WINDOWS)
        return s.reshape(N_CHIP, ng, s.shape[1] // ng, s.shape[2]).transpose(1, 0, 2, 3).reshape(ng, -1, s.shape[2])
    if BIG[name] == 0:
        return s.reshape(-1, s.shape[2])
    return s.transpose(1, 0, 2).reshape(s.shape[1], -1)


def _stack(name, g):
    if g.ndim == 3 and name != "pl_w_grp":
        return g
    if name == "pl_w_grp":
        ng = len(POOL_WINDOWS)
        return g.reshape(ng, N_CHIP, -1, g.shape[2]).transpose(1, 0, 2, 3).reshape(N_CHIP, -1, g.shape[2])
    if BIG[name] == 0:
        return g.reshape(N_CHIP, -1, g.shape[1])
    return g.reshape(g.shape[0], N_CHIP, -1).transpose(1, 0, 2)


L0, L1, L2, L3 = LAYER_WEIGHTS
EARLY_SMALL = ("ch_w_s", "ch_b_s", "ch_ln_g", "ch_ln_b")
MESH_SCHEDULE = {
    "ag8_inputs": [("gather", L0[:1])], "mod_fwd": [("gfill", L0[:1])],
    "cv_in_fwd": [("gather", L0[1:]), ("gather", L1[:1])], "conv_fwd": [("gfill", L0[1:]), ("gather", L1[1:])],
    "cv_out_fwd": [("gfill", L1)],
    "pl_in_fwd": [("gather", L2[:1])], "pool_fwd": [("gather", L2[1:])], "pl_out_fwd": [("gfill", L2)],
    "attn_fwd": [("gather", L3)], "ml_out_fwd": [("gfill", L3)],
    "ch_in_bwd": [("small", EARLY_SMALL)],
    "ml_out_bwd": [("swap", L3)], "attn_bwd": [("exch", L3)], "ml_queries_bwd": [("ofill", L3)],
    "pl_out_bwd": [("swap", L2)], "pl_in_bwd": [("exch", L2)],
    "cv_out_bwd": [("swap", L1), ("ofill", L2)], "conv_bwd": [("exch", L1)],
    "ag8_dmod": [("swap", L0)], "mod_bwd_w": [("exch", L0), ("ofill", L1)], "ag8_small_grads": [("ofill", L0)],
}


class _MeshPlan:
    def __init__(self, weights, m, v, core):
        self.W, self.M, self.V, self.core = weights, m, v, core
        self.small = None
        self.stack, self.gstack, self.part, self.half, self.out = {}, {}, {}, {}, {}
        self.notes, self.early = {}, {}
        self.live, self.done = {}, set()

    def _make(self, op, names):
        if op == "gather":
            return _gather_ici([_shard2d(n, self.W[n]).astype(BF16) for n in names])
        if op == "gfill":
            return _sibling_fill([self.stack[n] for n in names], 1, True)
        if op == "swap":
            return _grad_swap_d2d([self.gstack[n] for n in names])
        if op == "exch":
            return _grad_exchange_ici([self.part[n] for n in names])
        if op == "ofill":
            return _sibling_fill([t for n in names for t in self.half[n]], 0, False)
        pack, self.early_spans = _pack_rows([self.notes[n] for n in names], LANES, 128)
        return _ag8_copies(pack)

    def _finish_op(self, op, names, hosted):
        self.done.add((op, names))
        res = hosted.results
        if op in ("gather", "gfill"):
            self.stack.update(zip(names, res))
        elif op == "swap":
            for n, r in zip(names, res):
                self.part[n] = _grad_add_half(self.core, self.gstack[n], r)
        elif op == "exch":
            for n, q in zip(names, res):
                rh = q.shape[1]
                self.half[n] = _adamw("adamw_" + n, self.core * rh, q, _shard2d(n, self.W[n]),
                                      _shard2d(n, self.M[n]), _shard2d(n, self.V[n]), rh)
        elif op == "ofill":
            for k, n in enumerate(names):
                self.out[n] = tuple(r.reshape(self.W[n].shape) for r in res[4 * k:4 * k + 4])
        else:
            self.early.update(zip(names, _unpack_rows(_sum8(res[0]), self.early_spans)))

    def alone(self, op, names):
        hosted = self._make(op, names)
        _run_hosted("%s_%s" % (op, names[0]), hosted)
        self._finish_op(op, names, hosted)

    def weights(self, names):
        wk = {n: _unstack(n, self.stack[n]) for n in names}
        if "pl_w_grp" in wk:
            wk["pl_w_grp"] = wk["pl_w_grp"].astype(F32)
        return _prep_weights(wk)

    def hosted(self, tag):
        self.live[tag] = [(op, names, self._make(op, names)) for op, names in MESH_SCHEDULE.get(tag, [])]
        return _merge_hosted([h for _, _, h in self.live[tag]])

    def after(self, tag):
        for op, names, hosted in self.live.pop(tag, []):
            self._finish_op(op, names, hosted)

    def note(self, values):
        self.notes.update(values)

    def layer_grads(self, layer, grads):
        g = _unprep_grads(grads)
        for n in LAYER_WEIGHTS[layer]:
            self.gstack[n] = _stack(n, g[n])

    def finish(self):
        for names in (L3, L2, L1, L0):
            for op in ("swap", "exch", "ofill"):
                if (op, names) not in self.done:
                    self.alone(op, names)
        return self.out


def kernel(x, c, ctx, c_ctx, norm_g, w_mod, b_mod, cv_w_in, cv_dw, cv_db, cv_ln_g, cv_ln_b, cv_w_out, pl_w_in, pl_w_grp, pl_scale, pl_w_out, ml_w_in, ml_q_norm, ml_kv_norm, ml_w_uq, ml_w_ukv, ml_nope_norm, ml_rope_norm, ml_w_out, ch_w_in, ch_ln_g, ch_ln_b, ch_w_s, ch_b_s, ch_w_out, loss_target, m_c_ctx, m_norm_g, m_w_mod, m_b_mod, m_cv_w_in, m_cv_dw, m_cv_db, m_cv_ln_g, m_cv_ln_b, m_cv_w_out, m_pl_w_in, m_pl_w_grp, m_pl_scale, m_pl_w_out, m_ml_w_in, m_ml_q_norm, m_ml_kv_norm, m_ml_w_uq, m_ml_w_ukv, m_ml_nope_norm, m_ml_rope_norm, m_ml_w_out, m_ch_w_in, m_ch_ln_g, m_ch_ln_b, m_ch_w_s, m_ch_b_s, m_ch_w_out, v_c_ctx, v_norm_g, v_w_mod, v_b_mod, v_cv_w_in, v_cv_dw, v_cv_db, v_cv_ln_g, v_cv_ln_b, v_cv_w_out, v_pl_w_in, v_pl_w_grp, v_pl_scale, v_pl_w_out, v_ml_w_in, v_ml_q_norm, v_ml_kv_norm, v_ml_w_uq, v_ml_w_ukv, v_ml_nope_norm, v_ml_rope_norm, v_ml_w_out, v_ch_w_in, v_ch_ln_g, v_ch_ln_b, v_ch_w_s, v_ch_b_s, v_ch_w_out):
    W = dict(c_ctx=c_ctx, norm_g=norm_g, w_mod=w_mod, b_mod=b_mod, cv_w_in=cv_w_in, cv_dw=cv_dw, cv_db=cv_db, cv_ln_g=cv_ln_g, cv_ln_b=cv_ln_b, cv_w_out=cv_w_out, pl_w_in=pl_w_in, pl_w_grp=pl_w_grp, pl_scale=pl_scale, pl_w_out=pl_w_out, ml_w_in=ml_w_in, ml_q_norm=ml_q_norm, ml_kv_norm=ml_kv_norm, ml_w_uq=ml_w_uq, ml_w_ukv=ml_w_ukv, ml_nope_norm=ml_nope_norm, ml_rope_norm=ml_rope_norm, ml_w_out=ml_w_out, ch_w_in=ch_w_in, ch_ln_g=ch_ln_g, ch_ln_b=ch_ln_b, ch_w_s=ch_w_s, ch_b_s=ch_b_s, ch_w_out=ch_w_out)
    M = dict(c_ctx=m_c_ctx, norm_g=m_norm_g, w_mod=m_w_mod, b_mod=m_b_mod, cv_w_in=m_cv_w_in, cv_dw=m_cv_dw, cv_db=m_cv_db, cv_ln_g=m_cv_ln_g, cv_ln_b=m_cv_ln_b, cv_w_out=m_cv_w_out, pl_w_in=m_pl_w_in, pl_w_grp=m_pl_w_grp, pl_scale=m_pl_scale, pl_w_out=m_pl_w_out, ml_w_in=m_ml_w_in, ml_q_norm=m_ml_q_norm, ml_kv_norm=m_ml_kv_norm, ml_w_uq=m_ml_w_uq, ml_w_ukv=m_ml_w_ukv, ml_nope_norm=m_ml_nope_norm, ml_rope_norm=m_ml_rope_norm, ml_w_out=m_ml_w_out, ch_w_in=m_ch_w_in, ch_ln_g=m_ch_ln_g, ch_ln_b=m_ch_ln_b, ch_w_s=m_ch_w_s, ch_b_s=m_ch_b_s, ch_w_out=m_ch_w_out)
    V = dict(c_ctx=v_c_ctx, norm_g=v_norm_g, w_mod=v_w_mod, b_mod=v_b_mod, cv_w_in=v_cv_w_in, cv_dw=v_cv_dw, cv_db=v_cv_db, cv_ln_g=v_cv_ln_g, cv_ln_b=v_cv_ln_b, cv_w_out=v_cv_w_out, pl_w_in=v_pl_w_in, pl_w_grp=v_pl_w_grp, pl_scale=v_pl_scale, pl_w_out=v_pl_w_out, ml_w_in=v_ml_w_in, ml_q_norm=v_ml_q_norm, ml_kv_norm=v_ml_kv_norm, ml_w_uq=v_ml_w_uq, ml_w_ukv=v_ml_w_ukv, ml_nope_norm=v_ml_nope_norm, ml_rope_norm=v_ml_rope_norm, ml_w_out=v_ml_w_out, ch_w_in=v_ch_w_in, ch_ln_g=v_ch_ln_g, ch_ln_b=v_ch_ln_b, ch_w_s=v_ch_w_s, ch_b_s=v_ch_b_s, ch_w_out=v_ch_w_out)

    batch, lat_len, d = x.shape
    mx, my, mc = _my_place()
    chip = 2 * mx + my
    dev = 2 * chip + mc
    core = jnp.reshape(mc, (1,)).astype(jnp.int32)
    zero_off = jnp.zeros((1,), jnp.int32)
    big_names = list(BIG)

    sw = d // N_CHIP
    small_in = [c] + [jnp.pad(W[n].reshape(-1, W[n].shape[-1]), ((0, 0), (0, sw - W[n].shape[-1])))
                      for n in SMALL_SHARDED]
    pack1, spans1 = _pack_rows(small_in, sw)
    plan = _MeshPlan(W, M, V, core)
    gather1 = _ag8_copies(pack1)
    _run_hosted("ag8_inputs", _merge_hosted([gather1, plan.hosted("ag8_inputs")]))
    plan.after("ag8_inputs")
    got1 = gather1.results[0]
    c_all = got1[:, spans1[0][0]:spans1[0][0] + spans1[0][1]].reshape(N_DEV * batch, d)
    full_small = {}
    for n, (r0, nr, _) in zip(SMALL_SHARDED, spans1[1:]):
        blk = got1[0::2, r0:r0 + nr, :W[n].shape[-1]]
        full_small[n] = blk.transpose(1, 0, 2).reshape(nr, -1)

    c_rows = jnp.concatenate([c_all, c_ctx[None], jnp.zeros((MOD_ROWS - CTX_ROW - 1, d), F32)], axis=0)
    nmod = w_mod.shape[2]
    b_shard = lax.dynamic_slice(b_mod, (0, chip * nmod), (b_mod.shape[0], nmod))[:, None, :]
    mod_shard = _mod_fwd(c_rows, w_mod, b_shard, hosted=plan.hosted("mod_fwd"))
    plan.after("mod_fwd")
    mod_rows = mod_shard.transpose(1, 0, 2).reshape(MOD_ROWS, 1, 4 * nmod)
    got2 = _run_hosted("mod_exchange", _chip_rows_copies(mod_rows, batch, CTX_ROW))[0]
    mod_mine = got2.reshape(N_CHIP, batch + 1, 4, nmod).transpose(2, 1, 0, 3).reshape(4, batch + 1, 3 * d)
    mod_lat, mod_ctx = mod_mine[:, :batch], mod_mine[:, batch]
    mods = []
    for i in range(4):
        mods.append(tuple(
            jnp.stack([mod_lat[i, :, j * d:(j + 1) * d], jnp.broadcast_to(mod_ctx[i, j * d:(j + 1) * d], (batch, d))],
                      axis=1)[:, :, None, :] for j in range(3)))

    wk = dict(full_small)
    wk.update(norm_g=norm_g, cv_db=cv_db, cv_ln_g=cv_ln_g, cv_ln_b=cv_ln_b, ml_nope_norm=ml_nope_norm[0],
              ml_rope_norm=ml_rope_norm[0], ch_w_s=ch_w_s[0], ch_b_s=ch_b_s[0])
    plan.small = _prep_weights(wk)
    xm = jnp.concatenate([x, ctx], axis=1)
    loss_part, grad_x, dmods, g = _local_step(xm, loss_target, mods, plan, lat_len)
    g = _unprep_grads(g)

    lat_rows, ctx_rows = [], []
    for i in range(4):
        dsh, dsc, dgt = dmods[i]
        lat_rows.append(jnp.concatenate([dsh[:, 0, 0], dsc[:, 0, 0], dgt[:, 0, 0]], axis=1))
        zero = jnp.zeros((d,), F32)
        cs = [jnp.sum(t[:, 1, 0], axis=0) if ok else zero
              for t, ok in zip((dsh, dsc, dgt), (i <= 2, i <= 2, i <= 1))]
        ctx_rows.append(jnp.concatenate(cs, axis=0)[None])
    dmod_dev = jnp.concatenate(lat_rows + ctx_rows, axis=0)
    dmod_dev = jnp.pad(dmod_dev, ((0, (-dmod_dev.shape[0]) % 8), (0, 0)))
    gather3 = _ag8_column_copies(dmod_dev, nmod)
    _run_hosted("ag8_dmod", _merge_hosted([gather3, plan.hosted("ag8_dmod")]))
    plan.after("ag8_dmod")
    got3 = gather3.results[0]
    dlat = got3[:, :4 * batch].reshape(N_DEV, 4, batch, nmod).transpose(1, 0, 2, 3).reshape(4, N_DEV * batch, nmod)
    dctx_parts = got3[:, 4 * batch:4 * batch + 4].transpose(1, 0, 2)
    g_b_shard, dctx = _mod_bwd_rows(dlat, dctx_parts)
    d_rows = jnp.concatenate([dlat, dctx, jnp.zeros((4, MOD_ROWS - CTX_ROW - 1, nmod), F32)], axis=1)
    g_w_mod, dcc_part = _mod_bwd_w(c_rows.T, d_rows, w_mod, hosted=plan.hosted("mod_bwd_w"))
    plan.after("mod_bwd_w")

    wm2 = w_mod.reshape(-1, nmod)
    res_mod = _adamw("adamw_w_mod", None, g_w_mod.reshape(1, -1, nmod), wm2, M["w_mod"].reshape(-1, nmod),
                     V["w_mod"].reshape(-1, nmod), wm2.shape[0], hosted=plan.hosted("adamw_w_mod"))
    plan.after("adamw_w_mod")
    out = {"w_mod": tuple(r.reshape(w_mod.shape) for r in res_mod)}

    g_small_in = {n: g[n] for n in SMALL_SHARDED if n not in EARLY_SMALL}
    g_small_in.update(norm_g=g["norm_g"], cv_db=g["cv_db"], cv_ln_g=g["cv_ln_g"], cv_ln_b=g["cv_ln_b"],
                      ml_nope_norm=g["ml_nope_norm"], ml_rope_norm=g["ml_rope_norm"],
                      c_ctx=dcc_part.reshape(-1) * (mc == 0).astype(F32), loss=loss_part,
                      b_mod=lax.dynamic_update_slice(jnp.zeros((N_CHIP, 4, nmod), F32),
                                                     g_b_shard[None, :, 0] * (mc == 0).astype(F32), (chip, 0, 0)))
    small_names = list(g_small_in)
    pack4, spans4 = _pack_rows([g_small_in[n] for n in small_names], LANES, 128)
    gather4 = _ag8_copies(pack4)
    _run_hosted("ag8_small_grads", _merge_hosted([gather4, plan.hosted("ag8_small_grads")]))
    plan.after("ag8_small_grads")
    gs = dict(zip(small_names, _unpack_rows(_sum8(gather4.results[0]), spans4)))
    loss = gs["loss"][0, 0]
    gs.update(plan.early)
    gs["b_mod"] = gs["b_mod"].transpose(1, 0, 2).reshape(4, N_CHIP * nmod)
    for n in SMALL_SHARDED:
        wd = W[n].shape[-1]
        gs[n] = lax.dynamic_slice_in_dim(gs[n], chip * wd, wd, axis=1)
    upd_names = SMALL_REPLICATED + SMALL_SHARDED
    pw, spans_u = _pack_rows([W[n] for n in upd_names], LANES, 128)
    pm, _ = _pack_rows([M[n] for n in upd_names], LANES, 128)
    pv, _ = _pack_rows([V[n] for n in upd_names], LANES, 128)
    pg, _ = _pack_rows([gs[n].reshape(W[n].shape) for n in upd_names], LANES, 128)
    res_small = _adamw("adamw_small", None, pg[None], pw, pm, pv, pw.shape[0], hosted=plan.hosted("adamw_small"))
    plan.after("adamw_small")
    for n, vals in zip(upd_names, zip(*[_unpack_rows(r, spans_u) for r in res_small])):
        out[n] = vals
    out.update(plan.finish())

    outs = [loss, grad_x]
    for j in range(4):
        outs.extend(out[n][j] for n in WEIGHTS)
    return tuple(outs)
```

```python
import functools
import math

import jax
import jax.numpy as jnp
from jax import lax
from jax.experimental import pallas as pl
from jax.experimental.pallas import tpu as pltpu

F32 = jnp.float32
BF16 = jnp.bfloat16
ACT = jnp.float32
MESH = pl.DeviceIdType.MESH

EPS = 1e-6
GRID_W = 64
CONV_WIDTH = 31
CONV_HALF = CONV_WIDTH // 2
CONV_PAD = 16
POOL_WINDOWS = (2, 4, 8, 16)
POOL_HALF = max(POOL_WINDOWS) // 2
HEADS = 8
NOPE = 128
ROPE = 64
HEAD_W = 256
VDIM = 128
KV_RANK = 256
Q_RANK = 384
ATT_SCALE = (NOPE + ROPE) ** -0.5
LN2 = math.log(2.0)
Q_PRESCALE = ATT_SCALE / LN2
ROPE_THETA = 10000.0
CHUNK = 128
CHUNK_GROUPS = 8
LANES = 128
TM = 256
TM_LATENT = 512
TQ = 1024
TQ_BWD = 2048
ATT_RQ = 128
ATT_KC = 256
VMEM_LIMIT = 60 * 1024 * 1024

ADAM_LR = 0.001
ADAM_B1 = 0.9
ADAM_B2 = 0.999
ADAM_EPS = 1e-08
ADAM_WD = 0.01
ADAM_STEP = 10


def _dot(a, b):
    return jnp.dot(a.astype(BF16), b.astype(BF16), preferred_element_type=F32)


def _dot_nt(a, b):
    return lax.dot_general(a.astype(BF16), b.astype(BF16), (((1,), (1,)), ((), ())), preferred_element_type=F32)


def _dot_tn(a, b):
    return lax.dot_general(a.astype(BF16), b.astype(BF16), (((0,), (0,)), ((), ())), preferred_element_type=F32)


@jax.custom_vjp
def _mm(a, w):
    return _dot(a, w)


def _mm_fwd(a, w):
    return _dot(a, w), (a, w)


def _mm_bwd(res, ct):
    a, w = res
    return _dot_nt(ct, w), _dot_tn(a, ct)


_mm.defvjp(_mm_fwd, _mm_bwd)


def _swap16_impl(x):
    n = x.shape[-1]
    ax = x.ndim - 1
    lane = lax.broadcasted_iota(jnp.int32, x.shape, ax)
    up = pltpu.roll(x, n - 16, ax)
    dn = pltpu.roll(x, 16, ax)
    return jnp.where((lane % 32) < 16, up, dn)


@jax.custom_vjp
def _swap16(x):
    return _swap16_impl(x)


_swap16.defvjp(lambda x: (_swap16_impl(x), None), lambda _, ct: (_swap16_impl(ct),))


def _rms(x, g, n=None):
    n = x.shape[-1] if n is None else n
    return x * lax.rsqrt(jnp.sum(x * x, axis=-1, keepdims=True) * (1.0 / n) + EPS) * g


def _layernorm(x, g, b):
    mu = jnp.mean(x, axis=-1, keepdims=True)
    xc = x - mu
    var = jnp.mean(xc * xc, axis=-1, keepdims=True)
    return xc * lax.rsqrt(var + EPS) * g + b


def _silu(x):
    return x * jax.nn.sigmoid(x)


def _rope(x, cos, sin):
    return x * cos + _swap16(x) * sin


ANY = pl.BlockSpec(memory_space=pl.ANY)


class _Hosted:
    def __init__(self, arrays, out_shapes, sems, start, wait, aliases=None):
        self.arrays, self.out_shapes, self.sems = list(arrays), list(out_shapes), list(sems)
        self.start, self.wait, self.aliases = start, wait, dict(aliases or {})
        self.results = None


def _merge_hosted(parts):
    parts = [p for p in parts if p is not None]
    if not parts:
        return None
    if len(parts) == 1:
        return parts[0]
    offs, a0, o0, s0 = [], 0, 0, 0
    for p in parts:
        offs.append((a0, o0, s0))
        a0, o0, s0 = a0 + len(p.arrays), o0 + len(p.out_shapes), s0 + len(p.sems)

    def run(which):
        def f(ins, outs, sems):
            for p, (a, o, s) in zip(parts, offs):
                getattr(p, which)(ins[a:a + len(p.arrays)], outs[o:o + len(p.out_shapes)], sems[s:s + len(p.sems)])
        return f

    aliases = {}
    for p, (a, o, _) in zip(parts, offs):
        aliases.update({a + i: o + j for i, j in p.aliases.items()})
    merged = _Hosted(sum((p.arrays for p in parts), []), sum((p.out_shapes for p in parts), []),
                     sum((p.sems for p in parts), []), run("start"), run("wait"), aliases)
    merged.parts, merged.offs = parts, offs
    return merged


def _deliver(hosted, results):
    hosted.results = list(results)
    for p, (_, o, _) in zip(getattr(hosted, "parts", []), getattr(hosted, "offs", [])):
        p.results = list(results[o:o + len(p.out_shapes)])


def _pcall(body, *, name, grid, in_specs, out_specs, out_shape, args, hosted=None, vmem_limit=True, scratch=()):
    n_in, n_out, n_scr = len(args), len(out_shape), len(scratch)
    kwargs = dict(scratch_shapes=list(scratch)) if scratch else {}
    if hosted is not None:
        nhi, nho, inner = len(hosted.arrays), len(hosted.out_shapes), body

        def body(*refs):
            ins, hin = refs[:n_in], refs[n_in:n_in + nhi]
            outs, hout = refs[n_in + nhi:n_in + nhi + n_out], refs[n_in + nhi + n_out:n_in + nhi + n_out + nho]
            own = refs[n_in + nhi + n_out + nho:n_in + nhi + n_out + nho + n_scr]
            sems = refs[n_in + nhi + n_out + nho + n_scr:]
            first, last = None, None
            for k, g in enumerate(grid):
                f, l = pl.program_id(k) == 0, pl.program_id(k) == g - 1
                first = f if first is None else jnp.logical_and(first, f)
                last = l if last is None else jnp.logical_and(last, l)

            @pl.when(first)
            def _():
                hosted.start(hin, hout, sems)

            inner(*ins, *outs, *own)

            @pl.when(last)
            def _():
                hosted.wait(hin, hout, sems)

        in_specs = list(in_specs) + [ANY] * nhi
        out_specs = list(out_specs) + [ANY] * nho
        out_shape = list(out_shape) + hosted.out_shapes
        args = list(args) + hosted.arrays
        kwargs = dict(scratch_shapes=list(scratch) + hosted.sems,
                      input_output_aliases={n_in + i: n_out + j for i, j in hosted.aliases.items()})
    params = dict(dimension_semantics=("arbitrary",) * len(grid))
    if vmem_limit:
        params["vmem_limit_bytes"] = VMEM_LIMIT
    res = pl.pallas_call(body, name=name, grid=grid, in_specs=list(in_specs), out_specs=list(out_specs),
                         out_shape=list(out_shape), compiler_params=pltpu.CompilerParams(**params), **kwargs)(*args)
    if hosted is not None:
        _deliver(hosted, res[n_out:])
    return list(res[:n_out])


def _run_hosted(name, hosted):
    nhi, nho = len(hosted.arrays), len(hosted.out_shapes)

    def body(*refs):
        ins, outs, sems = refs[:nhi], refs[nhi:nhi + nho], refs[nhi + nho:]
        hosted.start(ins, outs, sems)
        hosted.wait(ins, outs, sems)

    res = pl.pallas_call(body, name=name, in_specs=[ANY] * nhi, out_specs=[ANY] * nho, out_shape=hosted.out_shapes,
                         scratch_shapes=hosted.sems, input_output_aliases=hosted.aliases)(*hosted.arrays)
    _deliver(hosted, res)
    return list(res)


def _const_spec(shape, single=False):
    nd = len(shape)
    if single:
        return pl.BlockSpec(shape, lambda b, i: (0,) * nd, pipeline_mode=pl.Buffered(1))
    return pl.BlockSpec(shape, lambda b, i: (0,) * nd)


def _tile_spec(arr, n_lat_tiles, lat_only=False, tm=TM):
    bt, _, cw = arr.shape
    if lat_only:
        return pl.BlockSpec((1, tm, cw), lambda b, i: (b if bt > 1 else 0, jnp.minimum(i, n_lat_tiles - 1), 0))
    return pl.BlockSpec((1, tm, cw), lambda b, i: (b if bt > 1 else 0, i, 0))


def _eparam_spec(arr, n_lat_tiles):
    cw = arr.shape[-1]
    return pl.BlockSpec((1, 1, 1, cw), lambda b, i: (b, (i >= n_lat_tiles).astype(jnp.int32), 0, 0))


def _stage_fwd(name, *, pre, post, wsel, splits, tiles, eparams, sparams, weights, out_widths, out_dtypes,
               batch, n_tiles, n_lat_tiles, hosted=None, tm=TM):
    nt, ne, ns, nw = len(tiles), len(eparams), len(sparams), len(weights)

    def body(*refs):
        t_refs = refs[:nt]
        e_refs = refs[nt:nt + ne]
        s_refs = refs[nt + ne:nt + ne + ns]
        w_refs = refs[nt + ne + ns:nt + ne + ns + nw]
        o_refs = refs[nt + ne + ns + nw:]
        tv = [r[0].astype(F32) for r in t_refs]
        ev = [r[0, 0] for r in e_refs]
        sv = [r[...] for r in s_refs]
        a = pre(tv, ev, sv)
        z = [_dot(a[wsel[j]], w_refs[j][...]) for j in range(nw)]
        if post is None:
            outs = [z[j][:, s:s + w] for (j, s, w) in splits]
        else:
            outs = post(z, tv, ev, sv)
        for o_ref, o in zip(o_refs, outs):
            o_ref[0] = o.astype(o_ref.dtype)

    in_specs = ([_tile_spec(t, n_lat_tiles, tm=tm) for t in tiles] + [_eparam_spec(e, n_lat_tiles) for e in eparams]
                + [_const_spec(s.shape) for s in sparams] + [_const_spec(w.shape, single=True) for w in weights])
    out_shape = [jax.ShapeDtypeStruct((batch, n_tiles * tm, w), dt) for w, dt in zip(out_widths, out_dtypes)]
    out_specs = [pl.BlockSpec((1, tm, w), lambda b, i: (b, i, 0)) for w in out_widths]
    return _pcall(body, name=name, grid=(batch, n_tiles), in_specs=in_specs, out_specs=out_specs,
                  out_shape=out_shape, args=[*tiles, *eparams, *sparams, *weights], hosted=hosted)


def _stage_bwd(name, *, pre, post, wsel, splits, tiles, tile_diff, eparams, sparams, weights, cots, cot_lat_only,
               batch, n_tiles, n_lat_tiles, add=None, add_lat_only=False, hosted=None, w_col_stack=None,
               dt_lat_only=False, tm=TM):
    nt, ne, ns, nw, nc = len(tiles), len(eparams), len(sparams), len(weights), len(cots)
    diff_idx = [k for k in range(nt) if tile_diff[k]]
    nd = len(diff_idx)
    has_add = add is not None
    w_col_stack = w_col_stack or [None] * nw

    def body(*refs):
        pos = 0
        t_refs = refs[pos:pos + nt]; pos += nt
        e_refs = refs[pos:pos + ne]; pos += ne
        s_refs = refs[pos:pos + ns]; pos += ns
        w_refs = refs[pos:pos + nw]; pos += nw
        c_refs = refs[pos:pos + nc]; pos += nc
        if has_add:
            add_ref = refs[pos]; pos += 1
        dt_refs = refs[pos:pos + nd]; pos += nd
        de_refs = refs[pos:pos + ne]; pos += ne
        ds_refs = refs[pos:pos + ns]; pos += ns
        dw_refs = refs[pos:pos + nw]; pos += nw

        b = pl.program_id(0)
        i = pl.program_id(1)
        is_lat = i < n_lat_tiles
        tv = [r[0].astype(F32) for r in t_refs]
        ev = tuple(r[0, 0] for r in e_refs)
        sv = tuple(r[...] for r in s_refs)
        dv0 = tuple(tv[k] for k in diff_idx)

        def merge(dv):
            full = list(tv)
            for k, v in zip(diff_idx, dv):
                full[k] = v
            return full

        def pre_f(dv, ev_, sv_):
            return tuple(pre(merge(dv), list(ev_), list(sv_)))

        a, vjp_pre = jax.vjp(pre_f, dv0, ev, sv)
        cv = []
        for c_ref, lat in zip(c_refs, cot_lat_only):
            c = c_ref[0].astype(F32)
            cv.append(jnp.where(is_lat, c, 0.0) if lat else c)
        if post is None:
            dz = []
            for j in range(nw):
                parts = [cv[k] for k, (jj, _, _) in enumerate(splits) if jj == j]
                dz.append(parts[0] if len(parts) == 1 else jnp.concatenate(parts, axis=1))
            dt2 = de2 = ds2 = None
        else:
            z = tuple(_dot(a[wsel[j]], w_refs[j][...]) for j in range(nw))

            def post_f(z_, dv, ev_, sv_):
                return tuple(post(list(z_), merge(dv), list(ev_), list(sv_)))

            _, vjp_post = jax.vjp(post_f, z, dv0, ev, sv)
            dz, dt2, de2, ds2 = vjp_post(tuple(cv))
        da = [None] * len(a)
        dws = []
        for j in range(nw):
            g = _dot_nt(dz[j], w_refs[j][...])
            da[wsel[j]] = g if da[wsel[j]] is None else da[wsel[j]] + g
            dws.append(_dot_tn(a[wsel[j]], dz[j]))
        da = tuple(jnp.zeros_like(a[k]) if da[k] is None else da[k] for k in range(len(a)))
        dt1, de1, ds1 = vjp_pre(da)

        def plus(u, v):
            return u if v is None else u + v

        for k in range(nd):
            val = plus(dt1[k], None if dt2 is None else dt2[k])
            if has_add and k == 0:
                addv = add_ref[0].astype(F32)
                val = val + (jnp.where(is_lat, addv, 0.0) if add_lat_only else addv)
            if dt_lat_only:
                @pl.when(is_lat)
                def _(k=k, val=val):
                    dt_refs[k][0] = val.astype(dt_refs[k].dtype)
            else:
                dt_refs[k][0] = val.astype(dt_refs[k].dtype)

        seg_first = jnp.logical_or(i == 0, i == n_lat_tiles)
        for k in range(ne):
            val = plus(de1[k], None if de2 is None else de2[k])

            @pl.when(seg_first)
            def _(k=k, val=val):
                de_refs[k][0, 0] = val

            @pl.when(jnp.logical_not(seg_first))
            def _(k=k, val=val):
                de_refs[k][0, 0] += val

        first = jnp.logical_and(b == 0, i == 0)
        acc = [(ds_refs[k], plus(ds1[k], None if ds2 is None else ds2[k])) for k in range(ns)]
        for j in range(nw):
            if w_col_stack[j]:
                cw = dws[j].shape[1] // w_col_stack[j]
                acc += [(dw_refs[j].at[c], dws[j][:, c * cw:(c + 1) * cw]) for c in range(w_col_stack[j])]
            else:
                acc.append((dw_refs[j], dws[j]))
        for ref, val in acc:
            @pl.when(first)
            def _(ref=ref, val=val):
                ref[...] = val

            @pl.when(jnp.logical_not(first))
            def _(ref=ref, val=val):
                ref[...] += val

    in_specs = ([_tile_spec(t, n_lat_tiles, tm=tm) for t in tiles] + [_eparam_spec(e, n_lat_tiles) for e in eparams]
                + [_const_spec(s.shape) for s in sparams] + [_const_spec(w.shape, single=True) for w in weights]
                + [_tile_spec(c, n_lat_tiles, lat, tm) for c, lat in zip(cots, cot_lat_only)])
    args = [*tiles, *eparams, *sparams, *weights, *cots]
    if has_add:
        in_specs.append(_tile_spec(add, n_lat_tiles, add_lat_only, tm))
        args.append(add)
    dt_tiles = n_lat_tiles if dt_lat_only else n_tiles
    out_shape = [jax.ShapeDtypeStruct((batch, dt_tiles * tm, tiles[k].shape[-1]), F32) for k in diff_idx]
    out_specs = [pl.BlockSpec((1, tm, tiles[k].shape[-1]), lambda b, i: (b, jnp.minimum(i, dt_tiles - 1), 0))
                 for k in diff_idx]
    out_shape += [jax.ShapeDtypeStruct(e.shape, F32) for e in eparams]
    out_specs += [_eparam_spec(e, n_lat_tiles) for e in eparams]
    out_shape += [jax.ShapeDtypeStruct(s.shape, F32) for s in sparams]
    out_specs += [_const_spec(s.shape) for s in sparams]
    dw_shapes = [(n, w.shape[0], w.shape[1] // n) if n else w.shape for w, n in zip(weights, w_col_stack)]
    out_shape += [jax.ShapeDtypeStruct(s, F32) for s in dw_shapes]
    out_specs += [_const_spec(s, single=True) for s in dw_shapes]
    res = _pcall(body, name=name, grid=(batch, n_tiles), in_specs=in_specs, out_specs=out_specs,
                 out_shape=out_shape, args=args, hosted=hosted)
    return res[:nd], res[nd:nd + ne], res[nd + ne:nd + ne + ns], res[nd + ne + ns:]


def _pre_adaln(tv, ev, sv):
    x = tv[0]
    sh, sc = ev[0], ev[1]
    return [_rms(x, sv[0]) * (1.0 + sc) + sh]


def _post_residual(x_index):
    def post(z, tv, ev, sv):
        return [tv[x_index] + ev[-1] * z[0]]
    return post


def _pre_conv_out(tv, ev, sv):
    c1, gg = tv[0], tv[1]
    return [_silu(_layernorm(c1, sv[0], sv[1])) * _silu(gg)]


def _pre_pool_out(tv, ev, sv):
    pooled, gg = tv[0], tv[1]
    w_grp, scale = sv[0], sv[1]
    gw = w_grp.shape[-1]
    y = jnp.concatenate([_mm(pooled[:, k * gw:(k + 1) * gw], w_grp[k]) for k in range(w_grp.shape[0])], axis=1)
    return [y * scale * _silu(gg)]


def _pre_rms_only(tv, ev, sv):
    return [_rms(tv[0], sv[0])]


def _post_mla_keys(z, tv, ev, sv):
    krp, cos, sin = tv[1], tv[2], tv[3]
    nope_g, rope_g = sv[1], sv[2]
    kv = z[0]
    kr = _rope(_rms(krp, rope_g, ROPE), cos, sin)
    ks, vs = [], []
    for h in range(HEADS):
        ks.append(_rms(kv[:, h * 2 * NOPE:h * 2 * NOPE + NOPE], nope_g))
        ks.append(kr)
        vs.append(kv[:, h * 2 * NOPE + NOPE:(h + 1) * 2 * NOPE])
    return [jnp.concatenate(ks, axis=1), jnp.concatenate(vs, axis=1)]


def _post_mla_queries(z, tv, ev, sv):
    cos, sin = tv[1], tv[2]
    nope_g, rope_g = sv[1], sv[2]
    q = z[0]
    qs = []
    for h in range(HEADS):
        qs.append(_rms(q[:, h * HEAD_W:h * HEAD_W + NOPE], nope_g))
        qs.append(_rope(_rms(q[:, h * HEAD_W + NOPE:(h + 1) * HEAD_W], rope_g, ROPE), cos, sin))
    return [jnp.concatenate(qs, axis=1) * Q_PRESCALE]


def _pre_mla_out(tv, ev, sv):
    return [tv[0] * _silu(tv[1])]


def _pre_chunk_out(tv, ev, sv):
    u, v, gg = tv[0], tv[1], tv[2]
    ln_g, ln_b, w_s, b_s = sv
    vn = _layernorm(v, ln_g, ln_b)
    rows = []
    for n in range(vn.shape[0] // CHUNK):
        blk = vn[n * CHUNK:(n + 1) * CHUNK]
        cols = [_mm(w_s[g], blk[:, g * LANES:(g + 1) * LANES]) + b_s[:, g:g + 1] for g in range(CHUNK_GROUPS)]
        rows.append(jnp.concatenate(cols, axis=1))
    s = jnp.concatenate(rows, axis=0)
    return [u * s * _silu(gg)]


def _segments(lat_len, tot_len):
    segs = [(0, lat_len)]
    if tot_len > lat_len:
        segs.append((lat_len, tot_len - lat_len))
    return segs


def _pad_rows(x):
    z = jnp.zeros((CONV_PAD, x.shape[1]), x.dtype)
    return jnp.concatenate([z, x, z], axis=0)


def _shifted(xp, j):
    n = xp.shape[0] - 2 * CONV_PAD
    if j != 0:
        xp = pltpu.roll(xp, (-j) % xp.shape[0], 0)
    return xp[CONV_PAD:CONV_PAD + n]


def _conv_fwd(a, bgate, dw, db, lat_len, hosted=None):
    batch, tot, e = a.shape
    segs = _segments(lat_len, tot)

    def body(a_ref, b_ref, dw_ref, db_ref, o_ref):
        w = dw_ref[...]
        for (s0, n) in segs:
            y = a_ref[0, s0:s0 + n, :].astype(F32) * jax.nn.sigmoid(b_ref[0, s0:s0 + n, :].astype(F32))
            yp = _pad_rows(y)
            acc = jnp.zeros_like(y) + db_ref[...]
            for k in range(CONV_WIDTH):
                acc = acc + _shifted(yp, k - CONV_HALF) * w[k:k + 1, :]
            o_ref[0, s0:s0 + n, :] = acc.astype(o_ref.dtype)

    blk = pl.BlockSpec((1, tot, LANES), lambda b, cb: (b, 0, cb))
    return _pcall(
        body, name="conv_fwd", grid=(batch, e // LANES),
        in_specs=[blk, blk, pl.BlockSpec((CONV_WIDTH, LANES), lambda b, cb: (0, cb)),
                  pl.BlockSpec((1, LANES), lambda b, cb: (0, cb))],
        out_specs=[blk], out_shape=[jax.ShapeDtypeStruct(a.shape, ACT)], args=[a, bgate, dw, db], hosted=hosted)[0]


def _conv_bwd(a, bgate, dw, dc1, lat_len, hosted=None):
    batch, tot, e = a.shape
    segs = _segments(lat_len, tot)

    def body(a_ref, b_ref, dw_ref, dc_ref, da_ref, dg_ref, ddw_ref, ddb_ref):
        b = pl.program_id(1)
        w = dw_ref[...]
        ddw_rows = [None] * CONV_WIDTH
        ddb = None
        for (s0, n) in segs:
            av = a_ref[0, s0:s0 + n, :].astype(F32)
            sg = jax.nn.sigmoid(b_ref[0, s0:s0 + n, :].astype(F32))
            y = av * sg
            dc = dc_ref[0, s0:s0 + n, :]
            yp, dcp = _pad_rows(y), _pad_rows(dc)
            dy = jnp.zeros_like(y)
            for k in range(CONV_WIDTH):
                j = k - CONV_HALF
                dy = dy + _shifted(dcp, -j) * w[k:k + 1, :]
                r = jnp.sum(dc * _shifted(yp, j), axis=0, keepdims=True)
                ddw_rows[k] = r if ddw_rows[k] is None else ddw_rows[k] + r
            r = jnp.sum(dc, axis=0, keepdims=True)
            ddb = r if ddb is None else ddb + r
            da_ref[0, s0:s0 + n, :] = dy * sg
            dg_ref[0, s0:s0 + n, :] = dy * av * sg * (1.0 - sg)

        @pl.when(b == 0)
        def _():
            ddw_ref[...] = jnp.zeros_like(ddw_ref)
            ddb_ref[...] = jnp.zeros_like(ddb_ref)

        for k in range(CONV_WIDTH):
            ddw_ref[k:k + 1, :] += ddw_rows[k]
        ddb_ref[...] += ddb

    blk = pl.BlockSpec((1, tot, LANES), lambda cb, b: (b, 0, cb))
    wspec = pl.BlockSpec((CONV_WIDTH, LANES), lambda cb, b: (0, cb))
    bspec = pl.BlockSpec((1, LANES), lambda cb, b: (0, cb))
    return _pcall(
        body, name="conv_bwd", grid=(e // LANES, batch),
        in_specs=[blk, blk, wspec, blk],
        out_specs=[blk, blk, wspec, bspec],
        out_shape=[jax.ShapeDtypeStruct(a.shape, F32), jax.ShapeDtypeStruct(a.shape, F32),
                   jax.ShapeDtypeStruct((CONV_WIDTH, e), F32), jax.ShapeDtypeStruct((1, e), F32)],
        args=[a, bgate, dw, dc1], hosted=hosted)


def _pool_counts(n, half, shape):
    t = lax.broadcasted_iota(jnp.int32, shape, 0)
    cnt = jnp.minimum(t + half, n) - jnp.maximum(t - half, 0)
    return cnt.astype(F32)


def _per_group(fn):
    for k, window in enumerate(POOL_WINDOWS):
        @pl.when(pl.program_id(1) == k)
        def _(window=window):
            fn(window // 2)


def _pool_fwd(v, lat_len, hosted=None):
    batch, tot, e = v.shape
    gw = e // len(POOL_WINDOWS)
    segs = _segments(lat_len, tot)

    def body(v_ref, o_ref):
        def group(half):
            for (s0, n) in segs:
                x = v_ref[0, s0:s0 + n, :]
                xp = _pad_rows(x)
                acc = _shifted(xp, -half)
                for j in range(-half + 1, half):
                    acc = acc + _shifted(xp, j)
                o_ref[0, s0:s0 + n, :] = (acc / _pool_counts(n, half, x.shape) - x).astype(o_ref.dtype)

        _per_group(group)

    blk = pl.BlockSpec((1, tot, gw), lambda b, g: (b, 0, g))
    return _pcall(body, name="pool_fwd", grid=(batch, len(POOL_WINDOWS)), in_specs=[blk], out_specs=[blk],
                  out_shape=[jax.ShapeDtypeStruct(v.shape, ACT)], args=[v], hosted=hosted)[0]


def _pool_bwd(dp, lat_len):
    batch, tot, e = dp.shape
    gw = e // len(POOL_WINDOWS)
    segs = _segments(lat_len, tot)

    def body(d_ref, o_ref):
        def group(half):
            for (s0, n) in segs:
                d = d_ref[0, s0:s0 + n, :]
                dnp = _pad_rows(d / _pool_counts(n, half, d.shape))
                acc = _shifted(dnp, half)
                for j in range(-half + 1, half):
                    acc = acc + _shifted(dnp, -j)
                o_ref[0, s0:s0 + n, :] = acc - d

        _per_group(group)

    blk = pl.BlockSpec((1, tot, gw), lambda b, g: (b, 0, g))
    return pl.pallas_call(
        body, name="pool_bwd", grid=(batch, len(POOL_WINDOWS)), in_specs=[blk], out_specs=blk,
        out_shape=jax.ShapeDtypeStruct(dp.shape, F32),
        compiler_params=pltpu.CompilerParams(dimension_semantics=("arbitrary", "arbitrary"),
                                             vmem_limit_bytes=VMEM_LIMIT),
    )(dp)


def _attn_fwd(q, k, v, hosted=None):
    batch, lq, _ = q.shape
    tk = k.shape[1]
    tq = min(TQ, lq)

    def body(q_ref, k_ref, v_ref, o_ref, lse_ref):
        s2 = _dot_nt(q_ref[0], k_ref[0])
        m2 = jnp.max(s2, axis=-1, keepdims=True)
        e = jnp.exp2(s2 - m2)
        l = jnp.sum(e, axis=-1, keepdims=True)
        o_ref[0] = (_dot(e, v_ref[0]) / l).astype(o_ref.dtype)
        lse_ref[0, 0] = m2 + jnp.log2(l)

    return _pcall(
        body, name="attn_fwd", grid=(batch, HEADS, lq // tq),
        in_specs=[pl.BlockSpec((1, tq, HEAD_W), lambda b, h, i: (b, i, h)),
                  pl.BlockSpec((1, tk, HEAD_W), lambda b, h, i: (b, 0, h)),
                  pl.BlockSpec((1, tk, VDIM), lambda b, h, i: (b, 0, h))],
        out_specs=[pl.BlockSpec((1, tq, VDIM), lambda b, h, i: (b, i, h)),
                   pl.BlockSpec((1, 1, tq, 1), lambda b, h, i: (b, h, i, 0))],
        out_shape=[jax.ShapeDtypeStruct((batch, lq, HEADS * VDIM), ACT),
                   jax.ShapeDtypeStruct((batch, HEADS, lq, 1), F32)], args=[q, k, v], hosted=hosted)


def _attn_bwd(q, k, v, o, lse, do, hosted=None):
    batch, lq, _ = q.shape
    tk = k.shape[1]
    tq = min(TQ_BWD, lq)

    def body(q_ref, k_ref, v_ref, o_ref, lse_ref, do_ref, dq_ref, dk_ref, dv_ref, p_scr, ds_scr):
        i = pl.program_id(2)
        nr = tq // ATT_RQ
        rows = [slice(r * ATT_RQ, (r + 1) * ATT_RQ) for r in range(nr)]
        qv = [q_ref[0, rw, :] for rw in rows]
        dob = [do_ref[0, rw, :].astype(BF16) for rw in rows]
        row_lse = [lse_ref[0, 0, rw, :] for rw in rows]
        delta = [jnp.sum(do_ref[0, rw, :] * o_ref[0, rw, :], axis=-1, keepdims=True) for rw in rows]
        for c in range(tk // ATT_KC):
            keys = slice(c * ATT_KC, (c + 1) * ATT_KC)
            kc, vc = k_ref[0, keys, :], v_ref[0, keys, :]
            for r in range(nr):
                p = jnp.exp2(_dot_nt(qv[r], kc) - row_lse[r])
                dp = _dot_nt(dob[r], vc)
                p_scr[rows[r], keys] = p.astype(BF16)
                ds_scr[rows[r], keys] = (p * (dp - delta[r]) * LN2).astype(BF16)
        dq_ref[0] = _dot(ds_scr[...], k_ref[0])
        dk = _dot_tn(ds_scr[...], q_ref[0])
        dv = _dot_tn(p_scr[...], do_ref[0])

        @pl.when(i == 0)
        def _():
            dk_ref[0] = dk
            dv_ref[0] = dv

        @pl.when(i != 0)
        def _():
            dk_ref[0] += dk
            dv_ref[0] += dv

    return _pcall(
        body, name="attn_bwd", grid=(batch, HEADS, lq // tq),
        in_specs=[pl.BlockSpec((1, tq, HEAD_W), lambda b, h, i: (b, i, h)),
                  pl.BlockSpec((1, tk, HEAD_W), lambda b, h, i: (b, 0, h)),
                  pl.BlockSpec((1, tk, VDIM), lambda b, h, i: (b, 0, h)),
                  pl.BlockSpec((1, tq, VDIM), lambda b, h, i: (b, i, h)),
                  pl.BlockSpec((1, 1, tq, 1), lambda b, h, i: (b, h, i, 0)),
                  pl.BlockSpec((1, tq, VDIM), lambda b, h, i: (b, i, h))],
        out_specs=[pl.BlockSpec((1, tq, HEAD_W), lambda b, h, i: (b, i, h)),
                   pl.BlockSpec((1, tk, HEAD_W), lambda b, h, i: (b, 0, h)),
                   pl.BlockSpec((1, tk, VDIM), lambda b, h, i: (b, 0, h))],
        out_shape=[jax.ShapeDtypeStruct(q.shape, F32), jax.ShapeDtypeStruct(k.shape, F32),
                   jax.ShapeDtypeStruct(v.shape, F32)],
        args=[q, k, v, o, lse, do], hosted=hosted,
        scratch=[pltpu.VMEM((tq, tk), BF16), pltpu.VMEM((tq, tk), BF16)])


def _loss_kernel(y, target):
    batch, lq, d = y.shape

    def body(y_ref, t_ref, l_ref, dy_ref):
        first = jnp.logical_and(pl.program_id(0) == 0, pl.program_id(1) == 0)
        err = y_ref[0] - t_ref[0]
        dy_ref[0] = err * (1.0 / d)
        part = jnp.zeros((1, LANES), F32) + jnp.sum(err * err) * (0.5 / d)

        @pl.when(first)
        def _():
            l_ref[...] = part

        @pl.when(jnp.logical_not(first))
        def _():
            l_ref[...] += part

    blk = pl.BlockSpec((1, TM, d), lambda b, i: (b, i, 0))
    return pl.pallas_call(
        body, name="loss_head", grid=(batch, lq // TM), in_specs=[blk, blk],
        out_specs=[pl.BlockSpec((1, LANES), lambda b, i: (0, 0)), blk],
        out_shape=[jax.ShapeDtypeStruct((1, LANES), F32), jax.ShapeDtypeStruct(y.shape, F32)],
        compiler_params=pltpu.CompilerParams(dimension_semantics=("arbitrary", "arbitrary")),
    )(y, target)


def _rope_tables(lat_len, ctx_len):
    rows = lat_len // GRID_W
    row_id = jnp.repeat(jnp.arange(rows), GRID_W).astype(F32)
    col_id = jnp.tile(jnp.arange(GRID_W), rows).astype(F32)
    axis_dim = ROPE // 2
    freqs = ROPE_THETA ** (-jnp.arange(0, axis_dim, 2, dtype=F32) / axis_dim)
    ar = row_id[:, None] * freqs
    ac = col_id[:, None] * freqs
    cr, sr, cc, sc = jnp.cos(ar), jnp.sin(ar), jnp.cos(ac), jnp.sin(ac)
    pad = jnp.zeros((lat_len, LANES - ROPE), F32)
    cos = jnp.concatenate([cr, cr, cc, cc, pad], axis=1)
    sin = jnp.concatenate([-sr, sr, -sc, sc, pad], axis=1)
    ident = jnp.concatenate([jnp.ones((ctx_len, ROPE), F32), jnp.zeros((ctx_len, LANES - ROPE), F32)], axis=1)
    cos = jnp.concatenate([cos, ident], axis=0)
    sin = jnp.concatenate([sin, jnp.zeros((ctx_len, LANES), F32)], axis=0)
    return cos[None], sin[None]


def _prep_weights(w):
    p = dict(w)
    kvc = KV_RANK + ROPE
    if "ml_w_in" in w:
        wi = w["ml_w_in"]
        p["ml_w_in"] = jnp.concatenate(
            [wi[:, :kvc], jnp.zeros((wi.shape[0], LANES - ROPE), wi.dtype), wi[:, kvc:]], axis=1)
    if "ml_w_uq" in w:
        uq = w["ml_w_uq"].reshape(Q_RANK, HEADS, NOPE + ROPE)
        p["ml_w_uq"] = jnp.pad(uq, ((0, 0), (0, 0), (0, HEAD_W - NOPE - ROPE))).reshape(Q_RANK, HEADS * HEAD_W)
    if "ml_rope_norm" in w:
        p["ml_rope_norm"] = jnp.pad(w["ml_rope_norm"], ((0, 0), (0, LANES - ROPE)))
    return p


def _unprep_grads(g):
    out = dict(g)
    kvc = KV_RANK + ROPE
    if "ml_w_in" in g:
        wi = g["ml_w_in"]
        out["ml_w_in"] = jnp.concatenate([wi[:, :kvc], wi[:, kvc + LANES - ROPE:]], axis=1)
    if "ml_w_uq" in g:
        uq = g["ml_w_uq"].reshape(Q_RANK, HEADS, HEAD_W)
        out["ml_w_uq"] = uq[:, :, :NOPE + ROPE].reshape(Q_RANK, HEADS * (NOPE + ROPE))
    if "ml_rope_norm" in g:
        out["ml_rope_norm"] = g["ml_rope_norm"][:, :ROPE]
    return out


LAYER_WEIGHTS = (("cv_w_in", "cv_w_out"), ("pl_w_in", "pl_w_grp", "pl_w_out"),
                 ("ml_w_in", "ml_w_uq", "ml_w_ukv", "ml_w_out"), ("ch_w_in", "ch_w_out"))


class _LocalPlan:
    def __init__(self, w):
        self.small = w
        self.grads = {}

    def weights(self, names):
        return {n: self.small[n] for n in names}

    def hosted(self, tag):
        return None

    def after(self, tag):
        pass

    def note(self, values):
        pass

    def layer_grads(self, layer, grads):
        self.grads.update(grads)


def _local_step(xm, target, mods, plan, lat_len):
    batch, tot, d = xm.shape
    e = d
    n_all, n_lat = tot // TM, lat_len // TM
    cos, sin = _rope_tables(lat_len, tot - lat_len)
    g = {}
    w = dict(plan.small)

    def hosting(tag, fn, *args, **kwargs):
        out = fn(*args, hosted=plan.hosted(tag), **kwargs)
        plan.after(tag)
        return out

    def s1_splits(widths):
        out, s = [], 0
        for wd in widths:
            out.append((0, s, wd))
            s += wd
        return out

    tml = TM_LATENT if lat_len % TM_LATENT == 0 else TM
    n_big = lat_len // tml

    def lat_tiles(n_tiles, tm):
        return n_lat if tm == TM else n_tiles

    def fwd_in(name, x, mod, gi, wname, widths, n_tiles, dtypes=None, tm=TM):
        return hosting(name, _stage_fwd, name, pre=_pre_adaln, post=None, wsel=[0], splits=s1_splits(widths),
                       tiles=[x], eparams=[mod[0], mod[1]], sparams=[w["norm_g"][gi:gi + 1]], weights=[w[wname]],
                       out_widths=widths, out_dtypes=dtypes or [ACT] * len(widths), batch=batch, n_tiles=n_tiles,
                       n_lat_tiles=lat_tiles(n_tiles, tm), tm=tm)

    def bwd_in(name, x, mod, gi, wname, widths, n_tiles, cots, lat_only, add, add_lat_only, stack=None,
               dx_lat_only=False):
        (dx,), (dsh, dsc), (dg,), (dw,) = hosting(
            name, _stage_bwd, name, pre=_pre_adaln, post=None, wsel=[0], splits=s1_splits(widths), tiles=[x],
            tile_diff=[True], eparams=[mod[0], mod[1]], sparams=[w["norm_g"][gi:gi + 1]], weights=[w[wname]],
            cots=cots, cot_lat_only=lat_only, batch=batch, n_tiles=n_tiles, n_lat_tiles=n_lat, add=add,
            add_lat_only=add_lat_only, w_col_stack=[stack], dt_lat_only=dx_lat_only)
        return dx, dsh, dsc, dg, dw

    def fwd_out(name, pre, tiles, mod, sparams, wname, n_tiles, tm=TM):
        return hosting(name, _stage_fwd, name, pre=pre, post=_post_residual(len(tiles) - 1), wsel=[0], splits=None,
                       tiles=tiles, eparams=[mod[2]], sparams=sparams, weights=[w[wname]], out_widths=[d],
                       out_dtypes=[F32], batch=batch, n_tiles=n_tiles, n_lat_tiles=lat_tiles(n_tiles, tm), tm=tm)[0]

    def bwd_out(name, pre, tiles, mod, sparams, wname, n_tiles, cot, tm=TM):
        diff = [True] * (len(tiles) - 1) + [False]
        dts, (dgt,), dss, (dw,) = hosting(
            name, _stage_bwd, name, pre=pre, post=_post_residual(len(tiles) - 1), wsel=[0], splits=None, tiles=tiles,
            tile_diff=diff, eparams=[mod[2]], sparams=sparams, weights=[w[wname]], cots=[cot], cot_lat_only=[False],
            batch=batch, n_tiles=n_tiles, n_lat_tiles=lat_tiles(n_tiles, tm), tm=tm)
        return dts, dgt, dss, dw

    w.update(plan.weights(("cv_w_in",)))
    cv_s = [w["cv_ln_g"], w["cv_ln_b"]]
    a0, b0, g0 = fwd_in("cv_in_fwd", xm, mods[0], 0, "cv_w_in", [e, e, e], n_all)
    c1 = hosting("conv_fwd", _conv_fwd, a0, b0, w["cv_dw"], w["cv_db"], lat_len)
    w.update(plan.weights(("cv_w_out",)))
    x1 = fwd_out("cv_out_fwd", _pre_conv_out, [c1, g0, xm], mods[0], cv_s, "cv_w_out", n_all)

    w.update(plan.weights(LAYER_WEIGHTS[1]))
    pl_s = [w["pl_w_grp"], w["pl_scale"]]
    v1, g1 = fwd_in("pl_in_fwd", x1, mods[1], 1, "pl_w_in", [e, e], n_all, dtypes=[F32, ACT])
    pooled = hosting("pool_fwd", _pool_fwd, v1, lat_len)
    x2 = fwd_out("pl_out_fwd", _pre_pool_out, [pooled, g1, x1], mods[1], pl_s, "pl_w_out", n_all)

    w.update(plan.weights(LAYER_WEIGHTS[2]))
    ml_widths = [KV_RANK, LANES, Q_RANK, HEADS * VDIM]
    ckv, krp, cq, g2 = fwd_in("ml_in_fwd", x2, mods[2], 2, "ml_w_in", ml_widths, n_all)
    k_s = [w["ml_kv_norm"], w["ml_nope_norm"][1:2], w["ml_rope_norm"][1:2]]
    q_s = [w["ml_q_norm"], w["ml_nope_norm"][0:1], w["ml_rope_norm"][0:1]]
    kk, vv = hosting("ml_keys_fwd", _stage_fwd, "ml_keys_fwd", pre=_pre_rms_only, post=_post_mla_keys, wsel=[0],
                     splits=None, tiles=[ckv, krp, cos, sin], eparams=[], sparams=k_s, weights=[w["ml_w_ukv"]],
                     out_widths=[HEADS * HEAD_W, HEADS * VDIM], out_dtypes=[BF16, BF16], batch=batch,
                     n_tiles=n_all, n_lat_tiles=n_lat)
    (qq,) = _stage_fwd("ml_queries_fwd", pre=_pre_rms_only, post=_post_mla_queries, wsel=[0], splits=None,
                       tiles=[cq, cos, sin], eparams=[], sparams=q_s, weights=[w["ml_w_uq"]],
                       out_widths=[HEADS * HEAD_W], out_dtypes=[BF16], batch=batch, n_tiles=n_big,
                       n_lat_tiles=n_big, tm=tml)
    att, lse = hosting("attn_fwd", _attn_fwd, qq, kk, vv)
    x3 = fwd_out("ml_out_fwd", _pre_mla_out, [att, g2, x2], mods[2], [], "ml_w_out", n_big, tm=tml)

    w.update(plan.weights(LAYER_WEIGHTS[3]))
    ch_s = [w["ch_ln_g"], w["ch_ln_b"], w["ch_w_s"], w["ch_b_s"]]
    u3, v3, g3 = fwd_in("ch_in_fwd", x3, mods[3], 3, "ch_w_in", [e, e, e], n_big, tm=tml)
    x4 = fwd_out("ch_out_fwd", _pre_chunk_out, [u3, v3, g3, x3], mods[3], ch_s, "ch_w_out", n_big, tm=tml)

    loss_part, dy = _loss_kernel(x4, target)

    dmods = [None] * 4
    dnorm = [None] * 4
    big = {}
    (du, dv, dg), dgt, (g["ch_ln_g"], g["ch_ln_b"], g["ch_w_s"], g["ch_b_s"]), big["ch_w_out"] = bwd_out(
        "ch_out_bwd", _pre_chunk_out, [u3, v3, g3, x3], mods[3], ch_s, "ch_w_out", n_big, dy, tm=tml)
    plan.note({n: g[n] for n in ("ch_ln_g", "ch_ln_b", "ch_w_s", "ch_b_s")})
    dx3, dsh, dsc, dnorm[3], big["ch_w_in"] = bwd_in("ch_in_bwd", x3, mods[3], 3, "ch_w_in", [e, e, e], n_lat,
                                                     [du, dv, dg], [False] * 3, dy, False, stack=N_CHIP)
    dmods[3] = (dsh, dsc, dgt)
    plan.layer_grads(3, big)

    big = {}
    (datt, dg), dgt, _, big["ml_w_out"] = bwd_out("ml_out_bwd", _pre_mla_out, [att, g2, x2], mods[2], [],
                                                  "ml_w_out", n_big, dx3, tm=tml)
    dq, dk, dvv = hosting("attn_bwd", _attn_bwd, qq, kk, vv, att, lse, datt)
    (dcq,), _, (g["ml_q_norm"], dnope0, drope0), (big["ml_w_uq"],) = hosting(
        "ml_queries_bwd", _stage_bwd, "ml_queries_bwd", pre=_pre_rms_only, post=_post_mla_queries, wsel=[0],
        splits=None, tiles=[cq, cos, sin], tile_diff=[True, False, False], eparams=[], sparams=q_s,
        weights=[w["ml_w_uq"]], cots=[dq], cot_lat_only=[False], batch=batch, n_tiles=n_big, n_lat_tiles=n_big,
        tm=tml)
    (dckv, dkrp), _, (g["ml_kv_norm"], dnope1, drope1), (big["ml_w_ukv"],) = hosting(
        "ml_keys_bwd", _stage_bwd, "ml_keys_bwd", pre=_pre_rms_only, post=_post_mla_keys, wsel=[0], splits=None,
        tiles=[ckv, krp, cos, sin], tile_diff=[True, True, False, False], eparams=[], sparams=k_s,
        weights=[w["ml_w_ukv"]], cots=[dk, dvv], cot_lat_only=[False, False], batch=batch, n_tiles=n_all,
        n_lat_tiles=n_lat, w_col_stack=[N_CHIP])
    g["ml_nope_norm"] = jnp.concatenate([dnope0, dnope1], axis=0)
    g["ml_rope_norm"] = jnp.concatenate([drope0, drope1], axis=0)
    dx2, dsh, dsc, dnorm[2], big["ml_w_in"] = bwd_in("ml_in_bwd", x2, mods[2], 2, "ml_w_in", ml_widths, n_all,
                                                     [dckv, dkrp, dcq, dg], [False, False, True, True], dx3, True)
    dmods[2] = (dsh, dsc, dgt)
    plan.layer_grads(2, big)

    big = {}
    (dpooled, dg), dgt, (big["pl_w_grp"], g["pl_scale"]), big["pl_w_out"] = bwd_out(
        "pl_out_bwd", _pre_pool_out, [pooled, g1, x1], mods[1], pl_s, "pl_w_out", n_all, dx2)
    dv1 = _pool_bwd(dpooled, lat_len)
    dx1, dsh, dsc, dnorm[1], big["pl_w_in"] = bwd_in("pl_in_bwd", x1, mods[1], 1, "pl_w_in", [e, e], n_all,
                                                     [dv1, dg], [False] * 2, dx2, False, stack=N_CHIP)
    dmods[1] = (dsh, dsc, dgt)
    plan.layer_grads(1, big)

    big = {}
    (dc1, dg), dgt, (g["cv_ln_g"], g["cv_ln_b"]), big["cv_w_out"] = bwd_out(
        "cv_out_bwd", _pre_conv_out, [c1, g0, xm], mods[0], cv_s, "cv_w_out", n_all, dx1)
    da, db, g["cv_dw"], g["cv_db"] = hosting("conv_bwd", _conv_bwd, a0, b0, w["cv_dw"], dc1, lat_len)
    dx0, dsh, dsc, dnorm[0], big["cv_w_in"] = bwd_in("cv_in_bwd", xm, mods[0], 0, "cv_w_in", [e, e, e], n_all,
                                                     [da, db, dg], [False] * 3, dx1, False, stack=N_CHIP,
                                                     dx_lat_only=True)
    dmods[0] = (dsh, dsc, dgt)
    plan.layer_grads(0, big)
    g["norm_g"] = jnp.concatenate(dnorm, axis=0)
    return loss_part, dx0, dmods, g


N_DEV = 8
N_CHIP = 4
ANY = pl.BlockSpec(memory_space=pl.ANY)


def _my_place():
    return lax.axis_index("x"), lax.axis_index("y"), lax.axis_index("c")


def _flip(v, f):
    return 1 - v if f else v


def _ag8_copies(x):
    def plan(ins, outs, sems):
        mx, my, mc = _my_place()
        me = 4 * mx + 2 * my + mc
        sends, recvs = [], []
        for rel in range(1, N_DEV):
            peer = (_flip(mx, rel & 4), _flip(my, rel & 2), _flip(mc, rel & 1))
            src_dev = 4 * peer[0] + 2 * peer[1] + peer[2]
            sends.append(_remote(ins[0], outs[0].at[me], sems, rel - 1, peer))
            recvs.append(_remote(ins[0], outs[0].at[src_dev], sems, rel - 1, peer))
        return sends, recvs, [pltpu.make_async_copy(ins[0], outs[0].at[me], sems[2].at[0])]

    return _copies_hosted([x], [jax.ShapeDtypeStruct((N_DEV,) + x.shape, x.dtype)], (N_DEV - 1, N_DEV - 1, 1), plan)


def _ag8(name, x):
    return _run_hosted(name, _ag8_copies(x))[0]


def _ag8_column_copies(x, width):
    def plan(ins, outs, sems):
        mx, my, mc = _my_place()
        me = 4 * mx + 2 * my + mc
        sends, recvs = [], []
        for rel in range(1, N_DEV):
            peer = (_flip(mx, rel & 4), _flip(my, rel & 2), _flip(mc, rel & 1))
            src_dev = 4 * peer[0] + 2 * peer[1] + peer[2]
            cols = pl.ds(pl.multiple_of((2 * peer[0] + peer[1]) * width, LANES), width)
            sends.append(_remote(ins[0].at[:, cols], outs[0].at[me], sems, rel - 1, peer))
            recvs.append(_remote(ins[0].at[:, cols], outs[0].at[src_dev], sems, rel - 1, peer))
        mine = pl.ds(pl.multiple_of((2 * mx + my) * width, LANES), width)
        return sends, recvs, [pltpu.make_async_copy(ins[0].at[:, mine], outs[0].at[me], sems[2].at[0])]

    return _copies_hosted([x], [jax.ShapeDtypeStruct((N_DEV, x.shape[0], width), x.dtype)],
                          (N_DEV - 1, N_DEV - 1, 1), plan)


def _chip_rows_copies(x, rows_per_dev, shared_row):
    n_out = rows_per_dev + 1

    def plan(ins, outs, sems):
        mx, my, mc = _my_place()
        chip = 2 * mx + my
        sends, recvs = [], []

        def pieces(dev):
            return [(ins[0].at[pl.ds(dev * rows_per_dev, rows_per_dev)], slice(0, rows_per_dev)),
                    (ins[0].at[pl.ds(shared_row, 1)], slice(rows_per_dev, n_out))]

        for k, peer, pchip in _chip_peers(mx, my, mc):
            for t, (src, where) in enumerate(pieces(2 * pchip + mc)):
                sends.append(_remote(src, outs[0].at[chip, where], sems, 2 * k + t, peer))
                recvs.append(_remote(src, outs[0].at[pchip, where], sems, 2 * k + t, peer))
        locals_ = [pltpu.make_async_copy(src, outs[0].at[chip, where], sems[2].at[t])
                   for t, (src, where) in enumerate(pieces(2 * chip + mc))]
        return sends, recvs, locals_

    return _copies_hosted([x], [jax.ShapeDtypeStruct((N_CHIP, n_out) + x.shape[1:], x.dtype)], (6, 6, 2), plan)


def _chip_peers(mx, my, mc):
    out = []
    for rel in range(1, N_CHIP):
        px, py = _flip(mx, rel & 2), _flip(my, rel & 1)
        out.append((rel - 1, (px, py, mc), 2 * px + py))
    return out


def _half(mc, rows):
    return pl.ds(pl.multiple_of(mc * (rows // 2), 8), rows // 2)


def _copies_hosted(arrays, out_shapes, n_sems, plan, aliases=None):
    def start(ins, outs, sems):
        sends, _, locals_ = plan(ins, outs, sems)
        for cp in locals_ + sends:
            cp.start()

    def wait(ins, outs, sems):
        sends, recvs, locals_ = plan(ins, outs, sems)
        for cp in recvs:
            cp.wait_recv()
        for cp in sends:
            cp.wait_send()
        for cp in locals_:
            cp.wait()

    return _Hosted(arrays, out_shapes, [pltpu.SemaphoreType.DMA((k,)) for k in n_sems], start, wait, aliases)


def _remote(src, dst, sems, k, peer):
    return pltpu.make_async_remote_copy(src_ref=src, dst_ref=dst, send_sem=sems[0].at[k], recv_sem=sems[1].at[k],
                                        device_id=peer, device_id_type=MESH)


def _gather_ici(shards):
    n = len(shards)

    def plan(ins, outs, sems):
        mx, my, mc = _my_place()
        chip = 2 * mx + my
        sends, recvs, locals_ = [], [], []
        for a in range(n):
            rows = ins[a].shape[0]
            locals_.append(pltpu.make_async_copy(ins[a], outs[a].at[chip], sems[2].at[a]))
            for k, peer, pchip in _chip_peers(mx, my, mc):
                src = ins[a].at[_half(mc, rows)]
                sends.append(_remote(src, outs[a].at[chip, _half(mc, rows)], sems, 3 * a + k, peer))
                recvs.append(_remote(src, outs[a].at[pchip, _half(mc, rows)], sems, 3 * a + k, peer))
        return sends, recvs, locals_

    return _copies_hosted(shards, [jax.ShapeDtypeStruct((N_CHIP,) + s.shape, s.dtype) for s in shards],
                          (3 * n, 3 * n, n), plan)


def _sibling_fill(arrays, row_axis, chips_only_other):
    n = len(arrays)
    per = 3 if chips_only_other else 1

    def plan(ins, outs, sems):
        mx, my, mc = _my_place()
        sibling = (mx, my, 1 - mc)

        def views(a, core):
            rows = outs[a].shape[row_axis]
            if chips_only_other:
                return [outs[a].at[pchip, _half(core, rows)] for _, _, pchip in _chip_peers(mx, my, mc)]
            return [outs[a].at[_half(core, rows)]]

        sends, recvs = [], []
        for a in range(n):
            for k, v in enumerate(views(a, mc)):
                sends.append(_remote(v, v, sems, per * a + k, sibling))
            for k, v in enumerate(views(a, 1 - mc)):
                recvs.append(_remote(v, v, sems, per * a + k, sibling))
        return sends, recvs, []

    return _copies_hosted(arrays, [jax.ShapeDtypeStruct(s.shape, s.dtype) for s in arrays], (per * n, per * n), plan,
                          aliases={a: a for a in range(n)})


def _grad_swap_d2d(stacks):
    n = len(stacks)

    def plan(ins, outs, sems):
        mx, my, mc = _my_place()
        sibling = (mx, my, 1 - mc)
        sends = [_remote(ins[a].at[:, _half(1 - mc, ins[a].shape[1])], outs[a], sems, a, sibling) for a in range(n)]
        return sends, sends, []

    return _copies_hosted(stacks, [jax.ShapeDtypeStruct((N_CHIP, s.shape[1] // 2, s.shape[2]), s.dtype)
                                   for s in stacks], (n, n), plan)


def _grad_exchange_ici(parts):
    n = len(parts)

    def plan(ins, outs, sems):
        mx, my, mc = _my_place()
        chip = 2 * mx + my
        sends, recvs, locals_ = [], [], []
        for a in range(n):
            locals_.append(pltpu.make_async_copy(ins[a].at[chip], outs[a].at[chip], sems[2].at[a]))
            for k, peer, pchip in _chip_peers(mx, my, mc):
                sends.append(_remote(ins[a].at[pchip], outs[a].at[chip], sems, 3 * a + k, peer))
                recvs.append(_remote(ins[a].at[pchip], outs[a].at[pchip], sems, 3 * a + k, peer))
        return sends, recvs, locals_

    return _copies_hosted(parts, [jax.ShapeDtypeStruct(s.shape, s.dtype) for s in parts], (3 * n, 3 * n, n), plan)


def _row_block(rows, limit=256):
    for t in range(min(rows, limit), 7, -8):
        if rows % t == 0 and t % 8 == 0:
            return t
    return rows


def _grad_add_half(core, stack, received):
    _, rows, cw = stack.shape
    rh = rows // 2
    tr = _row_block(rh)

    def body(s_ref, a_ref, b_ref, o_ref):
        o_ref[...] = (a_ref[...] + b_ref[...]).astype(o_ref.dtype)

    grid_spec = pltpu.PrefetchScalarGridSpec(
        num_scalar_prefetch=1, grid=(rh // tr,),
        in_specs=[pl.BlockSpec((N_CHIP, tr, cw), lambda i, s: (0, s[0] * (rh // tr) + i, 0)),
                  pl.BlockSpec((N_CHIP, tr, cw), lambda i, s: (0, i, 0))],
        out_specs=pl.BlockSpec((N_CHIP, tr, cw), lambda i, s: (0, i, 0)))
    return pl.pallas_call(
        body, name="grad_add_half", grid_spec=grid_spec, out_shape=jax.ShapeDtypeStruct(received.shape, BF16),
        compiler_params=pltpu.CompilerParams(dimension_semantics=("arbitrary",), vmem_limit_bytes=VMEM_LIMIT),
    )(core, stack, received)


def _adamw(name, row_off, parts, w, m, v, rows, hosted=None):
    n, _, cw = parts.shape
    tr = _row_block(rows, 128)

    def update(p_ref, w_ref, m_ref, v_ref, g_ref, d_ref, nm_ref, nv_ref):
        g = p_ref[0].astype(F32)
        for k in range(1, n):
            g = g + p_ref[k].astype(F32)
        nm = ADAM_B1 * m_ref[...] + (1.0 - ADAM_B1) * g
        nv = ADAM_B2 * v_ref[...] + (1.0 - ADAM_B2) * (g * g)
        m_hat = nm / (1.0 - ADAM_B1 ** ADAM_STEP)
        v_hat = nv / (1.0 - ADAM_B2 ** ADAM_STEP)
        g_ref[...] = g
        d_ref[...] = -ADAM_LR * (m_hat / (jnp.sqrt(v_hat) + ADAM_EPS) + ADAM_WD * w_ref[...])
        nm_ref[...] = nm
        nv_ref[...] = nv

    out_shape = [jax.ShapeDtypeStruct(w.shape, F32)] * 4
    if row_off is None:
        blk = pl.BlockSpec((tr, cw), lambda i: (i, 0))
        return _pcall(update, name=name, grid=(rows // tr,), out_specs=[blk] * 4, out_shape=out_shape,
                      in_specs=[pl.BlockSpec((n, tr, cw), lambda i: (0, i, 0)), blk, blk, blk],
                      args=[parts, w, m, v], hosted=hosted)

    def body(s_ref, *refs):
        update(*refs)

    full = pl.BlockSpec((tr, cw), lambda i, s: (s[0] // tr + i, 0))
    grid_spec = pltpu.PrefetchScalarGridSpec(
        num_scalar_prefetch=1, grid=(rows // tr,),
        in_specs=[pl.BlockSpec((n, tr, cw), lambda i, s: (0, i, 0)), full, full, full],
        out_specs=[full, full, full, full])
    return pl.pallas_call(
        body, name=name, grid_spec=grid_spec, out_shape=out_shape,
        compiler_params=pltpu.CompilerParams(dimension_semantics=("arbitrary",), vmem_limit_bytes=VMEM_LIMIT),
    )(row_off, parts, w, m, v)


def _sum8(x):
    _, r, cw = x.shape
    tr = _row_block(r, 64)

    def body(x_ref, o_ref):
        acc = x_ref[0]
        for k in range(1, N_DEV):
            acc = acc + x_ref[k]
        o_ref[...] = acc

    return pl.pallas_call(
        body, name="sum8", grid=(r // tr,), in_specs=[pl.BlockSpec((N_DEV, tr, cw), lambda i: (0, i, 0))],
        out_specs=pl.BlockSpec((tr, cw), lambda i: (i, 0)), out_shape=jax.ShapeDtypeStruct((r, cw), F32),
        compiler_params=pltpu.CompilerParams(dimension_semantics=("arbitrary",)),
    )(x)


MOD_ROWS = 24
CTX_ROW = 16


def _mod_fwd(c_rows, w_mod, b_mod, hosted=None):
    nl, d, nn = w_mod.shape

    def body(c_ref, w_ref, b_ref, o_ref):
        o_ref[0] = _dot(_silu(c_ref[...]), w_ref[0]) + b_ref[0]

    return _pcall(
        body, name="mod_fwd", grid=(nl,),
        in_specs=[pl.BlockSpec((MOD_ROWS, d), lambda i: (0, 0)), pl.BlockSpec((1, d, nn), lambda i: (i, 0, 0)),
                  pl.BlockSpec((1, 1, nn), lambda i: (i, 0, 0))],
        out_specs=[pl.BlockSpec((1, MOD_ROWS, nn), lambda i: (i, 0, 0))],
        out_shape=[jax.ShapeDtypeStruct((nl, MOD_ROWS, nn), F32)], args=[c_rows, w_mod, b_mod], hosted=hosted)[0]


def _mod_bwd_rows(dlat, dctx_parts):
    nl, ne, nn = dlat.shape

    def body(l_ref, c_ref, db_ref, dc_ref):
        dc = c_ref[0, 0:1, :]
        for k in range(1, N_DEV):
            dc = dc + c_ref[0, k:k + 1, :]
        db = dc
        for k in range(ne):
            db = db + l_ref[0, k:k + 1, :]
        db_ref[0] = db
        dc_ref[0] = dc

    return pl.pallas_call(
        body, name="mod_bwd_rows", grid=(nl,),
        in_specs=[pl.BlockSpec((1, ne, nn), lambda i: (i, 0, 0)), pl.BlockSpec((1, N_DEV, nn), lambda i: (i, 0, 0))],
        out_specs=[pl.BlockSpec((1, 1, nn), lambda i: (i, 0, 0))] * 2,
        out_shape=[jax.ShapeDtypeStruct((nl, 1, nn), F32)] * 2,
        compiler_params=pltpu.CompilerParams(dimension_semantics=("arbitrary",)),
    )(dlat, dctx_parts)


def _mod_bwd_w(c_cols, d_rows, w_mod, hosted=None):
    nl, d, nn = w_mod.shape

    def body(c_ref, d_ref, w_ref, dw_ref, dc_ref):
        i = pl.program_id(0)
        c = c_ref[...]
        sg = jax.nn.sigmoid(c)
        s = c * sg
        dv = d_ref[0]
        acc = s[:, 0:1] * dv[0:1, :]
        for r in range(1, CTX_ROW + 1):
            acc = acc + s[:, r:r + 1] * dv[r:r + 1, :]
        dw_ref[0] = acc
        ds_ctx = jnp.sum(w_ref[0] * dv[CTX_ROW:CTX_ROW + 1, :], axis=1, keepdims=True)
        cc, sc = c[:, CTX_ROW:CTX_ROW + 1], sg[:, CTX_ROW:CTX_ROW + 1]
        part = ds_ctx * (sc * (1.0 + cc * (1.0 - sc)))

        @pl.when(i == 0)
        def _():
            dc_ref[...] = part

        @pl.when(i != 0)
        def _():
            dc_ref[...] += part

    return _pcall(
        body, name="mod_bwd_w", grid=(nl,),
        in_specs=[pl.BlockSpec((d, MOD_ROWS), lambda i: (0, 0)), pl.BlockSpec((1, MOD_ROWS, nn), lambda i: (i, 0, 0)),
                  pl.BlockSpec((1, d, nn), lambda i: (i, 0, 0))],
        out_specs=[pl.BlockSpec((1, d, nn), lambda i: (i, 0, 0)), pl.BlockSpec((d, 1), lambda i: (0, 0))],
        out_shape=[jax.ShapeDtypeStruct((nl, d, nn), F32), jax.ShapeDtypeStruct((d, 1), F32)],
        args=[c_cols, d_rows, w_mod], hosted=hosted)


def _pack_rows(arrays, width, row_multiple=8):
    rows, spans, r0 = [], [], 0
    for a in arrays:
        flat = a.reshape(-1)
        nr = -(-flat.shape[0] // width)
        held = -(-nr // 8) * 8
        flat = jnp.pad(flat, (0, held * width - flat.shape[0]))
        rows.append(flat.reshape(held, width))
        spans.append((r0, nr, a.shape))
        r0 += held
    if r0 % row_multiple:
        rows.append(jnp.zeros((row_multiple - r0 % row_multiple, width), F32))
    return jnp.concatenate(rows, axis=0), spans


def _unpack_rows(packed, spans):
    out = []
    for r0, nr, shape in spans:
        out.append(packed[r0:r0 + nr].reshape(-1)[:math.prod(shape)].reshape(shape))
    return out


BIG = {"cv_w_in": 1, "cv_w_out": 0, "pl_w_in": 1, "pl_w_grp": None, "pl_w_out": 0, "ml_w_in": 1, "ml_w_uq": 1,
       "ml_w_ukv": 1, "ml_w_out": 0, "ch_w_in": 1, "ch_w_out": 0}
SMALL_SHARDED = ["cv_dw", "pl_scale", "ml_q_norm", "ml_kv_norm", "ch_ln_g", "ch_ln_b"]
SMALL_REPLICATED = ["c_ctx", "norm_g", "b_mod", "cv_db", "cv_ln_g", "cv_ln_b", "ml_nope_norm", "ml_rope_norm",
                    "ch_w_s", "ch_b_s"]
WEIGHTS = ['c_ctx', 'norm_g', 'w_mod', 'b_mod', 'cv_w_in', 'cv_dw', 'cv_db', 'cv_ln_g', 'cv_ln_b', 'cv_w_out',
           'pl_w_in', 'pl_w_grp', 'pl_scale', 'pl_w_out', 'ml_w_in', 'ml_q_norm', 'ml_kv_norm', 'ml_w_uq', 'ml_w_ukv',
           'ml_nope_norm', 'ml_rope_norm', 'ml_w_out', 'ch_w_in', 'ch_ln_g', 'ch_ln_b', 'ch_w_s', 'ch_b_s', 'ch_w_out']


def _shard2d(name, a):
    if name == "pl_w_grp":
        return a.reshape(a.shape[-3] * a.shape[-2], a.shape[-1])
    return a.reshape(a.shape[-2], a.shape[-1])


def _unstack(name, s):
    if name == "pl_w_grp":
        ng = len(POOL_WINDOWS)
        return s.reshape(N_CHIP, ng, s.shape[1] // ng, s.shape[2]).transpose(1, 0, 2, 3).reshape(ng, -1, s.shape[2])
    if BIG[name] == 0:
        return s.reshape(-1, s.shape[2])
    return s.transpose(1, 0, 2).reshape(s.shape[1], -1)


def _stack(name, g):
    if g.ndim == 3 and name != "pl_w_grp":
        return g
    if name == "pl_w_grp":
        ng = len(POOL_WINDOWS)
        return g.reshape(ng, N_CHIP, -1, g.shape[2]).transpose(1, 0, 2, 3).reshape(N_CHIP, -1, g.shape[2])
    if BIG[name] == 0:
        return g.reshape(N_CHIP, -1, g.shape[1])
    return g.reshape(g.shape[0], N_CHIP, -1).transpose(1, 0, 2)


L0, L1, L2, L3 = LAYER_WEIGHTS
EARLY_SMALL = ("ch_w_s", "ch_b_s", "ch_ln_g", "ch_ln_b")
MESH_SCHEDULE = {
    "ag8_inputs": [("gather", L0[:1])], "mod_fwd": [("gfill", L0[:1])],
    "cv_in_fwd": [("gather", L0[1:]), ("gather", L1[:1])], "conv_fwd": [("gfill", L0[1:]), ("gather", L1[1:])],
    "cv_out_fwd": [("gfill", L1)],
    "pl_in_fwd": [("gather", L2[:1])], "pool_fwd": [("gather", L2[1:])], "pl_out_fwd": [("gfill", L2)],
    "attn_fwd": [("gather", L3)], "ml_out_fwd": [("gfill", L3)],
    "ch_in_bwd": [("small", EARLY_SMALL)],
    "ml_out_bwd": [("swap", L3)], "attn_bwd": [("exch", L3)], "ml_queries_bwd": [("ofill", L3)],
    "pl_out_bwd": [("swap", L2)], "pl_in_bwd": [("exch", L2)],
    "cv_out_bwd": [("swap", L1), ("ofill", L2)], "conv_bwd": [("exch", L1)],
    "ag8_dmod": [("swap", L0)], "mod_bwd_w": [("exch", L0), ("ofill", L1)], "ag8_small_grads": [("ofill", L0)],
}


class _MeshPlan:
    def __init__(self, weights, m, v, core):
        self.W, self.M, self.V, self.core = weights, m, v, core
        self.small = None
        self.stack, self.gstack, self.part, self.half, self.out = {}, {}, {}, {}, {}
        self.notes, self.early = {}, {}
        self.live, self.done = {}, set()

    def _make(self, op, names):
        if op == "gather":
            return _gather_ici([_shard2d(n, self.W[n]).astype(BF16) for n in names])
        if op == "gfill":
            return _sibling_fill([self.stack[n] for n in names], 1, True)
        if op == "swap":
            return _grad_swap_d2d([self.gstack[n] for n in names])
        if op == "exch":
            return _grad_exchange_ici([self.part[n] for n in names])
        if op == "ofill":
            return _sibling_fill([t for n in names for t in self.half[n]], 0, False)
        pack, self.early_spans = _pack_rows([self.notes[n] for n in names], LANES, 128)
        return _ag8_copies(pack)

    def _finish_op(self, op, names, hosted):
        self.done.add((op, names))
        res = hosted.results
        if op in ("gather", "gfill"):
            self.stack.update(zip(names, res))
        elif op == "swap":
            for n, r in zip(names, res):
                self.part[n] = _grad_add_half(self.core, self.gstack[n], r)
        elif op == "exch":
            for n, q in zip(names, res):
                rh = q.shape[1]
                self.half[n] = _adamw("adamw_" + n, self.core * rh, q, _shard2d(n, self.W[n]),
                                      _shard2d(n, self.M[n]), _shard2d(n, self.V[n]), rh)
        elif op == "ofill":
            for k, n in enumerate(names):
                self.out[n] = tuple(r.reshape(self.W[n].shape) for r in res[4 * k:4 * k + 4])
        else:
            self.early.update(zip(names, _unpack_rows(_sum8(res[0]), self.early_spans)))

    def alone(self, op, names):
        hosted = self._make(op, names)
        _run_hosted("%s_%s" % (op, names[0]), hosted)
        self._finish_op(op, names, hosted)

    def weights(self, names):
        wk = {n: _unstack(n, self.stack[n]) for n in names}
        if "pl_w_grp" in wk:
            wk["pl_w_grp"] = wk["pl_w_grp"].astype(F32)
        return _prep_weights(wk)

    def hosted(self, tag):
        self.live[tag] = [(op, names, self._make(op, names)) for op, names in MESH_SCHEDULE.get(tag, [])]
        return _merge_hosted([h for _, _, h in self.live[tag]])

    def after(self, tag):
        for op, names, hosted in self.live.pop(tag, []):
            self._finish_op(op, names, hosted)

    def note(self, values):
        self.notes.update(values)

    def layer_grads(self, layer, grads):
        g = _unprep_grads(grads)
        for n in LAYER_WEIGHTS[layer]:
            self.gstack[n] = _stack(n, g[n])

    def finish(self):
        for names in (L3, L2, L1, L0):
            for op in ("swap", "exch", "ofill"):
                if (op, names) not in self.done:
                    self.alone(op, names)
        return self.out


def kernel(x, c, ctx, c_ctx, norm_g, w_mod, b_mod, cv_w_in, cv_dw, cv_db, cv_ln_g, cv_ln_b, cv_w_out, pl_w_in, pl_w_grp, pl_scale, pl_w_out, ml_w_in, ml_q_norm, ml_kv_norm, ml_w_uq, ml_w_ukv, ml_nope_norm, ml_rope_norm, ml_w_out, ch_w_in, ch_ln_g, ch_ln_b, ch_w_s, ch_b_s, ch_w_out, loss_target, m_c_ctx, m_norm_g, m_w_mod, m_b_mod, m_cv_w_in, m_cv_dw, m_cv_db, m_cv_ln_g, m_cv_ln_b, m_cv_w_out, m_pl_w_in, m_pl_w_grp, m_pl_scale, m_pl_w_out, m_ml_w_in, m_ml_q_norm, m_ml_kv_norm, m_ml_w_uq, m_ml_w_ukv, m_ml_nope_norm, m_ml_rope_norm, m_ml_w_out, m_ch_w_in, m_ch_ln_g, m_ch_ln_b, m_ch_w_s, m_ch_b_s, m_ch_w_out, v_c_ctx, v_norm_g, v_w_mod, v_b_mod, v_cv_w_in, v_cv_dw, v_cv_db, v_cv_ln_g, v_cv_ln_b, v_cv_w_out, v_pl_w_in, v_pl_w_grp, v_pl_scale, v_pl_w_out, v_ml_w_in, v_ml_q_norm, v_ml_kv_norm, v_ml_w_uq, v_ml_w_ukv, v_ml_nope_norm, v_ml_rope_norm, v_ml_w_out, v_ch_w_in, v_ch_ln_g, v_ch_ln_b, v_ch_w_s, v_ch_b_s, v_ch_w_out):
    W = dict(c_ctx=c_ctx, norm_g=norm_g, w_mod=w_mod, b_mod=b_mod, cv_w_in=cv_w_in, cv_dw=cv_dw, cv_db=cv_db, cv_ln_g=cv_ln_g, cv_ln_b=cv_ln_b, cv_w_out=cv_w_out, pl_w_in=pl_w_in, pl_w_grp=pl_w_grp, pl_scale=pl_scale, pl_w_out=pl_w_out, ml_w_in=ml_w_in, ml_q_norm=ml_q_norm, ml_kv_norm=ml_kv_norm, ml_w_uq=ml_w_uq, ml_w_ukv=ml_w_ukv, ml_nope_norm=ml_nope_norm, ml_rope_norm=ml_rope_norm, ml_w_out=ml_w_out, ch_w_in=ch_w_in, ch_ln_g=ch_ln_g, ch_ln_b=ch_ln_b, ch_w_s=ch_w_s, ch_b_s=ch_b_s, ch_w_out=ch_w_out)
    M = dict(c_ctx=m_c_ctx, norm_g=m_norm_g, w_mod=m_w_mod, b_mod=m_b_mod, cv_w_in=m_cv_w_in, cv_dw=m_cv_dw, cv_db=m_cv_db, cv_ln_g=m_cv_ln_g, cv_ln_b=m_cv_ln_b, cv_w_out=m_cv_w_out, pl_w_in=m_pl_w_in, pl_w_grp=m_pl_w_grp, pl_scale=m_pl_scale, pl_w_out=m_pl_w_out, ml_w_in=m_ml_w_in, ml_q_norm=m_ml_q_norm, ml_kv_norm=m_ml_kv_norm, ml_w_uq=m_ml_w_uq, ml_w_ukv=m_ml_w_ukv, ml_nope_norm=m_ml_nope_norm, ml_rope_norm=m_ml_rope_norm, ml_w_out=m_ml_w_out, ch_w_in=m_ch_w_in, ch_ln_g=m_ch_ln_g, ch_ln_b=m_ch_ln_b, ch_w_s=m_ch_w_s, ch_b_s=m_ch_b_s, ch_w_out=m_ch_w_out)
    V = dict(c_ctx=v_c_ctx, norm_g=v_norm_g, w_mod=v_w_mod, b_mod=v_b_mod, cv_w_in=v_cv_w_in, cv_dw=v_cv_dw, cv_db=v_cv_db, cv_ln_g=v_cv_ln_g, cv_ln_b=v_cv_ln_b, cv_w_out=v_cv_w_out, pl_w_in=v_pl_w_in, pl_w_grp=v_pl_w_grp, pl_scale=v_pl_scale, pl_w_out=v_pl_w_out, ml_w_in=v_ml_w_in, ml_q_norm=v_ml_q_norm, ml_kv_norm=v_ml_kv_norm, ml_w_uq=v_ml_w_uq, ml_w_ukv=v_ml_w_ukv, ml_nope_norm=v_ml_nope_norm, ml_rope_norm=v_ml_rope_norm, ml_w_out=v_ml_w_out, ch_w_in=v_ch_w_in, ch_ln_g=v_ch_ln_g, ch_ln_b=v_ch_ln_b, ch_w_s=v_ch_w_s, ch_b_s=v_ch_b_s, ch_w_out=v_ch_w_out)

    batch, lat_len, d = x.shape
    mx, my, mc = _my_place()
    chip = 2 * mx + my
    dev = 2 * chip + mc
    core = jnp.reshape(mc, (1,)).astype(jnp.int32)
    zero_off = jnp.zeros((1,), jnp.int32)
    big_names = list(BIG)

    sw = d // N_CHIP
    small_in = [c] + [jnp.pad(W[n].reshape(-1, W[n].shape[-1]), ((0, 0), (0, sw - W[n].shape[-1])))
                      for n in SMALL_SHARDED]
    pack1, spans1 = _pack_rows(small_in, sw)
    plan = _MeshPlan(W, M, V, core)
    gather1 = _ag8_copies(pack1)
    _run_hosted("ag8_inputs", _merge_hosted([gather1, plan.hosted("ag8_inputs")]))
    plan.after("ag8_inputs")
    got1 = gather1.results[0]
    c_all = got1[:, spans1[0][0]:spans1[0][0] + spans1[0][1]].reshape(N_DEV * batch, d)
    full_small = {}
    for n, (r0, nr, _) in zip(SMALL_SHARDED, spans1[1:]):
        blk = got1[0::2, r0:r0 + nr, :W[n].shape[-1]]
        full_small[n] = blk.transpose(1, 0, 2).reshape(nr, -1)

    c_rows = jnp.concatenate([c_all, c_ctx[None], jnp.zeros((MOD_ROWS - CTX_ROW - 1, d), F32)], axis=0)
    nmod = w_mod.shape[2]
    b_shard = lax.dynamic_slice(b_mod, (0, chip * nmod), (b_mod.shape[0], nmod))[:, None, :]
    mod_shard = _mod_fwd(c_rows, w_mod, b_shard, hosted=plan.hosted("mod_fwd"))
    plan.after("mod_fwd")
    mod_rows = mod_shard.transpose(1, 0, 2).reshape(MOD_ROWS, 1, 4 * nmod)
    got2 = _run_hosted("mod_exchange", _chip_rows_copies(mod_rows, batch, CTX_ROW))[0]
    mod_mine = got2.reshape(N_CHIP, batch + 1, 4, nmod).transpose(2, 1, 0, 3).reshape(4, batch + 1, 3 * d)
    mod_lat, mod_ctx = mod_mine[:, :batch], mod_mine[:, batch]
    mods = []
    for i in range(4):
        mods.append(tuple(
            jnp.stack([mod_lat[i, :, j * d:(j + 1) * d], jnp.broadcast_to(mod_ctx[i, j * d:(j + 1) * d], (batch, d))],
                      axis=1)[:, :, None, :] for j in range(3)))

    wk = dict(full_small)
    wk.update(norm_g=norm_g, cv_db=cv_db, cv_ln_g=cv_ln_g, cv_ln_b=cv_ln_b, ml_nope_norm=ml_nope_norm[0],
              ml_rope_norm=ml_rope_norm[0], ch_w_s=ch_w_s[0], ch_b_s=ch_b_s[0])
    plan.small = _prep_weights(wk)
    xm = jnp.concatenate([x, ctx], axis=1)
    loss_part, grad_x, dmods, g = _local_step(xm, loss_target, mods, plan, lat_len)
    g = _unprep_grads(g)

    lat_rows, ctx_rows = [], []
    for i in range(4):
        dsh, dsc, dgt = dmods[i]
        lat_rows.append(jnp.concatenate([dsh[:, 0, 0], dsc[:, 0, 0], dgt[:, 0, 0]], axis=1))
        zero = jnp.zeros((d,), F32)
        cs = [jnp.sum(t[:, 1, 0], axis=0) if ok else zero
              for t, ok in zip((dsh, dsc, dgt), (i <= 2, i <= 2, i <= 1))]
        ctx_rows.append(jnp.concatenate(cs, axis=0)[None])
    dmod_dev = jnp.concatenate(lat_rows + ctx_rows, axis=0)
    dmod_dev = jnp.pad(dmod_dev, ((0, (-dmod_dev.shape[0]) % 8), (0, 0)))
    gather3 = _ag8_column_copies(dmod_dev, nmod)
    _run_hosted("ag8_dmod", _merge_hosted([gather3, plan.hosted("ag8_dmod")]))
    plan.after("ag8_dmod")
    got3 = gather3.results[0]
    dlat = got3[:, :4 * batch].reshape(N_DEV, 4, batch, nmod).transpose(1, 0, 2, 3).reshape(4, N_DEV * batch, nmod)
    dctx_parts = got3[:, 4 * batch:4 * batch + 4].transpose(1, 0, 2)
    g_b_shard, dctx = _mod_bwd_rows(dlat, dctx_parts)
    d_rows = jnp.concatenate([dlat, dctx, jnp.zeros((4, MOD_ROWS - CTX_ROW - 1, nmod), F32)], axis=1)
    g_w_mod, dcc_part = _mod_bwd_w(c_rows.T, d_rows, w_mod, hosted=plan.hosted("mod_bwd_w"))
    plan.after("mod_bwd_w")

    wm2 = w_mod.reshape(-1, nmod)
    res_mod = _adamw("adamw_w_mod", None, g_w_mod.reshape(1, -1, nmod), wm2, M["w_mod"].reshape(-1, nmod),
                     V["w_mod"].reshape(-1, nmod), wm2.shape[0], hosted=plan.hosted("adamw_w_mod"))
    plan.after("adamw_w_mod")
    out = {"w_mod": tuple(r.reshape(w_mod.shape) for r in res_mod)}

    g_small_in = {n: g[n] for n in SMALL_SHARDED if n not in EARLY_SMALL}
    g_small_in.update(norm_g=g["norm_g"], cv_db=g["cv_db"], cv_ln_g=g["cv_ln_g"], cv_ln_b=g["cv_ln_b"],
                      ml_nope_norm=g["ml_nope_norm"], ml_rope_norm=g["ml_rope_norm"],
                      c_ctx=dcc_part.reshape(-1) * (mc == 0).astype(F32), loss=loss_part,
                      b_mod=lax.dynamic_update_slice(jnp.zeros((N_CHIP, 4, nmod), F32),
                                                     g_b_shard[None, :, 0] * (mc == 0).astype(F32), (chip, 0, 0)))
    small_names = list(g_small_in)
    pack4, spans4 = _pack_rows([g_small_in[n] for n in small_names], LANES, 128)
    gather4 = _ag8_copies(pack4)
    _run_hosted("ag8_small_grads", _merge_hosted([gather4, plan.hosted("ag8_small_grads")]))
    plan.after("ag8_small_grads")
    gs = dict(zip(small_names, _unpack_rows(_sum8(gather4.results[0]), spans4)))
    loss = gs["loss"][0, 0]
    gs.update(plan.early)
    gs["b_mod"] = gs["b_mod"].transpose(1, 0, 2).reshape(4, N_CHIP * nmod)
    for n in SMALL_SHARDED:
        wd = W[n].shape[-1]
        gs[n] = lax.dynamic_slice_in_dim(gs[n], chip * wd, wd, axis=1)
    upd_names = SMALL_REPLICATED + SMALL_SHARDED
    pw, spans_u = _pack_rows([W[n] for n in upd_names], LANES, 128)
    pm, _ = _pack_rows([M[n] for n in upd_names], LANES, 128)
    pv, _ = _pack_rows([V[n] for n in upd_names], LANES, 128)
    pg, _ = _pack_rows([gs[n].reshape(W[n].shape) for n in upd_names], LANES, 128)
    res_small = _adamw("adamw_small", None, pg[None], pw, pm, pv, pw.shape[0], hosted=plan.hosted("adamw_small"))
    plan.after("adamw_small")
    for n, vals in zip(upd_names, zip(*[_unpack_rows(r, spans_u) for r in res_small])):
        out[n] = vals
    out.update(plan.finish())

    outs = [loss, grad_x]
    for j in range(4):
        outs.extend(out[n][j] for n in WEIGHTS)
    return tuple(outs)
```

```python
import functools
import math

import jax
import jax.numpy as jnp
from jax import lax
from jax.experimental import pallas as pl
from jax.experimental.pallas import tpu as pltpu

F32 = jnp.float32
BF16 = jnp.bfloat16
ACT = jnp.float32
MESH = pl.DeviceIdType.MESH

EPS = 1e-6
GRID_W = 64
CONV_WIDTH = 31
CONV_HALF = CONV_WIDTH // 2
CONV_PAD = 16
POOL_WINDOWS = (2, 4, 8, 16)
POOL_HALF = max(POOL_WINDOWS) // 2
HEADS = 8
NOPE = 128
ROPE = 64
HEAD_W = 256
VDIM = 128
KV_RANK = 256
Q_RANK = 384
ATT_SCALE = (NOPE + ROPE) ** -0.5
LN2 = math.log(2.0)
Q_PRESCALE = ATT_SCALE / LN2
ROPE_THETA = 10000.0
CHUNK = 128
CHUNK_GROUPS = 8
LANES = 128
TM = 256
TM_LATENT = 512
TQ = 1024
TQ_BWD = 2048
ATT_RQ = 128
ATT_KC = 256
VMEM_LIMIT = 60 * 1024 * 1024

ADAM_LR = 0.001
ADAM_B1 = 0.9
ADAM_B2 = 0.999
ADAM_EPS = 1e-08
ADAM_WD = 0.01
ADAM_STEP = 10


def _dot(a, b):
    return jnp.dot(a.astype(BF16), b.astype(BF16), preferred_element_type=F32)


def _dot_nt(a, b):
    return lax.dot_general(a.astype(BF16), b.astype(BF16), (((1,), (1,)), ((), ())), preferred_element_type=F32)


def _dot_tn(a, b):
    return lax.dot_general(a.astype(BF16), b.astype(BF16), (((0,), (0,)), ((), ())), preferred_element_type=F32)


@jax.custom_vjp
def _mm(a, w):
    return _dot(a, w)


def _mm_fwd(a, w):
    return _dot(a, w), (a, w)


def _mm_bwd(res, ct):
    a, w = res
    return _dot_nt(ct, w), _dot_tn(a, ct)


_mm.defvjp(_mm_fwd, _mm_bwd)


def _swap16_impl(x):
    n = x.shape[-1]
    ax = x.ndim - 1
    lane = lax.broadcasted_iota(jnp.int32, x.shape, ax)
    up = pltpu.roll(x, n - 16, ax)
    dn = pltpu.roll(x, 16, ax)
    return jnp.where((lane % 32) < 16, up, dn)


@jax.custom_vjp
def _swap16(x):
    return _swap16_impl(x)


_swap16.defvjp(lambda x: (_swap16_impl(x), None), lambda _, ct: (_swap16_impl(ct),))


def _rms(x, g, n=None):
    n = x.shape[-1] if n is None else n
    return x * lax.rsqrt(jnp.sum(x * x, axis=-1, keepdims=True) * (1.0 / n) + EPS) * g


def _layernorm(x, g, b):
    mu = jnp.mean(x, axis=-1, keepdims=True)
    xc = x - mu
    var = jnp.mean(xc * xc, axis=-1, keepdims=True)
    return xc * lax.rsqrt(var + EPS) * g + b


def _silu(x):
    return x * jax.nn.sigmoid(x)


def _rope(x, cos, sin):
    return x * cos + _swap16(x) * sin


ANY = pl.BlockSpec(memory_space=pl.ANY)


class _Hosted:
    def __init__(self, arrays, out_shapes, sems, start, wait, aliases=None):
        self.arrays, self.out_shapes, self.sems = list(arrays), list(out_shapes), list(sems)
        self.start, self.wait, self.aliases = start, wait, dict(aliases or {})
        self.results = None


def _merge_hosted(parts):
    parts = [p for p in parts if p is not None]
    if not parts:
        return None
    if len(parts) == 1:
        return parts[0]
    offs, a0, o0, s0 = [], 0, 0, 0
    for p in parts:
        offs.append((a0, o0, s0))
        a0, o0, s0 = a0 + len(p.arrays), o0 + len(p.out_shapes), s0 + len(p.sems)

    def run(which):
        def f(ins, outs, sems):
            for p, (a, o, s) in zip(parts, offs):
                getattr(p, which)(ins[a:a + len(p.arrays)], outs[o:o + len(p.out_shapes)], sems[s:s + len(p.sems)])
        return f

    aliases = {}
    for p, (a, o, _) in zip(parts, offs):
        aliases.update({a + i: o + j for i, j in p.aliases.items()})
    merged = _Hosted(sum((p.arrays for p in parts), []), sum((p.out_shapes for p in parts), []),
                     sum((p.sems for p in parts), []), run("start"), run("wait"), aliases)
    merged.parts, merged.offs = parts, offs
    return merged


def _deliver(hosted, results):
    hosted.results = list(results)
    for p, (_, o, _) in zip(getattr(hosted, "parts", []), getattr(hosted, "offs", [])):
        p.results = list(results[o:o + len(p.out_shapes)])


def _pcall(body, *, name, grid, in_specs, out_specs, out_shape, args, hosted=None, vmem_limit=True, scratch=()):
    n_in, n_out, n_scr = len(args), len(out_shape), len(scratch)
    kwargs = dict(scratch_shapes=list(scratch)) if scratch else {}
    if hosted is not None:
        nhi, nho, inner = len(hosted.arrays), len(hosted.out_shapes), body

        def body(*refs):
            ins, hin = refs[:n_in], refs[n_in:n_in + nhi]
            outs, hout = refs[n_in + nhi:n_in + nhi + n_out], refs[n_in + nhi + n_out:n_in + nhi + n_out + nho]
            own = refs[n_in + nhi + n_out + nho:n_in + nhi + n_out + nho + n_scr]
            sems = refs[n_in + nhi + n_out + nho + n_scr:]
            first, last = None, None
            for k, g in enumerate(grid):
                f, l = pl.program_id(k) == 0, pl.program_id(k) == g - 1
                first = f if first is None else jnp.logical_and(first, f)
                last = l if last is None else jnp.logical_and(last, l)

            @pl.when(first)
            def _():
                hosted.start(hin, hout, sems)

            inner(*ins, *outs, *own)

            @pl.when(last)
            def _():
                hosted.wait(hin, hout, sems)

        in_specs = list(in_specs) + [ANY] * nhi
        out_specs = list(out_specs) + [ANY] * nho
        out_shape = list(out_shape) + hosted.out_shapes
        args = list(args) + hosted.arrays
        kwargs = dict(scratch_shapes=list(scratch) + hosted.sems,
                      input_output_aliases={n_in + i: n_out + j for i, j in hosted.aliases.items()})
    params = dict(dimension_semantics=("arbitrary",) * len(grid))
    if vmem_limit:
        params["vmem_limit_bytes"] = VMEM_LIMIT
    res = pl.pallas_call(body, name=name, grid=grid, in_specs=list(in_specs), out_specs=list(out_specs),
                         out_shape=list(out_shape), compiler_params=pltpu.CompilerParams(**params), **kwargs)(*args)
    if hosted is not None:
        _deliver(hosted, res[n_out:])
    return list(res[:n_out])


def _run_hosted(name, hosted):
    nhi, nho = len(hosted.arrays), len(hosted.out_shapes)

    def body(*refs):
        ins, outs, sems = refs[:nhi], refs[nhi:nhi + nho], refs[nhi + nho:]
        hosted.start(ins, outs, sems)
        hosted.wait(ins, outs, sems)

    res = pl.pallas_call(body, name=name, in_specs=[ANY] * nhi, out_specs=[ANY] * nho, out_shape=hosted.out_shapes,
                         scratch_shapes=hosted.sems, input_output_aliases=hosted.aliases)(*hosted.arrays)
    _deliver(hosted, res)
    return list(res)


def _const_spec(shape, single=False):
    nd = len(shape)
    if single:
        return pl.BlockSpec(shape, lambda b, i: (0,) * nd, pipeline_mode=pl.Buffered(1))
    return pl.BlockSpec(shape, lambda b, i: (0,) * nd)


def _tile_spec(arr, n_lat_tiles, lat_only=False, tm=TM):
    bt, _, cw = arr.shape
    if lat_only:
        return pl.BlockSpec((1, tm, cw), lambda b, i: (b if bt > 1 else 0, jnp.minimum(i, n_lat_tiles - 1), 0))
    return pl.BlockSpec((1, tm, cw), lambda b, i: (b if bt > 1 else 0, i, 0))


def _eparam_spec(arr, n_lat_tiles):
    cw = arr.shape[-1]
    return pl.BlockSpec((1, 1, 1, cw), lambda b, i: (b, (i >= n_lat_tiles).astype(jnp.int32), 0, 0))


def _stage_fwd(name, *, pre, post, wsel, splits, tiles, eparams, sparams, weights, out_widths, out_dtypes,
               batch, n_tiles, n_lat_tiles, hosted=None, tm=TM):
    nt, ne, ns, nw = len(tiles), len(eparams), len(sparams), len(weights)

    def body(*refs):
        t_refs = refs[:nt]
        e_refs = refs[nt:nt + ne]
        s_refs = refs[nt + ne:nt + ne + ns]
        w_refs = refs[nt + ne + ns:nt + ne + ns + nw]
        o_refs = refs[nt + ne + ns + nw:]
        tv = [r[0].astype(F32) for r in t_refs]
        ev = [r[0, 0] for r in e_refs]
        sv = [r[...] for r in s_refs]
        a = pre(tv, ev, sv)
        z = [_dot(a[wsel[j]], w_refs[j][...]) for j in range(nw)]
        if post is None:
            outs = [z[j][:, s:s + w] for (j, s, w) in splits]
        else:
            outs = post(z, tv, ev, sv)
        for o_ref, o in zip(o_refs, outs):
            o_ref[0] = o.astype(o_ref.dtype)

    in_specs = ([_tile_spec(t, n_lat_tiles, tm=tm) for t in tiles] + [_eparam_spec(e, n_lat_tiles) for e in eparams]
                + [_const_spec(s.shape) for s in sparams] + [_const_spec(w.shape, single=True) for w in weights])
    out_shape = [jax.ShapeDtypeStruct((batch, n_tiles * tm, w), dt) for w, dt in zip(out_widths, out_dtypes)]
    out_specs = [pl.BlockSpec((1, tm, w), lambda b, i: (b, i, 0)) for w in out_widths]
    return _pcall(body, name=name, grid=(batch, n_tiles), in_specs=in_specs, out_specs=out_specs,
                  out_shape=out_shape, args=[*tiles, *eparams, *sparams, *weights], hosted=hosted)


def _stage_bwd(name, *, pre, post, wsel, splits, tiles, tile_diff, eparams, sparams, weights, cots, cot_lat_only,
               batch, n_tiles, n_lat_tiles, add=None, add_lat_only=False, hosted=None, w_col_stack=None,
               dt_lat_only=False, tm=TM):
    nt, ne, ns, nw, nc = len(tiles), len(eparams), len(sparams), len(weights), len(cots)
    diff_idx = [k for k in range(nt) if tile_diff[k]]
    nd = len(diff_idx)
    has_add = add is not None
    w_col_stack = w_col_stack or [None] * nw

    def body(*refs):
        pos = 0
        t_refs = refs[pos:pos + nt]; pos += nt
        e_refs = refs[pos:pos + ne]; pos += ne
        s_refs = refs[pos:pos + ns]; pos += ns
        w_refs = refs[pos:pos + nw]; pos += nw
        c_refs = refs[pos:pos + nc]; pos += nc
        if has_add:
            add_ref = refs[pos]; pos += 1
        dt_refs = refs[pos:pos + nd]; pos += nd
        de_refs = refs[pos:pos + ne]; pos += ne
        ds_refs = refs[pos:pos + ns]; pos += ns
        dw_refs = refs[pos:pos + nw]; pos += nw

        b = pl.program_id(0)
        i = pl.program_id(1)
        is_lat = i < n_lat_tiles
        tv = [r[0].astype(F32) for r in t_refs]
        ev = tuple(r[0, 0] for r in e_refs)
        sv = tuple(r[...] for r in s_refs)
        dv0 = tuple(tv[k] for k in diff_idx)

        def merge(dv):
            full = list(tv)
            for k, v in zip(diff_idx, dv):
                full[k] = v
            return full

        def pre_f(dv, ev_, sv_):
            return tuple(pre(merge(dv), list(ev_), list(sv_)))

        a, vjp_pre = jax.vjp(pre_f, dv0, ev, sv)
        cv = []
        for c_ref, lat in zip(c_refs, cot_lat_only):
            c = c_ref[0].astype(F32)
            cv.append(jnp.where(is_lat, c, 0.0) if lat else c)
        if post is None:
            dz = []
            for j in range(nw):
                parts = [cv[k] for k, (jj, _, _) in enumerate(splits) if jj == j]
                dz.append(parts[0] if len(parts) == 1 else jnp.concatenate(parts, axis=1))
            dt2 = de2 = ds2 = None
        else:
            z = tuple(_dot(a[wsel[j]], w_refs[j][...]) for j in range(nw))

            def post_f(z_, dv, ev_, sv_):
                return tuple(post(list(z_), merge(dv), list(ev_), list(sv_)))

            _, vjp_post = jax.vjp(post_f, z, dv0, ev, sv)
            dz, dt2, de2, ds2 = vjp_post(tuple(cv))
        da = [None] * len(a)
        dws = []
        for j in range(nw):
            g = _dot_nt(dz[j], w_refs[j][...])
            da[wsel[j]] = g if da[wsel[j]] is None else da[wsel[j]] + g
            dws.append(_dot_tn(a[wsel[j]], dz[j]))
        da = tuple(jnp.zeros_like(a[k]) if da[k] is None else da[k] for k in range(len(a)))
        dt1, de1, ds1 = vjp_pre(da)

        def plus(u, v):
            return u if v is None else u + v

        for k in range(nd):
            val = plus(dt1[k], None if dt2 is None else dt2[k])
            if has_add and k == 0:
                addv = add_ref[0].astype(F32)
                val = val + (jnp.where(is_lat, addv, 0.0) if add_lat_only else addv)
            if dt_lat_only:
                @pl.when(is_lat)
                def _(k=k, val=val):
                    dt_refs[k][0] = val.astype(dt_refs[k].dtype)
            else:
                dt_refs[k][0] = val.astype(dt_refs[k].dtype)

        seg_first = jnp.logical_or(i == 0, i == n_lat_tiles)
        for k in range(ne):
            val = plus(de1[k], None if de2 is None else de2[k])

            @pl.when(seg_first)
            def _(k=k, val=val):
                de_refs[k][0, 0] = val

            @pl.when(jnp.logical_not(seg_first))
            def _(k=k, val=val):
                de_refs[k][0, 0] += val

        first = jnp.logical_and(b == 0, i == 0)
        acc = [(ds_refs[k], plus(ds1[k], None if ds2 is None else ds2[k])) for k in range(ns)]
        for j in range(nw):
            if w_col_stack[j]:
                cw = dws[j].shape[1] // w_col_stack[j]
                acc += [(dw_refs[j].at[c], dws[j][:, c * cw:(c + 1) * cw]) for c in range(w_col_stack[j])]
            else:
                acc.append((dw_refs[j], dws[j]))
        for ref, val in acc:
            @pl.when(first)
            def _(ref=ref, val=val):
                ref[...] = val

            @pl.when(jnp.logical_not(first))
            def _(ref=ref, val=val):
                ref[...] += val

    in_specs = ([_tile_spec(t, n_lat_tiles, tm=tm) for t in tiles] + [_eparam_spec(e, n_lat_tiles) for e in eparams]
                + [_const_spec(s.shape) for s in sparams] + [_const_spec(w.shape, single=True) for w in weights]
                + [_tile_spec(c, n_lat_tiles, lat, tm) for c, lat in zip(cots, cot_lat_only)])
    args = [*tiles, *eparams, *sparams, *weights, *cots]
    if has_add:
        in_specs.append(_tile_spec(add, n_lat_tiles, add_lat_only, tm))
        args.append(add)
    dt_tiles = n_lat_tiles if dt_lat_only else n_tiles
    out_shape = [jax.ShapeDtypeStruct((batch, dt_tiles * tm, tiles[k].shape[-1]), F32) for k in diff_idx]
    out_specs = [pl.BlockSpec((1, tm, tiles[k].shape[-1]), lambda b, i: (b, jnp.minimum(i, dt_tiles - 1), 0))
                 for k in diff_idx]
    out_shape += [jax.ShapeDtypeStruct(e.shape, F32) for e in eparams]
    out_specs += [_eparam_spec(e, n_lat_tiles) for e in eparams]
    out_shape += [jax.ShapeDtypeStruct(s.shape, F32) for s in sparams]
    out_specs += [_const_spec(s.shape) for s in sparams]
    dw_shapes = [(n, w.shape[0], w.shape[1] // n) if n else w.shape for w, n in zip(weights, w_col_stack)]
    out_shape += [jax.ShapeDtypeStruct(s, F32) for s in dw_shapes]
    out_specs += [_const_spec(s, single=True) for s in dw_shapes]
    res = _pcall(body, name=name, grid=(batch, n_tiles), in_specs=in_specs, out_specs=out_specs,
                 out_shape=out_shape, args=args, hosted=hosted)
    return res[:nd], res[nd:nd + ne], res[nd + ne:nd + ne + ns], res[nd + ne + ns:]


def _pre_adaln(tv, ev, sv):
    x = tv[0]
    sh, sc = ev[0], ev[1]
    return [_rms(x, sv[0]) * (1.0 + sc) + sh]


def _post_residual(x_index):
    def post(z, tv, ev, sv):
        return [tv[x_index] + ev[-1] * z[0]]
    return post


def _pre_conv_out(tv, ev, sv):
    c1, gg = tv[0], tv[1]
    return [_silu(_layernorm(c1, sv[0], sv[1])) * _silu(gg)]


def _pre_pool_out(tv, ev, sv):
    pooled, gg = tv[0], tv[1]
    w_grp, scale = sv[0], sv[1]
    gw = w_grp.shape[-1]
    y = jnp.concatenate([_mm(pooled[:, k * gw:(k + 1) * gw], w_grp[k]) for k in range(w_grp.shape[0])], axis=1)
    return [y * scale * _silu(gg)]


def _pre_rms_only(tv, ev, sv):
    return [_rms(tv[0], sv[0])]


def _post_mla_keys(z, tv, ev, sv):
    krp, cos, sin = tv[1], tv[2], tv[3]
    nope_g, rope_g = sv[1], sv[2]
    kv = z[0]
    kr = _rope(_rms(krp, rope_g, ROPE), cos, sin)
    ks, vs = [], []
    for h in range(HEADS):
        ks.append(_rms(kv[:, h * 2 * NOPE:h * 2 * NOPE + NOPE], nope_g))
        ks.append(kr)
        vs.append(kv[:, h * 2 * NOPE + NOPE:(h + 1) * 2 * NOPE])
    return [jnp.concatenate(ks, axis=1), jnp.concatenate(vs, axis=1)]


def _post_mla_queries(z, tv, ev, sv):
    cos, sin = tv[1], tv[2]
    nope_g, rope_g = sv[1], sv[2]
    q = z[0]
    qs = []
    for h in range(HEADS):
        qs.append(_rms(q[:, h * HEAD_W:h * HEAD_W + NOPE], nope_g))
        qs.append(_rope(_rms(q[:, h * HEAD_W + NOPE:(h + 1) * HEAD_W], rope_g, ROPE), cos, sin))
    return [jnp.concatenate(qs, axis=1) * Q_PRESCALE]


def _pre_mla_out(tv, ev, sv):
    return [tv[0] * _silu(tv[1])]


def _pre_chunk_out(tv, ev, sv):
    u, v, gg = tv[0], tv[1], tv[2]
    ln_g, ln_b, w_s, b_s = sv
    vn = _layernorm(v, ln_g, ln_b)
    rows = []
    for n in range(vn.shape[0] // CHUNK):
        blk = vn[n * CHUNK:(n + 1) * CHUNK]
        cols = [_mm(w_s[g], blk[:, g * LANES:(g + 1) * LANES]) + b_s[:, g:g + 1] for g in range(CHUNK_GROUPS)]
        rows.append(jnp.concatenate(cols, axis=1))
    s = jnp.concatenate(rows, axis=0)
    return [u * s * _silu(gg)]


def _segments(lat_len, tot_len):
    segs = [(0, lat_len)]
    if tot_len > lat_len:
        segs.append((lat_len, tot_len - lat_len))
    return segs


def _pad_rows(x):
    z = jnp.zeros((CONV_PAD, x.shape[1]), x.dtype)
    return jnp.concatenate([z, x, z], axis=0)


def _shifted(xp, j):
    n = xp.shape[0] - 2 * CONV_PAD
    if j != 0:
        xp = pltpu.roll(xp, (-j) % xp.shape[0], 0)
    return xp[CONV_PAD:CONV_PAD + n]


def _conv_fwd(a, bgate, dw, db, lat_len, hosted=None):
    batch, tot, e = a.shape
    segs = _segments(lat_len, tot)

    def body(a_ref, b_ref, dw_ref, db_ref, o_ref):
        w = dw_ref[...]
        for (s0, n) in segs:
            y = a_ref[0, s0:s0 + n, :].astype(F32) * jax.nn.sigmoid(b_ref[0, s0:s0 + n, :].astype(F32))
            yp = _pad_rows(y)
            acc = jnp.zeros_like(y) + db_ref[...]
            for k in range(CONV_WIDTH):
                acc = acc + _shifted(yp, k - CONV_HALF) * w[k:k + 1, :]
            o_ref[0, s0:s0 + n, :] = acc.astype(o_ref.dtype)

    blk = pl.BlockSpec((1, tot, LANES), lambda b, cb: (b, 0, cb))
    return _pcall(
        body, name="conv_fwd", grid=(batch, e // LANES),
        in_specs=[blk, blk, pl.BlockSpec((CONV_WIDTH, LANES), lambda b, cb: (0, cb)),
                  pl.BlockSpec((1, LANES), lambda b, cb: (0, cb))],
        out_specs=[blk], out_shape=[jax.ShapeDtypeStruct(a.shape, ACT)], args=[a, bgate, dw, db], hosted=hosted)[0]


def _conv_bwd(a, bgate, dw, dc1, lat_len, hosted=None):
    batch, tot, e = a.shape
    segs = _segments(lat_len, tot)

    def body(a_ref, b_ref, dw_ref, dc_ref, da_ref, dg_ref, ddw_ref, ddb_ref):
        b = pl.program_id(1)
        w = dw_ref[...]
        ddw_rows = [None] * CONV_WIDTH
        ddb = None
        for (s0, n) in segs:
            av = a_ref[0, s0:s0 + n, :].astype(F32)
            sg = jax.nn.sigmoid(b_ref[0, s0:s0 + n, :].astype(F32))
            y = av * sg
            dc = dc_ref[0, s0:s0 + n, :]
            yp, dcp = _pad_rows(y), _pad_rows(dc)
            dy = jnp.zeros_like(y)
            for k in range(CONV_WIDTH):
                j = k - CONV_HALF
                dy = dy + _shifted(dcp, -j) * w[k:k + 1, :]
                r = jnp.sum(dc * _shifted(yp, j), axis=0, keepdims=True)
                ddw_rows[k] = r if ddw_rows[k] is None else ddw_rows[k] + r
            r = jnp.sum(dc, axis=0, keepdims=True)
            ddb = r if ddb is None else ddb + r
            da_ref[0, s0:s0 + n, :] = dy * sg
            dg_ref[0, s0:s0 + n, :] = dy * av * sg * (1.0 - sg)

        @pl.when(b == 0)
        def _():
            ddw_ref[...] = jnp.zeros_like(ddw_ref)
            ddb_ref[...] = jnp.zeros_like(ddb_ref)

        for k in range(CONV_WIDTH):
            ddw_ref[k:k + 1, :] += ddw_rows[k]
        ddb_ref[...] += ddb

    blk = pl.BlockSpec((1, tot, LANES), lambda cb, b: (b, 0, cb))
    wspec = pl.BlockSpec((CONV_WIDTH, LANES), lambda cb, b: (0, cb))
    bspec = pl.BlockSpec((1, LANES), lambda cb, b: (0, cb))
    return _pcall(
        body, name="conv_bwd", grid=(e // LANES, batch),
        in_specs=[blk, blk, wspec, blk],
        out_specs=[blk, blk, wspec, bspec],
        out_shape=[jax.ShapeDtypeStruct(a.shape, F32), jax.ShapeDtypeStruct(a.shape, F32),
                   jax.ShapeDtypeStruct((CONV_WIDTH, e), F32), jax.ShapeDtypeStruct((1, e), F32)],
        args=[a, bgate, dw, dc1], hosted=hosted)


def _pool_counts(n, half, shape):
    t = lax.broadcasted_iota(jnp.int32, shape, 0)
    cnt = jnp.minimum(t + half, n) - jnp.maximum(t - half, 0)
    return cnt.astype(F32)


def _per_group(fn):
    for k, window in enumerate(POOL_WINDOWS):
        @pl.when(pl.program_id(1) == k)
        def _(window=window):
            fn(window // 2)


def _pool_fwd(v, lat_len, hosted=None):
    batch, tot, e = v.shape
    gw = e // len(POOL_WINDOWS)
    segs = _segments(lat_len, tot)

    def body(v_ref, o_ref):
        def group(half):
            for (s0, n) in segs:
                x = v_ref[0, s0:s0 + n, :]
                xp = _pad_rows(x)
                acc = _shifted(xp, -half)
                for j in range(-half + 1, half):
                    acc = acc + _shifted(xp, j)
                o_ref[0, s0:s0 + n, :] = (acc / _pool_counts(n, half, x.shape) - x).astype(o_ref.dtype)

        _per_group(group)

    blk = pl.BlockSpec((1, tot, gw), lambda b, g: (b, 0, g))
    return _pcall(body, name="pool_fwd", grid=(batch, len(POOL_WINDOWS)), in_specs=[blk], out_specs=[blk],
                  out_shape=[jax.ShapeDtypeStruct(v.shape, ACT)], args=[v], hosted=hosted)[0]


def _pool_bwd(dp, lat_len):
    batch, tot, e = dp.shape
    gw = e // len(POOL_WINDOWS)
    segs = _segments(lat_len, tot)

    def body(d_ref, o_ref):
        def group(half):
            for (s0, n) in segs:
                d = d_ref[0, s0:s0 + n, :]
                dnp = _pad_rows(d / _pool_counts(n, half, d.shape))
                acc = _shifted(dnp, half)
                for j in range(-half + 1, half):
                    acc = acc + _shifted(dnp, -j)
                o_ref[0, s0:s0 + n, :] = acc - d

        _per_group(group)

    blk = pl.BlockSpec((1, tot, gw), lambda b, g: (b, 0, g))
    return pl.pallas_call(
        body, name="pool_bwd", grid=(batch, len(POOL_WINDOWS)), in_specs=[blk], out_specs=blk,
        out_shape=jax.ShapeDtypeStruct(dp.shape, F32),
        compiler_params=pltpu.CompilerParams(dimension_semantics=("arbitrary", "arbitrary"),
                                             vmem_limit_bytes=VMEM_LIMIT),
    )(dp)


def _attn_fwd(q, k, v, hosted=None):
    batch, lq, _ = q.shape
    tk = k.shape[1]
    tq = min(TQ, lq)

    def body(q_ref, k_ref, v_ref, o_ref, lse_ref):
        s2 = _dot_nt(q_ref[0], k_ref[0])
        m2 = jnp.max(s2, axis=-1, keepdims=True)
        e = jnp.exp2(s2 - m2)
        l = jnp.sum(e, axis=-1, keepdims=True)
        o_ref[0] = (_dot(e, v_ref[0]) / l).astype(o_ref.dtype)
        lse_ref[0, 0] = m2 + jnp.log2(l)

    return _pcall(
        body, name="attn_fwd", grid=(batch, HEADS, lq // tq),
        in_specs=[pl.BlockSpec((1, tq, HEAD_W), lambda b, h, i: (b, i, h)),
                  pl.BlockSpec((1, tk, HEAD_W), lambda b, h, i: (b, 0, h)),
                  pl.BlockSpec((1, tk, VDIM), lambda b, h, i: (b, 0, h))],
        out_specs=[pl.BlockSpec((1, tq, VDIM), lambda b, h, i: (b, i, h)),
                   pl.BlockSpec((1, 1, tq, 1), lambda b, h, i: (b, h, i, 0))],
        out_shape=[jax.ShapeDtypeStruct((batch, lq, HEADS * VDIM), ACT),
                   jax.ShapeDtypeStruct((batch, HEADS, lq, 1), F32)], args=[q, k, v], hosted=hosted)


def _attn_bwd(q, k, v, o, lse, do, hosted=None):
    batch, lq, _ = q.shape
    tk = k.shape[1]
    tq = min(TQ_BWD, lq)

    def body(q_ref, k_ref, v_ref, o_ref, lse_ref, do_ref, dq_ref, dk_ref, dv_ref, p_scr, ds_scr):
        i = pl.program_id(2)
        nr = tq // ATT_RQ
        rows = [slice(r * ATT_RQ, (r + 1) * ATT_RQ) for r in range(nr)]
        qv = [q_ref[0, rw, :] for rw in rows]
        dob = [do_ref[0, rw, :].astype(BF16) for rw in rows]
        row_lse = [lse_ref[0, 0, rw, :] for rw in rows]
        delta = [jnp.sum(do_ref[0, rw, :] * o_ref[0, rw, :], axis=-1, keepdims=True) for rw in rows]
        for c in range(tk // ATT_KC):
            keys = slice(c * ATT_KC, (c + 1) * ATT_KC)
            kc, vc = k_ref[0, keys, :], v_ref[0, keys, :]
            for r in range(nr):
                p = jnp.exp2(_dot_nt(qv[r], kc) - row_lse[r])
                dp = _dot_nt(dob[r], vc)
                p_scr[rows[r], keys] = p.astype(BF16)
                ds_scr[rows[r], keys] = (p * (dp - delta[r]) * LN2).astype(BF16)
        dq_ref[0] = _dot(ds_scr[...], k_ref[0])
        dk = _dot_tn(ds_scr[...], q_ref[0])
        dv = _dot_tn(p_scr[...], do_ref[0])

        @pl.when(i == 0)
        def _():
            dk_ref[0] = dk
            dv_ref[0] = dv

        @pl.when(i != 0)
        def _():
            dk_ref[0] += dk
            dv_ref[0] += dv

    return _pcall(
        body, name="attn_bwd", grid=(batch, HEADS, lq // tq),
        in_specs=[pl.BlockSpec((1, tq, HEAD_W), lambda b, h, i: (b, i, h)),
                  pl.BlockSpec((1, tk, HEAD_W), lambda b, h, i: (b, 0, h)),
                  pl.BlockSpec((1, tk, VDIM), lambda b, h, i: (b, 0, h)),
                  pl.BlockSpec((1, tq, VDIM), lambda b, h, i: (b, i, h)),
                  pl.BlockSpec((1, 1, tq, 1), lambda b, h, i: (b, h, i, 0)),
                  pl.BlockSpec((1, tq, VDIM), lambda b, h, i: (b, i, h))],
        out_specs=[pl.BlockSpec((1, tq, HEAD_W), lambda b, h, i: (b, i, h)),
                   pl.BlockSpec((1, tk, HEAD_W), lambda b, h, i: (b, 0, h)),
                   pl.BlockSpec((1, tk, VDIM), lambda b, h, i: (b, 0, h))],
        out_shape=[jax.ShapeDtypeStruct(q.shape, F32), jax.ShapeDtypeStruct(k.shape, F32),
                   jax.ShapeDtypeStruct(v.shape, F32)],
        args=[q, k, v, o, lse, do], hosted=hosted,
        scratch=[pltpu.VMEM((tq, tk), BF16), pltpu.VMEM((tq, tk), BF16)])


def _loss_kernel(y, target):
    batch, lq, d = y.shape

    def body(y_ref, t_ref, l_ref, dy_ref):
        first = jnp.logical_and(pl.program_id(0) == 0, pl.program_id(1) == 0)
        err = y_ref[0] - t_ref[0]
        dy_ref[0] = err * (1.0 / d)
        part = jnp.zeros((1, LANES), F32) + jnp.sum(err * err) * (0.5 / d)

        @pl.when(first)
        def _():
            l_ref[...] = part

        @pl.when(jnp.logical_not(first))
        def _():
            l_ref[...] += part

    blk = pl.BlockSpec((1, TM, d), lambda b, i: (b, i, 0))
    return pl.pallas_call(
        body, name="loss_head", grid=(batch, lq // TM), in_specs=[blk, blk],
        out_specs=[pl.BlockSpec((1, LANES), lambda b, i: (0, 0)), blk],
        out_shape=[jax.ShapeDtypeStruct((1, LANES), F32), jax.ShapeDtypeStruct(y.shape, F32)],
        compiler_params=pltpu.CompilerParams(dimension_semantics=("arbitrary", "arbitrary")),
    )(y, target)


def _rope_tables(lat_len, ctx_len):
    rows = lat_len // GRID_W
    row_id = jnp.repeat(jnp.arange(rows), GRID_W).astype(F32)
    col_id = jnp.tile(jnp.arange(GRID_W), rows).astype(F32)
    axis_dim = ROPE // 2
    freqs = ROPE_THETA ** (-jnp.arange(0, axis_dim, 2, dtype=F32) / axis_dim)
    ar = row_id[:, None] * freqs
    ac = col_id[:, None] * freqs
    cr, sr, cc, sc = jnp.cos(ar), jnp.sin(ar), jnp.cos(ac), jnp.sin(ac)
    pad = jnp.zeros((lat_len, LANES - ROPE), F32)
    cos = jnp.concatenate([cr, cr, cc, cc, pad], axis=1)
    sin = jnp.concatenate([-sr, sr, -sc, sc, pad], axis=1)
    ident = jnp.concatenate([jnp.ones((ctx_len, ROPE), F32), jnp.zeros((ctx_len, LANES - ROPE), F32)], axis=1)
    cos = jnp.concatenate([cos, ident], axis=0)
    sin = jnp.concatenate([sin, jnp.zeros((ctx_len, LANES), F32)], axis=0)
    return cos[None], sin[None]


def _prep_weights(w):
    p = dict(w)
    kvc = KV_RANK + ROPE
    if "ml_w_in" in w:
        wi = w["ml_w_in"]
        p["ml_w_in"] = jnp.concatenate(
            [wi[:, :kvc], jnp.zeros((wi.shape[0], LANES - ROPE), wi.dtype), wi[:, kvc:]], axis=1)
    if "ml_w_uq" in w:
        uq = w["ml_w_uq"].reshape(Q_RANK, HEADS, NOPE + ROPE)
        p["ml_w_uq"] = jnp.pad(uq, ((0, 0), (0, 0), (0, HEAD_W - NOPE - ROPE))).reshape(Q_RANK, HEADS * HEAD_W)
    if "ml_rope_norm" in w:
        p["ml_rope_norm"] = jnp.pad(w["ml_rope_norm"], ((0, 0), (0, LANES - ROPE)))
    return p


def _unprep_grads(g):
    out = dict(g)
    kvc = KV_RANK + ROPE
    if "ml_w_in" in g:
        wi = g["ml_w_in"]
        out["ml_w_in"] = jnp.concatenate([wi[:, :kvc], wi[:, kvc + LANES - ROPE:]], axis=1)
    if "ml_w_uq" in g:
        uq = g["ml_w_uq"].reshape(Q_RANK, HEADS, HEAD_W)
        out["ml_w_uq"] = uq[:, :, :NOPE + ROPE].reshape(Q_RANK, HEADS * (NOPE + ROPE))
    if "ml_rope_norm" in g:
        out["ml_rope_norm"] = g["ml_rope_norm"][:, :ROPE]
    return out


LAYER_WEIGHTS = (("cv_w_in", "cv_w_out"), ("pl_w_in", "pl_w_grp", "pl_w_out"),
                 ("ml_w_in", "ml_w_uq", "ml_w_ukv", "ml_w_out"), ("ch_w_in", "ch_w_out"))


class _LocalPlan:
    def __init__(self, w):
        self.small = w
        self.grads = {}

    def weights(self, names):
        return {n: self.small[n] for n in names}

    def hosted(self, tag):
        return None

    def after(self, tag):
        pass

    def note(self, values):
        pass

    def layer_grads(self, layer, grads):
        self.grads.update(grads)


def _local_step(xm, target, mods, plan, lat_len):
    batch, tot, d = xm.shape
    e = d
    n_all, n_lat = tot // TM, lat_len // TM
    cos, sin = _rope_tables(lat_len, tot - lat_len)
    g = {}
    w = dict(plan.small)

    def hosting(tag, fn, *args, **kwargs):
        out = fn(*args, hosted=plan.hosted(tag), **kwargs)
        plan.after(tag)
        return out

    def s1_splits(widths):
        out, s = [], 0
        for wd in widths:
            out.append((0, s, wd))
            s += wd
        return out

    tml = TM_LATENT if lat_len % TM_LATENT == 0 else TM
    n_big = lat_len // tml

    def lat_tiles(n_tiles, tm):
        return n_lat if tm == TM else n_tiles

    def fwd_in(name, x, mod, gi, wname, widths, n_tiles, dtypes=None, tm=TM):
        return hosting(name, _stage_fwd, name, pre=_pre_adaln, post=None, wsel=[0], splits=s1_splits(widths),
                       tiles=[x], eparams=[mod[0], mod[1]], sparams=[w["norm_g"][gi:gi + 1]], weights=[w[wname]],
                       out_widths=widths, out_dtypes=dtypes or [ACT] * len(widths), batch=batch, n_tiles=n_tiles,
                       n_lat_tiles=lat_tiles(n_tiles, tm), tm=tm)

    def bwd_in(name, x, mod, gi, wname, widths, n_tiles, cots, lat_only, add, add_lat_only, stack=None,
               dx_lat_only=False):
        (dx,), (dsh, dsc), (dg,), (dw,) = hosting(
            name, _stage_bwd, name, pre=_pre_adaln, post=None, wsel=[0], splits=s1_splits(widths), tiles=[x],
            tile_diff=[True], eparams=[mod[0], mod[1]], sparams=[w["norm_g"][gi:gi + 1]], weights=[w[wname]],
            cots=cots, cot_lat_only=lat_only, batch=batch, n_tiles=n_tiles, n_lat_tiles=n_lat, add=add,
            add_lat_only=add_lat_only, w_col_stack=[stack], dt_lat_only=dx_lat_only)
        return dx, dsh, dsc, dg, dw

    def fwd_out(name, pre, tiles, mod, sparams, wname, n_tiles, tm=TM):
        return hosting(name, _stage_fwd, name, pre=pre, post=_post_residual(len(tiles) - 1), wsel=[0], splits=None,
                       tiles=tiles, eparams=[mod[2]], sparams=sparams, weights=[w[wname]], out_widths=[d],
                       out_dtypes=[F32], batch=batch, n_tiles=n_tiles, n_lat_tiles=lat_tiles(n_tiles, tm), tm=tm)[0]

    def bwd_out(name, pre, tiles, mod, sparams, wname, n_tiles, cot, tm=TM):
        diff = [True] * (len(tiles) - 1) + [False]
        dts, (dgt,), dss, (dw,) = hosting(
            name, _stage_bwd, name, pre=pre, post=_post_residual(len(tiles) - 1), wsel=[0], splits=None, tiles=tiles,
            tile_diff=diff, eparams=[mod[2]], sparams=sparams, weights=[w[wname]], cots=[cot], cot_lat_only=[False],
            batch=batch, n_tiles=n_tiles, n_lat_tiles=lat_tiles(n_tiles, tm), tm=tm)
        return dts, dgt, dss, dw

    w.update(plan.weights(("cv_w_in",)))
    cv_s = [w["cv_ln_g"], w["cv_ln_b"]]
    a0, b0, g0 = fwd_in("cv_in_fwd", xm, mods[0], 0, "cv_w_in", [e, e, e], n_all)
    c1 = hosting("conv_fwd", _conv_fwd, a0, b0, w["cv_dw"], w["cv_db"], lat_len)
    w.update(plan.weights(("cv_w_out",)))
    x1 = fwd_out("cv_out_fwd", _pre_conv_out, [c1, g0, xm], mods[0], cv_s, "cv_w_out", n_all)

    w.update(plan.weights(LAYER_WEIGHTS[1]))
    pl_s = [w["pl_w_grp"], w["pl_scale"]]
    v1, g1 = fwd_in("pl_in_fwd", x1, mods[1], 1, "pl_w_in", [e, e], n_all, dtypes=[F32, ACT])
    pooled = hosting("pool_fwd", _pool_fwd, v1, lat_len)
    x2 = fwd_out("pl_out_fwd", _pre_pool_out, [pooled, g1, x1], mods[1], pl_s, "pl_w_out", n_all)

    w.update(plan.weights(LAYER_WEIGHTS[2]))
    ml_widths = [KV_RANK, LANES, Q_RANK, HEADS * VDIM]
    ckv, krp, cq, g2 = fwd_in("ml_in_fwd", x2, mods[2], 2, "ml_w_in", ml_widths, n_all)
    k_s = [w["ml_kv_norm"], w["ml_nope_norm"][1:2], w["ml_rope_norm"][1:2]]
    q_s = [w["ml_q_norm"], w["ml_nope_norm"][0:1], w["ml_rope_norm"][0:1]]
    kk, vv = hosting("ml_keys_fwd", _stage_fwd, "ml_keys_fwd", pre=_pre_rms_only, post=_post_mla_keys, wsel=[0],
                     splits=None, tiles=[ckv, krp, cos, sin], eparams=[], sparams=k_s, weights=[w["ml_w_ukv"]],
                     out_widths=[HEADS * HEAD_W, HEADS * VDIM], out_dtypes=[BF16, BF16], batch=batch,
                     n_tiles=n_all, n_lat_tiles=n_lat)
    (qq,) = _stage_fwd("ml_queries_fwd", pre=_pre_rms_only, post=_post_mla_queries, wsel=[0], splits=None,
                       tiles=[cq, cos, sin], eparams=[], sparams=q_s, weights=[w["ml_w_uq"]],
                       out_widths=[HEADS * HEAD_W], out_dtypes=[BF16], batch=batch, n_tiles=n_big,
                       n_lat_tiles=n_big, tm=tml)
    att, lse = hosting("attn_fwd", _attn_fwd, qq, kk, vv)
    x3 = fwd_out("ml_out_fwd", _pre_mla_out, [att, g2, x2], mods[2], [], "ml_w_out", n_big, tm=tml)

    w.update(plan.weights(LAYER_WEIGHTS[3]))
    ch_s = [w["ch_ln_g"], w["ch_ln_b"], w["ch_w_s"], w["ch_b_s"]]
    u3, v3, g3 = fwd_in("ch_in_fwd", x3, mods[3], 3, "ch_w_in", [e, e, e], n_big, tm=tml)
    x4 = fwd_out("ch_out_fwd", _pre_chunk_out, [u3, v3, g3, x3], mods[3], ch_s, "ch_w_out", n_big, tm=tml)

    loss_part, dy = _loss_kernel(x4, target)

    dmods = [None] * 4
    dnorm = [None] * 4
    big = {}
    (du, dv, dg), dgt, (g["ch_ln_g"], g["ch_ln_b"], g["ch_w_s"], g["ch_b_s"]), big["ch_w_out"] = bwd_out(
        "ch_out_bwd", _pre_chunk_out, [u3, v3, g3, x3], mods[3], ch_s, "ch_w_out", n_big, dy, tm=tml)
    plan.note({n: g[n] for n in ("ch_ln_g", "ch_ln_b", "ch_w_s", "ch_b_s")})
    dx3, dsh, dsc, dnorm[3], big["ch_w_in"] = bwd_in("ch_in_bwd", x3, mods[3], 3, "ch_w_in", [e, e, e], n_lat,
                                                     [du, dv, dg], [False] * 3, dy, False, stack=N_CHIP)
    dmods[3] = (dsh, dsc, dgt)
    plan.layer_grads(3, big)

    big = {}
    (datt, dg), dgt, _, big["ml_w_out"] = bwd_out("ml_out_bwd", _pre_mla_out, [att, g2, x2], mods[2], [],
                                                  "ml_w_out", n_big, dx3, tm=tml)
    dq, dk, dvv = hosting("attn_bwd", _attn_bwd, qq, kk, vv, att, lse, datt)
    (dcq,), _, (g["ml_q_norm"], dnope0, drope0), (big["ml_w_uq"],) = hosting(
        "ml_queries_bwd", _stage_bwd, "ml_queries_bwd", pre=_pre_rms_only, post=_post_mla_queries, wsel=[0],
        splits=None, tiles=[cq, cos, sin], tile_diff=[True, False, False], eparams=[], sparams=q_s,
        weights=[w["ml_w_uq"]], cots=[dq], cot_lat_only=[False], batch=batch, n_tiles=n_big, n_lat_tiles=n_big,
        tm=tml)
    (dckv, dkrp), _, (g["ml_kv_norm"], dnope1, drope1), (big["ml_w_ukv"],) = hosting(
        "ml_keys_bwd", _stage_bwd, "ml_keys_bwd", pre=_pre_rms_only, post=_post_mla_keys, wsel=[0], splits=None,
        tiles=[ckv, krp, cos, sin], tile_diff=[True, True, False, False], eparams=[], sparams=k_s,
        weights=[w["ml_w_ukv"]], cots=[dk, dvv], cot_lat_only=[False, False], batch=batch, n_tiles=n_all,
        n_lat_tiles=n_lat, w_col_stack=[N_CHIP])
    g["ml_nope_norm"] = jnp.concatenate([dnope0, dnope1], axis=0)
    g["ml_rope_norm"] = jnp.concatenate([drope0, drope1], axis=0)
    dx2, dsh, dsc, dnorm[2], big["ml_w_in"] = bwd_in("ml_in_bwd", x2, mods[2], 2, "ml_w_in", ml_widths, n_all,
                                                     [dckv, dkrp, dcq, dg], [False, False, True, True], dx3, True)
    dmods[2] = (dsh, dsc, dgt)
    plan.layer_grads(2, big)

    big = {}
    (dpooled, dg), dgt, (big["pl_w_grp"], g["pl_scale"]), big["pl_w_out"] = bwd_out(
        "pl_out_bwd", _pre_pool_out, [pooled, g1, x1], mods[1], pl_s, "pl_w_out", n_all, dx2)
    dv1 = _pool_bwd(dpooled, lat_len)
    dx1, dsh, dsc, dnorm[1], big["pl_w_in"] = bwd_in("pl_in_bwd", x1, mods[1], 1, "pl_w_in", [e, e], n_all,
                                                     [dv1, dg], [False] * 2, dx2, False, stack=N_CHIP)
    dmods[1] = (dsh, dsc, dgt)
    plan.layer_grads(1, big)

    big = {}
    (dc1, dg), dgt, (g["cv_ln_g"], g["cv_ln_b"]), big["cv_w_out"] = bwd_out(
        "cv_out_bwd", _pre_conv_out, [c1, g0, xm], mods[0], cv_s, "cv_w_out", n_all, dx1)
    da, db, g["cv_dw"], g["cv_db"] = hosting("conv_bwd", _conv_bwd, a0, b0, w["cv_dw"], dc1, lat_len)
    dx0, dsh, dsc, dnorm[0], big["cv_w_in"] = bwd_in("cv_in_bwd", xm, mods[0], 0, "cv_w_in", [e, e, e], n_all,
                                                     [da, db, dg], [False] * 3, dx1, False, stack=N_CHIP,
                                                     dx_lat_only=True)
    dmods[0] = (dsh, dsc, dgt)
    plan.layer_grads(0, big)
    g["norm_g"] = jnp.concatenate(dnorm, axis=0)
    return loss_part, dx0, dmods, g


N_DEV = 8
N_CHIP = 4
ANY = pl.BlockSpec(memory_space=pl.ANY)


def _my_place():
    return lax.axis_index("x"), lax.axis_index("y"), lax.axis_index("c")


def _flip(v, f):
    return 1 - v if f else v


def _ag8_copies(x):
    def plan(ins, outs, sems):
        mx, my, mc = _my_place()
        me = 4 * mx + 2 * my + mc
        sends, recvs = [], []
        for rel in range(1, N_DEV):
            peer = (_flip(mx, rel & 4), _flip(my, rel & 2), _flip(mc, rel & 1))
            src_dev = 4 * peer[0] + 2 * peer[1] + peer[2]
            sends.append(_remote(ins[0], outs[0].at[me], sems, rel - 1, peer))
            recvs.append(_remote(ins[0], outs[0].at[src_dev], sems, rel - 1, peer))
        return sends, recvs, [pltpu.make_async_copy(ins[0], outs[0].at[me], sems[2].at[0])]

    return _copies_hosted([x], [jax.ShapeDtypeStruct((N_DEV,) + x.shape, x.dtype)], (N_DEV - 1, N_DEV - 1, 1), plan)


def _ag8(name, x):
    return _run_hosted(name, _ag8_copies(x))[0]


def _ag8_column_copies(x, width):
    def plan(ins, outs, sems):
        mx, my, mc = _my_place()
        me = 4 * mx + 2 * my + mc
        sends, recvs = [], []
        for rel in range(1, N_DEV):
            peer = (_flip(mx, rel & 4), _flip(my, rel & 2), _flip(mc, rel & 1))
            src_dev = 4 * peer[0] + 2 * peer[1] + peer[2]
            cols = pl.ds(pl.multiple_of((2 * peer[0] + peer[1]) * width, LANES), width)
            sends.append(_remote(ins[0].at[:, cols], outs[0].at[me], sems, rel - 1, peer))
            recvs.append(_remote(ins[0].at[:, cols], outs[0].at[src_dev], sems, rel - 1, peer))
        mine = pl.ds(pl.multiple_of((2 * mx + my) * width, LANES), width)
        return sends, recvs, [pltpu.make_async_copy(ins[0].at[:, mine], outs[0].at[me], sems[2].at[0])]

    return _copies_hosted([x], [jax.ShapeDtypeStruct((N_DEV, x.shape[0], width), x.dtype)],
                          (N_DEV - 1, N_DEV - 1, 1), plan)


def _chip_rows_copies(x, rows_per_dev, shared_row):
    n_out = rows_per_dev + 1

    def plan(ins, outs, sems):
        mx, my, mc = _my_place()
        chip = 2 * mx + my
        sends, recvs = [], []

        def pieces(dev):
            return [(ins[0].at[pl.ds(dev * rows_per_dev, rows_per_dev)], slice(0, rows_per_dev)),
                    (ins[0].at[pl.ds(shared_row, 1)], slice(rows_per_dev, n_out))]

        for k, peer, pchip in _chip_peers(mx, my, mc):
            for t, (src, where) in enumerate(pieces(2 * pchip + mc)):
                sends.append(_remote(src, outs[0].at[chip, where], sems, 2 * k + t, peer))
                recvs.append(_remote(src, outs[0].at[pchip, where], sems, 2 * k + t, peer))
        locals_ = [pltpu.make_async_copy(src, outs[0].at[chip, where], sems[2].at[t])
                   for t, (src, where) in enumerate(pieces(2 * chip + mc))]
        return sends, recvs, locals_

    return _copies_hosted([x], [jax.ShapeDtypeStruct((N_CHIP, n_out) + x.shape[1:], x.dtype)], (6, 6, 2), plan)


def _chip_peers(mx, my, mc):
    out = []
    for rel in range(1, N_CHIP):
        px, py = _flip(mx, rel & 2), _flip(my, rel & 1)
        out.append((rel - 1, (px, py, mc), 2 * px + py))
    return out


def _half(mc, rows):
    return pl.ds(pl.multiple_of(mc * (rows // 2), 8), rows // 2)


def _copies_hosted(arrays, out_shapes, n_sems, plan, aliases=None):
    def start(ins, outs, sems):
        sends, _, locals_ = plan(ins, outs, sems)
        for cp in locals_ + sends:
            cp.start()

    def wait(ins, outs, sems):
        sends, recvs, locals_ = plan(ins, outs, sems)
        for cp in recvs:
            cp.wait_recv()
        for cp in sends:
            cp.wait_send()
        for cp in locals_:
            cp.wait()

    return _Hosted(arrays, out_shapes, [pltpu.SemaphoreType.DMA((k,)) for k in n_sems], start, wait, aliases)


def _remote(src, dst, sems, k, peer):
    return pltpu.make_async_remote_copy(src_ref=src, dst_ref=dst, send_sem=sems[0].at[k], recv_sem=sems[1].at[k],
                                        device_id=peer, device_id_type=MESH)


def _gather_ici(shards):
    n = len(shards)

    def plan(ins, outs, sems):
        mx, my, mc = _my_place()
        chip = 2 * mx + my
        sends, recvs, locals_ = [], [], []
        for a in range(n):
            rows = ins[a].shape[0]
            locals_.append(pltpu.make_async_copy(ins[a], outs[a].at[chip], sems[2].at[a]))
            for k, peer, pchip in _chip_peers(mx, my, mc):
                src = ins[a].at[_half(mc, rows)]
                sends.append(_remote(src, outs[a].at[chip, _half(mc, rows)], sems, 3 * a + k, peer))
                recvs.append(_remote(src, outs[a].at[pchip, _half(mc, rows)], sems, 3 * a + k, peer))
        return sends, recvs, locals_

    return _copies_hosted(shards, [jax.ShapeDtypeStruct((N_CHIP,) + s.shape, s.dtype) for s in shards],
                          (3 * n, 3 * n, n), plan)


def _sibling_fill(arrays, row_axis, chips_only_other):
    n = len(arrays)
    per = 3 if chips_only_other else 1

    def plan(ins, outs, sems):
        mx, my, mc = _my_place()
        sibling = (mx, my, 1 - mc)

        def views(a, core):
            rows = outs[a].shape[row_axis]
            if chips_only_other:
                return [outs[a].at[pchip, _half(core, rows)] for _, _, pchip in _chip_peers(mx, my, mc)]
            return [outs[a].at[_half(core, rows)]]

        sends, recvs = [], []
        for a in range(n):
            for k, v in enumerate(views(a, mc)):
                sends.append(_remote(v, v, sems, per * a + k, sibling))
            for k, v in enumerate(views(a, 1 - mc)):
                recvs.append(_remote(v, v, sems, per * a + k, sibling))
        return sends, recvs, []

    return _copies_hosted(arrays, [jax.ShapeDtypeStruct(s.shape, s.dtype) for s in arrays], (per * n, per * n), plan,
                          aliases={a: a for a in range(n)})


def _grad_swap_d2d(stacks):
    n = len(stacks)

    def plan(ins, outs, sems):
        mx, my, mc = _my_place()
        sibling = (mx, my, 1 - mc)
        sends = [_remote(ins[a].at[:, _half(1 - mc, ins[a].shape[1])], outs[a], sems, a, sibling) for a in range(n)]
        return sends, sends, []

    return _copies_hosted(stacks, [jax.ShapeDtypeStruct((N_CHIP, s.shape[1] // 2, s.shape[2]), s.dtype)
                                   for s in stacks], (n, n), plan)


def _grad_exchange_ici(parts):
    n = len(parts)

    def plan(ins, outs, sems):
        mx, my, mc = _my_place()
        chip = 2 * mx + my
        sends, recvs, locals_ = [], [], []
        for a in range(n):
            locals_.append(pltpu.make_async_copy(ins[a].at[chip], outs[a].at[chip], sems[2].at[a]))
            for k, peer, pchip in _chip_peers(mx, my, mc):
                sends.append(_remote(ins[a].at[pchip], outs[a].at[chip], sems, 3 * a + k, peer))
                recvs.append(_remote(ins[a].at[pchip], outs[a].at[pchip], sems, 3 * a + k, peer))
        return sends, recvs, locals_

    return _copies_hosted(parts, [jax.ShapeDtypeStruct(s.shape, s.dtype) for s in parts], (3 * n, 3 * n, n), plan)


def _row_block(rows, limit=256):
    for t in range(min(rows, limit), 7, -8):
        if rows % t == 0 and t % 8 == 0:
            return t
    return rows


def _grad_add_half(core, stack, received):
    _, rows, cw = stack.shape
    rh = rows // 2
    tr = _row_block(rh)

    def body(s_ref, a_ref, b_ref, o_ref):
        o_ref[...] = (a_ref[...] + b_ref[...]).astype(o_ref.dtype)

    grid_spec = pltpu.PrefetchScalarGridSpec(
        num_scalar_prefetch=1, grid=(rh // tr,),
        in_specs=[pl.BlockSpec((N_CHIP, tr, cw), lambda i, s: (0, s[0] * (rh // tr) + i, 0)),
                  pl.BlockSpec((N_CHIP, tr, cw), lambda i, s: (0, i, 0))],
        out_specs=pl.BlockSpec((N_CHIP, tr, cw), lambda i, s: (0, i, 0)))
    return pl.pallas_call(
        body, name="grad_add_half", grid_spec=grid_spec, out_shape=jax.ShapeDtypeStruct(received.shape, BF16),
        compiler_params=pltpu.CompilerParams(dimension_semantics=("arbitrary",), vmem_limit_bytes=VMEM_LIMIT),
    )(core, stack, received)


def _adamw(name, row_off, parts, w, m, v, rows, hosted=None):
    n, _, cw = parts.shape
    tr = _row_block(rows, 128)

    def update(p_ref, w_ref, m_ref, v_ref, g_ref, d_ref, nm_ref, nv_ref):
        g = p_ref[0].astype(F32)
        for k in range(1, n):
            g = g + p_ref[k].astype(F32)
        nm = ADAM_B1 * m_ref[...] + (1.0 - ADAM_B1) * g
        nv = ADAM_B2 * v_ref[...] + (1.0 - ADAM_B2) * (g * g)
        m_hat = nm / (1.0 - ADAM_B1 ** ADAM_STEP)
        v_hat = nv / (1.0 - ADAM_B2 ** ADAM_STEP)
        g_ref[...] = g
        d_ref[...] = -ADAM_LR * (m_hat / (jnp.sqrt(v_hat) + ADAM_EPS) + ADAM_WD * w_ref[...])
        nm_ref[...] = nm
        nv_ref[...] = nv

    out_shape = [jax.ShapeDtypeStruct(w.shape, F32)] * 4
    if row_off is None:
        blk = pl.BlockSpec((tr, cw), lambda i: (i, 0))
        return _pcall(update, name=name, grid=(rows // tr,), out_specs=[blk] * 4, out_shape=out_shape,
                      in_specs=[pl.BlockSpec((n, tr, cw), lambda i: (0, i, 0)), blk, blk, blk],
                      args=[parts, w, m, v], hosted=hosted)

    def body(s_ref, *refs):
        update(*refs)

    full = pl.BlockSpec((tr, cw), lambda i, s: (s[0] // tr + i, 0))
    grid_spec = pltpu.PrefetchScalarGridSpec(
        num_scalar_prefetch=1, grid=(rows // tr,),
        in_specs=[pl.BlockSpec((n, tr, cw), lambda i, s: (0, i, 0)), full, full, full],
        out_specs=[full, full, full, full])
    return pl.pallas_call(
        body, name=name, grid_spec=grid_spec, out_shape=out_shape,
        compiler_params=pltpu.CompilerParams(dimension_semantics=("arbitrary",), vmem_limit_bytes=VMEM_LIMIT),
    )(row_off, parts, w, m, v)


def _sum8(x):
    _, r, cw = x.shape
    tr = _row_block(r, 64)

    def body(x_ref, o_ref):
        acc = x_ref[0]
        for k in range(1, N_DEV):
            acc = acc + x_ref[k]
        o_ref[...] = acc

    return pl.pallas_call(
        body, name="sum8", grid=(r // tr,), in_specs=[pl.BlockSpec((N_DEV, tr, cw), lambda i: (0, i, 0))],
        out_specs=pl.BlockSpec((tr, cw), lambda i: (i, 0)), out_shape=jax.ShapeDtypeStruct((r, cw), F32),
        compiler_params=pltpu.CompilerParams(dimension_semantics=("arbitrary",)),
    )(x)


MOD_ROWS = 24
CTX_ROW = 16


def _mod_fwd(c_rows, w_mod, b_mod, hosted=None):
    nl, d, nn = w_mod.shape

    def body(c_ref, w_ref, b_ref, o_ref):
        o_ref[0] = _dot(_silu(c_ref[...]), w_ref[0]) + b_ref[0]

    return _pcall(
        body, name="mod_fwd", grid=(nl,),
        in_specs=[pl.BlockSpec((MOD_ROWS, d), lambda i: (0, 0)), pl.BlockSpec((1, d, nn), lambda i: (i, 0, 0)),
                  pl.BlockSpec((1, 1, nn), lambda i: (i, 0, 0))],
        out_specs=[pl.BlockSpec((1, MOD_ROWS, nn), lambda i: (i, 0, 0))],
        out_shape=[jax.ShapeDtypeStruct((nl, MOD_ROWS, nn), F32)], args=[c_rows, w_mod, b_mod], hosted=hosted)[0]


def _mod_bwd_rows(dlat, dctx_parts):
    nl, ne, nn = dlat.shape

    def body(l_ref, c_ref, db_ref, dc_ref):
        dc = c_ref[0, 0:1, :]
        for k in range(1, N_DEV):
            dc = dc + c_ref[0, k:k + 1, :]
        db = dc
        for k in range(ne):
            db = db + l_ref[0, k:k + 1, :]
        db_ref[0] = db
        dc_ref[0] = dc

    return pl.pallas_call(
        body, name="mod_bwd_rows", grid=(nl,),
        in_specs=[pl.BlockSpec((1, ne, nn), lambda i: (i, 0, 0)), pl.BlockSpec((1, N_DEV, nn), lambda i: (i, 0, 0))],
        out_specs=[pl.BlockSpec((1, 1, nn), lambda i: (i, 0, 0))] * 2,
        out_shape=[jax.ShapeDtypeStruct((nl, 1, nn), F32)] * 2,
        compiler_params=pltpu.CompilerParams(dimension_semantics=("arbitrary",)),
    )(dlat, dctx_parts)


def _mod_bwd_w(c_cols, d_rows, w_mod, hosted=None):
    nl, d, nn = w_mod.shape

    def body(c_ref, d_ref, w_ref, dw_ref, dc_ref):
        i = pl.program_id(0)
        c = c_ref[...]
        sg = jax.nn.sigmoid(c)
        s = c * sg
        dv = d_ref[0]
        acc = s[:, 0:1] * dv[0:1, :]
        for r in range(1, CTX_ROW + 1):
            acc = acc + s[:, r:r + 1] * dv[r:r + 1, :]
        dw_ref[0] = acc
        ds_ctx = jnp.sum(w_ref[0] * dv[CTX_ROW:CTX_ROW + 1, :], axis=1, keepdims=True)
        cc, sc = c[:, CTX_ROW:CTX_ROW + 1], sg[:, CTX_ROW:CTX_ROW + 1]
        part = ds_ctx * (sc * (1.0 + cc * (1.0 - sc)))

        @pl.when(i == 0)
        def _():
            dc_ref[...] = part

        @pl.when(i != 0)
        def _():
            dc_ref[...] += part

    return _pcall(
        body, name="mod_bwd_w", grid=(nl,),
        in_specs=[pl.BlockSpec((d, MOD_ROWS), lambda i: (0, 0)), pl.BlockSpec((1, MOD_ROWS, nn), lambda i: (i, 0, 0)),
                  pl.BlockSpec((1, d, nn), lambda i: (i, 0, 0))],
        out_specs=[pl.BlockSpec((1, d, nn), lambda i: (i, 0, 0)), pl.BlockSpec((d, 1), lambda i: (0, 0))],
        out_shape=[jax.ShapeDtypeStruct((nl, d, nn), F32), jax.ShapeDtypeStruct((d, 1), F32)],
        args=[c_cols, d_rows, w_mod], hosted=hosted)


def _pack_rows(arrays, width, row_multiple=8):
    rows, spans, r0 = [], [], 0
    for a in arrays:
        flat = a.reshape(-1)
        nr = -(-flat.shape[0] // width)
        held = -(-nr // 8) * 8
        flat = jnp.pad(flat, (0, held * width - flat.shape[0]))
        rows.append(flat.reshape(held, width))
        spans.append((r0, nr, a.shape))
        r0 += held
    if r0 % row_multiple:
        rows.append(jnp.zeros((row_multiple - r0 % row_multiple, width), F32))
    return jnp.concatenate(rows, axis=0), spans


def _unpack_rows(packed, spans):
    out = []
    for r0, nr, shape in spans:
        out.append(packed[r0:r0 + nr].reshape(-1)[:math.prod(shape)].reshape(shape))
    return out


BIG = {"cv_w_in": 1, "cv_w_out": 0, "pl_w_in": 1, "pl_w_grp": None, "pl_w_out": 0, "ml_w_in": 1, "ml_w_uq": 1,
       "ml_w_ukv": 1, "ml_w_out": 0, "ch_w_in": 1, "ch_w_out": 0}
SMALL_SHARDED = ["cv_dw", "pl_scale", "ml_q_norm", "ml_kv_norm", "ch_ln_g", "ch_ln_b"]
SMALL_REPLICATED = ["c_ctx", "norm_g", "b_mod", "cv_db", "cv_ln_g", "cv_ln_b", "ml_nope_norm", "ml_rope_norm",
                    "ch_w_s", "ch_b_s"]
WEIGHTS = ['c_ctx', 'norm_g', 'w_mod', 'b_mod', 'cv_w_in', 'cv_dw', 'cv_db', 'cv_ln_g', 'cv_ln_b', 'cv_w_out',
           'pl_w_in', 'pl_w_grp', 'pl_scale', 'pl_w_out', 'ml_w_in', 'ml_q_norm', 'ml_kv_norm', 'ml_w_uq', 'ml_w_ukv',
           'ml_nope_norm', 'ml_rope_norm', 'ml_w_out', 'ch_w_in', 'ch_ln_g', 'ch_ln_b', 'ch_w_s', 'ch_b_s', 'ch_w_out']


def _shard2d(name, a):
    if name == "pl_w_grp":
        return a.reshape(a.shape[-3] * a.shape[-2], a.shape[-1])
    return a.reshape(a.shape[-2], a.shape[-1])


def _unstack(name, s):
    if name == "pl_w_grp":
        ng = len(POOL_WINDOWS)
        return s.reshape(N_CHIP, ng, s.shape[1] // ng, s.shape[2]).transpose(1, 0, 2, 3).reshape(ng, -1, s.shape[2])
    if BIG[name] == 0:
        return s.reshape(-1, s.shape[2])
    return s.transpose(1, 0, 2).reshape(s.shape[1], -1)


def _stack(name, g):
    if g.ndim == 3 and name != "pl_w_grp":
        return g
    if name == "pl_w_grp":
        ng = len(POOL_WINDOWS)
        return g.reshape(ng, N_CHIP, -1, g.shape[2]).transpose(1, 0, 2, 3).reshape(N_CHIP, -1, g.shape[2])
    if BIG[name] == 0:
        return g.reshape(N_CHIP, -1, g.shape[1])
    return g.reshape(g.shape[0], N_CHIP, -1).transpose(1, 0, 2)


L0, L1, L2, L3 = LAYER_WEIGHTS
EARLY_SMALL = ("ch_w_s", "ch_b_s", "ch_ln_g", "ch_ln_b")
MESH_SCHEDULE = {
    "ag8_inputs": [("gather", L0[:1])], "mod_fwd": [("gfill", L0[:1])],
    "cv_in_fwd": [("gather", L0[1:]), ("gather", L1[1:])], "conv_fwd": [("gfill", L0[1:]), ("gather", L1[:1])],
    "cv_out_fwd": [("gfill", L1), ("gather", L2[3:])],
    "pl_in_fwd": [("gather", L2[:1])], "pool_fwd": [("gather", L2[1:3])], "pl_out_fwd": [("gfill", L2)],
    "attn_fwd": [("gather", L3)], "ml_out_fwd": [("gfill", L3)],
    "ch_in_bwd": [("small", EARLY_SMALL)],
    "ml_out_bwd": [("swap", L3)], "attn_bwd": [("exch", L3)], "ml_queries_bwd": [("ofill", L3)],
    "pl_out_bwd": [("swap", L2)], "pl_in_bwd": [("exch", L2)],
    "cv_out_bwd": [("swap", L1), ("ofill", L2)], "conv_bwd": [("exch", L1)],
    "ag8_dmod": [("swap", L0)], "mod_bwd_w": [("exch", L0), ("ofill", L1)], "ag8_small_grads": [("ofill", L0)],
}


class _MeshPlan:
    def __init__(self, weights, m, v, core):
        self.W, self.M, self.V, self.core = weights, m, v, core
        self.small = None
        self.stack, self.gstack, self.part, self.half, self.out = {}, {}, {}, {}, {}
        self.notes, self.early = {}, {}
        self.live, self.done = {}, set()

    def _make(self, op, names):
        if op == "gather":
            return _gather_ici([_shard2d(n, self.W[n]).astype(BF16) for n in names])
        if op == "gfill":
            return _sibling_fill([self.stack[n] for n in names], 1, True)
        if op == "swap":
            return _grad_swap_d2d([self.gstack[n] for n in names])
        if op == "exch":
            return _grad_exchange_ici([self.part[n] for n in names])
        if op == "ofill":
            return _sibling_fill([t for n in names for t in self.half[n]], 0, False)
        pack, self.early_spans = _pack_rows([self.notes[n] for n in names], LANES, 128)
        return _ag8_copies(pack)

    def _finish_op(self, op, names, hosted):
        self.done.add((op, names))
        res = hosted.results
        if op in ("gather", "gfill"):
            self.stack.update(zip(names, res))
        elif op == "swap":
            for n, r in zip(names, res):
                self.part[n] = _grad_add_half(self.core, self.gstack[n], r)
        elif op == "exch":
            for n, q in zip(names, res):
                rh = q.shape[1]
                self.half[n] = _adamw("adamw_" + n, self.core * rh, q, _shard2d(n, self.W[n]),
                                      _shard2d(n, self.M[n]), _shard2d(n, self.V[n]), rh)
        elif op == "ofill":
            for k, n in enumerate(names):
                self.out[n] = tuple(r.reshape(self.W[n].shape) for r in res[4 * k:4 * k + 4])
        else:
            self.early.update(zip(names, _unpack_rows(_sum8(res[0]), self.early_spans)))

    def alone(self, op, names):
        hosted = self._make(op, names)
        _run_hosted("%s_%s" % (op, names[0]), hosted)
        self._finish_op(op, names, hosted)

    def weights(self, names):
        wk = {n: _unstack(n, self.stack[n]) for n in names}
        if "pl_w_grp" in wk:
            wk["pl_w_grp"] = wk["pl_w_grp"].astype(F32)
        return _prep_weights(wk)

    def hosted(self, tag):
        self.live[tag] = [(op, names, self._make(op, names)) for op, names in MESH_SCHEDULE.get(tag, [])]
        return _merge_hosted([h for _, _, h in self.live[tag]])

    def after(self, tag):
        for op, names, hosted in self.live.pop(tag, []):
            self._finish_op(op, names, hosted)

    def note(self, values):
        self.notes.update(values)

    def layer_grads(self, layer, grads):
        g = _unprep_grads(grads)
        for n in LAYER_WEIGHTS[layer]:
            self.gstack[n] = _stack(n, g[n])

    def finish(self):
        for names in (L3, L2, L1, L0):
            for op in ("swap", "exch", "ofill"):
                if (op, names) not in self.done:
                    self.alone(op, names)
        return self.out


def kernel(x, c, ctx, c_ctx, norm_g, w_mod, b_mod, cv_w_in, cv_dw, cv_db, cv_ln_g, cv_ln_b, cv_w_out, pl_w_in, pl_w_grp, pl_scale, pl_w_out, ml_w_in, ml_q_norm, ml_kv_norm, ml_w_uq, ml_w_ukv, ml_nope_norm, ml_rope_norm, ml_w_out, ch_w_in, ch_ln_g, ch_ln_b, ch_w_s, ch_b_s, ch_w_out, loss_target, m_c_ctx, m_norm_g, m_w_mod, m_b_mod, m_cv_w_in, m_cv_dw, m_cv_db, m_cv_ln_g, m_cv_ln_b, m_cv_w_out, m_pl_w_in, m_pl_w_grp, m_pl_scale, m_pl_w_out, m_ml_w_in, m_ml_q_norm, m_ml_kv_norm, m_ml_w_uq, m_ml_w_ukv, m_ml_nope_norm, m_ml_rope_norm, m_ml_w_out, m_ch_w_in, m_ch_ln_g, m_ch_ln_b, m_ch_w_s, m_ch_b_s, m_ch_w_out, v_c_ctx, v_norm_g, v_w_mod, v_b_mod, v_cv_w_in, v_cv_dw, v_cv_db, v_cv_ln_g, v_cv_ln_b, v_cv_w_out, v_pl_w_in, v_pl_w_grp, v_pl_scale, v_pl_w_out, v_ml_w_in, v_ml_q_norm, v_ml_kv_norm, v_ml_w_uq, v_ml_w_ukv, v_ml_nope_norm, v_ml_rope_norm, v_ml_w_out, v_ch_w_in, v_ch_ln_g, v_ch_ln_b, v_ch_w_s, v_ch_b_s, v_ch_w_out):
    W = dict(c_ctx=c_ctx, norm_g=norm_g, w_mod=w_mod, b_mod=b_mod, cv_w_in=cv_w_in, cv_dw=cv_dw, cv_db=cv_db, cv_ln_g=cv_ln_g, cv_ln_b=cv_ln_b, cv_w_out=cv_w_out, pl_w_in=pl_w_in, pl_w_grp=pl_w_grp, pl_scale=pl_scale, pl_w_out=pl_w_out, ml_w_in=ml_w_in, ml_q_norm=ml_q_norm, ml_kv_norm=ml_kv_norm, ml_w_uq=ml_w_uq, ml_w_ukv=ml_w_ukv, ml_nope_norm=ml_nope_norm, ml_rope_norm=ml_rope_norm, ml_w_out=ml_w_out, ch_w_in=ch_w_in, ch_ln_g=ch_ln_g, ch_ln_b=ch_ln_b, ch_w_s=ch_w_s, ch_b_s=ch_b_s, ch_w_out=ch_w_out)
    M = dict(c_ctx=m_c_ctx, norm_g=m_norm_g, w_mod=m_w_mod, b_mod=m_b_mod, cv_w_in=m_cv_w_in, cv_dw=m_cv_dw, cv_db=m_cv_db, cv_ln_g=m_cv_ln_g, cv_ln_b=m_cv_ln_b, cv_w_out=m_cv_w_out, pl_w_in=m_pl_w_in, pl_w_grp=m_pl_w_grp, pl_scale=m_pl_scale, pl_w_out=m_pl_w_out, ml_w_in=m_ml_w_in, ml_q_norm=m_ml_q_norm, ml_kv_norm=m_ml_kv_norm, ml_w_uq=m_ml_w_uq, ml_w_ukv=m_ml_w_ukv, ml_nope_norm=m_ml_nope_norm, ml_rope_norm=m_ml_rope_norm, ml_w_out=m_ml_w_out, ch_w_in=m_ch_w_in, ch_ln_g=m_ch_ln_g, ch_ln_b=m_ch_ln_b, ch_w_s=m_ch_w_s, ch_b_s=m_ch_b_s, ch_w_out=m_ch_w_out)
    V = dict(c_ctx=v_c_ctx, norm_g=v_norm_g, w_mod=v_w_mod, b_mod=v_b_mod, cv_w_in=v_cv_w_in, cv_dw=v_cv_dw, cv_db=v_cv_db, cv_ln_g=v_cv_ln_g, cv_ln_b=v_cv_ln_b, cv_w_out=v_cv_w_out, pl_w_in=v_pl_w_in, pl_w_grp=v_pl_w_grp, pl_scale=v_pl_scale, pl_w_out=v_pl_w_out, ml_w_in=v_ml_w_in, ml_q_norm=v_ml_q_norm, ml_kv_norm=v_ml_kv_norm, ml_w_uq=v_ml_w_uq, ml_w_ukv=v_ml_w_ukv, ml_nope_norm=v_ml_nope_norm, ml_rope_norm=v_ml_rope_norm, ml_w_out=v_ml_w_out, ch_w_in=v_ch_w_in, ch_ln_g=v_ch_ln_g, ch_ln_b=v_ch_ln_b, ch_w_s=v_ch_w_s, ch_b_s=v_ch_b_s, ch_w_out=v_ch_w_out)

    batch, lat_len, d = x.shape
    mx, my, mc = _my_place()
    chip = 2 * mx + my
    dev = 2 * chip + mc
    core = jnp.reshape(mc, (1,)).astype(jnp.int32)
    zero_off = jnp.zeros((1,), jnp.int32)
    big_names = list(BIG)

    sw = d // N_CHIP
    small_in = [c] + [jnp.pad(W[n].reshape(-1, W[n].shape[-1]), ((0, 0), (0, sw - W[n].shape[-1])))
                      for n in SMALL_SHARDED]
    pack1, spans1 = _pack_rows(small_in, sw)
    plan = _MeshPlan(W, M, V, core)
    gather1 = _ag8_copies(pack1)
    _run_hosted("ag8_inputs", _merge_hosted([gather1, plan.hosted("ag8_inputs")]))
    plan.after("ag8_inputs")
    got1 = gather1.results[0]
    c_all = got1[:, spans1[0][0]:spans1[0][0] + spans1[0][1]].reshape(N_DEV * batch, d)
    full_small = {}
    for n, (r0, nr, _) in zip(SMALL_SHARDED, spans1[1:]):
        blk = got1[0::2, r0:r0 + nr, :W[n].shape[-1]]
        full_small[n] = blk.transpose(1, 0, 2).reshape(nr, -1)

    c_rows = jnp.concatenate([c_all, c_ctx[None], jnp.zeros((MOD_ROWS - CTX_ROW - 1, d), F32)], axis=0)
    nmod = w_mod.shape[2]
    b_shard = lax.dynamic_slice(b_mod, (0, chip * nmod), (b_mod.shape[0], nmod))[:, None, :]
    mod_shard = _mod_fwd(c_rows, w_mod, b_shard, hosted=plan.hosted("mod_fwd"))
    plan.after("mod_fwd")
    mod_rows = mod_shard.transpose(1, 0, 2).reshape(MOD_ROWS, 1, 4 * nmod)
    got2 = _run_hosted("mod_exchange", _chip_rows_copies(mod_rows, batch, CTX_ROW))[0]
    mod_mine = got2.reshape(N_CHIP, batch + 1, 4, nmod).transpose(2, 1, 0, 3).reshape(4, batch + 1, 3 * d)
    mod_lat, mod_ctx = mod_mine[:, :batch], mod_mine[:, batch]
    mods = []
    for i in range(4):
        mods.append(tuple(
            jnp.stack([mod_lat[i, :, j * d:(j + 1) * d], jnp.broadcast_to(mod_ctx[i, j * d:(j + 1) * d], (batch, d))],
                      axis=1)[:, :, None, :] for j in range(3)))

    wk = dict(full_small)
    wk.update(norm_g=norm_g, cv_db=cv_db, cv_ln_g=cv_ln_g, cv_ln_b=cv_ln_b, ml_nope_norm=ml_nope_norm[0],
              ml_rope_norm=ml_rope_norm[0], ch_w_s=ch_w_s[0], ch_b_s=ch_b_s[0])
    plan.small = _prep_weights(wk)
    xm = jnp.concatenate([x, ctx], axis=1)
    loss_part, grad_x, dmods, g = _local_step(xm, loss_target, mods, plan, lat_len)
    g = _unprep_grads(g)

    lat_rows, ctx_rows = [], []
    for i in range(4):
        dsh, dsc, dgt = dmods[i]
        lat_rows.append(jnp.concatenate([dsh[:, 0, 0], dsc[:, 0, 0], dgt[:, 0, 0]], axis=1))
        zero = jnp.zeros((d,), F32)
        cs = [jnp.sum(t[:, 1, 0], axis=0) if ok else zero
              for t, ok in zip((dsh, dsc, dgt), (i <= 2, i <= 2, i <= 1))]
        ctx_rows.append(jnp.concatenate(cs, axis=0)[None])
    dmod_dev = jnp.concatenate(lat_rows + ctx_rows, axis=0)
    dmod_dev = jnp.pad(dmod_dev, ((0, (-dmod_dev.shape[0]) % 8), (0, 0)))
    gather3 = _ag8_column_copies(dmod_dev, nmod)
    _run_hosted("ag8_dmod", _merge_hosted([gather3, plan.hosted("ag8_dmod")]))
    plan.after("ag8_dmod")
    got3 = gather3.results[0]
    dlat = got3[:, :4 * batch].reshape(N_DEV, 4, batch, nmod).transpose(1, 0, 2, 3).reshape(4, N_DEV * batch, nmod)
    dctx_parts = got3[:, 4 * batch:4 * batch + 4].transpose(1, 0, 2)
    g_b_shard, dctx = _mod_bwd_rows(dlat, dctx_parts)
    d_rows = jnp.concatenate([dlat, dctx, jnp.zeros((4, MOD_ROWS - CTX_ROW - 1, nmod), F32)], axis=1)
    g_w_mod, dcc_part = _mod_bwd_w(c_rows.T, d_rows, w_mod, hosted=plan.hosted("mod_bwd_w"))
    plan.after("mod_bwd_w")

    wm2 = w_mod.reshape(-1, nmod)
    res_mod = _adamw("adamw_w_mod", None, g_w_mod.reshape(1, -1, nmod), wm2, M["w_mod"].reshape(-1, nmod),
                     V["w_mod"].reshape(-1, nmod), wm2.shape[0], hosted=plan.hosted("adamw_w_mod"))
    plan.after("adamw_w_mod")
    out = {"w_mod": tuple(r.reshape(w_mod.shape) for r in res_mod)}

    g_small_in = {n: g[n] for n in SMALL_SHARDED if n not in EARLY_SMALL}
    g_small_in.update(norm_g=g["norm_g"], cv_db=g["cv_db"], cv_ln_g=g["cv_ln_g"], cv_ln_b=g["cv_ln_b"],
                      ml_nope_norm=g["ml_nope_norm"], ml_rope_norm=g["ml_rope_norm"],
                      c_ctx=dcc_part.reshape(-1) * (mc == 0).astype(F32), loss=loss_part,
                      b_mod=lax.dynamic_update_slice(jnp.zeros((N_CHIP, 4, nmod), F32),
                                                     g_b_shard[None, :, 0] * (mc == 0).astype(F32), (chip, 0, 0)))
    small_names = list(g_small_in)
    pack4, spans4 = _pack_rows([g_small_in[n] for n in small_names], LANES, 128)
    gather4 = _ag8_copies(pack4)
    _run_hosted("ag8_small_grads", _merge_hosted([gather4, plan.hosted("ag8_small_grads")]))
    plan.after("ag8_small_grads")
    gs = dict(zip(small_names, _unpack_rows(_sum8(gather4.results[0]), spans4)))
    loss = gs["loss"][0, 0]
    gs.update(plan.early)
    gs["b_mod"] = gs["b_mod"].transpose(1, 0, 2).reshape(4, N_CHIP * nmod)
    for n in SMALL_SHARDED:
        wd = W[n].shape[-1]
        gs[n] = lax.dynamic_slice_in_dim(gs[n], chip * wd, wd, axis=1)
    upd_names = SMALL_REPLICATED + SMALL_SHARDED
    pw, spans_u = _pack_rows([W[n] for n in upd_names], LANES, 128)
    pm, _ = _pack_rows([M[n] for n in upd_names], LANES, 128)
    pv, _ = _pack_rows([V[n] for n in upd_names], LANES, 128)
    pg, _ = _pack_rows([gs[n].reshape(W[n].shape) for n in upd_names], LANES, 128)
    res_small = _adamw("adamw_small", None, pg[None], pw, pm, pv, pw.shape[0], hosted=plan.hosted("adamw_small"))
    plan.after("adamw_small")
    for n, vals in zip(upd_names, zip(*[_unpack_rows(r, spans_u) for r in res_small])):
        out[n] = vals
    out.update(plan.finish())

    outs = [loss, grad_x]
    for j in range(4):
        outs.extend(out[n][j] for n in WEIGHTS)
    return tuple(outs)
```

```python
import functools
import math

import jax
import jax.numpy as jnp
from jax import lax
from jax.experimental import pallas as pl
from jax.experimental.pallas import tpu as pltpu

F32 = jnp.float32
BF16 = jnp.bfloat16
ACT = jnp.float32
MESH = pl.DeviceIdType.MESH

EPS = 1e-6
GRID_W = 64
CONV_WIDTH = 31
CONV_HALF = CONV_WIDTH // 2
CONV_PAD = 16
POOL_WINDOWS = (2, 4, 8, 16)
POOL_HALF = max(POOL_WINDOWS) // 2
HEADS = 8
NOPE = 128
ROPE = 64
HEAD_W = 256
VDIM = 128
KV_RANK = 256
Q_RANK = 384
ATT_SCALE = (NOPE + ROPE) ** -0.5
LN2 = math.log(2.0)
Q_PRESCALE = ATT_SCALE / LN2
ROPE_THETA = 10000.0
CHUNK = 128
CHUNK_GROUPS = 8
LANES = 128
TM = 256
TM_LATENT = 512
TQ = 1024
TQ_BWD = 2048
ATT_RQ = 128
ATT_KC = 256
VMEM_LIMIT = 60 * 1024 * 1024

ADAM_LR = 0.001
ADAM_B1 = 0.9
ADAM_B2 = 0.999
ADAM_EPS = 1e-08
ADAM_WD = 0.01
ADAM_STEP = 10


def _dot(a, b):
    return jnp.dot(a.astype(BF16), b.astype(BF16), preferred_element_type=F32)


def _dot_nt(a, b):
    return lax.dot_general(a.astype(BF16), b.astype(BF16), (((1,), (1,)), ((), ())), preferred_element_type=F32)


def _dot_tn(a, b):
    return lax.dot_general(a.astype(BF16), b.astype(BF16), (((0,), (0,)), ((), ())), preferred_element_type=F32)


@jax.custom_vjp
def _mm(a, w):
    return _dot(a, w)


def _mm_fwd(a, w):
    return _dot(a, w), (a, w)


def _mm_bwd(res, ct):
    a, w = res
    return _dot_nt(ct, w), _dot_tn(a, ct)


_mm.defvjp(_mm_fwd, _mm_bwd)


def _swap16_impl(x):
    n = x.shape[-1]
    ax = x.ndim - 1
    lane = lax.broadcasted_iota(jnp.int32, x.shape, ax)
    up = pltpu.roll(x, n - 16, ax)
    dn = pltpu.roll(x, 16, ax)
    return jnp.where((lane % 32) < 16, up, dn)


@jax.custom_vjp
def _swap16(x):
    return _swap16_impl(x)


_swap16.defvjp(lambda x: (_swap16_impl(x), None), lambda _, ct: (_swap16_impl(ct),))


def _rms(x, g, n=None):
    n = x.shape[-1] if n is None else n
    return x * lax.rsqrt(jnp.sum(x * x, axis=-1, keepdims=True) * (1.0 / n) + EPS) * g


def _layernorm(x, g, b):
    mu = jnp.mean(x, axis=-1, keepdims=True)
    xc = x - mu
    var = jnp.mean(xc * xc, axis=-1, keepdims=True)
    return xc * lax.rsqrt(var + EPS) * g + b


def _silu(x):
    return x * jax.nn.sigmoid(x)


def _rope(x, cos, sin):
    return x * cos + _swap16(x) * sin


ANY = pl.BlockSpec(memory_space=pl.ANY)


class _Hosted:
    def __init__(self, arrays, out_shapes, sems, start, wait, aliases=None):
        self.arrays, self.out_shapes, self.sems = list(arrays), list(out_shapes), list(sems)
        self.start, self.wait, self.aliases = start, wait, dict(aliases or {})
        self.results = None


def _merge_hosted(parts):
    parts = [p for p in parts if p is not None]
    if not parts:
        return None
    if len(parts) == 1:
        return parts[0]
    offs, a0, o0, s0 = [], 0, 0, 0
    for p in parts:
        offs.append((a0, o0, s0))
        a0, o0, s0 = a0 + len(p.arrays), o0 + len(p.out_shapes), s0 + len(p.sems)

    def run(which):
        def f(ins, outs, sems):
            for p, (a, o, s) in zip(parts, offs):
                getattr(p, which)(ins[a:a + len(p.arrays)], outs[o:o + len(p.out_shapes)], sems[s:s + len(p.sems)])
        return f

    aliases = {}
    for p, (a, o, _) in zip(parts, offs):
        aliases.update({a + i: o + j for i, j in p.aliases.items()})
    merged = _Hosted(sum((p.arrays for p in parts), []), sum((p.out_shapes for p in parts), []),
                     sum((p.sems for p in parts), []), run("start"), run("wait"), aliases)
    merged.parts, merged.offs = parts, offs
    return merged


def _deliver(hosted, results):
    hosted.results = list(results)
    for p, (_, o, _) in zip(getattr(hosted, "parts", []), getattr(hosted, "offs", [])):
        _deliver(p, results[o:o + len(p.out_shapes)])


def _pcall(body, *, name, grid, in_specs, out_specs, out_shape, args, hosted=None, vmem_limit=True, scratch=()):
    n_in, n_out, n_scr = len(args), len(out_shape), len(scratch)
    kwargs = dict(scratch_shapes=list(scratch)) if scratch else {}
    if hosted is not None:
        nhi, nho, inner = len(hosted.arrays), len(hosted.out_shapes), body

        def body(*refs):
            ins, hin = refs[:n_in], refs[n_in:n_in + nhi]
            outs, hout = refs[n_in + nhi:n_in + nhi + n_out], refs[n_in + nhi + n_out:n_in + nhi + n_out + nho]
            own = refs[n_in + nhi + n_out + nho:n_in + nhi + n_out + nho + n_scr]
            sems = refs[n_in + nhi + n_out + nho + n_scr:]
            first, last = None, None
            for k, g in enumerate(grid):
                f, l = pl.program_id(k) == 0, pl.program_id(k) == g - 1
                first = f if first is None else jnp.logical_and(first, f)
                last = l if last is None else jnp.logical_and(last, l)

            @pl.when(first)
            def _():
                hosted.start(hin, hout, sems)

            inner(*ins, *outs, *own)

            @pl.when(last)
            def _():
                hosted.wait(hin, hout, sems)

        in_specs = list(in_specs) + [ANY] * nhi
        out_specs = list(out_specs) + [ANY] * nho
        out_shape = list(out_shape) + hosted.out_shapes
        args = list(args) + hosted.arrays
        kwargs = dict(scratch_shapes=list(scratch) + hosted.sems,
                      input_output_aliases={n_in + i: n_out + j for i, j in hosted.aliases.items()})
    params = dict(dimension_semantics=("arbitrary",) * len(grid))
    if vmem_limit:
        params["vmem_limit_bytes"] = VMEM_LIMIT
    res = pl.pallas_call(body, name=name, grid=grid, in_specs=list(in_specs), out_specs=list(out_specs),
                         out_shape=list(out_shape), compiler_params=pltpu.CompilerParams(**params), **kwargs)(*args)
    if hosted is not None:
        _deliver(hosted, res[n_out:])
    return list(res[:n_out])


def _run_hosted(name, hosted):
    nhi, nho = len(hosted.arrays), len(hosted.out_shapes)

    def body(*refs):
        ins, outs, sems = refs[:nhi], refs[nhi:nhi + nho], refs[nhi + nho:]
        hosted.start(ins, outs, sems)
        hosted.wait(ins, outs, sems)

    res = pl.pallas_call(body, name=name, in_specs=[ANY] * nhi, out_specs=[ANY] * nho, out_shape=hosted.out_shapes,
                         scratch_shapes=hosted.sems, input_output_aliases=hosted.aliases)(*hosted.arrays)
    _deliver(hosted, res)
    return list(res)


def _const_spec(shape, single=False):
    nd = len(shape)
    if single:
        return pl.BlockSpec(shape, lambda b, i: (0,) * nd, pipeline_mode=pl.Buffered(1))
    return pl.BlockSpec(shape, lambda b, i: (0,) * nd)


def _tile_spec(arr, n_lat_tiles, lat_only=False, tm=TM):
    bt, _, cw = arr.shape
    if lat_only:
        return pl.BlockSpec((1, tm, cw), lambda b, i: (b if bt > 1 else 0, jnp.minimum(i, n_lat_tiles - 1), 0))
    return pl.BlockSpec((1, tm, cw), lambda b, i: (b if bt > 1 else 0, i, 0))


def _eparam_spec(arr, n_lat_tiles):
    cw = arr.shape[-1]
    return pl.BlockSpec((1, 1, 1, cw), lambda b, i: (b, (i >= n_lat_tiles).astype(jnp.int32), 0, 0))


def _stage_fwd(name, *, pre, post, wsel, splits, tiles, eparams, sparams, weights, out_widths, out_dtypes,
               batch, n_tiles, n_lat_tiles, hosted=None, tm=TM):
    nt, ne, ns, nw = len(tiles), len(eparams), len(sparams), len(weights)

    def body(*refs):
        t_refs = refs[:nt]
        e_refs = refs[nt:nt + ne]
        s_refs = refs[nt + ne:nt + ne + ns]
        w_refs = refs[nt + ne + ns:nt + ne + ns + nw]
        o_refs = refs[nt + ne + ns + nw:]
        tv = [r[0].astype(F32) for r in t_refs]
        ev = [r[0, 0] for r in e_refs]
        sv = [r[...] for r in s_refs]
        a = pre(tv, ev, sv)
        z = [_dot(a[wsel[j]], w_refs[j][...]) for j in range(nw)]
        if post is None:
            outs = [z[j][:, s:s + w] for (j, s, w) in splits]
        else:
            outs = post(z, tv, ev, sv)
        for o_ref, o in zip(o_refs, outs):
            o_ref[0] = o.astype(o_ref.dtype)

    in_specs = ([_tile_spec(t, n_lat_tiles, tm=tm) for t in tiles] + [_eparam_spec(e, n_lat_tiles) for e in eparams]
                + [_const_spec(s.shape) for s in sparams] + [_const_spec(w.shape, single=True) for w in weights])
    out_shape = [jax.ShapeDtypeStruct((batch, n_tiles * tm, w), dt) for w, dt in zip(out_widths, out_dtypes)]
    out_specs = [pl.BlockSpec((1, tm, w), lambda b, i: (b, i, 0)) for w in out_widths]
    return _pcall(body, name=name, grid=(batch, n_tiles), in_specs=in_specs, out_specs=out_specs,
                  out_shape=out_shape, args=[*tiles, *eparams, *sparams, *weights], hosted=hosted)


def _stage_bwd(name, *, pre, post, wsel, splits, tiles, tile_diff, eparams, sparams, weights, cots, cot_lat_only,
               batch, n_tiles, n_lat_tiles, add=None, add_lat_only=False, hosted=None, w_col_stack=None,
               dt_lat_only=False, tm=TM):
    nt, ne, ns, nw, nc = len(tiles), len(eparams), len(sparams), len(weights), len(cots)
    diff_idx = [k for k in range(nt) if tile_diff[k]]
    nd = len(diff_idx)
    has_add = add is not None
    w_col_stack = w_col_stack or [None] * nw

    def body(*refs):
        pos = 0
        t_refs = refs[pos:pos + nt]; pos += nt
        e_refs = refs[pos:pos + ne]; pos += ne
        s_refs = refs[pos:pos + ns]; pos += ns
        w_refs = refs[pos:pos + nw]; pos += nw
        c_refs = refs[pos:pos + nc]; pos += nc
        if has_add:
            add_ref = refs[pos]; pos += 1
        dt_refs = refs[pos:pos + nd]; pos += nd
        de_refs = refs[pos:pos + ne]; pos += ne
        ds_refs = refs[pos:pos + ns]; pos += ns
        dw_refs = refs[pos:pos + nw]; pos += nw

        b = pl.program_id(0)
        i = pl.program_id(1)
        is_lat = i < n_lat_tiles
        tv = [r[0].astype(F32) for r in t_refs]
        ev = tuple(r[0, 0] for r in e_refs)
        sv = tuple(r[...] for r in s_refs)
        dv0 = tuple(tv[k] for k in diff_idx)

        def merge(dv):
            full = list(tv)
            for k, v in zip(diff_idx, dv):
                full[k] = v
            return full

        def pre_f(dv, ev_, sv_):
            return tuple(pre(merge(dv), list(ev_), list(sv_)))

        a, vjp_pre = jax.vjp(pre_f, dv0, ev, sv)
        cv = []
        for c_ref, lat in zip(c_refs, cot_lat_only):
            c = c_ref[0].astype(F32)
            cv.append(jnp.where(is_lat, c, 0.0) if lat else c)
        if post is None:
            dz = []
            for j in range(nw):
                parts = [cv[k] for k, (jj, _, _) in enumerate(splits) if jj == j]
                dz.append(parts[0] if len(parts) == 1 else jnp.concatenate(parts, axis=1))
            dt2 = de2 = ds2 = None
        else:
            z = tuple(_dot(a[wsel[j]], w_refs[j][...]) for j in range(nw))

            def post_f(z_, dv, ev_, sv_):
                return tuple(post(list(z_), merge(dv), list(ev_), list(sv_)))

            _, vjp_post = jax.vjp(post_f, z, dv0, ev, sv)
            dz, dt2, de2, ds2 = vjp_post(tuple(cv))
        da = [None] * len(a)
        dws = []
        for j in range(nw):
            g = _dot_nt(dz[j], w_refs[j][...])
            da[wsel[j]] = g if da[wsel[j]] is None else da[wsel[j]] + g
            dws.append(_dot_tn(a[wsel[j]], dz[j]))
        da = tuple(jnp.zeros_like(a[k]) if da[k] is None else da[k] for k in range(len(a)))
        dt1, de1, ds1 = vjp_pre(da)

        def plus(u, v):
            return u if v is None else u + v

        for k in range(nd):
            val = plus(dt1[k], None if dt2 is None else dt2[k])
            if has_add and k == 0:
                addv = add_ref[0].astype(F32)
                val = val + (jnp.where(is_lat, addv, 0.0) if add_lat_only else addv)
            if dt_lat_only:
                @pl.when(is_lat)
                def _(k=k, val=val):
                    dt_refs[k][0] = val.astype(dt_refs[k].dtype)
            else:
                dt_refs[k][0] = val.astype(dt_refs[k].dtype)

        seg_first = jnp.logical_or(i == 0, i == n_lat_tiles)
        for k in range(ne):
            val = plus(de1[k], None if de2 is None else de2[k])

            @pl.when(seg_first)
            def _(k=k, val=val):
                de_refs[k][0, 0] = val

            @pl.when(jnp.logical_not(seg_first))
            def _(k=k, val=val):
                de_refs[k][0, 0] += val

        first = jnp.logical_and(b == 0, i == 0)
        acc = [(ds_refs[k], plus(ds1[k], None if ds2 is None else ds2[k])) for k in range(ns)]
        for j in range(nw):
            if w_col_stack[j]:
                cw = dws[j].shape[1] // w_col_stack[j]
                acc += [(dw_refs[j].at[c], dws[j][:, c * cw:(c + 1) * cw]) for c in range(w_col_stack[j])]
            else:
                acc.append((dw_refs[j], dws[j]))
        for ref, val in acc:
            @pl.when(first)
            def _(ref=ref, val=val):
                ref[...] = val

            @pl.when(jnp.logical_not(first))
            def _(ref=ref, val=val):
                ref[...] += val

    in_specs = ([_tile_spec(t, n_lat_tiles, tm=tm) for t in tiles] + [_eparam_spec(e, n_lat_tiles) for e in eparams]
                + [_const_spec(s.shape) for s in sparams] + [_const_spec(w.shape, single=True) for w in weights]
                + [_tile_spec(c, n_lat_tiles, lat, tm) for c, lat in zip(cots, cot_lat_only)])
    args = [*tiles, *eparams, *sparams, *weights, *cots]
    if has_add:
        in_specs.append(_tile_spec(add, n_lat_tiles, add_lat_only, tm))
        args.append(add)
    dt_tiles = n_lat_tiles if dt_lat_only else n_tiles
    out_shape = [jax.ShapeDtypeStruct((batch, dt_tiles * tm, tiles[k].shape[-1]), F32) for k in diff_idx]
    out_specs = [pl.BlockSpec((1, tm, tiles[k].shape[-1]), lambda b, i: (b, jnp.minimum(i, dt_tiles - 1), 0))
                 for k in diff_idx]
    out_shape += [jax.ShapeDtypeStruct(e.shape, F32) for e in eparams]
    out_specs += [_eparam_spec(e, n_lat_tiles) for e in eparams]
    out_shape += [jax.ShapeDtypeStruct(s.shape, F32) for s in sparams]
    out_specs += [_const_spec(s.shape) for s in sparams]
    dw_shapes = [(n, w.shape[0], w.shape[1] // n) if n else w.shape for w, n in zip(weights, w_col_stack)]
    out_shape += [jax.ShapeDtypeStruct(s, F32) for s in dw_shapes]
    out_specs += [_const_spec(s, single=True) for s in dw_shapes]
    res = _pcall(body, name=name, grid=(batch, n_tiles), in_specs=in_specs, out_specs=out_specs,
                 out_shape=out_shape, args=args, hosted=hosted)
    return res[:nd], res[nd:nd + ne], res[nd + ne:nd + ne + ns], res[nd + ne + ns:]


def _pre_adaln(tv, ev, sv):
    x = tv[0]
    sh, sc = ev[0], ev[1]
    return [_rms(x, sv[0]) * (1.0 + sc) + sh]


def _post_residual(x_index):
    def post(z, tv, ev, sv):
        return [tv[x_index] + ev[-1] * z[0]]
    return post


def _pre_conv_out(tv, ev, sv):
    c1, gg = tv[0], tv[1]
    return [_silu(_layernorm(c1, sv[0], sv[1])) * _silu(gg)]


def _pre_pool_out(tv, ev, sv):
    pooled, gg = tv[0], tv[1]
    w_grp, scale = sv[0], sv[1]
    gw = w_grp.shape[-1]
    y = jnp.concatenate([_mm(pooled[:, k * gw:(k + 1) * gw], w_grp[k]) for k in range(w_grp.shape[0])], axis=1)
    return [y * scale * _silu(gg)]


def _pre_rms_only(tv, ev, sv):
    return [_rms(tv[0], sv[0])]


def _post_mla_keys(z, tv, ev, sv):
    krp, cos, sin = tv[1], tv[2], tv[3]
    nope_g, rope_g = sv[1], sv[2]
    kv = z[0]
    kr = _rope(_rms(krp, rope_g, ROPE), cos, sin)
    ks, vs = [], []
    for h in range(HEADS):
        ks.append(_rms(kv[:, h * 2 * NOPE:h * 2 * NOPE + NOPE], nope_g))
        ks.append(kr)
        vs.append(kv[:, h * 2 * NOPE + NOPE:(h + 1) * 2 * NOPE])
    return [jnp.concatenate(ks, axis=1), jnp.concatenate(vs, axis=1)]


def _post_mla_queries(z, tv, ev, sv):
    cos, sin = tv[1], tv[2]
    nope_g, rope_g = sv[1], sv[2]
    q = z[0]
    qs = []
    for h in range(HEADS):
        qs.append(_rms(q[:, h * HEAD_W:h * HEAD_W + NOPE], nope_g))
        qs.append(_rope(_rms(q[:, h * HEAD_W + NOPE:(h + 1) * HEAD_W], rope_g, ROPE), cos, sin))
    return [jnp.concatenate(qs, axis=1) * Q_PRESCALE]


def _pre_mla_out(tv, ev, sv):
    return [tv[0] * _silu(tv[1])]


def _pre_chunk_out(tv, ev, sv):
    u, v, gg = tv[0], tv[1], tv[2]
    ln_g, ln_b, w_s, b_s = sv
    vn = _layernorm(v, ln_g, ln_b)
    rows = []
    for n in range(vn.shape[0] // CHUNK):
        blk = vn[n * CHUNK:(n + 1) * CHUNK]
        cols = [_mm(w_s[g], blk[:, g * LANES:(g + 1) * LANES]) + b_s[:, g:g + 1] for g in range(CHUNK_GROUPS)]
        rows.append(jnp.concatenate(cols, axis=1))
    s = jnp.concatenate(rows, axis=0)
    return [u * s * _silu(gg)]


def _segments(lat_len, tot_len):
    segs = [(0, lat_len)]
    if tot_len > lat_len:
        segs.append((lat_len, tot_len - lat_len))
    return segs


def _pad_rows(x):
    z = jnp.zeros((CONV_PAD, x.shape[1]), x.dtype)
    return jnp.concatenate([z, x, z], axis=0)


def _shifted(xp, j):
    n = xp.shape[0] - 2 * CONV_PAD
    if j != 0:
        xp = pltpu.roll(xp, (-j) % xp.shape[0], 0)
    return xp[CONV_PAD:CONV_PAD + n]


def _conv_fwd(a, bgate, dw, db, lat_len, hosted=None):
    batch, tot, e = a.shape
    segs = _segments(lat_len, tot)

    def body(a_ref, b_ref, dw_ref, db_ref, o_ref):
        w = dw_ref[...]
        for (s0, n) in segs:
            y = a_ref[0, s0:s0 + n, :].astype(F32) * jax.nn.sigmoid(b_ref[0, s0:s0 + n, :].astype(F32))
            yp = _pad_rows(y)
            acc = jnp.zeros_like(y) + db_ref[...]
            for k in range(CONV_WIDTH):
                acc = acc + _shifted(yp, k - CONV_HALF) * w[k:k + 1, :]
            o_ref[0, s0:s0 + n, :] = acc.astype(o_ref.dtype)

    blk = pl.BlockSpec((1, tot, LANES), lambda b, cb: (b, 0, cb))
    return _pcall(
        body, name="conv_fwd", grid=(batch, e // LANES),
        in_specs=[blk, blk, pl.BlockSpec((CONV_WIDTH, LANES), lambda b, cb: (0, cb)),
                  pl.BlockSpec((1, LANES), lambda b, cb: (0, cb))],
        out_specs=[blk], out_shape=[jax.ShapeDtypeStruct(a.shape, ACT)], args=[a, bgate, dw, db], hosted=hosted)[0]


def _conv_bwd(a, bgate, dw, dc1, lat_len, hosted=None):
    batch, tot, e = a.shape
    segs = _segments(lat_len, tot)

    def body(a_ref, b_ref, dw_ref, dc_ref, da_ref, dg_ref, ddw_ref, ddb_ref):
        b = pl.program_id(1)
        w = dw_ref[...]
        ddw_rows = [None] * CONV_WIDTH
        ddb = None
        for (s0, n) in segs:
            av = a_ref[0, s0:s0 + n, :].astype(F32)
            sg = jax.nn.sigmoid(b_ref[0, s0:s0 + n, :].astype(F32))
            y = av * sg
            dc = dc_ref[0, s0:s0 + n, :]
            yp, dcp = _pad_rows(y), _pad_rows(dc)
            dy = jnp.zeros_like(y)
            for k in range(CONV_WIDTH):
                j = k - CONV_HALF
                dy = dy + _shifted(dcp, -j) * w[k:k + 1, :]
                r = jnp.sum(dc * _shifted(yp, j), axis=0, keepdims=True)
                ddw_rows[k] = r if ddw_rows[k] is None else ddw_rows[k] + r
            r = jnp.sum(dc, axis=0, keepdims=True)
            ddb = r if ddb is None else ddb + r
            da_ref[0, s0:s0 + n, :] = dy * sg
            dg_ref[0, s0:s0 + n, :] = dy * av * sg * (1.0 - sg)

        @pl.when(b == 0)
        def _():
            ddw_ref[...] = jnp.zeros_like(ddw_ref)
            ddb_ref[...] = jnp.zeros_like(ddb_ref)

        for k in range(CONV_WIDTH):
            ddw_ref[k:k + 1, :] += ddw_rows[k]
        ddb_ref[...] += ddb

    blk = pl.BlockSpec((1, tot, LANES), lambda cb, b: (b, 0, cb))
    wspec = pl.BlockSpec((CONV_WIDTH, LANES), lambda cb, b: (0, cb))
    bspec = pl.BlockSpec((1, LANES), lambda cb, b: (0, cb))
    return _pcall(
        body, name="conv_bwd", grid=(e // LANES, batch),
        in_specs=[blk, blk, wspec, blk],
        out_specs=[blk, blk, wspec, bspec],
        out_shape=[jax.ShapeDtypeStruct(a.shape, F32), jax.ShapeDtypeStruct(a.shape, F32),
                   jax.ShapeDtypeStruct((CONV_WIDTH, e), F32), jax.ShapeDtypeStruct((1, e), F32)],
        args=[a, bgate, dw, dc1], hosted=hosted)


def _pool_counts(n, half, shape):
    t = lax.broadcasted_iota(jnp.int32, shape, 0)
    cnt = jnp.minimum(t + half, n) - jnp.maximum(t - half, 0)
    return cnt.astype(F32)


def _per_group(fn):
    for k, window in enumerate(POOL_WINDOWS):
        @pl.when(pl.program_id(1) == k)
        def _(window=window):
            fn(window // 2)


def _pool_fwd(v, lat_len, hosted=None):
    batch, tot, e = v.shape
    gw = e // len(POOL_WINDOWS)
    segs = _segments(lat_len, tot)

    def body(v_ref, o_ref):
        def group(half):
            for (s0, n) in segs:
                x = v_ref[0, s0:s0 + n, :]
                xp = _pad_rows(x)
                acc = _shifted(xp, -half)
                for j in range(-half + 1, half):
                    acc = acc + _shifted(xp, j)
                o_ref[0, s0:s0 + n, :] = (acc / _pool_counts(n, half, x.shape) - x).astype(o_ref.dtype)

        _per_group(group)

    blk = pl.BlockSpec((1, tot, gw), lambda b, g: (b, 0, g))
    return _pcall(body, name="pool_fwd", grid=(batch, len(POOL_WINDOWS)), in_specs=[blk], out_specs=[blk],
                  out_shape=[jax.ShapeDtypeStruct(v.shape, ACT)], args=[v], hosted=hosted)[0]


def _pool_bwd(dp, lat_len):
    batch, tot, e = dp.shape
    gw = e // len(POOL_WINDOWS)
    segs = _segments(lat_len, tot)

    def body(d_ref, o_ref):
        def group(half):
            for (s0, n) in segs:
                d = d_ref[0, s0:s0 + n, :]
                dnp = _pad_rows(d / _pool_counts(n, half, d.shape))
                acc = _shifted(dnp, half)
                for j in range(-half + 1, half):
                    acc = acc + _shifted(dnp, -j)
                o_ref[0, s0:s0 + n, :] = acc - d

        _per_group(group)

    blk = pl.BlockSpec((1, tot, gw), lambda b, g: (b, 0, g))
    return pl.pallas_call(
        body, name="pool_bwd", grid=(batch, len(POOL_WINDOWS)), in_specs=[blk], out_specs=blk,
        out_shape=jax.ShapeDtypeStruct(dp.shape, F32),
        compiler_params=pltpu.CompilerParams(dimension_semantics=("arbitrary", "arbitrary"),
                                             vmem_limit_bytes=VMEM_LIMIT),
    )(dp)


def _attn_fwd(q, k, v, hosted=None):
    batch, lq, _ = q.shape
    tk = k.shape[1]
    tq = min(TQ, lq)

    def body(q_ref, k_ref, v_ref, o_ref, lse_ref):
        s2 = _dot_nt(q_ref[0], k_ref[0])
        m2 = jnp.max(s2, axis=-1, keepdims=True)
        e = jnp.exp2(s2 - m2)
        l = jnp.sum(e, axis=-1, keepdims=True)
        o_ref[0] = (_dot(e, v_ref[0]) / l).astype(o_ref.dtype)
        lse_ref[0, 0] = m2 + jnp.log2(l)

    return _pcall(
        body, name="attn_fwd", grid=(batch, HEADS, lq // tq),
        in_specs=[pl.BlockSpec((1, tq, HEAD_W), lambda b, h, i: (b, i, h)),
                  pl.BlockSpec((1, tk, HEAD_W), lambda b, h, i: (b, 0, h)),
                  pl.BlockSpec((1, tk, VDIM), lambda b, h, i: (b, 0, h))],
        out_specs=[pl.BlockSpec((1, tq, VDIM), lambda b, h, i: (b, i, h)),
                   pl.BlockSpec((1, 1, tq, 1), lambda b, h, i: (b, h, i, 0))],
        out_shape=[jax.ShapeDtypeStruct((batch, lq, HEADS * VDIM), ACT),
                   jax.ShapeDtypeStruct((batch, HEADS, lq, 1), F32)], args=[q, k, v], hosted=hosted)


def _attn_bwd(q, k, v, o, lse, do, hosted=None):
    batch, lq, _ = q.shape
    tk = k.shape[1]
    tq = min(TQ_BWD, lq)

    def body(q_ref, k_ref, v_ref, o_ref, lse_ref, do_ref, dq_ref, dk_ref, dv_ref, p_scr, ds_scr):
        i = pl.program_id(2)
        nr = tq // ATT_RQ
        rows = [slice(r * ATT_RQ, (r + 1) * ATT_RQ) for r in range(nr)]
        qv = [q_ref[0, rw, :] for rw in rows]
        dob = [do_ref[0, rw, :].astype(BF16) for rw in rows]
        row_lse = [lse_ref[0, 0, rw, :] for rw in rows]
        delta = [jnp.sum(do_ref[0, rw, :] * o_ref[0, rw, :], axis=-1, keepdims=True) for rw in rows]
        for c in range(tk // ATT_KC):
            keys = slice(c * ATT_KC, (c + 1) * ATT_KC)
            kc, vc = k_ref[0, keys, :], v_ref[0, keys, :]
            for r in range(nr):
                p = jnp.exp2(_dot_nt(qv[r], kc) - row_lse[r])
                dp = _dot_nt(dob[r], vc)
                p_scr[rows[r], keys] = p.astype(BF16)
                ds_scr[rows[r], keys] = (p * (dp - delta[r]) * LN2).astype(BF16)
        dq_ref[0] = _dot(ds_scr[...], k_ref[0])
        dk = _dot_tn(ds_scr[...], q_ref[0])
        dv = _dot_tn(p_scr[...], do_ref[0])

        @pl.when(i == 0)
        def _():
            dk_ref[0] = dk
            dv_ref[0] = dv

        @pl.when(i != 0)
        def _():
            dk_ref[0] += dk
            dv_ref[0] += dv

    return _pcall(
        body, name="attn_bwd", grid=(batch, HEADS, lq // tq),
        in_specs=[pl.BlockSpec((1, tq, HEAD_W), lambda b, h, i: (b, i, h)),
                  pl.BlockSpec((1, tk, HEAD_W), lambda b, h, i: (b, 0, h)),
                  pl.BlockSpec((1, tk, VDIM), lambda b, h, i: (b, 0, h)),
                  pl.BlockSpec((1, tq, VDIM), lambda b, h, i: (b, i, h)),
                  pl.BlockSpec((1, 1, tq, 1), lambda b, h, i: (b, h, i, 0)),
                  pl.BlockSpec((1, tq, VDIM), lambda b, h, i: (b, i, h))],
        out_specs=[pl.BlockSpec((1, tq, HEAD_W), lambda b, h, i: (b, i, h)),
                   pl.BlockSpec((1, tk, HEAD_W), lambda b, h, i: (b, 0, h)),
                   pl.BlockSpec((1, tk, VDIM), lambda b, h, i: (b, 0, h))],
        out_shape=[jax.ShapeDtypeStruct(q.shape, F32), jax.ShapeDtypeStruct(k.shape, F32),
                   jax.ShapeDtypeStruct(v.shape, F32)],
        args=[q, k, v, o, lse, do], hosted=hosted,
        scratch=[pltpu.VMEM((tq, tk), BF16), pltpu.VMEM((tq, tk), BF16)])


def _loss_kernel(y, target):
    batch, lq, d = y.shape

    def body(y_ref, t_ref, l_ref, dy_ref):
        first = jnp.logical_and(pl.program_id(0) == 0, pl.program_id(1) == 0)
        err = y_ref[0] - t_ref[0]
        dy_ref[0] = err * (1.0 / d)
        part = jnp.zeros((1, LANES), F32) + jnp.sum(err * err) * (0.5 / d)

        @pl.when(first)
        def _():
            l_ref[...] = part

        @pl.when(jnp.logical_not(first))
        def _():
            l_ref[...] += part

    blk = pl.BlockSpec((1, TM, d), lambda b, i: (b, i, 0))
    return pl.pallas_call(
        body, name="loss_head", grid=(batch, lq // TM), in_specs=[blk, blk],
        out_specs=[pl.BlockSpec((1, LANES), lambda b, i: (0, 0)), blk],
        out_shape=[jax.ShapeDtypeStruct((1, LANES), F32), jax.ShapeDtypeStruct(y.shape, F32)],
        compiler_params=pltpu.CompilerParams(dimension_semantics=("arbitrary", "arbitrary")),
    )(y, target)


def _rope_tables(lat_len, ctx_len):
    rows = lat_len // GRID_W
    axis_dim = ROPE // 2
    freqs = ROPE_THETA ** (-jnp.arange(0, axis_dim, 2, dtype=F32) / axis_dim)
    ar = jnp.arange(rows, dtype=F32)[:, None] * freqs
    ac = jnp.arange(GRID_W, dtype=F32)[:, None] * freqs
    small = lax.optimization_barrier((jnp.cos(ar), jnp.sin(ar), jnp.cos(ac), jnp.sin(ac)))
    cr, sr = (jnp.repeat(t, GRID_W, axis=0) for t in small[:2])
    cc, sc = (jnp.tile(t, (rows, 1)) for t in small[2:])
    pad = jnp.zeros((lat_len, LANES - ROPE), F32)
    cos = jnp.concatenate([cr, cr, cc, cc, pad], axis=1)
    sin = jnp.concatenate([-sr, sr, -sc, sc, pad], axis=1)
    ident = jnp.concatenate([jnp.ones((ctx_len, ROPE), F32), jnp.zeros((ctx_len, LANES - ROPE), F32)], axis=1)
    cos = jnp.concatenate([cos, ident], axis=0)
    sin = jnp.concatenate([sin, jnp.zeros((ctx_len, LANES), F32)], axis=0)
    return cos[None], sin[None]


def _prep_weights(w):
    p = dict(w)
    kvc = KV_RANK + ROPE
    if "ml_w_in" in w:
        wi = w["ml_w_in"]
        p["ml_w_in"] = jnp.concatenate(
            [wi[:, :kvc], jnp.zeros((wi.shape[0], LANES - ROPE), wi.dtype), wi[:, kvc:]], axis=1)
    if "ml_w_uq" in w:
        uq = w["ml_w_uq"].reshape(Q_RANK, HEADS, NOPE + ROPE)
        p["ml_w_uq"] = jnp.pad(uq, ((0, 0), (0, 0), (0, HEAD_W - NOPE - ROPE))).reshape(Q_RANK, HEADS * HEAD_W)
    if "ml_rope_norm" in w:
        p["ml_rope_norm"] = jnp.pad(w["ml_rope_norm"], ((0, 0), (0, LANES - ROPE)))
    return p


def _unprep_grads(g):
    out = dict(g)
    kvc = KV_RANK + ROPE
    if "ml_w_in" in g:
        wi = g["ml_w_in"]
        out["ml_w_in"] = jnp.concatenate([wi[:, :kvc], wi[:, kvc + LANES - ROPE:]], axis=1)
    if "ml_w_uq" in g:
        uq = g["ml_w_uq"].reshape(Q_RANK, HEADS, HEAD_W)
        out["ml_w_uq"] = uq[:, :, :NOPE + ROPE].reshape(Q_RANK, HEADS * (NOPE + ROPE))
    if "ml_rope_norm" in g:
        out["ml_rope_norm"] = g["ml_rope_norm"][:, :ROPE]
    return out


LAYER_WEIGHTS = (("cv_w_in", "cv_w_out"), ("pl_w_in", "pl_w_grp", "pl_w_out"),
                 ("ml_w_in", "ml_w_uq", "ml_w_ukv", "ml_w_out"), ("ch_w_in", "ch_w_out"))


class _LocalPlan:
    def __init__(self, w):
        self.small = w
        self.grads = {}

    def weights(self, names):
        return {n: self.small[n] for n in names}

    def hosted(self, tag):
        return None

    def after(self, tag):
        pass

    def note(self, values):
        pass

    def layer_grads(self, layer, grads):
        self.grads.update(grads)


def _local_step(xm, target, mods, plan, lat_len):
    batch, tot, d = xm.shape
    e = d
    n_all, n_lat = tot // TM, lat_len // TM
    cos, sin = _rope_tables(lat_len, tot - lat_len)
    g = {}
    w = dict(plan.small)

    def hosting(tag, fn, *args, **kwargs):
        out = fn(*args, hosted=plan.hosted(tag), **kwargs)
        plan.after(tag)
        return out

    def s1_splits(widths):
        out, s = [], 0
        for wd in widths:
            out.append((0, s, wd))
            s += wd
        return out

    tml = TM_LATENT if lat_len % TM_LATENT == 0 else TM
    n_big = lat_len // tml

    def lat_tiles(n_tiles, tm):
        return n_lat if tm == TM else n_tiles

    def fwd_in(name, x, mod, gi, wname, widths, n_tiles, dtypes=None, tm=TM):
        return hosting(name, _stage_fwd, name, pre=_pre_adaln, post=None, wsel=[0], splits=s1_splits(widths),
                       tiles=[x], eparams=[mod[0], mod[1]], sparams=[w["norm_g"][gi:gi + 1]], weights=[w[wname]],
                       out_widths=widths, out_dtypes=dtypes or [ACT] * len(widths), batch=batch, n_tiles=n_tiles,
                       n_lat_tiles=lat_tiles(n_tiles, tm), tm=tm)

    def bwd_in(name, x, mod, gi, wname, widths, n_tiles, cots, lat_only, add, add_lat_only, stack=None,
               dx_lat_only=False):
        (dx,), (dsh, dsc), (dg,), (dw,) = hosting(
            name, _stage_bwd, name, pre=_pre_adaln, post=None, wsel=[0], splits=s1_splits(widths), tiles=[x],
            tile_diff=[True], eparams=[mod[0], mod[1]], sparams=[w["norm_g"][gi:gi + 1]], weights=[w[wname]],
            cots=cots, cot_lat_only=lat_only, batch=batch, n_tiles=n_tiles, n_lat_tiles=n_lat, add=add,
            add_lat_only=add_lat_only, w_col_stack=[stack], dt_lat_only=dx_lat_only)
        return dx, dsh, dsc, dg, dw

    def fwd_out(name, pre, tiles, mod, sparams, wname, n_tiles, tm=TM):
        return hosting(name, _stage_fwd, name, pre=pre, post=_post_residual(len(tiles) - 1), wsel=[0], splits=None,
                       tiles=tiles, eparams=[mod[2]], sparams=sparams, weights=[w[wname]], out_widths=[d],
                       out_dtypes=[F32], batch=batch, n_tiles=n_tiles, n_lat_tiles=lat_tiles(n_tiles, tm), tm=tm)[0]

    def bwd_out(name, pre, tiles, mod, sparams, wname, n_tiles, cot, tm=TM):
        diff = [True] * (len(tiles) - 1) + [False]
        dts, (dgt,), dss, (dw,) = hosting(
            name, _stage_bwd, name, pre=pre, post=_post_residual(len(tiles) - 1), wsel=[0], splits=None, tiles=tiles,
            tile_diff=diff, eparams=[mod[2]], sparams=sparams, weights=[w[wname]], cots=[cot], cot_lat_only=[False],
            batch=batch, n_tiles=n_tiles, n_lat_tiles=lat_tiles(n_tiles, tm), tm=tm)
        return dts, dgt, dss, dw

    w.update(plan.weights(("cv_w_in",)))
    cv_s = [w["cv_ln_g"], w["cv_ln_b"]]
    a0, b0, g0 = fwd_in("cv_in_fwd", xm, mods[0], 0, "cv_w_in", [e, e, e], n_all)
    c1 = hosting("conv_fwd", _conv_fwd, a0, b0, w["cv_dw"], w["cv_db"], lat_len)
    w.update(plan.weights(("cv_w_out",)))
    x1 = fwd_out("cv_out_fwd", _pre_conv_out, [c1, g0, xm], mods[0], cv_s, "cv_w_out", n_all)

    w.update(plan.weights(LAYER_WEIGHTS[1]))
    pl_s = [w["pl_w_grp"], w["pl_scale"]]
    v1, g1 = fwd_in("pl_in_fwd", x1, mods[1], 1, "pl_w_in", [e, e], n_all, dtypes=[F32, ACT])
    pooled = hosting("pool_fwd", _pool_fwd, v1, lat_len)
    x2 = fwd_out("pl_out_fwd", _pre_pool_out, [pooled, g1, x1], mods[1], pl_s, "pl_w_out", n_all)

    w.update(plan.weights(LAYER_WEIGHTS[2]))
    ml_widths = [KV_RANK, LANES, Q_RANK, HEADS * VDIM]
    ckv, krp, cq, g2 = fwd_in("ml_in_fwd", x2, mods[2], 2, "ml_w_in", ml_widths, n_all)
    k_s = [w["ml_kv_norm"], w["ml_nope_norm"][1:2], w["ml_rope_norm"][1:2]]
    q_s = [w["ml_q_norm"], w["ml_nope_norm"][0:1], w["ml_rope_norm"][0:1]]
    kk, vv = hosting("ml_keys_fwd", _stage_fwd, "ml_keys_fwd", pre=_pre_rms_only, post=_post_mla_keys, wsel=[0],
                     splits=None, tiles=[ckv, krp, cos, sin], eparams=[], sparams=k_s, weights=[w["ml_w_ukv"]],
                     out_widths=[HEADS * HEAD_W, HEADS * VDIM], out_dtypes=[BF16, BF16], batch=batch,
                     n_tiles=n_all, n_lat_tiles=n_lat)
    (qq,) = _stage_fwd("ml_queries_fwd", pre=_pre_rms_only, post=_post_mla_queries, wsel=[0], splits=None,
                       tiles=[cq, cos, sin], eparams=[], sparams=q_s, weights=[w["ml_w_uq"]],
                       out_widths=[HEADS * HEAD_W], out_dtypes=[BF16], batch=batch, n_tiles=n_big,
                       n_lat_tiles=n_big, tm=tml)
    att, lse = hosting("attn_fwd", _attn_fwd, qq, kk, vv)
    x3 = fwd_out("ml_out_fwd", _pre_mla_out, [att, g2, x2], mods[2], [], "ml_w_out", n_big, tm=tml)

    w.update(plan.weights(LAYER_WEIGHTS[3]))
    ch_s = [w["ch_ln_g"], w["ch_ln_b"], w["ch_w_s"], w["ch_b_s"]]
    u3, v3, g3 = fwd_in("ch_in_fwd", x3, mods[3], 3, "ch_w_in", [e, e, e], n_big, tm=tml)
    x4 = fwd_out("ch_out_fwd", _pre_chunk_out, [u3, v3, g3, x3], mods[3], ch_s, "ch_w_out", n_big, tm=tml)

    loss_part, dy = _loss_kernel(x4, target)

    dmods = [None] * 4
    dnorm = [None] * 4
    big = {}
    (du, dv, dg), dgt, (g["ch_ln_g"], g["ch_ln_b"], g["ch_w_s"], g["ch_b_s"]), big["ch_w_out"] = bwd_out(
        "ch_out_bwd", _pre_chunk_out, [u3, v3, g3, x3], mods[3], ch_s, "ch_w_out", n_big, dy, tm=tml)
    plan.note({n: g[n] for n in ("ch_ln_g", "ch_ln_b", "ch_w_s", "ch_b_s")})
    dx3, dsh, dsc, dnorm[3], big["ch_w_in"] = bwd_in("ch_in_bwd", x3, mods[3], 3, "ch_w_in", [e, e, e], n_lat,
                                                     [du, dv, dg], [False] * 3, dy, False, stack=N_CHIP)
    dmods[3] = (dsh, dsc, dgt)
    plan.layer_grads(3, big)

    big = {}
    (datt, dg), dgt, _, big["ml_w_out"] = bwd_out("ml_out_bwd", _pre_mla_out, [att, g2, x2], mods[2], [],
                                                  "ml_w_out", n_big, dx3, tm=tml)
    dq, dk, dvv = hosting("attn_bwd", _attn_bwd, qq, kk, vv, att, lse, datt)
    (dcq,), _, (g["ml_q_norm"], dnope0, drope0), (big["ml_w_uq"],) = hosting(
        "ml_queries_bwd", _stage_bwd, "ml_queries_bwd", pre=_pre_rms_only, post=_post_mla_queries, wsel=[0],
        splits=None, tiles=[cq, cos, sin], tile_diff=[True, False, False], eparams=[], sparams=q_s,
        weights=[w["ml_w_uq"]], cots=[dq], cot_lat_only=[False], batch=batch, n_tiles=n_big, n_lat_tiles=n_big,
        tm=tml)
    (dckv, dkrp), _, (g["ml_kv_norm"], dnope1, drope1), (big["ml_w_ukv"],) = hosting(
        "ml_keys_bwd", _stage_bwd, "ml_keys_bwd", pre=_pre_rms_only, post=_post_mla_keys, wsel=[0], splits=None,
        tiles=[ckv, krp, cos, sin], tile_diff=[True, True, False, False], eparams=[], sparams=k_s,
        weights=[w["ml_w_ukv"]], cots=[dk, dvv], cot_lat_only=[False, False], batch=batch, n_tiles=n_all,
        n_lat_tiles=n_lat, w_col_stack=[N_CHIP])
    g["ml_nope_norm"] = jnp.concatenate([dnope0, dnope1], axis=0)
    g["ml_rope_norm"] = jnp.concatenate([drope0, drope1], axis=0)
    dx2, dsh, dsc, dnorm[2], big["ml_w_in"] = bwd_in("ml_in_bwd", x2, mods[2], 2, "ml_w_in", ml_widths, n_all,
                                                     [dckv, dkrp, dcq, dg], [False, False, True, True], dx3, True)
    dmods[2] = (dsh, dsc, dgt)
    plan.layer_grads(2, big)

    big = {}
    (dpooled, dg), dgt, (big["pl_w_grp"], g["pl_scale"]), big["pl_w_out"] = bwd_out(
        "pl_out_bwd", _pre_pool_out, [pooled, g1, x1], mods[1], pl_s, "pl_w_out", n_all, dx2)
    dv1 = _pool_bwd(dpooled, lat_len)
    dx1, dsh, dsc, dnorm[1], big["pl_w_in"] = bwd_in("pl_in_bwd", x1, mods[1], 1, "pl_w_in", [e, e], n_all,
                                                     [dv1, dg], [False] * 2, dx2, False, stack=N_CHIP)
    dmods[1] = (dsh, dsc, dgt)
    plan.layer_grads(1, big)

    big = {}
    (dc1, dg), dgt, (g["cv_ln_g"], g["cv_ln_b"]), big["cv_w_out"] = bwd_out(
        "cv_out_bwd", _pre_conv_out, [c1, g0, xm], mods[0], cv_s, "cv_w_out", n_all, dx1)
    plan.layer_grads(0, big)
    big = {}
    da, db, g["cv_dw"], g["cv_db"] = hosting("conv_bwd", _conv_bwd, a0, b0, w["cv_dw"], dc1, lat_len)
    dx0, dsh, dsc, dnorm[0], big["cv_w_in"] = bwd_in("cv_in_bwd", xm, mods[0], 0, "cv_w_in", [e, e, e], n_all,
                                                     [da, db, dg], [False] * 3, dx1, False, stack=N_CHIP,
                                                     dx_lat_only=True)
    dmods[0] = (dsh, dsc, dgt)
    plan.layer_grads(0, big)
    g["norm_g"] = jnp.concatenate(dnorm, axis=0)
    return loss_part, dx0, dmods, g


N_DEV = 8
N_CHIP = 4
ANY = pl.BlockSpec(memory_space=pl.ANY)


def _my_place():
    return lax.axis_index("x"), lax.axis_index("y"), lax.axis_index("c")


def _flip(v, f):
    return 1 - v if f else v


def _ag8_copies(x):
    def plan(ins, outs, sems):
        mx, my, mc = _my_place()
        me = 4 * mx + 2 * my + mc
        sends, recvs = [], []
        for rel in range(1, N_DEV):
            peer = (_flip(mx, rel & 4), _flip(my, rel & 2), _flip(mc, rel & 1))
            src_dev = 4 * peer[0] + 2 * peer[1] + peer[2]
            sends.append(_remote(ins[0], outs[0].at[me], sems, rel - 1, peer))
            recvs.append(_remote(ins[0], outs[0].at[src_dev], sems, rel - 1, peer))
        return sends, recvs, [pltpu.make_async_copy(ins[0], outs[0].at[me], sems[2].at[0])]

    return _copies_hosted([x], [jax.ShapeDtypeStruct((N_DEV,) + x.shape, x.dtype)], (N_DEV - 1, N_DEV - 1, 1), plan)


def _ag8(name, x):
    return _run_hosted(name, _ag8_copies(x))[0]


def _ag8_column_copies(x, width):
    def plan(ins, outs, sems):
        mx, my, mc = _my_place()
        me = 4 * mx + 2 * my + mc
        sends, recvs = [], []
        for rel in range(1, N_DEV):
            peer = (_flip(mx, rel & 4), _flip(my, rel & 2), _flip(mc, rel & 1))
            src_dev = 4 * peer[0] + 2 * peer[1] + peer[2]
            cols = pl.ds(pl.multiple_of((2 * peer[0] + peer[1]) * width, LANES), width)
            sends.append(_remote(ins[0].at[:, cols], outs[0].at[me], sems, rel - 1, peer))
            recvs.append(_remote(ins[0].at[:, cols], outs[0].at[src_dev], sems, rel - 1, peer))
        mine = pl.ds(pl.multiple_of((2 * mx + my) * width, LANES), width)
        return sends, recvs, [pltpu.make_async_copy(ins[0].at[:, mine], outs[0].at[me], sems[2].at[0])]

    return _copies_hosted([x], [jax.ShapeDtypeStruct((N_DEV, x.shape[0], width), x.dtype)],
                          (N_DEV - 1, N_DEV - 1, 1), plan)


def _chip_rows_copies(x, rows_per_dev, shared_row):
    n_out = rows_per_dev + 1

    def plan(ins, outs, sems):
        mx, my, mc = _my_place()
        chip = 2 * mx + my
        sends, recvs = [], []

        def pieces(dev):
            return [(ins[0].at[pl.ds(dev * rows_per_dev, rows_per_dev)], slice(0, rows_per_dev)),
                    (ins[0].at[pl.ds(shared_row, 1)], slice(rows_per_dev, n_out))]

        for k, peer, pchip in _chip_peers(mx, my, mc):
            for t, (src, where) in enumerate(pieces(2 * pchip + mc)):
                sends.append(_remote(src, outs[0].at[chip, where], sems, 2 * k + t, peer))
                recvs.append(_remote(src, outs[0].at[pchip, where], sems, 2 * k + t, peer))
        locals_ = [pltpu.make_async_copy(src, outs[0].at[chip, where], sems[2].at[t])
                   for t, (src, where) in enumerate(pieces(2 * chip + mc))]
        return sends, recvs, locals_

    return _copies_hosted([x], [jax.ShapeDtypeStruct((N_CHIP, n_out) + x.shape[1:], x.dtype)], (6, 6, 2), plan)


def _chip_peers(mx, my, mc):
    out = []
    for rel in range(1, N_CHIP):
        px, py = _flip(mx, rel & 2), _flip(my, rel & 1)
        out.append((rel - 1, (px, py, mc), 2 * px + py))
    return out


def _half(mc, rows):
    return pl.ds(pl.multiple_of(mc * (rows // 2), 8), rows // 2)


def _copies_hosted(arrays, out_shapes, n_sems, plan, aliases=None):
    def start(ins, outs, sems):
        sends, _, locals_ = plan(ins, outs, sems)
        for cp in locals_ + sends:
            cp.start()

    def wait(ins, outs, sems):
        sends, recvs, locals_ = plan(ins, outs, sems)
        for cp in recvs:
            cp.wait_recv()
        for cp in sends:
            cp.wait_send()
        for cp in locals_:
            cp.wait()

    return _Hosted(arrays, out_shapes, [pltpu.SemaphoreType.DMA((k,)) for k in n_sems], start, wait, aliases)


def _remote(src, dst, sems, k, peer):
    return pltpu.make_async_remote_copy(src_ref=src, dst_ref=dst, send_sem=sems[0].at[k], recv_sem=sems[1].at[k],
                                        device_id=peer, device_id_type=MESH)


def _gather_ici(shards):
    n = len(shards)

    def plan(ins, outs, sems):
        mx, my, mc = _my_place()
        chip = 2 * mx + my
        sends, recvs, locals_ = [], [], []
        for a in range(n):
            rows = ins[a].shape[0]
            locals_.append(pltpu.make_async_copy(ins[a], outs[a].at[chip], sems[2].at[a]))
            for k, peer, pchip in _chip_peers(mx, my, mc):
                src = ins[a].at[_half(mc, rows)]
                sends.append(_remote(src, outs[a].at[chip, _half(mc, rows)], sems, 3 * a + k, peer))
                recvs.append(_remote(src, outs[a].at[pchip, _half(mc, rows)], sems, 3 * a + k, peer))
        return sends, recvs, locals_

    return _copies_hosted(shards, [jax.ShapeDtypeStruct((N_CHIP,) + s.shape, s.dtype) for s in shards],
                          (3 * n, 3 * n, n), plan)


def _sibling_fill(arrays, row_axis, chips_only_other):
    n = len(arrays)
    per = 3 if chips_only_other else 1

    def plan(ins, outs, sems):
        mx, my, mc = _my_place()
        sibling = (mx, my, 1 - mc)

        def views(a, core):
            rows = outs[a].shape[row_axis]
            if chips_only_other:
                return [outs[a].at[pchip, _half(core, rows)] for _, _, pchip in _chip_peers(mx, my, mc)]
            return [outs[a].at[_half(core, rows)]]

        sends, recvs = [], []
        for a in range(n):
            for k, v in enumerate(views(a, mc)):
                sends.append(_remote(v, v, sems, per * a + k, sibling))
            for k, v in enumerate(views(a, 1 - mc)):
                recvs.append(_remote(v, v, sems, per * a + k, sibling))
        return sends, recvs, []

    return _copies_hosted(arrays, [jax.ShapeDtypeStruct(s.shape, s.dtype) for s in arrays], (per * n, per * n), plan,
                          aliases={a: a for a in range(n)})


def _grad_swap_d2d(stacks):
    n = len(stacks)

    def plan(ins, outs, sems):
        mx, my, mc = _my_place()
        sibling = (mx, my, 1 - mc)
        sends = [_remote(ins[a].at[:, _half(1 - mc, ins[a].shape[1])], outs[a], sems, a, sibling) for a in range(n)]
        return sends, sends, []

    return _copies_hosted(stacks, [jax.ShapeDtypeStruct((N_CHIP, s.shape[1] // 2, s.shape[2]), s.dtype)
                                   for s in stacks], (n, n), plan)


def _grad_exchange_ici(parts):
    n = len(parts)

    def plan(ins, outs, sems):
        mx, my, mc = _my_place()
        chip = 2 * mx + my
        sends, recvs, locals_ = [], [], []
        for a in range(n):
            locals_.append(pltpu.make_async_copy(ins[a].at[chip], outs[a].at[chip], sems[2].at[a]))
            for k, peer, pchip in _chip_peers(mx, my, mc):
                sends.append(_remote(ins[a].at[pchip], outs[a].at[chip], sems, 3 * a + k, peer))
                recvs.append(_remote(ins[a].at[pchip], outs[a].at[pchip], sems, 3 * a + k, peer))
        return sends, recvs, locals_

    return _copies_hosted(parts, [jax.ShapeDtypeStruct(s.shape, s.dtype) for s in parts], (3 * n, 3 * n, n), plan)


def _row_block(rows, limit=256):
    for t in range(min(rows, limit), 7, -8):
        if rows % t == 0 and t % 8 == 0:
            return t
    return rows


def _grad_add_half(core, stack, received):
    _, rows, cw = stack.shape
    rh = rows // 2
    tr = _row_block(rh)

    def body(s_ref, a_ref, b_ref, o_ref):
        o_ref[...] = (a_ref[...] + b_ref[...]).astype(o_ref.dtype)

    grid_spec = pltpu.PrefetchScalarGridSpec(
        num_scalar_prefetch=1, grid=(rh // tr,),
        in_specs=[pl.BlockSpec((N_CHIP, tr, cw), lambda i, s: (0, s[0] * (rh // tr) + i, 0)),
                  pl.BlockSpec((N_CHIP, tr, cw), lambda i, s: (0, i, 0))],
        out_specs=pl.BlockSpec((N_CHIP, tr, cw), lambda i, s: (0, i, 0)))
    return pl.pallas_call(
        body, name="grad_add_half", grid_spec=grid_spec, out_shape=jax.ShapeDtypeStruct(received.shape, BF16),
        compiler_params=pltpu.CompilerParams(dimension_semantics=("arbitrary",), vmem_limit_bytes=VMEM_LIMIT),
    )(core, stack, received)


def _adamw(name, row_off, parts, w, m, v, rows, hosted=None):
    n, _, cw = parts.shape
    tr = _row_block(rows, 128)

    def update(p_ref, w_ref, m_ref, v_ref, g_ref, d_ref, nm_ref, nv_ref):
        g = p_ref[0].astype(F32)
        for k in range(1, n):
            g = g + p_ref[k].astype(F32)
        nm = ADAM_B1 * m_ref[...] + (1.0 - ADAM_B1) * g
        nv = ADAM_B2 * v_ref[...] + (1.0 - ADAM_B2) * (g * g)
        m_hat = nm / (1.0 - ADAM_B1 ** ADAM_STEP)
        v_hat = nv / (1.0 - ADAM_B2 ** ADAM_STEP)
        g_ref[...] = g
        d_ref[...] = -ADAM_LR * (m_hat / (jnp.sqrt(v_hat) + ADAM_EPS) + ADAM_WD * w_ref[...])
        nm_ref[...] = nm
        nv_ref[...] = nv

    out_shape = [jax.ShapeDtypeStruct(w.shape, F32)] * 4
    if row_off is None:
        blk = pl.BlockSpec((tr, cw), lambda i: (i, 0))
        return _pcall(update, name=name, grid=(rows // tr,), out_specs=[blk] * 4, out_shape=out_shape,
                      in_specs=[pl.BlockSpec((n, tr, cw), lambda i: (0, i, 0)), blk, blk, blk],
                      args=[parts, w, m, v], hosted=hosted)

    def body(s_ref, *refs):
        update(*refs)

    full = pl.BlockSpec((tr, cw), lambda i, s: (s[0] // tr + i, 0))
    grid_spec = pltpu.PrefetchScalarGridSpec(
        num_scalar_prefetch=1, grid=(rows // tr,),
        in_specs=[pl.BlockSpec((n, tr, cw), lambda i, s: (0, i, 0)), full, full, full],
        out_specs=[full, full, full, full])
    return pl.pallas_call(
        body, name=name, grid_spec=grid_spec, out_shape=out_shape,
        compiler_params=pltpu.CompilerParams(dimension_semantics=("arbitrary",), vmem_limit_bytes=VMEM_LIMIT),
    )(row_off, parts, w, m, v)


def _sum8(x):
    _, r, cw = x.shape
    tr = _row_block(r, 64)

    def body(x_ref, o_ref):
        acc = x_ref[0]
        for k in range(1, N_DEV):
            acc = acc + x_ref[k]
        o_ref[...] = acc

    return pl.pallas_call(
        body, name="sum8", grid=(r // tr,), in_specs=[pl.BlockSpec((N_DEV, tr, cw), lambda i: (0, i, 0))],
        out_specs=pl.BlockSpec((tr, cw), lambda i: (i, 0)), out_shape=jax.ShapeDtypeStruct((r, cw), F32),
        compiler_params=pltpu.CompilerParams(dimension_semantics=("arbitrary",)),
    )(x)


MOD_ROWS = 24
CTX_ROW = 16


def _mod_fwd(c_rows, w_mod, b_mod, hosted=None):
    nl, d, nn = w_mod.shape

    def body(c_ref, w_ref, b_ref, o_ref):
        o_ref[0] = _dot(_silu(c_ref[...]), w_ref[0]) + b_ref[0]

    return _pcall(
        body, name="mod_fwd", grid=(nl,),
        in_specs=[pl.BlockSpec((MOD_ROWS, d), lambda i: (0, 0)), pl.BlockSpec((1, d, nn), lambda i: (i, 0, 0)),
                  pl.BlockSpec((1, 1, nn), lambda i: (i, 0, 0))],
        out_specs=[pl.BlockSpec((1, MOD_ROWS, nn), lambda i: (i, 0, 0))],
        out_shape=[jax.ShapeDtypeStruct((nl, MOD_ROWS, nn), F32)], args=[c_rows, w_mod, b_mod], hosted=hosted)[0]


def _mod_bwd_rows(dlat, dctx_parts):
    nl, ne, nn = dlat.shape

    def body(l_ref, c_ref, db_ref, dc_ref):
        dc = c_ref[0, 0:1, :]
        for k in range(1, N_DEV):
            dc = dc + c_ref[0, k:k + 1, :]
        db = dc
        for k in range(ne):
            db = db + l_ref[0, k:k + 1, :]
        db_ref[0] = db
        dc_ref[0] = dc

    return pl.pallas_call(
        body, name="mod_bwd_rows", grid=(nl,),
        in_specs=[pl.BlockSpec((1, ne, nn), lambda i: (i, 0, 0)), pl.BlockSpec((1, N_DEV, nn), lambda i: (i, 0, 0))],
        out_specs=[pl.BlockSpec((1, 1, nn), lambda i: (i, 0, 0))] * 2,
        out_shape=[jax.ShapeDtypeStruct((nl, 1, nn), F32)] * 2,
        compiler_params=pltpu.CompilerParams(dimension_semantics=("arbitrary",)),
    )(dlat, dctx_parts)


def _mod_bwd_w(c_cols, d_rows, w_mod, hosted=None):
    nl, d, nn = w_mod.shape

    def body(c_ref, d_ref, w_ref, dw_ref, dc_ref):
        i = pl.program_id(0)
        c = c_ref[...]
        sg = jax.nn.sigmoid(c)
        s = c * sg
        dv = d_ref[0]
        acc = s[:, 0:1] * dv[0:1, :]
        for r in range(1, CTX_ROW + 1):
            acc = acc + s[:, r:r + 1] * dv[r:r + 1, :]
        dw_ref[0] = acc
        ds_ctx = jnp.sum(w_ref[0] * dv[CTX_ROW:CTX_ROW + 1, :], axis=1, keepdims=True)
        cc, sc = c[:, CTX_ROW:CTX_ROW + 1], sg[:, CTX_ROW:CTX_ROW + 1]
        part = ds_ctx * (sc * (1.0 + cc * (1.0 - sc)))

        @pl.when(i == 0)
        def _():
            dc_ref[...] = part

        @pl.when(i != 0)
        def _():
            dc_ref[...] += part

    return _pcall(
        body, name="mod_bwd_w", grid=(nl,),
        in_specs=[pl.BlockSpec((d, MOD_ROWS), lambda i: (0, 0)), pl.BlockSpec((1, MOD_ROWS, nn), lambda i: (i, 0, 0)),
                  pl.BlockSpec((1, d, nn), lambda i: (i, 0, 0))],
        out_specs=[pl.BlockSpec((1, d, nn), lambda i: (i, 0, 0)), pl.BlockSpec((d, 1), lambda i: (0, 0))],
        out_shape=[jax.ShapeDtypeStruct((nl, d, nn), F32), jax.ShapeDtypeStruct((d, 1), F32)],
        args=[c_cols, d_rows, w_mod], hosted=hosted)


def _pack_rows(arrays, width, row_multiple=8):
    rows, spans, r0 = [], [], 0
    for a in arrays:
        flat = a.reshape(-1)
        nr = -(-flat.shape[0] // width)
        held = -(-nr // 8) * 8
        flat = jnp.pad(flat, (0, held * width - flat.shape[0]))
        rows.append(flat.reshape(held, width))
        spans.append((r0, nr, a.shape))
        r0 += held
    if r0 % row_multiple:
        rows.append(jnp.zeros((row_multiple - r0 % row_multiple, width), F32))
    return jnp.concatenate(rows, axis=0), spans


def _unpack_rows(packed, spans):
    out = []
    for r0, nr, shape in spans:
        out.append(packed[r0:r0 + nr].reshape(-1)[:math.prod(shape)].reshape(shape))
    return out


BIG = {"cv_w_in": 1, "cv_w_out": 0, "pl_w_in": 1, "pl_w_grp": None, "pl_w_out": 0, "ml_w_in": 1, "ml_w_uq": 1,
       "ml_w_ukv": 1, "ml_w_out": 0, "ch_w_in": 1, "ch_w_out": 0}
SMALL_SHARDED = ["cv_dw", "pl_scale", "ml_q_norm", "ml_kv_norm", "ch_ln_g", "ch_ln_b"]
SMALL_REPLICATED = ["c_ctx", "norm_g", "b_mod", "cv_db", "cv_ln_g", "cv_ln_b", "ml_nope_norm", "ml_rope_norm",
                    "ch_w_s", "ch_b_s"]
WEIGHTS = ['c_ctx', 'norm_g', 'w_mod', 'b_mod', 'cv_w_in', 'cv_dw', 'cv_db', 'cv_ln_g', 'cv_ln_b', 'cv_w_out',
           'pl_w_in', 'pl_w_grp', 'pl_scale', 'pl_w_out', 'ml_w_in', 'ml_q_norm', 'ml_kv_norm', 'ml_w_uq', 'ml_w_ukv',
           'ml_nope_norm', 'ml_rope_norm', 'ml_w_out', 'ch_w_in', 'ch_ln_g', 'ch_ln_b', 'ch_w_s', 'ch_b_s', 'ch_w_out']


def _shard2d(name, a):
    if name == "pl_w_grp":
        return a.reshape(a.shape[-3] * a.shape[-2], a.shape[-1])
    return a.reshape(a.shape[-2], a.shape[-1])


def _unstack(name, s):
    if name == "pl_w_grp":
        ng = len(POOL_WINDOWS)
        return s.reshape(N_CHIP, ng, s.shape[1] // ng, s.shape[2]).transpose(1, 0, 2, 3).reshape(ng, -1, s.shape[2])
    if BIG[name] == 0:
        return s.reshape(-1, s.shape[2])
    return s.transpose(1, 0, 2).reshape(s.shape[1], -1)


def _stack(name, g):
    if g.ndim == 3 and name != "pl_w_grp":
        return g
    if name == "pl_w_grp":
        ng = len(POOL_WINDOWS)
        return g.reshape(ng, N_CHIP, -1, g.shape[2]).transpose(1, 0, 2, 3).reshape(N_CHIP, -1, g.shape[2])
    if BIG[name] == 0:
        return g.reshape(N_CHIP, -1, g.shape[1])
    return g.reshape(g.shape[0], N_CHIP, -1).transpose(1, 0, 2)


L0, L1, L2, L3 = LAYER_WEIGHTS
EARLY_SMALL = ("ch_w_s", "ch_b_s", "ch_ln_g", "ch_ln_b")
MESH_SCHEDULE = {
    "ag8_inputs": [("gather", L0[:1])], "mod_fwd": [("gfill", L0[:1])],
    "cv_in_fwd": [("gather", L0[1:]), ("gather", L1[1:])], "conv_fwd": [("gfill", L0[1:]), ("gather", L1[:1])],
    "cv_out_fwd": [("gfill", L1), ("gather", L2[3:])],
    "pl_in_fwd": [("gather", L2[:1])], "pool_fwd": [("gather", L2[1:3])], "pl_out_fwd": [("gfill", L2)],
    "attn_fwd": [("gather", L3)], "ml_out_fwd": [("gfill", L3)],
    "ch_in_bwd": [("small", EARLY_SMALL)],
    "ml_out_bwd": [("swap", L3)], "attn_bwd": [("exch", L3)], "ml_queries_bwd": [("ofill", L3)],
    "pl_out_bwd": [("swap", L2)], "pl_in_bwd": [("exch", L2)],
    "cv_out_bwd": [("swap", L1), ("ofill", L2)], "conv_bwd": [("exch", L1), ("swap", L0[1:])],
    "ag8_dmod": [("swap", L0[:1]), ("exch", L0[1:])], "mod_bwd_w": [("exch", L0[:1]), ("ofill", L1)],
    "ag8_small_grads": [("ofill", L0[:1]), ("ofill", L0[1:])],
}
GRAD_GROUPS = (L3, L2, L1, L0[1:], L0[:1])


class _MeshPlan:
    def __init__(self, weights, m, v, core):
        self.W, self.M, self.V, self.core = weights, m, v, core
        self.small = None
        self.stack, self.gstack, self.part, self.half, self.out = {}, {}, {}, {}, {}
        self.notes, self.early = {}, {}
        self.live, self.done = {}, set()

    def _make(self, op, names):
        if op == "gather":
            return _gather_ici([_shard2d(n, self.W[n]).astype(BF16) for n in names])
        if op == "gfill":
            return _sibling_fill([self.stack[n] for n in names], 1, True)
        if op == "swap":
            return _grad_swap_d2d([self.gstack[n] for n in names])
        if op == "exch":
            return _grad_exchange_ici([self.part[n] for n in names])
        if op == "ofill":
            return _sibling_fill([t for n in names for t in self.half[n]], 0, False)
        pack, self.early_spans = _pack_rows([self.notes[n] for n in names], LANES, 128)
        return _ag8_copies(pack)

    def _finish_op(self, op, names, hosted):
        self.done.add((op, names))
        res = hosted.results
        if op in ("gather", "gfill"):
            self.stack.update(zip(names, res))
        elif op == "swap":
            for n, r in zip(names, res):
                self.part[n] = _grad_add_half(self.core, self.gstack[n], r)
        elif op == "exch":
            for n, q in zip(names, res):
                rh = q.shape[1]
                self.half[n] = _adamw("adamw_" + n, self.core * rh, q, _shard2d(n, self.W[n]),
                                      _shard2d(n, self.M[n]), _shard2d(n, self.V[n]), rh)
        elif op == "ofill":
            for k, n in enumerate(names):
                self.out[n] = tuple(r.reshape(self.W[n].shape) for r in res[4 * k:4 * k + 4])
        else:
            self.early.update(zip(names, _unpack_rows(_sum8(res[0]), self.early_spans)))

    def alone(self, op, names):
        hosted = self._make(op, names)
        _run_hosted("%s_%s" % (op, names[0]), hosted)
        self._finish_op(op, names, hosted)

    def weights(self, names):
        wk = {n: _unstack(n, self.stack[n]) for n in names}
        if "pl_w_grp" in wk:
            wk["pl_w_grp"] = wk["pl_w_grp"].astype(F32)
        return _prep_weights(wk)

    def hosted(self, tag):
        self.live[tag] = [(op, names, self._make(op, names)) for op, names in MESH_SCHEDULE.get(tag, [])]
        return _merge_hosted([h for _, _, h in self.live[tag]])

    def after(self, tag):
        for op, names, hosted in self.live.pop(tag, []):
            self._finish_op(op, names, hosted)

    def note(self, values):
        self.notes.update(values)

    def layer_grads(self, layer, grads):
        g = _unprep_grads(grads)
        for n in g:
            self.gstack[n] = _stack(n, g[n])

    def finish(self):
        for names in GRAD_GROUPS:
            for op in ("swap", "exch", "ofill"):
                if (op, names) not in self.done:
                    self.alone(op, names)
        return self.out


def kernel(x, c, ctx, c_ctx, norm_g, w_mod, b_mod, cv_w_in, cv_dw, cv_db, cv_ln_g, cv_ln_b, cv_w_out, pl_w_in, pl_w_grp, pl_scale, pl_w_out, ml_w_in, ml_q_norm, ml_kv_norm, ml_w_uq, ml_w_ukv, ml_nope_norm, ml_rope_norm, ml_w_out, ch_w_in, ch_ln_g, ch_ln_b, ch_w_s, ch_b_s, ch_w_out, loss_target, m_c_ctx, m_norm_g, m_w_mod, m_b_mod, m_cv_w_in, m_cv_dw, m_cv_db, m_cv_ln_g, m_cv_ln_b, m_cv_w_out, m_pl_w_in, m_pl_w_grp, m_pl_scale, m_pl_w_out, m_ml_w_in, m_ml_q_norm, m_ml_kv_norm, m_ml_w_uq, m_ml_w_ukv, m_ml_nope_norm, m_ml_rope_norm, m_ml_w_out, m_ch_w_in, m_ch_ln_g, m_ch_ln_b, m_ch_w_s, m_ch_b_s, m_ch_w_out, v_c_ctx, v_norm_g, v_w_mod, v_b_mod, v_cv_w_in, v_cv_dw, v_cv_db, v_cv_ln_g, v_cv_ln_b, v_cv_w_out, v_pl_w_in, v_pl_w_grp, v_pl_scale, v_pl_w_out, v_ml_w_in, v_ml_q_norm, v_ml_kv_norm, v_ml_w_uq, v_ml_w_ukv, v_ml_nope_norm, v_ml_rope_norm, v_ml_w_out, v_ch_w_in, v_ch_ln_g, v_ch_ln_b, v_ch_w_s, v_ch_b_s, v_ch_w_out):
    W = dict(c_ctx=c_ctx, norm_g=norm_g, w_mod=w_mod, b_mod=b_mod, cv_w_in=cv_w_in, cv_dw=cv_dw, cv_db=cv_db, cv_ln_g=cv_ln_g, cv_ln_b=cv_ln_b, cv_w_out=cv_w_out, pl_w_in=pl_w_in, pl_w_grp=pl_w_grp, pl_scale=pl_scale, pl_w_out=pl_w_out, ml_w_in=ml_w_in, ml_q_norm=ml_q_norm, ml_kv_norm=ml_kv_norm, ml_w_uq=ml_w_uq, ml_w_ukv=ml_w_ukv, ml_nope_norm=ml_nope_norm, ml_rope_norm=ml_rope_norm, ml_w_out=ml_w_out, ch_w_in=ch_w_in, ch_ln_g=ch_ln_g, ch_ln_b=ch_ln_b, ch_w_s=ch_w_s, ch_b_s=ch_b_s, ch_w_out=ch_w_out)
    M = dict(c_ctx=m_c_ctx, norm_g=m_norm_g, w_mod=m_w_mod, b_mod=m_b_mod, cv_w_in=m_cv_w_in, cv_dw=m_cv_dw, cv_db=m_cv_db, cv_ln_g=m_cv_ln_g, cv_ln_b=m_cv_ln_b, cv_w_out=m_cv_w_out, pl_w_in=m_pl_w_in, pl_w_grp=m_pl_w_grp, pl_scale=m_pl_scale, pl_w_out=m_pl_w_out, ml_w_in=m_ml_w_in, ml_q_norm=m_ml_q_norm, ml_kv_norm=m_ml_kv_norm, ml_w_uq=m_ml_w_uq, ml_w_ukv=m_ml_w_ukv, ml_nope_norm=m_ml_nope_norm, ml_rope_norm=m_ml_rope_norm, ml_w_out=m_ml_w_out, ch_w_in=m_ch_w_in, ch_ln_g=m_ch_ln_g, ch_ln_b=m_ch_ln_b, ch_w_s=m_ch_w_s, ch_b_s=m_ch_b_s, ch_w_out=m_ch_w_out)
    V = dict(c_ctx=v_c_ctx, norm_g=v_norm_g, w_mod=v_w_mod, b_mod=v_b_mod, cv_w_in=v_cv_w_in, cv_dw=v_cv_dw, cv_db=v_cv_db, cv_ln_g=v_cv_ln_g, cv_ln_b=v_cv_ln_b, cv_w_out=v_cv_w_out, pl_w_in=v_pl_w_in, pl_w_grp=v_pl_w_grp, pl_scale=v_pl_scale, pl_w_out=v_pl_w_out, ml_w_in=v_ml_w_in, ml_q_norm=v_ml_q_norm, ml_kv_norm=v_ml_kv_norm, ml_w_uq=v_ml_w_uq, ml_w_ukv=v_ml_w_ukv, ml_nope_norm=v_ml_nope_norm, ml_rope_norm=v_ml_rope_norm, ml_w_out=v_ml_w_out, ch_w_in=v_ch_w_in, ch_ln_g=v_ch_ln_g, ch_ln_b=v_ch_ln_b, ch_w_s=v_ch_w_s, ch_b_s=v_ch_b_s, ch_w_out=v_ch_w_out)

    batch, lat_len, d = x.shape
    mx, my, mc = _my_place()
    chip = 2 * mx + my
    dev = 2 * chip + mc
    core = jnp.reshape(mc, (1,)).astype(jnp.int32)
    zero_off = jnp.zeros((1,), jnp.int32)
    big_names = list(BIG)

    sw = d // N_CHIP
    small_in = [c] + [jnp.pad(W[n].reshape(-1, W[n].shape[-1]), ((0, 0), (0, sw - W[n].shape[-1])))
                      for n in SMALL_SHARDED]
    pack1, spans1 = _pack_rows(small_in, sw)
    plan = _MeshPlan(W, M, V, core)
    gather1 = _ag8_copies(pack1)
    _run_hosted("ag8_inputs", _merge_hosted([gather1, plan.hosted("ag8_inputs")]))
    plan.after("ag8_inputs")
    got1 = gather1.results[0]
    c_all = got1[:, spans1[0][0]:spans1[0][0] + spans1[0][1]].reshape(N_DEV * batch, d)
    full_small = {}
    for n, (r0, nr, _) in zip(SMALL_SHARDED, spans1[1:]):
        blk = got1[0::2, r0:r0 + nr, :W[n].shape[-1]]
        full_small[n] = blk.transpose(1, 0, 2).reshape(nr, -1)

    c_rows = jnp.concatenate([c_all, c_ctx[None], jnp.zeros((MOD_ROWS - CTX_ROW - 1, d), F32)], axis=0)
    nmod = w_mod.shape[2]
    b_shard = lax.dynamic_slice(b_mod, (0, chip * nmod), (b_mod.shape[0], nmod))[:, None, :]
    mod_shard = _mod_fwd(c_rows, w_mod, b_shard, hosted=plan.hosted("mod_fwd"))
    plan.after("mod_fwd")
    mod_rows = mod_shard.transpose(1, 0, 2).reshape(MOD_ROWS, 1, 4 * nmod)
    got2 = _run_hosted("mod_exchange", _chip_rows_copies(mod_rows, batch, CTX_ROW))[0]
    mod_mine = got2.reshape(N_CHIP, batch + 1, 4, nmod).transpose(2, 1, 0, 3).reshape(4, batch + 1, 3 * d)
    mod_lat, mod_ctx = mod_mine[:, :batch], mod_mine[:, batch]
    mods = []
    for i in range(4):
        mods.append(tuple(
            jnp.stack([mod_lat[i, :, j * d:(j + 1) * d], jnp.broadcast_to(mod_ctx[i, j * d:(j + 1) * d], (batch, d))],
                      axis=1)[:, :, None, :] for j in range(3)))

    wk = dict(full_small)
    wk.update(norm_g=norm_g, cv_db=cv_db, cv_ln_g=cv_ln_g, cv_ln_b=cv_ln_b, ml_nope_norm=ml_nope_norm[0],
              ml_rope_norm=ml_rope_norm[0], ch_w_s=ch_w_s[0], ch_b_s=ch_b_s[0])
    plan.small = _prep_weights(wk)
    xm = jnp.concatenate([x, ctx], axis=1)
    loss_part, grad_x, dmods, g = _local_step(xm, loss_target, mods, plan, lat_len)
    g = _unprep_grads(g)

    lat_rows, ctx_rows = [], []
    for i in range(4):
        dsh, dsc, dgt = dmods[i]
        lat_rows.append(jnp.concatenate([dsh[:, 0, 0], dsc[:, 0, 0], dgt[:, 0, 0]], axis=1))
        zero = jnp.zeros((d,), F32)
        cs = [jnp.sum(t[:, 1, 0], axis=0) if ok else zero
              for t, ok in zip((dsh, dsc, dgt), (i <= 2, i <= 2, i <= 1))]
        ctx_rows.append(jnp.concatenate(cs, axis=0)[None])
    dmod_dev = jnp.concatenate(lat_rows + ctx_rows, axis=0)
    dmod_dev = jnp.pad(dmod_dev, ((0, (-dmod_dev.shape[0]) % 8), (0, 0)))
    gather3 = _ag8_column_copies(dmod_dev, nmod)
    _run_hosted("ag8_dmod", _merge_hosted([gather3, plan.hosted("ag8_dmod")]))
    plan.after("ag8_dmod")
    got3 = gather3.results[0]
    dlat = got3[:, :4 * batch].reshape(N_DEV, 4, batch, nmod).transpose(1, 0, 2, 3).reshape(4, N_DEV * batch, nmod)
    dctx_parts = got3[:, 4 * batch:4 * batch + 4].transpose(1, 0, 2)
    g_b_shard, dctx = _mod_bwd_rows(dlat, dctx_parts)
    d_rows = jnp.concatenate([dlat, dctx, jnp.zeros((4, MOD_ROWS - CTX_ROW - 1, nmod), F32)], axis=1)
    g_w_mod, dcc_part = _mod_bwd_w(c_rows.T, d_rows, w_mod, hosted=plan.hosted("mod_bwd_w"))
    plan.after("mod_bwd_w")

    wm2 = w_mod.reshape(-1, nmod)
    res_mod = _adamw("adamw_w_mod", None, g_w_mod.reshape(1, -1, nmod), wm2, M["w_mod"].reshape(-1, nmod),
                     V["w_mod"].reshape(-1, nmod), wm2.shape[0], hosted=plan.hosted("adamw_w_mod"))
    plan.after("adamw_w_mod")
    out = {"w_mod": tuple(r.reshape(w_mod.shape) for r in res_mod)}

    g_small_in = {n: g[n] for n in SMALL_SHARDED if n not in EARLY_SMALL}
    g_small_in.update(norm_g=g["norm_g"], cv_db=g["cv_db"], cv_ln_g=g["cv_ln_g"], cv_ln_b=g["cv_ln_b"],
                      ml_nope_norm=g["ml_nope_norm"], ml_rope_norm=g["ml_rope_norm"],
                      c_ctx=dcc_part.reshape(-1) * (mc == 0).astype(F32), loss=loss_part,
                      b_mod=lax.dynamic_update_slice(jnp.zeros((N_CHIP, 4, nmod), F32),
                                                     g_b_shard[None, :, 0] * (mc == 0).astype(F32), (chip, 0, 0)))
    small_names = list(g_small_in)
    pack4, spans4 = _pack_rows([g_small_in[n] for n in small_names], LANES, 128)
    gather4 = _ag8_copies(pack4)
    _run_hosted("ag8_small_grads", _merge_hosted([gather4, plan.hosted("ag8_small_grads")]))
    plan.after("ag8_small_grads")
    gs = dict(zip(small_names, _unpack_rows(_sum8(gather4.results[0]), spans4)))
    loss = gs["loss"][0, 0]
    gs.update(plan.early)
    gs["b_mod"] = gs["b_mod"].transpose(1, 0, 2).reshape(4, N_CHIP * nmod)
    for n in SMALL_SHARDED:
        wd = W[n].shape[-1]
        gs[n] = lax.dynamic_slice_in_dim(gs[n], chip * wd, wd, axis=1)
    upd_names = SMALL_REPLICATED + SMALL_SHARDED
    pw, spans_u = _pack_rows([W[n] for n in upd_names], LANES, 128)
    pm, _ = _pack_rows([M[n] for n in upd_names], LANES, 128)
    pv, _ = _pack_rows([V[n] for n in upd_names], LANES, 128)
    pg, _ = _pack_rows([gs[n].reshape(W[n].shape) for n in upd_names], LANES, 128)
    res_small = _adamw("adamw_small", None, pg[None], pw, pm, pv, pw.shape[0], hosted=plan.hosted("adamw_small"))
    plan.after("adamw_small")
    for n, vals in zip(upd_names, zip(*[_unpack_rows(r, spans_u) for r in res_small])):
        out[n] = vals
    out.update(plan.finish())

    outs = [loss, grad_x]
    for j in range(4):
        outs.extend(out[n][j] for n in WEIGHTS)
    return tuple(outs)
```

```python
import functools
import math

import jax
import jax.numpy as jnp
from jax import lax
from jax.experimental import pallas as pl
from jax.experimental.pallas import tpu as pltpu

F32 = jnp.float32
BF16 = jnp.bfloat16
ACT = jnp.float32
MESH = pl.DeviceIdType.MESH

EPS = 1e-6
GRID_W = 64
CONV_WIDTH = 31
CONV_HALF = CONV_WIDTH // 2
CONV_PAD = 16
POOL_WINDOWS = (2, 4, 8, 16)
POOL_HALF = max(POOL_WINDOWS) // 2
HEADS = 8
NOPE = 128
ROPE = 64
HEAD_W = 256
VDIM = 128
KV_RANK = 256
Q_RANK = 384
ATT_SCALE = (NOPE + ROPE) ** -0.5
LN2 = math.log(2.0)
Q_PRESCALE = ATT_SCALE / LN2
ROPE_THETA = 10000.0
CHUNK = 128
CHUNK_GROUPS = 8
LANES = 128
TM = 256
TM_LATENT = 512
TQ = 1024
TQ_BWD = 2048
ATT_RQ = 128
ATT_KC = 256
VMEM_LIMIT = 60 * 1024 * 1024

ADAM_LR = 0.001
ADAM_B1 = 0.9
ADAM_B2 = 0.999
ADAM_EPS = 1e-08
ADAM_WD = 0.01
ADAM_STEP = 10


def _dot(a, b):
    return jnp.dot(a.astype(BF16), b.astype(BF16), preferred_element_type=F32)


def _dot_nt(a, b):
    return lax.dot_general(a.astype(BF16), b.astype(BF16), (((1,), (1,)), ((), ())), preferred_element_type=F32)


def _dot_tn(a, b):
    return lax.dot_general(a.astype(BF16), b.astype(BF16), (((0,), (0,)), ((), ())), preferred_element_type=F32)


@jax.custom_vjp
def _mm(a, w):
    return _dot(a, w)


def _mm_fwd(a, w):
    return _dot(a, w), (a, w)


def _mm_bwd(res, ct):
    a, w = res
    return _dot_nt(ct, w), _dot_tn(a, ct)


_mm.defvjp(_mm_fwd, _mm_bwd)


def _swap16_impl(x):
    n = x.shape[-1]
    ax = x.ndim - 1
    lane = lax.broadcasted_iota(jnp.int32, x.shape, ax)
    up = pltpu.roll(x, n - 16, ax)
    dn = pltpu.roll(x, 16, ax)
    return jnp.where((lane % 32) < 16, up, dn)


@jax.custom_vjp
def _swap16(x):
    return _swap16_impl(x)


_swap16.defvjp(lambda x: (_swap16_impl(x), None), lambda _, ct: (_swap16_impl(ct),))


def _rms(x, g, n=None):
    n = x.shape[-1] if n is None else n
    return x * lax.rsqrt(jnp.sum(x * x, axis=-1, keepdims=True) * (1.0 / n) + EPS) * g


def _layernorm(x, g, b):
    mu = jnp.mean(x, axis=-1, keepdims=True)
    xc = x - mu
    var = jnp.mean(xc * xc, axis=-1, keepdims=True)
    return xc * lax.rsqrt(var + EPS) * g + b


def _silu(x):
    return x * jax.nn.sigmoid(x)


def _rope(x, cos, sin):
    return x * cos + _swap16(x) * sin


ANY = pl.BlockSpec(memory_space=pl.ANY)


class _Hosted:
    def __init__(self, arrays, out_shapes, sems, start, wait, aliases=None):
        self.arrays, self.out_shapes, self.sems = list(arrays), list(out_shapes), list(sems)
        self.start, self.wait, self.aliases = start, wait, dict(aliases or {})
        self.results = None


def _merge_hosted(parts):
    parts = [p for p in parts if p is not None]
    if not parts:
        return None
    if len(parts) == 1:
        return parts[0]
    offs, a0, o0, s0 = [], 0, 0, 0
    for p in parts:
        offs.append((a0, o0, s0))
        a0, o0, s0 = a0 + len(p.arrays), o0 + len(p.out_shapes), s0 + len(p.sems)

    def run(which):
        def f(ins, outs, sems):
            for p, (a, o, s) in zip(parts, offs):
                getattr(p, which)(ins[a:a + len(p.arrays)], outs[o:o + len(p.out_shapes)], sems[s:s + len(p.sems)])
        return f

    aliases = {}
    for p, (a, o, _) in zip(parts, offs):
        aliases.update({a + i: o + j for i, j in p.aliases.items()})
    merged = _Hosted(sum((p.arrays for p in parts), []), sum((p.out_shapes for p in parts), []),
                     sum((p.sems for p in parts), []), run("start"), run("wait"), aliases)
    merged.parts, merged.offs = parts, offs
    return merged


def _deliver(hosted, results):
    hosted.results = list(results)
    for p, (_, o, _) in zip(getattr(hosted, "parts", []), getattr(hosted, "offs", [])):
        _deliver(p, results[o:o + len(p.out_shapes)])


def _pcall(body, *, name, grid, in_specs, out_specs, out_shape, args, hosted=None, vmem_limit=True, scratch=()):
    n_in, n_out, n_scr = len(args), len(out_shape), len(scratch)
    kwargs = dict(scratch_shapes=list(scratch)) if scratch else {}
    if hosted is not None:
        nhi, nho, inner = len(hosted.arrays), len(hosted.out_shapes), body

        def body(*refs):
            ins, hin = refs[:n_in], refs[n_in:n_in + nhi]
            outs, hout = refs[n_in + nhi:n_in + nhi + n_out], refs[n_in + nhi + n_out:n_in + nhi + n_out + nho]
            own = refs[n_in + nhi + n_out + nho:n_in + nhi + n_out + nho + n_scr]
            sems = refs[n_in + nhi + n_out + nho + n_scr:]
            first, last = None, None
            for k, g in enumerate(grid):
                f, l = pl.program_id(k) == 0, pl.program_id(k) == g - 1
                first = f if first is None else jnp.logical_and(first, f)
                last = l if last is None else jnp.logical_and(last, l)

            @pl.when(first)
            def _():
                hosted.start(hin, hout, sems)

            inner(*ins, *outs, *own)

            @pl.when(last)
            def _():
                hosted.wait(hin, hout, sems)

        in_specs = list(in_specs) + [ANY] * nhi
        out_specs = list(out_specs) + [ANY] * nho
        out_shape = list(out_shape) + hosted.out_shapes
        args = list(args) + hosted.arrays
        kwargs = dict(scratch_shapes=list(scratch) + hosted.sems,
                      input_output_aliases={n_in + i: n_out + j for i, j in hosted.aliases.items()})
    params = dict(dimension_semantics=("arbitrary",) * len(grid))
    if vmem_limit:
        params["vmem_limit_bytes"] = VMEM_LIMIT
    res = pl.pallas_call(body, name=name, grid=grid, in_specs=list(in_specs), out_specs=list(out_specs),
                         out_shape=list(out_shape), compiler_params=pltpu.CompilerParams(**params), **kwargs)(*args)
    if hosted is not None:
        _deliver(hosted, res[n_out:])
    return list(res[:n_out])


def _run_hosted(name, hosted):
    nhi, nho = len(hosted.arrays), len(hosted.out_shapes)

    def body(*refs):
        ins, outs, sems = refs[:nhi], refs[nhi:nhi + nho], refs[nhi + nho:]
        hosted.start(ins, outs, sems)
        hosted.wait(ins, outs, sems)

    res = pl.pallas_call(body, name=name, in_specs=[ANY] * nhi, out_specs=[ANY] * nho, out_shape=hosted.out_shapes,
                         scratch_shapes=hosted.sems, input_output_aliases=hosted.aliases)(*hosted.arrays)
    _deliver(hosted, res)
    return list(res)


def _const_spec(shape, single=False):
    nd = len(shape)
    if single:
        return pl.BlockSpec(shape, lambda b, i: (0,) * nd, pipeline_mode=pl.Buffered(1))
    return pl.BlockSpec(shape, lambda b, i: (0,) * nd)


def _tile_spec(arr, n_lat_tiles, lat_only=False, tm=TM):
    bt, _, cw = arr.shape
    if lat_only:
        return pl.BlockSpec((1, tm, cw), lambda b, i: (b if bt > 1 else 0, jnp.minimum(i, n_lat_tiles - 1), 0))
    return pl.BlockSpec((1, tm, cw), lambda b, i: (b if bt > 1 else 0, i, 0))


def _eparam_spec(arr, n_lat_tiles):
    cw = arr.shape[-1]
    return pl.BlockSpec((1, 1, 1, cw), lambda b, i: (b, (i >= n_lat_tiles).astype(jnp.int32), 0, 0))


def _stage_fwd(name, *, pre, post, wsel, splits, tiles, eparams, sparams, weights, out_widths, out_dtypes,
               batch, n_tiles, n_lat_tiles, hosted=None, tm=TM):
    nt, ne, ns, nw = len(tiles), len(eparams), len(sparams), len(weights)

    def body(*refs):
        t_refs = refs[:nt]
        e_refs = refs[nt:nt + ne]
        s_refs = refs[nt + ne:nt + ne + ns]
        w_refs = refs[nt + ne + ns:nt + ne + ns + nw]
        o_refs = refs[nt + ne + ns + nw:]
        tv = [r[0].astype(F32) for r in t_refs]
        ev = [r[0, 0] for r in e_refs]
        sv = [r[...] for r in s_refs]
        a = pre(tv, ev, sv)
        z = [_dot(a[wsel[j]], w_refs[j][...]) for j in range(nw)]
        if post is None:
            outs = [z[j][:, s:s + w] for (j, s, w) in splits]
        else:
            outs = post(z, tv, ev, sv)
        for o_ref, o in zip(o_refs, outs):
            o_ref[0] = o.astype(o_ref.dtype)

    in_specs = ([_tile_spec(t, n_lat_tiles, tm=tm) for t in tiles] + [_eparam_spec(e, n_lat_tiles) for e in eparams]
                + [_const_spec(s.shape) for s in sparams] + [_const_spec(w.shape, single=True) for w in weights])
    out_shape = [jax.ShapeDtypeStruct((batch, n_tiles * tm, w), dt) for w, dt in zip(out_widths, out_dtypes)]
    out_specs = [pl.BlockSpec((1, tm, w), lambda b, i: (b, i, 0)) for w in out_widths]
    return _pcall(body, name=name, grid=(batch, n_tiles), in_specs=in_specs, out_specs=out_specs,
                  out_shape=out_shape, args=[*tiles, *eparams, *sparams, *weights], hosted=hosted)


def _stage_bwd(name, *, pre, post, wsel, splits, tiles, tile_diff, eparams, sparams, weights, cots, cot_lat_only,
               batch, n_tiles, n_lat_tiles, add=None, add_lat_only=False, hosted=None, w_col_stack=None,
               dt_lat_only=False, tm=TM):
    nt, ne, ns, nw, nc = len(tiles), len(eparams), len(sparams), len(weights), len(cots)
    diff_idx = [k for k in range(nt) if tile_diff[k]]
    nd = len(diff_idx)
    has_add = add is not None
    w_col_stack = w_col_stack or [None] * nw

    def body(*refs):
        pos = 0
        t_refs = refs[pos:pos + nt]; pos += nt
        e_refs = refs[pos:pos + ne]; pos += ne
        s_refs = refs[pos:pos + ns]; pos += ns
        w_refs = refs[pos:pos + nw]; pos += nw
        c_refs = refs[pos:pos + nc]; pos += nc
        if has_add:
            add_ref = refs[pos]; pos += 1
        dt_refs = refs[pos:pos + nd]; pos += nd
        de_refs = refs[pos:pos + ne]; pos += ne
        ds_refs = refs[pos:pos + ns]; pos += ns
        dw_refs = refs[pos:pos + nw]; pos += nw

        b = pl.program_id(0)
        i = pl.program_id(1)
        is_lat = i < n_lat_tiles
        tv = [r[0].astype(F32) for r in t_refs]
        ev = tuple(r[0, 0] for r in e_refs)
        sv = tuple(r[...] for r in s_refs)
        dv0 = tuple(tv[k] for k in diff_idx)

        def merge(dv):
            full = list(tv)
            for k, v in zip(diff_idx, dv):
                full[k] = v
            return full

        def pre_f(dv, ev_, sv_):
            return tuple(pre(merge(dv), list(ev_), list(sv_)))

        a, vjp_pre = jax.vjp(pre_f, dv0, ev, sv)
        cv = []
        for c_ref, lat in zip(c_refs, cot_lat_only):
            c = c_ref[0].astype(F32)
            cv.append(jnp.where(is_lat, c, 0.0) if lat else c)
        if post is None:
            dz = []
            for j in range(nw):
                parts = [cv[k] for k, (jj, _, _) in enumerate(splits) if jj == j]
                dz.append(parts[0] if len(parts) == 1 else jnp.concatenate(parts, axis=1))
            dt2 = de2 = ds2 = None
        else:
            z = tuple(_dot(a[wsel[j]], w_refs[j][...]) for j in range(nw))

            def post_f(z_, dv, ev_, sv_):
                return tuple(post(list(z_), merge(dv), list(ev_), list(sv_)))

            _, vjp_post = jax.vjp(post_f, z, dv0, ev, sv)
            dz, dt2, de2, ds2 = vjp_post(tuple(cv))
        da = [None] * len(a)
        dws = []
        for j in range(nw):
            g = _dot_nt(dz[j], w_refs[j][...])
            da[wsel[j]] = g if da[wsel[j]] is None else da[wsel[j]] + g
            dws.append(_dot_tn(a[wsel[j]], dz[j]))
        da = tuple(jnp.zeros_like(a[k]) if da[k] is None else da[k] for k in range(len(a)))
        dt1, de1, ds1 = vjp_pre(da)

        def plus(u, v):
            return u if v is None else u + v

        for k in range(nd):
            val = plus(dt1[k], None if dt2 is None else dt2[k])
            if has_add and k == 0:
                addv = add_ref[0].astype(F32)
                val = val + (jnp.where(is_lat, addv, 0.0) if add_lat_only else addv)
            if dt_lat_only:
                @pl.when(is_lat)
                def _(k=k, val=val):
                    dt_refs[k][0] = val.astype(dt_refs[k].dtype)
            else:
                dt_refs[k][0] = val.astype(dt_refs[k].dtype)

        seg_first = jnp.logical_or(i == 0, i == n_lat_tiles)
        for k in range(ne):
            val = plus(de1[k], None if de2 is None else de2[k])

            @pl.when(seg_first)
            def _(k=k, val=val):
                de_refs[k][0, 0] = val

            @pl.when(jnp.logical_not(seg_first))
            def _(k=k, val=val):
                de_refs[k][0, 0] += val

        first = jnp.logical_and(b == 0, i == 0)
        acc = [(ds_refs[k], plus(ds1[k], None if ds2 is None else ds2[k])) for k in range(ns)]
        for j in range(nw):
            if w_col_stack[j]:
                cw = dws[j].shape[1] // w_col_stack[j]
                acc += [(dw_refs[j].at[c], dws[j][:, c * cw:(c + 1) * cw]) for c in range(w_col_stack[j])]
            else:
                acc.append((dw_refs[j], dws[j]))
        for ref, val in acc:
            @pl.when(first)
            def _(ref=ref, val=val):
                ref[...] = val

            @pl.when(jnp.logical_not(first))
            def _(ref=ref, val=val):
                ref[...] += val

    in_specs = ([_tile_spec(t, n_lat_tiles, tm=tm) for t in tiles] + [_eparam_spec(e, n_lat_tiles) for e in eparams]
                + [_const_spec(s.shape) for s in sparams] + [_const_spec(w.shape, single=True) for w in weights]
                + [_tile_spec(c, n_lat_tiles, lat, tm) for c, lat in zip(cots, cot_lat_only)])
    args = [*tiles, *eparams, *sparams, *weights, *cots]
    if has_add:
        in_specs.append(_tile_spec(add, n_lat_tiles, add_lat_only, tm))
        args.append(add)
    dt_tiles = n_lat_tiles if dt_lat_only else n_tiles
    out_shape = [jax.ShapeDtypeStruct((batch, dt_tiles * tm, tiles[k].shape[-1]), F32) for k in diff_idx]
    out_specs = [pl.BlockSpec((1, tm, tiles[k].shape[-1]), lambda b, i: (b, jnp.minimum(i, dt_tiles - 1), 0))
                 for k in diff_idx]
    out_shape += [jax.ShapeDtypeStruct(e.shape, F32) for e in eparams]
    out_specs += [_eparam_spec(e, n_lat_tiles) for e in eparams]
    out_shape += [jax.ShapeDtypeStruct(s.shape, F32) for s in sparams]
    out_specs += [_const_spec(s.shape) for s in sparams]
    dw_shapes = [(n, w.shape[0], w.shape[1] // n) if n else w.shape for w, n in zip(weights, w_col_stack)]
    out_shape += [jax.ShapeDtypeStruct(s, F32) for s in dw_shapes]
    out_specs += [_const_spec(s, single=True) for s in dw_shapes]
    res = _pcall(body, name=name, grid=(batch, n_tiles), in_specs=in_specs, out_specs=out_specs,
                 out_shape=out_shape, args=args, hosted=hosted)
    return res[:nd], res[nd:nd + ne], res[nd + ne:nd + ne + ns], res[nd + ne + ns:]


def _pre_adaln(tv, ev, sv):
    x = tv[0]
    sh, sc = ev[0], ev[1]
    return [_rms(x, sv[0]) * (1.0 + sc) + sh]


def _post_residual(x_index):
    def post(z, tv, ev, sv):
        return [tv[x_index] + ev[-1] * z[0]]
    return post


def _pre_conv_out(tv, ev, sv):
    c1, gg = tv[0], tv[1]
    return [_silu(_layernorm(c1, sv[0], sv[1])) * _silu(gg)]


def _pre_pool_out(tv, ev, sv):
    pooled, gg = tv[0], tv[1]
    w_grp, scale = sv[0], sv[1]
    gw = w_grp.shape[-1]
    y = jnp.concatenate([_mm(pooled[:, k * gw:(k + 1) * gw], w_grp[k]) for k in range(w_grp.shape[0])], axis=1)
    return [y * scale * _silu(gg)]


def _pre_rms_only(tv, ev, sv):
    return [_rms(tv[0], sv[0])]


def _post_mla_keys(z, tv, ev, sv):
    krp, cos, sin = tv[1], tv[2], tv[3]
    nope_g, rope_g = sv[1], sv[2]
    kv = z[0]
    kr = _rope(_rms(krp, rope_g, ROPE), cos, sin)
    ks, vs = [], []
    for h in range(HEADS):
        ks.append(_rms(kv[:, h * 2 * NOPE:h * 2 * NOPE + NOPE], nope_g))
        ks.append(kr)
        vs.append(kv[:, h * 2 * NOPE + NOPE:(h + 1) * 2 * NOPE])
    return [jnp.concatenate(ks, axis=1), jnp.concatenate(vs, axis=1)]


def _post_mla_queries(z, tv, ev, sv):
    cos, sin = tv[1], tv[2]
    nope_g, rope_g = sv[1], sv[2]
    q = z[0]
    qs = []
    for h in range(HEADS):
        qs.append(_rms(q[:, h * HEAD_W:h * HEAD_W + NOPE], nope_g))
        qs.append(_rope(_rms(q[:, h * HEAD_W + NOPE:(h + 1) * HEAD_W], rope_g, ROPE), cos, sin))
    return [jnp.concatenate(qs, axis=1) * Q_PRESCALE]


def _pre_mla_out(tv, ev, sv):
    return [tv[0] * _silu(tv[1])]


def _pre_chunk_out(tv, ev, sv):
    u, v, gg = tv[0], tv[1], tv[2]
    ln_g, ln_b, w_s, b_s = sv
    vn = _layernorm(v, ln_g, ln_b)
    rows = []
    for n in range(vn.shape[0] // CHUNK):
        blk = vn[n * CHUNK:(n + 1) * CHUNK]
        cols = [_mm(w_s[g], blk[:, g * LANES:(g + 1) * LANES]) + b_s[:, g:g + 1] for g in range(CHUNK_GROUPS)]
        rows.append(jnp.concatenate(cols, axis=1))
    s = jnp.concatenate(rows, axis=0)
    return [u * s * _silu(gg)]


def _segments(lat_len, tot_len):
    segs = [(0, lat_len)]
    if tot_len > lat_len:
        segs.append((lat_len, tot_len - lat_len))
    return segs


def _pad_rows(x):
    z = jnp.zeros((CONV_PAD, x.shape[1]), x.dtype)
    return jnp.concatenate([z, x, z], axis=0)


def _shifted(xp, j):
    n = xp.shape[0] - 2 * CONV_PAD
    if j != 0:
        xp = pltpu.roll(xp, (-j) % xp.shape[0], 0)
    return xp[CONV_PAD:CONV_PAD + n]


def _conv_fwd(a, bgate, dw, db, lat_len, hosted=None):
    batch, tot, e = a.shape
    segs = _segments(lat_len, tot)

    def body(a_ref, b_ref, dw_ref, db_ref, o_ref):
        w = dw_ref[...]
        for (s0, n) in segs:
            y = a_ref[0, s0:s0 + n, :].astype(F32) * jax.nn.sigmoid(b_ref[0, s0:s0 + n, :].astype(F32))
            yp = _pad_rows(y)
            acc = jnp.zeros_like(y) + db_ref[...]
            for k in range(CONV_WIDTH):
                acc = acc + _shifted(yp, k - CONV_HALF) * w[k:k + 1, :]
            o_ref[0, s0:s0 + n, :] = acc.astype(o_ref.dtype)

    blk = pl.BlockSpec((1, tot, LANES), lambda b, cb: (b, 0, cb))
    return _pcall(
        body, name="conv_fwd", grid=(batch, e // LANES),
        in_specs=[blk, blk, pl.BlockSpec((CONV_WIDTH, LANES), lambda b, cb: (0, cb)),
                  pl.BlockSpec((1, LANES), lambda b, cb: (0, cb))],
        out_specs=[blk], out_shape=[jax.ShapeDtypeStruct(a.shape, ACT)], args=[a, bgate, dw, db], hosted=hosted)[0]


def _conv_bwd(a, bgate, dw, dc1, lat_len, hosted=None):
    batch, tot, e = a.shape
    segs = _segments(lat_len, tot)

    def body(a_ref, b_ref, dw_ref, dc_ref, da_ref, dg_ref, ddw_ref, ddb_ref):
        b = pl.program_id(1)
        w = dw_ref[...]
        ddw_rows = [None] * CONV_WIDTH
        ddb = None
        for (s0, n) in segs:
            av = a_ref[0, s0:s0 + n, :].astype(F32)
            sg = jax.nn.sigmoid(b_ref[0, s0:s0 + n, :].astype(F32))
            y = av * sg
            dc = dc_ref[0, s0:s0 + n, :]
            yp, dcp = _pad_rows(y), _pad_rows(dc)
            dy = jnp.zeros_like(y)
            for k in range(CONV_WIDTH):
                j = k - CONV_HALF
                dy = dy + _shifted(dcp, -j) * w[k:k + 1, :]
                r = jnp.sum(dc * _shifted(yp, j), axis=0, keepdims=True)
                ddw_rows[k] = r if ddw_rows[k] is None else ddw_rows[k] + r
            r = jnp.sum(dc, axis=0, keepdims=True)
            ddb = r if ddb is None else ddb + r
            da_ref[0, s0:s0 + n, :] = dy * sg
            dg_ref[0, s0:s0 + n, :] = dy * av * sg * (1.0 - sg)

        @pl.when(b == 0)
        def _():
            ddw_ref[...] = jnp.zeros_like(ddw_ref)
            ddb_ref[...] = jnp.zeros_like(ddb_ref)

        for k in range(CONV_WIDTH):
            ddw_ref[k:k + 1, :] += ddw_rows[k]
        ddb_ref[...] += ddb

    blk = pl.BlockSpec((1, tot, LANES), lambda cb, b: (b, 0, cb))
    wspec = pl.BlockSpec((CONV_WIDTH, LANES), lambda cb, b: (0, cb))
    bspec = pl.BlockSpec((1, LANES), lambda cb, b: (0, cb))
    return _pcall(
        body, name="conv_bwd", grid=(e // LANES, batch),
        in_specs=[blk, blk, wspec, blk],
        out_specs=[blk, blk, wspec, bspec],
        out_shape=[jax.ShapeDtypeStruct(a.shape, F32), jax.ShapeDtypeStruct(a.shape, F32),
                   jax.ShapeDtypeStruct((CONV_WIDTH, e), F32), jax.ShapeDtypeStruct((1, e), F32)],
        args=[a, bgate, dw, dc1], hosted=hosted)


def _pool_counts(n, half, shape):
    t = lax.broadcasted_iota(jnp.int32, shape, 0)
    cnt = jnp.minimum(t + half, n) - jnp.maximum(t - half, 0)
    return cnt.astype(F32)


def _per_group(fn):
    for k, window in enumerate(POOL_WINDOWS):
        @pl.when(pl.program_id(1) == k)
        def _(window=window):
            fn(window // 2)


def _pool_fwd(v, lat_len, hosted=None):
    batch, tot, e = v.shape
    gw = e // len(POOL_WINDOWS)
    segs = _segments(lat_len, tot)

    def body(v_ref, o_ref):
        def group(half):
            for (s0, n) in segs:
                x = v_ref[0, s0:s0 + n, :]
                xp = _pad_rows(x)
                acc = _shifted(xp, -half)
                for j in range(-half + 1, half):
                    acc = acc + _shifted(xp, j)
                o_ref[0, s0:s0 + n, :] = (acc / _pool_counts(n, half, x.shape) - x).astype(o_ref.dtype)

        _per_group(group)

    blk = pl.BlockSpec((1, tot, gw), lambda b, g: (b, 0, g))
    return _pcall(body, name="pool_fwd", grid=(batch, len(POOL_WINDOWS)), in_specs=[blk], out_specs=[blk],
                  out_shape=[jax.ShapeDtypeStruct(v.shape, ACT)], args=[v], hosted=hosted)[0]


def _pool_bwd(dp, lat_len):
    batch, tot, e = dp.shape
    gw = e // len(POOL_WINDOWS)
    segs = _segments(lat_len, tot)

    def body(d_ref, o_ref):
        def group(half):
            for (s0, n) in segs:
                d = d_ref[0, s0:s0 + n, :]
                dnp = _pad_rows(d / _pool_counts(n, half, d.shape))
                acc = _shifted(dnp, half)
                for j in range(-half + 1, half):
                    acc = acc + _shifted(dnp, -j)
                o_ref[0, s0:s0 + n, :] = acc - d

        _per_group(group)

    blk = pl.BlockSpec((1, tot, gw), lambda b, g: (b, 0, g))
    return pl.pallas_call(
        body, name="pool_bwd", grid=(batch, len(POOL_WINDOWS)), in_specs=[blk], out_specs=blk,
        out_shape=jax.ShapeDtypeStruct(dp.shape, F32),
        compiler_params=pltpu.CompilerParams(dimension_semantics=("arbitrary", "arbitrary"),
                                             vmem_limit_bytes=VMEM_LIMIT),
    )(dp)


def _attn_fwd(q, k, v, hosted=None):
    batch, lq, _ = q.shape
    tk = k.shape[1]
    tq = min(TQ, lq)

    def body(q_ref, k_ref, v_ref, o_ref, lse_ref):
        s2 = _dot_nt(q_ref[0], k_ref[0])
        m2 = jnp.max(s2, axis=-1, keepdims=True)
        e = jnp.exp2(s2 - m2)
        l = jnp.sum(e, axis=-1, keepdims=True)
        o_ref[0] = (_dot(e, v_ref[0]) / l).astype(o_ref.dtype)
        lse_ref[0, 0] = m2 + jnp.log2(l)

    return _pcall(
        body, name="attn_fwd", grid=(batch, HEADS, lq // tq),
        in_specs=[pl.BlockSpec((1, tq, HEAD_W), lambda b, h, i: (b, i, h)),
                  pl.BlockSpec((1, tk, HEAD_W), lambda b, h, i: (b, 0, h)),
                  pl.BlockSpec((1, tk, VDIM), lambda b, h, i: (b, 0, h))],
        out_specs=[pl.BlockSpec((1, tq, VDIM), lambda b, h, i: (b, i, h)),
                   pl.BlockSpec((1, 1, tq, 1), lambda b, h, i: (b, h, i, 0))],
        out_shape=[jax.ShapeDtypeStruct((batch, lq, HEADS * VDIM), ACT),
                   jax.ShapeDtypeStruct((batch, HEADS, lq, 1), F32)], args=[q, k, v], hosted=hosted)


def _attn_bwd(q, k, v, o, lse, do, hosted=None):
    batch, lq, _ = q.shape
    tk = k.shape[1]
    tq = min(TQ_BWD, lq)

    def body(q_ref, k_ref, v_ref, o_ref, lse_ref, do_ref, dq_ref, dk_ref, dv_ref, p_scr, ds_scr):
        i = pl.program_id(2)
        nr = tq // ATT_RQ
        rows = [slice(r * ATT_RQ, (r + 1) * ATT_RQ) for r in range(nr)]
        qv = [q_ref[0, rw, :] for rw in rows]
        dob = [do_ref[0, rw, :].astype(BF16) for rw in rows]
        row_lse = [lse_ref[0, 0, rw, :] for rw in rows]
        delta = [jnp.sum(do_ref[0, rw, :] * o_ref[0, rw, :], axis=-1, keepdims=True) for rw in rows]
        for c in range(tk // ATT_KC):
            keys = slice(c * ATT_KC, (c + 1) * ATT_KC)
            kc, vc = k_ref[0, keys, :], v_ref[0, keys, :]
            for r in range(nr):
                p = jnp.exp2(_dot_nt(qv[r], kc) - row_lse[r])
                dp = _dot_nt(dob[r], vc)
                p_scr[rows[r], keys] = p.astype(BF16)
                ds_scr[rows[r], keys] = (p * (dp - delta[r]) * LN2).astype(BF16)
        dq_ref[0] = _dot(ds_scr[...], k_ref[0])
        dk = _dot_tn(ds_scr[...], q_ref[0])
        dv = _dot_tn(p_scr[...], do_ref[0])

        @pl.when(i == 0)
        def _():
            dk_ref[0] = dk
            dv_ref[0] = dv

        @pl.when(i != 0)
        def _():
            dk_ref[0] += dk
            dv_ref[0] += dv

    return _pcall(
        body, name="attn_bwd", grid=(batch, HEADS, lq // tq),
        in_specs=[pl.BlockSpec((1, tq, HEAD_W), lambda b, h, i: (b, i, h)),
                  pl.BlockSpec((1, tk, HEAD_W), lambda b, h, i: (b, 0, h)),
                  pl.BlockSpec((1, tk, VDIM), lambda b, h, i: (b, 0, h)),
                  pl.BlockSpec((1, tq, VDIM), lambda b, h, i: (b, i, h)),
                  pl.BlockSpec((1, 1, tq, 1), lambda b, h, i: (b, h, i, 0)),
                  pl.BlockSpec((1, tq, VDIM), lambda b, h, i: (b, i, h))],
        out_specs=[pl.BlockSpec((1, tq, HEAD_W), lambda b, h, i: (b, i, h)),
                   pl.BlockSpec((1, tk, HEAD_W), lambda b, h, i: (b, 0, h)),
                   pl.BlockSpec((1, tk, VDIM), lambda b, h, i: (b, 0, h))],
        out_shape=[jax.ShapeDtypeStruct(q.shape, F32), jax.ShapeDtypeStruct(k.shape, F32),
                   jax.ShapeDtypeStruct(v.shape, F32)],
        args=[q, k, v, o, lse, do], hosted=hosted,
        scratch=[pltpu.VMEM((tq, tk), BF16), pltpu.VMEM((tq, tk), BF16)])


def _loss_kernel(y, target):
    batch, lq, d = y.shape

    def body(y_ref, t_ref, l_ref, dy_ref):
        first = jnp.logical_and(pl.program_id(0) == 0, pl.program_id(1) == 0)
        err = y_ref[0] - t_ref[0]
        dy_ref[0] = err * (1.0 / d)
        part = jnp.zeros((1, LANES), F32) + jnp.sum(err * err) * (0.5 / d)

        @pl.when(first)
        def _():
            l_ref[...] = part

        @pl.when(jnp.logical_not(first))
        def _():
            l_ref[...] += part

    blk = pl.BlockSpec((1, TM, d), lambda b, i: (b, i, 0))
    return pl.pallas_call(
        body, name="loss_head", grid=(batch, lq // TM), in_specs=[blk, blk],
        out_specs=[pl.BlockSpec((1, LANES), lambda b, i: (0, 0)), blk],
        out_shape=[jax.ShapeDtypeStruct((1, LANES), F32), jax.ShapeDtypeStruct(y.shape, F32)],
        compiler_params=pltpu.CompilerParams(dimension_semantics=("arbitrary", "arbitrary")),
    )(y, target)


def _rope_tables(lat_len, ctx_len):
    rows = lat_len // GRID_W
    axis_dim = ROPE // 2
    freqs = ROPE_THETA ** (-jnp.arange(0, axis_dim, 2, dtype=F32) / axis_dim)
    ar = jnp.arange(rows, dtype=F32)[:, None] * freqs
    ac = jnp.arange(GRID_W, dtype=F32)[:, None] * freqs
    small = lax.optimization_barrier((jnp.cos(ar), jnp.sin(ar), jnp.cos(ac), jnp.sin(ac)))
    cr, sr = (jnp.repeat(t, GRID_W, axis=0) for t in small[:2])
    cc, sc = (jnp.tile(t, (rows, 1)) for t in small[2:])
    pad = jnp.zeros((lat_len, LANES - ROPE), F32)
    cos = jnp.concatenate([cr, cr, cc, cc, pad], axis=1)
    sin = jnp.concatenate([-sr, sr, -sc, sc, pad], axis=1)
    ident = jnp.concatenate([jnp.ones((ctx_len, ROPE), F32), jnp.zeros((ctx_len, LANES - ROPE), F32)], axis=1)
    cos = jnp.concatenate([cos, ident], axis=0)
    sin = jnp.concatenate([sin, jnp.zeros((ctx_len, LANES), F32)], axis=0)
    return cos[None], sin[None]


def _prep_weights(w):
    p = dict(w)
    kvc = KV_RANK + ROPE
    if "ml_w_in" in w:
        wi = w["ml_w_in"]
        p["ml_w_in"] = jnp.concatenate(
            [wi[:, :kvc], jnp.zeros((wi.shape[0], LANES - ROPE), wi.dtype), wi[:, kvc:]], axis=1)
    if "ml_w_uq" in w:
        uq = w["ml_w_uq"].reshape(Q_RANK, HEADS, NOPE + ROPE)
        p["ml_w_uq"] = jnp.pad(uq, ((0, 0), (0, 0), (0, HEAD_W - NOPE - ROPE))).reshape(Q_RANK, HEADS * HEAD_W)
    if "ml_rope_norm" in w:
        p["ml_rope_norm"] = jnp.pad(w["ml_rope_norm"], ((0, 0), (0, LANES - ROPE)))
    return p


def _unprep_grads(g):
    out = dict(g)
    kvc = KV_RANK + ROPE
    if "ml_w_in" in g:
        wi = g["ml_w_in"]
        out["ml_w_in"] = jnp.concatenate([wi[:, :kvc], wi[:, kvc + LANES - ROPE:]], axis=1)
    if "ml_w_uq" in g:
        uq = g["ml_w_uq"].reshape(Q_RANK, HEADS, HEAD_W)
        out["ml_w_uq"] = uq[:, :, :NOPE + ROPE].reshape(Q_RANK, HEADS * (NOPE + ROPE))
    if "ml_rope_norm" in g:
        out["ml_rope_norm"] = g["ml_rope_norm"][:, :ROPE]
    return out


LAYER_WEIGHTS = (("cv_w_in", "cv_w_out"), ("pl_w_in", "pl_w_grp", "pl_w_out"),
                 ("ml_w_in", "ml_w_uq", "ml_w_ukv", "ml_w_out"), ("ch_w_in", "ch_w_out"))


class _LocalPlan:
    def __init__(self, w):
        self.small = w
        self.grads = {}

    def weights(self, names):
        return {n: self.small[n] for n in names}

    def hosted(self, tag):
        return None

    def after(self, tag):
        pass

    def note(self, values):
        pass

    def layer_grads(self, layer, grads):
        self.grads.update(grads)


def _local_step(xm, target, mods, plan, lat_len):
    batch, tot, d = xm.shape
    e = d
    n_all, n_lat = tot // TM, lat_len // TM
    cos, sin = _rope_tables(lat_len, tot - lat_len)
    g = {}
    w = dict(plan.small)

    def hosting(tag, fn, *args, **kwargs):
        out = fn(*args, hosted=plan.hosted(tag), **kwargs)
        plan.after(tag)
        return out

    def s1_splits(widths):
        out, s = [], 0
        for wd in widths:
            out.append((0, s, wd))
            s += wd
        return out

    tml = TM_LATENT if lat_len % TM_LATENT == 0 else TM
    n_big = lat_len // tml

    def lat_tiles(n_tiles, tm):
        return n_lat if tm == TM else n_tiles

    def fwd_in(name, x, mod, gi, wname, widths, n_tiles, dtypes=None, tm=TM):
        return hosting(name, _stage_fwd, name, pre=_pre_adaln, post=None, wsel=[0], splits=s1_splits(widths),
                       tiles=[x], eparams=[mod[0], mod[1]], sparams=[w["norm_g"][gi:gi + 1]], weights=[w[wname]],
                       out_widths=widths, out_dtypes=dtypes or [ACT] * len(widths), batch=batch, n_tiles=n_tiles,
                       n_lat_tiles=lat_tiles(n_tiles, tm), tm=tm)

    def bwd_in(name, x, mod, gi, wname, widths, n_tiles, cots, lat_only, add, add_lat_only, stack=None,
               dx_lat_only=False):
        (dx,), (dsh, dsc), (dg,), (dw,) = hosting(
            name, _stage_bwd, name, pre=_pre_adaln, post=None, wsel=[0], splits=s1_splits(widths), tiles=[x],
            tile_diff=[True], eparams=[mod[0], mod[1]], sparams=[w["norm_g"][gi:gi + 1]], weights=[w[wname]],
            cots=cots, cot_lat_only=lat_only, batch=batch, n_tiles=n_tiles, n_lat_tiles=n_lat, add=add,
            add_lat_only=add_lat_only, w_col_stack=[stack], dt_lat_only=dx_lat_only)
        return dx, dsh, dsc, dg, dw

    def fwd_out(name, pre, tiles, mod, sparams, wname, n_tiles, tm=TM):
        return hosting(name, _stage_fwd, name, pre=pre, post=_post_residual(len(tiles) - 1), wsel=[0], splits=None,
                       tiles=tiles, eparams=[mod[2]], sparams=sparams, weights=[w[wname]], out_widths=[d],
                       out_dtypes=[F32], batch=batch, n_tiles=n_tiles, n_lat_tiles=lat_tiles(n_tiles, tm), tm=tm)[0]

    def bwd_out(name, pre, tiles, mod, sparams, wname, n_tiles, cot, tm=TM):
        diff = [True] * (len(tiles) - 1) + [False]
        dts, (dgt,), dss, (dw,) = hosting(
            name, _stage_bwd, name, pre=pre, post=_post_residual(len(tiles) - 1), wsel=[0], splits=None, tiles=tiles,
            tile_diff=diff, eparams=[mod[2]], sparams=sparams, weights=[w[wname]], cots=[cot], cot_lat_only=[False],
            batch=batch, n_tiles=n_tiles, n_lat_tiles=lat_tiles(n_tiles, tm), tm=tm)
        return dts, dgt, dss, dw

    w.update(plan.weights(("cv_w_in",)))
    cv_s = [w["cv_ln_g"], w["cv_ln_b"]]
    a0, b0, g0 = fwd_in("cv_in_fwd", xm, mods[0], 0, "cv_w_in", [e, e, e], n_all)
    c1 = hosting("conv_fwd", _conv_fwd, a0, b0, w["cv_dw"], w["cv_db"], lat_len)
    w.update(plan.weights(("cv_w_out",)))
    x1 = fwd_out("cv_out_fwd", _pre_conv_out, [c1, g0, xm], mods[0], cv_s, "cv_w_out", n_all)

    w.update(plan.weights(LAYER_WEIGHTS[1]))
    pl_s = [w["pl_w_grp"], w["pl_scale"]]
    v1, g1 = fwd_in("pl_in_fwd", x1, mods[1], 1, "pl_w_in", [e, e], n_all, dtypes=[F32, ACT])
    pooled = hosting("pool_fwd", _pool_fwd, v1, lat_len)
    x2 = fwd_out("pl_out_fwd", _pre_pool_out, [pooled, g1, x1], mods[1], pl_s, "pl_w_out", n_all)

    w.update(plan.weights(LAYER_WEIGHTS[2]))
    ml_widths = [KV_RANK, LANES, Q_RANK, HEADS * VDIM]
    ckv, krp, cq, g2 = fwd_in("ml_in_fwd", x2, mods[2], 2, "ml_w_in", ml_widths, n_all)
    k_s = [w["ml_kv_norm"], w["ml_nope_norm"][1:2], w["ml_rope_norm"][1:2]]
    q_s = [w["ml_q_norm"], w["ml_nope_norm"][0:1], w["ml_rope_norm"][0:1]]
    kk, vv = hosting("ml_keys_fwd", _stage_fwd, "ml_keys_fwd", pre=_pre_rms_only, post=_post_mla_keys, wsel=[0],
                     splits=None, tiles=[ckv, krp, cos, sin], eparams=[], sparams=k_s, weights=[w["ml_w_ukv"]],
                     out_widths=[HEADS * HEAD_W, HEADS * VDIM], out_dtypes=[BF16, BF16], batch=batch,
                     n_tiles=n_all, n_lat_tiles=n_lat)
    (qq,) = _stage_fwd("ml_queries_fwd", pre=_pre_rms_only, post=_post_mla_queries, wsel=[0], splits=None,
                       tiles=[cq, cos, sin], eparams=[], sparams=q_s, weights=[w["ml_w_uq"]],
                       out_widths=[HEADS * HEAD_W], out_dtypes=[BF16], batch=batch, n_tiles=n_big,
                       n_lat_tiles=n_big, tm=tml)
    att, lse = hosting("attn_fwd", _attn_fwd, qq, kk, vv)
    x3 = fwd_out("ml_out_fwd", _pre_mla_out, [att, g2, x2], mods[2], [], "ml_w_out", n_big, tm=tml)

    w.update(plan.weights(LAYER_WEIGHTS[3]))
    ch_s = [w["ch_ln_g"], w["ch_ln_b"], w["ch_w_s"], w["ch_b_s"]]
    u3, v3, g3 = fwd_in("ch_in_fwd", x3, mods[3], 3, "ch_w_in", [e, e, e], n_big, tm=tml)
    x4 = fwd_out("ch_out_fwd", _pre_chunk_out, [u3, v3, g3, x3], mods[3], ch_s, "ch_w_out", n_big, tm=tml)

    loss_part, dy = _loss_kernel(x4, target)

    dmods = [None] * 4
    dnorm = [None] * 4
    big = {}
    (du, dv, dg), dgt, (g["ch_ln_g"], g["ch_ln_b"], g["ch_w_s"], g["ch_b_s"]), big["ch_w_out"] = bwd_out(
        "ch_out_bwd", _pre_chunk_out, [u3, v3, g3, x3], mods[3], ch_s, "ch_w_out", n_big, dy, tm=tml)
    plan.note({n: g[n] for n in ("ch_ln_g", "ch_ln_b", "ch_w_s", "ch_b_s")})
    dx3, dsh, dsc, dnorm[3], big["ch_w_in"] = bwd_in("ch_in_bwd", x3, mods[3], 3, "ch_w_in", [e, e, e], n_lat,
                                                     [du, dv, dg], [False] * 3, dy, False, stack=N_CHIP)
    dmods[3] = (dsh, dsc, dgt)
    plan.layer_grads(3, big)

    big = {}
    (datt, dg), dgt, _, big["ml_w_out"] = bwd_out("ml_out_bwd", _pre_mla_out, [att, g2, x2], mods[2], [],
                                                  "ml_w_out", n_big, dx3, tm=tml)
    dq, dk, dvv = hosting("attn_bwd", _attn_bwd, qq, kk, vv, att, lse, datt)
    (dcq,), _, (g["ml_q_norm"], dnope0, drope0), (big["ml_w_uq"],) = hosting(
        "ml_queries_bwd", _stage_bwd, "ml_queries_bwd", pre=_pre_rms_only, post=_post_mla_queries, wsel=[0],
        splits=None, tiles=[cq, cos, sin], tile_diff=[True, False, False], eparams=[], sparams=q_s,
        weights=[w["ml_w_uq"]], cots=[dq], cot_lat_only=[False], batch=batch, n_tiles=n_big, n_lat_tiles=n_big,
        tm=tml)
    (dckv, dkrp), _, (g["ml_kv_norm"], dnope1, drope1), (big["ml_w_ukv"],) = hosting(
        "ml_keys_bwd", _stage_bwd, "ml_keys_bwd", pre=_pre_rms_only, post=_post_mla_keys, wsel=[0], splits=None,
        tiles=[ckv, krp, cos, sin], tile_diff=[True, True, False, False], eparams=[], sparams=k_s,
        weights=[w["ml_w_ukv"]], cots=[dk, dvv], cot_lat_only=[False, False], batch=batch, n_tiles=n_all,
        n_lat_tiles=n_lat, w_col_stack=[N_CHIP])
    g["ml_nope_norm"] = jnp.concatenate([dnope0, dnope1], axis=0)
    g["ml_rope_norm"] = jnp.concatenate([drope0, drope1], axis=0)
    dx2, dsh, dsc, dnorm[2], big["ml_w_in"] = bwd_in("ml_in_bwd", x2, mods[2], 2, "ml_w_in", ml_widths, n_all,
                                                     [dckv, dkrp, dcq, dg], [False, False, True, True], dx3, True)
    dmods[2] = (dsh, dsc, dgt)
    plan.layer_grads(2, big)

    big = {}
    (dpooled, dg), dgt, (big["pl_w_grp"], g["pl_scale"]), big["pl_w_out"] = bwd_out(
        "pl_out_bwd", _pre_pool_out, [pooled, g1, x1], mods[1], pl_s, "pl_w_out", n_all, dx2)
    dv1 = _pool_bwd(dpooled, lat_len)
    dx1, dsh, dsc, dnorm[1], big["pl_w_in"] = bwd_in("pl_in_bwd", x1, mods[1], 1, "pl_w_in", [e, e], n_all,
                                                     [dv1, dg], [False] * 2, dx2, False, stack=N_CHIP)
    dmods[1] = (dsh, dsc, dgt)
    plan.layer_grads(1, big)

    big = {}
    (dc1, dg), dgt, (g["cv_ln_g"], g["cv_ln_b"]), big["cv_w_out"] = bwd_out(
        "cv_out_bwd", _pre_conv_out, [c1, g0, xm], mods[0], cv_s, "cv_w_out", n_all, dx1)
    plan.layer_grads(0, big)
    big = {}
    da, db, g["cv_dw"], g["cv_db"] = hosting("conv_bwd", _conv_bwd, a0, b0, w["cv_dw"], dc1, lat_len)
    dx0, dsh, dsc, dnorm[0], big["cv_w_in"] = bwd_in("cv_in_bwd", xm, mods[0], 0, "cv_w_in", [e, e, e], n_all,
                                                     [da, db, dg], [False] * 3, dx1, False, stack=N_CHIP,
                                                     dx_lat_only=True)
    dmods[0] = (dsh, dsc, dgt)
    plan.layer_grads(0, big)
    g["norm_g"] = jnp.concatenate(dnorm, axis=0)
    return loss_part, dx0, dmods, g


N_DEV = 8
N_CHIP = 4
ANY = pl.BlockSpec(memory_space=pl.ANY)


def _my_place():
    return lax.axis_index("x"), lax.axis_index("y"), lax.axis_index("c")


def _flip(v, f):
    return 1 - v if f else v


def _ag8_copies(x):
    def plan(ins, outs, sems):
        mx, my, mc = _my_place()
        me = 4 * mx + 2 * my + mc
        sends, recvs = [], []
        for rel in range(1, N_DEV):
            peer = (_flip(mx, rel & 4), _flip(my, rel & 2), _flip(mc, rel & 1))
            src_dev = 4 * peer[0] + 2 * peer[1] + peer[2]
            sends.append(_remote(ins[0], outs[0].at[me], sems, rel - 1, peer))
            recvs.append(_remote(ins[0], outs[0].at[src_dev], sems, rel - 1, peer))
        return sends, recvs, [pltpu.make_async_copy(ins[0], outs[0].at[me], sems[2].at[0])]

    return _copies_hosted([x], [jax.ShapeDtypeStruct((N_DEV,) + x.shape, x.dtype)], (N_DEV - 1, N_DEV - 1, 1), plan)


def _ag8(name, x):
    return _run_hosted(name, _ag8_copies(x))[0]


def _ag8_column_copies(x, width):
    def plan(ins, outs, sems):
        mx, my, mc = _my_place()
        me = 4 * mx + 2 * my + mc
        sends, recvs = [], []
        for rel in range(1, N_DEV):
            peer = (_flip(mx, rel & 4), _flip(my, rel & 2), _flip(mc, rel & 1))
            src_dev = 4 * peer[0] + 2 * peer[1] + peer[2]
            cols = pl.ds(pl.multiple_of((2 * peer[0] + peer[1]) * width, LANES), width)
            sends.append(_remote(ins[0].at[:, cols], outs[0].at[me], sems, rel - 1, peer))
            recvs.append(_remote(ins[0].at[:, cols], outs[0].at[src_dev], sems, rel - 1, peer))
        mine = pl.ds(pl.multiple_of((2 * mx + my) * width, LANES), width)
        return sends, recvs, [pltpu.make_async_copy(ins[0].at[:, mine], outs[0].at[me], sems[2].at[0])]

    return _copies_hosted([x], [jax.ShapeDtypeStruct((N_DEV, x.shape[0], width), x.dtype)],
                          (N_DEV - 1, N_DEV - 1, 1), plan)


def _chip_rows_copies(x, rows_per_dev, shared_row):
    n_out = rows_per_dev + 1

    def plan(ins, outs, sems):
        mx, my, mc = _my_place()
        chip = 2 * mx + my
        sends, recvs = [], []

        def pieces(dev):
            return [(ins[0].at[pl.ds(dev * rows_per_dev, rows_per_dev)], slice(0, rows_per_dev)),
                    (ins[0].at[pl.ds(shared_row, 1)], slice(rows_per_dev, n_out))]

        for k, peer, pchip in _chip_peers(mx, my, mc):
            for t, (src, where) in enumerate(pieces(2 * pchip + mc)):
                sends.append(_remote(src, outs[0].at[chip, where], sems, 2 * k + t, peer))
                recvs.append(_remote(src, outs[0].at[pchip, where], sems, 2 * k + t, peer))
        locals_ = [pltpu.make_async_copy(src, outs[0].at[chip, where], sems[2].at[t])
                   for t, (src, where) in enumerate(pieces(2 * chip + mc))]
        return sends, recvs, locals_

    return _copies_hosted([x], [jax.ShapeDtypeStruct((N_CHIP, n_out) + x.shape[1:], x.dtype)], (6, 6, 2), plan)


def _chip_peers(mx, my, mc):
    out = []
    for rel in range(1, N_CHIP):
        px, py = _flip(mx, rel & 2), _flip(my, rel & 1)
        out.append((rel - 1, (px, py, mc), 2 * px + py))
    return out


def _half(mc, rows):
    return pl.ds(pl.multiple_of(mc * (rows // 2), 8), rows // 2)


def _copies_hosted(arrays, out_shapes, n_sems, plan, aliases=None):
    def start(ins, outs, sems):
        sends, _, locals_ = plan(ins, outs, sems)
        for cp in locals_ + sends:
            cp.start()

    def wait(ins, outs, sems):
        sends, recvs, locals_ = plan(ins, outs, sems)
        for cp in recvs:
            cp.wait_recv()
        for cp in sends:
            cp.wait_send()
        for cp in locals_:
            cp.wait()

    return _Hosted(arrays, out_shapes, [pltpu.SemaphoreType.DMA((k,)) for k in n_sems], start, wait, aliases)


def _remote(src, dst, sems, k, peer):
    return pltpu.make_async_remote_copy(src_ref=src, dst_ref=dst, send_sem=sems[0].at[k], recv_sem=sems[1].at[k],
                                        device_id=peer, device_id_type=MESH)


def _gather_ici(shards):
    n = len(shards)

    def plan(ins, outs, sems):
        mx, my, mc = _my_place()
        chip = 2 * mx + my
        sends, recvs, locals_ = [], [], []
        for a in range(n):
            rows = ins[a].shape[0]
            locals_.append(pltpu.make_async_copy(ins[a], outs[a].at[chip], sems[2].at[a]))
            for k, peer, pchip in _chip_peers(mx, my, mc):
                src = ins[a].at[_half(mc, rows)]
                sends.append(_remote(src, outs[a].at[chip, _half(mc, rows)], sems, 3 * a + k, peer))
                recvs.append(_remote(src, outs[a].at[pchip, _half(mc, rows)], sems, 3 * a + k, peer))
        return sends, recvs, locals_

    return _copies_hosted(shards, [jax.ShapeDtypeStruct((N_CHIP,) + s.shape, s.dtype) for s in shards],
                          (3 * n, 3 * n, n), plan)


def _sibling_fill(arrays, row_axis, chips_only_other):
    n = len(arrays)
    per = 3 if chips_only_other else 1

    def plan(ins, outs, sems):
        mx, my, mc = _my_place()
        sibling = (mx, my, 1 - mc)

        def views(a, core):
            rows = outs[a].shape[row_axis]
            if chips_only_other:
                return [outs[a].at[pchip, _half(core, rows)] for _, _, pchip in _chip_peers(mx, my, mc)]
            return [outs[a].at[_half(core, rows)]]

        sends, recvs = [], []
        for a in range(n):
            for k, v in enumerate(views(a, mc)):
                sends.append(_remote(v, v, sems, per * a + k, sibling))
            for k, v in enumerate(views(a, 1 - mc)):
                recvs.append(_remote(v, v, sems, per * a + k, sibling))
        return sends, recvs, []

    return _copies_hosted(arrays, [jax.ShapeDtypeStruct(s.shape, s.dtype) for s in arrays], (per * n, per * n), plan,
                          aliases={a: a for a in range(n)})


def _grad_swap_d2d(stacks):
    n = len(stacks)

    def plan(ins, outs, sems):
        mx, my, mc = _my_place()
        sibling = (mx, my, 1 - mc)
        sends = [_remote(ins[a].at[:, _half(1 - mc, ins[a].shape[1])], outs[a], sems, a, sibling) for a in range(n)]
        return sends, sends, []

    return _copies_hosted(stacks, [jax.ShapeDtypeStruct((N_CHIP, s.shape[1] // 2, s.shape[2]), s.dtype)
                                   for s in stacks], (n, n), plan)


def _grad_exchange_ici(parts):
    n = len(parts)

    def plan(ins, outs, sems):
        mx, my, mc = _my_place()
        chip = 2 * mx + my
        sends, recvs, locals_ = [], [], []
        for a in range(n):
            locals_.append(pltpu.make_async_copy(ins[a].at[chip], outs[a].at[chip], sems[2].at[a]))
            for k, peer, pchip in _chip_peers(mx, my, mc):
                sends.append(_remote(ins[a].at[pchip], outs[a].at[chip], sems, 3 * a + k, peer))
                recvs.append(_remote(ins[a].at[pchip], outs[a].at[pchip], sems, 3 * a + k, peer))
        return sends, recvs, locals_

    return _copies_hosted(parts, [jax.ShapeDtypeStruct(s.shape, s.dtype) for s in parts], (3 * n, 3 * n, n), plan)


def _row_block(rows, limit=256):
    for t in range(min(rows, limit), 7, -8):
        if rows % t == 0 and t % 8 == 0:
            return t
    return rows


def _grad_add_half(core, stack, received):
    _, rows, cw = stack.shape
    rh = rows // 2
    tr = _row_block(rh)

    def body(s_ref, a_ref, b_ref, o_ref):
        o_ref[...] = (a_ref[...] + b_ref[...]).astype(o_ref.dtype)

    grid_spec = pltpu.PrefetchScalarGridSpec(
        num_scalar_prefetch=1, grid=(rh // tr,),
        in_specs=[pl.BlockSpec((N_CHIP, tr, cw), lambda i, s: (0, s[0] * (rh // tr) + i, 0)),
                  pl.BlockSpec((N_CHIP, tr, cw), lambda i, s: (0, i, 0))],
        out_specs=pl.BlockSpec((N_CHIP, tr, cw), lambda i, s: (0, i, 0)))
    return pl.pallas_call(
        body, name="grad_add_half", grid_spec=grid_spec, out_shape=jax.ShapeDtypeStruct(received.shape, BF16),
        compiler_params=pltpu.CompilerParams(dimension_semantics=("arbitrary",), vmem_limit_bytes=VMEM_LIMIT),
    )(core, stack, received)


def _adamw(name, row_off, parts, w, m, v, rows, hosted=None):
    n, _, cw = parts.shape
    tr = _row_block(rows, 128)

    def update(p_ref, w_ref, m_ref, v_ref, g_ref, d_ref, nm_ref, nv_ref):
        g = p_ref[0].astype(F32)
        for k in range(1, n):
            g = g + p_ref[k].astype(F32)
        nm = ADAM_B1 * m_ref[...] + (1.0 - ADAM_B1) * g
        nv = ADAM_B2 * v_ref[...] + (1.0 - ADAM_B2) * (g * g)
        m_hat = nm / (1.0 - ADAM_B1 ** ADAM_STEP)
        v_hat = nv / (1.0 - ADAM_B2 ** ADAM_STEP)
        g_ref[...] = g
        d_ref[...] = -ADAM_LR * (m_hat / (jnp.sqrt(v_hat) + ADAM_EPS) + ADAM_WD * w_ref[...])
        nm_ref[...] = nm
        nv_ref[...] = nv

    out_shape = [jax.ShapeDtypeStruct(w.shape, F32)] * 4
    if row_off is None:
        blk = pl.BlockSpec((tr, cw), lambda i: (i, 0))
        return _pcall(update, name=name, grid=(rows // tr,), out_specs=[blk] * 4, out_shape=out_shape,
                      in_specs=[pl.BlockSpec((n, tr, cw), lambda i: (0, i, 0)), blk, blk, blk],
                      args=[parts, w, m, v], hosted=hosted)

    def body(s_ref, *refs):
        update(*refs)

    full = pl.BlockSpec((tr, cw), lambda i, s: (s[0] // tr + i, 0))
    grid_spec = pltpu.PrefetchScalarGridSpec(
        num_scalar_prefetch=1, grid=(rows // tr,),
        in_specs=[pl.BlockSpec((n, tr, cw), lambda i, s: (0, i, 0)), full, full, full],
        out_specs=[full, full, full, full])
    return pl.pallas_call(
        body, name=name, grid_spec=grid_spec, out_shape=out_shape,
        compiler_params=pltpu.CompilerParams(dimension_semantics=("arbitrary",), vmem_limit_bytes=VMEM_LIMIT),
    )(row_off, parts, w, m, v)


def _sum8(x):
    _, r, cw = x.shape
    tr = _row_block(r, 64)

    def body(x_ref, o_ref):
        acc = x_ref[0]
        for k in range(1, N_DEV):
            acc = acc + x_ref[k]
        o_ref[...] = acc

    return pl.pallas_call(
        body, name="sum8", grid=(r // tr,), in_specs=[pl.BlockSpec((N_DEV, tr, cw), lambda i: (0, i, 0))],
        out_specs=pl.BlockSpec((tr, cw), lambda i: (i, 0)), out_shape=jax.ShapeDtypeStruct((r, cw), F32),
        compiler_params=pltpu.CompilerParams(dimension_semantics=("arbitrary",)),
    )(x)


MOD_ROWS = 24
CTX_ROW = 16


def _mod_fwd(c_rows, w_mod, b_mod, hosted=None):
    nl, d, nn = w_mod.shape

    def body(c_ref, w_ref, b_ref, o_ref):
        o_ref[0] = _dot(_silu(c_ref[...]), w_ref[0]) + b_ref[0]

    return _pcall(
        body, name="mod_fwd", grid=(nl,),
        in_specs=[pl.BlockSpec((MOD_ROWS, d), lambda i: (0, 0)), pl.BlockSpec((1, d, nn), lambda i: (i, 0, 0)),
                  pl.BlockSpec((1, 1, nn), lambda i: (i, 0, 0))],
        out_specs=[pl.BlockSpec((1, MOD_ROWS, nn), lambda i: (i, 0, 0))],
        out_shape=[jax.ShapeDtypeStruct((nl, MOD_ROWS, nn), F32)], args=[c_rows, w_mod, b_mod], hosted=hosted)[0]


def _mod_bwd_rows(dlat, dctx_parts):
    nl, ne, nn = dlat.shape

    def body(l_ref, c_ref, db_ref, dc_ref):
        dc = c_ref[0, 0:1, :]
        for k in range(1, N_DEV):
            dc = dc + c_ref[0, k:k + 1, :]
        db = dc
        for k in range(ne):
            db = db + l_ref[0, k:k + 1, :]
        db_ref[0] = db
        dc_ref[0] = dc

    return pl.pallas_call(
        body, name="mod_bwd_rows", grid=(nl,),
        in_specs=[pl.BlockSpec((1, ne, nn), lambda i: (i, 0, 0)), pl.BlockSpec((1, N_DEV, nn), lambda i: (i, 0, 0))],
        out_specs=[pl.BlockSpec((1, 1, nn), lambda i: (i, 0, 0))] * 2,
        out_shape=[jax.ShapeDtypeStruct((nl, 1, nn), F32)] * 2,
        compiler_params=pltpu.CompilerParams(dimension_semantics=("arbitrary",)),
    )(dlat, dctx_parts)


def _mod_bwd_w(c_cols, d_rows, w_mod, hosted=None):
    nl, d, nn = w_mod.shape

    def body(c_ref, d_ref, w_ref, dw_ref, dc_ref):
        i = pl.program_id(0)
        c = c_ref[...]
        sg = jax.nn.sigmoid(c)
        s = c * sg
        dv = d_ref[0]
        acc = s[:, 0:1] * dv[0:1, :]
        for r in range(1, CTX_ROW + 1):
            acc = acc + s[:, r:r + 1] * dv[r:r + 1, :]
        dw_ref[0] = acc
        ds_ctx = jnp.sum(w_ref[0] * dv[CTX_ROW:CTX_ROW + 1, :], axis=1, keepdims=True)
        cc, sc = c[:, CTX_ROW:CTX_ROW + 1], sg[:, CTX_ROW:CTX_ROW + 1]
        part = ds_ctx * (sc * (1.0 + cc * (1.0 - sc)))

        @pl.when(i == 0)
        def _():
            dc_ref[...] = part

        @pl.when(i != 0)
        def _():
            dc_ref[...] += part

    return _pcall(
        body, name="mod_bwd_w", grid=(nl,),
        in_specs=[pl.BlockSpec((d, MOD_ROWS), lambda i: (0, 0)), pl.BlockSpec((1, MOD_ROWS, nn), lambda i: (i, 0, 0)),
                  pl.BlockSpec((1, d, nn), lambda i: (i, 0, 0))],
        out_specs=[pl.BlockSpec((1, d, nn), lambda i: (i, 0, 0)), pl.BlockSpec((d, 1), lambda i: (0, 0))],
        out_shape=[jax.ShapeDtypeStruct((nl, d, nn), F32), jax.ShapeDtypeStruct((d, 1), F32)],
        args=[c_cols, d_rows, w_mod], hosted=hosted)


def _pack_rows(arrays, width, row_multiple=8):
    rows, spans, r0 = [], [], 0
    for a in arrays:
        flat = a.reshape(-1)
        nr = -(-flat.shape[0] // width)
        held = -(-nr // 8) * 8
        flat = jnp.pad(flat, (0, held * width - flat.shape[0]))
        rows.append(flat.reshape(held, width))
        spans.append((r0, nr, a.shape))
        r0 += held
    if r0 % row_multiple:
        rows.append(jnp.zeros((row_multiple - r0 % row_multiple, width), F32))
    return jnp.concatenate(rows, axis=0), spans


def _unpack_rows(packed, spans):
    out = []
    for r0, nr, shape in spans:
        out.append(packed[r0:r0 + nr].reshape(-1)[:math.prod(shape)].reshape(shape))
    return out


BIG = {"cv_w_in": 1, "cv_w_out": 0, "pl_w_in": 1, "pl_w_grp": None, "pl_w_out": 0, "ml_w_in": 1, "ml_w_uq": 1,
       "ml_w_ukv": 1, "ml_w_out": 0, "ch_w_in": 1, "ch_w_out": 0}
SMALL_SHARDED = ["cv_dw", "pl_scale", "ml_q_norm", "ml_kv_norm", "ch_ln_g", "ch_ln_b"]
SMALL_REPLICATED = ["c_ctx", "norm_g", "b_mod", "cv_db", "cv_ln_g", "cv_ln_b", "ml_nope_norm", "ml_rope_norm",
                    "ch_w_s", "ch_b_s"]
WEIGHTS = ['c_ctx', 'norm_g', 'w_mod', 'b_mod', 'cv_w_in', 'cv_dw', 'cv_db', 'cv_ln_g', 'cv_ln_b', 'cv_w_out',
           'pl_w_in', 'pl_w_grp', 'pl_scale', 'pl_w_out', 'ml_w_in', 'ml_q_norm', 'ml_kv_norm', 'ml_w_uq', 'ml_w_ukv',
           'ml_nope_norm', 'ml_rope_norm', 'ml_w_out', 'ch_w_in', 'ch_ln_g', 'ch_ln_b', 'ch_w_s', 'ch_b_s', 'ch_w_out']


def _shard2d(name, a):
    if name == "pl_w_grp":
        return a.reshape(a.shape[-3] * a.shape[-2], a.shape[-1])
    return a.reshape(a.shape[-2], a.shape[-1])


def _unstack(name, s):
    if name == "pl_w_grp":
        ng = len(POOL_WINDOWS)
        return s.reshape(N_CHIP, ng, s.shape[1] // ng, s.shape[2]).transpose(1, 0, 2, 3).reshape(ng, -1, s.shape[2])
    if BIG[name] == 0:
        return s.reshape(-1, s.shape[2])
    return s.transpose(1, 0, 2).reshape(s.shape[1], -1)


def _stack(name, g):
    if g.ndim == 3 and name != "pl_w_grp":
        return g
    if name == "pl_w_grp":
        ng = len(POOL_WINDOWS)
        return g.reshape(ng, N_CHIP, -1, g.shape[2]).transpose(1, 0, 2, 3).reshape(N_CHIP, -1, g.shape[2])
    if BIG[name] == 0:
        return g.reshape(N_CHIP, -1, g.shape[1])
    return g.reshape(g.shape[0], N_CHIP, -1).transpose(1, 0, 2)


L0, L1, L2, L3 = LAYER_WEIGHTS
EARLY_SMALL = ("ch_w_s", "ch_b_s", "ch_ln_g", "ch_ln_b")
MESH_SCHEDULE = {
    "mod_fwd": [("gather", L0[:1])], "mod_exchange": [("gfill", L0[:1])],
    "cv_in_fwd": [("gather", L0[1:]), ("gather", L1[1:])], "conv_fwd": [("gfill", L0[1:]), ("gather", L1[:1])],
    "cv_out_fwd": [("gfill", L1), ("gather", L2[3:])],
    "pl_in_fwd": [("gather", L2[:1])], "pool_fwd": [("gather", L2[1:3])], "pl_out_fwd": [("gfill", L2)],
    "attn_fwd": [("gather", L3)], "ml_out_fwd": [("gfill", L3)],
    "ch_in_bwd": [("small", EARLY_SMALL)],
    "ml_out_bwd": [("swap", L3)], "attn_bwd": [("exch", L3)], "ml_queries_bwd": [("ofill", L3)],
    "pl_out_bwd": [("swap", L2)], "pl_in_bwd": [("exch", L2)],
    "cv_out_bwd": [("swap", L1), ("ofill", L2)], "conv_bwd": [("exch", L1), ("swap", L0[1:])],
    "ag8_dmod": [("swap", L0[:1]), ("exch", L0[1:])],
    "mod_bwd_w": [("exch", L0[:1]), ("ofill", L1), ("ofill", L0[1:])], "ag8_small_grads": [("ofill", L0[:1])],
}
GRAD_GROUPS = (L3, L2, L1, L0[1:], L0[:1])


class _MeshPlan:
    def __init__(self, weights, m, v, core):
        self.W, self.M, self.V, self.core = weights, m, v, core
        self.small = None
        self.stack, self.gstack, self.part, self.half, self.out = {}, {}, {}, {}, {}
        self.notes, self.early = {}, {}
        self.live, self.done = {}, set()

    def _make(self, op, names):
        if op == "gather":
            return _gather_ici([_shard2d(n, self.W[n]).astype(BF16) for n in names])
        if op == "gfill":
            return _sibling_fill([self.stack[n] for n in names], 1, True)
        if op == "swap":
            return _grad_swap_d2d([self.gstack[n] for n in names])
        if op == "exch":
            return _grad_exchange_ici([self.part[n] for n in names])
        if op == "ofill":
            return _sibling_fill([t for n in names for t in self.half[n]], 0, False)
        pack, self.early_spans = _pack_rows([self.notes[n] for n in names], LANES, 128)
        return _ag8_copies(pack)

    def _finish_op(self, op, names, hosted):
        self.done.add((op, names))
        res = hosted.results
        if op in ("gather", "gfill"):
            self.stack.update(zip(names, res))
        elif op == "swap":
            for n, r in zip(names, res):
                self.part[n] = _grad_add_half(self.core, self.gstack[n], r)
        elif op == "exch":
            for n, q in zip(names, res):
                rh = q.shape[1]
                self.half[n] = _adamw("adamw_" + n, self.core * rh, q, _shard2d(n, self.W[n]),
                                      _shard2d(n, self.M[n]), _shard2d(n, self.V[n]), rh)
        elif op == "ofill":
            for k, n in enumerate(names):
                self.out[n] = tuple(r.reshape(self.W[n].shape) for r in res[4 * k:4 * k + 4])
        else:
            self.early.update(zip(names, _unpack_rows(_sum8(res[0]), self.early_spans)))

    def alone(self, op, names):
        hosted = self._make(op, names)
        _run_hosted("%s_%s" % (op, names[0]), hosted)
        self._finish_op(op, names, hosted)

    def weights(self, names):
        wk = {n: _unstack(n, self.stack[n]) for n in names}
        if "pl_w_grp" in wk:
            wk["pl_w_grp"] = wk["pl_w_grp"].astype(F32)
        return _prep_weights(wk)

    def hosted(self, tag):
        self.live[tag] = [(op, names, self._make(op, names)) for op, names in MESH_SCHEDULE.get(tag, [])]
        return _merge_hosted([h for _, _, h in self.live[tag]])

    def after(self, tag):
        for op, names, hosted in self.live.pop(tag, []):
            self._finish_op(op, names, hosted)

    def note(self, values):
        self.notes.update(values)

    def layer_grads(self, layer, grads):
        g = _unprep_grads(grads)
        for n in g:
            self.gstack[n] = _stack(n, g[n])

    def finish(self):
        for names in GRAD_GROUPS:
            for op in ("swap", "exch", "ofill"):
                if (op, names) not in self.done:
                    self.alone(op, names)
        return self.out


def kernel(x, c, ctx, c_ctx, norm_g, w_mod, b_mod, cv_w_in, cv_dw, cv_db, cv_ln_g, cv_ln_b, cv_w_out, pl_w_in, pl_w_grp, pl_scale, pl_w_out, ml_w_in, ml_q_norm, ml_kv_norm, ml_w_uq, ml_w_ukv, ml_nope_norm, ml_rope_norm, ml_w_out, ch_w_in, ch_ln_g, ch_ln_b, ch_w_s, ch_b_s, ch_w_out, loss_target, m_c_ctx, m_norm_g, m_w_mod, m_b_mod, m_cv_w_in, m_cv_dw, m_cv_db, m_cv_ln_g, m_cv_ln_b, m_cv_w_out, m_pl_w_in, m_pl_w_grp, m_pl_scale, m_pl_w_out, m_ml_w_in, m_ml_q_norm, m_ml_kv_norm, m_ml_w_uq, m_ml_w_ukv, m_ml_nope_norm, m_ml_rope_norm, m_ml_w_out, m_ch_w_in, m_ch_ln_g, m_ch_ln_b, m_ch_w_s, m_ch_b_s, m_ch_w_out, v_c_ctx, v_norm_g, v_w_mod, v_b_mod, v_cv_w_in, v_cv_dw, v_cv_db, v_cv_ln_g, v_cv_ln_b, v_cv_w_out, v_pl_w_in, v_pl_w_grp, v_pl_scale, v_pl_w_out, v_ml_w_in, v_ml_q_norm, v_ml_kv_norm, v_ml_w_uq, v_ml_w_ukv, v_ml_nope_norm, v_ml_rope_norm, v_ml_w_out, v_ch_w_in, v_ch_ln_g, v_ch_ln_b, v_ch_w_s, v_ch_b_s, v_ch_w_out):
    W = dict(c_ctx=c_ctx, norm_g=norm_g, w_mod=w_mod, b_mod=b_mod, cv_w_in=cv_w_in, cv_dw=cv_dw, cv_db=cv_db, cv_ln_g=cv_ln_g, cv_ln_b=cv_ln_b, cv_w_out=cv_w_out, pl_w_in=pl_w_in, pl_w_grp=pl_w_grp, pl_scale=pl_scale, pl_w_out=pl_w_out, ml_w_in=ml_w_in, ml_q_norm=ml_q_norm, ml_kv_norm=ml_kv_norm, ml_w_uq=ml_w_uq, ml_w_ukv=ml_w_ukv, ml_nope_norm=ml_nope_norm, ml_rope_norm=ml_rope_norm, ml_w_out=ml_w_out, ch_w_in=ch_w_in, ch_ln_g=ch_ln_g, ch_ln_b=ch_ln_b, ch_w_s=ch_w_s, ch_b_s=ch_b_s, ch_w_out=ch_w_out)
    M = dict(c_ctx=m_c_ctx, norm_g=m_norm_g, w_mod=m_w_mod, b_mod=m_b_mod, cv_w_in=m_cv_w_in, cv_dw=m_cv_dw, cv_db=m_cv_db, cv_ln_g=m_cv_ln_g, cv_ln_b=m_cv_ln_b, cv_w_out=m_cv_w_out, pl_w_in=m_pl_w_in, pl_w_grp=m_pl_w_grp, pl_scale=m_pl_scale, pl_w_out=m_pl_w_out, ml_w_in=m_ml_w_in, ml_q_norm=m_ml_q_norm, ml_kv_norm=m_ml_kv_norm, ml_w_uq=m_ml_w_uq, ml_w_ukv=m_ml_w_ukv, ml_nope_norm=m_ml_nope_norm, ml_rope_norm=m_ml_rope_norm, ml_w_out=m_ml_w_out, ch_w_in=m_ch_w_in, ch_ln_g=m_ch_ln_g, ch_ln_b=m_ch_ln_b, ch_w_s=m_ch_w_s, ch_b_s=m_ch_b_s, ch_w_out=m_ch_w_out)
    V = dict(c_ctx=v_c_ctx, norm_g=v_norm_g, w_mod=v_w_mod, b_mod=v_b_mod, cv_w_in=v_cv_w_in, cv_dw=v_cv_dw, cv_db=v_cv_db, cv_ln_g=v_cv_ln_g, cv_ln_b=v_cv_ln_b, cv_w_out=v_cv_w_out, pl_w_in=v_pl_w_in, pl_w_grp=v_pl_w_grp, pl_scale=v_pl_scale, pl_w_out=v_pl_w_out, ml_w_in=v_ml_w_in, ml_q_norm=v_ml_q_norm, ml_kv_norm=v_ml_kv_norm, ml_w_uq=v_ml_w_uq, ml_w_ukv=v_ml_w_ukv, ml_nope_norm=v_ml_nope_norm, ml_rope_norm=v_ml_rope_norm, ml_w_out=v_ml_w_out, ch_w_in=v_ch_w_in, ch_ln_g=v_ch_ln_g, ch_ln_b=v_ch_ln_b, ch_w_s=v_ch_w_s, ch_b_s=v_ch_b_s, ch_w_out=v_ch_w_out)

    batch, lat_len, d = x.shape
    mx, my, mc = _my_place()
    chip = 2 * mx + my
    dev = 2 * chip + mc
    core = jnp.reshape(mc, (1,)).astype(jnp.int32)
    zero_off = jnp.zeros((1,), jnp.int32)
    big_names = list(BIG)

    sw = d // N_CHIP
    small_in = [c] + [jnp.pad(W[n].reshape(-1, W[n].shape[-1]), ((0, 0), (0, sw - W[n].shape[-1])))
                      for n in SMALL_SHARDED]
    pack1, spans1 = _pack_rows(small_in, sw)
    plan = _MeshPlan(W, M, V, core)
    gather1 = _ag8_copies(pack1)
    _run_hosted("ag8_inputs", _merge_hosted([gather1, plan.hosted("ag8_inputs")]))
    plan.after("ag8_inputs")
    got1 = gather1.results[0]
    c_all = got1[:, spans1[0][0]:spans1[0][0] + spans1[0][1]].reshape(N_DEV * batch, d)
    full_small = {}
    for n, (r0, nr, _) in zip(SMALL_SHARDED, spans1[1:]):
        blk = got1[0::2, r0:r0 + nr, :W[n].shape[-1]]
        full_small[n] = blk.transpose(1, 0, 2).reshape(nr, -1)

    c_rows = jnp.concatenate([c_all, c_ctx[None], jnp.zeros((MOD_ROWS - CTX_ROW - 1, d), F32)], axis=0)
    nmod = w_mod.shape[2]
    b_shard = lax.dynamic_slice(b_mod, (0, chip * nmod), (b_mod.shape[0], nmod))[:, None, :]
    mod_shard = _mod_fwd(c_rows, w_mod, b_shard, hosted=plan.hosted("mod_fwd"))
    plan.after("mod_fwd")
    mod_rows = mod_shard.transpose(1, 0, 2).reshape(MOD_ROWS, 1, 4 * nmod)
    rows2 = _chip_rows_copies(mod_rows, batch, CTX_ROW)
    _run_hosted("mod_exchange", _merge_hosted([rows2, plan.hosted("mod_exchange")]))
    plan.after("mod_exchange")
    got2 = rows2.results[0]
    mod_mine = got2.reshape(N_CHIP, batch + 1, 4, nmod).transpose(2, 1, 0, 3).reshape(4, batch + 1, 3 * d)
    mod_lat, mod_ctx = mod_mine[:, :batch], mod_mine[:, batch]
    mods = []
    for i in range(4):
        mods.append(tuple(
            jnp.stack([mod_lat[i, :, j * d:(j + 1) * d], jnp.broadcast_to(mod_ctx[i, j * d:(j + 1) * d], (batch, d))],
                      axis=1)[:, :, None, :] for j in range(3)))

    wk = dict(full_small)
    wk.update(norm_g=norm_g, cv_db=cv_db, cv_ln_g=cv_ln_g, cv_ln_b=cv_ln_b, ml_nope_norm=ml_nope_norm[0],
              ml_rope_norm=ml_rope_norm[0], ch_w_s=ch_w_s[0], ch_b_s=ch_b_s[0])
    plan.small = _prep_weights(wk)
    xm = jnp.concatenate([x, ctx], axis=1)
    loss_part, grad_x, dmods, g = _local_step(xm, loss_target, mods, plan, lat_len)
    g = _unprep_grads(g)

    lat_rows, ctx_rows = [], []
    for i in range(4):
        dsh, dsc, dgt = dmods[i]
        lat_rows.append(jnp.concatenate([dsh[:, 0, 0], dsc[:, 0, 0], dgt[:, 0, 0]], axis=1))
        zero = jnp.zeros((d,), F32)
        cs = [jnp.sum(t[:, 1, 0], axis=0) if ok else zero
              for t, ok in zip((dsh, dsc, dgt), (i <= 2, i <= 2, i <= 1))]
        ctx_rows.append(jnp.concatenate(cs, axis=0)[None])
    dmod_dev = jnp.concatenate(lat_rows + ctx_rows, axis=0)
    dmod_dev = jnp.pad(dmod_dev, ((0, (-dmod_dev.shape[0]) % 8), (0, 0)))
    gather3 = _ag8_column_copies(dmod_dev, nmod)
    _run_hosted("ag8_dmod", _merge_hosted([gather3, plan.hosted("ag8_dmod")]))
    plan.after("ag8_dmod")
    got3 = gather3.results[0]
    dlat = got3[:, :4 * batch].reshape(N_DEV, 4, batch, nmod).transpose(1, 0, 2, 3).reshape(4, N_DEV * batch, nmod)
    dctx_parts = got3[:, 4 * batch:4 * batch + 4].transpose(1, 0, 2)
    g_b_shard, dctx = _mod_bwd_rows(dlat, dctx_parts)
    d_rows = jnp.concatenate([dlat, dctx, jnp.zeros((4, MOD_ROWS - CTX_ROW - 1, nmod), F32)], axis=1)
    g_w_mod, dcc_part = _mod_bwd_w(c_rows.T, d_rows, w_mod, hosted=plan.hosted("mod_bwd_w"))
    plan.after("mod_bwd_w")

    wm2 = w_mod.reshape(-1, nmod)
    res_mod = _adamw("adamw_w_mod", None, g_w_mod.reshape(1, -1, nmod), wm2, M["w_mod"].reshape(-1, nmod),
                     V["w_mod"].reshape(-1, nmod), wm2.shape[0], hosted=plan.hosted("adamw_w_mod"))
    plan.after("adamw_w_mod")
    out = {"w_mod": tuple(r.reshape(w_mod.shape) for r in res_mod)}

    g_small_in = {n: g[n] for n in SMALL_SHARDED if n not in EARLY_SMALL}
    g_small_in.update(norm_g=g["norm_g"], cv_db=g["cv_db"], cv_ln_g=g["cv_ln_g"], cv_ln_b=g["cv_ln_b"],
                      ml_nope_norm=g["ml_nope_norm"], ml_rope_norm=g["ml_rope_norm"],
                      c_ctx=dcc_part.reshape(-1) * (mc == 0).astype(F32), loss=loss_part,
                      b_mod=lax.dynamic_update_slice(jnp.zeros((N_CHIP, 4, nmod), F32),
                                                     g_b_shard[None, :, 0] * (mc == 0).astype(F32), (chip, 0, 0)))
    small_names = list(g_small_in)
    pack4, spans4 = _pack_rows([g_small_in[n] for n in small_names], LANES, 128)
    gather4 = _ag8_copies(pack4)
    _run_hosted("ag8_small_grads", _merge_hosted([gather4, plan.hosted("ag8_small_grads")]))
    plan.after("ag8_small_grads")
    gs = dict(zip(small_names, _unpack_rows(_sum8(gather4.results[0]), spans4)))
    loss = gs["loss"][0, 0]
    gs.update(plan.early)
    gs["b_mod"] = gs["b_mod"].transpose(1, 0, 2).reshape(4, N_CHIP * nmod)
    for n in SMALL_SHARDED:
        wd = W[n].shape[-1]
        gs[n] = lax.dynamic_slice_in_dim(gs[n], chip * wd, wd, axis=1)
    upd_names = SMALL_REPLICATED + SMALL_SHARDED
    pw, spans_u = _pack_rows([W[n] for n in upd_names], LANES, 128)
    pm, _ = _pack_rows([M[n] for n in upd_names], LANES, 128)
    pv, _ = _pack_rows([V[n] for n in upd_names], LANES, 128)
    pg, _ = _pack_rows([gs[n].reshape(W[n].shape) for n in upd_names], LANES, 128)
    res_small = _adamw("adamw_small", None, pg[None], pw, pm, pv, pw.shape[0], hosted=plan.hosted("adamw_small"))
    plan.after("adamw_small")
    for n, vals in zip(upd_names, zip(*[_unpack_rows(r, spans_u) for r in res_small])):
        out[n] = vals
    out.update(plan.finish())

    outs = [loss, grad_x]
    for j in range(4):
        outs.extend(out[n][j] for n in WEIGHTS)
    return tuple(outs)
```

```python
import functools
import math

import jax
import jax.numpy as jnp
from jax import lax
from jax.experimental import pallas as pl
from jax.experimental.pallas import tpu as pltpu

F32 = jnp.float32
BF16 = jnp.bfloat16
ACT = jnp.float32
MESH = pl.DeviceIdType.MESH

EPS = 1e-6
GRID_W = 64
CONV_WIDTH = 31
CONV_HALF = CONV_WIDTH // 2
CONV_PAD = 16
POOL_WINDOWS = (2, 4, 8, 16)
POOL_HALF = max(POOL_WINDOWS) // 2
HEADS = 8
NOPE = 128
ROPE = 64
HEAD_W = 256
VDIM = 128
KV_RANK = 256
Q_RANK = 384
ATT_SCALE = (NOPE + ROPE) ** -0.5
LN2 = math.log(2.0)
Q_PRESCALE = ATT_SCALE / LN2
ROPE_THETA = 10000.0
CHUNK = 128
CHUNK_GROUPS = 8
LANES = 128
TM = 256
TM_LATENT = 512
TQ = 1024
TQ_BWD = 2048
ATT_RQ = 128
ATT_KC = 256
VMEM_LIMIT = 60 * 1024 * 1024

ADAM_LR = 0.001
ADAM_B1 = 0.9
ADAM_B2 = 0.999
ADAM_EPS = 1e-08
ADAM_WD = 0.01
ADAM_STEP = 10


def _dot(a, b):
    return jnp.dot(a.astype(BF16), b.astype(BF16), preferred_element_type=F32)


def _dot_nt(a, b):
    return lax.dot_general(a.astype(BF16), b.astype(BF16), (((1,), (1,)), ((), ())), preferred_element_type=F32)


def _dot_tn(a, b):
    return lax.dot_general(a.astype(BF16), b.astype(BF16), (((0,), (0,)), ((), ())), preferred_element_type=F32)


@jax.custom_vjp
def _mm(a, w):
    return _dot(a, w)


def _mm_fwd(a, w):
    return _dot(a, w), (a, w)


def _mm_bwd(res, ct):
    a, w = res
    return _dot_nt(ct, w), _dot_tn(a, ct)


_mm.defvjp(_mm_fwd, _mm_bwd)


def _swap16_impl(x):
    n = x.shape[-1]
    ax = x.ndim - 1
    lane = lax.broadcasted_iota(jnp.int32, x.shape, ax)
    up = pltpu.roll(x, n - 16, ax)
    dn = pltpu.roll(x, 16, ax)
    return jnp.where((lane % 32) < 16, up, dn)


@jax.custom_vjp
def _swap16(x):
    return _swap16_impl(x)


_swap16.defvjp(lambda x: (_swap16_impl(x), None), lambda _, ct: (_swap16_impl(ct),))


def _rms(x, g, n=None):
    n = x.shape[-1] if n is None else n
    return x * lax.rsqrt(jnp.sum(x * x, axis=-1, keepdims=True) * (1.0 / n) + EPS) * g


def _layernorm(x, g, b):
    mu = jnp.mean(x, axis=-1, keepdims=True)
    xc = x - mu
    var = jnp.mean(xc * xc, axis=-1, keepdims=True)
    return xc * lax.rsqrt(var + EPS) * g + b


def _silu(x):
    return x * jax.nn.sigmoid(x)


def _rope(x, cos, sin):
    return x * cos + _swap16(x) * sin


ANY = pl.BlockSpec(memory_space=pl.ANY)


class _Hosted:
    def __init__(self, arrays, out_shapes, sems, start, wait, aliases=None):
        self.arrays, self.out_shapes, self.sems = list(arrays), list(out_shapes), list(sems)
        self.start, self.wait, self.aliases = start, wait, dict(aliases or {})
        self.results = None


def _merge_hosted(parts):
    parts = [p for p in parts if p is not None]
    if not parts:
        return None
    if len(parts) == 1:
        return parts[0]
    offs, a0, o0, s0 = [], 0, 0, 0
    for p in parts:
        offs.append((a0, o0, s0))
        a0, o0, s0 = a0 + len(p.arrays), o0 + len(p.out_shapes), s0 + len(p.sems)

    def run(which):
        def f(ins, outs, sems):
            for p, (a, o, s) in zip(parts, offs):
                getattr(p, which)(ins[a:a + len(p.arrays)], outs[o:o + len(p.out_shapes)], sems[s:s + len(p.sems)])
        return f

    aliases = {}
    for p, (a, o, _) in zip(parts, offs):
        aliases.update({a + i: o + j for i, j in p.aliases.items()})
    merged = _Hosted(sum((p.arrays for p in parts), []), sum((p.out_shapes for p in parts), []),
                     sum((p.sems for p in parts), []), run("start"), run("wait"), aliases)
    merged.parts, merged.offs = parts, offs
    return merged


def _deliver(hosted, results):
    hosted.results = list(results)
    for p, (_, o, _) in zip(getattr(hosted, "parts", []), getattr(hosted, "offs", [])):
        _deliver(p, results[o:o + len(p.out_shapes)])


def _pcall(body, *, name, grid, in_specs, out_specs, out_shape, args, hosted=None, vmem_limit=True, scratch=()):
    n_in, n_out, n_scr = len(args), len(out_shape), len(scratch)
    kwargs = dict(scratch_shapes=list(scratch)) if scratch else {}
    if hosted is not None:
        nhi, nho, inner = len(hosted.arrays), len(hosted.out_shapes), body

        def body(*refs):
            ins, hin = refs[:n_in], refs[n_in:n_in + nhi]
            outs, hout = refs[n_in + nhi:n_in + nhi + n_out], refs[n_in + nhi + n_out:n_in + nhi + n_out + nho]
            own = refs[n_in + nhi + n_out + nho:n_in + nhi + n_out + nho + n_scr]
            sems = refs[n_in + nhi + n_out + nho + n_scr:]
            first, last = None, None
            for k, g in enumerate(grid):
                f, l = pl.program_id(k) == 0, pl.program_id(k) == g - 1
                first = f if first is None else jnp.logical_and(first, f)
                last = l if last is None else jnp.logical_and(last, l)

            @pl.when(first)
            def _():
                hosted.start(hin, hout, sems)

            inner(*ins, *outs, *own)

            @pl.when(last)
            def _():
                hosted.wait(hin, hout, sems)

        in_specs = list(in_specs) + [ANY] * nhi
        out_specs = list(out_specs) + [ANY] * nho
        out_shape = list(out_shape) + hosted.out_shapes
        args = list(args) + hosted.arrays
        kwargs = dict(scratch_shapes=list(scratch) + hosted.sems,
                      input_output_aliases={n_in + i: n_out + j for i, j in hosted.aliases.items()})
    params = dict(dimension_semantics=("arbitrary",) * len(grid))
    if vmem_limit:
        params["vmem_limit_bytes"] = VMEM_LIMIT
    res = pl.pallas_call(body, name=name, grid=grid, in_specs=list(in_specs), out_specs=list(out_specs),
                         out_shape=list(out_shape), compiler_params=pltpu.CompilerParams(**params), **kwargs)(*args)
    if hosted is not None:
        _deliver(hosted, res[n_out:])
    return list(res[:n_out])


def _run_hosted(name, hosted):
    nhi, nho = len(hosted.arrays), len(hosted.out_shapes)

    def body(*refs):
        ins, outs, sems = refs[:nhi], refs[nhi:nhi + nho], refs[nhi + nho:]
        hosted.start(ins, outs, sems)
        hosted.wait(ins, outs, sems)

    res = pl.pallas_call(body, name=name, in_specs=[ANY] * nhi, out_specs=[ANY] * nho, out_shape=hosted.out_shapes,
                         scratch_shapes=hosted.sems, input_output_aliases=hosted.aliases)(*hosted.arrays)
    _deliver(hosted, res)
    return list(res)


def _const_spec(shape, single=False):
    nd = len(shape)
    if single:
        return pl.BlockSpec(shape, lambda b, i: (0,) * nd, pipeline_mode=pl.Buffered(1))
    return pl.BlockSpec(shape, lambda b, i: (0,) * nd)


def _tile_spec(arr, n_lat_tiles, lat_only=False, tm=TM):
    bt, _, cw = arr.shape
    if lat_only:
        return pl.BlockSpec((1, tm, cw), lambda b, i: (b if bt > 1 else 0, jnp.minimum(i, n_lat_tiles - 1), 0))
    return pl.BlockSpec((1, tm, cw), lambda b, i: (b if bt > 1 else 0, i, 0))


def _eparam_spec(arr, n_lat_tiles):
    cw = arr.shape[-1]
    return pl.BlockSpec((1, 1, 1, cw), lambda b, i: (b, (i >= n_lat_tiles).astype(jnp.int32), 0, 0))


def _stage_fwd(name, *, pre, post, wsel, splits, tiles, eparams, sparams, weights, out_widths, out_dtypes,
               batch, n_tiles, n_lat_tiles, hosted=None, tm=TM):
    nt, ne, ns, nw = len(tiles), len(eparams), len(sparams), len(weights)

    def body(*refs):
        t_refs = refs[:nt]
        e_refs = refs[nt:nt + ne]
        s_refs = refs[nt + ne:nt + ne + ns]
        w_refs = refs[nt + ne + ns:nt + ne + ns + nw]
        o_refs = refs[nt + ne + ns + nw:]
        tv = [r[0].astype(F32) for r in t_refs]
        ev = [r[0, 0] for r in e_refs]
        sv = [r[...] for r in s_refs]
        a = pre(tv, ev, sv)
        z = [_dot(a[wsel[j]], w_refs[j][...]) for j in range(nw)]
        if post is None:
            outs = [z[j][:, s:s + w] for (j, s, w) in splits]
        else:
            outs = post(z, tv, ev, sv)
        for o_ref, o in zip(o_refs, outs):
            o_ref[0] = o.astype(o_ref.dtype)

    in_specs = ([_tile_spec(t, n_lat_tiles, tm=tm) for t in tiles] + [_eparam_spec(e, n_lat_tiles) for e in eparams]
                + [_const_spec(s.shape) for s in sparams] + [_const_spec(w.shape, single=True) for w in weights])
    out_shape = [jax.ShapeDtypeStruct((batch, n_tiles * tm, w), dt) for w, dt in zip(out_widths, out_dtypes)]
    out_specs = [pl.BlockSpec((1, tm, w), lambda b, i: (b, i, 0)) for w in out_widths]
    return _pcall(body, name=name, grid=(batch, n_tiles), in_specs=in_specs, out_specs=out_specs,
                  out_shape=out_shape, args=[*tiles, *eparams, *sparams, *weights], hosted=hosted)


def _stage_bwd(name, *, pre, post, wsel, splits, tiles, tile_diff, eparams, sparams, weights, cots, cot_lat_only,
               batch, n_tiles, n_lat_tiles, add=None, add_lat_only=False, hosted=None, w_col_stack=None,
               dt_lat_only=False, tm=TM):
    nt, ne, ns, nw, nc = len(tiles), len(eparams), len(sparams), len(weights), len(cots)
    diff_idx = [k for k in range(nt) if tile_diff[k]]
    nd = len(diff_idx)
    has_add = add is not None
    w_col_stack = w_col_stack or [None] * nw

    def body(*refs):
        pos = 0
        t_refs = refs[pos:pos + nt]; pos += nt
        e_refs = refs[pos:pos + ne]; pos += ne
        s_refs = refs[pos:pos + ns]; pos += ns
        w_refs = refs[pos:pos + nw]; pos += nw
        c_refs = refs[pos:pos + nc]; pos += nc
        if has_add:
            add_ref = refs[pos]; pos += 1
        dt_refs = refs[pos:pos + nd]; pos += nd
        de_refs = refs[pos:pos + ne]; pos += ne
        ds_refs = refs[pos:pos + ns]; pos += ns
        dw_refs = refs[pos:pos + nw]; pos += nw

        b = pl.program_id(0)
        i = pl.program_id(1)
        is_lat = i < n_lat_tiles
        tv = [r[0].astype(F32) for r in t_refs]
        ev = tuple(r[0, 0] for r in e_refs)
        sv = tuple(r[...] for r in s_refs)
        dv0 = tuple(tv[k] for k in diff_idx)

        def merge(dv):
            full = list(tv)
            for k, v in zip(diff_idx, dv):
                full[k] = v
            return full

        def pre_f(dv, ev_, sv_):
            return tuple(pre(merge(dv), list(ev_), list(sv_)))

        a, vjp_pre = jax.vjp(pre_f, dv0, ev, sv)
        cv = []
        for c_ref, lat in zip(c_refs, cot_lat_only):
            c = c_ref[0].astype(F32)
            cv.append(jnp.where(is_lat, c, 0.0) if lat else c)
        if post is None:
            dz = []
            for j in range(nw):
                parts = [cv[k] for k, (jj, _, _) in enumerate(splits) if jj == j]
                dz.append(parts[0] if len(parts) == 1 else jnp.concatenate(parts, axis=1))
            dt2 = de2 = ds2 = None
        else:
            z = tuple(_dot(a[wsel[j]], w_refs[j][...]) for j in range(nw))

            def post_f(z_, dv, ev_, sv_):
                return tuple(post(list(z_), merge(dv), list(ev_), list(sv_)))

            _, vjp_post = jax.vjp(post_f, z, dv0, ev, sv)
            dz, dt2, de2, ds2 = vjp_post(tuple(cv))
        da = [None] * len(a)
        dws = []
        for j in range(nw):
            g = _dot_nt(dz[j], w_refs[j][...])
            da[wsel[j]] = g if da[wsel[j]] is None else da[wsel[j]] + g
            dws.append(_dot_tn(a[wsel[j]], dz[j]))
        da = tuple(jnp.zeros_like(a[k]) if da[k] is None else da[k] for k in range(len(a)))
        dt1, de1, ds1 = vjp_pre(da)

        def plus(u, v):
            return u if v is None else u + v

        for k in range(nd):
            val = plus(dt1[k], None if dt2 is None else dt2[k])
            if has_add and k == 0:
                addv = add_ref[0].astype(F32)
                val = val + (jnp.where(is_lat, addv, 0.0) if add_lat_only else addv)
            if dt_lat_only:
                @pl.when(is_lat)
                def _(k=k, val=val):
                    dt_refs[k][0] = val.astype(dt_refs[k].dtype)
            else:
                dt_refs[k][0] = val.astype(dt_refs[k].dtype)

        seg_first = jnp.logical_or(i == 0, i == n_lat_tiles)
        for k in range(ne):
            val = plus(de1[k], None if de2 is None else de2[k])

            @pl.when(seg_first)
            def _(k=k, val=val):
                de_refs[k][0, 0] = val

            @pl.when(jnp.logical_not(seg_first))
            def _(k=k, val=val):
                de_refs[k][0, 0] += val

        first = jnp.logical_and(b == 0, i == 0)
        acc = [(ds_refs[k], plus(ds1[k], None if ds2 is None else ds2[k])) for k in range(ns)]
        for j in range(nw):
            if w_col_stack[j]:
                cw = dws[j].shape[1] // w_col_stack[j]
                acc += [(dw_refs[j].at[c], dws[j][:, c * cw:(c + 1) * cw]) for c in range(w_col_stack[j])]
            else:
                acc.append((dw_refs[j], dws[j]))
        for ref, val in acc:
            @pl.when(first)
            def _(ref=ref, val=val):
                ref[...] = val

            @pl.when(jnp.logical_not(first))
            def _(ref=ref, val=val):
                ref[...] += val

    in_specs = ([_tile_spec(t, n_lat_tiles, tm=tm) for t in tiles] + [_eparam_spec(e, n_lat_tiles) for e in eparams]
                + [_const_spec(s.shape) for s in sparams] + [_const_spec(w.shape, single=True) for w in weights]
                + [_tile_spec(c, n_lat_tiles, lat, tm) for c, lat in zip(cots, cot_lat_only)])
    args = [*tiles, *eparams, *sparams, *weights, *cots]
    if has_add:
        in_specs.append(_tile_spec(add, n_lat_tiles, add_lat_only, tm))
        args.append(add)
    dt_tiles = n_lat_tiles if dt_lat_only else n_tiles
    out_shape = [jax.ShapeDtypeStruct((batch, dt_tiles * tm, tiles[k].shape[-1]), F32) for k in diff_idx]
    out_specs = [pl.BlockSpec((1, tm, tiles[k].shape[-1]), lambda b, i: (b, jnp.minimum(i, dt_tiles - 1), 0))
                 for k in diff_idx]
    out_shape += [jax.ShapeDtypeStruct(e.shape, F32) for e in eparams]
    out_specs += [_eparam_spec(e, n_lat_tiles) for e in eparams]
    out_shape += [jax.ShapeDtypeStruct(s.shape, F32) for s in sparams]
    out_specs += [_const_spec(s.shape) for s in sparams]
    dw_shapes = [(n, w.shape[0], w.shape[1] // n) if n else w.shape for w, n in zip(weights, w_col_stack)]
    out_shape += [jax.ShapeDtypeStruct(s, F32) for s in dw_shapes]
    out_specs += [_const_spec(s, single=True) for s in dw_shapes]
    res = _pcall(body, name=name, grid=(batch, n_tiles), in_specs=in_specs, out_specs=out_specs,
                 out_shape=out_shape, args=args, hosted=hosted)
    return res[:nd], res[nd:nd + ne], res[nd + ne:nd + ne + ns], res[nd + ne + ns:]


def _pre_adaln(tv, ev, sv):
    x = tv[0]
    sh, sc = ev[0], ev[1]
    return [_rms(x, sv[0]) * (1.0 + sc) + sh]


def _post_residual(x_index):
    def post(z, tv, ev, sv):
        return [tv[x_index] + ev[-1] * z[0]]
    return post


def _pre_conv_out(tv, ev, sv):
    c1, gg = tv[0], tv[1]
    return [_silu(_layernorm(c1, sv[0], sv[1])) * _silu(gg)]


def _pre_pool_out(tv, ev, sv):
    pooled, gg = tv[0], tv[1]
    w_grp, scale = sv[0], sv[1]
    gw = w_grp.shape[-1]
    y = jnp.concatenate([_mm(pooled[:, k * gw:(k + 1) * gw], w_grp[k]) for k in range(w_grp.shape[0])], axis=1)
    return [y * scale * _silu(gg)]


def _pre_rms_only(tv, ev, sv):
    return [_rms(tv[0], sv[0])]


def _post_mla_keys(z, tv, ev, sv):
    krp, cos, sin = tv[1], tv[2], tv[3]
    nope_g, rope_g = sv[1], sv[2]
    kv = z[0]
    kr = _rope(_rms(krp, rope_g, ROPE), cos, sin)
    ks, vs = [], []
    for h in range(HEADS):
        ks.append(_rms(kv[:, h * 2 * NOPE:h * 2 * NOPE + NOPE], nope_g))
        ks.append(kr)
        vs.append(kv[:, h * 2 * NOPE + NOPE:(h + 1) * 2 * NOPE])
    return [jnp.concatenate(ks, axis=1), jnp.concatenate(vs, axis=1)]


def _post_mla_queries(z, tv, ev, sv):
    cos, sin = tv[1], tv[2]
    nope_g, rope_g = sv[1], sv[2]
    q = z[0]
    qs = []
    for h in range(HEADS):
        qs.append(_rms(q[:, h * HEAD_W:h * HEAD_W + NOPE], nope_g))
        qs.append(_rope(_rms(q[:, h * HEAD_W + NOPE:(h + 1) * HEAD_W], rope_g, ROPE), cos, sin))
    return [jnp.concatenate(qs, axis=1) * Q_PRESCALE]


def _pre_mla_out(tv, ev, sv):
    return [tv[0] * _silu(tv[1])]


def _pre_chunk_out(tv, ev, sv):
    u, v, gg = tv[0], tv[1], tv[2]
    ln_g, ln_b, w_s, b_s = sv
    vn = _layernorm(v, ln_g, ln_b)
    rows = []
    for n in range(vn.shape[0] // CHUNK):
        blk = vn[n * CHUNK:(n + 1) * CHUNK]
        cols = [_mm(w_s[g], blk[:, g * LANES:(g + 1) * LANES]) + b_s[:, g:g + 1] for g in range(CHUNK_GROUPS)]
        rows.append(jnp.concatenate(cols, axis=1))
    s = jnp.concatenate(rows, axis=0)
    return [u * s * _silu(gg)]


def _segments(lat_len, tot_len):
    segs = [(0, lat_len)]
    if tot_len > lat_len:
        segs.append((lat_len, tot_len - lat_len))
    return segs


def _pad_rows(x):
    z = jnp.zeros((CONV_PAD, x.shape[1]), x.dtype)
    return jnp.concatenate([z, x, z], axis=0)


def _shifted(xp, j):
    n = xp.shape[0] - 2 * CONV_PAD
    if j != 0:
        xp = pltpu.roll(xp, (-j) % xp.shape[0], 0)
    return xp[CONV_PAD:CONV_PAD + n]


def _conv_fwd(a, bgate, dw, db, lat_len, hosted=None):
    batch, tot, e = a.shape
    segs = _segments(lat_len, tot)

    def body(a_ref, b_ref, dw_ref, db_ref, o_ref):
        w = dw_ref[...]
        for (s0, n) in segs:
            y = a_ref[0, s0:s0 + n, :].astype(F32) * jax.nn.sigmoid(b_ref[0, s0:s0 + n, :].astype(F32))
            yp = _pad_rows(y)
            acc = jnp.zeros_like(y) + db_ref[...]
            for k in range(CONV_WIDTH):
                acc = acc + _shifted(yp, k - CONV_HALF) * w[k:k + 1, :]
            o_ref[0, s0:s0 + n, :] = acc.astype(o_ref.dtype)

    blk = pl.BlockSpec((1, tot, LANES), lambda b, cb: (b, 0, cb))
    return _pcall(
        body, name="conv_fwd", grid=(batch, e // LANES),
        in_specs=[blk, blk, pl.BlockSpec((CONV_WIDTH, LANES), lambda b, cb: (0, cb)),
                  pl.BlockSpec((1, LANES), lambda b, cb: (0, cb))],
        out_specs=[blk], out_shape=[jax.ShapeDtypeStruct(a.shape, ACT)], args=[a, bgate, dw, db], hosted=hosted)[0]


def _conv_bwd(a, bgate, dw, dc1, lat_len, hosted=None):
    batch, tot, e = a.shape
    segs = _segments(lat_len, tot)

    def body(a_ref, b_ref, dw_ref, dc_ref, da_ref, dg_ref, ddw_ref, ddb_ref):
        b = pl.program_id(1)
        w = dw_ref[...]
        ddw_rows = [None] * CONV_WIDTH
        ddb = None
        for (s0, n) in segs:
            av = a_ref[0, s0:s0 + n, :].astype(F32)
            sg = jax.nn.sigmoid(b_ref[0, s0:s0 + n, :].astype(F32))
            y = av * sg
            dc = dc_ref[0, s0:s0 + n, :]
            yp, dcp = _pad_rows(y), _pad_rows(dc)
            dy = jnp.zeros_like(y)
            for k in range(CONV_WIDTH):
                j = k - CONV_HALF
                dy = dy + _shifted(dcp, -j) * w[k:k + 1, :]
                r = jnp.sum(dc * _shifted(yp, j), axis=0, keepdims=True)
                ddw_rows[k] = r if ddw_rows[k] is None else ddw_rows[k] + r
            r = jnp.sum(dc, axis=0, keepdims=True)
            ddb = r if ddb is None else ddb + r
            da_ref[0, s0:s0 + n, :] = dy * sg
            dg_ref[0, s0:s0 + n, :] = dy * av * sg * (1.0 - sg)

        @pl.when(b == 0)
        def _():
            ddw_ref[...] = jnp.zeros_like(ddw_ref)
            ddb_ref[...] = jnp.zeros_like(ddb_ref)

        for k in range(CONV_WIDTH):
            ddw_ref[k:k + 1, :] += ddw_rows[k]
        ddb_ref[...] += ddb

    blk = pl.BlockSpec((1, tot, LANES), lambda cb, b: (b, 0, cb))
    wspec = pl.BlockSpec((CONV_WIDTH, LANES), lambda cb, b: (0, cb))
    bspec = pl.BlockSpec((1, LANES), lambda cb, b: (0, cb))
    return _pcall(
        body, name="conv_bwd", grid=(e // LANES, batch),
        in_specs=[blk, blk, wspec, blk],
        out_specs=[blk, blk, wspec, bspec],
        out_shape=[jax.ShapeDtypeStruct(a.shape, F32), jax.ShapeDtypeStruct(a.shape, F32),
                   jax.ShapeDtypeStruct((CONV_WIDTH, e), F32), jax.ShapeDtypeStruct((1, e), F32)],
        args=[a, bgate, dw, dc1], hosted=hosted)


def _pool_counts(n, half, shape):
    t = lax.broadcasted_iota(jnp.int32, shape, 0)
    cnt = jnp.minimum(t + half, n) - jnp.maximum(t - half, 0)
    return cnt.astype(F32)


def _per_group(fn):
    for k, window in enumerate(POOL_WINDOWS):
        @pl.when(pl.program_id(1) == k)
        def _(window=window):
            fn(window // 2)


def _pool_fwd(v, lat_len, hosted=None):
    batch, tot, e = v.shape
    gw = e // len(POOL_WINDOWS)
    segs = _segments(lat_len, tot)

    def body(v_ref, o_ref):
        def group(half):
            for (s0, n) in segs:
                x = v_ref[0, s0:s0 + n, :]
                xp = _pad_rows(x)
                acc = _shifted(xp, -half)
                for j in range(-half + 1, half):
                    acc = acc + _shifted(xp, j)
                o_ref[0, s0:s0 + n, :] = (acc / _pool_counts(n, half, x.shape) - x).astype(o_ref.dtype)

        _per_group(group)

    blk = pl.BlockSpec((1, tot, gw), lambda b, g: (b, 0, g))
    return _pcall(body, name="pool_fwd", grid=(batch, len(POOL_WINDOWS)), in_specs=[blk], out_specs=[blk],
                  out_shape=[jax.ShapeDtypeStruct(v.shape, ACT)], args=[v], hosted=hosted)[0]


def _pool_bwd(dp, lat_len):
    batch, tot, e = dp.shape
    gw = e // len(POOL_WINDOWS)
    segs = _segments(lat_len, tot)

    def body(d_ref, o_ref):
        def group(half):
            for (s0, n) in segs:
                d = d_ref[0, s0:s0 + n, :]
                dnp = _pad_rows(d / _pool_counts(n, half, d.shape))
                acc = _shifted(dnp, half)
                for j in range(-half + 1, half):
                    acc = acc + _shifted(dnp, -j)
                o_ref[0, s0:s0 + n, :] = acc - d

        _per_group(group)

    blk = pl.BlockSpec((1, tot, gw), lambda b, g: (b, 0, g))
    return pl.pallas_call(
        body, name="pool_bwd", grid=(batch, len(POOL_WINDOWS)), in_specs=[blk], out_specs=blk,
        out_shape=jax.ShapeDtypeStruct(dp.shape, F32),
        compiler_params=pltpu.CompilerParams(dimension_semantics=("arbitrary", "arbitrary"),
                                             vmem_limit_bytes=VMEM_LIMIT),
    )(dp)


def _attn_fwd(q, k, v, hosted=None):
    batch, lq, _ = q.shape
    tk = k.shape[1]
    tq = min(TQ, lq)

    def body(q_ref, k_ref, v_ref, o_ref, lse_ref):
        s2 = _dot_nt(q_ref[0], k_ref[0])
        m2 = jnp.max(s2, axis=-1, keepdims=True)
        e = jnp.exp2(s2 - m2)
        l = jnp.sum(e, axis=-1, keepdims=True)
        o_ref[0] = (_dot(e, v_ref[0]) / l).astype(o_ref.dtype)
        lse_ref[0, 0] = m2 + jnp.log2(l)

    return _pcall(
        body, name="attn_fwd", grid=(batch, HEADS, lq // tq),
        in_specs=[pl.BlockSpec((1, tq, HEAD_W), lambda b, h, i: (b, i, h)),
                  pl.BlockSpec((1, tk, HEAD_W), lambda b, h, i: (b, 0, h)),
                  pl.BlockSpec((1, tk, VDIM), lambda b, h, i: (b, 0, h))],
        out_specs=[pl.BlockSpec((1, tq, VDIM), lambda b, h, i: (b, i, h)),
                   pl.BlockSpec((1, 1, tq, 1), lambda b, h, i: (b, h, i, 0))],
        out_shape=[jax.ShapeDtypeStruct((batch, lq, HEADS * VDIM), ACT),
                   jax.ShapeDtypeStruct((batch, HEADS, lq, 1), F32)], args=[q, k, v], hosted=hosted)


def _attn_bwd(q, k, v, o, lse, do, hosted=None):
    batch, lq, _ = q.shape
    tk = k.shape[1]
    tq = min(TQ_BWD, lq)

    def body(q_ref, k_ref, v_ref, o_ref, lse_ref, do_ref, dq_ref, dk_ref, dv_ref, p_scr, ds_scr):
        i = pl.program_id(2)
        nr = tq // ATT_RQ
        rows = [slice(r * ATT_RQ, (r + 1) * ATT_RQ) for r in range(nr)]
        qv = [q_ref[0, rw, :] for rw in rows]
        dob = [do_ref[0, rw, :].astype(BF16) for rw in rows]
        row_lse = [lse_ref[0, 0, rw, :] for rw in rows]
        delta = [jnp.sum(do_ref[0, rw, :] * o_ref[0, rw, :], axis=-1, keepdims=True) for rw in rows]
        for c in range(tk // ATT_KC):
            keys = slice(c * ATT_KC, (c + 1) * ATT_KC)
            kc, vc = k_ref[0, keys, :], v_ref[0, keys, :]
            for r in range(nr):
                p = jnp.exp2(_dot_nt(qv[r], kc) - row_lse[r])
                dp = _dot_nt(dob[r], vc)
                p_scr[rows[r], keys] = p.astype(BF16)
                ds_scr[rows[r], keys] = (p * (dp - delta[r]) * LN2).astype(BF16)
        dq_ref[0] = _dot(ds_scr[...], k_ref[0])
        dk = _dot_tn(ds_scr[...], q_ref[0])
        dv = _dot_tn(p_scr[...], do_ref[0])

        @pl.when(i == 0)
        def _():
            dk_ref[0] = dk
            dv_ref[0] = dv

        @pl.when(i != 0)
        def _():
            dk_ref[0] += dk
            dv_ref[0] += dv

    return _pcall(
        body, name="attn_bwd", grid=(batch, HEADS, lq // tq),
        in_specs=[pl.BlockSpec((1, tq, HEAD_W), lambda b, h, i: (b, i, h)),
                  pl.BlockSpec((1, tk, HEAD_W), lambda b, h, i: (b, 0, h)),
                  pl.BlockSpec((1, tk, VDIM), lambda b, h, i: (b, 0, h)),
                  pl.BlockSpec((1, tq, VDIM), lambda b, h, i: (b, i, h)),
                  pl.BlockSpec((1, 1, tq, 1), lambda b, h, i: (b, h, i, 0)),
                  pl.BlockSpec((1, tq, VDIM), lambda b, h, i: (b, i, h))],
        out_specs=[pl.BlockSpec((1, tq, HEAD_W), lambda b, h, i: (b, i, h)),
                   pl.BlockSpec((1, tk, HEAD_W), lambda b, h, i: (b, 0, h)),
                   pl.BlockSpec((1, tk, VDIM), lambda b, h, i: (b, 0, h))],
        out_shape=[jax.ShapeDtypeStruct(q.shape, F32), jax.ShapeDtypeStruct(k.shape, F32),
                   jax.ShapeDtypeStruct(v.shape, F32)],
        args=[q, k, v, o, lse, do], hosted=hosted,
        scratch=[pltpu.VMEM((tq, tk), BF16), pltpu.VMEM((tq, tk), BF16)])


def _loss_kernel(y, target):
    batch, lq, d = y.shape

    def body(y_ref, t_ref, l_ref, dy_ref):
        first = jnp.logical_and(pl.program_id(0) == 0, pl.program_id(1) == 0)
        err = y_ref[0] - t_ref[0]
        dy_ref[0] = err * (1.0 / d)
        part = jnp.zeros((1, LANES), F32) + jnp.sum(err * err) * (0.5 / d)

        @pl.when(first)
        def _():
            l_ref[...] = part

        @pl.when(jnp.logical_not(first))
        def _():
            l_ref[...] += part

    tm = TM_LATENT if lq % TM_LATENT == 0 else TM
    blk = pl.BlockSpec((1, tm, d), lambda b, i: (b, i, 0))
    return pl.pallas_call(
        body, name="loss_head", grid=(batch, lq // tm), in_specs=[blk, blk],
        out_specs=[pl.BlockSpec((1, LANES), lambda b, i: (0, 0)), blk],
        out_shape=[jax.ShapeDtypeStruct((1, LANES), F32), jax.ShapeDtypeStruct(y.shape, F32)],
        compiler_params=pltpu.CompilerParams(dimension_semantics=("arbitrary", "arbitrary")),
    )(y, target)


def _rope_tables(lat_len, ctx_len):
    rows = lat_len // GRID_W
    axis_dim = ROPE // 2
    freqs = ROPE_THETA ** (-jnp.arange(0, axis_dim, 2, dtype=F32) / axis_dim)
    ar = jnp.arange(rows, dtype=F32)[:, None] * freqs
    ac = jnp.arange(GRID_W, dtype=F32)[:, None] * freqs
    small = lax.optimization_barrier((jnp.cos(ar), jnp.sin(ar), jnp.cos(ac), jnp.sin(ac)))
    cr, sr = (jnp.repeat(t, GRID_W, axis=0) for t in small[:2])
    cc, sc = (jnp.tile(t, (rows, 1)) for t in small[2:])
    pad = jnp.zeros((lat_len, LANES - ROPE), F32)
    cos = jnp.concatenate([cr, cr, cc, cc, pad], axis=1)
    sin = jnp.concatenate([-sr, sr, -sc, sc, pad], axis=1)
    ident = jnp.concatenate([jnp.ones((ctx_len, ROPE), F32), jnp.zeros((ctx_len, LANES - ROPE), F32)], axis=1)
    cos = jnp.concatenate([cos, ident], axis=0)
    sin = jnp.concatenate([sin, jnp.zeros((ctx_len, LANES), F32)], axis=0)
    return cos[None], sin[None]


def _prep_weights(w):
    p = dict(w)
    kvc = KV_RANK + ROPE
    if "ml_w_in" in w:
        wi = w["ml_w_in"]
        p["ml_w_in"] = jnp.concatenate(
            [wi[:, :kvc], jnp.zeros((wi.shape[0], LANES - ROPE), wi.dtype), wi[:, kvc:]], axis=1)
    if "ml_w_uq" in w:
        uq = w["ml_w_uq"].reshape(Q_RANK, HEADS, NOPE + ROPE)
        p["ml_w_uq"] = jnp.pad(uq, ((0, 0), (0, 0), (0, HEAD_W - NOPE - ROPE))).reshape(Q_RANK, HEADS * HEAD_W)
    if "ml_rope_norm" in w:
        p["ml_rope_norm"] = jnp.pad(w["ml_rope_norm"], ((0, 0), (0, LANES - ROPE)))
    return p


def _unprep_grads(g):
    out = dict(g)
    kvc = KV_RANK + ROPE
    if "ml_w_in" in g:
        wi = g["ml_w_in"]
        out["ml_w_in"] = jnp.concatenate([wi[:, :kvc], wi[:, kvc + LANES - ROPE:]], axis=1)
    if "ml_w_uq" in g:
        uq = g["ml_w_uq"].reshape(Q_RANK, HEADS, HEAD_W)
        out["ml_w_uq"] = uq[:, :, :NOPE + ROPE].reshape(Q_RANK, HEADS * (NOPE + ROPE))
    if "ml_rope_norm" in g:
        out["ml_rope_norm"] = g["ml_rope_norm"][:, :ROPE]
    return out


LAYER_WEIGHTS = (("cv_w_in", "cv_w_out"), ("pl_w_in", "pl_w_grp", "pl_w_out"),
                 ("ml_w_in", "ml_w_uq", "ml_w_ukv", "ml_w_out"), ("ch_w_in", "ch_w_out"))


class _LocalPlan:
    def __init__(self, w):
        self.small = w
        self.grads = {}

    def weights(self, names):
        return {n: self.small[n] for n in names}

    def hosted(self, tag):
        return None

    def after(self, tag):
        pass

    def note(self, values):
        pass

    def layer_grads(self, layer, grads):
        self.grads.update(grads)


def _local_step(xm, target, mods, plan, lat_len):
    batch, tot, d = xm.shape
    e = d
    n_all, n_lat = tot // TM, lat_len // TM
    cos, sin = _rope_tables(lat_len, tot - lat_len)
    g = {}
    w = dict(plan.small)

    def hosting(tag, fn, *args, **kwargs):
        out = fn(*args, hosted=plan.hosted(tag), **kwargs)
        plan.after(tag)
        return out

    def s1_splits(widths):
        out, s = [], 0
        for wd in widths:
            out.append((0, s, wd))
            s += wd
        return out

    tml = TM_LATENT if lat_len % TM_LATENT == 0 else TM
    n_big = lat_len // tml

    def lat_tiles(n_tiles, tm):
        return n_lat if tm == TM else n_tiles

    def fwd_in(name, x, mod, gi, wname, widths, n_tiles, dtypes=None, tm=TM):
        return hosting(name, _stage_fwd, name, pre=_pre_adaln, post=None, wsel=[0], splits=s1_splits(widths),
                       tiles=[x], eparams=[mod[0], mod[1]], sparams=[w["norm_g"][gi:gi + 1]], weights=[w[wname]],
                       out_widths=widths, out_dtypes=dtypes or [ACT] * len(widths), batch=batch, n_tiles=n_tiles,
                       n_lat_tiles=lat_tiles(n_tiles, tm), tm=tm)

    def bwd_in(name, x, mod, gi, wname, widths, n_tiles, cots, lat_only, add, add_lat_only, stack=None,
               dx_lat_only=False):
        (dx,), (dsh, dsc), (dg,), (dw,) = hosting(
            name, _stage_bwd, name, pre=_pre_adaln, post=None, wsel=[0], splits=s1_splits(widths), tiles=[x],
            tile_diff=[True], eparams=[mod[0], mod[1]], sparams=[w["norm_g"][gi:gi + 1]], weights=[w[wname]],
            cots=cots, cot_lat_only=lat_only, batch=batch, n_tiles=n_tiles, n_lat_tiles=n_lat, add=add,
            add_lat_only=add_lat_only, w_col_stack=[stack], dt_lat_only=dx_lat_only)
        return dx, dsh, dsc, dg, dw

    def fwd_out(name, pre, tiles, mod, sparams, wname, n_tiles, tm=TM):
        return hosting(name, _stage_fwd, name, pre=pre, post=_post_residual(len(tiles) - 1), wsel=[0], splits=None,
                       tiles=tiles, eparams=[mod[2]], sparams=sparams, weights=[w[wname]], out_widths=[d],
                       out_dtypes=[F32], batch=batch, n_tiles=n_tiles, n_lat_tiles=lat_tiles(n_tiles, tm), tm=tm)[0]

    def bwd_out(name, pre, tiles, mod, sparams, wname, n_tiles, cot, tm=TM):
        diff = [True] * (len(tiles) - 1) + [False]
        dts, (dgt,), dss, (dw,) = hosting(
            name, _stage_bwd, name, pre=pre, post=_post_residual(len(tiles) - 1), wsel=[0], splits=None, tiles=tiles,
            tile_diff=diff, eparams=[mod[2]], sparams=sparams, weights=[w[wname]], cots=[cot], cot_lat_only=[False],
            batch=batch, n_tiles=n_tiles, n_lat_tiles=lat_tiles(n_tiles, tm), tm=tm)
        return dts, dgt, dss, dw

    w.update(plan.weights(("cv_w_in",)))
    cv_s = [w["cv_ln_g"], w["cv_ln_b"]]
    a0, b0, g0 = fwd_in("cv_in_fwd", xm, mods[0], 0, "cv_w_in", [e, e, e], n_all)
    c1 = hosting("conv_fwd", _conv_fwd, a0, b0, w["cv_dw"], w["cv_db"], lat_len)
    w.update(plan.weights(("cv_w_out",)))
    x1 = fwd_out("cv_out_fwd", _pre_conv_out, [c1, g0, xm], mods[0], cv_s, "cv_w_out", n_all)

    w.update(plan.weights(LAYER_WEIGHTS[1]))
    pl_s = [w["pl_w_grp"], w["pl_scale"]]
    v1, g1 = fwd_in("pl_in_fwd", x1, mods[1], 1, "pl_w_in", [e, e], n_all, dtypes=[F32, ACT])
    pooled = hosting("pool_fwd", _pool_fwd, v1, lat_len)
    x2 = fwd_out("pl_out_fwd", _pre_pool_out, [pooled, g1, x1], mods[1], pl_s, "pl_w_out", n_all)

    w.update(plan.weights(LAYER_WEIGHTS[2]))
    ml_widths = [KV_RANK, LANES, Q_RANK, HEADS * VDIM]
    ckv, krp, cq, g2 = fwd_in("ml_in_fwd", x2, mods[2], 2, "ml_w_in", ml_widths, n_all)
    k_s = [w["ml_kv_norm"], w["ml_nope_norm"][1:2], w["ml_rope_norm"][1:2]]
    q_s = [w["ml_q_norm"], w["ml_nope_norm"][0:1], w["ml_rope_norm"][0:1]]
    kk, vv = hosting("ml_keys_fwd", _stage_fwd, "ml_keys_fwd", pre=_pre_rms_only, post=_post_mla_keys, wsel=[0],
                     splits=None, tiles=[ckv, krp, cos, sin], eparams=[], sparams=k_s, weights=[w["ml_w_ukv"]],
                     out_widths=[HEADS * HEAD_W, HEADS * VDIM], out_dtypes=[BF16, BF16], batch=batch,
                     n_tiles=n_all, n_lat_tiles=n_lat)
    (qq,) = _stage_fwd("ml_queries_fwd", pre=_pre_rms_only, post=_post_mla_queries, wsel=[0], splits=None,
                       tiles=[cq, cos, sin], eparams=[], sparams=q_s, weights=[w["ml_w_uq"]],
                       out_widths=[HEADS * HEAD_W], out_dtypes=[BF16], batch=batch, n_tiles=n_big,
                       n_lat_tiles=n_big, tm=tml)
    att, lse = hosting("attn_fwd", _attn_fwd, qq, kk, vv)
    x3 = fwd_out("ml_out_fwd", _pre_mla_out, [att, g2, x2], mods[2], [], "ml_w_out", n_big, tm=tml)

    w.update(plan.weights(LAYER_WEIGHTS[3]))
    ch_s = [w["ch_ln_g"], w["ch_ln_b"], w["ch_w_s"], w["ch_b_s"]]
    u3, v3, g3 = fwd_in("ch_in_fwd", x3, mods[3], 3, "ch_w_in", [e, e, e], n_big, tm=tml)
    x4 = fwd_out("ch_out_fwd", _pre_chunk_out, [u3, v3, g3, x3], mods[3], ch_s, "ch_w_out", n_big, tm=tml)

    loss_part, dy = _loss_kernel(x4, target)

    dmods = [None] * 4
    dnorm = [None] * 4
    big = {}
    (du, dv, dg), dgt, (g["ch_ln_g"], g["ch_ln_b"], g["ch_w_s"], g["ch_b_s"]), big["ch_w_out"] = bwd_out(
        "ch_out_bwd", _pre_chunk_out, [u3, v3, g3, x3], mods[3], ch_s, "ch_w_out", n_big, dy, tm=tml)
    plan.note({n: g[n] for n in ("ch_ln_g", "ch_ln_b", "ch_w_s", "ch_b_s")})
    dx3, dsh, dsc, dnorm[3], big["ch_w_in"] = bwd_in("ch_in_bwd", x3, mods[3], 3, "ch_w_in", [e, e, e], n_lat,
                                                     [du, dv, dg], [False] * 3, dy, False, stack=N_CHIP)
    dmods[3] = (dsh, dsc, dgt)
    plan.layer_grads(3, big)

    big = {}
    (datt, dg), dgt, _, big["ml_w_out"] = bwd_out("ml_out_bwd", _pre_mla_out, [att, g2, x2], mods[2], [],
                                                  "ml_w_out", n_big, dx3, tm=tml)
    dq, dk, dvv = hosting("attn_bwd", _attn_bwd, qq, kk, vv, att, lse, datt)
    (dcq,), _, (g["ml_q_norm"], dnope0, drope0), (big["ml_w_uq"],) = hosting(
        "ml_queries_bwd", _stage_bwd, "ml_queries_bwd", pre=_pre_rms_only, post=_post_mla_queries, wsel=[0],
        splits=None, tiles=[cq, cos, sin], tile_diff=[True, False, False], eparams=[], sparams=q_s,
        weights=[w["ml_w_uq"]], cots=[dq], cot_lat_only=[False], batch=batch, n_tiles=n_big, n_lat_tiles=n_big,
        tm=tml)
    (dckv, dkrp), _, (g["ml_kv_norm"], dnope1, drope1), (big["ml_w_ukv"],) = hosting(
        "ml_keys_bwd", _stage_bwd, "ml_keys_bwd", pre=_pre_rms_only, post=_post_mla_keys, wsel=[0], splits=None,
        tiles=[ckv, krp, cos, sin], tile_diff=[True, True, False, False], eparams=[], sparams=k_s,
        weights=[w["ml_w_ukv"]], cots=[dk, dvv], cot_lat_only=[False, False], batch=batch, n_tiles=n_all,
        n_lat_tiles=n_lat, w_col_stack=[N_CHIP])
    g["ml_nope_norm"] = jnp.concatenate([dnope0, dnope1], axis=0)
    g["ml_rope_norm"] = jnp.concatenate([drope0, drope1], axis=0)
    dx2, dsh, dsc, dnorm[2], big["ml_w_in"] = bwd_in("ml_in_bwd", x2, mods[2], 2, "ml_w_in", ml_widths, n_all,
                                                     [dckv, dkrp, dcq, dg], [False, False, True, True], dx3, True)
    dmods[2] = (dsh, dsc, dgt)
    plan.layer_grads(2, big)

    big = {}
    (dpooled, dg), dgt, (big["pl_w_grp"], g["pl_scale"]), big["pl_w_out"] = bwd_out(
        "pl_out_bwd", _pre_pool_out, [pooled, g1, x1], mods[1], pl_s, "pl_w_out", n_all, dx2)
    dv1 = _pool_bwd(dpooled, lat_len)
    dx1, dsh, dsc, dnorm[1], big["pl_w_in"] = bwd_in("pl_in_bwd", x1, mods[1], 1, "pl_w_in", [e, e], n_all,
                                                     [dv1, dg], [False] * 2, dx2, False, stack=N_CHIP)
    dmods[1] = (dsh, dsc, dgt)
    plan.layer_grads(1, big)

    big = {}
    (dc1, dg), dgt, (g["cv_ln_g"], g["cv_ln_b"]), big["cv_w_out"] = bwd_out(
        "cv_out_bwd", _pre_conv_out, [c1, g0, xm], mods[0], cv_s, "cv_w_out", n_all, dx1)
    plan.layer_grads(0, big)
    big = {}
    da, db, g["cv_dw"], g["cv_db"] = hosting("conv_bwd", _conv_bwd, a0, b0, w["cv_dw"], dc1, lat_len)
    dx0, dsh, dsc, dnorm[0], big["cv_w_in"] = bwd_in("cv_in_bwd", xm, mods[0], 0, "cv_w_in", [e, e, e], n_all,
                                                     [da, db, dg], [False] * 3, dx1, False, stack=N_CHIP,
                                                     dx_lat_only=True)
    dmods[0] = (dsh, dsc, dgt)
    plan.layer_grads(0, big)
    g["norm_g"] = jnp.concatenate(dnorm, axis=0)
    return loss_part, dx0, dmods, g


N_DEV = 8
N_CHIP = 4
ANY = pl.BlockSpec(memory_space=pl.ANY)


def _my_place():
    return lax.axis_index("x"), lax.axis_index("y"), lax.axis_index("c")


def _flip(v, f):
    return 1 - v if f else v


def _ag8_copies(x):
    def plan(ins, outs, sems):
        mx, my, mc = _my_place()
        me = 4 * mx + 2 * my + mc
        sends, recvs = [], []
        for rel in range(1, N_DEV):
            peer = (_flip(mx, rel & 4), _flip(my, rel & 2), _flip(mc, rel & 1))
            src_dev = 4 * peer[0] + 2 * peer[1] + peer[2]
            sends.append(_remote(ins[0], outs[0].at[me], sems, rel - 1, peer))
            recvs.append(_remote(ins[0], outs[0].at[src_dev], sems, rel - 1, peer))
        return sends, recvs, [pltpu.make_async_copy(ins[0], outs[0].at[me], sems[2].at[0])]

    return _copies_hosted([x], [jax.ShapeDtypeStruct((N_DEV,) + x.shape, x.dtype)], (N_DEV - 1, N_DEV - 1, 1), plan)


def _ag8(name, x):
    return _run_hosted(name, _ag8_copies(x))[0]


def _ag8_column_copies(x, width):
    def plan(ins, outs, sems):
        mx, my, mc = _my_place()
        me = 4 * mx + 2 * my + mc
        sends, recvs = [], []
        for rel in range(1, N_DEV):
            peer = (_flip(mx, rel & 4), _flip(my, rel & 2), _flip(mc, rel & 1))
            src_dev = 4 * peer[0] + 2 * peer[1] + peer[2]
            cols = pl.ds(pl.multiple_of((2 * peer[0] + peer[1]) * width, LANES), width)
            sends.append(_remote(ins[0].at[:, cols], outs[0].at[me], sems, rel - 1, peer))
            recvs.append(_remote(ins[0].at[:, cols], outs[0].at[src_dev], sems, rel - 1, peer))
        mine = pl.ds(pl.multiple_of((2 * mx + my) * width, LANES), width)
        return sends, recvs, [pltpu.make_async_copy(ins[0].at[:, mine], outs[0].at[me], sems[2].at[0])]

    return _copies_hosted([x], [jax.ShapeDtypeStruct((N_DEV, x.shape[0], width), x.dtype)],
                          (N_DEV - 1, N_DEV - 1, 1), plan)


def _chip_rows_copies(x, rows_per_dev, shared_row):
    n_out = rows_per_dev + 1

    def plan(ins, outs, sems):
        mx, my, mc = _my_place()
        chip = 2 * mx + my
        sends, recvs = [], []

        def pieces(dev):
            return [(ins[0].at[pl.ds(dev * rows_per_dev, rows_per_dev)], slice(0, rows_per_dev)),
                    (ins[0].at[pl.ds(shared_row, 1)], slice(rows_per_dev, n_out))]

        for k, peer, pchip in _chip_peers(mx, my, mc):
            for t, (src, where) in enumerate(pieces(2 * pchip + mc)):
                sends.append(_remote(src, outs[0].at[chip, where], sems, 2 * k + t, peer))
                recvs.append(_remote(src, outs[0].at[pchip, where], sems, 2 * k + t, peer))
        locals_ = [pltpu.make_async_copy(src, outs[0].at[chip, where], sems[2].at[t])
                   for t, (src, where) in enumerate(pieces(2 * chip + mc))]
        return sends, recvs, locals_

    return _copies_hosted([x], [jax.ShapeDtypeStruct((N_CHIP, n_out) + x.shape[1:], x.dtype)], (6, 6, 2), plan)


def _chip_peers(mx, my, mc):
    out = []
    for rel in range(1, N_CHIP):
        px, py = _flip(mx, rel & 2), _flip(my, rel & 1)
        out.append((rel - 1, (px, py, mc), 2 * px + py))
    return out


def _half(mc, rows):
    return pl.ds(pl.multiple_of(mc * (rows // 2), 8), rows // 2)


def _copies_hosted(arrays, out_shapes, n_sems, plan, aliases=None):
    def start(ins, outs, sems):
        sends, _, locals_ = plan(ins, outs, sems)
        for cp in locals_ + sends:
            cp.start()

    def wait(ins, outs, sems):
        sends, recvs, locals_ = plan(ins, outs, sems)
        for cp in recvs:
            cp.wait_recv()
        for cp in sends:
            cp.wait_send()
        for cp in locals_:
            cp.wait()

    return _Hosted(arrays, out_shapes, [pltpu.SemaphoreType.DMA((k,)) for k in n_sems], start, wait, aliases)


def _remote(src, dst, sems, k, peer):
    return pltpu.make_async_remote_copy(src_ref=src, dst_ref=dst, send_sem=sems[0].at[k], recv_sem=sems[1].at[k],
                                        device_id=peer, device_id_type=MESH)


def _gather_ici(shards):
    n = len(shards)

    def plan(ins, outs, sems):
        mx, my, mc = _my_place()
        chip = 2 * mx + my
        sends, recvs, locals_ = [], [], []
        for a in range(n):
            rows = ins[a].shape[0]
            locals_.append(pltpu.make_async_copy(ins[a], outs[a].at[chip], sems[2].at[a]))
            for k, peer, pchip in _chip_peers(mx, my, mc):
                src = ins[a].at[_half(mc, rows)]
                sends.append(_remote(src, outs[a].at[chip, _half(mc, rows)], sems, 3 * a + k, peer))
                recvs.append(_remote(src, outs[a].at[pchip, _half(mc, rows)], sems, 3 * a + k, peer))
        return sends, recvs, locals_

    return _copies_hosted(shards, [jax.ShapeDtypeStruct((N_CHIP,) + s.shape, s.dtype) for s in shards],
                          (3 * n, 3 * n, n), plan)


def _sibling_fill(arrays, row_axis, chips_only_other):
    n = len(arrays)
    per = 3 if chips_only_other else 1

    def plan(ins, outs, sems):
        mx, my, mc = _my_place()
        sibling = (mx, my, 1 - mc)

        def views(a, core):
            rows = outs[a].shape[row_axis]
            if chips_only_other:
                return [outs[a].at[pchip, _half(core, rows)] for _, _, pchip in _chip_peers(mx, my, mc)]
            return [outs[a].at[_half(core, rows)]]

        sends, recvs = [], []
        for a in range(n):
            for k, v in enumerate(views(a, mc)):
                sends.append(_remote(v, v, sems, per * a + k, sibling))
            for k, v in enumerate(views(a, 1 - mc)):
                recvs.append(_remote(v, v, sems, per * a + k, sibling))
        return sends, recvs, []

    return _copies_hosted(arrays, [jax.ShapeDtypeStruct(s.shape, s.dtype) for s in arrays], (per * n, per * n), plan,
                          aliases={a: a for a in range(n)})


def _grad_swap_d2d(stacks):
    n = len(stacks)

    def plan(ins, outs, sems):
        mx, my, mc = _my_place()
        sibling = (mx, my, 1 - mc)
        sends = [_remote(ins[a].at[:, _half(1 - mc, ins[a].shape[1])], outs[a], sems, a, sibling) for a in range(n)]
        return sends, sends, []

    return _copies_hosted(stacks, [jax.ShapeDtypeStruct((N_CHIP, s.shape[1] // 2, s.shape[2]), s.dtype)
                                   for s in stacks], (n, n), plan)


def _grad_exchange_ici(parts):
    n = len(parts)

    def plan(ins, outs, sems):
        mx, my, mc = _my_place()
        chip = 2 * mx + my
        sends, recvs, locals_ = [], [], []
        for a in range(n):
            locals_.append(pltpu.make_async_copy(ins[a].at[chip], outs[a].at[chip], sems[2].at[a]))
            for k, peer, pchip in _chip_peers(mx, my, mc):
                sends.append(_remote(ins[a].at[pchip], outs[a].at[chip], sems, 3 * a + k, peer))
                recvs.append(_remote(ins[a].at[pchip], outs[a].at[pchip], sems, 3 * a + k, peer))
        return sends, recvs, locals_

    return _copies_hosted(parts, [jax.ShapeDtypeStruct(s.shape, s.dtype) for s in parts], (3 * n, 3 * n, n), plan)


def _row_block(rows, limit=256):
    for t in range(min(rows, limit), 7, -8):
        if rows % t == 0 and t % 8 == 0:
            return t
    return rows


def _grad_add_half(core, stack, received):
    _, rows, cw = stack.shape
    rh = rows // 2
    tr = _row_block(rh)

    def body(s_ref, a_ref, b_ref, o_ref):
        o_ref[...] = (a_ref[...] + b_ref[...]).astype(o_ref.dtype)

    grid_spec = pltpu.PrefetchScalarGridSpec(
        num_scalar_prefetch=1, grid=(rh // tr,),
        in_specs=[pl.BlockSpec((N_CHIP, tr, cw), lambda i, s: (0, s[0] * (rh // tr) + i, 0)),
                  pl.BlockSpec((N_CHIP, tr, cw), lambda i, s: (0, i, 0))],
        out_specs=pl.BlockSpec((N_CHIP, tr, cw), lambda i, s: (0, i, 0)))
    return pl.pallas_call(
        body, name="grad_add_half", grid_spec=grid_spec, out_shape=jax.ShapeDtypeStruct(received.shape, BF16),
        compiler_params=pltpu.CompilerParams(dimension_semantics=("arbitrary",), vmem_limit_bytes=VMEM_LIMIT),
    )(core, stack, received)


def _adamw(name, row_off, parts, w, m, v, rows, hosted=None):
    n, _, cw = parts.shape
    tr = _row_block(rows, 256)

    def update(p_ref, w_ref, m_ref, v_ref, g_ref, d_ref, nm_ref, nv_ref):
        g = p_ref[0].astype(F32)
        for k in range(1, n):
            g = g + p_ref[k].astype(F32)
        nm = ADAM_B1 * m_ref[...] + (1.0 - ADAM_B1) * g
        nv = ADAM_B2 * v_ref[...] + (1.0 - ADAM_B2) * (g * g)
        m_hat = nm / (1.0 - ADAM_B1 ** ADAM_STEP)
        v_hat = nv / (1.0 - ADAM_B2 ** ADAM_STEP)
        g_ref[...] = g
        d_ref[...] = -ADAM_LR * (m_hat / (jnp.sqrt(v_hat) + ADAM_EPS) + ADAM_WD * w_ref[...])
        nm_ref[...] = nm
        nv_ref[...] = nv

    out_shape = [jax.ShapeDtypeStruct(w.shape, F32)] * 4
    if row_off is None:
        blk = pl.BlockSpec((tr, cw), lambda i: (i, 0))
        return _pcall(update, name=name, grid=(rows // tr,), out_specs=[blk] * 4, out_shape=out_shape,
                      in_specs=[pl.BlockSpec((n, tr, cw), lambda i: (0, i, 0)), blk, blk, blk],
                      args=[parts, w, m, v], hosted=hosted)

    def body(s_ref, *refs):
        update(*refs)

    full = pl.BlockSpec((tr, cw), lambda i, s: (s[0] // tr + i, 0))
    grid_spec = pltpu.PrefetchScalarGridSpec(
        num_scalar_prefetch=1, grid=(rows // tr,),
        in_specs=[pl.BlockSpec((n, tr, cw), lambda i, s: (0, i, 0)), full, full, full],
        out_specs=[full, full, full, full])
    return pl.pallas_call(
        body, name=name, grid_spec=grid_spec, out_shape=out_shape,
        compiler_params=pltpu.CompilerParams(dimension_semantics=("arbitrary",), vmem_limit_bytes=VMEM_LIMIT),
    )(row_off, parts, w, m, v)


def _sum8(x):
    _, r, cw = x.shape
    tr = _row_block(r, 64)

    def body(x_ref, o_ref):
        acc = x_ref[0]
        for k in range(1, N_DEV):
            acc = acc + x_ref[k]
        o_ref[...] = acc

    return pl.pallas_call(
        body, name="sum8", grid=(r // tr,), in_specs=[pl.BlockSpec((N_DEV, tr, cw), lambda i: (0, i, 0))],
        out_specs=pl.BlockSpec((tr, cw), lambda i: (i, 0)), out_shape=jax.ShapeDtypeStruct((r, cw), F32),
        compiler_params=pltpu.CompilerParams(dimension_semantics=("arbitrary",)),
    )(x)


MOD_ROWS = 24
CTX_ROW = 16


def _mod_fwd(c_rows, w_mod, b_mod, hosted=None):
    nl, d, nn = w_mod.shape

    def body(c_ref, w_ref, b_ref, o_ref):
        o_ref[0] = _dot(_silu(c_ref[...]), w_ref[0]) + b_ref[0]

    return _pcall(
        body, name="mod_fwd", grid=(nl,),
        in_specs=[pl.BlockSpec((MOD_ROWS, d), lambda i: (0, 0)), pl.BlockSpec((1, d, nn), lambda i: (i, 0, 0)),
                  pl.BlockSpec((1, 1, nn), lambda i: (i, 0, 0))],
        out_specs=[pl.BlockSpec((1, MOD_ROWS, nn), lambda i: (i, 0, 0))],
        out_shape=[jax.ShapeDtypeStruct((nl, MOD_ROWS, nn), F32)], args=[c_rows, w_mod, b_mod], hosted=hosted)[0]


def _mod_bwd_rows(dlat, dctx_parts):
    nl, ne, nn = dlat.shape

    def body(l_ref, c_ref, db_ref, dc_ref):
        dc = c_ref[0, 0:1, :]
        for k in range(1, N_DEV):
            dc = dc + c_ref[0, k:k + 1, :]
        db = dc
        for k in range(ne):
            db = db + l_ref[0, k:k + 1, :]
        db_ref[0] = db
        dc_ref[0] = dc

    return pl.pallas_call(
        body, name="mod_bwd_rows", grid=(nl,),
        in_specs=[pl.BlockSpec((1, ne, nn), lambda i: (i, 0, 0)), pl.BlockSpec((1, N_DEV, nn), lambda i: (i, 0, 0))],
        out_specs=[pl.BlockSpec((1, 1, nn), lambda i: (i, 0, 0))] * 2,
        out_shape=[jax.ShapeDtypeStruct((nl, 1, nn), F32)] * 2,
        compiler_params=pltpu.CompilerParams(dimension_semantics=("arbitrary",)),
    )(dlat, dctx_parts)


def _mod_bwd_w(c_cols, d_rows, w_mod, hosted=None):
    nl, d, nn = w_mod.shape

    def body(c_ref, d_ref, w_ref, dw_ref, dc_ref):
        i = pl.program_id(0)
        c = c_ref[...]
        sg = jax.nn.sigmoid(c)
        s = c * sg
        dv = d_ref[0]
        acc = s[:, 0:1] * dv[0:1, :]
        for r in range(1, CTX_ROW + 1):
            acc = acc + s[:, r:r + 1] * dv[r:r + 1, :]
        dw_ref[0] = acc
        ds_ctx = jnp.sum(w_ref[0] * dv[CTX_ROW:CTX_ROW + 1, :], axis=1, keepdims=True)
        cc, sc = c[:, CTX_ROW:CTX_ROW + 1], sg[:, CTX_ROW:CTX_ROW + 1]
        part = ds_ctx * (sc * (1.0 + cc * (1.0 - sc)))

        @pl.when(i == 0)
        def _():
            dc_ref[...] = part

        @pl.when(i != 0)
        def _():
            dc_ref[...] += part

    return _pcall(
        body, name="mod_bwd_w", grid=(nl,),
        in_specs=[pl.BlockSpec((d, MOD_ROWS), lambda i: (0, 0)), pl.BlockSpec((1, MOD_ROWS, nn), lambda i: (i, 0, 0)),
                  pl.BlockSpec((1, d, nn), lambda i: (i, 0, 0))],
        out_specs=[pl.BlockSpec((1, d, nn), lambda i: (i, 0, 0)), pl.BlockSpec((d, 1), lambda i: (0, 0))],
        out_shape=[jax.ShapeDtypeStruct((nl, d, nn), F32), jax.ShapeDtypeStruct((d, 1), F32)],
        args=[c_cols, d_rows, w_mod], hosted=hosted)


def _pack_rows(arrays, width, row_multiple=8):
    rows, spans, r0 = [], [], 0
    for a in arrays:
        flat = a.reshape(-1)
        nr = -(-flat.shape[0] // width)
        held = -(-nr // 8) * 8
        flat = jnp.pad(flat, (0, held * width - flat.shape[0]))
        rows.append(flat.reshape(held, width))
        spans.append((r0, nr, a.shape))
        r0 += held
    if r0 % row_multiple:
        rows.append(jnp.zeros((row_multiple - r0 % row_multiple, width), F32))
    return jnp.concatenate(rows, axis=0), spans


def _unpack_rows(packed, spans):
    out = []
    for r0, nr, shape in spans:
        out.append(packed[r0:r0 + nr].reshape(-1)[:math.prod(shape)].reshape(shape))
    return out


BIG = {"cv_w_in": 1, "cv_w_out": 0, "pl_w_in": 1, "pl_w_grp": None, "pl_w_out": 0, "ml_w_in": 1, "ml_w_uq": 1,
       "ml_w_ukv": 1, "ml_w_out": 0, "ch_w_in": 1, "ch_w_out": 0}
SMALL_SHARDED = ["cv_dw", "pl_scale", "ml_q_norm", "ml_kv_norm", "ch_ln_g", "ch_ln_b"]
SMALL_REPLICATED = ["c_ctx", "norm_g", "b_mod", "cv_db", "cv_ln_g", "cv_ln_b", "ml_nope_norm", "ml_rope_norm",
                    "ch_w_s", "ch_b_s"]
WEIGHTS = ['c_ctx', 'norm_g', 'w_mod', 'b_mod', 'cv_w_in', 'cv_dw', 'cv_db', 'cv_ln_g', 'cv_ln_b', 'cv_w_out',
           'pl_w_in', 'pl_w_grp', 'pl_scale', 'pl_w_out', 'ml_w_in', 'ml_q_norm', 'ml_kv_norm', 'ml_w_uq', 'ml_w_ukv',
           'ml_nope_norm', 'ml_rope_norm', 'ml_w_out', 'ch_w_in', 'ch_ln_g', 'ch_ln_b', 'ch_w_s', 'ch_b_s', 'ch_w_out']


def _shard2d(name, a):
    if name == "pl_w_grp":
        return a.reshape(a.shape[-3] * a.shape[-2], a.shape[-1])
    return a.reshape(a.shape[-2], a.shape[-1])


def _unstack(name, s):
    if name == "pl_w_grp":
        ng = len(POOL_WINDOWS)
        return s.reshape(N_CHIP, ng, s.shape[1] // ng, s.shape[2]).transpose(1, 0, 2, 3).reshape(ng, -1, s.shape[2])
    if BIG[name] == 0:
        return s.reshape(-1, s.shape[2])
    return s.transpose(1, 0, 2).reshape(s.shape[1], -1)


def _stack(name, g):
    if g.ndim == 3 and name != "pl_w_grp":
        return g
    if name == "pl_w_grp":
        ng = len(POOL_WINDOWS)
        return g.reshape(ng, N_CHIP, -1, g.shape[2]).transpose(1, 0, 2, 3).reshape(N_CHIP, -1, g.shape[2])
    if BIG[name] == 0:
        return g.reshape(N_CHIP, -1, g.shape[1])
    return g.reshape(g.shape[0], N_CHIP, -1).transpose(1, 0, 2)


L0, L1, L2, L3 = LAYER_WEIGHTS
EARLY_SMALL = ("ch_w_s", "ch_b_s", "ch_ln_g", "ch_ln_b")
MESH_SCHEDULE = {
    "ag8_inputs": [("gather", L0[:1])], "mod_fwd": [("gfill", L0[:1])],
    "cv_in_fwd": [("gather", L0[1:]), ("gather", L1[1:])], "conv_fwd": [("gfill", L0[1:]), ("gather", L1[:1])],
    "cv_out_fwd": [("gfill", L1), ("gather", L2[3:])],
    "pl_in_fwd": [("gather", L2[:1])], "pool_fwd": [("gather", L2[1:3])], "pl_out_fwd": [("gfill", L2)],
    "attn_fwd": [("gather", L3)], "ml_out_fwd": [("gfill", L3)],
    "ch_in_bwd": [("small", EARLY_SMALL)],
    "ml_out_bwd": [("swap", L3)], "attn_bwd": [("exch", L3)], "ml_queries_bwd": [("ofill", L3)],
    "pl_out_bwd": [("swap", L2)], "pl_in_bwd": [("exch", L2)],
    "cv_out_bwd": [("swap", L1), ("ofill", L2)], "conv_bwd": [("exch", L1), ("swap", L0[1:])],
    "ag8_dmod": [("swap", L0[:1]), ("exch", L0[1:])], "mod_bwd_w": [("exch", L0[:1]), ("ofill", L1)],
    "ag8_small_grads": [("ofill", L0[:1]), ("ofill", L0[1:])],
}
GRAD_GROUPS = (L3, L2, L1, L0[1:], L0[:1])


class _MeshPlan:
    def __init__(self, weights, m, v, core):
        self.W, self.M, self.V, self.core = weights, m, v, core
        self.small = None
        self.stack, self.gstack, self.part, self.half, self.out = {}, {}, {}, {}, {}
        self.notes, self.early = {}, {}
        self.live, self.done = {}, set()

    def _make(self, op, names):
        if op == "gather":
            return _gather_ici([_shard2d(n, self.W[n]).astype(BF16) for n in names])
        if op == "gfill":
            return _sibling_fill([self.stack[n] for n in names], 1, True)
        if op == "swap":
            return _grad_swap_d2d([self.gstack[n] for n in names])
        if op == "exch":
            return _grad_exchange_ici([self.part[n] for n in names])
        if op == "ofill":
            return _sibling_fill([t for n in names for t in self.half[n]], 0, False)
        pack, self.early_spans = _pack_rows([self.notes[n] for n in names], LANES, 128)
        return _ag8_copies(pack)

    def _finish_op(self, op, names, hosted):
        self.done.add((op, names))
        res = hosted.results
        if op in ("gather", "gfill"):
            self.stack.update(zip(names, res))
        elif op == "swap":
            for n, r in zip(names, res):
                self.part[n] = _grad_add_half(self.core, self.gstack[n], r)
        elif op == "exch":
            for n, q in zip(names, res):
                rh = q.shape[1]
                self.half[n] = _adamw("adamw_" + n, self.core * rh, q, _shard2d(n, self.W[n]),
                                      _shard2d(n, self.M[n]), _shard2d(n, self.V[n]), rh)
        elif op == "ofill":
            for k, n in enumerate(names):
                self.out[n] = tuple(r.reshape(self.W[n].shape) for r in res[4 * k:4 * k + 4])
        else:
            self.early.update(zip(names, _unpack_rows(_sum8(res[0]), self.early_spans)))

    def alone(self, op, names):
        hosted = self._make(op, names)
        _run_hosted("%s_%s" % (op, names[0]), hosted)
        self._finish_op(op, names, hosted)

    def weights(self, names):
        wk = {n: _unstack(n, self.stack[n]) for n in names}
        if "pl_w_grp" in wk:
            wk["pl_w_grp"] = wk["pl_w_grp"].astype(F32)
        return _prep_weights(wk)

    def hosted(self, tag):
        self.live[tag] = [(op, names, self._make(op, names)) for op, names in MESH_SCHEDULE.get(tag, [])]
        return _merge_hosted([h for _, _, h in self.live[tag]])

    def after(self, tag):
        for op, names, hosted in self.live.pop(tag, []):
            self._finish_op(op, names, hosted)

    def note(self, values):
        self.notes.update(values)

    def layer_grads(self, layer, grads):
        g = _unprep_grads(grads)
        for n in g:
            self.gstack[n] = _stack(n, g[n])

    def finish(self):
        for names in GRAD_GROUPS:
            for op in ("swap", "exch", "ofill"):
                if (op, names) not in self.done:
                    self.alone(op, names)
        return self.out


def kernel(x, c, ctx, c_ctx, norm_g, w_mod, b_mod, cv_w_in, cv_dw, cv_db, cv_ln_g, cv_ln_b, cv_w_out, pl_w_in, pl_w_grp, pl_scale, pl_w_out, ml_w_in, ml_q_norm, ml_kv_norm, ml_w_uq, ml_w_ukv, ml_nope_norm, ml_rope_norm, ml_w_out, ch_w_in, ch_ln_g, ch_ln_b, ch_w_s, ch_b_s, ch_w_out, loss_target, m_c_ctx, m_norm_g, m_w_mod, m_b_mod, m_cv_w_in, m_cv_dw, m_cv_db, m_cv_ln_g, m_cv_ln_b, m_cv_w_out, m_pl_w_in, m_pl_w_grp, m_pl_scale, m_pl_w_out, m_ml_w_in, m_ml_q_norm, m_ml_kv_norm, m_ml_w_uq, m_ml_w_ukv, m_ml_nope_norm, m_ml_rope_norm, m_ml_w_out, m_ch_w_in, m_ch_ln_g, m_ch_ln_b, m_ch_w_s, m_ch_b_s, m_ch_w_out, v_c_ctx, v_norm_g, v_w_mod, v_b_mod, v_cv_w_in, v_cv_dw, v_cv_db, v_cv_ln_g, v_cv_ln_b, v_cv_w_out, v_pl_w_in, v_pl_w_grp, v_pl_scale, v_pl_w_out, v_ml_w_in, v_ml_q_norm, v_ml_kv_norm, v_ml_w_uq, v_ml_w_ukv, v_ml_nope_norm, v_ml_rope_norm, v_ml_w_out, v_ch_w_in, v_ch_ln_g, v_ch_ln_b, v_ch_w_s, v_ch_b_s, v_ch_w_out):
    W = dict(c_ctx=c_ctx, norm_g=norm_g, w_mod=w_mod, b_mod=b_mod, cv_w_in=cv_w_in, cv_dw=cv_dw, cv_db=cv_db, cv_ln_g=cv_ln_g, cv_ln_b=cv_ln_b, cv_w_out=cv_w_out, pl_w_in=pl_w_in, pl_w_grp=pl_w_grp, pl_scale=pl_scale, pl_w_out=pl_w_out, ml_w_in=ml_w_in, ml_q_norm=ml_q_norm, ml_kv_norm=ml_kv_norm, ml_w_uq=ml_w_uq, ml_w_ukv=ml_w_ukv, ml_nope_norm=ml_nope_norm, ml_rope_norm=ml_rope_norm, ml_w_out=ml_w_out, ch_w_in=ch_w_in, ch_ln_g=ch_ln_g, ch_ln_b=ch_ln_b, ch_w_s=ch_w_s, ch_b_s=ch_b_s, ch_w_out=ch_w_out)
    M = dict(c_ctx=m_c_ctx, norm_g=m_norm_g, w_mod=m_w_mod, b_mod=m_b_mod, cv_w_in=m_cv_w_in, cv_dw=m_cv_dw, cv_db=m_cv_db, cv_ln_g=m_cv_ln_g, cv_ln_b=m_cv_ln_b, cv_w_out=m_cv_w_out, pl_w_in=m_pl_w_in, pl_w_grp=m_pl_w_grp, pl_scale=m_pl_scale, pl_w_out=m_pl_w_out, ml_w_in=m_ml_w_in, ml_q_norm=m_ml_q_norm, ml_kv_norm=m_ml_kv_norm, ml_w_uq=m_ml_w_uq, ml_w_ukv=m_ml_w_ukv, ml_nope_norm=m_ml_nope_norm, ml_rope_norm=m_ml_rope_norm, ml_w_out=m_ml_w_out, ch_w_in=m_ch_w_in, ch_ln_g=m_ch_ln_g, ch_ln_b=m_ch_ln_b, ch_w_s=m_ch_w_s, ch_b_s=m_ch_b_s, ch_w_out=m_ch_w_out)
    V = dict(c_ctx=v_c_ctx, norm_g=v_norm_g, w_mod=v_w_mod, b_mod=v_b_mod, cv_w_in=v_cv_w_in, cv_dw=v_cv_dw, cv_db=v_cv_db, cv_ln_g=v_cv_ln_g, cv_ln_b=v_cv_ln_b, cv_w_out=v_cv_w_out, pl_w_in=v_pl_w_in, pl_w_grp=v_pl_w_grp, pl_scale=v_pl_scale, pl_w_out=v_pl_w_out, ml_w_in=v_ml_w_in, ml_q_norm=v_ml_q_norm, ml_kv_norm=v_ml_kv_norm, ml_w_uq=v_ml_w_uq, ml_w_ukv=v_ml_w_ukv, ml_nope_norm=v_ml_nope_norm, ml_rope_norm=v_ml_rope_norm, ml_w_out=v_ml_w_out, ch_w_in=v_ch_w_in, ch_ln_g=v_ch_ln_g, ch_ln_b=v_ch_ln_b, ch_w_s=v_ch_w_s, ch_b_s=v_ch_b_s, ch_w_out=v_ch_w_out)

    batch, lat_len, d = x.shape
    mx, my, mc = _my_place()
    chip = 2 * mx + my
    dev = 2 * chip + mc
    core = jnp.reshape(mc, (1,)).astype(jnp.int32)
    zero_off = jnp.zeros((1,), jnp.int32)
    big_names = list(BIG)

    sw = d // N_CHIP
    small_in = [c] + [jnp.pad(W[n].reshape(-1, W[n].shape[-1]), ((0, 0), (0, sw - W[n].shape[-1])))
                      for n in SMALL_SHARDED]
    pack1, spans1 = _pack_rows(small_in, sw)
    plan = _MeshPlan(W, M, V, core)
    gather1 = _ag8_copies(pack1)
    _run_hosted("ag8_inputs", _merge_hosted([gather1, plan.hosted("ag8_inputs")]))
    plan.after("ag8_inputs")
    got1 = gather1.results[0]
    c_all = got1[:, spans1[0][0]:spans1[0][0] + spans1[0][1]].reshape(N_DEV * batch, d)
    full_small = {}
    for n, (r0, nr, _) in zip(SMALL_SHARDED, spans1[1:]):
        blk = got1[0::2, r0:r0 + nr, :W[n].shape[-1]]
        full_small[n] = blk.transpose(1, 0, 2).reshape(nr, -1)

    c_rows = jnp.concatenate([c_all, c_ctx[None], jnp.zeros((MOD_ROWS - CTX_ROW - 1, d), F32)], axis=0)
    nmod = w_mod.shape[2]
    b_shard = lax.dynamic_slice(b_mod, (0, chip * nmod), (b_mod.shape[0], nmod))[:, None, :]
    mod_shard = _mod_fwd(c_rows, w_mod, b_shard, hosted=plan.hosted("mod_fwd"))
    plan.after("mod_fwd")
    mod_rows = mod_shard.transpose(1, 0, 2).reshape(MOD_ROWS, 1, 4 * nmod)
    got2 = _run_hosted("mod_exchange", _chip_rows_copies(mod_rows, batch, CTX_ROW))[0]
    mod_mine = got2.reshape(N_CHIP, batch + 1, 4, nmod).transpose(2, 1, 0, 3).reshape(4, batch + 1, 3 * d)
    mod_lat, mod_ctx = mod_mine[:, :batch], mod_mine[:, batch]
    mods = []
    for i in range(4):
        mods.append(tuple(
            jnp.stack([mod_lat[i, :, j * d:(j + 1) * d], jnp.broadcast_to(mod_ctx[i, j * d:(j + 1) * d], (batch, d))],
                      axis=1)[:, :, None, :] for j in range(3)))

    wk = dict(full_small)
    wk.update(norm_g=norm_g, cv_db=cv_db, cv_ln_g=cv_ln_g, cv_ln_b=cv_ln_b, ml_nope_norm=ml_nope_norm[0],
              ml_rope_norm=ml_rope_norm[0], ch_w_s=ch_w_s[0], ch_b_s=ch_b_s[0])
    plan.small = _prep_weights(wk)
    xm = jnp.concatenate([x, ctx], axis=1)
    loss_part, grad_x, dmods, g = _local_step(xm, loss_target, mods, plan, lat_len)
    g = _unprep_grads(g)

    lat_rows, ctx_rows = [], []
    for i in range(4):
        dsh, dsc, dgt = dmods[i]
        lat_rows.append(jnp.concatenate([dsh[:, 0, 0], dsc[:, 0, 0], dgt[:, 0, 0]], axis=1))
        zero = jnp.zeros((d,), F32)
        cs = [jnp.sum(t[:, 1, 0], axis=0) if ok else zero
              for t, ok in zip((dsh, dsc, dgt), (i <= 2, i <= 2, i <= 1))]
        ctx_rows.append(jnp.concatenate(cs, axis=0)[None])
    dmod_dev = jnp.concatenate(lat_rows + ctx_rows, axis=0)
    dmod_dev = jnp.pad(dmod_dev, ((0, (-dmod_dev.shape[0]) % 8), (0, 0)))
    gather3 = _ag8_column_copies(dmod_dev, nmod)
    _run_hosted("ag8_dmod", _merge_hosted([gather3, plan.hosted("ag8_dmod")]))
    plan.after("ag8_dmod")
    got3 = gather3.results[0]
    dlat = got3[:, :4 * batch].reshape(N_DEV, 4, batch, nmod).transpose(1, 0, 2, 3).reshape(4, N_DEV * batch, nmod)
    dctx_parts = got3[:, 4 * batch:4 * batch + 4].transpose(1, 0, 2)
    g_b_shard, dctx = _mod_bwd_rows(dlat, dctx_parts)
    d_rows = jnp.concatenate([dlat, dctx, jnp.zeros((4, MOD_ROWS - CTX_ROW - 1, nmod), F32)], axis=1)
    g_w_mod, dcc_part = _mod_bwd_w(c_rows.T, d_rows, w_mod, hosted=plan.hosted("mod_bwd_w"))
    plan.after("mod_bwd_w")

    wm2 = w_mod.reshape(-1, nmod)
    res_mod = _adamw("adamw_w_mod", None, g_w_mod.reshape(1, -1, nmod), wm2, M["w_mod"].reshape(-1, nmod),
                     V["w_mod"].reshape(-1, nmod), wm2.shape[0], hosted=plan.hosted("adamw_w_mod"))
    plan.after("adamw_w_mod")
    out = {"w_mod": tuple(r.reshape(w_mod.shape) for r in res_mod)}

    g_small_in = {n: g[n] for n in SMALL_SHARDED if n not in EARLY_SMALL}
    g_small_in.update(norm_g=g["norm_g"], cv_db=g["cv_db"], cv_ln_g=g["cv_ln_g"], cv_ln_b=g["cv_ln_b"],
                      ml_nope_norm=g["ml_nope_norm"], ml_rope_norm=g["ml_rope_norm"],
                      c_ctx=dcc_part.reshape(-1) * (mc == 0).astype(F32), loss=loss_part,
                      b_mod=lax.dynamic_update_slice(jnp.zeros((N_CHIP, 4, nmod), F32),
                                                     g_b_shard[None, :, 0] * (mc == 0).astype(F32), (chip, 0, 0)))
    small_names = list(g_small_in)
    pack4, spans4 = _pack_rows([g_small_in[n] for n in small_names], LANES, 128)
    gather4 = _ag8_copies(pack4)
    _run_hosted("ag8_small_grads", _merge_hosted([gather4, plan.hosted("ag8_small_grads")]))
    plan.after("ag8_small_grads")
    gs = dict(zip(small_names, _unpack_rows(_sum8(gather4.results[0]), spans4)))
    loss = gs["loss"][0, 0]
    gs.update(plan.early)
    gs["b_mod"] = gs["b_mod"].transpose(1, 0, 2).reshape(4, N_CHIP * nmod)
    for n in SMALL_SHARDED:
        wd = W[n].shape[-1]
        gs[n] = lax.dynamic_slice_in_dim(gs[n], chip * wd, wd, axis=1)
    upd_names = SMALL_REPLICATED + SMALL_SHARDED
    pw, spans_u = _pack_rows([W[n] for n in upd_names], LANES, 128)
    pm, _ = _pack_rows([M[n] for n in upd_names], LANES, 128)
    pv, _ = _pack_rows([V[n] for n in upd_names], LANES, 128)
    pg, _ = _pack_rows([gs[n].reshape(W[n].shape) for n in upd_names], LANES, 128)
    res_small = _adamw("adamw_small", None, pg[None], pw, pm, pv, pw.shape[0], hosted=plan.hosted("adamw_small"))
    plan.after("adamw_small")
    for n, vals in zip(upd_names, zip(*[_unpack_rows(r, spans_u) for r in res_small])):
        out[n] = vals
    out.update(plan.finish())

    outs = [loss, grad_x]
    for j in range(4):
        outs.extend(out[n][j] for n in WEIGHTS)
    return tuple(outs)
```

```python
import functools
import math

import jax
import jax.numpy as jnp
from jax import lax
from jax.experimental import pallas as pl
from jax.experimental.pallas import tpu as pltpu

F32 = jnp.float32
BF16 = jnp.bfloat16
ACT = jnp.float32
MESH = pl.DeviceIdType.MESH

EPS = 1e-6
GRID_W = 64
CONV_WIDTH = 31
CONV_HALF = CONV_WIDTH // 2
CONV_PAD = 16
POOL_WINDOWS = (2, 4, 8, 16)
POOL_HALF = max(POOL_WINDOWS) // 2
HEADS = 8
NOPE = 128
ROPE = 64
HEAD_W = 256
VDIM = 128
KV_RANK = 256
Q_RANK = 384
ATT_SCALE = (NOPE + ROPE) ** -0.5
LN2 = math.log(2.0)
Q_PRESCALE = ATT_SCALE / LN2
ROPE_THETA = 10000.0
CHUNK = 128
CHUNK_GROUPS = 8
LANES = 128
TM = 256
TM_LATENT = 512
TQ = 1024
TQ_BWD = 2048
ATT_RQ = 128
ATT_KC = 256
VMEM_LIMIT = 60 * 1024 * 1024

ADAM_LR = 0.001
ADAM_B1 = 0.9
ADAM_B2 = 0.999
ADAM_EPS = 1e-08
ADAM_WD = 0.01
ADAM_STEP = 10


def _dot(a, b):
    return jnp.dot(a.astype(BF16), b.astype(BF16), preferred_element_type=F32)


def _dot_nt(a, b):
    return lax.dot_general(a.astype(BF16), b.astype(BF16), (((1,), (1,)), ((), ())), preferred_element_type=F32)


def _dot_tn(a, b):
    return lax.dot_general(a.astype(BF16), b.astype(BF16), (((0,), (0,)), ((), ())), preferred_element_type=F32)


@jax.custom_vjp
def _mm(a, w):
    return _dot(a, w)


def _mm_fwd(a, w):
    return _dot(a, w), (a, w)


def _mm_bwd(res, ct):
    a, w = res
    return _dot_nt(ct, w), _dot_tn(a, ct)


_mm.defvjp(_mm_fwd, _mm_bwd)


def _swap16_impl(x):
    n = x.shape[-1]
    ax = x.ndim - 1
    lane = lax.broadcasted_iota(jnp.int32, x.shape, ax)
    up = pltpu.roll(x, n - 16, ax)
    dn = pltpu.roll(x, 16, ax)
    return jnp.where((lane % 32) < 16, up, dn)


@jax.custom_vjp
def _swap16(x):
    return _swap16_impl(x)


_swap16.defvjp(lambda x: (_swap16_impl(x), None), lambda _, ct: (_swap16_impl(ct),))


def _rms(x, g, n=None):
    n = x.shape[-1] if n is None else n
    return x * lax.rsqrt(jnp.sum(x * x, axis=-1, keepdims=True) * (1.0 / n) + EPS) * g


def _layernorm(x, g, b):
    mu = jnp.mean(x, axis=-1, keepdims=True)
    xc = x - mu
    var = jnp.mean(xc * xc, axis=-1, keepdims=True)
    return xc * lax.rsqrt(var + EPS) * g + b


def _silu(x):
    return x * jax.nn.sigmoid(x)


def _rope(x, cos, sin):
    return x * cos + _swap16(x) * sin


ANY = pl.BlockSpec(memory_space=pl.ANY)


class _Hosted:
    def __init__(self, arrays, out_shapes, sems, start, wait, aliases=None):
        self.arrays, self.out_shapes, self.sems = list(arrays), list(out_shapes), list(sems)
        self.start, self.wait, self.aliases = start, wait, dict(aliases or {})
        self.results = None


def _merge_hosted(parts):
    parts = [p for p in parts if p is not None]
    if not parts:
        return None
    if len(parts) == 1:
        return parts[0]
    offs, a0, o0, s0 = [], 0, 0, 0
    for p in parts:
        offs.append((a0, o0, s0))
        a0, o0, s0 = a0 + len(p.arrays), o0 + len(p.out_shapes), s0 + len(p.sems)

    def run(which):
        def f(ins, outs, sems):
            for p, (a, o, s) in zip(parts, offs):
                getattr(p, which)(ins[a:a + len(p.arrays)], outs[o:o + len(p.out_shapes)], sems[s:s + len(p.sems)])
        return f

    aliases = {}
    for p, (a, o, _) in zip(parts, offs):
        aliases.update({a + i: o + j for i, j in p.aliases.items()})
    merged = _Hosted(sum((p.arrays for p in parts), []), sum((p.out_shapes for p in parts), []),
                     sum((p.sems for p in parts), []), run("start"), run("wait"), aliases)
    merged.parts, merged.offs = parts, offs
    return merged


def _deliver(hosted, results):
    hosted.results = list(results)
    for p, (_, o, _) in zip(getattr(hosted, "parts", []), getattr(hosted, "offs", [])):
        _deliver(p, results[o:o + len(p.out_shapes)])


def _pcall(body, *, name, grid, in_specs, out_specs, out_shape, args, hosted=None, vmem_limit=True, scratch=()):
    n_in, n_out, n_scr = len(args), len(out_shape), len(scratch)
    kwargs = dict(scratch_shapes=list(scratch)) if scratch else {}
    if hosted is not None:
        nhi, nho, inner = len(hosted.arrays), len(hosted.out_shapes), body

        def body(*refs):
            ins, hin = refs[:n_in], refs[n_in:n_in + nhi]
            outs, hout = refs[n_in + nhi:n_in + nhi + n_out], refs[n_in + nhi + n_out:n_in + nhi + n_out + nho]
            own = refs[n_in + nhi + n_out + nho:n_in + nhi + n_out + nho + n_scr]
            sems = refs[n_in + nhi + n_out + nho + n_scr:]
            first, last = None, None
            for k, g in enumerate(grid):
                f, l = pl.program_id(k) == 0, pl.program_id(k) == g - 1
                first = f if first is None else jnp.logical_and(first, f)
                last = l if last is None else jnp.logical_and(last, l)

            @pl.when(first)
            def _():
                hosted.start(hin, hout, sems)

            inner(*ins, *outs, *own)

            @pl.when(last)
            def _():
                hosted.wait(hin, hout, sems)

        in_specs = list(in_specs) + [ANY] * nhi
        out_specs = list(out_specs) + [ANY] * nho
        out_shape = list(out_shape) + hosted.out_shapes
        args = list(args) + hosted.arrays
        kwargs = dict(scratch_shapes=list(scratch) + hosted.sems,
                      input_output_aliases={n_in + i: n_out + j for i, j in hosted.aliases.items()})
    params = dict(dimension_semantics=("arbitrary",) * len(grid))
    if vmem_limit:
        params["vmem_limit_bytes"] = VMEM_LIMIT
    res = pl.pallas_call(body, name=name, grid=grid, in_specs=list(in_specs), out_specs=list(out_specs),
                         out_shape=list(out_shape), compiler_params=pltpu.CompilerParams(**params), **kwargs)(*args)
    if hosted is not None:
        _deliver(hosted, res[n_out:])
    return list(res[:n_out])


def _run_hosted(name, hosted):
    nhi, nho = len(hosted.arrays), len(hosted.out_shapes)

    def body(*refs):
        ins, outs, sems = refs[:nhi], refs[nhi:nhi + nho], refs[nhi + nho:]
        hosted.start(ins, outs, sems)
        hosted.wait(ins, outs, sems)

    res = pl.pallas_call(body, name=name, in_specs=[ANY] * nhi, out_specs=[ANY] * nho, out_shape=hosted.out_shapes,
                         scratch_shapes=hosted.sems, input_output_aliases=hosted.aliases)(*hosted.arrays)
    _deliver(hosted, res)
    return list(res)


def _const_spec(shape, single=False):
    nd = len(shape)
    if single:
        return pl.BlockSpec(shape, lambda b, i: (0,) * nd, pipeline_mode=pl.Buffered(1))
    return pl.BlockSpec(shape, lambda b, i: (0,) * nd)


def _tile_spec(arr, n_lat_tiles, lat_only=False, tm=TM):
    bt, _, cw = arr.shape
    if lat_only:
        return pl.BlockSpec((1, tm, cw), lambda b, i: (b if bt > 1 else 0, jnp.minimum(i, n_lat_tiles - 1), 0))
    return pl.BlockSpec((1, tm, cw), lambda b, i: (b if bt > 1 else 0, i, 0))


def _eparam_spec(arr, n_lat_tiles):
    cw = arr.shape[-1]
    return pl.BlockSpec((1, 1, 1, cw), lambda b, i: (b, (i >= n_lat_tiles).astype(jnp.int32), 0, 0))


def _stage_fwd(name, *, pre, post, wsel, splits, tiles, eparams, sparams, weights, out_widths, out_dtypes,
               batch, n_tiles, n_lat_tiles, hosted=None, tm=TM):
    nt, ne, ns, nw = len(tiles), len(eparams), len(sparams), len(weights)

    def body(*refs):
        t_refs = refs[:nt]
        e_refs = refs[nt:nt + ne]
        s_refs = refs[nt + ne:nt + ne + ns]
        w_refs = refs[nt + ne + ns:nt + ne + ns + nw]
        o_refs = refs[nt + ne + ns + nw:]
        tv = [r[0].astype(F32) for r in t_refs]
        ev = [r[0, 0] for r in e_refs]
        sv = [r[...] for r in s_refs]
        a = pre(tv, ev, sv)
        z = [_dot(a[wsel[j]], w_refs[j][...]) for j in range(nw)]
        if post is None:
            outs = [z[j][:, s:s + w] for (j, s, w) in splits]
        else:
            outs = post(z, tv, ev, sv)
        for o_ref, o in zip(o_refs, outs):
            o_ref[0] = o.astype(o_ref.dtype)

    in_specs = ([_tile_spec(t, n_lat_tiles, tm=tm) for t in tiles] + [_eparam_spec(e, n_lat_tiles) for e in eparams]
                + [_const_spec(s.shape) for s in sparams] + [_const_spec(w.shape, single=True) for w in weights])
    out_shape = [jax.ShapeDtypeStruct((batch, n_tiles * tm, w), dt) for w, dt in zip(out_widths, out_dtypes)]
    out_specs = [pl.BlockSpec((1, tm, w), lambda b, i: (b, i, 0)) for w in out_widths]
    return _pcall(body, name=name, grid=(batch, n_tiles), in_specs=in_specs, out_specs=out_specs,
                  out_shape=out_shape, args=[*tiles, *eparams, *sparams, *weights], hosted=hosted)


def _stage_bwd(name, *, pre, post, wsel, splits, tiles, tile_diff, eparams, sparams, weights, cots, cot_lat_only,
               batch, n_tiles, n_lat_tiles, add=None, add_lat_only=False, hosted=None, w_col_stack=None,
               dt_lat_only=False, tm=TM):
    nt, ne, ns, nw, nc = len(tiles), len(eparams), len(sparams), len(weights), len(cots)
    diff_idx = [k for k in range(nt) if tile_diff[k]]
    nd = len(diff_idx)
    has_add = add is not None
    w_col_stack = w_col_stack or [None] * nw

    def body(*refs):
        pos = 0
        t_refs = refs[pos:pos + nt]; pos += nt
        e_refs = refs[pos:pos + ne]; pos += ne
        s_refs = refs[pos:pos + ns]; pos += ns
        w_refs = refs[pos:pos + nw]; pos += nw
        c_refs = refs[pos:pos + nc]; pos += nc
        if has_add:
            add_ref = refs[pos]; pos += 1
        dt_refs = refs[pos:pos + nd]; pos += nd
        de_refs = refs[pos:pos + ne]; pos += ne
        ds_refs = refs[pos:pos + ns]; pos += ns
        dw_refs = refs[pos:pos + nw]; pos += nw

        b = pl.program_id(0)
        i = pl.program_id(1)
        is_lat = i < n_lat_tiles
        tv = [r[0].astype(F32) for r in t_refs]
        ev = tuple(r[0, 0] for r in e_refs)
        sv = tuple(r[...] for r in s_refs)
        dv0 = tuple(tv[k] for k in diff_idx)

        def merge(dv):
            full = list(tv)
            for k, v in zip(diff_idx, dv):
                full[k] = v
            return full

        def pre_f(dv, ev_, sv_):
            return tuple(pre(merge(dv), list(ev_), list(sv_)))

        a, vjp_pre = jax.vjp(pre_f, dv0, ev, sv)
        cv = []
        for c_ref, lat in zip(c_refs, cot_lat_only):
            c = c_ref[0].astype(F32)
            cv.append(jnp.where(is_lat, c, 0.0) if lat else c)
        if post is None:
            dz = None
            dt2 = de2 = ds2 = None
        else:
            z = tuple(_dot(a[wsel[j]], w_refs[j][...]) for j in range(nw))

            def post_f(z_, dv, ev_, sv_):
                return tuple(post(list(z_), merge(dv), list(ev_), list(sv_)))

            _, vjp_post = jax.vjp(post_f, z, dv0, ev, sv)
            dz, dt2, de2, ds2 = vjp_post(tuple(cv))
        da = [None] * len(a)
        dws = []
        w_pieces = []
        for j in range(nw):
            if dz is None:
                for k, (jj, s, wd) in enumerate(splits):
                    if jj != j:
                        continue
                    g = _dot_nt(cv[k], w_refs[j][:, s:s + wd])
                    da[wsel[j]] = g if da[wsel[j]] is None else da[wsel[j]] + g
                    dwk = _dot_tn(a[wsel[j]], cv[k])
                    if w_col_stack[j]:
                        cw = w_refs[j].shape[1] // w_col_stack[j]
                        for c in range(w_col_stack[j]):
                            lo, hi = max(s, c * cw), min(s + wd, (c + 1) * cw)
                            if lo < hi:
                                w_pieces.append((dw_refs[j].at[c, :, pl.ds(lo - c * cw, hi - lo)],
                                                 dwk[:, lo - s:hi - s]))
                    else:
                        w_pieces.append((dw_refs[j].at[:, pl.ds(s, wd)], dwk))
                dws.append(None)
                continue
            g = _dot_nt(dz[j], w_refs[j][...])
            da[wsel[j]] = g if da[wsel[j]] is None else da[wsel[j]] + g
            dws.append(_dot_tn(a[wsel[j]], dz[j]))
        da = tuple(jnp.zeros_like(a[k]) if da[k] is None else da[k] for k in range(len(a)))
        dt1, de1, ds1 = vjp_pre(da)

        def plus(u, v):
            return u if v is None else u + v

        for k in range(nd):
            val = plus(dt1[k], None if dt2 is None else dt2[k])
            if has_add and k == 0:
                addv = add_ref[0].astype(F32)
                val = val + (jnp.where(is_lat, addv, 0.0) if add_lat_only else addv)
            if dt_lat_only:
                @pl.when(is_lat)
                def _(k=k, val=val):
                    dt_refs[k][0] = val.astype(dt_refs[k].dtype)
            else:
                dt_refs[k][0] = val.astype(dt_refs[k].dtype)

        seg_first = jnp.logical_or(i == 0, i == n_lat_tiles)
        for k in range(ne):
            val = plus(de1[k], None if de2 is None else de2[k])

            @pl.when(seg_first)
            def _(k=k, val=val):
                de_refs[k][0, 0] = val

            @pl.when(jnp.logical_not(seg_first))
            def _(k=k, val=val):
                de_refs[k][0, 0] += val

        first = jnp.logical_and(b == 0, i == 0)
        acc = [(ds_refs[k], plus(ds1[k], None if ds2 is None else ds2[k])) for k in range(ns)]
        acc += w_pieces
        for j in range(nw):
            if dws[j] is None:
                continue
            if w_col_stack[j]:
                cw = dws[j].shape[1] // w_col_stack[j]
                acc += [(dw_refs[j].at[c], dws[j][:, c * cw:(c + 1) * cw]) for c in range(w_col_stack[j])]
            else:
                acc.append((dw_refs[j], dws[j]))
        for ref, val in acc:
            @pl.when(first)
            def _(ref=ref, val=val):
                ref[...] = val

            @pl.when(jnp.logical_not(first))
            def _(ref=ref, val=val):
                ref[...] += val

    in_specs = ([_tile_spec(t, n_lat_tiles, tm=tm) for t in tiles] + [_eparam_spec(e, n_lat_tiles) for e in eparams]
                + [_const_spec(s.shape) for s in sparams] + [_const_spec(w.shape, single=True) for w in weights]
                + [_tile_spec(c, n_lat_tiles, lat, tm) for c, lat in zip(cots, cot_lat_only)])
    args = [*tiles, *eparams, *sparams, *weights, *cots]
    if has_add:
        in_specs.append(_tile_spec(add, n_lat_tiles, add_lat_only, tm))
        args.append(add)
    dt_tiles = n_lat_tiles if dt_lat_only else n_tiles
    out_shape = [jax.ShapeDtypeStruct((batch, dt_tiles * tm, tiles[k].shape[-1]), F32) for k in diff_idx]
    out_specs = [pl.BlockSpec((1, tm, tiles[k].shape[-1]), lambda b, i: (b, jnp.minimum(i, dt_tiles - 1), 0))
                 for k in diff_idx]
    out_shape += [jax.ShapeDtypeStruct(e.shape, F32) for e in eparams]
    out_specs += [_eparam_spec(e, n_lat_tiles) for e in eparams]
    out_shape += [jax.ShapeDtypeStruct(s.shape, F32) for s in sparams]
    out_specs += [_const_spec(s.shape) for s in sparams]
    dw_shapes = [(n, w.shape[0], w.shape[1] // n) if n else w.shape for w, n in zip(weights, w_col_stack)]
    out_shape += [jax.ShapeDtypeStruct(s, F32) for s in dw_shapes]
    out_specs += [_const_spec(s, single=True) for s in dw_shapes]
    res = _pcall(body, name=name, grid=(batch, n_tiles), in_specs=in_specs, out_specs=out_specs,
                 out_shape=out_shape, args=args, hosted=hosted)
    return res[:nd], res[nd:nd + ne], res[nd + ne:nd + ne + ns], res[nd + ne + ns:]


def _pre_adaln(tv, ev, sv):
    x = tv[0]
    sh, sc = ev[0], ev[1]
    return [_rms(x, sv[0]) * (1.0 + sc) + sh]


def _post_residual(x_index):
    def post(z, tv, ev, sv):
        return [tv[x_index] + ev[-1] * z[0]]
    return post


def _pre_conv_out(tv, ev, sv):
    c1, gg = tv[0], tv[1]
    return [_silu(_layernorm(c1, sv[0], sv[1])) * _silu(gg)]


def _pre_pool_out(tv, ev, sv):
    pooled, gg = tv[0], tv[1]
    w_grp, scale = sv[0], sv[1]
    gw = w_grp.shape[-1]
    y = jnp.concatenate([_mm(pooled[:, k * gw:(k + 1) * gw], w_grp[k]) for k in range(w_grp.shape[0])], axis=1)
    return [y * scale * _silu(gg)]


def _pre_rms_only(tv, ev, sv):
    return [_rms(tv[0], sv[0])]


def _post_mla_keys(z, tv, ev, sv):
    krp, cos, sin = tv[1], tv[2], tv[3]
    nope_g, rope_g = sv[1], sv[2]
    kv = z[0]
    kr = _rope(_rms(krp, rope_g, ROPE), cos, sin)
    ks, vs = [], []
    for h in range(HEADS):
        ks.append(_rms(kv[:, h * 2 * NOPE:h * 2 * NOPE + NOPE], nope_g))
        ks.append(kr)
        vs.append(kv[:, h * 2 * NOPE + NOPE:(h + 1) * 2 * NOPE])
    return [jnp.concatenate(ks, axis=1), jnp.concatenate(vs, axis=1)]


def _post_mla_queries(z, tv, ev, sv):
    cos, sin = tv[1], tv[2]
    nope_g, rope_g = sv[1], sv[2]
    q = z[0]
    qs = []
    for h in range(HEADS):
        qs.append(_rms(q[:, h * HEAD_W:h * HEAD_W + NOPE], nope_g))
        qs.append(_rope(_rms(q[:, h * HEAD_W + NOPE:(h + 1) * HEAD_W], rope_g, ROPE), cos, sin))
    return [jnp.concatenate(qs, axis=1) * Q_PRESCALE]


def _pre_mla_out(tv, ev, sv):
    return [tv[0] * _silu(tv[1])]


def _pre_chunk_out(tv, ev, sv):
    u, v, gg = tv[0], tv[1], tv[2]
    ln_g, ln_b, w_s, b_s = sv
    vn = _layernorm(v, ln_g, ln_b)
    rows = []
    for n in range(vn.shape[0] // CHUNK):
        blk = vn[n * CHUNK:(n + 1) * CHUNK]
        cols = [_mm(w_s[g], blk[:, g * LANES:(g + 1) * LANES]) + b_s[:, g:g + 1] for g in range(CHUNK_GROUPS)]
        rows.append(jnp.concatenate(cols, axis=1))
    s = jnp.concatenate(rows, axis=0)
    return [u * s * _silu(gg)]


def _segments(lat_len, tot_len):
    segs = [(0, lat_len)]
    if tot_len > lat_len:
        segs.append((lat_len, tot_len - lat_len))
    return segs


def _pad_rows(x):
    z = jnp.zeros((CONV_PAD, x.shape[1]), x.dtype)
    return jnp.concatenate([z, x, z], axis=0)


def _shifted(xp, j):
    n = xp.shape[0] - 2 * CONV_PAD
    if j != 0:
        xp = pltpu.roll(xp, (-j) % xp.shape[0], 0)
    return xp[CONV_PAD:CONV_PAD + n]


def _conv_fwd(a, bgate, dw, db, lat_len, hosted=None):
    batch, tot, e = a.shape
    segs = _segments(lat_len, tot)

    def body(a_ref, b_ref, dw_ref, db_ref, o_ref):
        w = dw_ref[...]
        for (s0, n) in segs:
            y = a_ref[0, s0:s0 + n, :].astype(F32) * jax.nn.sigmoid(b_ref[0, s0:s0 + n, :].astype(F32))
            yp = _pad_rows(y)
            acc = jnp.zeros_like(y) + db_ref[...]
            for k in range(CONV_WIDTH):
                acc = acc + _shifted(yp, k - CONV_HALF) * w[k:k + 1, :]
            o_ref[0, s0:s0 + n, :] = acc.astype(o_ref.dtype)

    blk = pl.BlockSpec((1, tot, LANES), lambda b, cb: (b, 0, cb))
    return _pcall(
        body, name="conv_fwd", grid=(batch, e // LANES),
        in_specs=[blk, blk, pl.BlockSpec((CONV_WIDTH, LANES), lambda b, cb: (0, cb)),
                  pl.BlockSpec((1, LANES), lambda b, cb: (0, cb))],
        out_specs=[blk], out_shape=[jax.ShapeDtypeStruct(a.shape, ACT)], args=[a, bgate, dw, db], hosted=hosted)[0]


def _conv_bwd(a, bgate, dw, dc1, lat_len, hosted=None):
    batch, tot, e = a.shape
    segs = _segments(lat_len, tot)

    def body(a_ref, b_ref, dw_ref, dc_ref, da_ref, dg_ref, ddw_ref, ddb_ref):
        b = pl.program_id(1)
        w = dw_ref[...]
        ddw_rows = [None] * CONV_WIDTH
        ddb = None
        for (s0, n) in segs:
            av = a_ref[0, s0:s0 + n, :].astype(F32)
            sg = jax.nn.sigmoid(b_ref[0, s0:s0 + n, :].astype(F32))
            y = av * sg
            dc = dc_ref[0, s0:s0 + n, :]
            yp, dcp = _pad_rows(y), _pad_rows(dc)
            dy = jnp.zeros_like(y)
            for k in range(CONV_WIDTH):
                j = k - CONV_HALF
                dy = dy + _shifted(dcp, -j) * w[k:k + 1, :]
                r = jnp.sum(dc * _shifted(yp, j), axis=0, keepdims=True)
                ddw_rows[k] = r if ddw_rows[k] is None else ddw_rows[k] + r
            r = jnp.sum(dc, axis=0, keepdims=True)
            ddb = r if ddb is None else ddb + r
            da_ref[0, s0:s0 + n, :] = dy * sg
            dg_ref[0, s0:s0 + n, :] = dy * av * sg * (1.0 - sg)

        @pl.when(b == 0)
        def _():
            ddw_ref[...] = jnp.zeros_like(ddw_ref)
            ddb_ref[...] = jnp.zeros_like(ddb_ref)

        for k in range(CONV_WIDTH):
            ddw_ref[k:k + 1, :] += ddw_rows[k]
        ddb_ref[...] += ddb

    blk = pl.BlockSpec((1, tot, LANES), lambda cb, b: (b, 0, cb))
    wspec = pl.BlockSpec((CONV_WIDTH, LANES), lambda cb, b: (0, cb))
    bspec = pl.BlockSpec((1, LANES), lambda cb, b: (0, cb))
    return _pcall(
        body, name="conv_bwd", grid=(e // LANES, batch),
        in_specs=[blk, blk, wspec, blk],
        out_specs=[blk, blk, wspec, bspec],
        out_shape=[jax.ShapeDtypeStruct(a.shape, F32), jax.ShapeDtypeStruct(a.shape, F32),
                   jax.ShapeDtypeStruct((CONV_WIDTH, e), F32), jax.ShapeDtypeStruct((1, e), F32)],
        args=[a, bgate, dw, dc1], hosted=hosted)


def _pool_counts(n, half, shape):
    t = lax.broadcasted_iota(jnp.int32, shape, 0)
    cnt = jnp.minimum(t + half, n) - jnp.maximum(t - half, 0)
    return cnt.astype(F32)


def _per_group(fn):
    for k, window in enumerate(POOL_WINDOWS):
        @pl.when(pl.program_id(1) == k)
        def _(window=window):
            fn(window // 2)


def _pool_fwd(v, lat_len, hosted=None):
    batch, tot, e = v.shape
    gw = e // len(POOL_WINDOWS)
    segs = _segments(lat_len, tot)

    def body(v_ref, o_ref):
        def group(half):
            for (s0, n) in segs:
                x = v_ref[0, s0:s0 + n, :]
                xp = _pad_rows(x)
                acc = _shifted(xp, -half)
                for j in range(-half + 1, half):
                    acc = acc + _shifted(xp, j)
                o_ref[0, s0:s0 + n, :] = (acc / _pool_counts(n, half, x.shape) - x).astype(o_ref.dtype)

        _per_group(group)

    blk = pl.BlockSpec((1, tot, gw), lambda b, g: (b, 0, g))
    return _pcall(body, name="pool_fwd", grid=(batch, len(POOL_WINDOWS)), in_specs=[blk], out_specs=[blk],
                  out_shape=[jax.ShapeDtypeStruct(v.shape, ACT)], args=[v], hosted=hosted)[0]


def _pool_bwd(dp, lat_len):
    batch, tot, e = dp.shape
    gw = e // len(POOL_WINDOWS)
    segs = _segments(lat_len, tot)

    def body(d_ref, o_ref):
        def group(half):
            for (s0, n) in segs:
                d = d_ref[0, s0:s0 + n, :]
                dnp = _pad_rows(d / _pool_counts(n, half, d.shape))
                acc = _shifted(dnp, half)
                for j in range(-half + 1, half):
                    acc = acc + _shifted(dnp, -j)
                o_ref[0, s0:s0 + n, :] = acc - d

        _per_group(group)

    blk = pl.BlockSpec((1, tot, gw), lambda b, g: (b, 0, g))
    return pl.pallas_call(
        body, name="pool_bwd", grid=(batch, len(POOL_WINDOWS)), in_specs=[blk], out_specs=blk,
        out_shape=jax.ShapeDtypeStruct(dp.shape, F32),
        compiler_params=pltpu.CompilerParams(dimension_semantics=("arbitrary", "arbitrary"),
                                             vmem_limit_bytes=VMEM_LIMIT),
    )(dp)


def _attn_fwd(q, k, v, hosted=None):
    batch, lq, _ = q.shape
    tk = k.shape[1]
    tq = min(TQ, lq)

    def body(q_ref, k_ref, v_ref, o_ref, lse_ref):
        s2 = _dot_nt(q_ref[0], k_ref[0])
        m2 = jnp.max(s2, axis=-1, keepdims=True)
        e = jnp.exp2(s2 - m2)
        l = jnp.sum(e, axis=-1, keepdims=True)
        o_ref[0] = (_dot(e, v_ref[0]) / l).astype(o_ref.dtype)
        lse_ref[0, 0] = m2 + jnp.log2(l)

    return _pcall(
        body, name="attn_fwd", grid=(batch, HEADS, lq // tq),
        in_specs=[pl.BlockSpec((1, tq, HEAD_W), lambda b, h, i: (b, i, h)),
                  pl.BlockSpec((1, tk, HEAD_W), lambda b, h, i: (b, 0, h)),
                  pl.BlockSpec((1, tk, VDIM), lambda b, h, i: (b, 0, h))],
        out_specs=[pl.BlockSpec((1, tq, VDIM), lambda b, h, i: (b, i, h)),
                   pl.BlockSpec((1, 1, tq, 1), lambda b, h, i: (b, h, i, 0))],
        out_shape=[jax.ShapeDtypeStruct((batch, lq, HEADS * VDIM), ACT),
                   jax.ShapeDtypeStruct((batch, HEADS, lq, 1), F32)], args=[q, k, v], hosted=hosted)


def _attn_bwd(q, k, v, o, lse, do, hosted=None):
    batch, lq, _ = q.shape
    tk = k.shape[1]
    tq = min(TQ_BWD, lq)

    def body(q_ref, k_ref, v_ref, o_ref, lse_ref, do_ref, dq_ref, dk_ref, dv_ref, p_scr, ds_scr):
        i = pl.program_id(2)
        nr = tq // ATT_RQ
        rows = [slice(r * ATT_RQ, (r + 1) * ATT_RQ) for r in range(nr)]
        qv = [q_ref[0, rw, :] for rw in rows]
        dob = [do_ref[0, rw, :].astype(BF16) for rw in rows]
        row_lse = [lse_ref[0, 0, rw, :] for rw in rows]
        delta = [jnp.sum(do_ref[0, rw, :] * o_ref[0, rw, :], axis=-1, keepdims=True) for rw in rows]
        for c in range(tk // ATT_KC):
            keys = slice(c * ATT_KC, (c + 1) * ATT_KC)
            kc, vc = k_ref[0, keys, :], v_ref[0, keys, :]
            for r in range(nr):
                p = jnp.exp2(_dot_nt(qv[r], kc) - row_lse[r])
                dp = _dot_nt(dob[r], vc)
                p_scr[rows[r], keys] = p.astype(BF16)
                ds_scr[rows[r], keys] = (p * (dp - delta[r]) * LN2).astype(BF16)
        dq_ref[0] = _dot(ds_scr[...], k_ref[0])
        dk = _dot_tn(ds_scr[...], q_ref[0])
        dv = _dot_tn(p_scr[...], do_ref[0])

        @pl.when(i == 0)
        def _():
            dk_ref[0] = dk
            dv_ref[0] = dv

        @pl.when(i != 0)
        def _():
            dk_ref[0] += dk
            dv_ref[0] += dv

    return _pcall(
        body, name="attn_bwd", grid=(batch, HEADS, lq // tq),
        in_specs=[pl.BlockSpec((1, tq, HEAD_W), lambda b, h, i: (b, i, h)),
                  pl.BlockSpec((1, tk, HEAD_W), lambda b, h, i: (b, 0, h)),
                  pl.BlockSpec((1, tk, VDIM), lambda b, h, i: (b, 0, h)),
                  pl.BlockSpec((1, tq, VDIM), lambda b, h, i: (b, i, h)),
                  pl.BlockSpec((1, 1, tq, 1), lambda b, h, i: (b, h, i, 0)),
                  pl.BlockSpec((1, tq, VDIM), lambda b, h, i: (b, i, h))],
        out_specs=[pl.BlockSpec((1, tq, HEAD_W), lambda b, h, i: (b, i, h)),
                   pl.BlockSpec((1, tk, HEAD_W), lambda b, h, i: (b, 0, h)),
                   pl.BlockSpec((1, tk, VDIM), lambda b, h, i: (b, 0, h))],
        out_shape=[jax.ShapeDtypeStruct(q.shape, F32), jax.ShapeDtypeStruct(k.shape, F32),
                   jax.ShapeDtypeStruct(v.shape, F32)],
        args=[q, k, v, o, lse, do], hosted=hosted,
        scratch=[pltpu.VMEM((tq, tk), BF16), pltpu.VMEM((tq, tk), BF16)])


def _loss_kernel(y, target):
    batch, lq, d = y.shape

    def body(y_ref, t_ref, l_ref, dy_ref):
        first = jnp.logical_and(pl.program_id(0) == 0, pl.program_id(1) == 0)
        err = y_ref[0] - t_ref[0]
        dy_ref[0] = err * (1.0 / d)
        part = jnp.zeros((1, LANES), F32) + jnp.sum(err * err) * (0.5 / d)

        @pl.when(first)
        def _():
            l_ref[...] = part

        @pl.when(jnp.logical_not(first))
        def _():
            l_ref[...] += part

    tm = TM_LATENT if lq % TM_LATENT == 0 else TM
    blk = pl.BlockSpec((1, tm, d), lambda b, i: (b, i, 0))
    return pl.pallas_call(
        body, name="loss_head", grid=(batch, lq // tm), in_specs=[blk, blk],
        out_specs=[pl.BlockSpec((1, LANES), lambda b, i: (0, 0)), blk],
        out_shape=[jax.ShapeDtypeStruct((1, LANES), F32), jax.ShapeDtypeStruct(y.shape, F32)],
        compiler_params=pltpu.CompilerParams(dimension_semantics=("arbitrary", "arbitrary")),
    )(y, target)


def _rope_tables(lat_len, ctx_len):
    rows = lat_len // GRID_W
    axis_dim = ROPE // 2
    freqs = ROPE_THETA ** (-jnp.arange(0, axis_dim, 2, dtype=F32) / axis_dim)
    ar = jnp.arange(rows, dtype=F32)[:, None] * freqs
    ac = jnp.arange(GRID_W, dtype=F32)[:, None] * freqs
    small = lax.optimization_barrier((jnp.cos(ar), jnp.sin(ar), jnp.cos(ac), jnp.sin(ac)))
    cr, sr = (jnp.repeat(t, GRID_W, axis=0) for t in small[:2])
    cc, sc = (jnp.tile(t, (rows, 1)) for t in small[2:])
    pad = jnp.zeros((lat_len, LANES - ROPE), F32)
    cos = jnp.concatenate([cr, cr, cc, cc, pad], axis=1)
    sin = jnp.concatenate([-sr, sr, -sc, sc, pad], axis=1)
    ident = jnp.concatenate([jnp.ones((ctx_len, ROPE), F32), jnp.zeros((ctx_len, LANES - ROPE), F32)], axis=1)
    cos = jnp.concatenate([cos, ident], axis=0)
    sin = jnp.concatenate([sin, jnp.zeros((ctx_len, LANES), F32)], axis=0)
    return cos[None], sin[None]


def _prep_weights(w):
    p = dict(w)
    kvc = KV_RANK + ROPE
    if "ml_w_in" in w:
        wi = w["ml_w_in"]
        p["ml_w_in"] = jnp.concatenate(
            [wi[:, :kvc], jnp.zeros((wi.shape[0], LANES - ROPE), wi.dtype), wi[:, kvc:]], axis=1)
    if "ml_w_uq" in w:
        uq = w["ml_w_uq"].reshape(Q_RANK, HEADS, NOPE + ROPE)
        p["ml_w_uq"] = jnp.pad(uq, ((0, 0), (0, 0), (0, HEAD_W - NOPE - ROPE))).reshape(Q_RANK, HEADS * HEAD_W)
    if "ml_rope_norm" in w:
        p["ml_rope_norm"] = jnp.pad(w["ml_rope_norm"], ((0, 0), (0, LANES - ROPE)))
    return p


def _unprep_grads(g):
    out = dict(g)
    kvc = KV_RANK + ROPE
    if "ml_w_in" in g:
        wi = g["ml_w_in"]
        out["ml_w_in"] = jnp.concatenate([wi[:, :kvc], wi[:, kvc + LANES - ROPE:]], axis=1)
    if "ml_w_uq" in g:
        uq = g["ml_w_uq"].reshape(Q_RANK, HEADS, HEAD_W)
        out["ml_w_uq"] = uq[:, :, :NOPE + ROPE].reshape(Q_RANK, HEADS * (NOPE + ROPE))
    if "ml_rope_norm" in g:
        out["ml_rope_norm"] = g["ml_rope_norm"][:, :ROPE]
    return out


LAYER_WEIGHTS = (("cv_w_in", "cv_w_out"), ("pl_w_in", "pl_w_grp", "pl_w_out"),
                 ("ml_w_in", "ml_w_uq", "ml_w_ukv", "ml_w_out"), ("ch_w_in", "ch_w_out"))


class _LocalPlan:
    def __init__(self, w):
        self.small = w
        self.grads = {}

    def weights(self, names):
        return {n: self.small[n] for n in names}

    def hosted(self, tag):
        return None

    def after(self, tag):
        pass

    def note(self, values):
        pass

    def layer_grads(self, layer, grads):
        self.grads.update(grads)


def _local_step(xm, target, mods, plan, lat_len):
    batch, tot, d = xm.shape
    e = d
    n_all, n_lat = tot // TM, lat_len // TM
    cos, sin = _rope_tables(lat_len, tot - lat_len)
    g = {}
    w = dict(plan.small)

    def hosting(tag, fn, *args, **kwargs):
        out = fn(*args, hosted=plan.hosted(tag), **kwargs)
        plan.after(tag)
        return out

    def s1_splits(widths):
        out, s = [], 0
        for wd in widths:
            out.append((0, s, wd))
            s += wd
        return out

    tml = TM_LATENT if lat_len % TM_LATENT == 0 else TM
    n_big = lat_len // tml

    def lat_tiles(n_tiles, tm):
        return n_lat if tm == TM else n_tiles

    def fwd_in(name, x, mod, gi, wname, widths, n_tiles, dtypes=None, tm=TM):
        return hosting(name, _stage_fwd, name, pre=_pre_adaln, post=None, wsel=[0], splits=s1_splits(widths),
                       tiles=[x], eparams=[mod[0], mod[1]], sparams=[w["norm_g"][gi:gi + 1]], weights=[w[wname]],
                       out_widths=widths, out_dtypes=dtypes or [ACT] * len(widths), batch=batch, n_tiles=n_tiles,
                       n_lat_tiles=lat_tiles(n_tiles, tm), tm=tm)

    def bwd_in(name, x, mod, gi, wname, widths, n_tiles, cots, lat_only, add, add_lat_only, stack=None,
               dx_lat_only=False):
        (dx,), (dsh, dsc), (dg,), (dw,) = hosting(
            name, _stage_bwd, name, pre=_pre_adaln, post=None, wsel=[0], splits=s1_splits(widths), tiles=[x],
            tile_diff=[True], eparams=[mod[0], mod[1]], sparams=[w["norm_g"][gi:gi + 1]], weights=[w[wname]],
            cots=cots, cot_lat_only=lat_only, batch=batch, n_tiles=n_tiles, n_lat_tiles=n_lat, add=add,
            add_lat_only=add_lat_only, w_col_stack=[stack], dt_lat_only=dx_lat_only)
        return dx, dsh, dsc, dg, dw

    def fwd_out(name, pre, tiles, mod, sparams, wname, n_tiles, tm=TM):
        return hosting(name, _stage_fwd, name, pre=pre, post=_post_residual(len(tiles) - 1), wsel=[0], splits=None,
                       tiles=tiles, eparams=[mod[2]], sparams=sparams, weights=[w[wname]], out_widths=[d],
                       out_dtypes=[F32], batch=batch, n_tiles=n_tiles, n_lat_tiles=lat_tiles(n_tiles, tm), tm=tm)[0]

    def bwd_out(name, pre, tiles, mod, sparams, wname, n_tiles, cot, tm=TM):
        diff = [True] * (len(tiles) - 1) + [False]
        dts, (dgt,), dss, (dw,) = hosting(
            name, _stage_bwd, name, pre=pre, post=_post_residual(len(tiles) - 1), wsel=[0], splits=None, tiles=tiles,
            tile_diff=diff, eparams=[mod[2]], sparams=sparams, weights=[w[wname]], cots=[cot], cot_lat_only=[False],
            batch=batch, n_tiles=n_tiles, n_lat_tiles=lat_tiles(n_tiles, tm), tm=tm)
        return dts, dgt, dss, dw

    w.update(plan.weights(("cv_w_in",)))
    cv_s = [w["cv_ln_g"], w["cv_ln_b"]]
    a0, b0, g0 = fwd_in("cv_in_fwd", xm, mods[0], 0, "cv_w_in", [e, e, e], n_all)
    c1 = hosting("conv_fwd", _conv_fwd, a0, b0, w["cv_dw"], w["cv_db"], lat_len)
    w.update(plan.weights(("cv_w_out",)))
    x1 = fwd_out("cv_out_fwd", _pre_conv_out, [c1, g0, xm], mods[0], cv_s, "cv_w_out", n_all)

    w.update(plan.weights(LAYER_WEIGHTS[1]))
    pl_s = [w["pl_w_grp"], w["pl_scale"]]
    v1, g1 = fwd_in("pl_in_fwd", x1, mods[1], 1, "pl_w_in", [e, e], n_all, dtypes=[F32, ACT])
    pooled = hosting("pool_fwd", _pool_fwd, v1, lat_len)
    x2 = fwd_out("pl_out_fwd", _pre_pool_out, [pooled, g1, x1], mods[1], pl_s, "pl_w_out", n_all)

    w.update(plan.weights(LAYER_WEIGHTS[2]))
    ml_widths = [KV_RANK, LANES, Q_RANK, HEADS * VDIM]
    ckv, krp, cq, g2 = fwd_in("ml_in_fwd", x2, mods[2], 2, "ml_w_in", ml_widths, n_all)
    k_s = [w["ml_kv_norm"], w["ml_nope_norm"][1:2], w["ml_rope_norm"][1:2]]
    q_s = [w["ml_q_norm"], w["ml_nope_norm"][0:1], w["ml_rope_norm"][0:1]]
    kk, vv = hosting("ml_keys_fwd", _stage_fwd, "ml_keys_fwd", pre=_pre_rms_only, post=_post_mla_keys, wsel=[0],
                     splits=None, tiles=[ckv, krp, cos, sin], eparams=[], sparams=k_s, weights=[w["ml_w_ukv"]],
                     out_widths=[HEADS * HEAD_W, HEADS * VDIM], out_dtypes=[BF16, BF16], batch=batch,
                     n_tiles=n_all, n_lat_tiles=n_lat)
    (qq,) = _stage_fwd("ml_queries_fwd", pre=_pre_rms_only, post=_post_mla_queries, wsel=[0], splits=None,
                       tiles=[cq, cos, sin], eparams=[], sparams=q_s, weights=[w["ml_w_uq"]],
                       out_widths=[HEADS * HEAD_W], out_dtypes=[BF16], batch=batch, n_tiles=n_big,
                       n_lat_tiles=n_big, tm=tml)
    att, lse = hosting("attn_fwd", _attn_fwd, qq, kk, vv)
    x3 = fwd_out("ml_out_fwd", _pre_mla_out, [att, g2, x2], mods[2], [], "ml_w_out", n_big, tm=tml)

    w.update(plan.weights(LAYER_WEIGHTS[3]))
    ch_s = [w["ch_ln_g"], w["ch_ln_b"], w["ch_w_s"], w["ch_b_s"]]
    u3, v3, g3 = fwd_in("ch_in_fwd", x3, mods[3], 3, "ch_w_in", [e, e, e], n_big, tm=tml)
    x4 = fwd_out("ch_out_fwd", _pre_chunk_out, [u3, v3, g3, x3], mods[3], ch_s, "ch_w_out", n_big, tm=tml)

    loss_part, dy = _loss_kernel(x4, target)

    dmods = [None] * 4
    dnorm = [None] * 4
    big = {}
    (du, dv, dg), dgt, (g["ch_ln_g"], g["ch_ln_b"], g["ch_w_s"], g["ch_b_s"]), big["ch_w_out"] = bwd_out(
        "ch_out_bwd", _pre_chunk_out, [u3, v3, g3, x3], mods[3], ch_s, "ch_w_out", n_big, dy, tm=tml)
    plan.note({n: g[n] for n in ("ch_ln_g", "ch_ln_b", "ch_w_s", "ch_b_s")})
    dx3, dsh, dsc, dnorm[3], big["ch_w_in"] = bwd_in("ch_in_bwd", x3, mods[3], 3, "ch_w_in", [e, e, e], n_lat,
                                                     [du, dv, dg], [False] * 3, dy, False, stack=N_CHIP)
    dmods[3] = (dsh, dsc, dgt)
    plan.layer_grads(3, big)

    big = {}
    (datt, dg), dgt, _, big["ml_w_out"] = bwd_out("ml_out_bwd", _pre_mla_out, [att, g2, x2], mods[2], [],
                                                  "ml_w_out", n_big, dx3, tm=tml)
    dq, dk, dvv = hosting("attn_bwd", _attn_bwd, qq, kk, vv, att, lse, datt)
    (dcq,), _, (g["ml_q_norm"], dnope0, drope0), (big["ml_w_uq"],) = hosting(
        "ml_queries_bwd", _stage_bwd, "ml_queries_bwd", pre=_pre_rms_only, post=_post_mla_queries, wsel=[0],
        splits=None, tiles=[cq, cos, sin], tile_diff=[True, False, False], eparams=[], sparams=q_s,
        weights=[w["ml_w_uq"]], cots=[dq], cot_lat_only=[False], batch=batch, n_tiles=n_big, n_lat_tiles=n_big,
        tm=tml)
    (dckv, dkrp), _, (g["ml_kv_norm"], dnope1, drope1), (big["ml_w_ukv"],) = hosting(
        "ml_keys_bwd", _stage_bwd, "ml_keys_bwd", pre=_pre_rms_only, post=_post_mla_keys, wsel=[0], splits=None,
        tiles=[ckv, krp, cos, sin], tile_diff=[True, True, False, False], eparams=[], sparams=k_s,
        weights=[w["ml_w_ukv"]], cots=[dk, dvv], cot_lat_only=[False, False], batch=batch, n_tiles=n_all,
        n_lat_tiles=n_lat, w_col_stack=[N_CHIP])
    g["ml_nope_norm"] = jnp.concatenate([dnope0, dnope1], axis=0)
    g["ml_rope_norm"] = jnp.concatenate([drope0, drope1], axis=0)
    dx2, dsh, dsc, dnorm[2], big["ml_w_in"] = bwd_in("ml_in_bwd", x2, mods[2], 2, "ml_w_in", ml_widths, n_all,
                                                     [dckv, dkrp, dcq, dg], [False, False, True, True], dx3, True)
    dmods[2] = (dsh, dsc, dgt)
    plan.layer_grads(2, big)

    big = {}
    (dpooled, dg), dgt, (big["pl_w_grp"], g["pl_scale"]), big["pl_w_out"] = bwd_out(
        "pl_out_bwd", _pre_pool_out, [pooled, g1, x1], mods[1], pl_s, "pl_w_out", n_all, dx2)
    dv1 = _pool_bwd(dpooled, lat_len)
    dx1, dsh, dsc, dnorm[1], big["pl_w_in"] = bwd_in("pl_in_bwd", x1, mods[1], 1, "pl_w_in", [e, e], n_all,
                                                     [dv1, dg], [False] * 2, dx2, False, stack=N_CHIP)
    dmods[1] = (dsh, dsc, dgt)
    plan.layer_grads(1, big)

    big = {}
    (dc1, dg), dgt, (g["cv_ln_g"], g["cv_ln_b"]), big["cv_w_out"] = bwd_out(
        "cv_out_bwd", _pre_conv_out, [c1, g0, xm], mods[0], cv_s, "cv_w_out", n_all, dx1)
    plan.layer_grads(0, big)
    big = {}
    da, db, g["cv_dw"], g["cv_db"] = hosting("conv_bwd", _conv_bwd, a0, b0, w["cv_dw"], dc1, lat_len)
    dx0, dsh, dsc, dnorm[0], big["cv_w_in"] = bwd_in("cv_in_bwd", xm, mods[0], 0, "cv_w_in", [e, e, e], n_all,
                                                     [da, db, dg], [False] * 3, dx1, False, stack=N_CHIP,
                                                     dx_lat_only=True)
    dmods[0] = (dsh, dsc, dgt)
    plan.layer_grads(0, big)
    g["norm_g"] = jnp.concatenate(dnorm, axis=0)
    return loss_part, dx0, dmods, g


N_DEV = 8
N_CHIP = 4
ANY = pl.BlockSpec(memory_space=pl.ANY)


def _my_place():
    return lax.axis_index("x"), lax.axis_index("y"), lax.axis_index("c")


def _flip(v, f):
    return 1 - v if f else v


def _ag8_copies(x):
    def plan(ins, outs, sems):
        mx, my, mc = _my_place()
        me = 4 * mx + 2 * my + mc
        sends, recvs = [], []
        for rel in range(1, N_DEV):
            peer = (_flip(mx, rel & 4), _flip(my, rel & 2), _flip(mc, rel & 1))
            src_dev = 4 * peer[0] + 2 * peer[1] + peer[2]
            sends.append(_remote(ins[0], outs[0].at[me], sems, rel - 1, peer))
            recvs.append(_remote(ins[0], outs[0].at[src_dev], sems, rel - 1, peer))
        return sends, recvs, [pltpu.make_async_copy(ins[0], outs[0].at[me], sems[2].at[0])]

    return _copies_hosted([x], [jax.ShapeDtypeStruct((N_DEV,) + x.shape, x.dtype)], (N_DEV - 1, N_DEV - 1, 1), plan)


def _ag8(name, x):
    return _run_hosted(name, _ag8_copies(x))[0]


def _ag8_column_copies(x, width):
    def plan(ins, outs, sems):
        mx, my, mc = _my_place()
        me = 4 * mx + 2 * my + mc
        sends, recvs = [], []
        for rel in range(1, N_DEV):
            peer = (_flip(mx, rel & 4), _flip(my, rel & 2), _flip(mc, rel & 1))
            src_dev = 4 * peer[0] + 2 * peer[1] + peer[2]
            cols = pl.ds(pl.multiple_of((2 * peer[0] + peer[1]) * width, LANES), width)
            sends.append(_remote(ins[0].at[:, cols], outs[0].at[me], sems, rel - 1, peer))
            recvs.append(_remote(ins[0].at[:, cols], outs[0].at[src_dev], sems, rel - 1, peer))
        mine = pl.ds(pl.multiple_of((2 * mx + my) * width, LANES), width)
        return sends, recvs, [pltpu.make_async_copy(ins[0].at[:, mine], outs[0].at[me], sems[2].at[0])]

    return _copies_hosted([x], [jax.ShapeDtypeStruct((N_DEV, x.shape[0], width), x.dtype)],
                          (N_DEV - 1, N_DEV - 1, 1), plan)


def _chip_rows_copies(x, rows_per_dev, shared_row):
    n_out = rows_per_dev + 1

    def plan(ins, outs, sems):
        mx, my, mc = _my_place()
        chip = 2 * mx + my
        sends, recvs = [], []

        def pieces(dev):
            return [(ins[0].at[pl.ds(dev * rows_per_dev, rows_per_dev)], slice(0, rows_per_dev)),
                    (ins[0].at[pl.ds(shared_row, 1)], slice(rows_per_dev, n_out))]

        for k, peer, pchip in _chip_peers(mx, my, mc):
            for t, (src, where) in enumerate(pieces(2 * pchip + mc)):
                sends.append(_remote(src, outs[0].at[chip, where], sems, 2 * k + t, peer))
                recvs.append(_remote(src, outs[0].at[pchip, where], sems, 2 * k + t, peer))
        locals_ = [pltpu.make_async_copy(src, outs[0].at[chip, where], sems[2].at[t])
                   for t, (src, where) in enumerate(pieces(2 * chip + mc))]
        return sends, recvs, locals_

    return _copies_hosted([x], [jax.ShapeDtypeStruct((N_CHIP, n_out) + x.shape[1:], x.dtype)], (6, 6, 2), plan)


def _chip_peers(mx, my, mc):
    out = []
    for rel in range(1, N_CHIP):
        px, py = _flip(mx, rel & 2), _flip(my, rel & 1)
        out.append((rel - 1, (px, py, mc), 2 * px + py))
    return out


def _half(mc, rows):
    return pl.ds(pl.multiple_of(mc * (rows // 2), 8), rows // 2)


def _copies_hosted(arrays, out_shapes, n_sems, plan, aliases=None):
    def start(ins, outs, sems):
        sends, _, locals_ = plan(ins, outs, sems)
        for cp in locals_ + sends:
            cp.start()

    def wait(ins, outs, sems):
        sends, recvs, locals_ = plan(ins, outs, sems)
        for cp in recvs:
            cp.wait_recv()
        for cp in sends:
            cp.wait_send()
        for cp in locals_:
            cp.wait()

    return _Hosted(arrays, out_shapes, [pltpu.SemaphoreType.DMA((k,)) for k in n_sems], start, wait, aliases)


def _remote(src, dst, sems, k, peer):
    return pltpu.make_async_remote_copy(src_ref=src, dst_ref=dst, send_sem=sems[0].at[k], recv_sem=sems[1].at[k],
                                        device_id=peer, device_id_type=MESH)


def _gather_ici(shards):
    n = len(shards)

    def plan(ins, outs, sems):
        mx, my, mc = _my_place()
        chip = 2 * mx + my
        sends, recvs, locals_ = [], [], []
        for a in range(n):
            rows = ins[a].shape[0]
            locals_.append(pltpu.make_async_copy(ins[a], outs[a].at[chip], sems[2].at[a]))
            for k, peer, pchip in _chip_peers(mx, my, mc):
                src = ins[a].at[_half(mc, rows)]
                sends.append(_remote(src, outs[a].at[chip, _half(mc, rows)], sems, 3 * a + k, peer))
                recvs.append(_remote(src, outs[a].at[pchip, _half(mc, rows)], sems, 3 * a + k, peer))
        return sends, recvs, locals_

    return _copies_hosted(shards, [jax.ShapeDtypeStruct((N_CHIP,) + s.shape, s.dtype) for s in shards],
                          (3 * n, 3 * n, n), plan)


def _sibling_fill(arrays, row_axis, chips_only_other):
    n = len(arrays)
    per = 3 if chips_only_other else 1

    def plan(ins, outs, sems):
        mx, my, mc = _my_place()
        sibling = (mx, my, 1 - mc)

        def views(a, core):
            rows = outs[a].shape[row_axis]
            if chips_only_other:
                return [outs[a].at[pchip, _half(core, rows)] for _, _, pchip in _chip_peers(mx, my, mc)]
            return [outs[a].at[_half(core, rows)]]

        sends, recvs = [], []
        for a in range(n):
            for k, v in enumerate(views(a, mc)):
                sends.append(_remote(v, v, sems, per * a + k, sibling))
            for k, v in enumerate(views(a, 1 - mc)):
                recvs.append(_remote(v, v, sems, per * a + k, sibling))
        return sends, recvs, []

    return _copies_hosted(arrays, [jax.ShapeDtypeStruct(s.shape, s.dtype) for s in arrays], (per * n, per * n), plan,
                          aliases={a: a for a in range(n)})


def _grad_swap_d2d(stacks):
    n = len(stacks)

    def plan(ins, outs, sems):
        mx, my, mc = _my_place()
        sibling = (mx, my, 1 - mc)
        sends = [_remote(ins[a].at[:, _half(1 - mc, ins[a].shape[1])], outs[a], sems, a, sibling) for a in range(n)]
        return sends, sends, []

    return _copies_hosted(stacks, [jax.ShapeDtypeStruct((N_CHIP, s.shape[1] // 2, s.shape[2]), s.dtype)
                                   for s in stacks], (n, n), plan)


def _grad_exchange_ici(parts):
    n = len(parts)

    def plan(ins, outs, sems):
        mx, my, mc = _my_place()
        chip = 2 * mx + my
        sends, recvs, locals_ = [], [], []
        for a in range(n):
            locals_.append(pltpu.make_async_copy(ins[a].at[chip], outs[a].at[chip], sems[2].at[a]))
            for k, peer, pchip in _chip_peers(mx, my, mc):
                sends.append(_remote(ins[a].at[pchip], outs[a].at[chip], sems, 3 * a + k, peer))
                recvs.append(_remote(ins[a].at[pchip], outs[a].at[pchip], sems, 3 * a + k, peer))
        return sends, recvs, locals_

    return _copies_hosted(parts, [jax.ShapeDtypeStruct(s.shape, s.dtype) for s in parts], (3 * n, 3 * n, n), plan)


def _row_block(rows, limit=256):
    for t in range(min(rows, limit), 7, -8):
        if rows % t == 0 and t % 8 == 0:
            return t
    return rows


def _grad_add_half(core, stack, received):
    _, rows, cw = stack.shape
    rh = rows // 2
    tr = _row_block(rh)

    def body(s_ref, a_ref, b_ref, o_ref):
        o_ref[...] = (a_ref[...] + b_ref[...]).astype(o_ref.dtype)

    grid_spec = pltpu.PrefetchScalarGridSpec(
        num_scalar_prefetch=1, grid=(rh // tr,),
        in_specs=[pl.BlockSpec((N_CHIP, tr, cw), lambda i, s: (0, s[0] * (rh // tr) + i, 0)),
                  pl.BlockSpec((N_CHIP, tr, cw), lambda i, s: (0, i, 0))],
        out_specs=pl.BlockSpec((N_CHIP, tr, cw), lambda i, s: (0, i, 0)))
    return pl.pallas_call(
        body, name="grad_add_half", grid_spec=grid_spec, out_shape=jax.ShapeDtypeStruct(received.shape, BF16),
        compiler_params=pltpu.CompilerParams(dimension_semantics=("arbitrary",), vmem_limit_bytes=VMEM_LIMIT),
    )(core, stack, received)


def _adamw(name, row_off, parts, w, m, v, rows, hosted=None):
    n, _, cw = parts.shape
    tr = _row_block(rows, 256)

    def update(p_ref, w_ref, m_ref, v_ref, g_ref, d_ref, nm_ref, nv_ref):
        g = p_ref[0].astype(F32)
        for k in range(1, n):
            g = g + p_ref[k].astype(F32)
        nm = ADAM_B1 * m_ref[...] + (1.0 - ADAM_B1) * g
        nv = ADAM_B2 * v_ref[...] + (1.0 - ADAM_B2) * (g * g)
        m_hat = nm / (1.0 - ADAM_B1 ** ADAM_STEP)
        v_hat = nv / (1.0 - ADAM_B2 ** ADAM_STEP)
        g_ref[...] = g
        d_ref[...] = -ADAM_LR * (m_hat / (jnp.sqrt(v_hat) + ADAM_EPS) + ADAM_WD * w_ref[...])
        nm_ref[...] = nm
        nv_ref[...] = nv

    out_shape = [jax.ShapeDtypeStruct(w.shape, F32)] * 4
    if row_off is None:
        blk = pl.BlockSpec((tr, cw), lambda i: (i, 0))
        return _pcall(update, name=name, grid=(rows // tr,), out_specs=[blk] * 4, out_shape=out_shape,
                      in_specs=[pl.BlockSpec((n, tr, cw), lambda i: (0, i, 0)), blk, blk, blk],
                      args=[parts, w, m, v], hosted=hosted)

    def body(s_ref, *refs):
        update(*refs)

    full = pl.BlockSpec((tr, cw), lambda i, s: (s[0] // tr + i, 0))
    grid_spec = pltpu.PrefetchScalarGridSpec(
        num_scalar_prefetch=1, grid=(rows // tr,),
        in_specs=[pl.BlockSpec((n, tr, cw), lambda i, s: (0, i, 0)), full, full, full],
        out_specs=[full, full, full, full])
    return pl.pallas_call(
        body, name=name, grid_spec=grid_spec, out_shape=out_shape,
        compiler_params=pltpu.CompilerParams(dimension_semantics=("arbitrary",), vmem_limit_bytes=VMEM_LIMIT),
    )(row_off, parts, w, m, v)


def _sum8(x):
    _, r, cw = x.shape
    tr = _row_block(r, 64)

    def body(x_ref, o_ref):
        acc = x_ref[0]
        for k in range(1, N_DEV):
            acc = acc + x_ref[k]
        o_ref[...] = acc

    return pl.pallas_call(
        body, name="sum8", grid=(r // tr,), in_specs=[pl.BlockSpec((N_DEV, tr, cw), lambda i: (0, i, 0))],
        out_specs=pl.BlockSpec((tr, cw), lambda i: (i, 0)), out_shape=jax.ShapeDtypeStruct((r, cw), F32),
        compiler_params=pltpu.CompilerParams(dimension_semantics=("arbitrary",)),
    )(x)


MOD_ROWS = 24
CTX_ROW = 16


def _mod_fwd(c_rows, w_mod, b_mod, hosted=None):
    nl, d, nn = w_mod.shape

    def body(c_ref, w_ref, b_ref, o_ref):
        o_ref[0] = _dot(_silu(c_ref[...]), w_ref[0]) + b_ref[0]

    return _pcall(
        body, name="mod_fwd", grid=(nl,),
        in_specs=[pl.BlockSpec((MOD_ROWS, d), lambda i: (0, 0)), pl.BlockSpec((1, d, nn), lambda i: (i, 0, 0)),
                  pl.BlockSpec((1, 1, nn), lambda i: (i, 0, 0))],
        out_specs=[pl.BlockSpec((1, MOD_ROWS, nn), lambda i: (i, 0, 0))],
        out_shape=[jax.ShapeDtypeStruct((nl, MOD_ROWS, nn), F32)], args=[c_rows, w_mod, b_mod], hosted=hosted)[0]


def _mod_bwd_rows(dlat, dctx_parts):
    nl, ne, nn = dlat.shape

    def body(l_ref, c_ref, db_ref, dc_ref):
        dc = c_ref[0, 0:1, :]
        for k in range(1, N_DEV):
            dc = dc + c_ref[0, k:k + 1, :]
        db = dc
        for k in range(ne):
            db = db + l_ref[0, k:k + 1, :]
        db_ref[0] = db
        dc_ref[0] = dc

    return pl.pallas_call(
        body, name="mod_bwd_rows", grid=(nl,),
        in_specs=[pl.BlockSpec((1, ne, nn), lambda i: (i, 0, 0)), pl.BlockSpec((1, N_DEV, nn), lambda i: (i, 0, 0))],
        out_specs=[pl.BlockSpec((1, 1, nn), lambda i: (i, 0, 0))] * 2,
        out_shape=[jax.ShapeDtypeStruct((nl, 1, nn), F32)] * 2,
        compiler_params=pltpu.CompilerParams(dimension_semantics=("arbitrary",)),
    )(dlat, dctx_parts)


def _mod_bwd_w(c_cols, d_rows, w_mod, hosted=None):
    nl, d, nn = w_mod.shape

    def body(c_ref, d_ref, w_ref, dw_ref, dc_ref):
        i = pl.program_id(0)
        c = c_ref[...]
        sg = jax.nn.sigmoid(c)
        s = c * sg
        dv = d_ref[0]
        acc = s[:, 0:1] * dv[0:1, :]
        for r in range(1, CTX_ROW + 1):
            acc = acc + s[:, r:r + 1] * dv[r:r + 1, :]
        dw_ref[0] = acc
        ds_ctx = jnp.sum(w_ref[0] * dv[CTX_ROW:CTX_ROW + 1, :], axis=1, keepdims=True)
        cc, sc = c[:, CTX_ROW:CTX_ROW + 1], sg[:, CTX_ROW:CTX_ROW + 1]
        part = ds_ctx * (sc * (1.0 + cc * (1.0 - sc)))

        @pl.when(i == 0)
        def _():
            dc_ref[...] = part

        @pl.when(i != 0)
        def _():
            dc_ref[...] += part

    return _pcall(
        body, name="mod_bwd_w", grid=(nl,),
        in_specs=[pl.BlockSpec((d, MOD_ROWS), lambda i: (0, 0)), pl.BlockSpec((1, MOD_ROWS, nn), lambda i: (i, 0, 0)),
                  pl.BlockSpec((1, d, nn), lambda i: (i, 0, 0))],
        out_specs=[pl.BlockSpec((1, d, nn), lambda i: (i, 0, 0)), pl.BlockSpec((d, 1), lambda i: (0, 0))],
        out_shape=[jax.ShapeDtypeStruct((nl, d, nn), F32), jax.ShapeDtypeStruct((d, 1), F32)],
        args=[c_cols, d_rows, w_mod], hosted=hosted)


def _pack_rows(arrays, width, row_multiple=8):
    rows, spans, r0 = [], [], 0
    for a in arrays:
        flat = a.reshape(-1)
        nr = -(-flat.shape[0] // width)
        held = -(-nr // 8) * 8
        flat = jnp.pad(flat, (0, held * width - flat.shape[0]))
        rows.append(flat.reshape(held, width))
        spans.append((r0, nr, a.shape))
        r0 += held
    if r0 % row_multiple:
        rows.append(jnp.zeros((row_multiple - r0 % row_multiple, width), F32))
    return jnp.concatenate(rows, axis=0), spans


def _unpack_rows(packed, spans):
    out = []
    for r0, nr, shape in spans:
        out.append(packed[r0:r0 + nr].reshape(-1)[:math.prod(shape)].reshape(shape))
    return out


BIG = {"cv_w_in": 1, "cv_w_out": 0, "pl_w_in": 1, "pl_w_grp": None, "pl_w_out": 0, "ml_w_in": 1, "ml_w_uq": 1,
       "ml_w_ukv": 1, "ml_w_out": 0, "ch_w_in": 1, "ch_w_out": 0}
SMALL_SHARDED = ["cv_dw", "pl_scale", "ml_q_norm", "ml_kv_norm", "ch_ln_g", "ch_ln_b"]
SMALL_REPLICATED = ["c_ctx", "norm_g", "b_mod", "cv_db", "cv_ln_g", "cv_ln_b", "ml_nope_norm", "ml_rope_norm",
                    "ch_w_s", "ch_b_s"]
WEIGHTS = ['c_ctx', 'norm_g', 'w_mod', 'b_mod', 'cv_w_in', 'cv_dw', 'cv_db', 'cv_ln_g', 'cv_ln_b', 'cv_w_out',
           'pl_w_in', 'pl_w_grp', 'pl_scale', 'pl_w_out', 'ml_w_in', 'ml_q_norm', 'ml_kv_norm', 'ml_w_uq', 'ml_w_ukv',
           'ml_nope_norm', 'ml_rope_norm', 'ml_w_out', 'ch_w_in', 'ch_ln_g', 'ch_ln_b', 'ch_w_s', 'ch_b_s', 'ch_w_out']


def _shard2d(name, a):
    if name == "pl_w_grp":
        return a.reshape(a.shape[-3] * a.shape[-2], a.shape[-1])
    return a.reshape(a.shape[-2], a.shape[-1])


def _unstack(name, s):
    if name == "pl_w_grp":
        ng = len(POOL_WINDOWS)
        return s.reshape(N_CHIP, ng, s.shape[1] // ng, s.shape[2]).transpose(1, 0, 2, 3).reshape(ng, -1, s.shape[2])
    if BIG[name] == 0:
        return s.reshape(-1, s.shape[2])
    return s.transpose(1, 0, 2).reshape(s.shape[1], -1)


def _stack(name, g):
    if g.ndim == 3 and name != "pl_w_grp":
        return g
    if name == "pl_w_grp":
        ng = len(POOL_WINDOWS)
        return g.reshape(ng, N_CHIP, -1, g.shape[2]).transpose(1, 0, 2, 3).reshape(N_CHIP, -1, g.shape[2])
    if BIG[name] == 0:
        return g.reshape(N_CHIP, -1, g.shape[1])
    return g.reshape(g.shape[0], N_CHIP, -1).transpose(1, 0, 2)


L0, L1, L2, L3 = LAYER_WEIGHTS
EARLY_SMALL = ("ch_w_s", "ch_b_s", "ch_ln_g", "ch_ln_b")
MESH_SCHEDULE = {
    "ag8_inputs": [("gather", L0[:1])], "mod_fwd": [("gfill", L0[:1])],
    "cv_in_fwd": [("gather", L0[1:]), ("gather", L1[1:])], "conv_fwd": [("gfill", L0[1:]), ("gather", L1[:1])],
    "cv_out_fwd": [("gfill", L1), ("gather", L2[3:])],
    "pl_in_fwd": [("gather", L2[:1])], "pool_fwd": [("gather", L2[1:3])], "pl_out_fwd": [("gfill", L2)],
    "attn_fwd": [("gather", L3)], "ml_out_fwd": [("gfill", L3)],
    "ch_in_bwd": [("small", EARLY_SMALL)],
    "ml_out_bwd": [("swap", L3)], "attn_bwd": [("exch", L3)], "ml_queries_bwd": [("ofill", L3)],
    "pl_out_bwd": [("swap", L2)], "pl_in_bwd": [("exch", L2)],
    "cv_out_bwd": [("swap", L1), ("ofill", L2)], "conv_bwd": [("exch", L1), ("swap", L0[1:])],
    "ag8_dmod": [("swap", L0[:1]), ("exch", L0[1:])], "mod_bwd_w": [("exch", L0[:1]), ("ofill", L1)],
    "ag8_small_grads": [("ofill", L0[:1]), ("ofill", L0[1:])],
}
GRAD_GROUPS = (L3, L2, L1, L0[1:], L0[:1])


class _MeshPlan:
    def __init__(self, weights, m, v, core):
        self.W, self.M, self.V, self.core = weights, m, v, core
        self.small = None
        self.stack, self.gstack, self.part, self.half, self.out = {}, {}, {}, {}, {}
        self.notes, self.early = {}, {}
        self.live, self.done = {}, set()

    def _make(self, op, names):
        if op == "gather":
            return _gather_ici([_shard2d(n, self.W[n]).astype(BF16) for n in names])
        if op == "gfill":
            return _sibling_fill([self.stack[n] for n in names], 1, True)
        if op == "swap":
            return _grad_swap_d2d([self.gstack[n] for n in names])
        if op == "exch":
            return _grad_exchange_ici([self.part[n] for n in names])
        if op == "ofill":
            return _sibling_fill([t for n in names for t in self.half[n]], 0, False)
        pack, self.early_spans = _pack_rows([self.notes[n] for n in names], LANES, 128)
        return _ag8_copies(pack)

    def _finish_op(self, op, names, hosted):
        self.done.add((op, names))
        res = hosted.results
        if op in ("gather", "gfill"):
            self.stack.update(zip(names, res))
        elif op == "swap":
            for n, r in zip(names, res):
                self.part[n] = _grad_add_half(self.core, self.gstack[n], r)
        elif op == "exch":
            for n, q in zip(names, res):
                rh = q.shape[1]
                self.half[n] = _adamw("adamw_" + n, self.core * rh, q, _shard2d(n, self.W[n]),
                                      _shard2d(n, self.M[n]), _shard2d(n, self.V[n]), rh)
        elif op == "ofill":
            for k, n in enumerate(names):
                self.out[n] = tuple(r.reshape(self.W[n].shape) for r in res[4 * k:4 * k + 4])
        else:
            self.early.update(zip(names, _unpack_rows(_sum8(res[0]), self.early_spans)))

    def alone(self, op, names):
        hosted = self._make(op, names)
        _run_hosted("%s_%s" % (op, names[0]), hosted)
        self._finish_op(op, names, hosted)

    def weights(self, names):
        wk = {n: _unstack(n, self.stack[n]) for n in names}
        if "pl_w_grp" in wk:
            wk["pl_w_grp"] = wk["pl_w_grp"].astype(F32)
        return _prep_weights(wk)

    def hosted(self, tag):
        self.live[tag] = [(op, names, self._make(op, names)) for op, names in MESH_SCHEDULE.get(tag, [])]
        return _merge_hosted([h for _, _, h in self.live[tag]])

    def after(self, tag):
        for op, names, hosted in self.live.pop(tag, []):
            self._finish_op(op, names, hosted)

    def note(self, values):
        self.notes.update(values)

    def layer_grads(self, layer, grads):
        g = _unprep_grads(grads)
        for n in g:
            self.gstack[n] = _stack(n, g[n])

    def finish(self):
        for names in GRAD_GROUPS:
            for op in ("swap", "exch", "ofill"):
                if (op, names) not in self.done:
                    self.alone(op, names)
        return self.out


def kernel(x, c, ctx, c_ctx, norm_g, w_mod, b_mod, cv_w_in, cv_dw, cv_db, cv_ln_g, cv_ln_b, cv_w_out, pl_w_in, pl_w_grp, pl_scale, pl_w_out, ml_w_in, ml_q_norm, ml_kv_norm, ml_w_uq, ml_w_ukv, ml_nope_norm, ml_rope_norm, ml_w_out, ch_w_in, ch_ln_g, ch_ln_b, ch_w_s, ch_b_s, ch_w_out, loss_target, m_c_ctx, m_norm_g, m_w_mod, m_b_mod, m_cv_w_in, m_cv_dw, m_cv_db, m_cv_ln_g, m_cv_ln_b, m_cv_w_out, m_pl_w_in, m_pl_w_grp, m_pl_scale, m_pl_w_out, m_ml_w_in, m_ml_q_norm, m_ml_kv_norm, m_ml_w_uq, m_ml_w_ukv, m_ml_nope_norm, m_ml_rope_norm, m_ml_w_out, m_ch_w_in, m_ch_ln_g, m_ch_ln_b, m_ch_w_s, m_ch_b_s, m_ch_w_out, v_c_ctx, v_norm_g, v_w_mod, v_b_mod, v_cv_w_in, v_cv_dw, v_cv_db, v_cv_ln_g, v_cv_ln_b, v_cv_w_out, v_pl_w_in, v_pl_w_grp, v_pl_scale, v_pl_w_out, v_ml_w_in, v_ml_q_norm, v_ml_kv_norm, v_ml_w_uq, v_ml_w_ukv, v_ml_nope_norm, v_ml_rope_norm, v_ml_w_out, v_ch_w_in, v_ch_ln_g, v_ch_ln_b, v_ch_w_s, v_ch_b_s, v_ch_w_out):
    W = dict(c_ctx=c_ctx, norm_g=norm_g, w_mod=w_mod, b_mod=b_mod, cv_w_in=cv_w_in, cv_dw=cv_dw, cv_db=cv_db, cv_ln_g=cv_ln_g, cv_ln_b=cv_ln_b, cv_w_out=cv_w_out, pl_w_in=pl_w_in, pl_w_grp=pl_w_grp, pl_scale=pl_scale, pl_w_out=pl_w_out, ml_w_in=ml_w_in, ml_q_norm=ml_q_norm, ml_kv_norm=ml_kv_norm, ml_w_uq=ml_w_uq, ml_w_ukv=ml_w_ukv, ml_nope_norm=ml_nope_norm, ml_rope_norm=ml_rope_norm, ml_w_out=ml_w_out, ch_w_in=ch_w_in, ch_ln_g=ch_ln_g, ch_ln_b=ch_ln_b, ch_w_s=ch_w_s, ch_b_s=ch_b_s, ch_w_out=ch_w_out)
    M = dict(c_ctx=m_c_ctx, norm_g=m_norm_g, w_mod=m_w_mod, b_mod=m_b_mod, cv_w_in=m_cv_w_in, cv_dw=m_cv_dw, cv_db=m_cv_db, cv_ln_g=m_cv_ln_g, cv_ln_b=m_cv_ln_b, cv_w_out=m_cv_w_out, pl_w_in=m_pl_w_in, pl_w_grp=m_pl_w_grp, pl_scale=m_pl_scale, pl_w_out=m_pl_w_out, ml_w_in=m_ml_w_in, ml_q_norm=m_ml_q_norm, ml_kv_norm=m_ml_kv_norm, ml_w_uq=m_ml_w_uq, ml_w_ukv=m_ml_w_ukv, ml_nope_norm=m_ml_nope_norm, ml_rope_norm=m_ml_rope_norm, ml_w_out=m_ml_w_out, ch_w_in=m_ch_w_in, ch_ln_g=m_ch_ln_g, ch_ln_b=m_ch_ln_b, ch_w_s=m_ch_w_s, ch_b_s=m_ch_b_s, ch_w_out=m_ch_w_out)
    V = dict(c_ctx=v_c_ctx, norm_g=v_norm_g, w_mod=v_w_mod, b_mod=v_b_mod, cv_w_in=v_cv_w_in, cv_dw=v_cv_dw, cv_db=v_cv_db, cv_ln_g=v_cv_ln_g, cv_ln_b=v_cv_ln_b, cv_w_out=v_cv_w_out, pl_w_in=v_pl_w_in, pl_w_grp=v_pl_w_grp, pl_scale=v_pl_scale, pl_w_out=v_pl_w_out, ml_w_in=v_ml_w_in, ml_q_norm=v_ml_q_norm, ml_kv_norm=v_ml_kv_norm, ml_w_uq=v_ml_w_uq, ml_w_ukv=v_ml_w_ukv, ml_nope_norm=v_ml_nope_norm, ml_rope_norm=v_ml_rope_norm, ml_w_out=v_ml_w_out, ch_w_in=v_ch_w_in, ch_ln_g=v_ch_ln_g, ch_ln_b=v_ch_ln_b, ch_w_s=v_ch_w_s, ch_b_s=v_ch_b_s, ch_w_out=v_ch_w_out)

    batch, lat_len, d = x.shape
    mx, my, mc = _my_place()
    chip = 2 * mx + my
    dev = 2 * chip + mc
    core = jnp.reshape(mc, (1,)).astype(jnp.int32)
    zero_off = jnp.zeros((1,), jnp.int32)
    big_names = list(BIG)

    sw = d // N_CHIP
    small_in = [c] + [jnp.pad(W[n].reshape(-1, W[n].shape[-1]), ((0, 0), (0, sw - W[n].shape[-1])))
                      for n in SMALL_SHARDED]
    pack1, spans1 = _pack_rows(small_in, sw)
    plan = _MeshPlan(W, M, V, core)
    gather1 = _ag8_copies(pack1)
    _run_hosted("ag8_inputs", _merge_hosted([gather1, plan.hosted("ag8_inputs")]))
    plan.after("ag8_inputs")
    got1 = gather1.results[0]
    c_all = got1[:, spans1[0][0]:spans1[0][0] + spans1[0][1]].reshape(N_DEV * batch, d)
    full_small = {}
    for n, (r0, nr, _) in zip(SMALL_SHARDED, spans1[1:]):
        blk = got1[0::2, r0:r0 + nr, :W[n].shape[-1]]
        full_small[n] = blk.transpose(1, 0, 2).reshape(nr, -1)

    c_rows = jnp.concatenate([c_all, c_ctx[None], jnp.zeros((MOD_ROWS - CTX_ROW - 1, d), F32)], axis=0)
    nmod = w_mod.shape[2]
    b_shard = lax.dynamic_slice(b_mod, (0, chip * nmod), (b_mod.shape[0], nmod))[:, None, :]
    mod_shard = _mod_fwd(c_rows, w_mod, b_shard, hosted=plan.hosted("mod_fwd"))
    plan.after("mod_fwd")
    mod_rows = mod_shard.transpose(1, 0, 2).reshape(MOD_ROWS, 1, 4 * nmod)
    got2 = _run_hosted("mod_exchange", _chip_rows_copies(mod_rows, batch, CTX_ROW))[0]
    mod_mine = got2.reshape(N_CHIP, batch + 1, 4, nmod).transpose(2, 1, 0, 3).reshape(4, batch + 1, 3 * d)
    mod_lat, mod_ctx = mod_mine[:, :batch], mod_mine[:, batch]
    mods = []
    for i in range(4):
        mods.append(tuple(
            jnp.stack([mod_lat[i, :, j * d:(j + 1) * d], jnp.broadcast_to(mod_ctx[i, j * d:(j + 1) * d], (batch, d))],
                      axis=1)[:, :, None, :] for j in range(3)))

    wk = dict(full_small)
    wk.update(norm_g=norm_g, cv_db=cv_db, cv_ln_g=cv_ln_g, cv_ln_b=cv_ln_b, ml_nope_norm=ml_nope_norm[0],
              ml_rope_norm=ml_rope_norm[0], ch_w_s=ch_w_s[0], ch_b_s=ch_b_s[0])
    plan.small = _prep_weights(wk)
    xm = jnp.concatenate([x, ctx], axis=1)
    loss_part, grad_x, dmods, g = _local_step(xm, loss_target, mods, plan, lat_len)
    g = _unprep_grads(g)

    lat_rows, ctx_rows = [], []
    for i in range(4):
        dsh, dsc, dgt = dmods[i]
        lat_rows.append(jnp.concatenate([dsh[:, 0, 0], dsc[:, 0, 0], dgt[:, 0, 0]], axis=1))
        zero = jnp.zeros((d,), F32)
        cs = [jnp.sum(t[:, 1, 0], axis=0) if ok else zero
              for t, ok in zip((dsh, dsc, dgt), (i <= 2, i <= 2, i <= 1))]
        ctx_rows.append(jnp.concatenate(cs, axis=0)[None])
    dmod_dev = jnp.concatenate(lat_rows + ctx_rows, axis=0)
    dmod_dev = jnp.pad(dmod_dev, ((0, (-dmod_dev.shape[0]) % 8), (0, 0)))
    gather3 = _ag8_column_copies(dmod_dev, nmod)
    _run_hosted("ag8_dmod", _merge_hosted([gather3, plan.hosted("ag8_dmod")]))
    plan.after("ag8_dmod")
    got3 = gather3.results[0]
    dlat = got3[:, :4 * batch].reshape(N_DEV, 4, batch, nmod).transpose(1, 0, 2, 3).reshape(4, N_DEV * batch, nmod)
    dctx_parts = got3[:, 4 * batch:4 * batch + 4].transpose(1, 0, 2)
    g_b_shard, dctx = _mod_bwd_rows(dlat, dctx_parts)
    d_rows = jnp.concatenate([dlat, dctx, jnp.zeros((4, MOD_ROWS - CTX_ROW - 1, nmod), F32)], axis=1)
    g_w_mod, dcc_part = _mod_bwd_w(c_rows.T, d_rows, w_mod, hosted=plan.hosted("mod_bwd_w"))
    plan.after("mod_bwd_w")

    wm2 = w_mod.reshape(-1, nmod)
    res_mod = _adamw("adamw_w_mod", None, g_w_mod.reshape(1, -1, nmod), wm2, M["w_mod"].reshape(-1, nmod),
                     V["w_mod"].reshape(-1, nmod), wm2.shape[0], hosted=plan.hosted("adamw_w_mod"))
    plan.after("adamw_w_mod")
    out = {"w_mod": tuple(r.reshape(w_mod.shape) for r in res_mod)}

    g_small_in = {n: g[n] for n in SMALL_SHARDED if n not in EARLY_SMALL}
    g_small_in.update(norm_g=g["norm_g"], cv_db=g["cv_db"], cv_ln_g=g["cv_ln_g"], cv_ln_b=g["cv_ln_b"],
                      ml_nope_norm=g["ml_nope_norm"], ml_rope_norm=g["ml_rope_norm"],
                      c_ctx=dcc_part.reshape(-1) * (mc == 0).astype(F32), loss=loss_part,
                      b_mod=lax.dynamic_update_slice(jnp.zeros((N_CHIP, 4, nmod), F32),
                                                     g_b_shard[None, :, 0] * (mc == 0).astype(F32), (chip, 0, 0)))
    small_names = list(g_small_in)
    pack4, spans4 = _pack_rows([g_small_in[n] for n in small_names], LANES, 128)
    gather4 = _ag8_copies(pack4)
    _run_hosted("ag8_small_grads", _merge_hosted([gather4, plan.hosted("ag8_small_grads")]))
    plan.after("ag8_small_grads")
    gs = dict(zip(small_names, _unpack_rows(_sum8(gather4.results[0]), spans4)))
    loss = gs["loss"][0, 0]
    gs.update(plan.early)
    gs["b_mod"] = gs["b_mod"].transpose(1, 0, 2).reshape(4, N_CHIP * nmod)
    for n in SMALL_SHARDED:
        wd = W[n].shape[-1]
        gs[n] = lax.dynamic_slice_in_dim(gs[n], chip * wd, wd, axis=1)
    upd_names = SMALL_REPLICATED + SMALL_SHARDED
    pw, spans_u = _pack_rows([W[n] for n in upd_names], LANES, 128)
    pm, _ = _pack_rows([M[n] for n in upd_names], LANES, 128)
    pv, _ = _pack_rows([V[n] for n in upd_names], LANES, 128)
    pg, _ = _pack_rows([gs[n].reshape(W[n].shape) for n in upd_names], LANES, 128)
    res_small = _adamw("adamw_small", None, pg[None], pw, pm, pv, pw.shape[0], hosted=plan.hosted("adamw_small"))
    plan.after("adamw_small")
    for n, vals in zip(upd_names, zip(*[_unpack_rows(r, spans_u) for r in res_small])):
        out[n] = vals
    out.update(plan.finish())

    outs = [loss, grad_x]
    for j in range(4):
        outs.extend(out[n][j] for n in WEIGHTS)
    return tuple(outs)
```
